```python
import math
import jax, jax.numpy as jnp
from jax import lax
import numpy as np

D_MODEL = 1024
BATCH = 4
SEQ = 4096
DEPTH = 1
DEC_BATCH = 128
DEC_SEQ = 1
PAST_LEN = 16384
PAGE_SIZE = 128

HEAD_DIM = 64
N_HEADS = 8
N_KV_HEADS = 2
GROUP = N_HEADS // N_KV_HEADS
ATTN_WIDTH = N_HEADS * HEAD_DIM
KV_WIDTH = N_KV_HEADS * HEAD_DIM
POOL_WIDTH = D_MODEL - ATTN_WIDTH
POOL_WINDOWS = (2, 4, 8, 16)
POOL_GROUP = POOL_WIDTH // len(POOL_WINDOWS)
POOL_BUF = max(POOL_WINDOWS) - 1
IN_WIDTH = ATTN_WIDTH + 2 * KV_WIDTH + POOL_WIDTH
WINDOW = 128
ATTN_BLOCK = 128
N_BUCKETS = 32
MAX_EXACT = 16
REL_MAX_DIST = 128
N_EXPERTS = 32
TOP_K = 4
D_FF = D_MODEL
SWIGLU_LIMIT = 7.0
SWIGLU_ALPHA = 1.702
MOE_BLOCK = 128
EPS = 1e-6
NEG_INF = -1e30

kernel_name = "hybrid_swa_pool_moe_adaln_step"


def rms_norm(x, g):
    xf = x.astype(jnp.float32)
    y = xf * lax.rsqrt(jnp.mean(xf * xf, axis=-1, keepdims=True) + EPS)
    return (y * g.astype(jnp.float32)).astype(x.dtype)


def modulate(x, shift, scale):
    return x * (1 + scale[:, None, :]) + shift[:, None, :]


def modulation(c, w_ada, b_ada):
    return jnp.split(jax.nn.silu(c) @ w_ada + b_ada, 6, axis=-1)


def t5_bucket(rel):
    n = jnp.maximum(rel, 0)
    nf = jnp.maximum(n, 1).astype(jnp.float32)
    large = MAX_EXACT + (jnp.log(nf / MAX_EXACT) / math.log(REL_MAX_DIST / MAX_EXACT)
                         * (N_BUCKETS - MAX_EXACT)).astype(jnp.int32)
    large = jnp.minimum(large, N_BUCKETS - 1)
    return jnp.where(n < MAX_EXACT, n, large)


def attend(q, k, v, rel, valid, sinks, rel_table):
    s = jnp.einsum('...qkgd,...skd->...kgqs', q, k).astype(jnp.float32) * (HEAD_DIM ** -0.5)
    bias = rel_table[t5_bucket(rel)].astype(jnp.float32)
    bias = bias.reshape(rel.shape + (N_KV_HEADS, GROUP)).transpose(2, 3, 0, 1)
    s = jnp.where(valid[..., None, None, :, :], s + bias, NEG_INF)
    sink = jnp.broadcast_to(sinks.astype(jnp.float32).reshape(N_KV_HEADS, GROUP, 1, 1),
                            s.shape[:-1] + (1,))
    p = jax.nn.softmax(jnp.concatenate([s, sink], axis=-1), axis=-1)[..., :-1]
    return jnp.einsum('...kgqs,...skd->...qkgd', p.astype(v.dtype), v)


def swa_prompt(q, k, v, sinks, rel_table):
    B, S = q.shape[:2]
    nblk = S // ATTN_BLOCK
    qb = q.reshape(B, nblk, ATTN_BLOCK, N_KV_HEADS, GROUP, HEAD_DIM)
    kb = k.reshape(B, nblk, ATTN_BLOCK, N_KV_HEADS, HEAD_DIM)
    vb = v.reshape(B, nblk, ATTN_BLOCK, N_KV_HEADS, HEAD_DIM)

    def band(t):
        prev = jnp.concatenate([jnp.zeros_like(t[:, :1]), t[:, :-1]], axis=1)
        return jnp.concatenate([prev, t], axis=2)

    qi = jnp.arange(ATTN_BLOCK, dtype=jnp.int32)[:, None]
    si = jnp.arange(2 * ATTN_BLOCK, dtype=jnp.int32)[None, :]
    rel = qi + ATTN_BLOCK - si
    kpos = (jnp.arange(nblk, dtype=jnp.int32)[:, None] - 1) * ATTN_BLOCK + si
    valid = ((rel >= 0) & (rel < WINDOW))[None] & (kpos >= 0)[:, None, :]
    out = attend(qb, band(kb), band(vb), rel, valid, sinks, rel_table)
    return out.reshape(B, S, ATTN_WIDTH)


def swa_sample(q, k_new, v_new, cache_k, cache_v, sinks, rel_table):
    Bd, T = q.shape[:2]
    W = cache_k.shape[1]
    kk = jnp.concatenate([cache_k, k_new], axis=1)
    vv = jnp.concatenate([cache_v, v_new], axis=1)
    rel = jnp.arange(T, dtype=jnp.int32)[:, None] + W - jnp.arange(W + T, dtype=jnp.int32)[None, :]
    valid = (rel >= 0) & (rel < WINDOW)
    out = attend(q, kk, vv, rel, valid, sinks, rel_table)
    return out.reshape(Bd, T, ATTN_WIDTH), kk[:, -W:], vv[:, -W:]


def pool_mixer(ext, pos0, w_pool, pool_scale):
    B, L, C = ext.shape
    T = L - POOL_BUF
    e32 = ext.astype(jnp.float32)
    cs = jnp.concatenate([jnp.zeros((B, 1, C), jnp.float32), jnp.cumsum(e32, axis=1)], axis=1)
    hi = cs[:, POOL_BUF + 1:]
    u = e32[:, POOL_BUF:]
    pos = pos0 + jnp.arange(T, dtype=jnp.int32)
    outs = []
    for g, w in enumerate(POOL_WINDOWS):
        sl = slice(g * POOL_GROUP, (g + 1) * POOL_GROUP)
        lo = cs[:, POOL_BUF + 1 - w: POOL_BUF + 1 - w + T, sl]
        cnt = jnp.minimum(pos + 1, w).astype(jnp.float32)[None, :, None]
        d = (hi[..., sl] - lo) / cnt - u[..., sl]
        outs.append(d.astype(ext.dtype) @ w_pool[g])
    return jnp.concatenate(outs, axis=-1) * pool_scale


def moe_ffn(h, w_router, b_router, w1, b1, w2, b2):
    N, D = h.shape
    logits = (h @ w_router + b_router).astype(jnp.float32)
    topv, topi = lax.top_k(logits, TOP_K)
    gates = jax.nn.softmax(topv, axis=-1)
    NK = N * TOP_K
    flat_e = topi.reshape(NK)
    flat_g = gates.reshape(NK)
    flat_t = jnp.repeat(jnp.arange(N, dtype=jnp.int32), TOP_K)
    order = jnp.argsort(flat_e)
    se, st, sg = flat_e[order], flat_t[order], flat_g[order]
    counts = jnp.bincount(flat_e, length=N_EXPERTS)
    starts = jnp.cumsum(counts) - counts
    padded = (counts + MOE_BLOCK - 1) // MOE_BLOCK * MOE_BLOCK
    pad_end = jnp.cumsum(padded)
    pad_start = pad_end - padded
    dest = pad_start[se] + (jnp.arange(NK, dtype=jnp.int32) - starts[se])
    n_blocks = -(-NK // MOE_BLOCK) + N_EXPERTS
    P = n_blocks * MOE_BLOCK
    buf_t = jnp.zeros((P,), jnp.int32).at[dest].set(st)
    buf_g = jnp.zeros((P,), jnp.float32).at[dest].set(sg)
    block_e = jnp.minimum(jnp.searchsorted(pad_end, jnp.arange(n_blocks, dtype=jnp.int32) * MOE_BLOCK,
                                           side='right'), N_EXPERTS - 1)

    def run_block(args):
        tok, e = args
        xb = h[tok]
        gu = xb @ w1[e] + b1[e]
        gate, up = gu[:, :D_FF], gu[:, D_FF:]
        gate = jnp.minimum(gate, SWIGLU_LIMIT)
        up = jnp.clip(up, -SWIGLU_LIMIT, SWIGLU_LIMIT)
        act = (up + 1) * (gate * jax.nn.sigmoid(SWIGLU_ALPHA * gate))
        return act @ w2[e] + b2[e]

    out = lax.map(run_block, (buf_t.reshape(n_blocks, MOE_BLOCK), block_e)).reshape(P, D)
    out = out * buf_g[:, None].astype(out.dtype)
    return jnp.zeros_like(h).at[buf_t].add(out)


def mixer_inputs(x, shift, scale, norm_mix, w_in, q_norm, k_norm):
    B, T, _ = x.shape
    h = modulate(rms_norm(x, norm_mix), shift, scale)
    z = h @ w_in
    q, k, v, u = jnp.split(z, [ATTN_WIDTH, ATTN_WIDTH + KV_WIDTH, ATTN_WIDTH + 2 * KV_WIDTH], axis=-1)
    q = rms_norm(q.reshape(B, T, N_KV_HEADS, GROUP, HEAD_DIM), q_norm)
    k = rms_norm(k.reshape(B, T, N_KV_HEADS, HEAD_DIM), k_norm)
    v = v.reshape(B, T, N_KV_HEADS, HEAD_DIM)
    return q, k, v, u


def block_output(x, attn, pool, gate_mix, shift_ffn, scale_ffn, gate_ffn, w_out, norm_ffn,
                 w_router, b_router, w1, b1, w2, b2):
    x = x + gate_mix[:, None, :] * (jnp.concatenate([attn, pool], axis=-1) @ w_out)
    h = modulate(rms_norm(x, norm_ffn), shift_ffn, scale_ffn)
    B, T, D = h.shape
    y = moe_ffn(h.reshape(B * T, D), w_router, b_router, w1, b1, w2, b2).reshape(B, T, D)
    return x + gate_ffn[:, None, :] * y


def setup_inputs(seed: int = 0) -> dict:
    key = jax.random.key(seed)
    ks = jax.random.split(key, 26)

    def nrm(k, shape, scale):
        return jax.random.normal(k, shape, jnp.float32) * scale

    W_BUF = min(WINDOW, PAST_LEN)
    D = D_MODEL
    return {
        'x_prompt': nrm(ks[0], (BATCH, SEQ, D), 1.0),
        'x_sample': nrm(ks[1], (DEC_BATCH, DEC_SEQ, D), 1.0),
        'cache_k': nrm(ks[2], (DEPTH, DEC_BATCH, W_BUF, N_KV_HEADS, HEAD_DIM), 1.0),
        'cache_v': nrm(ks[3], (DEPTH, DEC_BATCH, W_BUF, N_KV_HEADS, HEAD_DIM), 1.0),
        'state_pool': nrm(ks[4], (DEPTH, DEC_BATCH, POOL_BUF, POOL_WIDTH), 1.0),
        'c_prompt': nrm(ks[5], (BATCH, D), 1.0),
        'c_sample': nrm(ks[6], (DEC_BATCH, D), 1.0),
        'rel_bias': nrm(ks[7], (N_BUCKETS, N_HEADS), 0.5),
        'norm_mix': 1.0 + nrm(ks[8], (DEPTH, D), 0.1),
        'w_ada': nrm(ks[9], (DEPTH, D, 6 * D), D ** -0.5),
        'b_ada': nrm(ks[10], (DEPTH, 6 * D), 0.02),
        'w_in': nrm(ks[11], (DEPTH, D, IN_WIDTH), D ** -0.5),
        'q_norm': 1.0 + nrm(ks[12], (DEPTH, HEAD_DIM), 0.1),
        'k_norm': 1.0 + nrm(ks[13], (DEPTH, HEAD_DIM), 0.1),
        'sinks': nrm(ks[14], (DEPTH, N_HEADS), 1.0),
        'w_pool': nrm(ks[15], (DEPTH, len(POOL_WINDOWS), POOL_GROUP, POOL_GROUP), POOL_GROUP ** -0.5),
        'pool_scale': 0.5 + nrm(ks[16], (DEPTH, POOL_WIDTH), 0.1),
        'w_out': nrm(ks[17], (DEPTH, D, D), D ** -0.5),
        'norm_ffn': 1.0 + nrm(ks[18], (DEPTH, D), 0.1),
        'w_router': nrm(ks[19], (DEPTH, D, N_EXPERTS), D ** -0.5),
        'b_router': nrm(ks[20], (DEPTH, N_EXPERTS), 0.01),
        'w1': nrm(ks[21], (DEPTH, N_EXPERTS, D, 2 * D_FF), D ** -0.5),
        'b1': nrm(ks[22], (DEPTH, N_EXPERTS, 2 * D_FF), 0.01),
        'w2': nrm(ks[23], (DEPTH, N_EXPERTS, D_FF, D), D_FF ** -0.5),
        'b2': nrm(ks[24], (DEPTH, N_EXPERTS, D), 0.01),
    }


def reference(x_prompt, x_sample, cache_k, cache_v, state_pool, c_prompt, c_sample, rel_bias,
              norm_mix, w_ada, b_ada, w_in, q_norm, k_norm, sinks, w_pool, pool_scale, w_out,
              norm_ffn, w_router, b_router, w1, b1, w2, b2):
    xp, xs = x_prompt, x_sample
    S = xp.shape[1]
    w_keep_p = min(WINDOW, S)
    nkp, nvp, npp, nks, nvs, nps = [], [], [], [], [], []
    for l in range(DEPTH):
        sh_m, sc_m, g_m, sh_f, sc_f, g_f = modulation(c_prompt, w_ada[l], b_ada[l])
        q, k, v, u = mixer_inputs(xp, sh_m, sc_m, norm_mix[l], w_in[l], q_norm[l], k_norm[l])
        attn = swa_prompt(q, k, v, sinks[l], rel_bias)
        ext = jnp.concatenate([jnp.zeros((u.shape[0], POOL_BUF, POOL_WIDTH), u.dtype), u], axis=1)
        pool = pool_mixer(ext, 0, w_pool[l], pool_scale[l])
        nkp.append(k[:, -w_keep_p:])
        nvp.append(v[:, -w_keep_p:])
        npp.append(ext[:, -POOL_BUF:])
        xp = block_output(xp, attn, pool, g_m, sh_f, sc_f, g_f, w_out[l], norm_ffn[l],
                          w_router[l], b_router[l], w1[l], b1[l], w2[l], b2[l])
        sh_m, sc_m, g_m, sh_f, sc_f, g_f = modulation(c_sample, w_ada[l], b_ada[l])
        q, k, v, u = mixer_inputs(xs, sh_m, sc_m, norm_mix[l], w_in[l], q_norm[l], k_norm[l])
        attn, k_win, v_win = swa_sample(q, k, v, cache_k[l], cache_v[l], sinks[l], rel_bias)
        ext = jnp.concatenate([state_pool[l], u], axis=1)
        pool = pool_mixer(ext, PAST_LEN, w_pool[l], pool_scale[l])
        nks.append(k_win)
        nvs.append(v_win)
        nps.append(ext[:, -POOL_BUF:])
        xs = block_output(xs, attn, pool, g_m, sh_f, sc_f, g_f, w_out[l], norm_ffn[l],
                          w_router[l], b_router[l], w1[l], b1[l], w2[l], b2[l])
    return (xp, xs, jnp.stack(nkp), jnp.stack(nvp), jnp.stack(npp),
            jnp.stack(nks), jnp.stack(nvs), jnp.stack(nps))
```

```python
import functools
import math

import jax
import jax.numpy as jnp
from jax import lax
from jax.experimental import pallas as pl
from jax.experimental.pallas import tpu as pltpu

F32 = jnp.float32
BF16 = jnp.bfloat16

D_MODEL = 1024
HEAD_DIM = 64
N_HEADS = 8
N_KV_HEADS = 2
GROUP = N_HEADS // N_KV_HEADS
ATTN_WIDTH = N_HEADS * HEAD_DIM
KV_WIDTH = N_KV_HEADS * HEAD_DIM
POOL_WIDTH = D_MODEL - ATTN_WIDTH
POOL_WINDOWS = (2, 4, 8, 16)
POOL_GROUP = POOL_WIDTH // len(POOL_WINDOWS)
POOL_BUF = max(POOL_WINDOWS) - 1
IN_WIDTH = ATTN_WIDTH + 2 * KV_WIDTH + POOL_WIDTH
WINDOW = 128
ATTN_BLOCK = 128
N_BUCKETS = 32
MAX_EXACT = 16
REL_MAX_DIST = 128
N_EXPERTS = 32
TOP_K = 4
D_FF = D_MODEL
SWIGLU_LIMIT = 7.0
SWIGLU_ALPHA = 1.702
EPS = 1e-6
NEG_INF = -1e30
PAST_LEN = 16384

LANES = 128
SUBLANES = 8
VMEM_LIMIT = 56 * 1024 * 1024

SORT_TILE = 256
CHUNK = SUBLANES
TILE_ROWS = -(-(SORT_TILE * TOP_K + N_EXPERTS * (CHUNK - 1)) // LANES) * LANES
TILE_CHUNKS = TILE_ROWS // CHUNK
MOE_BLOCK = 256
BLOCK_CHUNKS = MOE_BLOCK // CHUNK
ROW_W = D_MODEL + LANES


def _bdot(a, b):
    return jnp.dot(a.astype(BF16), b.astype(BF16), preferred_element_type=F32)


def _split(a):
    hi = a.astype(BF16)
    lo = (a - hi.astype(F32)).astype(BF16)
    return hi, lo


def _dot3(a, b):
    ah, al = _split(a)
    bh, bl = _split(b)
    d = functools.partial(jnp.dot, preferred_element_type=F32)
    return d(ah, bh) + d(al, bh) + d(ah, bl)


def _rms(x, g):
    return x * lax.rsqrt(jnp.mean(x * x, axis=-1, keepdims=True) + EPS) * g


def _cparams(sem, **kw):
    return pltpu.CompilerParams(dimension_semantics=sem, vmem_limit_bytes=VMEM_LIMIT, **kw)


def _ada_kernel(c_ref, w_ref, b_ref, o_ref):
    c = c_ref[...]
    s = c * jax.nn.sigmoid(c)
    o_ref[...] = _dot3(s, w_ref[...]) + b_ref[...]


def _modulation(c, w_ada, b_ada):
    rows = c.shape[0]
    n = w_ada.shape[1]
    tn = 512
    return pl.pallas_call(
        _ada_kernel,
        grid=(n // tn,),
        in_specs=[pl.BlockSpec((rows, D_MODEL), lambda j: (0, 0)),
                  pl.BlockSpec((D_MODEL, tn), lambda j: (0, j)),
                  pl.BlockSpec((1, tn), lambda j: (0, j))],
        out_specs=pl.BlockSpec((rows, tn), lambda j: (0, j)),
        out_shape=jax.ShapeDtypeStruct((rows, n), F32),
        compiler_params=_cparams(("parallel",)),
        name="modulation",
    )(c, w_ada, b_ada.reshape(1, n))


def _head_rms(t, bd, g):
    hi, lo = _split(t * t)
    ss = jnp.dot(hi, bd, preferred_element_type=F32) + jnp.dot(lo, bd, preferred_element_type=F32)
    return t * lax.rsqrt(ss * (1.0 / HEAD_DIM) + EPS) * g


def _mixin_kernel(x_ref, sh_ref, sc_ref, g_ref, w_ref, bd_ref, qn_ref, kn_ref,
                  q_ref, k_ref, v_ref, u_ref, *, precise):
    h = _rms(x_ref[...], g_ref[...]) * (1.0 + sc_ref[...]) + sh_ref[...]
    z = _dot3(h, w_ref[...]) if precise else _bdot(h, w_ref[...])
    q = z[:, :ATTN_WIDTH]
    k = z[:, ATTN_WIDTH:ATTN_WIDTH + KV_WIDTH]
    bd = bd_ref[...]
    q = _head_rms(q, bd, qn_ref[...])
    k = _head_rms(k, bd[:KV_WIDTH, :KV_WIDTH], kn_ref[...])
    q_ref[...] = (q * (HEAD_DIM ** -0.5)).astype(BF16)
    k_ref[...] = k
    v_ref[...] = z[:, ATTN_WIDTH + KV_WIDTH:ATTN_WIDTH + 2 * KV_WIDTH]
    u_ref[...] = z[:, ATTN_WIDTH + 2 * KV_WIDTH:]


def _mixer_inputs(x2d, shift, scale, norm_mix, w_in, bd, q_norm, k_norm, *, tile, rows_per_mod, precise):
    n = x2d.shape[0]
    mrows = shift.shape[1]
    mod_spec = pl.BlockSpec((None, mrows, D_MODEL), lambda i: ((i * tile) // rows_per_mod, 0, 0))
    const = lambda shape: pl.BlockSpec(shape, lambda i: (0,) * len(shape))
    row = lambda w: pl.BlockSpec((tile, w), lambda i: (i, 0))
    return pl.pallas_call(
        functools.partial(_mixin_kernel, precise=precise),
        grid=(n // tile,),
        in_specs=[row(D_MODEL), mod_spec, mod_spec, const((1, D_MODEL)), const((D_MODEL, IN_WIDTH)),
                  const((ATTN_WIDTH, ATTN_WIDTH)), const((1, ATTN_WIDTH)), const((1, KV_WIDTH))],
        out_specs=[row(ATTN_WIDTH), row(KV_WIDTH), row(KV_WIDTH), row(POOL_WIDTH)],
        out_shape=[jax.ShapeDtypeStruct((n, ATTN_WIDTH), BF16),
                   jax.ShapeDtypeStruct((n, KV_WIDTH), F32),
                   jax.ShapeDtypeStruct((n, KV_WIDTH), F32),
                   jax.ShapeDtypeStruct((n, POOL_WIDTH), F32)],
        compiler_params=_cparams(("parallel",)),
        name="mixer_inputs",
    )(x2d, shift, scale, norm_mix.reshape(1, D_MODEL), w_in, bd,
      jnp.tile(q_norm, N_HEADS).reshape(1, ATTN_WIDTH), jnp.tile(k_norm, N_KV_HEADS).reshape(1, KV_WIDTH))


def _t5_bucket(rel):
    n = jnp.maximum(rel, 0)
    nf = jnp.maximum(n, 1).astype(F32)
    large = MAX_EXACT + (jnp.log(nf / MAX_EXACT) / math.log(REL_MAX_DIST / MAX_EXACT)
                         * (N_BUCKETS - MAX_EXACT)).astype(jnp.int32)
    large = jnp.minimum(large, N_BUCKETS - 1)
    return jnp.where(n < MAX_EXACT, n, large)


def _bias_table(rel, rel_table):
    bias = rel_table[_t5_bucket(rel)].astype(F32)
    valid = (rel >= 0) & (rel < WINDOW)
    bias = jnp.where(valid[..., None], bias, NEG_INF)
    return jnp.moveaxis(bias, -1, 0)


def _attn_prompt_kernel(sink_ref, q_ref, kp_ref, kc_ref, vp_ref, vc_ref, bias_ref, o_ref):
    first = pl.program_id(1) == 0
    q = q_ref[...]
    kk = jnp.concatenate([kp_ref[...], kc_ref[...]], axis=0).astype(BF16)
    vv = jnp.concatenate([vp_ref[...], vc_ref[...]], axis=0).astype(BF16)
    col = lax.broadcasted_iota(jnp.int32, (GROUP * ATTN_BLOCK, 2 * ATTN_BLOCK), 1)
    row = lax.broadcasted_iota(jnp.int32, (GROUP * ATTN_BLOCK, 1), 0)
    no_prev = jnp.logical_and(first, col < ATTN_BLOCK)
    outs = []
    for kv in range(N_KV_HEADS):
        heads = range(kv * GROUP, (kv + 1) * GROUP)
        qg = jnp.concatenate([q[:, h * HEAD_DIM:(h + 1) * HEAD_DIM] for h in heads], axis=0)
        kh = kk[:, kv * HEAD_DIM:(kv + 1) * HEAD_DIM]
        vh = vv[:, kv * HEAD_DIM:(kv + 1) * HEAD_DIM]
        s = lax.dot_general(qg, kh, (((1,), (1,)), ((), ())), preferred_element_type=F32)
        s = s + bias_ref[kv]
        s = jnp.where(no_prev, NEG_INF, s)
        sink = jnp.zeros((GROUP * ATTN_BLOCK, 1), F32)
        for g, h in enumerate(heads):
            sink = jnp.where(row // ATTN_BLOCK == g, sink_ref[h], sink)
        m = jnp.maximum(jnp.max(s, axis=-1, keepdims=True), sink)
        p = jnp.exp(s - m)
        denom = jnp.sum(p, axis=-1, keepdims=True) + jnp.exp(sink - m)
        o = jnp.dot(p.astype(BF16), vh, preferred_element_type=F32) / denom
        outs += [o[g * ATTN_BLOCK:(g + 1) * ATTN_BLOCK] for g in range(GROUP)]
    o_ref[...] = jnp.concatenate(outs, axis=-1).astype(BF16)


def _attn_prompt(q, k, v, sinks, rel_table):
    b, s = q.shape[:2]
    nblk = s // ATTN_BLOCK
    qi = jnp.arange(ATTN_BLOCK, dtype=jnp.int32)[:, None]
    si = jnp.arange(2 * ATTN_BLOCK, dtype=jnp.int32)[None, :]
    bias = _bias_table(qi + ATTN_BLOCK - si, rel_table)
    bias = bias.reshape(N_KV_HEADS, GROUP * ATTN_BLOCK, 2 * ATTN_BLOCK)
    cur = lambda w: pl.BlockSpec((None, ATTN_BLOCK, w), lambda i, j, *_: (i, j, 0))
    prev = lambda w: pl.BlockSpec((None, ATTN_BLOCK, w), lambda i, j, *_: (i, jnp.maximum(j - 1, 0), 0))
    return pl.pallas_call(
        _attn_prompt_kernel,
        grid_spec=pltpu.PrefetchScalarGridSpec(
            num_scalar_prefetch=1,
            grid=(b, nblk),
            in_specs=[cur(ATTN_WIDTH), prev(KV_WIDTH), cur(KV_WIDTH), prev(KV_WIDTH), cur(KV_WIDTH),
                      pl.BlockSpec(bias.shape, lambda i, j, *_: (0, 0, 0))],
            out_specs=cur(ATTN_WIDTH)),
        out_shape=jax.ShapeDtypeStruct((b, s, ATTN_WIDTH), BF16),
        compiler_params=_cparams(("parallel", "parallel")),
        name="attn_prompt",
    )(sinks.astype(F32), q, k, k, v, v, bias)


def _attn_sample_kernel(sink_ref, q_ref, kc_ref, vc_ref, kn_ref, vn_ref, bias_ref, bnew_ref,
                        o_ref, nk_ref, nv_ref):
    kc = kc_ref[...]
    vc = vc_ref[...]
    kn = kn_ref[...]
    vn = vn_ref[...]
    w = kc.shape[1]
    pos = lax.broadcasted_iota(jnp.int32, kc.shape, 1)
    nk_ref[...] = jnp.where(pos == w - 1, kn[:, None, :], pltpu.roll(kc, w - 1, 1))
    nv_ref[...] = jnp.where(pos == w - 1, vn[:, None, :], pltpu.roll(vc, w - 1, 1))
    gi = lax.broadcasted_iota(jnp.int32, (1, GROUP, 1), 1)
    for kv in range(N_KV_HEADS):
        sl = slice(kv * HEAD_DIM, (kv + 1) * HEAD_DIM)
        qg = q_ref[:, kv]
        s = jnp.einsum('bgd,bsd->bgs', qg, kc[:, :, sl].astype(BF16), preferred_element_type=F32)
        s = s + bias_ref[kv][None]
        s_new = jnp.sum(qg.astype(F32) * kn[:, None, sl], axis=-1, keepdims=True) + bnew_ref[kv][None]
        sink = jnp.zeros((1, GROUP, 1), F32)
        for g in range(GROUP):
            sink = jnp.where(gi == g, sink_ref[kv * GROUP + g], sink)
        m = jnp.maximum(jnp.maximum(jnp.max(s, axis=-1, keepdims=True), s_new), sink)
        p = jnp.exp(s - m)
        p_new = jnp.exp(s_new - m)
        denom = jnp.sum(p, axis=-1, keepdims=True) + p_new + jnp.exp(sink - m)
        o = jnp.einsum('bgs,bsd->bgd', p.astype(BF16), vc[:, :, sl].astype(BF16), preferred_element_type=F32)
        o = o + p_new * vn[:, None, sl]
        o_ref[:, kv] = o / denom


def _attn_sample(q, k_new, v_new, cache_k, cache_v, sinks, rel_table, *, tile=16):
    bd, w = cache_k.shape[:2]
    rel = w - jnp.arange(w, dtype=jnp.int32)
    bias = _bias_table(rel, rel_table).reshape(N_KV_HEADS, GROUP, w)
    bnew = _bias_table(jnp.zeros((1,), jnp.int32), rel_table).reshape(N_KV_HEADS, GROUP, 1)
    q4 = q.reshape(bd, N_KV_HEADS, GROUP, HEAD_DIM)
    spec4 = pl.BlockSpec((tile, N_KV_HEADS, GROUP, HEAD_DIM), lambda i, *_: (i, 0, 0, 0))
    cache = pl.BlockSpec((tile, w, KV_WIDTH), lambda i, *_: (i, 0, 0))
    new = pl.BlockSpec((tile, KV_WIDTH), lambda i, *_: (i, 0))
    const3 = lambda a: pl.BlockSpec(a.shape, lambda i, *_: (0, 0, 0))
    o, nk, nv = pl.pallas_call(
        _attn_sample_kernel,
        grid_spec=pltpu.PrefetchScalarGridSpec(
            num_scalar_prefetch=1,
            grid=(bd // tile,),
            in_specs=[spec4, cache, cache, new, new, const3(bias), const3(bnew)],
            out_specs=[spec4, cache, cache]),
        out_shape=[jax.ShapeDtypeStruct(q4.shape, F32),
                   jax.ShapeDtypeStruct(cache_k.shape, F32),
                   jax.ShapeDtypeStruct(cache_v.shape, F32)],
        compiler_params=_cparams(("parallel",)),
        name="attn_sample",
    )(sinks.astype(F32), q4, cache_k, cache_v, k_new, v_new, bias, bnew)
    return o.reshape(bd, ATTN_WIDTH), nk, nv


def _pool_project(d_groups, wp_ref, ps_ref):
    out = [_bdot(d, wp_ref[g]) for g, d in enumerate(d_groups)]
    return (jnp.concatenate(out, axis=-1) * ps_ref[...]).astype(BF16)


def _pool_prompt_kernel(u_ref, halo_ref, wp_ref, ps_ref, o_ref, ext):
    t = pl.program_id(1)
    tile = u_ref.shape[0]
    halo = halo_ref[...]
    ext[0:2 * SUBLANES, :] = jnp.where(t == 0, jnp.zeros_like(halo), halo)
    ext[2 * SUBLANES:, :] = u_ref[...]
    pos = t * tile + lax.broadcasted_iota(jnp.int32, (tile, 1), 0)
    base = 2 * SUBLANES
    ds = []
    for g, w in enumerate(POOL_WINDOWS):
        sl = slice(g * POOL_GROUP, (g + 1) * POOL_GROUP)
        acc = ext[base:base + tile, sl]
        for j in range(1, w):
            acc = acc + ext[base - j:base - j + tile, sl]
        cnt = jnp.minimum(pos + 1, w).astype(F32)
        ds.append(acc / cnt - ext[base:base + tile, sl])
    o_ref[...] = _pool_project(ds, wp_ref, ps_ref)


def _pool_prompt(u, w_pool, pool_scale, *, tile=512):
    b, s, c = u.shape
    hb = 2 * SUBLANES
    return pl.pallas_call(
        _pool_prompt_kernel,
        grid=(b, s // tile),
        in_specs=[pl.BlockSpec((None, tile, c), lambda i, t: (i, t, 0)),
                  pl.BlockSpec((None, hb, c), lambda i, t: (i, jnp.maximum(t * (tile // hb) - 1, 0), 0)),
                  pl.BlockSpec(w_pool.shape, lambda i, t: (0, 0, 0)),
                  pl.BlockSpec((1, c), lambda i, t: (0, 0))],
        out_specs=pl.BlockSpec((None, tile, c), lambda i, t: (i, t, 0)),
        out_shape=jax.ShapeDtypeStruct((b, s, c), BF16),
        scratch_shapes=[pltpu.VMEM((tile + hb, c), F32)],
        compiler_params=_cparams(("parallel", "parallel")),
        name="pool_prompt",
    )(u, u, w_pool, pool_scale.reshape(1, c))


def _pool_sample_kernel(st_ref, u_ref, wp_ref, ps_ref, o_ref, ns_ref):
    u = u_ref[...]
    ns_ref[0:POOL_BUF - 1] = st_ref[1:POOL_BUF]
    ns_ref[POOL_BUF - 1] = u
    ds = []
    for g, w in enumerate(POOL_WINDOWS):
        sl = slice(g * POOL_GROUP, (g + 1) * POOL_GROUP)
        acc = u[:, sl]
        for j in range(1, w):
            acc = acc + st_ref[POOL_BUF - j][:, sl]
        cnt = float(min(PAST_LEN + 1, w))
        ds.append(acc / cnt - u[:, sl])
    o_ref[...] = _pool_project(ds, wp_ref, ps_ref)


def _pool_sample(state_t, u, w_pool, pool_scale):
    nb, bd, c = state_t.shape
    full = lambda a: pl.BlockSpec(a.shape, lambda: (0,) * a.ndim)
    ps = pool_scale.reshape(1, c)
    return pl.pallas_call(
        _pool_sample_kernel,
        in_specs=[full(state_t), full(u), full(w_pool), full(ps)],
        out_specs=[pl.BlockSpec((bd, c), lambda: (0, 0)), full(state_t)],
        out_shape=[jax.ShapeDtypeStruct((bd, c), BF16), jax.ShapeDtypeStruct(state_t.shape, F32)],
        compiler_params=pltpu.CompilerParams(vmem_limit_bytes=VMEM_LIMIT),
        name="pool_sample",
    )(state_t, u, w_pool, ps)


def _route_kernel(x_ref, attn_ref, pool_ref, gm_ref, sh_ref, sc_ref, wo_ref, nf_ref, wr_ref, br_ref,
                  tri_t_ref, tri_e_ref,
                  x2_ref, hs_ref, slot_ref, cnt_ref):
    tile = x_ref.shape[0]
    mix = (jnp.dot(attn_ref[...], wo_ref[:ATTN_WIDTH, :], preferred_element_type=F32)
           + jnp.dot(pool_ref[...], wo_ref[ATTN_WIDTH:, :], preferred_element_type=F32))
    x2 = x_ref[...] + gm_ref[...] * mix
    x2_ref[...] = x2
    h = _rms(x2, nf_ref[...]) * (1.0 + sc_ref[...]) + sh_ref[...]

    hh, hl = _split(h)
    wh, wl = _split(wr_ref[...])
    nt = functools.partial(lax.dot_general, dimension_numbers=(((1,), (1,)), ((), ())),
                           preferred_element_type=F32)
    logits = nt(wh, hh) + nt(wl, hh) + nt(wh, hl) + br_ref[...]

    eidx = lax.broadcasted_iota(jnp.int32, (N_EXPERTS, tile), 0).astype(F32)
    work = logits
    tops, picks = [], []
    for _ in range(TOP_K):
        m = jnp.max(work, axis=0, keepdims=True)
        pick = jnp.min(jnp.where(work == m, eidx, float(N_EXPERTS)), axis=0, keepdims=True)
        work = jnp.where(eidx == pick, -jnp.inf, work)
        tops.append(m)
        picks.append(pick)
    ex = [jnp.exp(t - tops[0]) for t in tops]
    den = ex[0] + ex[1] + ex[2] + ex[3]
    gates = [e / den for e in ex]

    sel = jnp.zeros((N_EXPERTS, tile), F32)
    for pick in picks:
        sel = sel + (eidx == pick).astype(F32)
    rank = jnp.dot(sel.astype(BF16), tri_t_ref[...], preferred_element_type=F32)
    cnt = jnp.sum(sel, axis=1, keepdims=True)
    padded = jnp.ceil(cnt * (1.0 / CHUNK)) * CHUNK
    seg = jnp.dot(tri_e_ref[...], jnp.broadcast_to(padded, (N_EXPERTS, LANES)).astype(BF16),
                  preferred_element_type=F32)[:, :1]
    dest = seg + rank
    cnt_ref[...] = jnp.broadcast_to(cnt, (N_EXPERTS, LANES))
    slots = [jnp.sum(jnp.where(eidx == pick, dest, 0.0), axis=0, keepdims=True) for pick in picks]
    slot_ref[...] = jnp.concatenate(slots + slots, axis=0)

    ridx = lax.broadcasted_iota(jnp.int32, (TILE_ROWS, tile), 0).astype(F32)
    perm = jnp.zeros((TILE_ROWS, tile), F32)
    gmat = jnp.zeros((TILE_ROWS, tile), F32)
    for s, g in zip(slots, gates):
        hit = ridx == s
        perm = jnp.where(hit, 1.0, perm)
        gmat = jnp.where(hit, g, gmat)
    hs_ref[:, :D_MODEL] = jnp.dot(perm.astype(BF16), hh, preferred_element_type=F32)
    hs_ref[:, D_MODEL:] = jnp.broadcast_to(jnp.sum(gmat, axis=1, keepdims=True), (TILE_ROWS, LANES))


def _route(x2d, attn, pool, gm, sh, sc, w_out, norm_ffn, w_router_t, b_router, *, tile, rows_per_mod):
    n = x2d.shape[0]
    nt = n // tile
    mrows = gm.shape[1]
    mod_spec = pl.BlockSpec((None, mrows, D_MODEL), lambda i: ((i * tile) // rows_per_mod, 0, 0))
    const = lambda shape: pl.BlockSpec(shape, lambda i: (0,) * len(shape))
    row = lambda w: pl.BlockSpec((tile, w), lambda i: (i, 0))
    tri_t = (jnp.arange(tile)[:, None] < jnp.arange(tile)[None, :]).astype(BF16)
    tri_e = (jnp.arange(N_EXPERTS)[None, :] < jnp.arange(N_EXPERTS)[:, None]).astype(BF16)
    in_specs = [row(D_MODEL), row(ATTN_WIDTH), row(POOL_WIDTH), mod_spec, mod_spec, mod_spec,
                const((D_MODEL, D_MODEL)), const((1, D_MODEL)), const((N_EXPERTS, D_MODEL)),
                const((N_EXPERTS, 1)), const((tile, tile)), const((N_EXPERTS, N_EXPERTS))]
    args = [x2d, attn, pool, gm, sh, sc, w_out, norm_ffn.reshape(1, D_MODEL), w_router_t,
            b_router.reshape(N_EXPERTS, 1), tri_t, tri_e]
    return pl.pallas_call(
        _route_kernel,
        grid=(nt,),
        in_specs=in_specs,
        out_specs=[pl.BlockSpec((tile, D_MODEL), lambda i: (i, 0)),
                   pl.BlockSpec((TILE_ROWS, ROW_W), lambda i: (i, 0)),
                   pl.BlockSpec((None, 2 * TOP_K, tile), lambda i: (i, 0, 0)),
                   pl.BlockSpec((None, N_EXPERTS, LANES), lambda i: (i, 0, 0))],
        out_shape=[jax.ShapeDtypeStruct((n, D_MODEL), F32),
                   jax.ShapeDtypeStruct((nt * TILE_ROWS, ROW_W), F32),
                   jax.ShapeDtypeStruct((nt, 2 * TOP_K, tile), F32),
                   jax.ShapeDtypeStruct((nt, N_EXPERTS, LANES), F32)],
        compiler_params=_cparams(("parallel",)),
        name="route",
    )(*args)


def _moe_kernel(src_ref, be_ref, nb_ref, hp_hbm, hs_hbm, w1_ref, b1_ref, w2_ref, b2_ref, op_hbm, os_hbm,
                lhs, obuf, w1c, w2c, sem_in, sem_out):
    b = pl.program_id(0)
    nb = nb_ref[0]
    slot = b % 2
    p_chunks = hp_hbm.shape[0] // CHUNK

    def chunk_rows(c):
        return pl.ds(pl.multiple_of(c * CHUNK, CHUNK), CHUNK)

    def in_copy(hbm, c, j, s):
        return pltpu.make_async_copy(hbm.at[chunk_rows(c)], lhs.at[s, chunk_rows(j)], sem_in.at[s])

    def out_copy(hbm, c, j, s):
        return pltpu.make_async_copy(obuf.at[s, chunk_rows(j)], hbm.at[chunk_rows(c), pl.ds(0, D_MODEL)],
                                     sem_out.at[s])

    def for_chunks(blk, s, copy_fn, refs, action, pad_action=None):
        def body(j, _):
            c = src_ref[blk * BLOCK_CHUNKS + j]

            @pl.when(jnp.logical_and(c >= 0, c < p_chunks))
            def _():
                action(copy_fn(refs[0], c, j, s))

            @pl.when(c >= p_chunks)
            def _():
                action(copy_fn(refs[1], c - p_chunks, j, s))

            if pad_action is not None:
                @pl.when(c < 0)
                def _():
                    pad_action(j)
            return 0
        lax.fori_loop(0, BLOCK_CHUNKS, body, 0)

    def start_in(blk, s):
        def zero_rows(j):
            lhs[s, chunk_rows(j), :] = jnp.zeros((CHUNK, ROW_W), F32)
        for_chunks(blk, s, in_copy, (hp_hbm, hs_hbm), lambda cp: cp.start(), zero_rows)

    def wait_in(blk, s):
        for_chunks(blk, s, in_copy, (hp_hbm, hs_hbm), lambda cp: cp.wait())

    def start_out(blk, s):
        for_chunks(blk, s, out_copy, (op_hbm, os_hbm), lambda cp: cp.start())

    def wait_out(blk, s):
        for_chunks(blk, s, out_copy, (op_hbm, os_hbm), lambda cp: cp.wait())

    @pl.when(b < nb)
    def _():
        @pl.when(b == 0)
        def _():
            start_in(0, 0)

        @pl.when(b + 1 < nb)
        def _():
            start_in(b + 1, 1 - slot)

        wait_in(b, slot)

        @pl.when(jnp.logical_or(b == 0, be_ref[b] != be_ref[jnp.maximum(b - 1, 0)]))
        def _():
            w1c[...] = w1_ref[...].astype(BF16)
            w2c[...] = w2_ref[...].astype(BF16)

        @pl.when(b >= 2)
        def _():
            wait_out(b - 2, slot)

        rows = lhs[slot]
        x = rows[:, :D_MODEL].astype(BF16)
        gate_w = rows[:, D_MODEL:D_MODEL + 1]
        gu = jnp.dot(x, w1c[...], preferred_element_type=F32) + b1_ref[...]
        gate = jnp.minimum(gu[:, :D_FF], SWIGLU_LIMIT)
        up = jnp.clip(gu[:, D_FF:], -SWIGLU_LIMIT, SWIGLU_LIMIT)
        act = (up + 1.0) * (gate * jax.nn.sigmoid(SWIGLU_ALPHA * gate))
        y = jnp.dot(act.astype(BF16), w2c[...], preferred_element_type=F32) + b2_ref[...]
        obuf[slot] = y * gate_w

        start_out(b, slot)

        @pl.when(b == nb - 1)
        def _():
            @pl.when(b >= 1)
            def _():
                wait_out(b - 1, 1 - slot)
            wait_out(b, slot)


def _moe(hs_p, hs_s, src, block_e, nblocks, w1, b1, w2, b2, n_blocks_max):
    wspec = lambda shape: pl.BlockSpec((None,) + shape, lambda b, src, be, nb: (be[b], 0, 0))
    hbm = pl.BlockSpec(memory_space=pl.ANY)
    return pl.pallas_call(
        _moe_kernel,
        grid_spec=pltpu.PrefetchScalarGridSpec(
            num_scalar_prefetch=3,
            grid=(n_blocks_max,),
            in_specs=[hbm, hbm,
                      wspec((D_MODEL, 2 * D_FF)), wspec((1, 2 * D_FF)),
                      wspec((D_FF, D_MODEL)), wspec((1, D_MODEL))],
            out_specs=[hbm, hbm],
            scratch_shapes=[pltpu.VMEM((2, MOE_BLOCK, ROW_W), F32),
                            pltpu.VMEM((2, MOE_BLOCK, D_MODEL), F32),
                            pltpu.VMEM((D_MODEL, 2 * D_FF), BF16),
                            pltpu.VMEM((D_FF, D_MODEL), BF16),
                            pltpu.SemaphoreType.DMA((2,)),
                            pltpu.SemaphoreType.DMA((2,))]),
        out_shape=[jax.ShapeDtypeStruct(hs_p.shape, F32), jax.ShapeDtypeStruct(hs_s.shape, F32)],
        input_output_aliases={3: 0, 4: 1},
        compiler_params=_cparams(("arbitrary",)),
        name="moe_experts",
    )(src, block_e, nblocks, hs_p, hs_s, w1, b1.reshape(N_EXPERTS, 1, 2 * D_FF), w2,
      b2.reshape(N_EXPERTS, 1, D_MODEL))


def _plan(cnt, n_blocks_max):
    nt = cnt.shape[0]
    nch = (cnt + (CHUNK - 1)) // CHUNK
    lstart = jnp.cumsum(nch, axis=1) - nch
    ne = jnp.sum(nch, axis=0)
    nbe = (ne + (BLOCK_CHUNKS - 1)) // BLOCK_CHUNKS
    bend = jnp.cumsum(nbe)
    nblocks = bend[-1]
    gstart = (bend - nbe)[None, :] * BLOCK_CHUNKS + (jnp.cumsum(nch, axis=0) - nch)
    gs = gstart.T.reshape(-1)
    nc = nch.T.reshape(-1)
    s0 = (jnp.arange(nt, dtype=jnp.int32)[:, None] * TILE_CHUNKS + lstart).T.reshape(-1)
    c = jnp.arange(n_blocks_max * BLOCK_CHUNKS, dtype=jnp.int32)
    sidx = jnp.sum((c[:, None] >= gs[None, :]).astype(jnp.int32), axis=1) - 1
    off = c - gs[sidx]
    src = jnp.where(off < nc[sidx], s0[sidx] + off, -1).astype(jnp.int32)
    blk = jnp.arange(n_blocks_max, dtype=jnp.int32)
    be = jnp.sum((blk[:, None] >= bend[None, :]).astype(jnp.int32), axis=1)
    be_last = jnp.sum((nblocks - 1 >= bend).astype(jnp.int32))
    be = jnp.minimum(be, be_last).astype(jnp.int32)
    return src, be, nblocks.reshape(1).astype(jnp.int32)


def _combine_kernel(o_ref, slot_ref, x2_ref, gf_ref, y_ref):
    tile = x2_ref.shape[0]
    ridx = lax.broadcasted_iota(jnp.int32, (TILE_ROWS, tile), 0).astype(F32)
    hit = ridx == slot_ref[0:1, :]
    for k in range(1, TOP_K):
        hit = jnp.logical_or(hit, ridx == slot_ref[k:k + 1, :])
    perm = jnp.where(hit, 1.0, 0.0).astype(BF16)
    y = lax.dot_general(perm, o_ref[...].astype(BF16), (((0,), (0,)), ((), ())), preferred_element_type=F32)
    y_ref[...] = x2_ref[...] + gf_ref[...] * y


def _combine(outs, slots, x2, gf, *, tile, rows_per_mod):
    n = x2.shape[0]
    mrows = gf.shape[1]
    return pl.pallas_call(
        _combine_kernel,
        grid=(n // tile,),
        in_specs=[pl.BlockSpec((TILE_ROWS, D_MODEL), lambda i: (i, 0)),
                  pl.BlockSpec((None, 2 * TOP_K, tile), lambda i: (i, 0, 0)),
                  pl.BlockSpec((tile, D_MODEL), lambda i: (i, 0)),
                  pl.BlockSpec((None, mrows, D_MODEL), lambda i: ((i * tile) // rows_per_mod, 0, 0))],
        out_specs=pl.BlockSpec((tile, D_MODEL), lambda i: (i, 0)),
        out_shape=jax.ShapeDtypeStruct((n, D_MODEL), F32),
        compiler_params=_cparams(("parallel",)),
        name="combine",
    )(outs, slots, x2, gf)


def kernel(x_prompt, x_sample, cache_k, cache_v, state_pool, c_prompt, c_sample, rel_bias, norm_mix, w_ada,
           b_ada, w_in, q_norm, k_norm, sinks, w_pool, pool_scale, w_out, norm_ffn, w_router, b_router,
           w1, b1, w2, b2):
    depth = w_in.shape[0]
    assert depth == 1
    l = 0
    bp, sp, _ = x_prompt.shape
    bs = x_sample.shape[0]
    assert x_sample.shape[1] == 1 and sp % SORT_TILE == 0 and bs <= SORT_TILE
    n_p = bp * sp
    tiles_p = n_p // SORT_TILE
    max_chunks = tiles_p * (SORT_TILE * TOP_K // CHUNK + N_EXPERTS) + (bs * TOP_K // CHUNK + N_EXPERTS)
    n_blocks_max = -(-max_chunks // BLOCK_CHUNKS) + N_EXPERTS

    mod = _modulation(jnp.concatenate([c_prompt, c_sample], axis=0), w_ada[l], b_ada[l])
    mod_p = [m.reshape(bp, 1, D_MODEL) for m in jnp.split(mod[:bp], 6, axis=-1)]
    mod_s = [m.reshape(1, bs, D_MODEL) for m in jnp.split(mod[bp:], 6, axis=-1)]

    head = jnp.arange(ATTN_WIDTH) // HEAD_DIM
    bd = (head[:, None] == head[None, :]).astype(BF16)
    w_in_b = w_in[l].astype(BF16)
    w_out_b = w_out[l].astype(BF16)
    w_pool_b = w_pool[l].astype(BF16)
    w_router_t = w_router[l].T

    xp = x_prompt.reshape(n_p, D_MODEL)
    q, k, v, u = _mixer_inputs(xp, mod_p[0], mod_p[1], norm_mix[l], w_in_b, bd, q_norm[l], k_norm[l],
                               tile=512, rows_per_mod=sp, precise=False)
    k3 = k.reshape(bp, sp, KV_WIDTH)
    v3 = v.reshape(bp, sp, KV_WIDTH)
    u3 = u.reshape(bp, sp, POOL_WIDTH)
    attn = _attn_prompt(q.reshape(bp, sp, ATTN_WIDTH), k3, v3, sinks[l], rel_bias)
    pool = _pool_prompt(u3, w_pool_b, pool_scale[l])
    keep = min(WINDOW, sp)
    nkp = k3[:, -keep:].reshape(bp, keep, N_KV_HEADS, HEAD_DIM)
    nvp = v3[:, -keep:].reshape(bp, keep, N_KV_HEADS, HEAD_DIM)
    if sp >= POOL_BUF:
        npp = u3[:, -POOL_BUF:]
    else:
        npp = jnp.concatenate([jnp.zeros((bp, POOL_BUF - sp, POOL_WIDTH), F32), u3], axis=1)
    x2_p, hs_p, slots_p, cnt_p = _route(
        xp, attn.reshape(n_p, ATTN_WIDTH), pool.reshape(n_p, POOL_WIDTH), mod_p[2], mod_p[3], mod_p[4],
        w_out_b, norm_ffn[l], w_router_t, b_router[l], tile=SORT_TILE, rows_per_mod=sp)

    xs = x_sample.reshape(bs, D_MODEL)
    qs, ks, vs, us = _mixer_inputs(xs, mod_s[0], mod_s[1], norm_mix[l], w_in[l], bd, q_norm[l], k_norm[l],
                                   tile=bs, rows_per_mod=bs, precise=True)
    wbuf = cache_k.shape[2]
    attn_s, nks, nvs = _attn_sample(qs, ks, vs, cache_k[l].reshape(bs, wbuf, KV_WIDTH),
                                    cache_v[l].reshape(bs, wbuf, KV_WIDTH), sinks[l], rel_bias)
    pool_s, nps_t = _pool_sample(jnp.swapaxes(state_pool[l], 0, 1), us, w_pool_b, pool_scale[l])
    x2_s, hs_s, slots_s, cnt_s = _route(
        xs, attn_s.astype(BF16), pool_s, mod_s[2], mod_s[3], mod_s[4], w_out_b, norm_ffn[l],
        w_router_t, b_router[l], tile=bs, rows_per_mod=bs)

    cnt = jnp.concatenate([cnt_p[:, :, 0], cnt_s[:, :, 0]], axis=0).astype(jnp.int32)
    src, block_e, nblocks = _plan(cnt, n_blocks_max)
    outs_p, outs_s = _moe(hs_p, hs_s, src, block_e, nblocks, w1[l], b1[l], w2[l], b2[l], n_blocks_max)

    y_p = _combine(outs_p, slots_p, x2_p, mod_p[5], tile=SORT_TILE, rows_per_mod=sp)
    y_s = _combine(outs_s, slots_s, x2_s, mod_s[5], tile=bs, rows_per_mod=bs)

    return (y_p.reshape(bp, sp, D_MODEL), y_s.reshape(bs, 1, D_MODEL),
            nkp[None], nvp[None], npp[None],
            nks.reshape(1, bs, wbuf, N_KV_HEADS, HEAD_DIM), nvs.reshape(1, bs, wbuf, N_KV_HEADS, HEAD_DIM),
            jnp.swapaxes(nps_t, 0, 1)[None])
```

```python
import functools
import math

import jax
import jax.numpy as jnp
from jax import lax
from jax.experimental import pallas as pl
from jax.experimental.pallas import tpu as pltpu

F32 = jnp.float32
BF16 = jnp.bfloat16

D_MODEL = 1024
HEAD_DIM = 64
N_HEADS = 8
N_KV_HEADS = 2
GROUP = N_HEADS // N_KV_HEADS
ATTN_WIDTH = N_HEADS * HEAD_DIM
KV_WIDTH = N_KV_HEADS * HEAD_DIM
POOL_WIDTH = D_MODEL - ATTN_WIDTH
POOL_WINDOWS = (2, 4, 8, 16)
POOL_GROUP = POOL_WIDTH // len(POOL_WINDOWS)
POOL_BUF = max(POOL_WINDOWS) - 1
IN_WIDTH = ATTN_WIDTH + 2 * KV_WIDTH + POOL_WIDTH
WINDOW = 128
ATTN_BLOCK = 128
N_BUCKETS = 32
MAX_EXACT = 16
REL_MAX_DIST = 128
N_EXPERTS = 32
TOP_K = 4
D_FF = D_MODEL
SWIGLU_LIMIT = 7.0
SWIGLU_ALPHA = 1.702
EPS = 1e-6
NEG_INF = -1e30
PAST_LEN = 16384

LANES = 128
SUBLANES = 8
VMEM_LIMIT = 56 * 1024 * 1024

SORT_TILE = 256
CHUNK = SUBLANES
TILE_ROWS = -(-(SORT_TILE * TOP_K + N_EXPERTS * (CHUNK - 1)) // LANES) * LANES
TILE_CHUNKS = TILE_ROWS // CHUNK
MOE_BLOCK = 256
BLOCK_CHUNKS = MOE_BLOCK // CHUNK
PACKED_W = D_MODEL // 2
ROW_W = PACKED_W + LANES
ZERO_CHUNK = TILE_CHUNKS - 1


def _bdot(a, b):
    return jnp.dot(a.astype(BF16), b.astype(BF16), preferred_element_type=F32)


def _split(a):
    hi = a.astype(BF16)
    lo = (a - hi.astype(F32)).astype(BF16)
    return hi, lo


def _dot3(a, b):
    ah, al = _split(a)
    bh, bl = _split(b)
    d = functools.partial(jnp.dot, preferred_element_type=F32)
    return d(ah, bh) + d(al, bh) + d(ah, bl)


def _pack_rows(x):
    bits = lax.bitcast_convert_type(x, jnp.int32)
    return bits[:, :PACKED_W] | lax.shift_right_logical(bits[:, PACKED_W:], 16)


def _unpack_rows(w):
    hi = lax.bitcast_convert_type(w & jnp.int32(-65536), F32)
    lo = lax.bitcast_convert_type(lax.shift_left(w, 16), F32)
    return hi.astype(BF16), lo.astype(BF16)


def _rms(x, g):
    return x * lax.rsqrt(jnp.mean(x * x, axis=-1, keepdims=True) + EPS) * g


def _cparams(sem, **kw):
    return pltpu.CompilerParams(dimension_semantics=sem, vmem_limit_bytes=VMEM_LIMIT, **kw)


def _ada_kernel(c_ref, w_ref, b_ref, o_ref):
    c = c_ref[...]
    s = c * jax.nn.sigmoid(c)
    o_ref[...] = _dot3(s, w_ref[...]) + b_ref[...]


def _modulation(c, w_ada, b_ada):
    rows = c.shape[0]
    n = w_ada.shape[1]
    tn = 512
    return pl.pallas_call(
        _ada_kernel,
        grid=(n // tn,),
        in_specs=[pl.BlockSpec((rows, D_MODEL), lambda j: (0, 0)),
                  pl.BlockSpec((D_MODEL, tn), lambda j: (0, j)),
                  pl.BlockSpec((1, tn), lambda j: (0, j))],
        out_specs=pl.BlockSpec((rows, tn), lambda j: (0, j)),
        out_shape=jax.ShapeDtypeStruct((rows, n), F32),
        compiler_params=_cparams(("parallel",)),
        name="modulation",
    )(c, w_ada, b_ada.reshape(1, n))


def _head_rms(t, bd, g):
    hi, lo = _split(t * t)
    ss = jnp.dot(hi, bd, preferred_element_type=F32) + jnp.dot(lo, bd, preferred_element_type=F32)
    return t * lax.rsqrt(ss * (1.0 / HEAD_DIM) + EPS) * g


def _mixin_kernel(x_ref, sh_ref, sc_ref, g_ref, w_ref, bd_ref, qn_ref, kn_ref,
                  q_ref, k_ref, v_ref, u_ref, *, precise):
    h = _rms(x_ref[...], g_ref[...]) * (1.0 + sc_ref[...]) + sh_ref[...]
    z = _dot3(h, w_ref[...]) if precise else _bdot(h, w_ref[...])
    q = z[:, :ATTN_WIDTH]
    k = z[:, ATTN_WIDTH:ATTN_WIDTH + KV_WIDTH]
    bd = bd_ref[...]
    q = _head_rms(q, bd, qn_ref[...])
    k = _head_rms(k, bd[:KV_WIDTH, :KV_WIDTH], kn_ref[...])
    q_ref[...] = (q * (HEAD_DIM ** -0.5)).astype(BF16)
    k_ref[...] = k
    v_ref[...] = z[:, ATTN_WIDTH + KV_WIDTH:ATTN_WIDTH + 2 * KV_WIDTH]
    u_ref[...] = z[:, ATTN_WIDTH + 2 * KV_WIDTH:]


def _mixer_inputs(x2d, shift, scale, norm_mix, w_in, bd, q_norm, k_norm, *, tile, rows_per_mod, precise):
    n = x2d.shape[0]
    mrows = shift.shape[1]
    mod_spec = pl.BlockSpec((None, mrows, D_MODEL), lambda i: ((i * tile) // rows_per_mod, 0, 0))
    const = lambda shape: pl.BlockSpec(shape, lambda i: (0,) * len(shape))
    row = lambda w: pl.BlockSpec((tile, w), lambda i: (i, 0))
    return pl.pallas_call(
        functools.partial(_mixin_kernel, precise=precise),
        grid=(n // tile,),
        in_specs=[row(D_MODEL), mod_spec, mod_spec, const((1, D_MODEL)), const((D_MODEL, IN_WIDTH)),
                  const((ATTN_WIDTH, ATTN_WIDTH)), const((1, ATTN_WIDTH)), const((1, KV_WIDTH))],
        out_specs=[row(ATTN_WIDTH), row(KV_WIDTH), row(KV_WIDTH), row(POOL_WIDTH)],
        out_shape=[jax.ShapeDtypeStruct((n, ATTN_WIDTH), BF16),
                   jax.ShapeDtypeStruct((n, KV_WIDTH), F32),
                   jax.ShapeDtypeStruct((n, KV_WIDTH), F32),
                   jax.ShapeDtypeStruct((n, POOL_WIDTH), F32)],
        compiler_params=_cparams(("parallel",)),
        name="mixer_inputs",
    )(x2d, shift, scale, norm_mix.reshape(1, D_MODEL), w_in, bd,
      jnp.tile(q_norm, N_HEADS).reshape(1, ATTN_WIDTH), jnp.tile(k_norm, N_KV_HEADS).reshape(1, KV_WIDTH))


def _t5_bucket(rel):
    n = jnp.maximum(rel, 0)
    nf = jnp.maximum(n, 1).astype(F32)
    large = MAX_EXACT + (jnp.log(nf / MAX_EXACT) / math.log(REL_MAX_DIST / MAX_EXACT)
                         * (N_BUCKETS - MAX_EXACT)).astype(jnp.int32)
    large = jnp.minimum(large, N_BUCKETS - 1)
    return jnp.where(n < MAX_EXACT, n, large)


def _bias_table(rel, rel_table):
    bias = rel_table[_t5_bucket(rel)].astype(F32)
    valid = (rel >= 0) & (rel < WINDOW)
    bias = jnp.where(valid[..., None], bias, NEG_INF)
    return jnp.moveaxis(bias, -1, 0)


def _attn_prompt_kernel(sink_ref, q_ref, kp_ref, kc_ref, vp_ref, vc_ref, bias_ref, o_ref):
    first = pl.program_id(1) == 0
    q = q_ref[...]
    kk = jnp.concatenate([kp_ref[...], kc_ref[...]], axis=0).astype(BF16)
    vv = jnp.concatenate([vp_ref[...], vc_ref[...]], axis=0).astype(BF16)
    col = lax.broadcasted_iota(jnp.int32, (GROUP * ATTN_BLOCK, 2 * ATTN_BLOCK), 1)
    row = lax.broadcasted_iota(jnp.int32, (GROUP * ATTN_BLOCK, 1), 0)
    no_prev = jnp.logical_and(first, col < ATTN_BLOCK)
    outs = []
    for kv in range(N_KV_HEADS):
        heads = range(kv * GROUP, (kv + 1) * GROUP)
        qg = jnp.concatenate([q[:, h * HEAD_DIM:(h + 1) * HEAD_DIM] for h in heads], axis=0)
        kh = kk[:, kv * HEAD_DIM:(kv + 1) * HEAD_DIM]
        vh = vv[:, kv * HEAD_DIM:(kv + 1) * HEAD_DIM]
        s = lax.dot_general(qg, kh, (((1,), (1,)), ((), ())), preferred_element_type=F32)
        s = s + bias_ref[kv]
        s = jnp.where(no_prev, NEG_INF, s)
        sink = jnp.zeros((GROUP * ATTN_BLOCK, 1), F32)
        for g, h in enumerate(heads):
            sink = jnp.where(row // ATTN_BLOCK == g, sink_ref[h], sink)
        m = jnp.maximum(jnp.max(s, axis=-1, keepdims=True), sink)
        p = jnp.exp(s - m)
        denom = jnp.sum(p, axis=-1, keepdims=True) + jnp.exp(sink - m)
        o = jnp.dot(p.astype(BF16), vh, preferred_element_type=F32) / denom
        outs += [o[g * ATTN_BLOCK:(g + 1) * ATTN_BLOCK] for g in range(GROUP)]
    o_ref[...] = jnp.concatenate(outs, axis=-1).astype(BF16)


def _attn_prompt(q, k, v, sinks, rel_table):
    b, s = q.shape[:2]
    nblk = s // ATTN_BLOCK
    qi = jnp.arange(ATTN_BLOCK, dtype=jnp.int32)[:, None]
    si = jnp.arange(2 * ATTN_BLOCK, dtype=jnp.int32)[None, :]
    bias = _bias_table(qi + ATTN_BLOCK - si, rel_table)
    bias = bias.reshape(N_KV_HEADS, GROUP * ATTN_BLOCK, 2 * ATTN_BLOCK)
    cur = lambda w: pl.BlockSpec((None, ATTN_BLOCK, w), lambda i, j, *_: (i, j, 0))
    prev = lambda w: pl.BlockSpec((None, ATTN_BLOCK, w), lambda i, j, *_: (i, jnp.maximum(j - 1, 0), 0))
    return pl.pallas_call(
        _attn_prompt_kernel,
        grid_spec=pltpu.PrefetchScalarGridSpec(
            num_scalar_prefetch=1,
            grid=(b, nblk),
            in_specs=[cur(ATTN_WIDTH), prev(KV_WIDTH), cur(KV_WIDTH), prev(KV_WIDTH), cur(KV_WIDTH),
                      pl.BlockSpec(bias.shape, lambda i, j, *_: (0, 0, 0))],
            out_specs=cur(ATTN_WIDTH)),
        out_shape=jax.ShapeDtypeStruct((b, s, ATTN_WIDTH), BF16),
        compiler_params=_cparams(("parallel", "parallel")),
        name="attn_prompt",
    )(sinks.astype(F32), q, k, k, v, v, bias)


def _attn_sample_kernel(sink_ref, q_ref, kc_ref, vc_ref, kn_ref, vn_ref, bias_ref, bnew_ref,
                        o_ref, nk_ref, nv_ref):
    kc = kc_ref[...]
    vc = vc_ref[...]
    kn = kn_ref[...]
    vn = vn_ref[...]
    w = kc.shape[1]
    pos = lax.broadcasted_iota(jnp.int32, kc.shape, 1)
    nk_ref[...] = jnp.where(pos == w - 1, kn[:, None, :], pltpu.roll(kc, w - 1, 1))
    nv_ref[...] = jnp.where(pos == w - 1, vn[:, None, :], pltpu.roll(vc, w - 1, 1))
    gi = lax.broadcasted_iota(jnp.int32, (1, GROUP, 1), 1)
    for kv in range(N_KV_HEADS):
        sl = slice(kv * HEAD_DIM, (kv + 1) * HEAD_DIM)
        qg = q_ref[:, kv]
        s = jnp.einsum('bgd,bsd->bgs', qg, kc[:, :, sl].astype(BF16), preferred_element_type=F32)
        s = s + bias_ref[kv][None]
        s_new = jnp.sum(qg.astype(F32) * kn[:, None, sl], axis=-1, keepdims=True) + bnew_ref[kv][None]
        sink = jnp.zeros((1, GROUP, 1), F32)
        for g in range(GROUP):
            sink = jnp.where(gi == g, sink_ref[kv * GROUP + g], sink)
        m = jnp.maximum(jnp.maximum(jnp.max(s, axis=-1, keepdims=True), s_new), sink)
        p = jnp.exp(s - m)
        p_new = jnp.exp(s_new - m)
        denom = jnp.sum(p, axis=-1, keepdims=True) + p_new + jnp.exp(sink - m)
        o = jnp.einsum('bgs,bsd->bgd', p.astype(BF16), vc[:, :, sl].astype(BF16), preferred_element_type=F32)
        o = o + p_new * vn[:, None, sl]
        o_ref[:, kv] = o / denom


def _attn_sample(q, k_new, v_new, cache_k, cache_v, sinks, rel_table, *, tile=16):
    bd, w = cache_k.shape[:2]
    rel = w - jnp.arange(w, dtype=jnp.int32)
    bias = _bias_table(rel, rel_table).reshape(N_KV_HEADS, GROUP, w)
    bnew = _bias_table(jnp.zeros((1,), jnp.int32), rel_table).reshape(N_KV_HEADS, GROUP, 1)
    q4 = q.reshape(bd, N_KV_HEADS, GROUP, HEAD_DIM)
    spec4 = pl.BlockSpec((tile, N_KV_HEADS, GROUP, HEAD_DIM), lambda i, *_: (i, 0, 0, 0))
    cache = pl.BlockSpec((tile, w, KV_WIDTH), lambda i, *_: (i, 0, 0))
    new = pl.BlockSpec((tile, KV_WIDTH), lambda i, *_: (i, 0))
    const3 = lambda a: pl.BlockSpec(a.shape, lambda i, *_: (0, 0, 0))
    o, nk, nv = pl.pallas_call(
        _attn_sample_kernel,
        grid_spec=pltpu.PrefetchScalarGridSpec(
            num_scalar_prefetch=1,
            grid=(bd // tile,),
            in_specs=[spec4, cache, cache, new, new, const3(bias), const3(bnew)],
            out_specs=[spec4, cache, cache]),
        out_shape=[jax.ShapeDtypeStruct(q4.shape, F32),
                   jax.ShapeDtypeStruct(cache_k.shape, F32),
                   jax.ShapeDtypeStruct(cache_v.shape, F32)],
        compiler_params=_cparams(("parallel",)),
        name="attn_sample",
    )(sinks.astype(F32), q4, cache_k, cache_v, k_new, v_new, bias, bnew)
    return o.reshape(bd, ATTN_WIDTH), nk, nv


def _pool_project(d_groups, wp_ref, ps_ref):
    out = [_bdot(d, wp_ref[g]) for g, d in enumerate(d_groups)]
    return (jnp.concatenate(out, axis=-1) * ps_ref[...]).astype(BF16)


def _pool_prompt_kernel(u_ref, halo_ref, wp_ref, ps_ref, o_ref, ext):
    t = pl.program_id(1)
    tile = u_ref.shape[0]
    halo = halo_ref[...]
    ext[0:2 * SUBLANES, :] = jnp.where(t == 0, jnp.zeros_like(halo), halo)
    ext[2 * SUBLANES:, :] = u_ref[...]
    pos = t * tile + lax.broadcasted_iota(jnp.int32, (tile, 1), 0)
    base = 2 * SUBLANES
    ds = []
    for g, w in enumerate(POOL_WINDOWS):
        sl = slice(g * POOL_GROUP, (g + 1) * POOL_GROUP)
        acc = ext[base:base + tile, sl]
        for j in range(1, w):
            acc = acc + ext[base - j:base - j + tile, sl]
        cnt = jnp.minimum(pos + 1, w).astype(F32)
        ds.append(acc / cnt - ext[base:base + tile, sl])
    o_ref[...] = _pool_project(ds, wp_ref, ps_ref)


def _pool_prompt(u, w_pool, pool_scale, *, tile=512):
    b, s, c = u.shape
    hb = 2 * SUBLANES
    return pl.pallas_call(
        _pool_prompt_kernel,
        grid=(b, s // tile),
        in_specs=[pl.BlockSpec((None, tile, c), lambda i, t: (i, t, 0)),
                  pl.BlockSpec((None, hb, c), lambda i, t: (i, jnp.maximum(t * (tile // hb) - 1, 0), 0)),
                  pl.BlockSpec(w_pool.shape, lambda i, t: (0, 0, 0)),
                  pl.BlockSpec((1, c), lambda i, t: (0, 0))],
        out_specs=pl.BlockSpec((None, tile, c), lambda i, t: (i, t, 0)),
        out_shape=jax.ShapeDtypeStruct((b, s, c), BF16),
        scratch_shapes=[pltpu.VMEM((tile + hb, c), F32)],
        compiler_params=_cparams(("parallel", "parallel")),
        name="pool_prompt",
    )(u, u, w_pool, pool_scale.reshape(1, c))


def _pool_sample_kernel(st_ref, u_ref, wp_ref, ps_ref, o_ref, ns_ref):
    u = u_ref[...]
    ns_ref[0:POOL_BUF - 1] = st_ref[1:POOL_BUF]
    ns_ref[POOL_BUF - 1] = u
    ds = []
    for g, w in enumerate(POOL_WINDOWS):
        sl = slice(g * POOL_GROUP, (g + 1) * POOL_GROUP)
        acc = u[:, sl]
        for j in range(1, w):
            acc = acc + st_ref[POOL_BUF - j][:, sl]
        cnt = float(min(PAST_LEN + 1, w))
        ds.append(acc / cnt - u[:, sl])
    o_ref[...] = _pool_project(ds, wp_ref, ps_ref)


def _pool_sample(state_t, u, w_pool, pool_scale):
    nb, bd, c = state_t.shape
    full = lambda a: pl.BlockSpec(a.shape, lambda: (0,) * a.ndim)
    ps = pool_scale.reshape(1, c)
    return pl.pallas_call(
        _pool_sample_kernel,
        in_specs=[full(state_t), full(u), full(w_pool), full(ps)],
        out_specs=[pl.BlockSpec((bd, c), lambda: (0, 0)), full(state_t)],
        out_shape=[jax.ShapeDtypeStruct((bd, c), BF16), jax.ShapeDtypeStruct(state_t.shape, F32)],
        compiler_params=pltpu.CompilerParams(vmem_limit_bytes=VMEM_LIMIT),
        name="pool_sample",
    )(state_t, u, w_pool, ps)


def _route_kernel(x_ref, attn_ref, pool_ref, gm_ref, sh_ref, sc_ref, wo_ref, nf_ref, wr_ref, br_ref,
                  tri_t_ref, tri_e_ref,
                  x2_ref, hs_ref, slot_ref, cnt_ref):
    tile = x_ref.shape[0]
    mix = (jnp.dot(attn_ref[...], wo_ref[:ATTN_WIDTH, :], preferred_element_type=F32)
           + jnp.dot(pool_ref[...], wo_ref[ATTN_WIDTH:, :], preferred_element_type=F32))
    x2 = x_ref[...] + gm_ref[...] * mix
    x2_ref[...] = x2
    h = _rms(x2, nf_ref[...]) * (1.0 + sc_ref[...]) + sh_ref[...]

    hh, hl = _split(h)
    wh, wl = _split(wr_ref[...])
    nt = functools.partial(lax.dot_general, dimension_numbers=(((1,), (1,)), ((), ())),
                           preferred_element_type=F32)
    logits = nt(wh, hh) + nt(wl, hh) + nt(wh, hl) + br_ref[...]

    eidx = lax.broadcasted_iota(jnp.int32, (N_EXPERTS, tile), 0).astype(F32)
    work = logits
    tops, picks = [], []
    for _ in range(TOP_K):
        m = jnp.max(work, axis=0, keepdims=True)
        pick = jnp.min(jnp.where(work == m, eidx, float(N_EXPERTS)), axis=0, keepdims=True)
        work = jnp.where(eidx == pick, -jnp.inf, work)
        tops.append(m)
        picks.append(pick)
    ex = [jnp.exp(t - tops[0]) for t in tops]
    den = ex[0] + ex[1] + ex[2] + ex[3]
    gates = [e / den for e in ex]

    sel = jnp.zeros((N_EXPERTS, tile), F32)
    for pick in picks:
        sel = sel + (eidx == pick).astype(F32)
    rank = jnp.dot(sel.astype(BF16), tri_t_ref[...], preferred_element_type=F32)
    cnt = jnp.sum(sel, axis=1, keepdims=True)
    padded = jnp.ceil(cnt * (1.0 / CHUNK)) * CHUNK
    seg = jnp.dot(tri_e_ref[...], jnp.broadcast_to(padded, (N_EXPERTS, LANES)).astype(BF16),
                  preferred_element_type=F32)[:, :1]
    dest = seg + rank
    cnt_ref[...] = jnp.broadcast_to(cnt, (N_EXPERTS, LANES))
    slots = [jnp.sum(jnp.where(eidx == pick, dest, 0.0), axis=0, keepdims=True) for pick in picks]
    slot_ref[...] = jnp.concatenate(slots + slots, axis=0)

    ridx = lax.broadcasted_iota(jnp.int32, (TILE_ROWS, tile), 0).astype(F32)
    perm = jnp.zeros((TILE_ROWS, tile), F32)
    gmat = jnp.zeros((TILE_ROWS, tile), F32)
    for s, g in zip(slots, gates):
        hit = ridx == s
        perm = jnp.where(hit, 1.0, perm)
        gmat = jnp.where(hit, g, gmat)
    hs_ref[:, :PACKED_W] = _pack_rows(jnp.dot(perm.astype(BF16), hh, preferred_element_type=F32))
    gate_rows = jnp.broadcast_to(jnp.sum(gmat, axis=1, keepdims=True), (TILE_ROWS, LANES))
    hs_ref[:, PACKED_W:] = lax.bitcast_convert_type(gate_rows, jnp.int32)


def _route(x2d, attn, pool, gm, sh, sc, w_out, norm_ffn, w_router_t, b_router, hs_prev, *, tile, rows_per_mod,
           tile0, extra_tiles):
    n = x2d.shape[0]
    nt = n // tile
    steps = nt + extra_tiles
    mrows = gm.shape[1]
    last = lambda i: jnp.minimum(i, nt - 1)
    mod_spec = pl.BlockSpec((None, mrows, D_MODEL), lambda i: ((last(i) * tile) // rows_per_mod, 0, 0))
    const = lambda shape: pl.BlockSpec(shape, lambda i: (0,) * len(shape))
    row = lambda w: pl.BlockSpec((tile, w), lambda i: (last(i), 0))
    tri_t = (jnp.arange(tile)[:, None] < jnp.arange(tile)[None, :]).astype(BF16)
    tri_e = (jnp.arange(N_EXPERTS)[None, :] < jnp.arange(N_EXPERTS)[:, None]).astype(BF16)
    in_specs = [row(D_MODEL), row(ATTN_WIDTH), row(POOL_WIDTH), mod_spec, mod_spec, mod_spec,
                const((D_MODEL, D_MODEL)), const((1, D_MODEL)), const((N_EXPERTS, D_MODEL)),
                const((N_EXPERTS, 1)), const((tile, tile)), const((N_EXPERTS, N_EXPERTS))]
    args = [x2d, attn, pool, gm, sh, sc, w_out, norm_ffn.reshape(1, D_MODEL), w_router_t,
            b_router.reshape(N_EXPERTS, 1), tri_t, tri_e]
    n_in = len(args)
    kern = _route_kernel
    aliases = {}
    hs_rows = (tile0 + steps) * TILE_ROWS
    if hs_prev is not None:
        in_specs.append(pl.BlockSpec(memory_space=pl.ANY))
        args.append(hs_prev)
        aliases = {n_in: 1}
        kern = lambda *refs: _route_kernel(*refs[:n_in], *refs[n_in + 1:])
        hs_rows = hs_prev.shape[0]
    return pl.pallas_call(
        kern,
        grid=(steps,),
        in_specs=in_specs,
        out_specs=[pl.BlockSpec((tile, D_MODEL), lambda i: (last(i), 0)),
                   pl.BlockSpec((TILE_ROWS, ROW_W), lambda i: (i + tile0, 0)),
                   pl.BlockSpec((None, 2 * TOP_K, tile), lambda i: (last(i), 0, 0)),
                   pl.BlockSpec((None, N_EXPERTS, LANES), lambda i: (last(i), 0, 0))],
        out_shape=[jax.ShapeDtypeStruct((n, D_MODEL), F32),
                   jax.ShapeDtypeStruct((hs_rows, ROW_W), jnp.int32),
                   jax.ShapeDtypeStruct((nt, 2 * TOP_K, tile), F32),
                   jax.ShapeDtypeStruct((nt, N_EXPERTS, LANES), F32)],
        input_output_aliases=aliases,
        compiler_params=_cparams(("arbitrary",)),
        name="route",
    )(*args)


def _moe_kernel(src_ref, be_ref, nb_ref, hs_hbm, w1_ref, b1_ref, w2_ref, b2_ref, out_hbm,
                lhs, obuf, w1c, w2c, sem_in, sem_out, *, scratch_chunk):
    b = pl.program_id(0)
    nb = nb_ref[0]
    slot = b % 2

    def chunk_rows(c):
        return pl.ds(pl.multiple_of(c * CHUNK, CHUNK), CHUNK)

    def start_in(blk, s):
        for j in range(BLOCK_CHUNKS):
            c = src_ref[blk * BLOCK_CHUNKS + j]
            c = jnp.where(c < 0, ZERO_CHUNK, c)
            pltpu.make_async_copy(hs_hbm.at[chunk_rows(c)], lhs.at[s, pl.ds(j * CHUNK, CHUNK)],
                                  sem_in.at[s]).start()

    def wait_in(s):
        pltpu.make_async_copy(hs_hbm.at[pl.ds(0, MOE_BLOCK)], lhs.at[s], sem_in.at[s]).wait()

    def start_out(blk, s):
        for j in range(BLOCK_CHUNKS):
            c = src_ref[blk * BLOCK_CHUNKS + j]
            c = jnp.where(c < 0, scratch_chunk + s * BLOCK_CHUNKS + j, c)
            pltpu.make_async_copy(obuf.at[s, pl.ds(j * CHUNK, CHUNK)],
                                  out_hbm.at[chunk_rows(c), pl.ds(0, PACKED_W)], sem_out.at[s]).start()

    def wait_out(s):
        pltpu.make_async_copy(obuf.at[s], out_hbm.at[pl.ds(0, MOE_BLOCK), pl.ds(0, PACKED_W)],
                              sem_out.at[s]).wait()

    @pl.when(b < nb)
    def _():
        @pl.when(b == 0)
        def _():
            start_in(0, 0)

        wait_in(slot)

        @pl.when(jnp.logical_or(b == 0, be_ref[b] != be_ref[jnp.maximum(b - 1, 0)]))
        def _():
            w1c[...] = w1_ref[...].astype(BF16)
            w2c[...] = w2_ref[...].astype(BF16)

        @pl.when(b >= 2)
        def _():
            wait_out(slot)

        start_in(b + 1, 1 - slot)

        rows = lhs[slot]
        xh, xl = _unpack_rows(rows[:, :PACKED_W])
        gate_w = lax.bitcast_convert_type(rows[:, PACKED_W:PACKED_W + 1], F32)
        gu = (jnp.dot(xh, w1c[:PACKED_W, :], preferred_element_type=F32)
              + jnp.dot(xl, w1c[PACKED_W:, :], preferred_element_type=F32) + b1_ref[...])
        gate = jnp.minimum(gu[:, :D_FF], SWIGLU_LIMIT)
        up = jnp.clip(gu[:, D_FF:], -SWIGLU_LIMIT, SWIGLU_LIMIT)
        act = (up + 1.0) * (gate * jax.nn.sigmoid(SWIGLU_ALPHA * gate))
        y = jnp.dot(act.astype(BF16), w2c[...], preferred_element_type=F32) + b2_ref[...]
        obuf[slot] = _pack_rows((y * gate_w).astype(BF16).astype(F32))

        start_out(b, slot)

        @pl.when(b == nb - 1)
        def _():
            wait_in(1 - slot)

            @pl.when(b >= 1)
            def _():
                wait_out(1 - slot)
            wait_out(slot)


def _moe(hs, src, block_e, nblocks, w1, b1, w2, b2, n_blocks_max, scratch_chunk):
    wspec = lambda shape: pl.BlockSpec((None,) + shape, lambda b, src, be, nb: (be[b], 0, 0))
    hbm = pl.BlockSpec(memory_space=pl.ANY)
    return pl.pallas_call(
        functools.partial(_moe_kernel, scratch_chunk=scratch_chunk),
        grid_spec=pltpu.PrefetchScalarGridSpec(
            num_scalar_prefetch=3,
            grid=(n_blocks_max,),
            in_specs=[hbm,
                      wspec((D_MODEL, 2 * D_FF)), wspec((1, 2 * D_FF)),
                      wspec((D_FF, D_MODEL)), wspec((1, D_MODEL))],
            out_specs=hbm,
            scratch_shapes=[pltpu.VMEM((2, MOE_BLOCK, ROW_W), jnp.int32),
                            pltpu.VMEM((2, MOE_BLOCK, PACKED_W), jnp.int32),
                            pltpu.VMEM((D_MODEL, 2 * D_FF), BF16),
                            pltpu.VMEM((D_FF, D_MODEL), BF16),
                            pltpu.SemaphoreType.DMA((2,)),
                            pltpu.SemaphoreType.DMA((2,))]),
        out_shape=jax.ShapeDtypeStruct(hs.shape, jnp.int32),
        input_output_aliases={3: 0},
        compiler_params=_cparams(("arbitrary",)),
        name="moe_experts",
    )(src, block_e, nblocks, hs, w1, b1.reshape(N_EXPERTS, 1, 2 * D_FF), w2, b2.reshape(N_EXPERTS, 1, D_MODEL))


def _plan(cnt, n_blocks_max):
    nt = cnt.shape[0]
    nch = (cnt + (CHUNK - 1)) // CHUNK
    lstart = jnp.cumsum(nch, axis=1) - nch
    ne = jnp.sum(nch, axis=0)
    nbe = (ne + (BLOCK_CHUNKS - 1)) // BLOCK_CHUNKS
    bend = jnp.cumsum(nbe)
    nblocks = bend[-1]
    gstart = (bend - nbe)[None, :] * BLOCK_CHUNKS + (jnp.cumsum(nch, axis=0) - nch)
    gs = gstart.T.reshape(-1)
    nc = nch.T.reshape(-1)
    s0 = (jnp.arange(nt, dtype=jnp.int32)[:, None] * TILE_CHUNKS + lstart).T.reshape(-1)
    c = jnp.arange((n_blocks_max + 1) * BLOCK_CHUNKS, dtype=jnp.int32)[:, None]
    inside = jnp.logical_and(c >= gs[None, :], c < (gs + nc)[None, :])
    src = jnp.sum(jnp.where(inside, (s0 - gs)[None, :] + c + 1, 0), axis=1) - 1
    blk = jnp.arange(n_blocks_max, dtype=jnp.int32)
    be = jnp.sum((blk[:, None] >= bend[None, :]).astype(jnp.int32), axis=1)
    be_last = jnp.sum((nblocks - 1 >= bend).astype(jnp.int32))
    be = jnp.minimum(be, be_last).astype(jnp.int32)
    return src, be, nblocks.reshape(1).astype(jnp.int32)


def _combine_kernel(o_ref, slot_ref, x2_ref, gf_ref, y_ref):
    tile = x2_ref.shape[0]
    ridx = lax.broadcasted_iota(jnp.int32, (TILE_ROWS, tile), 0).astype(F32)
    hit = ridx == slot_ref[0:1, :]
    for k in range(1, TOP_K):
        hit = jnp.logical_or(hit, ridx == slot_ref[k:k + 1, :])
    perm = jnp.where(hit, 1.0, 0.0).astype(BF16)
    tn = functools.partial(lax.dot_general, dimension_numbers=(((0,), (0,)), ((), ())),
                           preferred_element_type=F32)
    oh, ol = _unpack_rows(o_ref[...])
    y = jnp.concatenate([tn(perm, oh), tn(perm, ol)], axis=1)
    y_ref[...] = x2_ref[...] + gf_ref[...] * y


def _combine(outs, slots, x2, gf, *, tile, rows_per_mod, tile0):
    n = x2.shape[0]
    mrows = gf.shape[1]
    return pl.pallas_call(
        _combine_kernel,
        grid=(n // tile,),
        in_specs=[pl.BlockSpec((TILE_ROWS, PACKED_W), lambda i: (i + tile0, 0)),
                  pl.BlockSpec((None, 2 * TOP_K, tile), lambda i: (i, 0, 0)),
                  pl.BlockSpec((tile, D_MODEL), lambda i: (i, 0)),
                  pl.BlockSpec((None, mrows, D_MODEL), lambda i: ((i * tile) // rows_per_mod, 0, 0))],
        out_specs=pl.BlockSpec((tile, D_MODEL), lambda i: (i, 0)),
        out_shape=jax.ShapeDtypeStruct((n, D_MODEL), F32),
        compiler_params=_cparams(("parallel",)),
        name="combine",
    )(outs, slots, x2, gf)


def kernel(x_prompt, x_sample, cache_k, cache_v, state_pool, c_prompt, c_sample, rel_bias, norm_mix, w_ada,
           b_ada, w_in, q_norm, k_norm, sinks, w_pool, pool_scale, w_out, norm_ffn, w_router, b_router,
           w1, b1, w2, b2):
    depth = w_in.shape[0]
    assert depth == 1
    l = 0
    bp, sp, _ = x_prompt.shape
    bs = x_sample.shape[0]
    assert x_sample.shape[1] == 1 and sp % SORT_TILE == 0 and bs <= SORT_TILE
    n_p = bp * sp
    tiles_p = n_p // SORT_TILE
    max_chunks = tiles_p * (SORT_TILE * TOP_K // CHUNK + N_EXPERTS) + (bs * TOP_K // CHUNK + N_EXPERTS)
    n_blocks_max = -(-max_chunks // BLOCK_CHUNKS) + N_EXPERTS

    mod = _modulation(jnp.concatenate([c_prompt, c_sample], axis=0), w_ada[l], b_ada[l])
    mod_p = [m.reshape(bp, 1, D_MODEL) for m in jnp.split(mod[:bp], 6, axis=-1)]
    mod_s = [m.reshape(1, bs, D_MODEL) for m in jnp.split(mod[bp:], 6, axis=-1)]

    head = jnp.arange(ATTN_WIDTH) // HEAD_DIM
    bd = (head[:, None] == head[None, :]).astype(BF16)
    w_in_b = w_in[l].astype(BF16)
    w_out_b = w_out[l].astype(BF16)
    w_pool_b = w_pool[l].astype(BF16)
    w_router_t = w_router[l].T

    xp = x_prompt.reshape(n_p, D_MODEL)
    q, k, v, u = _mixer_inputs(xp, mod_p[0], mod_p[1], norm_mix[l], w_in_b, bd, q_norm[l], k_norm[l],
                               tile=512, rows_per_mod=sp, precise=False)
    k3 = k.reshape(bp, sp, KV_WIDTH)
    v3 = v.reshape(bp, sp, KV_WIDTH)
    u3 = u.reshape(bp, sp, POOL_WIDTH)
    attn = _attn_prompt(q.reshape(bp, sp, ATTN_WIDTH), k3, v3, sinks[l], rel_bias)
    pool = _pool_prompt(u3, w_pool_b, pool_scale[l])
    keep = min(WINDOW, sp)
    nkp = k3[:, -keep:].reshape(bp, keep, N_KV_HEADS, HEAD_DIM)
    nvp = v3[:, -keep:].reshape(bp, keep, N_KV_HEADS, HEAD_DIM)
    if sp >= POOL_BUF:
        npp = u3[:, -POOL_BUF:]
    else:
        npp = jnp.concatenate([jnp.zeros((bp, POOL_BUF - sp, POOL_WIDTH), F32), u3], axis=1)
    x2_p, hs, slots_p, cnt_p = _route(
        xp, attn.reshape(n_p, ATTN_WIDTH), pool.reshape(n_p, POOL_WIDTH), mod_p[2], mod_p[3], mod_p[4],
        w_out_b, norm_ffn[l], w_router_t, b_router[l], None,
        tile=SORT_TILE, rows_per_mod=sp, tile0=0, extra_tiles=2)

    xs = x_sample.reshape(bs, D_MODEL)
    qs, ks, vs, us = _mixer_inputs(xs, mod_s[0], mod_s[1], norm_mix[l], w_in[l], bd, q_norm[l], k_norm[l],
                                   tile=bs, rows_per_mod=bs, precise=True)
    wbuf = cache_k.shape[2]
    attn_s, nks, nvs = _attn_sample(qs, ks, vs, cache_k[l].reshape(bs, wbuf, KV_WIDTH),
                                    cache_v[l].reshape(bs, wbuf, KV_WIDTH), sinks[l], rel_bias)
    pool_s, nps_t = _pool_sample(jnp.swapaxes(state_pool[l], 0, 1), us, w_pool_b, pool_scale[l])
    x2_s, hs, slots_s, cnt_s = _route(
        xs, attn_s.astype(BF16), pool_s, mod_s[2], mod_s[3], mod_s[4], w_out_b, norm_ffn[l],
        w_router_t, b_router[l], hs, tile=bs, rows_per_mod=bs, tile0=tiles_p, extra_tiles=0)

    cnt = jnp.concatenate([cnt_p[:, :, 0], cnt_s[:, :, 0]], axis=0).astype(jnp.int32)
    src, block_e, nblocks = _plan(cnt, n_blocks_max)
    outs = _moe(hs, src, block_e, nblocks, w1[l], b1[l], w2[l], b2[l], n_blocks_max,
                scratch_chunk=(tiles_p + 1) * TILE_CHUNKS)

    y_p = _combine(outs, slots_p, x2_p, mod_p[5], tile=SORT_TILE, rows_per_mod=sp, tile0=0)
    y_s = _combine(outs, slots_s, x2_s, mod_s[5], tile=bs, rows_per_mod=bs, tile0=tiles_p)

    return (y_p.reshape(bp, sp, D_MODEL), y_s.reshape(bs, 1, D_MODEL),
            nkp[None], nvp[None], npp[None],
            nks.reshape(1, bs, wbuf, N_KV_HEADS, HEAD_DIM), nvs.reshape(1, bs, wbuf, N_KV_HEADS, HEAD_DIM),
            jnp.swapaxes(nps_t, 0, 1)[None])
```

```python
import functools
import math

import jax
import jax.numpy as jnp
from jax import lax
from jax.experimental import pallas as pl
from jax.experimental.pallas import tpu as pltpu

F32 = jnp.float32
BF16 = jnp.bfloat16

D_MODEL = 1024
HEAD_DIM = 64
N_HEADS = 8
N_KV_HEADS = 2
GROUP = N_HEADS // N_KV_HEADS
ATTN_WIDTH = N_HEADS * HEAD_DIM
KV_WIDTH = N_KV_HEADS * HEAD_DIM
POOL_WIDTH = D_MODEL - ATTN_WIDTH
POOL_WINDOWS = (2, 4, 8, 16)
POOL_GROUP = POOL_WIDTH // len(POOL_WINDOWS)
POOL_BUF = max(POOL_WINDOWS) - 1
IN_WIDTH = ATTN_WIDTH + 2 * KV_WIDTH + POOL_WIDTH
WINDOW = 128
ATTN_BLOCK = 128
N_BUCKETS = 32
MAX_EXACT = 16
REL_MAX_DIST = 128
N_EXPERTS = 32
TOP_K = 4
D_FF = D_MODEL
SWIGLU_LIMIT = 7.0
SWIGLU_ALPHA = 1.702
EPS = 1e-6
NEG_INF = -1e30
PAST_LEN = 16384

LANES = 128
SUBLANES = 8
VMEM_LIMIT = 56 * 1024 * 1024

SORT_TILE = 256
CHUNK = SUBLANES
TILE_ROWS = -(-(SORT_TILE * TOP_K + N_EXPERTS * (CHUNK - 1)) // LANES) * LANES
TILE_CHUNKS = TILE_ROWS // CHUNK
MOE_BLOCK = 256
BLOCK_CHUNKS = MOE_BLOCK // CHUNK
FF_CHUNK = 1024
PACKED_W = D_MODEL // 2
ROW_W = PACKED_W + LANES
ZERO_CHUNK = TILE_CHUNKS - 1


def _bdot(a, b):
    return jnp.dot(a.astype(BF16), b.astype(BF16), preferred_element_type=F32)


def _split(a):
    hi = a.astype(BF16)
    lo = (a - hi.astype(F32)).astype(BF16)
    return hi, lo


def _dot3(a, b):
    ah, al = _split(a)
    bh, bl = _split(b)
    d = functools.partial(jnp.dot, preferred_element_type=F32)
    return d(ah, bh) + d(al, bh) + d(ah, bl)


def _pack_rows(x):
    bits = lax.bitcast_convert_type(x, jnp.int32)
    return bits[:, :PACKED_W] | lax.shift_right_logical(bits[:, PACKED_W:], 16)


def _unpack_rows(w):
    hi = lax.bitcast_convert_type(w & jnp.int32(-65536), F32)
    lo = lax.bitcast_convert_type(lax.shift_left(w, 16), F32)
    return hi.astype(BF16), lo.astype(BF16)


def _rms(x, g):
    return x * lax.rsqrt(jnp.mean(x * x, axis=-1, keepdims=True) + EPS) * g


def _cparams(sem, **kw):
    return pltpu.CompilerParams(dimension_semantics=sem, vmem_limit_bytes=VMEM_LIMIT, **kw)


def _ada_kernel(c_ref, w_ref, b_ref, o_ref):
    c = c_ref[...]
    s = c * jax.nn.sigmoid(c)
    o_ref[...] = _dot3(s, w_ref[...]) + b_ref[...]


def _modulation(c, w_ada, b_ada):
    rows = c.shape[0]
    n = w_ada.shape[1]
    tn = 512
    return pl.pallas_call(
        _ada_kernel,
        grid=(n // tn,),
        in_specs=[pl.BlockSpec((rows, D_MODEL), lambda j: (0, 0)),
                  pl.BlockSpec((D_MODEL, tn), lambda j: (0, j)),
                  pl.BlockSpec((1, tn), lambda j: (0, j))],
        out_specs=pl.BlockSpec((rows, tn), lambda j: (0, j)),
        out_shape=jax.ShapeDtypeStruct((rows, n), F32),
        compiler_params=_cparams(("parallel",)),
        name="modulation",
    )(c, w_ada, b_ada.reshape(1, n))


def _head_rms(t, bd, g):
    hi, lo = _split(t * t)
    ss = jnp.dot(hi, bd, preferred_element_type=F32) + jnp.dot(lo, bd, preferred_element_type=F32)
    return t * lax.rsqrt(ss * (1.0 / HEAD_DIM) + EPS) * g


def _mixin_kernel(x_ref, sh_ref, sc_ref, g_ref, w_ref, bd_ref, qn_ref, kn_ref,
                  q_ref, k_ref, v_ref, u_ref, *, precise):
    h = _rms(x_ref[...], g_ref[...]) * (1.0 + sc_ref[...]) + sh_ref[...]
    z = _dot3(h, w_ref[...]) if precise else _bdot(h, w_ref[...])
    q = z[:, :ATTN_WIDTH]
    k = z[:, ATTN_WIDTH:ATTN_WIDTH + KV_WIDTH]
    bd = bd_ref[...]
    q = _head_rms(q, bd, qn_ref[...])
    k = _head_rms(k, bd[:KV_WIDTH, :KV_WIDTH], kn_ref[...])
    q_ref[...] = (q * (HEAD_DIM ** -0.5)).astype(BF16)
    k_ref[...] = k
    v_ref[...] = z[:, ATTN_WIDTH + KV_WIDTH:ATTN_WIDTH + 2 * KV_WIDTH]
    u_ref[...] = z[:, ATTN_WIDTH + 2 * KV_WIDTH:]


def _mixer_inputs(x2d, shift, scale, norm_mix, w_in, bd, q_norm, k_norm, *, tile, rows_per_mod, precise):
    n = x2d.shape[0]
    mrows = shift.shape[1]
    mod_spec = pl.BlockSpec((None, mrows, D_MODEL), lambda i: ((i * tile) // rows_per_mod, 0, 0))
    const = lambda shape: pl.BlockSpec(shape, lambda i: (0,) * len(shape))
    row = lambda w: pl.BlockSpec((tile, w), lambda i: (i, 0))
    return pl.pallas_call(
        functools.partial(_mixin_kernel, precise=precise),
        grid=(n // tile,),
        in_specs=[row(D_MODEL), mod_spec, mod_spec, const((1, D_MODEL)), const((D_MODEL, IN_WIDTH)),
                  const((ATTN_WIDTH, ATTN_WIDTH)), const((1, ATTN_WIDTH)), const((1, KV_WIDTH))],
        out_specs=[row(ATTN_WIDTH), row(KV_WIDTH), row(KV_WIDTH), row(POOL_WIDTH)],
        out_shape=[jax.ShapeDtypeStruct((n, ATTN_WIDTH), BF16),
                   jax.ShapeDtypeStruct((n, KV_WIDTH), F32),
                   jax.ShapeDtypeStruct((n, KV_WIDTH), F32),
                   jax.ShapeDtypeStruct((n, POOL_WIDTH), F32)],
        compiler_params=_cparams(("parallel",)),
        name="mixer_inputs",
    )(x2d, shift, scale, norm_mix.reshape(1, D_MODEL), w_in, bd,
      jnp.tile(q_norm, N_HEADS).reshape(1, ATTN_WIDTH), jnp.tile(k_norm, N_KV_HEADS).reshape(1, KV_WIDTH))


def _t5_bucket(rel):
    n = jnp.maximum(rel, 0)
    nf = jnp.maximum(n, 1).astype(F32)
    large = MAX_EXACT + (jnp.log(nf / MAX_EXACT) / math.log(REL_MAX_DIST / MAX_EXACT)
                         * (N_BUCKETS - MAX_EXACT)).astype(jnp.int32)
    large = jnp.minimum(large, N_BUCKETS - 1)
    return jnp.where(n < MAX_EXACT, n, large)


def _bias_table(rel, rel_table):
    bucket = _t5_bucket(rel)
    table = rel_table.astype(F32)
    ids = jnp.arange(N_BUCKETS, dtype=bucket.dtype).reshape((N_BUCKETS, 1) + (1,) * rel.ndim)
    onehot = bucket[None, None] == ids
    bias = jnp.sum(jnp.where(onehot, table.reshape(table.shape + (1,) * rel.ndim), 0.0), axis=0)
    valid = (rel >= 0) & (rel < WINDOW)
    return jnp.where(valid[None], bias, NEG_INF)


def _attn_prompt_kernel(sink_ref, q_ref, kp_ref, kc_ref, vp_ref, vc_ref, bias_ref, o_ref):
    first = pl.program_id(1) == 0
    q = q_ref[...]
    kk = jnp.concatenate([kp_ref[...], kc_ref[...]], axis=0).astype(BF16)
    vv = jnp.concatenate([vp_ref[...], vc_ref[...]], axis=0).astype(BF16)
    col = lax.broadcasted_iota(jnp.int32, (GROUP * ATTN_BLOCK, 2 * ATTN_BLOCK), 1)
    row = lax.broadcasted_iota(jnp.int32, (GROUP * ATTN_BLOCK, 1), 0)
    no_prev = jnp.logical_and(first, col < ATTN_BLOCK)
    outs = []
    for kv in range(N_KV_HEADS):
        heads = range(kv * GROUP, (kv + 1) * GROUP)
        qg = jnp.concatenate([q[:, h * HEAD_DIM:(h + 1) * HEAD_DIM] for h in heads], axis=0)
        kh = kk[:, kv * HEAD_DIM:(kv + 1) * HEAD_DIM]
        vh = vv[:, kv * HEAD_DIM:(kv + 1) * HEAD_DIM]
        s = lax.dot_general(qg, kh, (((1,), (1,)), ((), ())), preferred_element_type=F32)
        s = s + bias_ref[kv]
        s = jnp.where(no_prev, NEG_INF, s)
        sink = jnp.zeros((GROUP * ATTN_BLOCK, 1), F32)
        for g, h in enumerate(heads):
            sink = jnp.where(row // ATTN_BLOCK == g, sink_ref[h], sink)
        m = jnp.maximum(jnp.max(s, axis=-1, keepdims=True), sink)
        p = jnp.exp(s - m)
        denom = jnp.sum(p, axis=-1, keepdims=True) + jnp.exp(sink - m)
        o = jnp.dot(p.astype(BF16), vh, preferred_element_type=F32) / denom
        outs += [o[g * ATTN_BLOCK:(g + 1) * ATTN_BLOCK] for g in range(GROUP)]
    o_ref[...] = jnp.concatenate(outs, axis=-1).astype(BF16)


def _attn_prompt(q, k, v, sinks, rel_table):
    b, s = q.shape[:2]
    nblk = s // ATTN_BLOCK
    qi = jnp.arange(ATTN_BLOCK, dtype=jnp.int32)[:, None]
    si = jnp.arange(2 * ATTN_BLOCK, dtype=jnp.int32)[None, :]
    bias = _bias_table(qi + ATTN_BLOCK - si, rel_table)
    bias = bias.reshape(N_KV_HEADS, GROUP * ATTN_BLOCK, 2 * ATTN_BLOCK)
    cur = lambda w: pl.BlockSpec((None, ATTN_BLOCK, w), lambda i, j, *_: (i, j, 0))
    prev = lambda w: pl.BlockSpec((None, ATTN_BLOCK, w), lambda i, j, *_: (i, jnp.maximum(j - 1, 0), 0))
    return pl.pallas_call(
        _attn_prompt_kernel,
        grid_spec=pltpu.PrefetchScalarGridSpec(
            num_scalar_prefetch=1,
            grid=(b, nblk),
            in_specs=[cur(ATTN_WIDTH), prev(KV_WIDTH), cur(KV_WIDTH), prev(KV_WIDTH), cur(KV_WIDTH),
                      pl.BlockSpec(bias.shape, lambda i, j, *_: (0, 0, 0))],
            out_specs=cur(ATTN_WIDTH)),
        out_shape=jax.ShapeDtypeStruct((b, s, ATTN_WIDTH), BF16),
        compiler_params=_cparams(("parallel", "parallel")),
        name="attn_prompt",
    )(sinks.astype(F32), q, k, k, v, v, bias)


def _attn_sample_kernel(sink_ref, q_ref, kc_ref, vc_ref, kn_ref, vn_ref, bias_ref, bnew_ref,
                        o_ref, nk_ref, nv_ref):
    kc = kc_ref[...]
    vc = vc_ref[...]
    kn = kn_ref[...]
    vn = vn_ref[...]
    w = kc.shape[1]
    pos = lax.broadcasted_iota(jnp.int32, kc.shape, 1)
    nk_ref[...] = jnp.where(pos == w - 1, kn[:, None, :], pltpu.roll(kc, w - 1, 1))
    nv_ref[...] = jnp.where(pos == w - 1, vn[:, None, :], pltpu.roll(vc, w - 1, 1))
    gi = lax.broadcasted_iota(jnp.int32, (1, GROUP, 1), 1)
    for kv in range(N_KV_HEADS):
        sl = slice(kv * HEAD_DIM, (kv + 1) * HEAD_DIM)
        qg = q_ref[:, kv]
        s = jnp.einsum('bgd,bsd->bgs', qg, kc[:, :, sl].astype(BF16), preferred_element_type=F32)
        s = s + bias_ref[kv][None]
        s_new = jnp.sum(qg.astype(F32) * kn[:, None, sl], axis=-1, keepdims=True) + bnew_ref[kv][None]
        sink = jnp.zeros((1, GROUP, 1), F32)
        for g in range(GROUP):
            sink = jnp.where(gi == g, sink_ref[kv * GROUP + g], sink)
        m = jnp.maximum(jnp.maximum(jnp.max(s, axis=-1, keepdims=True), s_new), sink)
        p = jnp.exp(s - m)
        p_new = jnp.exp(s_new - m)
        denom = jnp.sum(p, axis=-1, keepdims=True) + p_new + jnp.exp(sink - m)
        o = jnp.einsum('bgs,bsd->bgd', p.astype(BF16), vc[:, :, sl].astype(BF16), preferred_element_type=F32)
        o = o + p_new * vn[:, None, sl]
        o_ref[:, kv] = o / denom


def _attn_sample(q, k_new, v_new, cache_k, cache_v, sinks, rel_table, *, tile=16):
    bd, w = cache_k.shape[:2]
    rel = w - jnp.arange(w, dtype=jnp.int32)
    bias = _bias_table(rel, rel_table).reshape(N_KV_HEADS, GROUP, w)
    bnew = _bias_table(jnp.zeros((1,), jnp.int32), rel_table).reshape(N_KV_HEADS, GROUP, 1)
    q4 = q.reshape(bd, N_KV_HEADS, GROUP, HEAD_DIM)
    spec4 = pl.BlockSpec((tile, N_KV_HEADS, GROUP, HEAD_DIM), lambda i, *_: (i, 0, 0, 0))
    cache = pl.BlockSpec((tile, w, KV_WIDTH), lambda i, *_: (i, 0, 0))
    new = pl.BlockSpec((tile, KV_WIDTH), lambda i, *_: (i, 0))
    const3 = lambda a: pl.BlockSpec(a.shape, lambda i, *_: (0, 0, 0))
    o, nk, nv = pl.pallas_call(
        _attn_sample_kernel,
        grid_spec=pltpu.PrefetchScalarGridSpec(
            num_scalar_prefetch=1,
            grid=(bd // tile,),
            in_specs=[spec4, cache, cache, new, new, const3(bias), const3(bnew)],
            out_specs=[spec4, cache, cache]),
        out_shape=[jax.ShapeDtypeStruct(q4.shape, F32),
                   jax.ShapeDtypeStruct(cache_k.shape, F32),
                   jax.ShapeDtypeStruct(cache_v.shape, F32)],
        compiler_params=_cparams(("parallel",)),
        name="attn_sample",
    )(sinks.astype(F32), q4, cache_k, cache_v, k_new, v_new, bias, bnew)
    return o.reshape(bd, ATTN_WIDTH), nk, nv


def _pool_project(d_groups, wp_ref, ps_ref):
    out = [_bdot(d, wp_ref[g]) for g, d in enumerate(d_groups)]
    return (jnp.concatenate(out, axis=-1) * ps_ref[...]).astype(BF16)


def _pool_prompt_kernel(u_ref, halo_ref, wp_ref, ps_ref, o_ref, ext):
    t = pl.program_id(1)
    tile = u_ref.shape[0]
    halo = halo_ref[...]
    ext[0:2 * SUBLANES, :] = jnp.where(t == 0, jnp.zeros_like(halo), halo)
    ext[2 * SUBLANES:, :] = u_ref[...]
    pos = t * tile + lax.broadcasted_iota(jnp.int32, (tile, 1), 0)
    base = 2 * SUBLANES
    ds = []
    for g, w in enumerate(POOL_WINDOWS):
        sl = slice(g * POOL_GROUP, (g + 1) * POOL_GROUP)
        acc = ext[base:base + tile, sl]
        for j in range(1, w):
            acc = acc + ext[base - j:base - j + tile, sl]
        cnt = jnp.minimum(pos + 1, w).astype(F32)
        ds.append(acc / cnt - ext[base:base + tile, sl])
    o_ref[...] = _pool_project(ds, wp_ref, ps_ref)


def _pool_prompt(u, w_pool, pool_scale, *, tile=512):
    b, s, c = u.shape
    hb = 2 * SUBLANES
    return pl.pallas_call(
        _pool_prompt_kernel,
        grid=(b, s // tile),
        in_specs=[pl.BlockSpec((None, tile, c), lambda i, t: (i, t, 0)),
                  pl.BlockSpec((None, hb, c), lambda i, t: (i, jnp.maximum(t * (tile // hb) - 1, 0), 0)),
                  pl.BlockSpec(w_pool.shape, lambda i, t: (0, 0, 0)),
                  pl.BlockSpec((1, c), lambda i, t: (0, 0))],
        out_specs=pl.BlockSpec((None, tile, c), lambda i, t: (i, t, 0)),
        out_shape=jax.ShapeDtypeStruct((b, s, c), BF16),
        scratch_shapes=[pltpu.VMEM((tile + hb, c), F32)],
        compiler_params=_cparams(("parallel", "parallel")),
        name="pool_prompt",
    )(u, u, w_pool, pool_scale.reshape(1, c))


def _pool_sample_kernel(st_ref, u_ref, wp_ref, ps_ref, o_ref, ns_ref):
    u = u_ref[...]
    ns_ref[0:POOL_BUF - 1] = st_ref[1:POOL_BUF]
    ns_ref[POOL_BUF - 1] = u
    ds = []
    for g, w in enumerate(POOL_WINDOWS):
        sl = slice(g * POOL_GROUP, (g + 1) * POOL_GROUP)
        acc = u[:, sl]
        for j in range(1, w):
            acc = acc + st_ref[POOL_BUF - j][:, sl]
        cnt = float(min(PAST_LEN + 1, w))
        ds.append(acc / cnt - u[:, sl])
    o_ref[...] = _pool_project(ds, wp_ref, ps_ref)


def _pool_sample(state_t, u, w_pool, pool_scale):
    nb, bd, c = state_t.shape
    full = lambda a: pl.BlockSpec(a.shape, lambda: (0,) * a.ndim)
    ps = pool_scale.reshape(1, c)
    return pl.pallas_call(
        _pool_sample_kernel,
        in_specs=[full(state_t), full(u), full(w_pool), full(ps)],
        out_specs=[pl.BlockSpec((bd, c), lambda: (0, 0)), full(state_t)],
        out_shape=[jax.ShapeDtypeStruct((bd, c), BF16), jax.ShapeDtypeStruct(state_t.shape, F32)],
        compiler_params=pltpu.CompilerParams(vmem_limit_bytes=VMEM_LIMIT),
        name="pool_sample",
    )(state_t, u, w_pool, ps)


def _route_kernel(x_ref, attn_ref, pool_ref, gm_ref, sh_ref, sc_ref, wo_ref, nf_ref, wr_ref, br_ref,
                  tri_t_ref, tri_e_ref,
                  x2_ref, hs_ref, slot_ref, cnt_ref):
    tile = x_ref.shape[0]
    mix = (jnp.dot(attn_ref[...], wo_ref[:ATTN_WIDTH, :], preferred_element_type=F32)
           + jnp.dot(pool_ref[...], wo_ref[ATTN_WIDTH:, :], preferred_element_type=F32))
    x2 = x_ref[...] + gm_ref[...] * mix
    x2_ref[...] = x2
    h = _rms(x2, nf_ref[...]) * (1.0 + sc_ref[...]) + sh_ref[...]

    hh, hl = _split(h)
    wh, wl = _split(wr_ref[...])
    nt = functools.partial(lax.dot_general, dimension_numbers=(((1,), (1,)), ((), ())),
                           preferred_element_type=F32)
    logits = nt(wh, hh) + nt(wl, hh) + nt(wh, hl) + br_ref[...]

    eidx = lax.broadcasted_iota(jnp.int32, (N_EXPERTS, tile), 0).astype(F32)
    work = logits
    tops, picks = [], []
    for _ in range(TOP_K):
        m = jnp.max(work, axis=0, keepdims=True)
        pick = jnp.min(jnp.where(work == m, eidx, float(N_EXPERTS)), axis=0, keepdims=True)
        work = jnp.where(eidx == pick, -jnp.inf, work)
        tops.append(m)
        picks.append(pick)
    ex = [jnp.exp(t - tops[0]) for t in tops]
    den = ex[0] + ex[1] + ex[2] + ex[3]
    gates = [e / den for e in ex]

    sel = jnp.zeros((N_EXPERTS, tile), F32)
    for pick in picks:
        sel = sel + (eidx == pick).astype(F32)
    rank = jnp.dot(sel.astype(BF16), tri_t_ref[...], preferred_element_type=F32)
    cnt = jnp.sum(sel, axis=1, keepdims=True)
    padded = jnp.ceil(cnt * (1.0 / CHUNK)) * CHUNK
    seg = jnp.dot(tri_e_ref[...], jnp.broadcast_to(padded, (N_EXPERTS, LANES)).astype(BF16),
                  preferred_element_type=F32)[:, :1]
    dest = seg + rank
    cnt_ref[...] = jnp.broadcast_to(cnt, (N_EXPERTS, LANES))
    slots = [jnp.sum(jnp.where(eidx == pick, dest, 0.0), axis=0, keepdims=True) for pick in picks]
    slot_ref[...] = jnp.concatenate(slots + slots, axis=0)

    ridx = lax.broadcasted_iota(jnp.int32, (TILE_ROWS, tile), 0).astype(F32)
    perm = jnp.zeros((TILE_ROWS, tile), F32)
    gmat = jnp.zeros((TILE_ROWS, tile), F32)
    for s, g in zip(slots, gates):
        hit = ridx == s
        perm = jnp.where(hit, 1.0, perm)
        gmat = jnp.where(hit, g, gmat)
    hs_ref[:, :PACKED_W] = _pack_rows(jnp.dot(perm.astype(BF16), hh, preferred_element_type=F32))
    gate_rows = jnp.broadcast_to(jnp.sum(gmat, axis=1, keepdims=True), (TILE_ROWS, LANES))
    hs_ref[:, PACKED_W:] = lax.bitcast_convert_type(gate_rows, jnp.int32)


def _route(x2d, attn, pool, gm, sh, sc, w_out, norm_ffn, w_router_t, b_router, hs_prev, *, tile, rows_per_mod,
           tile0, extra_tiles):
    n = x2d.shape[0]
    nt = n // tile
    steps = nt + extra_tiles
    mrows = gm.shape[1]
    last = lambda i: jnp.minimum(i, nt - 1)
    mod_spec = pl.BlockSpec((None, mrows, D_MODEL), lambda i: ((last(i) * tile) // rows_per_mod, 0, 0))
    const = lambda shape: pl.BlockSpec(shape, lambda i: (0,) * len(shape))
    row = lambda w: pl.BlockSpec((tile, w), lambda i: (last(i), 0))
    tri_t = (jnp.arange(tile)[:, None] < jnp.arange(tile)[None, :]).astype(BF16)
    tri_e = (jnp.arange(N_EXPERTS)[None, :] < jnp.arange(N_EXPERTS)[:, None]).astype(BF16)
    in_specs = [row(D_MODEL), row(ATTN_WIDTH), row(POOL_WIDTH), mod_spec, mod_spec, mod_spec,
                const((D_MODEL, D_MODEL)), const((1, D_MODEL)), const((N_EXPERTS, D_MODEL)),
                const((N_EXPERTS, 1)), const((tile, tile)), const((N_EXPERTS, N_EXPERTS))]
    args = [x2d, attn, pool, gm, sh, sc, w_out, norm_ffn.reshape(1, D_MODEL), w_router_t,
            b_router.reshape(N_EXPERTS, 1), tri_t, tri_e]
    n_in = len(args)
    kern = _route_kernel
    aliases = {}
    hs_rows = (tile0 + steps) * TILE_ROWS
    if hs_prev is not None:
        in_specs.append(pl.BlockSpec(memory_space=pl.ANY))
        args.append(hs_prev)
        aliases = {n_in: 1}
        kern = lambda *refs: _route_kernel(*refs[:n_in], *refs[n_in + 1:])
        hs_rows = hs_prev.shape[0]
    return pl.pallas_call(
        kern,
        grid=(steps,),
        in_specs=in_specs,
        out_specs=[pl.BlockSpec((tile, D_MODEL), lambda i: (last(i), 0)),
                   pl.BlockSpec((TILE_ROWS, ROW_W), lambda i: (i + tile0, 0)),
                   pl.BlockSpec((None, 2 * TOP_K, tile), lambda i: (last(i), 0, 0)),
                   pl.BlockSpec((None, N_EXPERTS, LANES), lambda i: (last(i), 0, 0))],
        out_shape=[jax.ShapeDtypeStruct((n, D_MODEL), F32),
                   jax.ShapeDtypeStruct((hs_rows, ROW_W), jnp.int32),
                   jax.ShapeDtypeStruct((nt, 2 * TOP_K, tile), F32),
                   jax.ShapeDtypeStruct((nt, N_EXPERTS, LANES), F32)],
        input_output_aliases=aliases,
        compiler_params=_cparams(("arbitrary",)),
        name="route",
    )(*args)


def _moe_kernel(src_ref, be_ref, nxt_ref, nb_ref, hs_hbm, w1_hbm, b1_ref, w2_hbm, b2_ref, out_hbm,
                lhs, obuf, w1s, w2s, w1c, w2c, sem_in, sem_out, sem_w, *, scratch_chunk):
    b = pl.program_id(0)
    nb = nb_ref[0]
    slot = b % 2

    def weight_copies(e):
        return (pltpu.make_async_copy(w1_hbm.at[e], w1s, sem_w.at[0]),
                pltpu.make_async_copy(w2_hbm.at[e], w2s, sem_w.at[1]))

    def chunk_rows(c):
        return pl.ds(pl.multiple_of(c * CHUNK, CHUNK), CHUNK)

    def start_in(blk, s):
        for j in range(BLOCK_CHUNKS):
            c = src_ref[blk * BLOCK_CHUNKS + j]
            c = jnp.where(c < 0, ZERO_CHUNK, c)
            pltpu.make_async_copy(hs_hbm.at[chunk_rows(c)], lhs.at[s, pl.ds(j * CHUNK, CHUNK)],
                                  sem_in.at[s]).start()

    def wait_in(s):
        pltpu.make_async_copy(hs_hbm.at[pl.ds(0, MOE_BLOCK)], lhs.at[s], sem_in.at[s]).wait()

    def start_out(blk, s):
        for j in range(BLOCK_CHUNKS):
            c = src_ref[blk * BLOCK_CHUNKS + j]
            c = jnp.where(c < 0, scratch_chunk + s * BLOCK_CHUNKS + j, c)
            pltpu.make_async_copy(obuf.at[s, pl.ds(j * CHUNK, CHUNK)],
                                  out_hbm.at[chunk_rows(c), pl.ds(0, PACKED_W)], sem_out.at[s]).start()

    def wait_out(s):
        pltpu.make_async_copy(obuf.at[s], out_hbm.at[pl.ds(0, MOE_BLOCK), pl.ds(0, PACKED_W)],
                              sem_out.at[s]).wait()

    @pl.when(b < nb)
    def _():
        @pl.when(b == 0)
        def _():
            start_in(0, 0)
            for cp in weight_copies(be_ref[0]):
                cp.start()

        @pl.when(jnp.logical_or(b == 0, be_ref[b] != be_ref[jnp.maximum(b - 1, 0)]))
        def _():
            for cp in weight_copies(be_ref[b]):
                cp.wait()
            w1c[...] = w1s[...].astype(BF16)
            w2c[...] = w2s[...].astype(BF16)
            nxt = nxt_ref[b]

            @pl.when(nxt >= 0)
            def _():
                for cp in weight_copies(nxt):
                    cp.start()

        wait_in(slot)

        @pl.when(b >= 2)
        def _():
            wait_out(slot)

        start_in(b + 1, 1 - slot)

        rows = lhs[slot]
        xh, xl = _unpack_rows(rows[:, :PACKED_W])
        gate_w = lax.bitcast_convert_type(rows[:, PACKED_W:PACKED_W + 1], F32)
        y = jnp.broadcast_to(b2_ref[...], (MOE_BLOCK, D_MODEL))
        for c in range(0, D_FF, FF_CHUNK):
            def w1_cols(lo):
                return (jnp.dot(xh, w1c[:PACKED_W, lo:lo + FF_CHUNK], preferred_element_type=F32)
                        + jnp.dot(xl, w1c[PACKED_W:, lo:lo + FF_CHUNK], preferred_element_type=F32)
                        + b1_ref[:, lo:lo + FF_CHUNK])
            gate = jnp.minimum(w1_cols(c), SWIGLU_LIMIT)
            up = jnp.clip(w1_cols(D_FF + c), -SWIGLU_LIMIT, SWIGLU_LIMIT)
            act = (up + 1.0) * (gate * jax.nn.sigmoid(SWIGLU_ALPHA * gate))
            y = y + jnp.dot(act.astype(BF16), w2c[c:c + FF_CHUNK, :], preferred_element_type=F32)
        obuf[slot] = _pack_rows((y * gate_w).astype(BF16).astype(F32))

        start_out(b, slot)

        @pl.when(b == nb - 1)
        def _():
            wait_in(1 - slot)

            @pl.when(b >= 1)
            def _():
                wait_out(1 - slot)
            wait_out(slot)


def _moe(hs, src, block_e, next_e, nblocks, w1, b1, w2, b2, n_blocks_max, scratch_chunk):
    bspec = lambda w: pl.BlockSpec((None, 1, w), lambda b, src, be, nxt, nb: (be[b], 0, 0))
    hbm = pl.BlockSpec(memory_space=pl.ANY)
    return pl.pallas_call(
        functools.partial(_moe_kernel, scratch_chunk=scratch_chunk),
        grid_spec=pltpu.PrefetchScalarGridSpec(
            num_scalar_prefetch=4,
            grid=(n_blocks_max,),
            in_specs=[hbm, hbm, bspec(2 * D_FF), hbm, bspec(D_MODEL)],
            out_specs=hbm,
            scratch_shapes=[pltpu.VMEM((2, MOE_BLOCK, ROW_W), jnp.int32),
                            pltpu.VMEM((2, MOE_BLOCK, PACKED_W), jnp.int32),
                            pltpu.VMEM((D_MODEL, 2 * D_FF), F32),
                            pltpu.VMEM((D_FF, D_MODEL), F32),
                            pltpu.VMEM((D_MODEL, 2 * D_FF), BF16),
                            pltpu.VMEM((D_FF, D_MODEL), BF16),
                            pltpu.SemaphoreType.DMA((2,)),
                            pltpu.SemaphoreType.DMA((2,)),
                            pltpu.SemaphoreType.DMA((2,))]),
        out_shape=jax.ShapeDtypeStruct(hs.shape, jnp.int32),
        input_output_aliases={4: 0},
        compiler_params=_cparams(("arbitrary",)),
        name="moe_experts",
    )(src, block_e, next_e, nblocks, hs, w1, b1.reshape(N_EXPERTS, 1, 2 * D_FF), w2,
      b2.reshape(N_EXPERTS, 1, D_MODEL))


def _plan(cnt, n_blocks_max):
    nt = cnt.shape[0]
    nch = (cnt + (CHUNK - 1)) // CHUNK
    lstart = jnp.cumsum(nch, axis=1) - nch
    ne = jnp.sum(nch, axis=0)
    nbe = (ne + (BLOCK_CHUNKS - 1)) // BLOCK_CHUNKS
    bend = jnp.cumsum(nbe)
    nblocks = bend[-1]
    gstart = (bend - nbe)[None, :] * BLOCK_CHUNKS + (jnp.cumsum(nch, axis=0) - nch)
    gs = gstart.T.reshape(-1)
    nc = nch.T.reshape(-1)
    s0 = (jnp.arange(nt, dtype=jnp.int32)[:, None] * TILE_CHUNKS + lstart).T.reshape(-1)
    c = jnp.arange((n_blocks_max + 1) * BLOCK_CHUNKS, dtype=jnp.int32)[:, None]
    inside = jnp.logical_and(c >= gs[None, :], c < (gs + nc)[None, :])
    src = jnp.sum(jnp.where(inside, (s0 - gs)[None, :] + c + 1, 0), axis=1) - 1
    blk = jnp.arange(n_blocks_max, dtype=jnp.int32)
    be = jnp.sum((blk[:, None] >= bend[None, :]).astype(jnp.int32), axis=1)
    be_last = jnp.sum((nblocks - 1 >= bend).astype(jnp.int32))
    be = jnp.minimum(be, be_last).astype(jnp.int32)
    eid = jnp.arange(N_EXPERTS, dtype=jnp.int32)
    later = jnp.logical_and(eid[None, :] > be[:, None], (nbe > 0)[None, :])
    nxt = jnp.min(jnp.where(later, eid[None, :], N_EXPERTS), axis=1)
    nxt = jnp.where(nxt == N_EXPERTS, -1, nxt).astype(jnp.int32)
    return src, be, nxt, nblocks.reshape(1).astype(jnp.int32)


def _combine_kernel(o_ref, slot_ref, x2_ref, gf_ref, y_ref):
    tile = x2_ref.shape[0]
    ridx = lax.broadcasted_iota(jnp.int32, (TILE_ROWS, tile), 0).astype(F32)
    hit = ridx == slot_ref[0:1, :]
    for k in range(1, TOP_K):
        hit = jnp.logical_or(hit, ridx == slot_ref[k:k + 1, :])
    perm = jnp.where(hit, 1.0, 0.0).astype(BF16)
    tn = functools.partial(lax.dot_general, dimension_numbers=(((0,), (0,)), ((), ())),
                           preferred_element_type=F32)
    oh, ol = _unpack_rows(o_ref[...])
    y = jnp.concatenate([tn(perm, oh), tn(perm, ol)], axis=1)
    y_ref[...] = x2_ref[...] + gf_ref[...] * y


def _combine(outs, slots, x2, gf, *, tile, rows_per_mod, tile0):
    n = x2.shape[0]
    mrows = gf.shape[1]
    return pl.pallas_call(
        _combine_kernel,
        grid=(n // tile,),
        in_specs=[pl.BlockSpec((TILE_ROWS, PACKED_W), lambda i: (i + tile0, 0)),
                  pl.BlockSpec((None, 2 * TOP_K, tile), lambda i: (i, 0, 0)),
                  pl.BlockSpec((tile, D_MODEL), lambda i: (i, 0)),
                  pl.BlockSpec((None, mrows, D_MODEL), lambda i: ((i * tile) // rows_per_mod, 0, 0))],
        out_specs=pl.BlockSpec((tile, D_MODEL), lambda i: (i, 0)),
        out_shape=jax.ShapeDtypeStruct((n, D_MODEL), F32),
        compiler_params=_cparams(("parallel",)),
        name="combine",
    )(outs, slots, x2, gf)


def kernel(x_prompt, x_sample, cache_k, cache_v, state_pool, c_prompt, c_sample, rel_bias, norm_mix, w_ada,
           b_ada, w_in, q_norm, k_norm, sinks, w_pool, pool_scale, w_out, norm_ffn, w_router, b_router,
           w1, b1, w2, b2):
    depth = w_in.shape[0]
    assert depth == 1
    l = 0
    bp, sp, _ = x_prompt.shape
    bs = x_sample.shape[0]
    assert x_sample.shape[1] == 1 and sp % SORT_TILE == 0 and bs <= SORT_TILE
    n_p = bp * sp
    tiles_p = n_p // SORT_TILE
    max_chunks = tiles_p * (SORT_TILE * TOP_K // CHUNK + N_EXPERTS) + (bs * TOP_K // CHUNK + N_EXPERTS)
    n_blocks_max = -(-max_chunks // BLOCK_CHUNKS) + N_EXPERTS

    mod = _modulation(jnp.concatenate([c_prompt, c_sample], axis=0), w_ada[l], b_ada[l])
    mod_p = [m.reshape(bp, 1, D_MODEL) for m in jnp.split(mod[:bp], 6, axis=-1)]
    mod_s = [m.reshape(1, bs, D_MODEL) for m in jnp.split(mod[bp:], 6, axis=-1)]

    head = jnp.arange(ATTN_WIDTH) // HEAD_DIM
    bd = (head[:, None] == head[None, :]).astype(BF16)
    w_in_b = w_in[l].astype(BF16)
    w_out_b = w_out[l].astype(BF16)
    w_pool_b = w_pool[l].astype(BF16)
    w_router_t = w_router[l].T

    xp = x_prompt.reshape(n_p, D_MODEL)
    q, k, v, u = _mixer_inputs(xp, mod_p[0], mod_p[1], norm_mix[l], w_in_b, bd, q_norm[l], k_norm[l],
                               tile=512, rows_per_mod=sp, precise=False)
    k3 = k.reshape(bp, sp, KV_WIDTH)
    v3 = v.reshape(bp, sp, KV_WIDTH)
    u3 = u.reshape(bp, sp, POOL_WIDTH)
    attn = _attn_prompt(q.reshape(bp, sp, ATTN_WIDTH), k3, v3, sinks[l], rel_bias)
    pool = _pool_prompt(u3, w_pool_b, pool_scale[l])
    keep = min(WINDOW, sp)
    nkp = k3[:, -keep:].reshape(bp, keep, N_KV_HEADS, HEAD_DIM)
    nvp = v3[:, -keep:].reshape(bp, keep, N_KV_HEADS, HEAD_DIM)
    if sp >= POOL_BUF:
        npp = u3[:, -POOL_BUF:]
    else:
        npp = jnp.concatenate([jnp.zeros((bp, POOL_BUF - sp, POOL_WIDTH), F32), u3], axis=1)
    x2_p, hs, slots_p, cnt_p = _route(
        xp, attn.reshape(n_p, ATTN_WIDTH), pool.reshape(n_p, POOL_WIDTH), mod_p[2], mod_p[3], mod_p[4],
        w_out_b, norm_ffn[l], w_router_t, b_router[l], None,
        tile=SORT_TILE, rows_per_mod=sp, tile0=0, extra_tiles=2)

    xs = x_sample.reshape(bs, D_MODEL)
    qs, ks, vs, us = _mixer_inputs(xs, mod_s[0], mod_s[1], norm_mix[l], w_in[l], bd, q_norm[l], k_norm[l],
                                   tile=bs, rows_per_mod=bs, precise=True)
    wbuf = cache_k.shape[2]
    attn_s, nks, nvs = _attn_sample(qs, ks, vs, cache_k[l].reshape(bs, wbuf, KV_WIDTH),
                                    cache_v[l].reshape(bs, wbuf, KV_WIDTH), sinks[l], rel_bias)
    pool_s, nps_t = _pool_sample(jnp.swapaxes(state_pool[l], 0, 1), us, w_pool_b, pool_scale[l])
    x2_s, hs, slots_s, cnt_s = _route(
        xs, attn_s.astype(BF16), pool_s, mod_s[2], mod_s[3], mod_s[4], w_out_b, norm_ffn[l],
        w_router_t, b_router[l], hs, tile=bs, rows_per_mod=bs, tile0=tiles_p, extra_tiles=0)

    cnt = jnp.concatenate([cnt_p[:, :, 0], cnt_s[:, :, 0]], axis=0).astype(jnp.int32)
    src, block_e, next_e, nblocks = _plan(cnt, n_blocks_max)
    outs = _moe(hs, src, block_e, next_e, nblocks, w1[l], b1[l], w2[l], b2[l], n_blocks_max,
                scratch_chunk=(tiles_p + 1) * TILE_CHUNKS)

    y_p = _combine(outs, slots_p, x2_p, mod_p[5], tile=SORT_TILE, rows_per_mod=sp, tile0=0)
    y_s = _combine(outs, slots_s, x2_s, mod_s[5], tile=bs, rows_per_mod=bs, tile0=tiles_p)

    return (y_p.reshape(bp, sp, D_MODEL), y_s.reshape(bs, 1, D_MODEL),
            nkp[None], nvp[None], npp[None],
            nks.reshape(1, bs, wbuf, N_KV_HEADS, HEAD_DIM), nvs.reshape(1, bs, wbuf, N_KV_HEADS, HEAD_DIM),
            jnp.swapaxes(nps_t, 0, 1)[None])
```

```python
import functools
import math

import jax
import jax.numpy as jnp
from jax import lax
from jax.experimental import pallas as pl
from jax.experimental.pallas import tpu as pltpu

F32 = jnp.float32
BF16 = jnp.bfloat16

D_MODEL = 1024
HEAD_DIM = 64
N_HEADS = 8
N_KV_HEADS = 2
GROUP = N_HEADS // N_KV_HEADS
ATTN_WIDTH = N_HEADS * HEAD_DIM
KV_WIDTH = N_KV_HEADS * HEAD_DIM
POOL_WIDTH = D_MODEL - ATTN_WIDTH
POOL_WINDOWS = (2, 4, 8, 16)
POOL_GROUP = POOL_WIDTH // len(POOL_WINDOWS)
POOL_BUF = max(POOL_WINDOWS) - 1
IN_WIDTH = ATTN_WIDTH + 2 * KV_WIDTH + POOL_WIDTH
WINDOW = 128
ATTN_BLOCK = 128
N_BUCKETS = 32
MAX_EXACT = 16
REL_MAX_DIST = 128
N_EXPERTS = 32
TOP_K = 4
D_FF = D_MODEL
SWIGLU_LIMIT = 7.0
SWIGLU_ALPHA = 1.702
EPS = 1e-6
NEG_INF = -1e30
PAST_LEN = 16384

LANES = 128
SUBLANES = 8
VMEM_LIMIT = 56 * 1024 * 1024

SORT_TILE = 256
CHUNK = SUBLANES
TILE_ROWS = -(-(SORT_TILE * TOP_K + N_EXPERTS * (CHUNK - 1)) // LANES) * LANES
TILE_CHUNKS = TILE_ROWS // CHUNK
MOE_BLOCK = 256
BLOCK_CHUNKS = MOE_BLOCK // CHUNK
FFN_ROWS = 256
PACKED_W = D_MODEL // 2
ROW_W = PACKED_W + LANES
ZERO_CHUNK = TILE_CHUNKS - 1


def _bdot(a, b):
    return jnp.dot(a.astype(BF16), b.astype(BF16), preferred_element_type=F32)


def _split(a):
    hi = a.astype(BF16)
    lo = (a - hi.astype(F32)).astype(BF16)
    return hi, lo


def _dot3(a, b):
    ah, al = _split(a)
    bh, bl = _split(b)
    d = functools.partial(jnp.dot, preferred_element_type=F32)
    return d(ah, bh) + d(al, bh) + d(ah, bl)


def _pack_rows(x):
    bits = lax.bitcast_convert_type(x, jnp.int32)
    return bits[:, :PACKED_W] | lax.shift_right_logical(bits[:, PACKED_W:], 16)


def _unpack_rows(w):
    hi = lax.bitcast_convert_type(w & jnp.int32(-65536), F32)
    lo = lax.bitcast_convert_type(lax.shift_left(w, 16), F32)
    return hi.astype(BF16), lo.astype(BF16)


def _rms(x, g):
    return x * lax.rsqrt(jnp.mean(x * x, axis=-1, keepdims=True) + EPS) * g


def _cparams(sem, **kw):
    return pltpu.CompilerParams(dimension_semantics=sem, vmem_limit_bytes=VMEM_LIMIT, **kw)


def _ada_kernel(c_ref, w_ref, b_ref, o_ref):
    c = c_ref[...]
    s = c * jax.nn.sigmoid(c)
    o_ref[...] = _dot3(s, w_ref[...]) + b_ref[...]


def _modulation(c, w_ada, b_ada):
    rows = c.shape[0]
    n = w_ada.shape[1]
    tn = 512
    return pl.pallas_call(
        _ada_kernel,
        grid=(n // tn,),
        in_specs=[pl.BlockSpec((rows, D_MODEL), lambda j: (0, 0)),
                  pl.BlockSpec((D_MODEL, tn), lambda j: (0, j)),
                  pl.BlockSpec((1, tn), lambda j: (0, j))],
        out_specs=pl.BlockSpec((rows, tn), lambda j: (0, j)),
        out_shape=jax.ShapeDtypeStruct((rows, n), F32),
        compiler_params=_cparams(("parallel",)),
        name="modulation",
    )(c, w_ada, b_ada.reshape(1, n))


def _head_rms(t, bd, g):
    hi, lo = _split(t * t)
    ss = jnp.dot(hi, bd, preferred_element_type=F32) + jnp.dot(lo, bd, preferred_element_type=F32)
    return t * lax.rsqrt(ss * (1.0 / HEAD_DIM) + EPS) * g


def _mixin_kernel(x_ref, sh_ref, sc_ref, g_ref, w_ref, bd_ref, qn_ref, kn_ref,
                  q_ref, k_ref, v_ref, u_ref, *, precise):
    h = _rms(x_ref[...], g_ref[...]) * (1.0 + sc_ref[...]) + sh_ref[...]
    z = _dot3(h, w_ref[...]) if precise else _bdot(h, w_ref[...])
    q = z[:, :ATTN_WIDTH]
    k = z[:, ATTN_WIDTH:ATTN_WIDTH + KV_WIDTH]
    bd = bd_ref[...]
    q = _head_rms(q, bd, qn_ref[...])
    k = _head_rms(k, bd[:KV_WIDTH, :KV_WIDTH], kn_ref[...])
    q_ref[...] = (q * (HEAD_DIM ** -0.5)).astype(BF16)
    k_ref[...] = k
    v_ref[...] = z[:, ATTN_WIDTH + KV_WIDTH:ATTN_WIDTH + 2 * KV_WIDTH]
    u_ref[...] = z[:, ATTN_WIDTH + 2 * KV_WIDTH:]


def _mixer_inputs(x2d, shift, scale, norm_mix, w_in, bd, q_norm, k_norm, *, tile, rows_per_mod, precise):
    n = x2d.shape[0]
    mrows = shift.shape[1]
    mod_spec = pl.BlockSpec((None, mrows, D_MODEL), lambda i: ((i * tile) // rows_per_mod, 0, 0))
    const = lambda shape: pl.BlockSpec(shape, lambda i: (0,) * len(shape))
    row = lambda w: pl.BlockSpec((tile, w), lambda i: (i, 0))
    return pl.pallas_call(
        functools.partial(_mixin_kernel, precise=precise),
        grid=(n // tile,),
        in_specs=[row(D_MODEL), mod_spec, mod_spec, const((1, D_MODEL)), const((D_MODEL, IN_WIDTH)),
                  const((ATTN_WIDTH, ATTN_WIDTH)), const((1, ATTN_WIDTH)), const((1, KV_WIDTH))],
        out_specs=[row(ATTN_WIDTH), row(KV_WIDTH), row(KV_WIDTH), row(POOL_WIDTH)],
        out_shape=[jax.ShapeDtypeStruct((n, ATTN_WIDTH), BF16),
                   jax.ShapeDtypeStruct((n, KV_WIDTH), F32),
                   jax.ShapeDtypeStruct((n, KV_WIDTH), F32),
                   jax.ShapeDtypeStruct((n, POOL_WIDTH), F32)],
        compiler_params=_cparams(("parallel",)),
        name="mixer_inputs",
    )(x2d, shift, scale, norm_mix.reshape(1, D_MODEL), w_in, bd,
      jnp.tile(q_norm, N_HEADS).reshape(1, ATTN_WIDTH), jnp.tile(k_norm, N_KV_HEADS).reshape(1, KV_WIDTH))


def _t5_bucket(rel):
    n = jnp.maximum(rel, 0)
    nf = jnp.maximum(n, 1).astype(F32)
    large = MAX_EXACT + (jnp.log(nf / MAX_EXACT) / math.log(REL_MAX_DIST / MAX_EXACT)
                         * (N_BUCKETS - MAX_EXACT)).astype(jnp.int32)
    large = jnp.minimum(large, N_BUCKETS - 1)
    return jnp.where(n < MAX_EXACT, n, large)


def _bias_table(rel, rel_table):
    bucket = _t5_bucket(rel)
    table = rel_table.astype(F32)
    ids = jnp.arange(N_BUCKETS, dtype=bucket.dtype).reshape((N_BUCKETS, 1) + (1,) * rel.ndim)
    onehot = bucket[None, None] == ids
    bias = jnp.sum(jnp.where(onehot, table.reshape(table.shape + (1,) * rel.ndim), 0.0), axis=0)
    valid = (rel >= 0) & (rel < WINDOW)
    return jnp.where(valid[None], bias, NEG_INF)


def _attn_prompt_kernel(sink_ref, q_ref, kp_ref, kc_ref, vp_ref, vc_ref, bias_ref, o_ref):
    first = pl.program_id(1) == 0
    q = q_ref[...]
    kk = jnp.concatenate([kp_ref[...], kc_ref[...]], axis=0).astype(BF16)
    vv = jnp.concatenate([vp_ref[...], vc_ref[...]], axis=0).astype(BF16)
    col = lax.broadcasted_iota(jnp.int32, (GROUP * ATTN_BLOCK, 2 * ATTN_BLOCK), 1)
    row = lax.broadcasted_iota(jnp.int32, (GROUP * ATTN_BLOCK, 1), 0)
    no_prev = jnp.logical_and(first, col < ATTN_BLOCK)
    outs = []
    for kv in range(N_KV_HEADS):
        heads = range(kv * GROUP, (kv + 1) * GROUP)
        qg = jnp.concatenate([q[:, h * HEAD_DIM:(h + 1) * HEAD_DIM] for h in heads], axis=0)
        kh = kk[:, kv * HEAD_DIM:(kv + 1) * HEAD_DIM]
        vh = vv[:, kv * HEAD_DIM:(kv + 1) * HEAD_DIM]
        s = lax.dot_general(qg, kh, (((1,), (1,)), ((), ())), preferred_element_type=F32)
        s = s + bias_ref[kv]
        s = jnp.where(no_prev, NEG_INF, s)
        sink = jnp.zeros((GROUP * ATTN_BLOCK, 1), F32)
        for g, h in enumerate(heads):
            sink = jnp.where(row // ATTN_BLOCK == g, sink_ref[h], sink)
        m = jnp.maximum(jnp.max(s, axis=-1, keepdims=True), sink)
        p = jnp.exp(s - m)
        denom = jnp.sum(p, axis=-1, keepdims=True) + jnp.exp(sink - m)
        o = jnp.dot(p.astype(BF16), vh, preferred_element_type=F32) / denom
        outs += [o[g * ATTN_BLOCK:(g + 1) * ATTN_BLOCK] for g in range(GROUP)]
    o_ref[...] = jnp.concatenate(outs, axis=-1).astype(BF16)


def _attn_prompt(q, k, v, sinks, rel_table):
    b, s = q.shape[:2]
    nblk = s // ATTN_BLOCK
    qi = jnp.arange(ATTN_BLOCK, dtype=jnp.int32)[:, None]
    si = jnp.arange(2 * ATTN_BLOCK, dtype=jnp.int32)[None, :]
    bias = _bias_table(qi + ATTN_BLOCK - si, rel_table)
    bias = bias.reshape(N_KV_HEADS, GROUP * ATTN_BLOCK, 2 * ATTN_BLOCK)
    cur = lambda w: pl.BlockSpec((None, ATTN_BLOCK, w), lambda i, j, *_: (i, j, 0))
    prev = lambda w: pl.BlockSpec((None, ATTN_BLOCK, w), lambda i, j, *_: (i, jnp.maximum(j - 1, 0), 0))
    return pl.pallas_call(
        _attn_prompt_kernel,
        grid_spec=pltpu.PrefetchScalarGridSpec(
            num_scalar_prefetch=1,
            grid=(b, nblk),
            in_specs=[cur(ATTN_WIDTH), prev(KV_WIDTH), cur(KV_WIDTH), prev(KV_WIDTH), cur(KV_WIDTH),
                      pl.BlockSpec(bias.shape, lambda i, j, *_: (0, 0, 0))],
            out_specs=cur(ATTN_WIDTH)),
        out_shape=jax.ShapeDtypeStruct((b, s, ATTN_WIDTH), BF16),
        compiler_params=_cparams(("parallel", "parallel")),
        name="attn_prompt",
    )(sinks.astype(F32), q, k, k, v, v, bias)


def _attn_sample_kernel(sink_ref, q_ref, kc_ref, vc_ref, kn_ref, vn_ref, bias_ref, bnew_ref,
                        o_ref, nk_ref, nv_ref):
    kc = kc_ref[...]
    vc = vc_ref[...]
    kn = kn_ref[...]
    vn = vn_ref[...]
    w = kc.shape[1]
    pos = lax.broadcasted_iota(jnp.int32, kc.shape, 1)
    nk_ref[...] = jnp.where(pos == w - 1, kn[:, None, :], pltpu.roll(kc, w - 1, 1))
    nv_ref[...] = jnp.where(pos == w - 1, vn[:, None, :], pltpu.roll(vc, w - 1, 1))
    gi = lax.broadcasted_iota(jnp.int32, (1, GROUP, 1), 1)
    for kv in range(N_KV_HEADS):
        sl = slice(kv * HEAD_DIM, (kv + 1) * HEAD_DIM)
        qg = q_ref[:, kv]
        s = jnp.einsum('bgd,bsd->bgs', qg, kc[:, :, sl].astype(BF16), preferred_element_type=F32)
        s = s + bias_ref[kv][None]
        s_new = jnp.sum(qg.astype(F32) * kn[:, None, sl], axis=-1, keepdims=True) + bnew_ref[kv][None]
        sink = jnp.zeros((1, GROUP, 1), F32)
        for g in range(GROUP):
            sink = jnp.where(gi == g, sink_ref[kv * GROUP + g], sink)
        m = jnp.maximum(jnp.maximum(jnp.max(s, axis=-1, keepdims=True), s_new), sink)
        p = jnp.exp(s - m)
        p_new = jnp.exp(s_new - m)
        denom = jnp.sum(p, axis=-1, keepdims=True) + p_new + jnp.exp(sink - m)
        o = jnp.einsum('bgs,bsd->bgd', p.astype(BF16), vc[:, :, sl].astype(BF16), preferred_element_type=F32)
        o = o + p_new * vn[:, None, sl]
        o_ref[:, kv] = o / denom


def _attn_sample(q, k_new, v_new, cache_k, cache_v, sinks, rel_table, *, tile=16):
    bd, w = cache_k.shape[:2]
    rel = w - jnp.arange(w, dtype=jnp.int32)
    bias = _bias_table(rel, rel_table).reshape(N_KV_HEADS, GROUP, w)
    bnew = _bias_table(jnp.zeros((1,), jnp.int32), rel_table).reshape(N_KV_HEADS, GROUP, 1)
    q4 = q.reshape(bd, N_KV_HEADS, GROUP, HEAD_DIM)
    spec4 = pl.BlockSpec((tile, N_KV_HEADS, GROUP, HEAD_DIM), lambda i, *_: (i, 0, 0, 0))
    cache = pl.BlockSpec((tile, w, KV_WIDTH), lambda i, *_: (i, 0, 0))
    new = pl.BlockSpec((tile, KV_WIDTH), lambda i, *_: (i, 0))
    const3 = lambda a: pl.BlockSpec(a.shape, lambda i, *_: (0, 0, 0))
    o, nk, nv = pl.pallas_call(
        _attn_sample_kernel,
        grid_spec=pltpu.PrefetchScalarGridSpec(
            num_scalar_prefetch=1,
            grid=(bd // tile,),
            in_specs=[spec4, cache, cache, new, new, const3(bias), const3(bnew)],
            out_specs=[spec4, cache, cache]),
        out_shape=[jax.ShapeDtypeStruct(q4.shape, F32),
                   jax.ShapeDtypeStruct(cache_k.shape, F32),
                   jax.ShapeDtypeStruct(cache_v.shape, F32)],
        compiler_params=_cparams(("parallel",)),
        name="attn_sample",
    )(sinks.astype(F32), q4, cache_k, cache_v, k_new, v_new, bias, bnew)
    return o.reshape(bd, ATTN_WIDTH), nk, nv


def _pool_project(d_groups, wp_ref, ps_ref):
    out = [_bdot(d, wp_ref[g]) for g, d in enumerate(d_groups)]
    return (jnp.concatenate(out, axis=-1) * ps_ref[...]).astype(BF16)


def _pool_prompt_kernel(u_ref, halo_ref, wp_ref, ps_ref, o_ref, ext):
    t = pl.program_id(1)
    tile = u_ref.shape[0]
    halo = halo_ref[...]
    ext[0:2 * SUBLANES, :] = jnp.where(t == 0, jnp.zeros_like(halo), halo)
    ext[2 * SUBLANES:, :] = u_ref[...]
    pos = t * tile + lax.broadcasted_iota(jnp.int32, (tile, 1), 0)
    base = 2 * SUBLANES
    ds = []
    for g, w in enumerate(POOL_WINDOWS):
        sl = slice(g * POOL_GROUP, (g + 1) * POOL_GROUP)
        acc = ext[base:base + tile, sl]
        for j in range(1, w):
            acc = acc + ext[base - j:base - j + tile, sl]
        cnt = jnp.minimum(pos + 1, w).astype(F32)
        ds.append(acc / cnt - ext[base:base + tile, sl])
    o_ref[...] = _pool_project(ds, wp_ref, ps_ref)


def _pool_prompt(u, w_pool, pool_scale, *, tile=512):
    b, s, c = u.shape
    hb = 2 * SUBLANES
    return pl.pallas_call(
        _pool_prompt_kernel,
        grid=(b, s // tile),
        in_specs=[pl.BlockSpec((None, tile, c), lambda i, t: (i, t, 0)),
                  pl.BlockSpec((None, hb, c), lambda i, t: (i, jnp.maximum(t * (tile // hb) - 1, 0), 0)),
                  pl.BlockSpec(w_pool.shape, lambda i, t: (0, 0, 0)),
                  pl.BlockSpec((1, c), lambda i, t: (0, 0))],
        out_specs=pl.BlockSpec((None, tile, c), lambda i, t: (i, t, 0)),
        out_shape=jax.ShapeDtypeStruct((b, s, c), BF16),
        scratch_shapes=[pltpu.VMEM((tile + hb, c), F32)],
        compiler_params=_cparams(("parallel", "parallel")),
        name="pool_prompt",
    )(u, u, w_pool, pool_scale.reshape(1, c))


def _pool_sample_kernel(st_ref, u_ref, wp_ref, ps_ref, o_ref, ns_ref):
    u = u_ref[...]
    ns_ref[0:POOL_BUF - 1] = st_ref[1:POOL_BUF]
    ns_ref[POOL_BUF - 1] = u
    ds = []
    for g, w in enumerate(POOL_WINDOWS):
        sl = slice(g * POOL_GROUP, (g + 1) * POOL_GROUP)
        acc = u[:, sl]
        for j in range(1, w):
            acc = acc + st_ref[POOL_BUF - j][:, sl]
        cnt = float(min(PAST_LEN + 1, w))
        ds.append(acc / cnt - u[:, sl])
    o_ref[...] = _pool_project(ds, wp_ref, ps_ref)


def _pool_sample(state_t, u, w_pool, pool_scale):
    nb, bd, c = state_t.shape
    full = lambda a: pl.BlockSpec(a.shape, lambda: (0,) * a.ndim)
    ps = pool_scale.reshape(1, c)
    return pl.pallas_call(
        _pool_sample_kernel,
        in_specs=[full(state_t), full(u), full(w_pool), full(ps)],
        out_specs=[pl.BlockSpec((bd, c), lambda: (0, 0)), full(state_t)],
        out_shape=[jax.ShapeDtypeStruct((bd, c), BF16), jax.ShapeDtypeStruct(state_t.shape, F32)],
        compiler_params=pltpu.CompilerParams(vmem_limit_bytes=VMEM_LIMIT),
        name="pool_sample",
    )(state_t, u, w_pool, ps)


def _route_kernel(x_ref, attn_ref, pool_ref, gm_ref, sh_ref, sc_ref, wo_ref, nf_ref, wr_ref, br_ref,
                  tri_t_ref, tri_e_ref,
                  x2_ref, hs_ref, slot_ref, cnt_ref):
    tile = x_ref.shape[0]
    mix = (jnp.dot(attn_ref[...], wo_ref[:ATTN_WIDTH, :], preferred_element_type=F32)
           + jnp.dot(pool_ref[...], wo_ref[ATTN_WIDTH:, :], preferred_element_type=F32))
    x2 = x_ref[...] + gm_ref[...] * mix
    x2_ref[...] = x2
    h = _rms(x2, nf_ref[...]) * (1.0 + sc_ref[...]) + sh_ref[...]

    hh, hl = _split(h)
    wh, wl = _split(wr_ref[...])
    nt = functools.partial(lax.dot_general, dimension_numbers=(((1,), (1,)), ((), ())),
                           preferred_element_type=F32)
    logits = nt(wh, hh) + nt(wl, hh) + nt(wh, hl) + br_ref[...]

    eidx = lax.broadcasted_iota(jnp.int32, (N_EXPERTS, tile), 0).astype(F32)
    work = logits
    tops, picks = [], []
    for _ in range(TOP_K):
        m = jnp.max(work, axis=0, keepdims=True)
        pick = jnp.min(jnp.where(work == m, eidx, float(N_EXPERTS)), axis=0, keepdims=True)
        work = jnp.where(eidx == pick, -jnp.inf, work)
        tops.append(m)
        picks.append(pick)
    ex = [jnp.exp(t - tops[0]) for t in tops]
    den = ex[0] + ex[1] + ex[2] + ex[3]
    gates = [e / den for e in ex]

    sel = jnp.zeros((N_EXPERTS, tile), F32)
    for pick in picks:
        sel = sel + (eidx == pick).astype(F32)
    rank = jnp.dot(sel.astype(BF16), tri_t_ref[...], preferred_element_type=F32)
    cnt = jnp.sum(sel, axis=1, keepdims=True)
    padded = jnp.ceil(cnt * (1.0 / CHUNK)) * CHUNK
    seg = jnp.dot(tri_e_ref[...], jnp.broadcast_to(padded, (N_EXPERTS, LANES)).astype(BF16),
                  preferred_element_type=F32)[:, :1]
    dest = seg + rank
    cnt_ref[...] = jnp.broadcast_to(cnt, (N_EXPERTS, LANES))
    slots = [jnp.sum(jnp.where(eidx == pick, dest, 0.0), axis=0, keepdims=True) for pick in picks]
    slot_ref[...] = jnp.concatenate(slots + slots, axis=0)

    ridx = lax.broadcasted_iota(jnp.int32, (TILE_ROWS, tile), 0).astype(F32)
    perm = jnp.zeros((TILE_ROWS, tile), F32)
    gmat = jnp.zeros((TILE_ROWS, tile), F32)
    for s, g in zip(slots, gates):
        hit = ridx == s
        perm = jnp.where(hit, 1.0, perm)
        gmat = jnp.where(hit, g, gmat)
    hs_ref[:, :PACKED_W] = _pack_rows(jnp.dot(perm.astype(BF16), hh, preferred_element_type=F32))
    gate_rows = jnp.broadcast_to(jnp.sum(gmat, axis=1, keepdims=True), (TILE_ROWS, LANES))
    hs_ref[:, PACKED_W:] = lax.bitcast_convert_type(gate_rows, jnp.int32)


def _route(x2d, attn, pool, gm, sh, sc, w_out, norm_ffn, w_router_t, b_router, hs_prev, *, tile, rows_per_mod,
           tile0, extra_tiles):
    n = x2d.shape[0]
    nt = n // tile
    steps = nt + extra_tiles
    mrows = gm.shape[1]
    last = lambda i: jnp.minimum(i, nt - 1)
    mod_spec = pl.BlockSpec((None, mrows, D_MODEL), lambda i: ((last(i) * tile) // rows_per_mod, 0, 0))
    const = lambda shape: pl.BlockSpec(shape, lambda i: (0,) * len(shape))
    row = lambda w: pl.BlockSpec((tile, w), lambda i: (last(i), 0))
    tri_t = (jnp.arange(tile)[:, None] < jnp.arange(tile)[None, :]).astype(BF16)
    tri_e = (jnp.arange(N_EXPERTS)[None, :] < jnp.arange(N_EXPERTS)[:, None]).astype(BF16)
    in_specs = [row(D_MODEL), row(ATTN_WIDTH), row(POOL_WIDTH), mod_spec, mod_spec, mod_spec,
                const((D_MODEL, D_MODEL)), const((1, D_MODEL)), const((N_EXPERTS, D_MODEL)),
                const((N_EXPERTS, 1)), const((tile, tile)), const((N_EXPERTS, N_EXPERTS))]
    args = [x2d, attn, pool, gm, sh, sc, w_out, norm_ffn.reshape(1, D_MODEL), w_router_t,
            b_router.reshape(N_EXPERTS, 1), tri_t, tri_e]
    n_in = len(args)
    kern = _route_kernel
    aliases = {}
    hs_rows = (tile0 + steps) * TILE_ROWS
    if hs_prev is not None:
        in_specs.append(pl.BlockSpec(memory_space=pl.ANY))
        args.append(hs_prev)
        aliases = {n_in: 1}
        kern = lambda *refs: _route_kernel(*refs[:n_in], *refs[n_in + 1:])
        hs_rows = hs_prev.shape[0]
    return pl.pallas_call(
        kern,
        grid=(steps,),
        in_specs=in_specs,
        out_specs=[pl.BlockSpec((tile, D_MODEL), lambda i: (last(i), 0)),
                   pl.BlockSpec((TILE_ROWS, ROW_W), lambda i: (i + tile0, 0)),
                   pl.BlockSpec((None, 2 * TOP_K, tile), lambda i: (last(i), 0, 0)),
                   pl.BlockSpec((None, N_EXPERTS, LANES), lambda i: (last(i), 0, 0))],
        out_shape=[jax.ShapeDtypeStruct((n, D_MODEL), F32),
                   jax.ShapeDtypeStruct((hs_rows, ROW_W), jnp.int32),
                   jax.ShapeDtypeStruct((nt, 2 * TOP_K, tile), F32),
                   jax.ShapeDtypeStruct((nt, N_EXPERTS, LANES), F32)],
        input_output_aliases=aliases,
        compiler_params=_cparams(("arbitrary",)),
        name="route",
    )(*args)


def _moe_kernel(src_ref, be_ref, nxt_ref, nb_ref, hs_hbm, w1_hbm, b1_ref, w2_hbm, b2_ref, out_hbm,
                lhs0, lhs1, xb0, xb1, gw0, gw1, yb0, yb1, ob0, ob1, w1s, w2s, w1c, w2c,
                sem_in, sem_out, sem_w, *, scratch_chunk):
    p = pl.program_id(0)
    nb = nb_ref[0]
    lhs, xb, gw, yb, ob = (lhs0, lhs1), (xb0, xb1), (gw0, gw1), (yb0, yb1), (ob0, ob1)

    def weight_copies(e):
        return (pltpu.make_async_copy(w1_hbm.at[e], w1s, sem_w.at[0]),
                pltpu.make_async_copy(w2_hbm.at[e], w2s, sem_w.at[1]))

    def chunk_rows(c):
        return pl.ds(pl.multiple_of(c * CHUNK, CHUNK), CHUNK)

    def start_in(blk, a):
        for j in range(BLOCK_CHUNKS):
            c = src_ref[blk * BLOCK_CHUNKS + j]
            c = jnp.where(c < 0, ZERO_CHUNK, c)
            pltpu.make_async_copy(hs_hbm.at[chunk_rows(c)], lhs[a].at[pl.ds(j * CHUNK, CHUNK)],
                                  sem_in.at[a]).start()

    def wait_in(a):
        pltpu.make_async_copy(hs_hbm.at[pl.ds(0, MOE_BLOCK)], lhs[a], sem_in.at[a]).wait()

    def start_out(blk, a, real):
        for j in range(BLOCK_CHUNKS):
            c = src_ref[blk * BLOCK_CHUNKS + j]
            c = jnp.where(jnp.logical_and(real, c >= 0), c, scratch_chunk + a * BLOCK_CHUNKS + j)
            pltpu.make_async_copy(ob[a].at[pl.ds(j * CHUNK, CHUNK)],
                                  out_hbm.at[chunk_rows(c), pl.ds(0, PACKED_W)], sem_out.at[a]).start()

    def wait_out(a):
        pltpu.make_async_copy(ob[a], out_hbm.at[pl.ds(0, MOE_BLOCK), pl.ds(0, PACKED_W)], sem_out.at[a]).wait()

    def unpack(a):
        rows = lhs[a][...]
        xh, xl = _unpack_rows(rows[:, :PACKED_W])
        xb[a][:, :PACKED_W] = xh
        xb[a][:, PACKED_W:] = xl
        gw[a][...] = lax.bitcast_convert_type(rows[:, PACKED_W:], F32)

    def ffn(a, e):
        for r in range(0, MOE_BLOCK, FFN_ROWS):
            rs = pl.ds(r, FFN_ROWS)
            gu = jnp.dot(xb[a][rs, :], w1c[...], preferred_element_type=F32) + b1_ref[e]
            gate = jnp.minimum(gu[:, :D_FF], SWIGLU_LIMIT)
            up = jnp.clip(gu[:, D_FF:], -SWIGLU_LIMIT, SWIGLU_LIMIT)
            act = ((up + 1.0) * (gate * jax.nn.sigmoid(SWIGLU_ALPHA * gate))).astype(BF16)
            y = jnp.dot(act, w2c[...], preferred_element_type=F32) + b2_ref[e]
            yb[a][rs, :] = y * gw[a][rs, :1]

    def emit(a):
        ob[a][...] = _pack_rows(yb[a][...].astype(BF16).astype(F32))

    def group(k, a):
        o = 1 - a

        @pl.when(k <= nb)
        def _():
            e = be_ref[k]

            @pl.when(jnp.logical_or(k == 0, e != be_ref[jnp.maximum(k - 1, 0)]))
            def _():
                for cp in weight_copies(e):
                    cp.wait()
                w1c[...] = w1s[...].astype(BF16)
                w2c[...] = w2s[...].astype(BF16)
                nxt = nxt_ref[k]

                @pl.when(nxt >= 0)
                def _():
                    for cp in weight_copies(nxt):
                        cp.start()

            wait_in(o)

            @pl.when(k >= 2)
            def _():
                wait_out(o)

            start_in(k + 2, a)
            ffn(a, e)
            emit(o)
            unpack(o)
            start_out(jnp.maximum(k - 1, 0), o, k >= 1)

            @pl.when(k == nb)
            def _():
                wait_in(a)
                wait_out(o)

                @pl.when(k >= 1)
                def _():
                    wait_out(a)

    @pl.when(p == 0)
    def _():
        for cp in weight_copies(be_ref[0]):
            cp.start()
        start_in(0, 0)
        start_in(1, 1)
        wait_in(0)
        unpack(0)
        yb1[...] = jnp.zeros_like(yb1)

    group(2 * p, 0)
    group(2 * p + 1, 1)


def _moe(hs, src, block_e, next_e, nblocks, w1, b1, w2, b2, n_steps, scratch_chunk):
    full = lambda shape: pl.BlockSpec(shape, lambda p, *_: (0,) * len(shape))
    hbm = pl.BlockSpec(memory_space=pl.ANY)
    per_parity = lambda shape, dtype: [pltpu.VMEM(shape, dtype)] * 2
    return pl.pallas_call(
        functools.partial(_moe_kernel, scratch_chunk=scratch_chunk),
        grid_spec=pltpu.PrefetchScalarGridSpec(
            num_scalar_prefetch=4,
            grid=(n_steps,),
            in_specs=[hbm, hbm, full((N_EXPERTS, 1, 2 * D_FF)), hbm, full((N_EXPERTS, 1, D_MODEL))],
            out_specs=hbm,
            scratch_shapes=(per_parity((MOE_BLOCK, ROW_W), jnp.int32)
                            + per_parity((MOE_BLOCK, D_MODEL), BF16)
                            + per_parity((MOE_BLOCK, LANES), F32)
                            + per_parity((MOE_BLOCK, D_MODEL), F32)
                            + per_parity((MOE_BLOCK, PACKED_W), jnp.int32)
                            + [pltpu.VMEM((D_MODEL, 2 * D_FF), F32),
                               pltpu.VMEM((D_FF, D_MODEL), F32),
                               pltpu.VMEM((D_MODEL, 2 * D_FF), BF16),
                               pltpu.VMEM((D_FF, D_MODEL), BF16),
                               pltpu.SemaphoreType.DMA((2,)),
                               pltpu.SemaphoreType.DMA((2,)),
                               pltpu.SemaphoreType.DMA((2,))])),
        out_shape=jax.ShapeDtypeStruct(hs.shape, jnp.int32),
        input_output_aliases={4: 0},
        compiler_params=_cparams(("arbitrary",)),
        name="moe_experts",
    )(src, block_e, next_e, nblocks, hs, w1, b1.reshape(N_EXPERTS, 1, 2 * D_FF), w2,
      b2.reshape(N_EXPERTS, 1, D_MODEL))


def _plan(cnt, n_steps):
    n_blocks_max = 2 * n_steps
    nt = cnt.shape[0]
    nch = (cnt + (CHUNK - 1)) // CHUNK
    lstart = jnp.cumsum(nch, axis=1) - nch
    ne = jnp.sum(nch, axis=0)
    nbe = (ne + (BLOCK_CHUNKS - 1)) // BLOCK_CHUNKS
    bend = jnp.cumsum(nbe)
    nblocks = bend[-1]
    gstart = (bend - nbe)[None, :] * BLOCK_CHUNKS + (jnp.cumsum(nch, axis=0) - nch)
    gs = gstart.T.reshape(-1)
    nc = nch.T.reshape(-1)
    s0 = (jnp.arange(nt, dtype=jnp.int32)[:, None] * TILE_CHUNKS + lstart).T.reshape(-1)
    c = jnp.arange((n_blocks_max + 2) * BLOCK_CHUNKS, dtype=jnp.int32)[:, None]
    inside = jnp.logical_and(c >= gs[None, :], c < (gs + nc)[None, :])
    src = jnp.sum(jnp.where(inside, (s0 - gs)[None, :] + c + 1, 0), axis=1) - 1
    blk = jnp.arange(n_blocks_max, dtype=jnp.int32)
    be = jnp.sum((blk[:, None] >= bend[None, :]).astype(jnp.int32), axis=1)
    be_last = jnp.sum((nblocks - 1 >= bend).astype(jnp.int32))
    be = jnp.minimum(be, be_last).astype(jnp.int32)
    eid = jnp.arange(N_EXPERTS, dtype=jnp.int32)
    later = jnp.logical_and(eid[None, :] > be[:, None], (nbe > 0)[None, :])
    nxt = jnp.min(jnp.where(later, eid[None, :], N_EXPERTS), axis=1)
    nxt = jnp.where(nxt == N_EXPERTS, -1, nxt).astype(jnp.int32)
    return src, be, nxt, nblocks.reshape(1).astype(jnp.int32)


def _combine_kernel(o_ref, slot_ref, x2_ref, gf_ref, y_ref):
    tile = x2_ref.shape[0]
    ridx = lax.broadcasted_iota(jnp.int32, (TILE_ROWS, tile), 0).astype(F32)
    hit = ridx == slot_ref[0:1, :]
    for k in range(1, TOP_K):
        hit = jnp.logical_or(hit, ridx == slot_ref[k:k + 1, :])
    perm = jnp.where(hit, 1.0, 0.0).astype(BF16)
    tn = functools.partial(lax.dot_general, dimension_numbers=(((0,), (0,)), ((), ())),
                           preferred_element_type=F32)
    oh, ol = _unpack_rows(o_ref[...])
    y = jnp.concatenate([tn(perm, oh), tn(perm, ol)], axis=1)
    y_ref[...] = x2_ref[...] + gf_ref[...] * y


def _combine(outs, slots, x2, gf, *, tile, rows_per_mod, tile0):
    n = x2.shape[0]
    mrows = gf.shape[1]
    return pl.pallas_call(
        _combine_kernel,
        grid=(n // tile,),
        in_specs=[pl.BlockSpec((TILE_ROWS, PACKED_W), lambda i: (i + tile0, 0)),
                  pl.BlockSpec((None, 2 * TOP_K, tile), lambda i: (i, 0, 0)),
                  pl.BlockSpec((tile, D_MODEL), lambda i: (i, 0)),
                  pl.BlockSpec((None, mrows, D_MODEL), lambda i: ((i * tile) // rows_per_mod, 0, 0))],
        out_specs=pl.BlockSpec((tile, D_MODEL), lambda i: (i, 0)),
        out_shape=jax.ShapeDtypeStruct((n, D_MODEL), F32),
        compiler_params=_cparams(("parallel",)),
        name="combine",
    )(outs, slots, x2, gf)


def kernel(x_prompt, x_sample, cache_k, cache_v, state_pool, c_prompt, c_sample, rel_bias, norm_mix, w_ada,
           b_ada, w_in, q_norm, k_norm, sinks, w_pool, pool_scale, w_out, norm_ffn, w_router, b_router,
           w1, b1, w2, b2):
    depth = w_in.shape[0]
    assert depth == 1
    l = 0
    bp, sp, _ = x_prompt.shape
    bs = x_sample.shape[0]
    assert x_sample.shape[1] == 1 and sp % SORT_TILE == 0 and bs <= SORT_TILE
    n_p = bp * sp
    tiles_p = n_p // SORT_TILE
    max_chunks = tiles_p * (SORT_TILE * TOP_K // CHUNK + N_EXPERTS) + (bs * TOP_K // CHUNK + N_EXPERTS)
    n_blocks_max = -(-max_chunks // BLOCK_CHUNKS) + N_EXPERTS
    moe_steps = (n_blocks_max + 1 + 1) // 2

    mod = _modulation(jnp.concatenate([c_prompt, c_sample], axis=0), w_ada[l], b_ada[l])
    mod_p = [m.reshape(bp, 1, D_MODEL) for m in jnp.split(mod[:bp], 6, axis=-1)]
    mod_s = [m.reshape(1, bs, D_MODEL) for m in jnp.split(mod[bp:], 6, axis=-1)]

    head = jnp.arange(ATTN_WIDTH) // HEAD_DIM
    bd = (head[:, None] == head[None, :]).astype(BF16)
    w_in_b = w_in[l].astype(BF16)
    w_out_b = w_out[l].astype(BF16)
    w_pool_b = w_pool[l].astype(BF16)
    w_router_t = w_router[l].T

    xp = x_prompt.reshape(n_p, D_MODEL)
    q, k, v, u = _mixer_inputs(xp, mod_p[0], mod_p[1], norm_mix[l], w_in_b, bd, q_norm[l], k_norm[l],
                               tile=512, rows_per_mod=sp, precise=False)
    k3 = k.reshape(bp, sp, KV_WIDTH)
    v3 = v.reshape(bp, sp, KV_WIDTH)
    u3 = u.reshape(bp, sp, POOL_WIDTH)
    attn = _attn_prompt(q.reshape(bp, sp, ATTN_WIDTH), k3, v3, sinks[l], rel_bias)
    pool = _pool_prompt(u3, w_pool_b, pool_scale[l])
    keep = min(WINDOW, sp)
    nkp = k3[:, -keep:].reshape(bp, keep, N_KV_HEADS, HEAD_DIM)
    nvp = v3[:, -keep:].reshape(bp, keep, N_KV_HEADS, HEAD_DIM)
    if sp >= POOL_BUF:
        npp = u3[:, -POOL_BUF:]
    else:
        npp = jnp.concatenate([jnp.zeros((bp, POOL_BUF - sp, POOL_WIDTH), F32), u3], axis=1)
    x2_p, hs, slots_p, cnt_p = _route(
        xp, attn.reshape(n_p, ATTN_WIDTH), pool.reshape(n_p, POOL_WIDTH), mod_p[2], mod_p[3], mod_p[4],
        w_out_b, norm_ffn[l], w_router_t, b_router[l], None,
        tile=SORT_TILE, rows_per_mod=sp, tile0=0, extra_tiles=2)

    xs = x_sample.reshape(bs, D_MODEL)
    qs, ks, vs, us = _mixer_inputs(xs, mod_s[0], mod_s[1], norm_mix[l], w_in[l], bd, q_norm[l], k_norm[l],
                                   tile=bs, rows_per_mod=bs, precise=True)
    wbuf = cache_k.shape[2]
    attn_s, nks, nvs = _attn_sample(qs, ks, vs, cache_k[l].reshape(bs, wbuf, KV_WIDTH),
                                    cache_v[l].reshape(bs, wbuf, KV_WIDTH), sinks[l], rel_bias)
    pool_s, nps_t = _pool_sample(jnp.swapaxes(state_pool[l], 0, 1), us, w_pool_b, pool_scale[l])
    x2_s, hs, slots_s, cnt_s = _route(
        xs, attn_s.astype(BF16), pool_s, mod_s[2], mod_s[3], mod_s[4], w_out_b, norm_ffn[l],
        w_router_t, b_router[l], hs, tile=bs, rows_per_mod=bs, tile0=tiles_p, extra_tiles=0)

    cnt = jnp.concatenate([cnt_p[:, :, 0], cnt_s[:, :, 0]], axis=0).astype(jnp.int32)
    src, block_e, next_e, nblocks = _plan(cnt, moe_steps)
    outs = _moe(hs, src, block_e, next_e, nblocks, w1[l], b1[l], w2[l], b2[l], moe_steps,
                scratch_chunk=(tiles_p + 1) * TILE_CHUNKS)

    y_p = _combine(outs, slots_p, x2_p, mod_p[5], tile=SORT_TILE, rows_per_mod=sp, tile0=0)
    y_s = _combine(outs, slots_s, x2_s, mod_s[5], tile=bs, rows_per_mod=bs, tile0=tiles_p)

    return (y_p.reshape(bp, sp, D_MODEL), y_s.reshape(bs, 1, D_MODEL),
            nkp[None], nvp[None], npp[None],
            nks.reshape(1, bs, wbuf, N_KV_HEADS, HEAD_DIM), nvs.reshape(1, bs, wbuf, N_KV_HEADS, HEAD_DIM),
            jnp.swapaxes(nps_t, 0, 1)[None])
```

```python
import functools
import math

import jax
import jax.numpy as jnp
from jax import lax
from jax.experimental import pallas as pl
from jax.experimental.pallas import tpu as pltpu

F32 = jnp.float32
BF16 = jnp.bfloat16

D_MODEL = 1024
HEAD_DIM = 64
N_HEADS = 8
N_KV_HEADS = 2
GROUP = N_HEADS // N_KV_HEADS
ATTN_WIDTH = N_HEADS * HEAD_DIM
KV_WIDTH = N_KV_HEADS * HEAD_DIM
POOL_WIDTH = D_MODEL - ATTN_WIDTH
POOL_WINDOWS = (2, 4, 8, 16)
POOL_GROUP = POOL_WIDTH // len(POOL_WINDOWS)
POOL_BUF = max(POOL_WINDOWS) - 1
IN_WIDTH = ATTN_WIDTH + 2 * KV_WIDTH + POOL_WIDTH
WINDOW = 128
ATTN_BLOCK = 128
N_BUCKETS = 32
MAX_EXACT = 16
REL_MAX_DIST = 128
N_EXPERTS = 32
TOP_K = 4
D_FF = D_MODEL
SWIGLU_LIMIT = 7.0
SWIGLU_ALPHA = 1.702
EPS = 1e-6
NEG_INF = -1e30
PAST_LEN = 16384

LANES = 128
SUBLANES = 8
VMEM_LIMIT = 56 * 1024 * 1024

SORT_TILE = 256
ROUTE_SUB = 4
CHUNK = SUBLANES
TILE_ROWS = -(-(SORT_TILE * TOP_K + N_EXPERTS * (CHUNK - 1)) // LANES) * LANES
TILE_CHUNKS = TILE_ROWS // CHUNK
MOE_BLOCK = 256
BLOCK_CHUNKS = MOE_BLOCK // CHUNK
PACKED_W = D_MODEL // 2
ROW_W = PACKED_W
ZERO_CHUNK = TILE_CHUNKS - 1


def _bdot(a, b):
    return jnp.dot(a.astype(BF16), b.astype(BF16), preferred_element_type=F32)


def _split(a):
    hi = a.astype(BF16)
    lo = (a - hi.astype(F32)).astype(BF16)
    return hi, lo


def _dot3(a, b):
    ah, al = _split(a)
    bh, bl = _split(b)
    d = functools.partial(jnp.dot, preferred_element_type=F32)
    return d(ah, bh) + d(al, bh) + d(ah, bl)


def _pack_rows(x):
    bits = lax.bitcast_convert_type(x, jnp.int32)
    return bits[:, :PACKED_W] | lax.shift_right_logical(bits[:, PACKED_W:], 16)


def _unpack_rows(w):
    hi = lax.bitcast_convert_type(w & jnp.int32(-65536), F32)
    lo = lax.bitcast_convert_type(lax.shift_left(w, 16), F32)
    return hi.astype(BF16), lo.astype(BF16)


def _rms(x, g):
    return x * lax.rsqrt(jnp.mean(x * x, axis=-1, keepdims=True) + EPS) * g


def _cparams(sem, **kw):
    return pltpu.CompilerParams(dimension_semantics=sem, vmem_limit_bytes=VMEM_LIMIT, **kw)


def _ada_kernel(c_ref, w_ref, b_ref, o_ref):
    c = c_ref[...]
    s = c * jax.nn.sigmoid(c)
    o_ref[...] = _dot3(s, w_ref[...]) + b_ref[...]


def _modulation(c, w_ada, b_ada):
    rows = c.shape[0]
    n = w_ada.shape[1]
    tn = 512
    return pl.pallas_call(
        _ada_kernel,
        grid=(n // tn,),
        in_specs=[pl.BlockSpec((rows, D_MODEL), lambda j: (0, 0)),
                  pl.BlockSpec((D_MODEL, tn), lambda j: (0, j)),
                  pl.BlockSpec((1, tn), lambda j: (0, j))],
        out_specs=pl.BlockSpec((rows, tn), lambda j: (0, j)),
        out_shape=jax.ShapeDtypeStruct((rows, n), F32),
        compiler_params=_cparams(("parallel",)),
        name="modulation",
    )(c, w_ada, b_ada.reshape(1, n))


def _head_rms(t, bd, g):
    hi, lo = _split(t * t)
    ss = jnp.dot(hi, bd, preferred_element_type=F32) + jnp.dot(lo, bd, preferred_element_type=F32)
    return t * lax.rsqrt(ss * (1.0 / HEAD_DIM) + EPS) * g


def _mixin_kernel(x_ref, sh_ref, sc_ref, g_ref, w_ref, bd_ref, qn_ref, kn_ref,
                  q_ref, k_ref, v_ref, u_ref, *, precise):
    h = _rms(x_ref[...], g_ref[...]) * (1.0 + sc_ref[...]) + sh_ref[...]
    z = _dot3(h, w_ref[...]) if precise else _bdot(h, w_ref[...])
    q = z[:, :ATTN_WIDTH]
    k = z[:, ATTN_WIDTH:ATTN_WIDTH + KV_WIDTH]
    bd = bd_ref[...]
    q = _head_rms(q, bd, qn_ref[...])
    k = _head_rms(k, bd[:KV_WIDTH, :KV_WIDTH], kn_ref[...])
    q_ref[...] = (q * (HEAD_DIM ** -0.5)).astype(BF16)
    k_ref[...] = k
    v_ref[...] = z[:, ATTN_WIDTH + KV_WIDTH:ATTN_WIDTH + 2 * KV_WIDTH]
    u_ref[...] = z[:, ATTN_WIDTH + 2 * KV_WIDTH:]


def _mixer_inputs(x2d, shift, scale, norm_mix, w_in, bd, q_norm, k_norm, *, tile, rows_per_mod, precise):
    n = x2d.shape[0]
    mrows = shift.shape[1]
    mod_spec = pl.BlockSpec((None, mrows, D_MODEL), lambda i: ((i * tile) // rows_per_mod, 0, 0))
    const = lambda shape: pl.BlockSpec(shape, lambda i: (0,) * len(shape))
    row = lambda w: pl.BlockSpec((tile, w), lambda i: (i, 0))
    return pl.pallas_call(
        functools.partial(_mixin_kernel, precise=precise),
        grid=(n // tile,),
        in_specs=[row(D_MODEL), mod_spec, mod_spec, const((1, D_MODEL)), const((D_MODEL, IN_WIDTH)),
                  const((ATTN_WIDTH, ATTN_WIDTH)), const((1, ATTN_WIDTH)), const((1, KV_WIDTH))],
        out_specs=[row(ATTN_WIDTH), row(KV_WIDTH), row(KV_WIDTH), row(POOL_WIDTH)],
        out_shape=[jax.ShapeDtypeStruct((n, ATTN_WIDTH), BF16),
                   jax.ShapeDtypeStruct((n, KV_WIDTH), F32),
                   jax.ShapeDtypeStruct((n, KV_WIDTH), F32),
                   jax.ShapeDtypeStruct((n, POOL_WIDTH), F32)],
        compiler_params=_cparams(("parallel",)),
        name="mixer_inputs",
    )(x2d, shift, scale, norm_mix.reshape(1, D_MODEL), w_in, bd,
      jnp.tile(q_norm, N_HEADS).reshape(1, ATTN_WIDTH), jnp.tile(k_norm, N_KV_HEADS).reshape(1, KV_WIDTH))


def _t5_bucket(rel):
    n = jnp.maximum(rel, 0)
    nf = jnp.maximum(n, 1).astype(F32)
    large = MAX_EXACT + (jnp.log(nf / MAX_EXACT) / math.log(REL_MAX_DIST / MAX_EXACT)
                         * (N_BUCKETS - MAX_EXACT)).astype(jnp.int32)
    large = jnp.minimum(large, N_BUCKETS - 1)
    return jnp.where(n < MAX_EXACT, n, large)


def _bias_table(rel, rel_table):
    bucket = _t5_bucket(rel)
    table = rel_table.astype(F32)
    ids = jnp.arange(N_BUCKETS, dtype=bucket.dtype).reshape((N_BUCKETS, 1) + (1,) * rel.ndim)
    onehot = bucket[None, None] == ids
    bias = jnp.sum(jnp.where(onehot, table.reshape(table.shape + (1,) * rel.ndim), 0.0), axis=0)
    valid = (rel >= 0) & (rel < WINDOW)
    return jnp.where(valid[None], bias, NEG_INF)


def _attn_prompt_kernel(sink_ref, q_ref, kp_ref, kc_ref, vp_ref, vc_ref, bias_ref, o_ref):
    first = pl.program_id(1) == 0
    q = q_ref[...]
    kk = jnp.concatenate([kp_ref[...], kc_ref[...]], axis=0).astype(BF16)
    vv = jnp.concatenate([vp_ref[...], vc_ref[...]], axis=0).astype(BF16)
    col = lax.broadcasted_iota(jnp.int32, (GROUP * ATTN_BLOCK, 2 * ATTN_BLOCK), 1)
    row = lax.broadcasted_iota(jnp.int32, (GROUP * ATTN_BLOCK, 1), 0)
    no_prev = jnp.logical_and(first, col < ATTN_BLOCK)
    outs = []
    for kv in range(N_KV_HEADS):
        heads = range(kv * GROUP, (kv + 1) * GROUP)
        qg = jnp.concatenate([q[:, h * HEAD_DIM:(h + 1) * HEAD_DIM] for h in heads], axis=0)
        kh = kk[:, kv * HEAD_DIM:(kv + 1) * HEAD_DIM]
        vh = vv[:, kv * HEAD_DIM:(kv + 1) * HEAD_DIM]
        s = lax.dot_general(qg, kh, (((1,), (1,)), ((), ())), preferred_element_type=F32)
        s = s + bias_ref[kv]
        s = jnp.where(no_prev, NEG_INF, s)
        sink = jnp.zeros((GROUP * ATTN_BLOCK, 1), F32)
        for g, h in enumerate(heads):
            sink = jnp.where(row // ATTN_BLOCK == g, sink_ref[h], sink)
        m = jnp.maximum(jnp.max(s, axis=-1, keepdims=True), sink)
        p = jnp.exp(s - m)
        denom = jnp.sum(p, axis=-1, keepdims=True) + jnp.exp(sink - m)
        o = jnp.dot(p.astype(BF16), vh, preferred_element_type=F32) / denom
        outs += [o[g * ATTN_BLOCK:(g + 1) * ATTN_BLOCK] for g in range(GROUP)]
    o_ref[...] = jnp.concatenate(outs, axis=-1).astype(BF16)


def _attn_prompt(q, k, v, sinks, rel_table):
    b, s = q.shape[:2]
    nblk = s // ATTN_BLOCK
    qi = jnp.arange(ATTN_BLOCK, dtype=jnp.int32)[:, None]
    si = jnp.arange(2 * ATTN_BLOCK, dtype=jnp.int32)[None, :]
    bias = _bias_table(qi + ATTN_BLOCK - si, rel_table)
    bias = bias.reshape(N_KV_HEADS, GROUP * ATTN_BLOCK, 2 * ATTN_BLOCK)
    cur = lambda w: pl.BlockSpec((None, ATTN_BLOCK, w), lambda i, j, *_: (i, j, 0))
    prev = lambda w: pl.BlockSpec((None, ATTN_BLOCK, w), lambda i, j, *_: (i, jnp.maximum(j - 1, 0), 0))
    return pl.pallas_call(
        _attn_prompt_kernel,
        grid_spec=pltpu.PrefetchScalarGridSpec(
            num_scalar_prefetch=1,
            grid=(b, nblk),
            in_specs=[cur(ATTN_WIDTH), prev(KV_WIDTH), cur(KV_WIDTH), prev(KV_WIDTH), cur(KV_WIDTH),
                      pl.BlockSpec(bias.shape, lambda i, j, *_: (0, 0, 0))],
            out_specs=cur(ATTN_WIDTH)),
        out_shape=jax.ShapeDtypeStruct((b, s, ATTN_WIDTH), BF16),
        compiler_params=_cparams(("parallel", "parallel")),
        name="attn_prompt",
    )(sinks.astype(F32), q, k, k, v, v, bias)


def _attn_sample_kernel(sink_ref, q_ref, kc_ref, vc_ref, kn_ref, vn_ref, bias_ref, bnew_ref,
                        o_ref, nk_ref, nv_ref):
    kc = kc_ref[...]
    vc = vc_ref[...]
    kn = kn_ref[...]
    vn = vn_ref[...]
    w = kc.shape[1]
    pos = lax.broadcasted_iota(jnp.int32, kc.shape, 1)
    nk_ref[...] = jnp.where(pos == w - 1, kn[:, None, :], pltpu.roll(kc, w - 1, 1))
    nv_ref[...] = jnp.where(pos == w - 1, vn[:, None, :], pltpu.roll(vc, w - 1, 1))
    gi = lax.broadcasted_iota(jnp.int32, (1, GROUP, 1), 1)
    for kv in range(N_KV_HEADS):
        sl = slice(kv * HEAD_DIM, (kv + 1) * HEAD_DIM)
        qg = q_ref[:, kv]
        s = jnp.einsum('bgd,bsd->bgs', qg, kc[:, :, sl].astype(BF16), preferred_element_type=F32)
        s = s + bias_ref[kv][None]
        s_new = jnp.sum(qg.astype(F32) * kn[:, None, sl], axis=-1, keepdims=True) + bnew_ref[kv][None]
        sink = jnp.zeros((1, GROUP, 1), F32)
        for g in range(GROUP):
            sink = jnp.where(gi == g, sink_ref[kv * GROUP + g], sink)
        m = jnp.maximum(jnp.maximum(jnp.max(s, axis=-1, keepdims=True), s_new), sink)
        p = jnp.exp(s - m)
        p_new = jnp.exp(s_new - m)
        denom = jnp.sum(p, axis=-1, keepdims=True) + p_new + jnp.exp(sink - m)
        o = jnp.einsum('bgs,bsd->bgd', p.astype(BF16), vc[:, :, sl].astype(BF16), preferred_element_type=F32)
        o = o + p_new * vn[:, None, sl]
        o_ref[:, kv] = o / denom


def _attn_sample(q, k_new, v_new, cache_k, cache_v, sinks, rel_table, *, tile=16):
    bd, w = cache_k.shape[:2]
    rel = w - jnp.arange(w, dtype=jnp.int32)
    bias = _bias_table(rel, rel_table).reshape(N_KV_HEADS, GROUP, w)
    bnew = _bias_table(jnp.zeros((1,), jnp.int32), rel_table).reshape(N_KV_HEADS, GROUP, 1)
    q4 = q.reshape(bd, N_KV_HEADS, GROUP, HEAD_DIM)
    spec4 = pl.BlockSpec((tile, N_KV_HEADS, GROUP, HEAD_DIM), lambda i, *_: (i, 0, 0, 0))
    cache = pl.BlockSpec((tile, w, KV_WIDTH), lambda i, *_: (i, 0, 0))
    new = pl.BlockSpec((tile, KV_WIDTH), lambda i, *_: (i, 0))
    const3 = lambda a: pl.BlockSpec(a.shape, lambda i, *_: (0, 0, 0))
    o, nk, nv = pl.pallas_call(
        _attn_sample_kernel,
        grid_spec=pltpu.PrefetchScalarGridSpec(
            num_scalar_prefetch=1,
            grid=(bd // tile,),
            in_specs=[spec4, cache, cache, new, new, const3(bias), const3(bnew)],
            out_specs=[spec4, cache, cache]),
        out_shape=[jax.ShapeDtypeStruct(q4.shape, F32),
                   jax.ShapeDtypeStruct(cache_k.shape, F32),
                   jax.ShapeDtypeStruct(cache_v.shape, F32)],
        compiler_params=_cparams(("parallel",)),
        name="attn_sample",
    )(sinks.astype(F32), q4, cache_k, cache_v, k_new, v_new, bias, bnew)
    return o.reshape(bd, ATTN_WIDTH), nk, nv


def _pool_project(d_groups, wp_ref, ps_ref):
    out = [_bdot(d, wp_ref[g]) for g, d in enumerate(d_groups)]
    return (jnp.concatenate(out, axis=-1) * ps_ref[...]).astype(BF16)


def _pool_prompt_kernel(u_ref, halo_ref, wp_ref, ps_ref, o_ref, ext):
    t = pl.program_id(1)
    tile = u_ref.shape[0]
    halo = halo_ref[...]
    ext[0:2 * SUBLANES, :] = jnp.where(t == 0, jnp.zeros_like(halo), halo)
    ext[2 * SUBLANES:, :] = u_ref[...]
    pos = t * tile + lax.broadcasted_iota(jnp.int32, (tile, 1), 0)
    base = 2 * SUBLANES
    ds = []
    for g, w in enumerate(POOL_WINDOWS):
        sl = slice(g * POOL_GROUP, (g + 1) * POOL_GROUP)
        acc = ext[base:base + tile, sl]
        for j in range(1, w):
            acc = acc + ext[base - j:base - j + tile, sl]
        cnt = jnp.minimum(pos + 1, w).astype(F32)
        ds.append(acc / cnt - ext[base:base + tile, sl])
    o_ref[...] = _pool_project(ds, wp_ref, ps_ref)


def _pool_prompt(u, w_pool, pool_scale, *, tile=512):
    b, s, c = u.shape
    hb = 2 * SUBLANES
    return pl.pallas_call(
        _pool_prompt_kernel,
        grid=(b, s // tile),
        in_specs=[pl.BlockSpec((None, tile, c), lambda i, t: (i, t, 0)),
                  pl.BlockSpec((None, hb, c), lambda i, t: (i, jnp.maximum(t * (tile // hb) - 1, 0), 0)),
                  pl.BlockSpec(w_pool.shape, lambda i, t: (0, 0, 0)),
                  pl.BlockSpec((1, c), lambda i, t: (0, 0))],
        out_specs=pl.BlockSpec((None, tile, c), lambda i, t: (i, t, 0)),
        out_shape=jax.ShapeDtypeStruct((b, s, c), BF16),
        scratch_shapes=[pltpu.VMEM((tile + hb, c), F32)],
        compiler_params=_cparams(("parallel", "parallel")),
        name="pool_prompt",
    )(u, u, w_pool, pool_scale.reshape(1, c))


def _pool_sample_kernel(st_ref, u_ref, wp_ref, ps_ref, o_ref, ns_ref):
    u = u_ref[...]
    ns_ref[0:POOL_BUF - 1] = st_ref[1:POOL_BUF]
    ns_ref[POOL_BUF - 1] = u
    ds = []
    for g, w in enumerate(POOL_WINDOWS):
        sl = slice(g * POOL_GROUP, (g + 1) * POOL_GROUP)
        acc = u[:, sl]
        for j in range(1, w):
            acc = acc + st_ref[POOL_BUF - j][:, sl]
        cnt = float(min(PAST_LEN + 1, w))
        ds.append(acc / cnt - u[:, sl])
    o_ref[...] = _pool_project(ds, wp_ref, ps_ref)


def _pool_sample(state_t, u, w_pool, pool_scale):
    nb, bd, c = state_t.shape
    full = lambda a: pl.BlockSpec(a.shape, lambda: (0,) * a.ndim)
    ps = pool_scale.reshape(1, c)
    return pl.pallas_call(
        _pool_sample_kernel,
        in_specs=[full(state_t), full(u), full(w_pool), full(ps)],
        out_specs=[pl.BlockSpec((bd, c), lambda: (0, 0)), full(state_t)],
        out_shape=[jax.ShapeDtypeStruct((bd, c), BF16), jax.ShapeDtypeStruct(state_t.shape, F32)],
        compiler_params=pltpu.CompilerParams(vmem_limit_bytes=VMEM_LIMIT),
        name="pool_sample",
    )(state_t, u, w_pool, ps)


def _route_kernel(x_ref, attn_ref, pool_ref, gm_ref, sh_ref, sc_ref, wo_ref, nf_ref, wr_ref, br_ref,
                  tri_t_ref, tri_e_ref,
                  x2_ref, hs_ref, slot_ref, cnt_ref):
    sub = slot_ref.shape[0]
    n = x_ref.shape[0]
    tile = n // sub
    mix = (jnp.dot(attn_ref[...], wo_ref[:ATTN_WIDTH, :], preferred_element_type=F32)
           + jnp.dot(pool_ref[...], wo_ref[ATTN_WIDTH:, :], preferred_element_type=F32))
    x2 = x_ref[...] + gm_ref[...] * mix
    x2_ref[...] = x2
    h = _rms(x2, nf_ref[...]) * (1.0 + sc_ref[...]) + sh_ref[...]

    hh, hl = _split(h)
    wh, wl = _split(wr_ref[...])
    nt = functools.partial(lax.dot_general, dimension_numbers=(((1,), (1,)), ((), ())),
                           preferred_element_type=F32)
    logits = nt(wh, hh) + nt(wl, hh) + nt(wh, hl) + br_ref[...]

    eidx = lax.broadcasted_iota(jnp.int32, (N_EXPERTS, n), 0).astype(F32)
    work = logits
    tops, picks = [], []
    for _ in range(TOP_K):
        m = jnp.max(work, axis=0, keepdims=True)
        pick = jnp.min(jnp.where(work == m, eidx, float(N_EXPERTS)), axis=0, keepdims=True)
        work = jnp.where(eidx == pick, -jnp.inf, work)
        tops.append(m)
        picks.append(pick)
    ex = [jnp.exp(v - tops[0]) for v in tops]
    den = ex[0] + ex[1] + ex[2] + ex[3]
    gates = [e / den for e in ex]

    sel = jnp.zeros((N_EXPERTS, n), F32)
    for pick in picks:
        sel = sel + (eidx == pick).astype(F32)
    rank = jnp.dot(sel.astype(BF16), tri_t_ref[...], preferred_element_type=F32)
    cnts = [jnp.sum(sel[:, t * tile:(t + 1) * tile], axis=1, keepdims=True) for t in range(sub)]
    padded = jnp.concatenate(
        [jnp.broadcast_to(jnp.ceil(c * (1.0 / CHUNK)) * CHUNK, (N_EXPERTS, LANES)) for c in cnts], axis=1)
    seg = jnp.dot(tri_e_ref[...], padded.astype(BF16), preferred_element_type=F32)
    dest = jnp.concatenate([seg[:, t * LANES:t * LANES + 1] + rank[:, t * tile:(t + 1) * tile]
                            for t in range(sub)], axis=1)
    slots = [jnp.sum(jnp.where(eidx == pick, dest, 0.0), axis=0, keepdims=True) for pick in picks]

    ridx = lax.broadcasted_iota(jnp.int32, (TILE_ROWS, tile), 0).astype(F32)
    for t in range(sub):
        cols = slice(t * tile, (t + 1) * tile)
        cnt_ref[t] = jnp.broadcast_to(cnts[t], (N_EXPERTS, LANES))
        slot_ref[t] = jnp.concatenate([v[:, cols] for v in slots + gates], axis=0)
        hit = ridx == slots[0][:, cols]
        for s in slots[1:]:
            hit = jnp.logical_or(hit, ridx == s[:, cols])
        perm = jnp.where(hit, 1.0, 0.0).astype(BF16)
        hs_ref[pl.ds(t * TILE_ROWS, TILE_ROWS), :] = _pack_rows(
            jnp.dot(perm, hh[t * tile:(t + 1) * tile, :], preferred_element_type=F32))


def _route(x2d, attn, pool, gm, sh, sc, w_out, norm_ffn, w_router_t, b_router, hs_prev, *, tile, sub,
           rows_per_mod, tile0, extra_tiles):
    n = x2d.shape[0]
    nt = n // tile
    assert nt % sub == 0 and extra_tiles % sub == 0
    own_steps = nt // sub
    steps = own_steps + extra_tiles // sub
    mrows = gm.shape[1]
    assert mrows == 1 or (sub == 1 and mrows == tile)
    last = lambda i: jnp.minimum(i, own_steps - 1)
    mod_spec = pl.BlockSpec((None, mrows, D_MODEL), lambda i: ((last(i) * sub * tile) // rows_per_mod, 0, 0))
    const = lambda shape: pl.BlockSpec(shape, lambda i: (0,) * len(shape))
    row = lambda w: pl.BlockSpec((sub * tile, w), lambda i: (last(i), 0))
    tok = jnp.arange(sub * tile)
    tri_t = jnp.logical_and(tok[:, None] < tok[None, :],
                            tok[:, None] // tile == tok[None, :] // tile).astype(BF16)
    tri_e = (jnp.arange(N_EXPERTS)[None, :] < jnp.arange(N_EXPERTS)[:, None]).astype(BF16)
    in_specs = [row(D_MODEL), row(ATTN_WIDTH), row(POOL_WIDTH), mod_spec, mod_spec, mod_spec,
                const((D_MODEL, D_MODEL)), const((1, D_MODEL)), const((N_EXPERTS, D_MODEL)),
                const((N_EXPERTS, 1)), const((sub * tile, sub * tile)), const((N_EXPERTS, N_EXPERTS))]
    args = [x2d, attn, pool, gm, sh, sc, w_out, norm_ffn.reshape(1, D_MODEL), w_router_t,
            b_router.reshape(N_EXPERTS, 1), tri_t, tri_e]
    n_in = len(args)
    kern = _route_kernel
    aliases = {}
    hs_rows = (tile0 + steps * sub) * TILE_ROWS
    assert tile0 % sub == 0
    if hs_prev is not None:
        in_specs.append(pl.BlockSpec(memory_space=pl.ANY))
        args.append(hs_prev)
        aliases = {n_in: 1}
        kern = lambda *refs: _route_kernel(*refs[:n_in], *refs[n_in + 1:])
        hs_rows = hs_prev.shape[0]
    return pl.pallas_call(
        kern,
        grid=(steps,),
        in_specs=in_specs,
        out_specs=[row(D_MODEL),
                   pl.BlockSpec((sub * TILE_ROWS, ROW_W), lambda i: (i + tile0 // sub, 0)),
                   pl.BlockSpec((sub, 2 * TOP_K, tile), lambda i: (last(i), 0, 0)),
                   pl.BlockSpec((sub, N_EXPERTS, LANES), lambda i: (last(i), 0, 0))],
        out_shape=[jax.ShapeDtypeStruct((n, D_MODEL), F32),
                   jax.ShapeDtypeStruct((hs_rows, ROW_W), jnp.int32),
                   jax.ShapeDtypeStruct((nt, 2 * TOP_K, tile), F32),
                   jax.ShapeDtypeStruct((nt, N_EXPERTS, LANES), F32)],
        input_output_aliases=aliases,
        compiler_params=_cparams(("arbitrary",)),
        name="route",
    )(*args)


def _moe_kernel(src_ref, be_ref, nxt_ref, nb_ref, hs_hbm, w1_hbm, b1_ref, w2_hbm, b2_ref, out_hbm,
                lhs, obuf, w1s, w2s, w1c, w2c, sem_in, sem_out, sem_w, *, scratch_chunk):
    b = pl.program_id(0)
    nb = nb_ref[0]
    slot = b % 2

    def weight_copies(e):
        return (pltpu.make_async_copy(w1_hbm.at[e], w1s, sem_w.at[0]),
                pltpu.make_async_copy(w2_hbm.at[e], w2s, sem_w.at[1]))

    def chunk_rows(c):
        return pl.ds(pl.multiple_of(c * CHUNK, CHUNK), CHUNK)

    def start_in(blk, s):
        for j in range(BLOCK_CHUNKS):
            c = src_ref[blk * BLOCK_CHUNKS + j]
            c = jnp.where(c < 0, ZERO_CHUNK, c)
            pltpu.make_async_copy(hs_hbm.at[chunk_rows(c)], lhs.at[s, pl.ds(j * CHUNK, CHUNK)],
                                  sem_in.at[s]).start()

    def wait_in(s):
        pltpu.make_async_copy(hs_hbm.at[pl.ds(0, MOE_BLOCK)], lhs.at[s], sem_in.at[s]).wait()

    def start_out(blk, s):
        for j in range(BLOCK_CHUNKS):
            c = src_ref[blk * BLOCK_CHUNKS + j]
            c = jnp.where(c < 0, scratch_chunk + s * BLOCK_CHUNKS + j, c)
            pltpu.make_async_copy(obuf.at[s, pl.ds(j * CHUNK, CHUNK)],
                                  out_hbm.at[chunk_rows(c), pl.ds(0, PACKED_W)], sem_out.at[s]).start()

    def wait_out(s):
        pltpu.make_async_copy(obuf.at[s], out_hbm.at[pl.ds(0, MOE_BLOCK), pl.ds(0, PACKED_W)],
                              sem_out.at[s]).wait()

    @pl.when(b < nb)
    def _():
        @pl.when(b == 0)
        def _():
            start_in(0, 0)
            for cp in weight_copies(be_ref[0]):
                cp.start()

        @pl.when(jnp.logical_or(b == 0, be_ref[b] != be_ref[jnp.maximum(b - 1, 0)]))
        def _():
            for cp in weight_copies(be_ref[b]):
                cp.wait()
            w1c[...] = w1s[...].astype(BF16)
            w2c[...] = w2s[...].astype(BF16)
            nxt = nxt_ref[b]

            @pl.when(nxt >= 0)
            def _():
                for cp in weight_copies(nxt):
                    cp.start()

        wait_in(slot)

        @pl.when(b >= 2)
        def _():
            wait_out(slot)

        start_in(b + 1, 1 - slot)

        xh, xl = _unpack_rows(lhs[slot])
        gu = (jnp.dot(xh, w1c[:PACKED_W, :], preferred_element_type=F32)
              + jnp.dot(xl, w1c[PACKED_W:, :], preferred_element_type=F32) + b1_ref[...])
        gate = jnp.minimum(gu[:, :D_FF], SWIGLU_LIMIT)
        up = jnp.clip(gu[:, D_FF:], -SWIGLU_LIMIT, SWIGLU_LIMIT)
        act = (up + 1.0) * (gate * jax.nn.sigmoid(SWIGLU_ALPHA * gate))
        y = jnp.dot(act.astype(BF16), w2c[...], preferred_element_type=F32) + b2_ref[...]
        obuf[slot] = _pack_rows(y.astype(BF16).astype(F32))

        start_out(b, slot)

        @pl.when(b == nb - 1)
        def _():
            wait_in(1 - slot)

            @pl.when(b >= 1)
            def _():
                wait_out(1 - slot)
            wait_out(slot)


def _moe(hs, src, block_e, next_e, nblocks, w1, b1, w2, b2, n_blocks_max, scratch_chunk):
    bspec = lambda w: pl.BlockSpec((None, 1, w), lambda b, src, be, nxt, nb: (be[b], 0, 0))
    hbm = pl.BlockSpec(memory_space=pl.ANY)
    return pl.pallas_call(
        functools.partial(_moe_kernel, scratch_chunk=scratch_chunk),
        grid_spec=pltpu.PrefetchScalarGridSpec(
            num_scalar_prefetch=4,
            grid=(n_blocks_max,),
            in_specs=[hbm, hbm, bspec(2 * D_FF), hbm, bspec(D_MODEL)],
            out_specs=hbm,
            scratch_shapes=[pltpu.VMEM((2, MOE_BLOCK, ROW_W), jnp.int32),
                            pltpu.VMEM((2, MOE_BLOCK, PACKED_W), jnp.int32),
                            pltpu.VMEM((D_MODEL, 2 * D_FF), F32),
                            pltpu.VMEM((D_FF, D_MODEL), F32),
                            pltpu.VMEM((D_MODEL, 2 * D_FF), BF16),
                            pltpu.VMEM((D_FF, D_MODEL), BF16),
                            pltpu.SemaphoreType.DMA((2,)),
                            pltpu.SemaphoreType.DMA((2,)),
                            pltpu.SemaphoreType.DMA((2,))]),
        out_shape=jax.ShapeDtypeStruct(hs.shape, jnp.int32),
        input_output_aliases={4: 0},
        compiler_params=_cparams(("arbitrary",)),
        name="moe_experts",
    )(src, block_e, next_e, nblocks, hs, w1, b1.reshape(N_EXPERTS, 1, 2 * D_FF), w2,
      b2.reshape(N_EXPERTS, 1, D_MODEL))


def _plan(cnt, n_blocks_max):
    nt = cnt.shape[0]
    nch = (cnt + (CHUNK - 1)) // CHUNK
    lstart = jnp.cumsum(nch, axis=1) - nch
    ne = jnp.sum(nch, axis=0)
    nbe = (ne + (BLOCK_CHUNKS - 1)) // BLOCK_CHUNKS
    bend = jnp.cumsum(nbe)
    nblocks = bend[-1]
    gstart = (bend - nbe)[None, :] * BLOCK_CHUNKS + (jnp.cumsum(nch, axis=0) - nch)
    gs = gstart.T.reshape(-1)
    nc = nch.T.reshape(-1)
    s0 = (jnp.arange(nt, dtype=jnp.int32)[:, None] * TILE_CHUNKS + lstart).T.reshape(-1)
    c = jnp.arange((n_blocks_max + 1) * BLOCK_CHUNKS, dtype=jnp.int32)[:, None]
    inside = jnp.logical_and(c >= gs[None, :], c < (gs + nc)[None, :])
    src = jnp.sum(jnp.where(inside, (s0 - gs)[None, :] + c + 1, 0), axis=1) - 1
    blk = jnp.arange(n_blocks_max, dtype=jnp.int32)
    be = jnp.sum((blk[:, None] >= bend[None, :]).astype(jnp.int32), axis=1)
    be_last = jnp.sum((nblocks - 1 >= bend).astype(jnp.int32))
    be = jnp.minimum(be, be_last).astype(jnp.int32)
    eid = jnp.arange(N_EXPERTS, dtype=jnp.int32)
    later = jnp.logical_and(eid[None, :] > be[:, None], (nbe > 0)[None, :])
    nxt = jnp.min(jnp.where(later, eid[None, :], N_EXPERTS), axis=1)
    nxt = jnp.where(nxt == N_EXPERTS, -1, nxt).astype(jnp.int32)
    return src, be, nxt, nblocks.reshape(1).astype(jnp.int32)


def _combine_kernel(o_ref, slot_ref, x2_ref, gf_ref, y_ref):
    tile = x2_ref.shape[0]
    ridx = lax.broadcasted_iota(jnp.int32, (TILE_ROWS, tile), 0).astype(F32)
    gmat = jnp.zeros((TILE_ROWS, tile), F32)
    for k in range(TOP_K):
        gmat = jnp.where(ridx == slot_ref[k:k + 1, :], slot_ref[TOP_K + k:TOP_K + k + 1, :], gmat)
    gh, gl = _split(gmat)
    tn = functools.partial(lax.dot_general, dimension_numbers=(((0,), (0,)), ((), ())),
                           preferred_element_type=F32)
    oh, ol = _unpack_rows(o_ref[...])
    y = jnp.concatenate([tn(gh, oh) + tn(gl, oh), tn(gh, ol) + tn(gl, ol)], axis=1)
    y_ref[...] = x2_ref[...] + gf_ref[...] * y


def _combine(outs, slots, x2, gf, *, tile, rows_per_mod, tile0):
    n = x2.shape[0]
    mrows = gf.shape[1]
    return pl.pallas_call(
        _combine_kernel,
        grid=(n // tile,),
        in_specs=[pl.BlockSpec((TILE_ROWS, PACKED_W), lambda i: (i + tile0, 0)),
                  pl.BlockSpec((None, 2 * TOP_K, tile), lambda i: (i, 0, 0)),
                  pl.BlockSpec((tile, D_MODEL), lambda i: (i, 0)),
                  pl.BlockSpec((None, mrows, D_MODEL), lambda i: ((i * tile) // rows_per_mod, 0, 0))],
        out_specs=pl.BlockSpec((tile, D_MODEL), lambda i: (i, 0)),
        out_shape=jax.ShapeDtypeStruct((n, D_MODEL), F32),
        compiler_params=_cparams(("parallel",)),
        name="combine",
    )(outs, slots, x2, gf)


def kernel(x_prompt, x_sample, cache_k, cache_v, state_pool, c_prompt, c_sample, rel_bias, norm_mix, w_ada,
           b_ada, w_in, q_norm, k_norm, sinks, w_pool, pool_scale, w_out, norm_ffn, w_router, b_router,
           w1, b1, w2, b2):
    depth = w_in.shape[0]
    assert depth == 1
    l = 0
    bp, sp, _ = x_prompt.shape
    bs = x_sample.shape[0]
    assert x_sample.shape[1] == 1 and sp % SORT_TILE == 0 and bs <= SORT_TILE
    n_p = bp * sp
    tiles_p = n_p // SORT_TILE
    max_chunks = tiles_p * (SORT_TILE * TOP_K // CHUNK + N_EXPERTS) + (bs * TOP_K // CHUNK + N_EXPERTS)
    n_blocks_max = -(-max_chunks // BLOCK_CHUNKS) + N_EXPERTS

    mod = _modulation(jnp.concatenate([c_prompt, c_sample], axis=0), w_ada[l], b_ada[l])
    mod_p = [m.reshape(bp, 1, D_MODEL) for m in jnp.split(mod[:bp], 6, axis=-1)]
    mod_s = [m.reshape(1, bs, D_MODEL) for m in jnp.split(mod[bp:], 6, axis=-1)]

    head = jnp.arange(ATTN_WIDTH) // HEAD_DIM
    bd = (head[:, None] == head[None, :]).astype(BF16)
    w_in_b = w_in[l].astype(BF16)
    w_out_b = w_out[l].astype(BF16)
    w_pool_b = w_pool[l].astype(BF16)
    w_router_t = w_router[l].T

    xp = x_prompt.reshape(n_p, D_MODEL)
    q, k, v, u = _mixer_inputs(xp, mod_p[0], mod_p[1], norm_mix[l], w_in_b, bd, q_norm[l], k_norm[l],
                               tile=512, rows_per_mod=sp, precise=False)
    k3 = k.reshape(bp, sp, KV_WIDTH)
    v3 = v.reshape(bp, sp, KV_WIDTH)
    u3 = u.reshape(bp, sp, POOL_WIDTH)
    attn = _attn_prompt(q.reshape(bp, sp, ATTN_WIDTH), k3, v3, sinks[l], rel_bias)
    pool = _pool_prompt(u3, w_pool_b, pool_scale[l])
    keep = min(WINDOW, sp)
    nkp = k3[:, -keep:].reshape(bp, keep, N_KV_HEADS, HEAD_DIM)
    nvp = v3[:, -keep:].reshape(bp, keep, N_KV_HEADS, HEAD_DIM)
    if sp >= POOL_BUF:
        npp = u3[:, -POOL_BUF:]
    else:
        npp = jnp.concatenate([jnp.zeros((bp, POOL_BUF - sp, POOL_WIDTH), F32), u3], axis=1)
    x2_p, hs, slots_p, cnt_p = _route(
        xp, attn.reshape(n_p, ATTN_WIDTH), pool.reshape(n_p, POOL_WIDTH), mod_p[2], mod_p[3], mod_p[4],
        w_out_b, norm_ffn[l], w_router_t, b_router[l], None,
        tile=SORT_TILE, sub=ROUTE_SUB, rows_per_mod=sp, tile0=0, extra_tiles=max(2, ROUTE_SUB))

    xs = x_sample.reshape(bs, D_MODEL)
    qs, ks, vs, us = _mixer_inputs(xs, mod_s[0], mod_s[1], norm_mix[l], w_in[l], bd, q_norm[l], k_norm[l],
                                   tile=bs, rows_per_mod=bs, precise=True)
    wbuf = cache_k.shape[2]
    attn_s, nks, nvs = _attn_sample(qs, ks, vs, cache_k[l].reshape(bs, wbuf, KV_WIDTH),
                                    cache_v[l].reshape(bs, wbuf, KV_WIDTH), sinks[l], rel_bias)
    pool_s, nps_t = _pool_sample(jnp.swapaxes(state_pool[l], 0, 1), us, w_pool_b, pool_scale[l])
    x2_s, hs, slots_s, cnt_s = _route(
        xs, attn_s.astype(BF16), pool_s, mod_s[2], mod_s[3], mod_s[4], w_out_b, norm_ffn[l],
        w_router_t, b_router[l], hs, tile=bs, sub=1, rows_per_mod=bs, tile0=tiles_p, extra_tiles=0)

    cnt = jnp.concatenate([cnt_p[:, :, 0], cnt_s[:, :, 0]], axis=0).astype(jnp.int32)
    src, block_e, next_e, nblocks = _plan(cnt, n_blocks_max)
    outs = _moe(hs, src, block_e, next_e, nblocks, w1[l], b1[l], w2[l], b2[l], n_blocks_max,
                scratch_chunk=(tiles_p + 1) * TILE_CHUNKS)

    y_p = _combine(outs, slots_p, x2_p, mod_p[5], tile=SORT_TILE, rows_per_mod=sp, tile0=0)
    y_s = _combine(outs, slots_s, x2_s, mod_s[5], tile=bs, rows_per_mod=bs, tile0=tiles_p)

    return (y_p.reshape(bp, sp, D_MODEL), y_s.reshape(bs, 1, D_MODEL),
            nkp[None], nvp[None], npp[None],
            nks.reshape(1, bs, wbuf, N_KV_HEADS, HEAD_DIM), nvs.reshape(1, bs, wbuf, N_KV_HEADS, HEAD_DIM),
            jnp.swapaxes(nps_t, 0, 1)[None])
```

```python
import functools
import math

import jax
import jax.numpy as jnp
from jax import lax
from jax.experimental import pallas as pl
from jax.experimental.pallas import tpu as pltpu

F32 = jnp.float32
BF16 = jnp.bfloat16

D_MODEL = 1024
HEAD_DIM = 64
N_HEADS = 8
N_KV_HEADS = 2
GROUP = N_HEADS // N_KV_HEADS
ATTN_WIDTH = N_HEADS * HEAD_DIM
KV_WIDTH = N_KV_HEADS * HEAD_DIM
POOL_WIDTH = D_MODEL - ATTN_WIDTH
POOL_WINDOWS = (2, 4, 8, 16)
POOL_GROUP = POOL_WIDTH // len(POOL_WINDOWS)
POOL_BUF = max(POOL_WINDOWS) - 1
IN_WIDTH = ATTN_WIDTH + 2 * KV_WIDTH + POOL_WIDTH
WINDOW = 128
ATTN_BLOCK = 128
N_BUCKETS = 32
MAX_EXACT = 16
REL_MAX_DIST = 128
N_EXPERTS = 32
TOP_K = 4
D_FF = D_MODEL
SWIGLU_LIMIT = 7.0
SWIGLU_ALPHA = 1.702
EPS = 1e-6
NEG_INF = -1e30
PAST_LEN = 16384

LANES = 128
SUBLANES = 8
VMEM_LIMIT = 56 * 1024 * 1024

ATTN_QB = 2

SORT_TILE = 256
ROUTE_SUB = 4
CHUNK = SUBLANES
TILE_ROWS = -(-(SORT_TILE * TOP_K + N_EXPERTS * (CHUNK - 1)) // LANES) * LANES
TILE_CHUNKS = TILE_ROWS // CHUNK
MOE_BLOCK = 256
BLOCK_CHUNKS = MOE_BLOCK // CHUNK
PACKED_W = D_MODEL // 2
ROW_W = PACKED_W
ZERO_CHUNK = TILE_CHUNKS - 1


def _bdot(a, b):
    return jnp.dot(a.astype(BF16), b.astype(BF16), preferred_element_type=F32)


def _split(a):
    hi = a.astype(BF16)
    lo = (a - hi.astype(F32)).astype(BF16)
    return hi, lo


def _dot3(a, b):
    ah, al = _split(a)
    bh, bl = _split(b)
    d = functools.partial(jnp.dot, preferred_element_type=F32)
    return d(ah, bh) + d(al, bh) + d(ah, bl)


def _pack_rows(x):
    bits = lax.bitcast_convert_type(x, jnp.int32)
    return bits[:, :PACKED_W] | lax.shift_right_logical(bits[:, PACKED_W:], 16)


def _unpack_rows(w):
    hi = lax.bitcast_convert_type(w & jnp.int32(-65536), F32)
    lo = lax.bitcast_convert_type(lax.shift_left(w, 16), F32)
    return hi.astype(BF16), lo.astype(BF16)


def _rms(x, g):
    return x * lax.rsqrt(jnp.mean(x * x, axis=-1, keepdims=True) + EPS) * g


def _cparams(sem, **kw):
    return pltpu.CompilerParams(dimension_semantics=sem, vmem_limit_bytes=VMEM_LIMIT, **kw)


def _ada_kernel(c_ref, w_ref, b_ref, o_ref):
    c = c_ref[...]
    s = c * jax.nn.sigmoid(c)
    o_ref[...] = _dot3(s, w_ref[...]) + b_ref[...]


def _modulation(c, w_ada, b_ada):
    rows = c.shape[0]
    n = w_ada.shape[1]
    tn = 512
    return pl.pallas_call(
        _ada_kernel,
        grid=(n // tn,),
        in_specs=[pl.BlockSpec((rows, D_MODEL), lambda j: (0, 0)),
                  pl.BlockSpec((D_MODEL, tn), lambda j: (0, j)),
                  pl.BlockSpec((1, tn), lambda j: (0, j))],
        out_specs=pl.BlockSpec((rows, tn), lambda j: (0, j)),
        out_shape=jax.ShapeDtypeStruct((rows, n), F32),
        compiler_params=_cparams(("parallel",)),
        name="modulation",
    )(c, w_ada, b_ada.reshape(1, n))


def _head_rms(t, bd, g, precise):
    if precise:
        hi, lo = _split(t * t)
        ss = jnp.dot(hi, bd, preferred_element_type=F32) + jnp.dot(lo, bd, preferred_element_type=F32)
    else:
        ss = _bdot(t * t, bd)
    return t * lax.rsqrt(ss * (1.0 / HEAD_DIM) + EPS) * g


def _mixin_kernel(x_ref, sh_ref, sc_ref, g_ref, w_ref, bd_ref, qn_ref, kn_ref,
                  q_ref, k_ref, v_ref, u_ref, *, precise):
    h = _rms(x_ref[...], g_ref[...]) * (1.0 + sc_ref[...]) + sh_ref[...]
    z = _dot3(h, w_ref[...]) if precise else _bdot(h, w_ref[...])
    q = z[:, :ATTN_WIDTH]
    k = z[:, ATTN_WIDTH:ATTN_WIDTH + KV_WIDTH]
    bd = bd_ref[...]
    q = _head_rms(q, bd, qn_ref[...], precise)
    k = _head_rms(k, bd[:KV_WIDTH, :KV_WIDTH], kn_ref[...], precise)
    q_ref[...] = (q * (HEAD_DIM ** -0.5)).astype(BF16)
    k_ref[...] = k
    v_ref[...] = z[:, ATTN_WIDTH + KV_WIDTH:ATTN_WIDTH + 2 * KV_WIDTH]
    u_ref[...] = z[:, ATTN_WIDTH + 2 * KV_WIDTH:]


def _mixer_inputs(x2d, shift, scale, norm_mix, w_in, bd, q_norm, k_norm, *, tile, rows_per_mod, precise):
    n = x2d.shape[0]
    mrows = shift.shape[1]
    mod_spec = pl.BlockSpec((None, mrows, D_MODEL), lambda i: ((i * tile) // rows_per_mod, 0, 0))
    const = lambda shape: pl.BlockSpec(shape, lambda i: (0,) * len(shape))
    row = lambda w: pl.BlockSpec((tile, w), lambda i: (i, 0))
    return pl.pallas_call(
        functools.partial(_mixin_kernel, precise=precise),
        grid=(n // tile,),
        in_specs=[row(D_MODEL), mod_spec, mod_spec, const((1, D_MODEL)), const((D_MODEL, IN_WIDTH)),
                  const((ATTN_WIDTH, ATTN_WIDTH)), const((1, ATTN_WIDTH)), const((1, KV_WIDTH))],
        out_specs=[row(ATTN_WIDTH), row(KV_WIDTH), row(KV_WIDTH), row(POOL_WIDTH)],
        out_shape=[jax.ShapeDtypeStruct((n, ATTN_WIDTH), BF16),
                   jax.ShapeDtypeStruct((n, KV_WIDTH), F32),
                   jax.ShapeDtypeStruct((n, KV_WIDTH), F32),
                   jax.ShapeDtypeStruct((n, POOL_WIDTH), F32)],
        compiler_params=_cparams(("parallel",)),
        name="mixer_inputs",
    )(x2d, shift, scale, norm_mix.reshape(1, D_MODEL), w_in, bd,
      jnp.tile(q_norm, N_HEADS).reshape(1, ATTN_WIDTH), jnp.tile(k_norm, N_KV_HEADS).reshape(1, KV_WIDTH))


def _t5_bucket(rel):
    n = jnp.maximum(rel, 0)
    nf = jnp.maximum(n, 1).astype(F32)
    large = MAX_EXACT + (jnp.log(nf / MAX_EXACT) / math.log(REL_MAX_DIST / MAX_EXACT)
                         * (N_BUCKETS - MAX_EXACT)).astype(jnp.int32)
    large = jnp.minimum(large, N_BUCKETS - 1)
    return jnp.where(n < MAX_EXACT, n, large)


def _bias_table(rel, rel_table):
    bucket = _t5_bucket(rel)
    table = rel_table.astype(F32)
    ids = jnp.arange(N_BUCKETS, dtype=bucket.dtype).reshape((N_BUCKETS, 1) + (1,) * rel.ndim)
    onehot = bucket[None, None] == ids
    bias = jnp.sum(jnp.where(onehot, table.reshape(table.shape + (1,) * rel.ndim), 0.0), axis=0)
    valid = (rel >= 0) & (rel < WINDOW)
    return jnp.where(valid[None], bias, NEG_INF)


def _attn_prompt_kernel(sink_ref, q_ref, kp_ref, kc_ref, vp_ref, vc_ref, bias_ref, o_ref):
    first = pl.program_id(1) == 0
    kk = jnp.concatenate([kp_ref[...], kc_ref[...]], axis=0).astype(BF16)
    vv = jnp.concatenate([vp_ref[...], vc_ref[...]], axis=0).astype(BF16)
    key = lax.broadcasted_iota(jnp.int32, (2 * ATTN_BLOCK, 1), 0)
    no_prev = jnp.logical_and(first, key < ATTN_BLOCK)
    lane = lax.broadcasted_iota(jnp.int32, (1, GROUP * ATTN_BLOCK), 1)
    contract = lambda a, b, dims: lax.dot_general(a, b, (dims, ((), ())), preferred_element_type=F32)
    for i in range(ATTN_QB):
        q = q_ref[i * ATTN_BLOCK:(i + 1) * ATTN_BLOCK, :]
        halves = []
        for kv in range(N_KV_HEADS):
            heads = range(kv * GROUP, (kv + 1) * GROUP)
            qg = jnp.concatenate([q[:, h * HEAD_DIM:(h + 1) * HEAD_DIM] for h in heads], axis=0)
            kh = kk[i * ATTN_BLOCK:(i + 2) * ATTN_BLOCK, kv * HEAD_DIM:(kv + 1) * HEAD_DIM]
            vh = vv[i * ATTN_BLOCK:(i + 2) * ATTN_BLOCK, kv * HEAD_DIM:(kv + 1) * HEAD_DIM]
            s = contract(kh, qg, ((1,), (1,))) + bias_ref[kv]
            if i == 0:
                s = jnp.where(no_prev, NEG_INF, s)
            sink = jnp.zeros((1, GROUP * ATTN_BLOCK), F32)
            for g, h in enumerate(heads):
                sink = jnp.where(lane // ATTN_BLOCK == g, sink_ref[h], sink)
            m = jnp.maximum(jnp.max(s, axis=0, keepdims=True), sink)
            p = jnp.exp(s - m)
            denom = jnp.sum(p, axis=0, keepdims=True) + jnp.exp(sink - m)
            halves.append(contract(vh, p.astype(BF16), ((0,), (0,))) / denom)
        o_t = jnp.concatenate(halves, axis=0)
        per_g = [o_t[:, g * ATTN_BLOCK:(g + 1) * ATTN_BLOCK].T for g in range(GROUP)]
        out = [t[:, kv * HEAD_DIM:(kv + 1) * HEAD_DIM] for kv in range(N_KV_HEADS) for t in per_g]
        o_ref[i * ATTN_BLOCK:(i + 1) * ATTN_BLOCK, :] = jnp.concatenate(out, axis=-1).astype(BF16)


def _attn_prompt(q, k, v, sinks, rel_table):
    b, s = q.shape[:2]
    qrows = ATTN_QB * ATTN_BLOCK
    assert s % qrows == 0
    qi = jnp.arange(ATTN_BLOCK, dtype=jnp.int32)[:, None]
    si = jnp.arange(2 * ATTN_BLOCK, dtype=jnp.int32)[None, :]
    bias = _bias_table(qi + ATTN_BLOCK - si, rel_table)
    bias = jnp.swapaxes(bias.reshape(N_KV_HEADS, GROUP * ATTN_BLOCK, 2 * ATTN_BLOCK), 1, 2)
    cur = lambda w: pl.BlockSpec((None, qrows, w), lambda i, j, *_: (i, j, 0))
    prev = lambda w: pl.BlockSpec((None, ATTN_BLOCK, w),
                                  lambda i, j, *_: (i, jnp.maximum(j * ATTN_QB - 1, 0), 0))
    return pl.pallas_call(
        _attn_prompt_kernel,
        grid_spec=pltpu.PrefetchScalarGridSpec(
            num_scalar_prefetch=1,
            grid=(b, s // qrows),
            in_specs=[cur(ATTN_WIDTH), prev(KV_WIDTH), cur(KV_WIDTH), prev(KV_WIDTH), cur(KV_WIDTH),
                      pl.BlockSpec(bias.shape, lambda i, j, *_: (0, 0, 0))],
            out_specs=cur(ATTN_WIDTH)),
        out_shape=jax.ShapeDtypeStruct((b, s, ATTN_WIDTH), BF16),
        compiler_params=_cparams(("parallel", "parallel")),
        name="attn_prompt",
    )(sinks.astype(F32), q, k, k, v, v, bias)


def _attn_sample_kernel(sink_ref, q_ref, kc_ref, vc_ref, kn_ref, vn_ref, bias_ref, bnew_ref,
                        o_ref, nk_ref, nv_ref):
    kc = kc_ref[...]
    vc = vc_ref[...]
    kn = kn_ref[...]
    vn = vn_ref[...]
    w = kc.shape[1]
    pos = lax.broadcasted_iota(jnp.int32, kc.shape, 1)
    nk_ref[...] = jnp.where(pos == w - 1, kn[:, None, :], pltpu.roll(kc, w - 1, 1))
    nv_ref[...] = jnp.where(pos == w - 1, vn[:, None, :], pltpu.roll(vc, w - 1, 1))
    gi = lax.broadcasted_iota(jnp.int32, (1, GROUP, 1), 1)
    for kv in range(N_KV_HEADS):
        sl = slice(kv * HEAD_DIM, (kv + 1) * HEAD_DIM)
        qg = q_ref[:, kv]
        s = jnp.einsum('bgd,bsd->bgs', qg, kc[:, :, sl].astype(BF16), preferred_element_type=F32)
        s = s + bias_ref[kv][None]
        s_new = jnp.sum(qg.astype(F32) * kn[:, None, sl], axis=-1, keepdims=True) + bnew_ref[kv][None]
        sink = jnp.zeros((1, GROUP, 1), F32)
        for g in range(GROUP):
            sink = jnp.where(gi == g, sink_ref[kv * GROUP + g], sink)
        m = jnp.maximum(jnp.maximum(jnp.max(s, axis=-1, keepdims=True), s_new), sink)
        p = jnp.exp(s - m)
        p_new = jnp.exp(s_new - m)
        denom = jnp.sum(p, axis=-1, keepdims=True) + p_new + jnp.exp(sink - m)
        o = jnp.einsum('bgs,bsd->bgd', p.astype(BF16), vc[:, :, sl].astype(BF16), preferred_element_type=F32)
        o = o + p_new * vn[:, None, sl]
        o_ref[:, kv] = o / denom


def _attn_sample(q, k_new, v_new, cache_k, cache_v, sinks, rel_table, *, tile=16):
    bd, w = cache_k.shape[:2]
    rel = w - jnp.arange(w, dtype=jnp.int32)
    bias = _bias_table(rel, rel_table).reshape(N_KV_HEADS, GROUP, w)
    bnew = _bias_table(jnp.zeros((1,), jnp.int32), rel_table).reshape(N_KV_HEADS, GROUP, 1)
    q4 = q.reshape(bd, N_KV_HEADS, GROUP, HEAD_DIM)
    spec4 = pl.BlockSpec((tile, N_KV_HEADS, GROUP, HEAD_DIM), lambda i, *_: (i, 0, 0, 0))
    cache = pl.BlockSpec((tile, w, KV_WIDTH), lambda i, *_: (i, 0, 0))
    new = pl.BlockSpec((tile, KV_WIDTH), lambda i, *_: (i, 0))
    const3 = lambda a: pl.BlockSpec(a.shape, lambda i, *_: (0, 0, 0))
    o, nk, nv = pl.pallas_call(
        _attn_sample_kernel,
        grid_spec=pltpu.PrefetchScalarGridSpec(
            num_scalar_prefetch=1,
            grid=(bd // tile,),
            in_specs=[spec4, cache, cache, new, new, const3(bias), const3(bnew)],
            out_specs=[spec4, cache, cache]),
        out_shape=[jax.ShapeDtypeStruct(q4.shape, F32),
                   jax.ShapeDtypeStruct(cache_k.shape, F32),
                   jax.ShapeDtypeStruct(cache_v.shape, F32)],
        compiler_params=_cparams(("parallel",)),
        name="attn_sample",
    )(sinks.astype(F32), q4, cache_k, cache_v, k_new, v_new, bias, bnew)
    return o.reshape(bd, ATTN_WIDTH), nk, nv


def _pool_project(d_groups, wp_ref, ps_ref):
    out = [_bdot(d, wp_ref[g]) for g, d in enumerate(d_groups)]
    return (jnp.concatenate(out, axis=-1) * ps_ref[...]).astype(BF16)


def _pool_prompt_kernel(u_ref, halo_ref, wp_ref, ps_ref, o_ref, ext):
    t = pl.program_id(1)
    tile = u_ref.shape[0]
    halo = halo_ref[...]
    ext[0:2 * SUBLANES, :] = jnp.where(t == 0, jnp.zeros_like(halo), halo)
    ext[2 * SUBLANES:, :] = u_ref[...]
    pos = t * tile + lax.broadcasted_iota(jnp.int32, (tile, 1), 0)
    base = 2 * SUBLANES
    ds = []
    for g, w in enumerate(POOL_WINDOWS):
        sl = slice(g * POOL_GROUP, (g + 1) * POOL_GROUP)
        acc = ext[base:base + tile, sl]
        for j in range(1, w):
            acc = acc + ext[base - j:base - j + tile, sl]
        cnt = jnp.minimum(pos + 1, w).astype(F32)
        ds.append(acc / cnt - ext[base:base + tile, sl])
    o_ref[...] = _pool_project(ds, wp_ref, ps_ref)


def _pool_prompt(u, w_pool, pool_scale, *, tile=512):
    b, s, c = u.shape
    hb = 2 * SUBLANES
    return pl.pallas_call(
        _pool_prompt_kernel,
        grid=(b, s // tile),
        in_specs=[pl.BlockSpec((None, tile, c), lambda i, t: (i, t, 0)),
                  pl.BlockSpec((None, hb, c), lambda i, t: (i, jnp.maximum(t * (tile // hb) - 1, 0), 0)),
                  pl.BlockSpec(w_pool.shape, lambda i, t: (0, 0, 0)),
                  pl.BlockSpec((1, c), lambda i, t: (0, 0))],
        out_specs=pl.BlockSpec((None, tile, c), lambda i, t: (i, t, 0)),
        out_shape=jax.ShapeDtypeStruct((b, s, c), BF16),
        scratch_shapes=[pltpu.VMEM((tile + hb, c), F32)],
        compiler_params=_cparams(("parallel", "parallel")),
        name="pool_prompt",
    )(u, u, w_pool, pool_scale.reshape(1, c))


def _pool_sample_kernel(st_ref, u_ref, wp_ref, ps_ref, o_ref, ns_ref):
    u = u_ref[...]
    ns_ref[0:POOL_BUF - 1] = st_ref[1:POOL_BUF]
    ns_ref[POOL_BUF - 1] = u
    ds = []
    for g, w in enumerate(POOL_WINDOWS):
        sl = slice(g * POOL_GROUP, (g + 1) * POOL_GROUP)
        acc = u[:, sl]
        for j in range(1, w):
            acc = acc + st_ref[POOL_BUF - j][:, sl]
        cnt = float(min(PAST_LEN + 1, w))
        ds.append(acc / cnt - u[:, sl])
    o_ref[...] = _pool_project(ds, wp_ref, ps_ref)


def _pool_sample(state_t, u, w_pool, pool_scale):
    nb, bd, c = state_t.shape
    full = lambda a: pl.BlockSpec(a.shape, lambda: (0,) * a.ndim)
    ps = pool_scale.reshape(1, c)
    return pl.pallas_call(
        _pool_sample_kernel,
        in_specs=[full(state_t), full(u), full(w_pool), full(ps)],
        out_specs=[pl.BlockSpec((bd, c), lambda: (0, 0)), full(state_t)],
        out_shape=[jax.ShapeDtypeStruct((bd, c), BF16), jax.ShapeDtypeStruct(state_t.shape, F32)],
        compiler_params=pltpu.CompilerParams(vmem_limit_bytes=VMEM_LIMIT),
        name="pool_sample",
    )(state_t, u, w_pool, ps)


def _route_kernel(x_ref, attn_ref, pool_ref, gm_ref, sh_ref, sc_ref, wo_ref, nf_ref, wr_ref, br_ref,
                  tri_t_ref, tri_e_ref,
                  x2_ref, hs_ref, slot_ref, cnt_ref):
    sub = slot_ref.shape[0]
    n = x_ref.shape[0]
    tile = n // sub
    mix = (jnp.dot(attn_ref[...], wo_ref[:ATTN_WIDTH, :], preferred_element_type=F32)
           + jnp.dot(pool_ref[...], wo_ref[ATTN_WIDTH:, :], preferred_element_type=F32))
    x2 = x_ref[...] + gm_ref[...] * mix
    x2_ref[...] = x2
    h = _rms(x2, nf_ref[...]) * (1.0 + sc_ref[...]) + sh_ref[...]

    hh, hl = _split(h)
    wh, wl = _split(wr_ref[...])
    nt = functools.partial(lax.dot_general, dimension_numbers=(((1,), (1,)), ((), ())),
                           preferred_element_type=F32)
    logits = nt(wh, hh) + nt(wl, hh) + nt(wh, hl) + br_ref[...]

    eidx = lax.broadcasted_iota(jnp.int32, (N_EXPERTS, n), 0).astype(F32)
    work = logits
    tops, picks = [], []
    for _ in range(TOP_K):
        m = jnp.max(work, axis=0, keepdims=True)
        pick = jnp.min(jnp.where(work == m, eidx, float(N_EXPERTS)), axis=0, keepdims=True)
        work = jnp.where(eidx == pick, -jnp.inf, work)
        tops.append(m)
        picks.append(pick)
    ex = [jnp.exp(v - tops[0]) for v in tops]
    den = ex[0] + ex[1] + ex[2] + ex[3]
    gates = [e / den for e in ex]

    sel = jnp.zeros((N_EXPERTS, n), F32)
    for pick in picks:
        sel = sel + (eidx == pick).astype(F32)
    rank = jnp.dot(sel.astype(BF16), tri_t_ref[...], preferred_element_type=F32)
    cnts = [jnp.sum(sel[:, t * tile:(t + 1) * tile], axis=1, keepdims=True) for t in range(sub)]
    padded = jnp.concatenate(
        [jnp.broadcast_to(jnp.ceil(c * (1.0 / CHUNK)) * CHUNK, (N_EXPERTS, LANES)) for c in cnts], axis=1)
    seg = jnp.dot(tri_e_ref[...], padded.astype(BF16), preferred_element_type=F32)
    dest = jnp.concatenate([seg[:, t * LANES:t * LANES + 1] + rank[:, t * tile:(t + 1) * tile]
                            for t in range(sub)], axis=1)
    slots = [jnp.sum(jnp.where(eidx == pick, dest, 0.0), axis=0, keepdims=True) for pick in picks]

    ridx = lax.broadcasted_iota(jnp.int32, (TILE_ROWS, tile), 0).astype(F32)
    for t in range(sub):
        cols = slice(t * tile, (t + 1) * tile)
        cnt_ref[t] = jnp.broadcast_to(cnts[t], (N_EXPERTS, LANES))
        slot_ref[t] = jnp.concatenate([v[:, cols] for v in slots + gates], axis=0)
        hit = ridx == slots[0][:, cols]
        for s in slots[1:]:
            hit = jnp.logical_or(hit, ridx == s[:, cols])
        perm = jnp.where(hit, 1.0, 0.0).astype(BF16)
        hs_ref[pl.ds(t * TILE_ROWS, TILE_ROWS), :] = _pack_rows(
            jnp.dot(perm, hh[t * tile:(t + 1) * tile, :], preferred_element_type=F32))


def _route(x2d, attn, pool, gm, sh, sc, w_out, norm_ffn, w_router_t, b_router, hs_prev, *, tile, sub,
           rows_per_mod, tile0, extra_tiles):
    n = x2d.shape[0]
    nt = n // tile
    assert nt % sub == 0 and extra_tiles % sub == 0
    own_steps = nt // sub
    steps = own_steps + extra_tiles // sub
    mrows = gm.shape[1]
    assert mrows == 1 or (sub == 1 and mrows == tile)
    last = lambda i: jnp.minimum(i, own_steps - 1)
    mod_spec = pl.BlockSpec((None, mrows, D_MODEL), lambda i: ((last(i) * sub * tile) // rows_per_mod, 0, 0))
    const = lambda shape: pl.BlockSpec(shape, lambda i: (0,) * len(shape))
    row = lambda w: pl.BlockSpec((sub * tile, w), lambda i: (last(i), 0))
    tok = jnp.arange(sub * tile)
    tri_t = jnp.logical_and(tok[:, None] < tok[None, :],
                            tok[:, None] // tile == tok[None, :] // tile).astype(BF16)
    tri_e = (jnp.arange(N_EXPERTS)[None, :] < jnp.arange(N_EXPERTS)[:, None]).astype(BF16)
    in_specs = [row(D_MODEL), row(ATTN_WIDTH), row(POOL_WIDTH), mod_spec, mod_spec, mod_spec,
                const((D_MODEL, D_MODEL)), const((1, D_MODEL)), const((N_EXPERTS, D_MODEL)),
                const((N_EXPERTS, 1)), const((sub * tile, sub * tile)), const((N_EXPERTS, N_EXPERTS))]
    args = [x2d, attn, pool, gm, sh, sc, w_out, norm_ffn.reshape(1, D_MODEL), w_router_t,
            b_router.reshape(N_EXPERTS, 1), tri_t, tri_e]
    n_in = len(args)
    kern = _route_kernel
    aliases = {}
    hs_rows = (tile0 + steps * sub) * TILE_ROWS
    assert tile0 % sub == 0
    if hs_prev is not None:
        in_specs.append(pl.BlockSpec(memory_space=pl.ANY))
        args.append(hs_prev)
        aliases = {n_in: 1}
        kern = lambda *refs: _route_kernel(*refs[:n_in], *refs[n_in + 1:])
        hs_rows = hs_prev.shape[0]
    return pl.pallas_call(
        kern,
        grid=(steps,),
        in_specs=in_specs,
        out_specs=[row(D_MODEL),
                   pl.BlockSpec((sub * TILE_ROWS, ROW_W), lambda i: (i + tile0 // sub, 0)),
                   pl.BlockSpec((sub, 2 * TOP_K, tile), lambda i: (last(i), 0, 0)),
                   pl.BlockSpec((sub, N_EXPERTS, LANES), lambda i: (last(i), 0, 0))],
        out_shape=[jax.ShapeDtypeStruct((n, D_MODEL), F32),
                   jax.ShapeDtypeStruct((hs_rows, ROW_W), jnp.int32),
                   jax.ShapeDtypeStruct((nt, 2 * TOP_K, tile), F32),
                   jax.ShapeDtypeStruct((nt, N_EXPERTS, LANES), F32)],
        input_output_aliases=aliases,
        compiler_params=_cparams(("arbitrary",)),
        name="route",
    )(*args)


def _moe_kernel(src_ref, be_ref, nxt_ref, nb_ref, hs_hbm, w1_hbm, b1_ref, w2_hbm, b2_ref, out_hbm,
                lhs, obuf, w1s, w2s, w1c, w2c, sem_in, sem_out, sem_w, *, scratch_chunk):
    b = pl.program_id(0)
    nb = nb_ref[0]
    slot = b % 2

    def weight_copies(e):
        return (pltpu.make_async_copy(w1_hbm.at[e], w1s, sem_w.at[0]),
                pltpu.make_async_copy(w2_hbm.at[e], w2s, sem_w.at[1]))

    def chunk_rows(c):
        return pl.ds(pl.multiple_of(c * CHUNK, CHUNK), CHUNK)

    def start_in(blk, s):
        for j in range(BLOCK_CHUNKS):
            c = src_ref[blk * BLOCK_CHUNKS + j]
            c = jnp.where(c < 0, ZERO_CHUNK, c)
            pltpu.make_async_copy(hs_hbm.at[chunk_rows(c)], lhs.at[s, pl.ds(j * CHUNK, CHUNK)],
                                  sem_in.at[s]).start()

    def wait_in(s):
        pltpu.make_async_copy(hs_hbm.at[pl.ds(0, MOE_BLOCK)], lhs.at[s], sem_in.at[s]).wait()

    def start_out(blk, s):
        for j in range(BLOCK_CHUNKS):
            c = src_ref[blk * BLOCK_CHUNKS + j]
            c = jnp.where(c < 0, scratch_chunk + s * BLOCK_CHUNKS + j, c)
            pltpu.make_async_copy(obuf.at[s, pl.ds(j * CHUNK, CHUNK)],
                                  out_hbm.at[chunk_rows(c), pl.ds(0, PACKED_W)], sem_out.at[s]).start()

    def wait_out(s):
        pltpu.make_async_copy(obuf.at[s], out_hbm.at[pl.ds(0, MOE_BLOCK), pl.ds(0, PACKED_W)],
                              sem_out.at[s]).wait()

    @pl.when(b < nb)
    def _():
        @pl.when(b == 0)
        def _():
            start_in(0, 0)
            for cp in weight_copies(be_ref[0]):
                cp.start()

        @pl.when(jnp.logical_or(b == 0, be_ref[b] != be_ref[jnp.maximum(b - 1, 0)]))
        def _():
            for cp in weight_copies(be_ref[b]):
                cp.wait()
            w1c[...] = w1s[...].astype(BF16)
            w2c[...] = w2s[...].astype(BF16)
            nxt = nxt_ref[b]

            @pl.when(nxt >= 0)
            def _():
                for cp in weight_copies(nxt):
                    cp.start()

        wait_in(slot)

        @pl.when(b >= 2)
        def _():
            wait_out(slot)

        start_in(b + 1, 1 - slot)

        xh, xl = _unpack_rows(lhs[slot])
        gu = (jnp.dot(xh, w1c[:PACKED_W, :], preferred_element_type=F32)
              + jnp.dot(xl, w1c[PACKED_W:, :], preferred_element_type=F32) + b1_ref[...])
        gate = jnp.minimum(gu[:, :D_FF], SWIGLU_LIMIT)
        up = jnp.clip(gu[:, D_FF:], -SWIGLU_LIMIT, SWIGLU_LIMIT)
        act = (up + 1.0) * (gate * jax.nn.sigmoid(SWIGLU_ALPHA * gate))
        y = jnp.dot(act.astype(BF16), w2c[...], preferred_element_type=F32) + b2_ref[...]
        obuf[slot] = _pack_rows(y.astype(BF16).astype(F32))

        start_out(b, slot)

        @pl.when(b == nb - 1)
        def _():
            wait_in(1 - slot)

            @pl.when(b >= 1)
            def _():
                wait_out(1 - slot)
            wait_out(slot)


def _moe(hs, src, block_e, next_e, nblocks, w1, b1, w2, b2, n_blocks_max, scratch_chunk):
    bspec = lambda w: pl.BlockSpec((None, 1, w), lambda b, src, be, nxt, nb: (be[b], 0, 0))
    hbm = pl.BlockSpec(memory_space=pl.ANY)
    return pl.pallas_call(
        functools.partial(_moe_kernel, scratch_chunk=scratch_chunk),
        grid_spec=pltpu.PrefetchScalarGridSpec(
            num_scalar_prefetch=4,
            grid=(n_blocks_max,),
            in_specs=[hbm, hbm, bspec(2 * D_FF), hbm, bspec(D_MODEL)],
            out_specs=hbm,
            scratch_shapes=[pltpu.VMEM((2, MOE_BLOCK, ROW_W), jnp.int32),
                            pltpu.VMEM((2, MOE_BLOCK, PACKED_W), jnp.int32),
                            pltpu.VMEM((D_MODEL, 2 * D_FF), F32),
                            pltpu.VMEM((D_FF, D_MODEL), F32),
                            pltpu.VMEM((D_MODEL, 2 * D_FF), BF16),
                            pltpu.VMEM((D_FF, D_MODEL), BF16),
                            pltpu.SemaphoreType.DMA((2,)),
                            pltpu.SemaphoreType.DMA((2,)),
                            pltpu.SemaphoreType.DMA((2,))]),
        out_shape=jax.ShapeDtypeStruct(hs.shape, jnp.int32),
        input_output_aliases={4: 0},
        compiler_params=_cparams(("arbitrary",)),
        name="moe_experts",
    )(src, block_e, next_e, nblocks, hs, w1, b1.reshape(N_EXPERTS, 1, 2 * D_FF), w2,
      b2.reshape(N_EXPERTS, 1, D_MODEL))


def _plan(cnt, n_blocks_max):
    nt = cnt.shape[0]
    nch = (cnt + (CHUNK - 1)) // CHUNK
    lstart = jnp.cumsum(nch, axis=1) - nch
    ne = jnp.sum(nch, axis=0)
    nbe = (ne + (BLOCK_CHUNKS - 1)) // BLOCK_CHUNKS
    bend = jnp.cumsum(nbe)
    nblocks = bend[-1]
    gstart = (bend - nbe)[None, :] * BLOCK_CHUNKS + (jnp.cumsum(nch, axis=0) - nch)
    gs = gstart.T.reshape(-1)
    nc = nch.T.reshape(-1)
    s0 = (jnp.arange(nt, dtype=jnp.int32)[:, None] * TILE_CHUNKS + lstart).T.reshape(-1)
    c = jnp.arange((n_blocks_max + 1) * BLOCK_CHUNKS, dtype=jnp.int32)[:, None]
    inside = jnp.logical_and(c >= gs[None, :], c < (gs + nc)[None, :])
    src = jnp.sum(jnp.where(inside, (s0 - gs)[None, :] + c + 1, 0), axis=1) - 1
    blk = jnp.arange(n_blocks_max, dtype=jnp.int32)
    be = jnp.sum((blk[:, None] >= bend[None, :]).astype(jnp.int32), axis=1)
    be_last = jnp.sum((nblocks - 1 >= bend).astype(jnp.int32))
    be = jnp.minimum(be, be_last).astype(jnp.int32)
    eid = jnp.arange(N_EXPERTS, dtype=jnp.int32)
    later = jnp.logical_and(eid[None, :] > be[:, None], (nbe > 0)[None, :])
    nxt = jnp.min(jnp.where(later, eid[None, :], N_EXPERTS), axis=1)
    nxt = jnp.where(nxt == N_EXPERTS, -1, nxt).astype(jnp.int32)
    return src, be, nxt, nblocks.reshape(1).astype(jnp.int32)


def _combine_kernel(o_ref, slot_ref, x2_ref, gf_ref, y_ref):
    tile = x2_ref.shape[0]
    ridx = lax.broadcasted_iota(jnp.int32, (TILE_ROWS, tile), 0).astype(F32)
    gmat = jnp.zeros((TILE_ROWS, tile), F32)
    for k in range(TOP_K):
        gmat = jnp.where(ridx == slot_ref[k:k + 1, :], slot_ref[TOP_K + k:TOP_K + k + 1, :], gmat)
    gb = gmat.astype(BF16)
    tn = functools.partial(lax.dot_general, dimension_numbers=(((0,), (0,)), ((), ())),
                           preferred_element_type=F32)
    oh, ol = _unpack_rows(o_ref[...])
    y = jnp.concatenate([tn(gb, oh), tn(gb, ol)], axis=1)
    y_ref[...] = x2_ref[...] + gf_ref[...] * y


def _combine(outs, slots, x2, gf, *, tile, rows_per_mod, tile0):
    n = x2.shape[0]
    mrows = gf.shape[1]
    return pl.pallas_call(
        _combine_kernel,
        grid=(n // tile,),
        in_specs=[pl.BlockSpec((TILE_ROWS, PACKED_W), lambda i: (i + tile0, 0)),
                  pl.BlockSpec((None, 2 * TOP_K, tile), lambda i: (i, 0, 0)),
                  pl.BlockSpec((tile, D_MODEL), lambda i: (i, 0)),
                  pl.BlockSpec((None, mrows, D_MODEL), lambda i: ((i * tile) // rows_per_mod, 0, 0))],
        out_specs=pl.BlockSpec((tile, D_MODEL), lambda i: (i, 0)),
        out_shape=jax.ShapeDtypeStruct((n, D_MODEL), F32),
        compiler_params=_cparams(("parallel",)),
        name="combine",
    )(outs, slots, x2, gf)


def kernel(x_prompt, x_sample, cache_k, cache_v, state_pool, c_prompt, c_sample, rel_bias, norm_mix, w_ada,
           b_ada, w_in, q_norm, k_norm, sinks, w_pool, pool_scale, w_out, norm_ffn, w_router, b_router,
           w1, b1, w2, b2):
    depth = w_in.shape[0]
    assert depth == 1
    l = 0
    bp, sp, _ = x_prompt.shape
    bs = x_sample.shape[0]
    assert x_sample.shape[1] == 1 and sp % SORT_TILE == 0 and bs <= SORT_TILE
    n_p = bp * sp
    tiles_p = n_p // SORT_TILE
    max_chunks = tiles_p * (SORT_TILE * TOP_K // CHUNK + N_EXPERTS) + (bs * TOP_K // CHUNK + N_EXPERTS)
    n_blocks_max = -(-max_chunks // BLOCK_CHUNKS) + N_EXPERTS

    mod = _modulation(jnp.concatenate([c_prompt, c_sample], axis=0), w_ada[l], b_ada[l])
    mod_p = [m.reshape(bp, 1, D_MODEL) for m in jnp.split(mod[:bp], 6, axis=-1)]
    mod_s = [m.reshape(1, bs, D_MODEL) for m in jnp.split(mod[bp:], 6, axis=-1)]

    head = jnp.arange(ATTN_WIDTH) // HEAD_DIM
    bd = (head[:, None] == head[None, :]).astype(BF16)
    w_in_b = w_in[l].astype(BF16)
    w_out_b = w_out[l].astype(BF16)
    w_pool_b = w_pool[l].astype(BF16)
    w_router_t = w_router[l].T

    xp = x_prompt.reshape(n_p, D_MODEL)
    q, k, v, u = _mixer_inputs(xp, mod_p[0], mod_p[1], norm_mix[l], w_in_b, bd, q_norm[l], k_norm[l],
                               tile=512, rows_per_mod=sp, precise=False)
    k3 = k.reshape(bp, sp, KV_WIDTH)
    v3 = v.reshape(bp, sp, KV_WIDTH)
    u3 = u.reshape(bp, sp, POOL_WIDTH)
    attn = _attn_prompt(q.reshape(bp, sp, ATTN_WIDTH), k3, v3, sinks[l], rel_bias)
    pool = _pool_prompt(u3, w_pool_b, pool_scale[l])
    keep = min(WINDOW, sp)
    nkp = k3[:, -keep:].reshape(bp, keep, N_KV_HEADS, HEAD_DIM)
    nvp = v3[:, -keep:].reshape(bp, keep, N_KV_HEADS, HEAD_DIM)
    if sp >= POOL_BUF:
        npp = u3[:, -POOL_BUF:]
    else:
        npp = jnp.concatenate([jnp.zeros((bp, POOL_BUF - sp, POOL_WIDTH), F32), u3], axis=1)
    x2_p, hs, slots_p, cnt_p = _route(
        xp, attn.reshape(n_p, ATTN_WIDTH), pool.reshape(n_p, POOL_WIDTH), mod_p[2], mod_p[3], mod_p[4],
        w_out_b, norm_ffn[l], w_router_t, b_router[l], None,
        tile=SORT_TILE, sub=ROUTE_SUB, rows_per_mod=sp, tile0=0, extra_tiles=max(2, ROUTE_SUB))

    xs = x_sample.reshape(bs, D_MODEL)
    qs, ks, vs, us = _mixer_inputs(xs, mod_s[0], mod_s[1], norm_mix[l], w_in[l], bd, q_norm[l], k_norm[l],
                                   tile=bs, rows_per_mod=bs, precise=True)
    wbuf = cache_k.shape[2]
    attn_s, nks, nvs = _attn_sample(qs, ks, vs, cache_k[l].reshape(bs, wbuf, KV_WIDTH),
                                    cache_v[l].reshape(bs, wbuf, KV_WIDTH), sinks[l], rel_bias)
    pool_s, nps_t = _pool_sample(jnp.swapaxes(state_pool[l], 0, 1), us, w_pool_b, pool_scale[l])
    x2_s, hs, slots_s, cnt_s = _route(
        xs, attn_s.astype(BF16), pool_s, mod_s[2], mod_s[3], mod_s[4], w_out_b, norm_ffn[l],
        w_router_t, b_router[l], hs, tile=bs, sub=1, rows_per_mod=bs, tile0=tiles_p, extra_tiles=0)

    cnt = jnp.concatenate([cnt_p[:, :, 0], cnt_s[:, :, 0]], axis=0).astype(jnp.int32)
    src, block_e, next_e, nblocks = _plan(cnt, n_blocks_max)
    outs = _moe(hs, src, block_e, next_e, nblocks, w1[l], b1[l], w2[l], b2[l], n_blocks_max,
                scratch_chunk=(tiles_p + 1) * TILE_CHUNKS)

    y_p = _combine(outs, slots_p, x2_p, mod_p[5], tile=SORT_TILE, rows_per_mod=sp, tile0=0)
    y_s = _combine(outs, slots_s, x2_s, mod_s[5], tile=bs, rows_per_mod=bs, tile0=tiles_p)

    return (y_p.reshape(bp, sp, D_MODEL), y_s.reshape(bs, 1, D_MODEL),
            nkp[None], nvp[None], npp[None],
            nks.reshape(1, bs, wbuf, N_KV_HEADS, HEAD_DIM), nvs.reshape(1, bs, wbuf, N_KV_HEADS, HEAD_DIM),
            jnp.swapaxes(nps_t, 0, 1)[None])
```

```python
import functools
import math

import jax
import jax.numpy as jnp
from jax import lax
from jax.experimental import pallas as pl
from jax.experimental.pallas import tpu as pltpu

F32 = jnp.float32
BF16 = jnp.bfloat16

D_MODEL = 1024
HEAD_DIM = 64
N_HEADS = 8
N_KV_HEADS = 2
GROUP = N_HEADS // N_KV_HEADS
ATTN_WIDTH = N_HEADS * HEAD_DIM
KV_WIDTH = N_KV_HEADS * HEAD_DIM
POOL_WIDTH = D_MODEL - ATTN_WIDTH
POOL_WINDOWS = (2, 4, 8, 16)
POOL_GROUP = POOL_WIDTH // len(POOL_WINDOWS)
POOL_BUF = max(POOL_WINDOWS) - 1
IN_WIDTH = ATTN_WIDTH + 2 * KV_WIDTH + POOL_WIDTH
WINDOW = 128
ATTN_BLOCK = 128
N_BUCKETS = 32
MAX_EXACT = 16
REL_MAX_DIST = 128
N_EXPERTS = 32
TOP_K = 4
D_FF = D_MODEL
SWIGLU_LIMIT = 7.0
SWIGLU_ALPHA = 1.702
EPS = 1e-6
NEG_INF = -1e30
PAST_LEN = 16384

LANES = 128
SUBLANES = 8
VMEM_LIMIT = 56 * 1024 * 1024

ATTN_QB = 2

SORT_TILE = 256
ROUTE_SUB = 4
CHUNK = SUBLANES
TILE_ROWS = -(-(SORT_TILE * TOP_K + N_EXPERTS * (CHUNK - 1)) // LANES) * LANES
TILE_CHUNKS = TILE_ROWS // CHUNK
MOE_BLOCK = 256
BLOCK_CHUNKS = MOE_BLOCK // CHUNK
PACKED_W = D_MODEL // 2
ROW_W = PACKED_W
ZERO_CHUNK = TILE_CHUNKS - 1


def _bdot(a, b):
    return jnp.dot(a.astype(BF16), b.astype(BF16), preferred_element_type=F32)


def _split(a):
    hi = a.astype(BF16)
    lo = (a - hi.astype(F32)).astype(BF16)
    return hi, lo


def _dot3(a, b):
    ah, al = _split(a)
    bh, bl = _split(b)
    d = functools.partial(jnp.dot, preferred_element_type=F32)
    return d(ah, bh) + d(al, bh) + d(ah, bl)


def _pack_rows(x):
    bits = lax.bitcast_convert_type(x, jnp.int32)
    return bits[:, :PACKED_W] | lax.shift_right_logical(bits[:, PACKED_W:], 16)


def _unpack_rows(w):
    hi = lax.bitcast_convert_type(w & jnp.int32(-65536), F32)
    lo = lax.bitcast_convert_type(lax.shift_left(w, 16), F32)
    return hi.astype(BF16), lo.astype(BF16)


def _rms(x, g):
    return x * lax.rsqrt(jnp.mean(x * x, axis=-1, keepdims=True) + EPS) * g


def _cparams(sem, **kw):
    return pltpu.CompilerParams(dimension_semantics=sem, vmem_limit_bytes=VMEM_LIMIT, **kw)


def _ada_kernel(c_ref, w_ref, b_ref, o_ref):
    c = c_ref[...]
    s = c * jax.nn.sigmoid(c)
    o_ref[...] = _dot3(s, w_ref[...]) + b_ref[...]


def _modulation(c, w_ada, b_ada):
    rows = c.shape[0]
    n = w_ada.shape[1]
    tn = 512
    return pl.pallas_call(
        _ada_kernel,
        grid=(n // tn,),
        in_specs=[pl.BlockSpec((rows, D_MODEL), lambda j: (0, 0)),
                  pl.BlockSpec((D_MODEL, tn), lambda j: (0, j)),
                  pl.BlockSpec((1, tn), lambda j: (0, j))],
        out_specs=pl.BlockSpec((rows, tn), lambda j: (0, j)),
        out_shape=jax.ShapeDtypeStruct((rows, n), F32),
        compiler_params=_cparams(("parallel",)),
        name="modulation",
    )(c, w_ada, b_ada.reshape(1, n))


def _head_rms(t, bd, g, precise):
    if precise:
        hi, lo = _split(t * t)
        ss = jnp.dot(hi, bd, preferred_element_type=F32) + jnp.dot(lo, bd, preferred_element_type=F32)
    else:
        ss = _bdot(t * t, bd)
    return t * lax.rsqrt(ss * (1.0 / HEAD_DIM) + EPS) * g


def _mixin_kernel(x_ref, sh_ref, sc_ref, g_ref, w_ref, bd_ref, qn_ref, kn_ref,
                  q_ref, k_ref, v_ref, u_ref, *, precise):
    h = _rms(x_ref[...], g_ref[...]) * (1.0 + sc_ref[...]) + sh_ref[...]
    z = _dot3(h, w_ref[...]) if precise else _bdot(h, w_ref[...])
    q = z[:, :ATTN_WIDTH]
    k = z[:, ATTN_WIDTH:ATTN_WIDTH + KV_WIDTH]
    bd = bd_ref[...]
    q = _head_rms(q, bd, qn_ref[...], precise)
    k = _head_rms(k, bd[:KV_WIDTH, :KV_WIDTH], kn_ref[...], precise)
    q_ref[...] = (q * (HEAD_DIM ** -0.5)).astype(BF16)
    k_ref[...] = k
    v_ref[...] = z[:, ATTN_WIDTH + KV_WIDTH:ATTN_WIDTH + 2 * KV_WIDTH]
    u_ref[...] = z[:, ATTN_WIDTH + 2 * KV_WIDTH:]


def _mixer_inputs(x2d, shift, scale, norm_mix, w_in, bd, q_norm, k_norm, *, tile, rows_per_mod, precise):
    n = x2d.shape[0]
    mrows = shift.shape[1]
    mod_spec = pl.BlockSpec((None, mrows, D_MODEL), lambda i: ((i * tile) // rows_per_mod, 0, 0))
    const = lambda shape: pl.BlockSpec(shape, lambda i: (0,) * len(shape))
    row = lambda w: pl.BlockSpec((tile, w), lambda i: (i, 0))
    return pl.pallas_call(
        functools.partial(_mixin_kernel, precise=precise),
        grid=(n // tile,),
        in_specs=[row(D_MODEL), mod_spec, mod_spec, const((1, D_MODEL)), const((D_MODEL, IN_WIDTH)),
                  const((ATTN_WIDTH, ATTN_WIDTH)), const((1, ATTN_WIDTH)), const((1, KV_WIDTH))],
        out_specs=[row(ATTN_WIDTH), row(KV_WIDTH), row(KV_WIDTH), row(POOL_WIDTH)],
        out_shape=[jax.ShapeDtypeStruct((n, ATTN_WIDTH), BF16),
                   jax.ShapeDtypeStruct((n, KV_WIDTH), F32),
                   jax.ShapeDtypeStruct((n, KV_WIDTH), F32),
                   jax.ShapeDtypeStruct((n, POOL_WIDTH), F32)],
        compiler_params=_cparams(("parallel",)),
        name="mixer_inputs",
    )(x2d, shift, scale, norm_mix.reshape(1, D_MODEL), w_in, bd,
      jnp.tile(q_norm, N_HEADS).reshape(1, ATTN_WIDTH), jnp.tile(k_norm, N_KV_HEADS).reshape(1, KV_WIDTH))


def _t5_bucket(rel):
    n = jnp.maximum(rel, 0)
    nf = jnp.maximum(n, 1).astype(F32)
    large = MAX_EXACT + (jnp.log(nf / MAX_EXACT) / math.log(REL_MAX_DIST / MAX_EXACT)
                         * (N_BUCKETS - MAX_EXACT)).astype(jnp.int32)
    large = jnp.minimum(large, N_BUCKETS - 1)
    return jnp.where(n < MAX_EXACT, n, large)


def _bias_table(rel, rel_table):
    bucket = _t5_bucket(rel)
    table = rel_table.astype(F32)
    ids = jnp.arange(N_BUCKETS, dtype=bucket.dtype).reshape((N_BUCKETS, 1) + (1,) * rel.ndim)
    onehot = bucket[None, None] == ids
    bias = jnp.sum(jnp.where(onehot, table.reshape(table.shape + (1,) * rel.ndim), 0.0), axis=0)
    valid = (rel >= 0) & (rel < WINDOW)
    return jnp.where(valid[None], bias, NEG_INF)


def _attn_prompt_kernel(sink_ref, q_ref, kp_ref, kc_ref, vp_ref, vc_ref, bias_ref, o_ref):
    first = pl.program_id(1) == 0
    kk = jnp.concatenate([kp_ref[...], kc_ref[...]], axis=0).astype(BF16)
    vv = jnp.concatenate([vp_ref[...], vc_ref[...]], axis=0).astype(BF16)
    key = lax.broadcasted_iota(jnp.int32, (2 * ATTN_BLOCK, 1), 0)
    no_prev = jnp.logical_and(first, key < ATTN_BLOCK)
    lane = lax.broadcasted_iota(jnp.int32, (1, GROUP * ATTN_BLOCK), 1)
    contract = lambda a, b, dims: lax.dot_general(a, b, (dims, ((), ())), preferred_element_type=F32)
    for i in range(ATTN_QB):
        q = q_ref[i * ATTN_BLOCK:(i + 1) * ATTN_BLOCK, :]
        halves = []
        for kv in range(N_KV_HEADS):
            heads = range(kv * GROUP, (kv + 1) * GROUP)
            qg = jnp.concatenate([q[:, h * HEAD_DIM:(h + 1) * HEAD_DIM] for h in heads], axis=0)
            kh = kk[i * ATTN_BLOCK:(i + 2) * ATTN_BLOCK, kv * HEAD_DIM:(kv + 1) * HEAD_DIM]
            vh = vv[i * ATTN_BLOCK:(i + 2) * ATTN_BLOCK, kv * HEAD_DIM:(kv + 1) * HEAD_DIM]
            s = contract(kh, qg, ((1,), (1,))) + bias_ref[kv]
            if i == 0:
                s = jnp.where(no_prev, NEG_INF, s)
            sink = jnp.zeros((1, GROUP * ATTN_BLOCK), F32)
            for g, h in enumerate(heads):
                sink = jnp.where(lane // ATTN_BLOCK == g, sink_ref[h], sink)
            m = jnp.maximum(jnp.max(s, axis=0, keepdims=True), sink)
            p = jnp.exp(s - m)
            denom = jnp.sum(p, axis=0, keepdims=True) + jnp.exp(sink - m)
            halves.append(contract(vh, p.astype(BF16), ((0,), (0,))) / denom)
        o_t = jnp.concatenate(halves, axis=0)
        per_g = [o_t[:, g * ATTN_BLOCK:(g + 1) * ATTN_BLOCK].T for g in range(GROUP)]
        out = [t[:, kv * HEAD_DIM:(kv + 1) * HEAD_DIM] for kv in range(N_KV_HEADS) for t in per_g]
        o_ref[i * ATTN_BLOCK:(i + 1) * ATTN_BLOCK, :] = jnp.concatenate(out, axis=-1).astype(BF16)


def _attn_prompt(q, k, v, sinks, rel_table):
    b, s = q.shape[:2]
    qrows = ATTN_QB * ATTN_BLOCK
    assert s % qrows == 0
    qi = jnp.arange(ATTN_BLOCK, dtype=jnp.int32)[:, None]
    si = jnp.arange(2 * ATTN_BLOCK, dtype=jnp.int32)[None, :]
    bias = _bias_table(qi + ATTN_BLOCK - si, rel_table)
    bias = jnp.swapaxes(bias.reshape(N_KV_HEADS, GROUP * ATTN_BLOCK, 2 * ATTN_BLOCK), 1, 2)
    cur = lambda w: pl.BlockSpec((None, qrows, w), lambda i, j, *_: (i, j, 0))
    prev = lambda w: pl.BlockSpec((None, ATTN_BLOCK, w),
                                  lambda i, j, *_: (i, jnp.maximum(j * ATTN_QB - 1, 0), 0))
    return pl.pallas_call(
        _attn_prompt_kernel,
        grid_spec=pltpu.PrefetchScalarGridSpec(
            num_scalar_prefetch=1,
            grid=(b, s // qrows),
            in_specs=[cur(ATTN_WIDTH), prev(KV_WIDTH), cur(KV_WIDTH), prev(KV_WIDTH), cur(KV_WIDTH),
                      pl.BlockSpec(bias.shape, lambda i, j, *_: (0, 0, 0))],
            out_specs=cur(ATTN_WIDTH)),
        out_shape=jax.ShapeDtypeStruct((b, s, ATTN_WIDTH), BF16),
        compiler_params=_cparams(("parallel", "parallel")),
        name="attn_prompt",
    )(sinks.astype(F32), q, k, k, v, v, bias)


def _attn_sample_kernel(sink_ref, q_ref, kc_ref, vc_ref, kn_ref, vn_ref, bias_ref, bnew_ref,
                        o_ref, nk_ref, nv_ref):
    kc = kc_ref[...]
    vc = vc_ref[...]
    kn = kn_ref[...]
    vn = vn_ref[...]
    w = kc.shape[1]
    pos = lax.broadcasted_iota(jnp.int32, kc.shape, 1)
    nk_ref[...] = jnp.where(pos == w - 1, kn[:, None, :], pltpu.roll(kc, w - 1, 1))
    nv_ref[...] = jnp.where(pos == w - 1, vn[:, None, :], pltpu.roll(vc, w - 1, 1))
    gi = lax.broadcasted_iota(jnp.int32, (1, GROUP, 1), 1)
    for kv in range(N_KV_HEADS):
        sl = slice(kv * HEAD_DIM, (kv + 1) * HEAD_DIM)
        qg = q_ref[:, kv]
        s = jnp.einsum('bgd,bsd->bgs', qg, kc[:, :, sl].astype(BF16), preferred_element_type=F32)
        s = s + bias_ref[kv][None]
        s_new = jnp.sum(qg.astype(F32) * kn[:, None, sl], axis=-1, keepdims=True) + bnew_ref[kv][None]
        sink = jnp.zeros((1, GROUP, 1), F32)
        for g in range(GROUP):
            sink = jnp.where(gi == g, sink_ref[kv * GROUP + g], sink)
        m = jnp.maximum(jnp.maximum(jnp.max(s, axis=-1, keepdims=True), s_new), sink)
        p = jnp.exp(s - m)
        p_new = jnp.exp(s_new - m)
        denom = jnp.sum(p, axis=-1, keepdims=True) + p_new + jnp.exp(sink - m)
        o = jnp.einsum('bgs,bsd->bgd', p.astype(BF16), vc[:, :, sl].astype(BF16), preferred_element_type=F32)
        o = o + p_new * vn[:, None, sl]
        o_ref[:, kv] = o / denom


def _attn_sample(q, k_new, v_new, cache_k, cache_v, sinks, rel_table, *, tile=16):
    bd, w = cache_k.shape[:2]
    rel = w - jnp.arange(w, dtype=jnp.int32)
    bias = _bias_table(rel, rel_table).reshape(N_KV_HEADS, GROUP, w)
    bnew = _bias_table(jnp.zeros((1,), jnp.int32), rel_table).reshape(N_KV_HEADS, GROUP, 1)
    q4 = q.reshape(bd, N_KV_HEADS, GROUP, HEAD_DIM)
    spec4 = pl.BlockSpec((tile, N_KV_HEADS, GROUP, HEAD_DIM), lambda i, *_: (i, 0, 0, 0))
    cache = pl.BlockSpec((tile, w, KV_WIDTH), lambda i, *_: (i, 0, 0))
    new = pl.BlockSpec((tile, KV_WIDTH), lambda i, *_: (i, 0))
    const3 = lambda a: pl.BlockSpec(a.shape, lambda i, *_: (0, 0, 0))
    o, nk, nv = pl.pallas_call(
        _attn_sample_kernel,
        grid_spec=pltpu.PrefetchScalarGridSpec(
            num_scalar_prefetch=1,
            grid=(bd // tile,),
            in_specs=[spec4, cache, cache, new, new, const3(bias), const3(bnew)],
            out_specs=[spec4, cache, cache]),
        out_shape=[jax.ShapeDtypeStruct(q4.shape, F32),
                   jax.ShapeDtypeStruct(cache_k.shape, F32),
                   jax.ShapeDtypeStruct(cache_v.shape, F32)],
        compiler_params=_cparams(("parallel",)),
        name="attn_sample",
    )(sinks.astype(F32), q4, cache_k, cache_v, k_new, v_new, bias, bnew)
    return o.reshape(bd, ATTN_WIDTH), nk, nv


def _pool_project(d_groups, wp_ref, ps_ref):
    out = [_bdot(d, wp_ref[g]) for g, d in enumerate(d_groups)]
    return (jnp.concatenate(out, axis=-1) * ps_ref[...]).astype(BF16)


def _pool_prompt_kernel(u_ref, halo_ref, wp_ref, ps_ref, o_ref, ext):
    t = pl.program_id(1)
    tile = u_ref.shape[0]
    halo = halo_ref[...]
    ext[0:2 * SUBLANES, :] = jnp.where(t == 0, jnp.zeros_like(halo), halo)
    ext[2 * SUBLANES:, :] = u_ref[...]
    pos = t * tile + lax.broadcasted_iota(jnp.int32, (tile, 1), 0)
    base = 2 * SUBLANES
    ds = []
    for g, w in enumerate(POOL_WINDOWS):
        sl = slice(g * POOL_GROUP, (g + 1) * POOL_GROUP)
        acc = ext[base:base + tile, sl]
        for j in range(1, w):
            acc = acc + ext[base - j:base - j + tile, sl]
        cnt = jnp.minimum(pos + 1, w).astype(F32)
        ds.append(acc / cnt - ext[base:base + tile, sl])
    o_ref[...] = _pool_project(ds, wp_ref, ps_ref)


def _pool_prompt(u, w_pool, pool_scale, *, tile=512):
    b, s, c = u.shape
    hb = 2 * SUBLANES
    return pl.pallas_call(
        _pool_prompt_kernel,
        grid=(b, s // tile),
        in_specs=[pl.BlockSpec((None, tile, c), lambda i, t: (i, t, 0)),
                  pl.BlockSpec((None, hb, c), lambda i, t: (i, jnp.maximum(t * (tile // hb) - 1, 0), 0)),
                  pl.BlockSpec(w_pool.shape, lambda i, t: (0, 0, 0)),
                  pl.BlockSpec((1, c), lambda i, t: (0, 0))],
        out_specs=pl.BlockSpec((None, tile, c), lambda i, t: (i, t, 0)),
        out_shape=jax.ShapeDtypeStruct((b, s, c), BF16),
        scratch_shapes=[pltpu.VMEM((tile + hb, c), F32)],
        compiler_params=_cparams(("parallel", "parallel")),
        name="pool_prompt",
    )(u, u, w_pool, pool_scale.reshape(1, c))


def _pool_sample_kernel(st_ref, u_ref, wp_ref, ps_ref, o_ref, ns_ref):
    u = u_ref[...]
    ns_ref[0:POOL_BUF - 1] = st_ref[1:POOL_BUF]
    ns_ref[POOL_BUF - 1] = u
    ds = []
    for g, w in enumerate(POOL_WINDOWS):
        sl = slice(g * POOL_GROUP, (g + 1) * POOL_GROUP)
        acc = u[:, sl]
        for j in range(1, w):
            acc = acc + st_ref[POOL_BUF - j][:, sl]
        cnt = float(min(PAST_LEN + 1, w))
        ds.append(acc / cnt - u[:, sl])
    o_ref[...] = _pool_project(ds, wp_ref, ps_ref)


def _pool_sample(state_t, u, w_pool, pool_scale):
    nb, bd, c = state_t.shape
    full = lambda a: pl.BlockSpec(a.shape, lambda: (0,) * a.ndim)
    ps = pool_scale.reshape(1, c)
    return pl.pallas_call(
        _pool_sample_kernel,
        in_specs=[full(state_t), full(u), full(w_pool), full(ps)],
        out_specs=[pl.BlockSpec((bd, c), lambda: (0, 0)), full(state_t)],
        out_shape=[jax.ShapeDtypeStruct((bd, c), BF16), jax.ShapeDtypeStruct(state_t.shape, F32)],
        compiler_params=pltpu.CompilerParams(vmem_limit_bytes=VMEM_LIMIT),
        name="pool_sample",
    )(state_t, u, w_pool, ps)


def _route_kernel(x_ref, attn_ref, pool_ref, gm_ref, sh_ref, sc_ref, wo_ref, nf_ref, wr_ref, br_ref,
                  tri_t_ref, tri_e_ref,
                  x2_ref, hs_ref, slot_ref, cnt_ref):
    sub = slot_ref.shape[0]
    n = x_ref.shape[0]
    tile = n // sub
    mix = (jnp.dot(attn_ref[...], wo_ref[:ATTN_WIDTH, :], preferred_element_type=F32)
           + jnp.dot(pool_ref[...], wo_ref[ATTN_WIDTH:, :], preferred_element_type=F32))
    x2 = x_ref[...] + gm_ref[...] * mix
    x2_ref[...] = x2
    h = _rms(x2, nf_ref[...]) * (1.0 + sc_ref[...]) + sh_ref[...]

    hh, hl = _split(h)
    wh, wl = _split(wr_ref[...])
    nt = functools.partial(lax.dot_general, dimension_numbers=(((1,), (1,)), ((), ())),
                           preferred_element_type=F32)
    logits = nt(wh, hh) + nt(wl, hh) + nt(wh, hl) + br_ref[...]

    eidx = lax.broadcasted_iota(jnp.int32, (N_EXPERTS, n), 0).astype(F32)
    work = logits
    tops, picks = [], []
    for _ in range(TOP_K):
        m = jnp.max(work, axis=0, keepdims=True)
        pick = jnp.min(jnp.where(work == m, eidx, float(N_EXPERTS)), axis=0, keepdims=True)
        work = jnp.where(eidx == pick, -jnp.inf, work)
        tops.append(m)
        picks.append(pick)
    ex = [jnp.exp(v - tops[0]) for v in tops]
    den = ex[0] + ex[1] + ex[2] + ex[3]
    gates = [e / den for e in ex]

    sel = jnp.zeros((N_EXPERTS, n), F32)
    for pick in picks:
        sel = sel + (eidx == pick).astype(F32)
    rank = jnp.dot(sel.astype(BF16), tri_t_ref[...], preferred_element_type=F32)
    cnts = [jnp.sum(sel[:, t * tile:(t + 1) * tile], axis=1, keepdims=True) for t in range(sub)]
    padded = jnp.concatenate(
        [jnp.broadcast_to(jnp.ceil(c * (1.0 / CHUNK)) * CHUNK, (N_EXPERTS, LANES)) for c in cnts], axis=1)
    seg = jnp.dot(tri_e_ref[...], padded.astype(BF16), preferred_element_type=F32)
    dest = jnp.concatenate([seg[:, t * LANES:t * LANES + 1] + rank[:, t * tile:(t + 1) * tile]
                            for t in range(sub)], axis=1)
    slots = [jnp.sum(jnp.where(eidx == pick, dest, 0.0), axis=0, keepdims=True) for pick in picks]

    ridx = lax.broadcasted_iota(jnp.int32, (TILE_ROWS, tile), 0).astype(F32)
    for t in range(sub):
        cols = slice(t * tile, (t + 1) * tile)
        cnt_ref[t] = jnp.broadcast_to(cnts[t], (N_EXPERTS, LANES))
        slot_ref[t] = jnp.concatenate([v[:, cols] for v in slots + gates], axis=0)
        hit = ridx == slots[0][:, cols]
        for s in slots[1:]:
            hit = jnp.logical_or(hit, ridx == s[:, cols])
        perm = jnp.where(hit, 1.0, 0.0).astype(BF16)
        hs_ref[pl.ds(t * TILE_ROWS, TILE_ROWS), :] = _pack_rows(
            jnp.dot(perm, hh[t * tile:(t + 1) * tile, :], preferred_element_type=F32))


def _route(x2d, attn, pool, gm, sh, sc, w_out, norm_ffn, w_router_t, b_router, hs_prev, *, tile, sub,
           rows_per_mod, tile0, extra_tiles):
    n = x2d.shape[0]
    nt = n // tile
    assert nt % sub == 0 and extra_tiles % sub == 0
    own_steps = nt // sub
    steps = own_steps + extra_tiles // sub
    mrows = gm.shape[1]
    assert mrows == 1 or (sub == 1 and mrows == tile)
    last = lambda i: jnp.minimum(i, own_steps - 1)
    mod_spec = pl.BlockSpec((None, mrows, D_MODEL), lambda i: ((last(i) * sub * tile) // rows_per_mod, 0, 0))
    const = lambda shape: pl.BlockSpec(shape, lambda i: (0,) * len(shape))
    row = lambda w: pl.BlockSpec((sub * tile, w), lambda i: (last(i), 0))
    tok = jnp.arange(sub * tile)
    tri_t = jnp.logical_and(tok[:, None] < tok[None, :],
                            tok[:, None] // tile == tok[None, :] // tile).astype(BF16)
    tri_e = (jnp.arange(N_EXPERTS)[None, :] < jnp.arange(N_EXPERTS)[:, None]).astype(BF16)
    in_specs = [row(D_MODEL), row(ATTN_WIDTH), row(POOL_WIDTH), mod_spec, mod_spec, mod_spec,
                const((D_MODEL, D_MODEL)), const((1, D_MODEL)), const((N_EXPERTS, D_MODEL)),
                const((N_EXPERTS, 1)), const((sub * tile, sub * tile)), const((N_EXPERTS, N_EXPERTS))]
    args = [x2d, attn, pool, gm, sh, sc, w_out, norm_ffn.reshape(1, D_MODEL), w_router_t,
            b_router.reshape(N_EXPERTS, 1), tri_t, tri_e]
    n_in = len(args)
    kern = _route_kernel
    aliases = {}
    hs_rows = (tile0 + steps * sub) * TILE_ROWS
    assert tile0 % sub == 0
    if hs_prev is not None:
        in_specs.append(pl.BlockSpec(memory_space=pl.ANY))
        args.append(hs_prev)
        aliases = {n_in: 1}
        kern = lambda *refs: _route_kernel(*refs[:n_in], *refs[n_in + 1:])
        hs_rows = hs_prev.shape[0]
    return pl.pallas_call(
        kern,
        grid=(steps,),
        in_specs=in_specs,
        out_specs=[row(D_MODEL),
                   pl.BlockSpec((sub * TILE_ROWS, ROW_W), lambda i: (i + tile0 // sub, 0)),
                   pl.BlockSpec((sub, 2 * TOP_K, tile), lambda i: (last(i), 0, 0)),
                   pl.BlockSpec((sub, N_EXPERTS, LANES), lambda i: (last(i), 0, 0))],
        out_shape=[jax.ShapeDtypeStruct((n, D_MODEL), F32),
                   jax.ShapeDtypeStruct((hs_rows, ROW_W), jnp.int32),
                   jax.ShapeDtypeStruct((nt, 2 * TOP_K, tile), F32),
                   jax.ShapeDtypeStruct((nt, N_EXPERTS, LANES), F32)],
        input_output_aliases=aliases,
        compiler_params=_cparams(("arbitrary",)),
        name="route",
    )(*args)


def _moe_kernel(src_ref, be_ref, nxt_ref, nb_ref, hs_hbm, w1_hbm, b1_ref, w2_hbm, b2_ref, out_hbm,
                lhs, obuf, w1s, w2s, w1c, w2c, sem_in, sem_out, sem_w, *, scratch_chunk):
    nb = nb_ref[0]

    def weight_copies(e):
        return (pltpu.make_async_copy(w1_hbm.at[e], w1s, sem_w.at[0]),
                pltpu.make_async_copy(w2_hbm.at[e], w2s, sem_w.at[1]))

    def chunk_rows(c):
        return pl.ds(pl.multiple_of(c * CHUNK, CHUNK), CHUNK)

    def start_in(blk, s):
        for j in range(BLOCK_CHUNKS):
            c = src_ref[blk * BLOCK_CHUNKS + j]
            c = jnp.where(c < 0, ZERO_CHUNK, c)
            pltpu.make_async_copy(hs_hbm.at[chunk_rows(c)], lhs.at[s, pl.ds(j * CHUNK, CHUNK)],
                                  sem_in.at[s]).start()

    def wait_in(s):
        pltpu.make_async_copy(hs_hbm.at[pl.ds(0, MOE_BLOCK)], lhs.at[s], sem_in.at[s]).wait()

    def start_out(blk, s):
        for j in range(BLOCK_CHUNKS):
            c = src_ref[blk * BLOCK_CHUNKS + j]
            c = jnp.where(c < 0, scratch_chunk + s * BLOCK_CHUNKS + j, c)
            pltpu.make_async_copy(obuf.at[s, pl.ds(j * CHUNK, CHUNK)],
                                  out_hbm.at[chunk_rows(c), pl.ds(0, PACKED_W)], sem_out.at[s]).start()

    def wait_out(s):
        pltpu.make_async_copy(obuf.at[s], out_hbm.at[pl.ds(0, MOE_BLOCK), pl.ds(0, PACKED_W)],
                              sem_out.at[s]).wait()

    @pl.when(nb > 0)
    def _():
        start_in(0, 0)
        for cp in weight_copies(be_ref[0]):
            cp.start()

    def block(b, carry):
        slot = b % 2
        e = be_ref[b]

        @pl.when(jnp.logical_or(b == 0, e != be_ref[jnp.maximum(b - 1, 0)]))
        def _():
            for cp in weight_copies(e):
                cp.wait()
            w1c[...] = w1s[...].astype(BF16)
            w2c[...] = w2s[...].astype(BF16)
            nxt = nxt_ref[b]

            @pl.when(nxt >= 0)
            def _():
                for cp in weight_copies(nxt):
                    cp.start()

        wait_in(slot)

        @pl.when(b >= 2)
        def _():
            wait_out(slot)

        start_in(b + 1, 1 - slot)

        xh, xl = _unpack_rows(lhs[slot])
        gu = (jnp.dot(xh, w1c[:PACKED_W, :], preferred_element_type=F32)
              + jnp.dot(xl, w1c[PACKED_W:, :], preferred_element_type=F32) + b1_ref[e])
        gate = jnp.minimum(gu[:, :D_FF], SWIGLU_LIMIT)
        up = jnp.clip(gu[:, D_FF:], -SWIGLU_LIMIT, SWIGLU_LIMIT)
        act = (up + 1.0) * (gate * jax.nn.sigmoid(SWIGLU_ALPHA * gate))
        y = jnp.dot(act.astype(BF16), w2c[...], preferred_element_type=F32) + b2_ref[e]
        obuf[slot] = _pack_rows(y.astype(BF16).astype(F32))

        start_out(b, slot)
        return carry

    lax.fori_loop(0, nb, block, 0)

    @pl.when(nb > 0)
    def _():
        last_slot = (nb - 1) % 2
        wait_in(1 - last_slot)

        @pl.when(nb >= 2)
        def _():
            wait_out(1 - last_slot)
        wait_out(last_slot)


def _moe(hs, src, block_e, next_e, nblocks, w1, b1, w2, b2, scratch_chunk):
    full = lambda shape: pl.BlockSpec(shape, lambda i, *_: (0,) * len(shape))
    hbm = pl.BlockSpec(memory_space=pl.ANY)
    return pl.pallas_call(
        functools.partial(_moe_kernel, scratch_chunk=scratch_chunk),
        grid_spec=pltpu.PrefetchScalarGridSpec(
            num_scalar_prefetch=4,
            grid=(1,),
            in_specs=[hbm, hbm, full((N_EXPERTS, 1, 2 * D_FF)), hbm, full((N_EXPERTS, 1, D_MODEL))],
            out_specs=hbm,
            scratch_shapes=[pltpu.VMEM((2, MOE_BLOCK, ROW_W), jnp.int32),
                            pltpu.VMEM((2, MOE_BLOCK, PACKED_W), jnp.int32),
                            pltpu.VMEM((D_MODEL, 2 * D_FF), F32),
                            pltpu.VMEM((D_FF, D_MODEL), F32),
                            pltpu.VMEM((D_MODEL, 2 * D_FF), BF16),
                            pltpu.VMEM((D_FF, D_MODEL), BF16),
                            pltpu.SemaphoreType.DMA((2,)),
                            pltpu.SemaphoreType.DMA((2,)),
                            pltpu.SemaphoreType.DMA((2,))]),
        out_shape=jax.ShapeDtypeStruct(hs.shape, jnp.int32),
        input_output_aliases={4: 0},
        compiler_params=_cparams(("arbitrary",)),
        name="moe_experts",
    )(src, block_e, next_e, nblocks, hs, w1, b1.reshape(N_EXPERTS, 1, 2 * D_FF), w2,
      b2.reshape(N_EXPERTS, 1, D_MODEL))


def _plan(cnt, n_blocks_max):
    nt = cnt.shape[0]
    nch = (cnt + (CHUNK - 1)) // CHUNK
    lstart = jnp.cumsum(nch, axis=1) - nch
    ne = jnp.sum(nch, axis=0)
    nbe = (ne + (BLOCK_CHUNKS - 1)) // BLOCK_CHUNKS
    bend = jnp.cumsum(nbe)
    nblocks = bend[-1]
    gstart = (bend - nbe)[None, :] * BLOCK_CHUNKS + (jnp.cumsum(nch, axis=0) - nch)
    gs = gstart.T.reshape(-1)
    nc = nch.T.reshape(-1)
    s0 = (jnp.arange(nt, dtype=jnp.int32)[:, None] * TILE_CHUNKS + lstart).T.reshape(-1)
    c = jnp.arange((n_blocks_max + 1) * BLOCK_CHUNKS, dtype=jnp.int32)[:, None]
    inside = jnp.logical_and(c >= gs[None, :], c < (gs + nc)[None, :])
    src = jnp.sum(jnp.where(inside, (s0 - gs)[None, :] + c + 1, 0), axis=1) - 1
    blk = jnp.arange(n_blocks_max, dtype=jnp.int32)
    be = jnp.sum((blk[:, None] >= bend[None, :]).astype(jnp.int32), axis=1)
    be_last = jnp.sum((nblocks - 1 >= bend).astype(jnp.int32))
    be = jnp.minimum(be, be_last).astype(jnp.int32)
    eid = jnp.arange(N_EXPERTS, dtype=jnp.int32)
    later = jnp.logical_and(eid[None, :] > be[:, None], (nbe > 0)[None, :])
    nxt = jnp.min(jnp.where(later, eid[None, :], N_EXPERTS), axis=1)
    nxt = jnp.where(nxt == N_EXPERTS, -1, nxt).astype(jnp.int32)
    return src, be, nxt, nblocks.reshape(1).astype(jnp.int32)


def _combine_kernel(o_ref, slot_ref, x2_ref, gf_ref, y_ref):
    tile = x2_ref.shape[0]
    ridx = lax.broadcasted_iota(jnp.int32, (TILE_ROWS, tile), 0).astype(F32)
    gmat = jnp.zeros((TILE_ROWS, tile), F32)
    for k in range(TOP_K):
        gmat = jnp.where(ridx == slot_ref[k:k + 1, :], slot_ref[TOP_K + k:TOP_K + k + 1, :], gmat)
    gb = gmat.astype(BF16)
    tn = functools.partial(lax.dot_general, dimension_numbers=(((0,), (0,)), ((), ())),
                           preferred_element_type=F32)
    oh, ol = _unpack_rows(o_ref[...])
    y = jnp.concatenate([tn(gb, oh), tn(gb, ol)], axis=1)
    y_ref[...] = x2_ref[...] + gf_ref[...] * y


def _combine(outs, slots, x2, gf, *, tile, rows_per_mod, tile0):
    n = x2.shape[0]
    mrows = gf.shape[1]
    return pl.pallas_call(
        _combine_kernel,
        grid=(n // tile,),
        in_specs=[pl.BlockSpec((TILE_ROWS, PACKED_W), lambda i: (i + tile0, 0)),
                  pl.BlockSpec((None, 2 * TOP_K, tile), lambda i: (i, 0, 0)),
                  pl.BlockSpec((tile, D_MODEL), lambda i: (i, 0)),
                  pl.BlockSpec((None, mrows, D_MODEL), lambda i: ((i * tile) // rows_per_mod, 0, 0))],
        out_specs=pl.BlockSpec((tile, D_MODEL), lambda i: (i, 0)),
        out_shape=jax.ShapeDtypeStruct((n, D_MODEL), F32),
        compiler_params=_cparams(("parallel",)),
        name="combine",
    )(outs, slots, x2, gf)


def kernel(x_prompt, x_sample, cache_k, cache_v, state_pool, c_prompt, c_sample, rel_bias, norm_mix, w_ada,
           b_ada, w_in, q_norm, k_norm, sinks, w_pool, pool_scale, w_out, norm_ffn, w_router, b_router,
           w1, b1, w2, b2):
    depth = w_in.shape[0]
    assert depth == 1
    l = 0
    bp, sp, _ = x_prompt.shape
    bs = x_sample.shape[0]
    assert x_sample.shape[1] == 1 and sp % SORT_TILE == 0 and bs <= SORT_TILE
    n_p = bp * sp
    tiles_p = n_p // SORT_TILE
    max_chunks = tiles_p * (SORT_TILE * TOP_K // CHUNK + N_EXPERTS) + (bs * TOP_K // CHUNK + N_EXPERTS)
    n_blocks_max = -(-max_chunks // BLOCK_CHUNKS) + N_EXPERTS

    mod = _modulation(jnp.concatenate([c_prompt, c_sample], axis=0), w_ada[l], b_ada[l])
    mod_p = [m.reshape(bp, 1, D_MODEL) for m in jnp.split(mod[:bp], 6, axis=-1)]
    mod_s = [m.reshape(1, bs, D_MODEL) for m in jnp.split(mod[bp:], 6, axis=-1)]

    head = jnp.arange(ATTN_WIDTH) // HEAD_DIM
    bd = (head[:, None] == head[None, :]).astype(BF16)
    w_in_b = w_in[l].astype(BF16)
    w_out_b = w_out[l].astype(BF16)
    w_pool_b = w_pool[l].astype(BF16)
    w_router_t = w_router[l].T

    xp = x_prompt.reshape(n_p, D_MODEL)
    q, k, v, u = _mixer_inputs(xp, mod_p[0], mod_p[1], norm_mix[l], w_in_b, bd, q_norm[l], k_norm[l],
                               tile=512, rows_per_mod=sp, precise=False)
    k3 = k.reshape(bp, sp, KV_WIDTH)
    v3 = v.reshape(bp, sp, KV_WIDTH)
    u3 = u.reshape(bp, sp, POOL_WIDTH)
    attn = _attn_prompt(q.reshape(bp, sp, ATTN_WIDTH), k3, v3, sinks[l], rel_bias)
    pool = _pool_prompt(u3, w_pool_b, pool_scale[l])
    keep = min(WINDOW, sp)
    nkp = k3[:, -keep:].reshape(bp, keep, N_KV_HEADS, HEAD_DIM)
    nvp = v3[:, -keep:].reshape(bp, keep, N_KV_HEADS, HEAD_DIM)
    if sp >= POOL_BUF:
        npp = u3[:, -POOL_BUF:]
    else:
        npp = jnp.concatenate([jnp.zeros((bp, POOL_BUF - sp, POOL_WIDTH), F32), u3], axis=1)
    x2_p, hs, slots_p, cnt_p = _route(
        xp, attn.reshape(n_p, ATTN_WIDTH), pool.reshape(n_p, POOL_WIDTH), mod_p[2], mod_p[3], mod_p[4],
        w_out_b, norm_ffn[l], w_router_t, b_router[l], None,
        tile=SORT_TILE, sub=ROUTE_SUB, rows_per_mod=sp, tile0=0, extra_tiles=max(2, ROUTE_SUB))

    xs = x_sample.reshape(bs, D_MODEL)
    qs, ks, vs, us = _mixer_inputs(xs, mod_s[0], mod_s[1], norm_mix[l], w_in[l], bd, q_norm[l], k_norm[l],
                                   tile=bs, rows_per_mod=bs, precise=True)
    wbuf = cache_k.shape[2]
    attn_s, nks, nvs = _attn_sample(qs, ks, vs, cache_k[l].reshape(bs, wbuf, KV_WIDTH),
                                    cache_v[l].reshape(bs, wbuf, KV_WIDTH), sinks[l], rel_bias)
    pool_s, nps_t = _pool_sample(jnp.swapaxes(state_pool[l], 0, 1), us, w_pool_b, pool_scale[l])
    x2_s, hs, slots_s, cnt_s = _route(
        xs, attn_s.astype(BF16), pool_s, mod_s[2], mod_s[3], mod_s[4], w_out_b, norm_ffn[l],
        w_router_t, b_router[l], hs, tile=bs, sub=1, rows_per_mod=bs, tile0=tiles_p, extra_tiles=0)

    cnt = jnp.concatenate([cnt_p[:, :, 0], cnt_s[:, :, 0]], axis=0).astype(jnp.int32)
    src, block_e, next_e, nblocks = _plan(cnt, n_blocks_max)
    outs = _moe(hs, src, block_e, next_e, nblocks, w1[l], b1[l], w2[l], b2[l],
                scratch_chunk=(tiles_p + 1) * TILE_CHUNKS)

    y_p = _combine(outs, slots_p, x2_p, mod_p[5], tile=SORT_TILE, rows_per_mod=sp, tile0=0)
    y_s = _combine(outs, slots_s, x2_s, mod_s[5], tile=bs, rows_per_mod=bs, tile0=tiles_p)

    return (y_p.reshape(bp, sp, D_MODEL), y_s.reshape(bs, 1, D_MODEL),
            nkp[None], nvp[None], npp[None],
            nks.reshape(1, bs, wbuf, N_KV_HEADS, HEAD_DIM), nvs.reshape(1, bs, wbuf, N_KV_HEADS, HEAD_DIM),
            jnp.swapaxes(nps_t, 0, 1)[None])
```

```python
import functools
import math

import jax
import jax.numpy as jnp
from jax import lax
from jax.experimental import pallas as pl
from jax.experimental.pallas import tpu as pltpu

F32 = jnp.float32
BF16 = jnp.bfloat16

D_MODEL = 1024
HEAD_DIM = 64
N_HEADS = 8
N_KV_HEADS = 2
GROUP = N_HEADS // N_KV_HEADS
ATTN_WIDTH = N_HEADS * HEAD_DIM
KV_WIDTH = N_KV_HEADS * HEAD_DIM
POOL_WIDTH = D_MODEL - ATTN_WIDTH
POOL_WINDOWS = (2, 4, 8, 16)
POOL_GROUP = POOL_WIDTH // len(POOL_WINDOWS)
POOL_BUF = max(POOL_WINDOWS) - 1
IN_WIDTH = ATTN_WIDTH + 2 * KV_WIDTH + POOL_WIDTH
WINDOW = 128
ATTN_BLOCK = 128
N_BUCKETS = 32
MAX_EXACT = 16
REL_MAX_DIST = 128
N_EXPERTS = 32
TOP_K = 4
D_FF = D_MODEL
SWIGLU_LIMIT = 7.0
SWIGLU_ALPHA = 1.702
EPS = 1e-6
NEG_INF = -1e30
PAST_LEN = 16384

LANES = 128
SUBLANES = 8
VMEM_LIMIT = 56 * 1024 * 1024

ATTN_QB = 2

SORT_TILE = 256
ROUTE_SUB = 4
COMBINE_SUB = 2
CHUNK = SUBLANES
TILE_ROWS = -(-(SORT_TILE * TOP_K + N_EXPERTS * (CHUNK - 1)) // LANES) * LANES
TILE_CHUNKS = TILE_ROWS // CHUNK
MOE_BLOCK = 256
BLOCK_CHUNKS = MOE_BLOCK // CHUNK
PACKED_W = D_MODEL // 2
ROW_W = PACKED_W
ZERO_CHUNK = TILE_CHUNKS - 1


def _bdot(a, b):
    return jnp.dot(a.astype(BF16), b.astype(BF16), preferred_element_type=F32)


def _split(a):
    hi = a.astype(BF16)
    lo = (a - hi.astype(F32)).astype(BF16)
    return hi, lo


def _dot3(a, b):
    ah, al = _split(a)
    bh, bl = _split(b)
    d = functools.partial(jnp.dot, preferred_element_type=F32)
    return d(ah, bh) + d(al, bh) + d(ah, bl)


def _pack_rows(x):
    bits = lax.bitcast_convert_type(x, jnp.int32)
    return bits[:, :PACKED_W] | lax.shift_right_logical(bits[:, PACKED_W:], 16)


def _unpack_rows(w):
    hi = lax.bitcast_convert_type(w & jnp.int32(-65536), F32)
    lo = lax.bitcast_convert_type(lax.shift_left(w, 16), F32)
    return hi.astype(BF16), lo.astype(BF16)


def _rms(x, g):
    return x * lax.rsqrt(jnp.mean(x * x, axis=-1, keepdims=True) + EPS) * g


def _cparams(sem, **kw):
    return pltpu.CompilerParams(dimension_semantics=sem, vmem_limit_bytes=VMEM_LIMIT, **kw)


def _ada_kernel(c_ref, w_ref, b_ref, o_ref):
    c = c_ref[...]
    s = c * jax.nn.sigmoid(c)
    o_ref[...] = _dot3(s, w_ref[...]) + b_ref[...]


def _modulation(c, w_ada, b_ada):
    rows = c.shape[0]
    n = w_ada.shape[1]
    tn = 512
    return pl.pallas_call(
        _ada_kernel,
        grid=(n // tn,),
        in_specs=[pl.BlockSpec((rows, D_MODEL), lambda j: (0, 0)),
                  pl.BlockSpec((D_MODEL, tn), lambda j: (0, j)),
                  pl.BlockSpec((1, tn), lambda j: (0, j))],
        out_specs=pl.BlockSpec((rows, tn), lambda j: (0, j)),
        out_shape=jax.ShapeDtypeStruct((rows, n), F32),
        compiler_params=_cparams(("parallel",)),
        name="modulation",
    )(c, w_ada, b_ada.reshape(1, n))


def _head_rms(t, bd, g, precise):
    if precise:
        hi, lo = _split(t * t)
        ss = jnp.dot(hi, bd, preferred_element_type=F32) + jnp.dot(lo, bd, preferred_element_type=F32)
    else:
        ss = _bdot(t * t, bd)
    return t * lax.rsqrt(ss * (1.0 / HEAD_DIM) + EPS) * g


def _mixin_kernel(x_ref, sh_ref, sc_ref, g_ref, w_ref, bd_ref, qn_ref, kn_ref,
                  q_ref, k_ref, v_ref, u_ref, *, precise):
    h = _rms(x_ref[...], g_ref[...]) * (1.0 + sc_ref[...]) + sh_ref[...]
    z = _dot3(h, w_ref[...]) if precise else _bdot(h, w_ref[...])
    q = z[:, :ATTN_WIDTH]
    k = z[:, ATTN_WIDTH:ATTN_WIDTH + KV_WIDTH]
    bd = bd_ref[...]
    q = _head_rms(q, bd, qn_ref[...], precise)
    k = _head_rms(k, bd[:KV_WIDTH, :KV_WIDTH], kn_ref[...], precise)
    q_ref[...] = (q * (HEAD_DIM ** -0.5)).astype(BF16)
    k_ref[...] = k
    v_ref[...] = z[:, ATTN_WIDTH + KV_WIDTH:ATTN_WIDTH + 2 * KV_WIDTH]
    u_ref[...] = z[:, ATTN_WIDTH + 2 * KV_WIDTH:]


def _mixer_inputs(x2d, shift, scale, norm_mix, w_in, bd, q_norm, k_norm, *, tile, rows_per_mod, precise):
    n = x2d.shape[0]
    mrows = shift.shape[1]
    mod_spec = pl.BlockSpec((None, mrows, D_MODEL), lambda i: ((i * tile) // rows_per_mod, 0, 0))
    const = lambda shape: pl.BlockSpec(shape, lambda i: (0,) * len(shape))
    row = lambda w: pl.BlockSpec((tile, w), lambda i: (i, 0))
    return pl.pallas_call(
        functools.partial(_mixin_kernel, precise=precise),
        grid=(n // tile,),
        in_specs=[row(D_MODEL), mod_spec, mod_spec, const((1, D_MODEL)), const((D_MODEL, IN_WIDTH)),
                  const((ATTN_WIDTH, ATTN_WIDTH)), const((1, ATTN_WIDTH)), const((1, KV_WIDTH))],
        out_specs=[row(ATTN_WIDTH), row(KV_WIDTH), row(KV_WIDTH), row(POOL_WIDTH)],
        out_shape=[jax.ShapeDtypeStruct((n, ATTN_WIDTH), BF16),
                   jax.ShapeDtypeStruct((n, KV_WIDTH), F32),
                   jax.ShapeDtypeStruct((n, KV_WIDTH), F32),
                   jax.ShapeDtypeStruct((n, POOL_WIDTH), F32)],
        compiler_params=_cparams(("parallel",)),
        name="mixer_inputs",
    )(x2d, shift, scale, norm_mix.reshape(1, D_MODEL), w_in, bd,
      jnp.tile(q_norm, N_HEADS).reshape(1, ATTN_WIDTH), jnp.tile(k_norm, N_KV_HEADS).reshape(1, KV_WIDTH))


def _t5_bucket(rel):
    n = jnp.maximum(rel, 0)
    nf = jnp.maximum(n, 1).astype(F32)
    large = MAX_EXACT + (jnp.log(nf / MAX_EXACT) / math.log(REL_MAX_DIST / MAX_EXACT)
                         * (N_BUCKETS - MAX_EXACT)).astype(jnp.int32)
    large = jnp.minimum(large, N_BUCKETS - 1)
    return jnp.where(n < MAX_EXACT, n, large)


def _bias_table(rel, rel_table):
    bucket = _t5_bucket(rel)
    table = rel_table.astype(F32)
    ids = jnp.arange(N_BUCKETS, dtype=bucket.dtype).reshape((N_BUCKETS, 1) + (1,) * rel.ndim)
    onehot = bucket[None, None] == ids
    bias = jnp.sum(jnp.where(onehot, table.reshape(table.shape + (1,) * rel.ndim), 0.0), axis=0)
    valid = (rel >= 0) & (rel < WINDOW)
    return jnp.where(valid[None], bias, NEG_INF)


def _attn_prompt_kernel(sink_ref, q_ref, kp_ref, kc_ref, vp_ref, vc_ref, bias_ref, o_ref):
    first = pl.program_id(1) == 0
    kk = jnp.concatenate([kp_ref[...], kc_ref[...]], axis=0).astype(BF16)
    vv = jnp.concatenate([vp_ref[...], vc_ref[...]], axis=0).astype(BF16)
    key = lax.broadcasted_iota(jnp.int32, (2 * ATTN_BLOCK, 1), 0)
    no_prev = jnp.logical_and(first, key < ATTN_BLOCK)
    lane = lax.broadcasted_iota(jnp.int32, (1, GROUP * ATTN_BLOCK), 1)
    contract = lambda a, b, dims: lax.dot_general(a, b, (dims, ((), ())), preferred_element_type=F32)
    for i in range(ATTN_QB):
        q = q_ref[i * ATTN_BLOCK:(i + 1) * ATTN_BLOCK, :]
        halves = []
        for kv in range(N_KV_HEADS):
            heads = range(kv * GROUP, (kv + 1) * GROUP)
            qg = jnp.concatenate([q[:, h * HEAD_DIM:(h + 1) * HEAD_DIM] for h in heads], axis=0)
            kh = kk[i * ATTN_BLOCK:(i + 2) * ATTN_BLOCK, kv * HEAD_DIM:(kv + 1) * HEAD_DIM]
            vh = vv[i * ATTN_BLOCK:(i + 2) * ATTN_BLOCK, kv * HEAD_DIM:(kv + 1) * HEAD_DIM]
            s = contract(kh, qg, ((1,), (1,))) + bias_ref[kv]
            if i == 0:
                s = jnp.where(no_prev, NEG_INF, s)
            sink = jnp.zeros((1, GROUP * ATTN_BLOCK), F32)
            for g, h in enumerate(heads):
                sink = jnp.where(lane // ATTN_BLOCK == g, sink_ref[h], sink)
            m = jnp.maximum(jnp.max(s, axis=0, keepdims=True), sink)
            p = jnp.exp(s - m)
            denom = jnp.sum(p, axis=0, keepdims=True) + jnp.exp(sink - m)
            halves.append(contract(vh, p.astype(BF16), ((0,), (0,))) / denom)
        o_t = jnp.concatenate(halves, axis=0)
        per_g = [o_t[:, g * ATTN_BLOCK:(g + 1) * ATTN_BLOCK].T for g in range(GROUP)]
        out = [t[:, kv * HEAD_DIM:(kv + 1) * HEAD_DIM] for kv in range(N_KV_HEADS) for t in per_g]
        o_ref[i * ATTN_BLOCK:(i + 1) * ATTN_BLOCK, :] = jnp.concatenate(out, axis=-1).astype(BF16)


def _attn_prompt(q, k, v, sinks, rel_table):
    b, s = q.shape[:2]
    qrows = ATTN_QB * ATTN_BLOCK
    assert s % qrows == 0
    qi = jnp.arange(ATTN_BLOCK, dtype=jnp.int32)[:, None]
    si = jnp.arange(2 * ATTN_BLOCK, dtype=jnp.int32)[None, :]
    bias = _bias_table(qi + ATTN_BLOCK - si, rel_table)
    bias = jnp.swapaxes(bias.reshape(N_KV_HEADS, GROUP * ATTN_BLOCK, 2 * ATTN_BLOCK), 1, 2)
    cur = lambda w: pl.BlockSpec((None, qrows, w), lambda i, j, *_: (i, j, 0))
    prev = lambda w: pl.BlockSpec((None, ATTN_BLOCK, w),
                                  lambda i, j, *_: (i, jnp.maximum(j * ATTN_QB - 1, 0), 0))
    return pl.pallas_call(
        _attn_prompt_kernel,
        grid_spec=pltpu.PrefetchScalarGridSpec(
            num_scalar_prefetch=1,
            grid=(b, s // qrows),
            in_specs=[cur(ATTN_WIDTH), prev(KV_WIDTH), cur(KV_WIDTH), prev(KV_WIDTH), cur(KV_WIDTH),
                      pl.BlockSpec(bias.shape, lambda i, j, *_: (0, 0, 0))],
            out_specs=cur(ATTN_WIDTH)),
        out_shape=jax.ShapeDtypeStruct((b, s, ATTN_WIDTH), BF16),
        compiler_params=_cparams(("parallel", "parallel")),
        name="attn_prompt",
    )(sinks.astype(F32), q, k, k, v, v, bias)


def _attn_sample_kernel(sink_ref, q_ref, kc_ref, vc_ref, kn_ref, vn_ref, bias_ref, bnew_ref,
                        o_ref, nk_ref, nv_ref):
    kc = kc_ref[...]
    vc = vc_ref[...]
    kn = kn_ref[...]
    vn = vn_ref[...]
    w = kc.shape[1]
    pos = lax.broadcasted_iota(jnp.int32, kc.shape, 1)
    nk_ref[...] = jnp.where(pos == w - 1, kn[:, None, :], pltpu.roll(kc, w - 1, 1))
    nv_ref[...] = jnp.where(pos == w - 1, vn[:, None, :], pltpu.roll(vc, w - 1, 1))
    gi = lax.broadcasted_iota(jnp.int32, (1, GROUP, 1), 1)
    for kv in range(N_KV_HEADS):
        sl = slice(kv * HEAD_DIM, (kv + 1) * HEAD_DIM)
        qg = q_ref[:, kv]
        s = jnp.einsum('bgd,bsd->bgs', qg, kc[:, :, sl].astype(BF16), preferred_element_type=F32)
        s = s + bias_ref[kv][None]
        s_new = jnp.sum(qg.astype(F32) * kn[:, None, sl], axis=-1, keepdims=True) + bnew_ref[kv][None]
        sink = jnp.zeros((1, GROUP, 1), F32)
        for g in range(GROUP):
            sink = jnp.where(gi == g, sink_ref[kv * GROUP + g], sink)
        m = jnp.maximum(jnp.maximum(jnp.max(s, axis=-1, keepdims=True), s_new), sink)
        p = jnp.exp(s - m)
        p_new = jnp.exp(s_new - m)
        denom = jnp.sum(p, axis=-1, keepdims=True) + p_new + jnp.exp(sink - m)
        o = jnp.einsum('bgs,bsd->bgd', p.astype(BF16), vc[:, :, sl].astype(BF16), preferred_element_type=F32)
        o = o + p_new * vn[:, None, sl]
        o_ref[:, kv] = o / denom


def _attn_sample(q, k_new, v_new, cache_k, cache_v, sinks, rel_table, *, tile=16):
    bd, w = cache_k.shape[:2]
    rel = w - jnp.arange(w, dtype=jnp.int32)
    bias = _bias_table(rel, rel_table).reshape(N_KV_HEADS, GROUP, w)
    bnew = _bias_table(jnp.zeros((1,), jnp.int32), rel_table).reshape(N_KV_HEADS, GROUP, 1)
    q4 = q.reshape(bd, N_KV_HEADS, GROUP, HEAD_DIM)
    spec4 = pl.BlockSpec((tile, N_KV_HEADS, GROUP, HEAD_DIM), lambda i, *_: (i, 0, 0, 0))
    cache = pl.BlockSpec((tile, w, KV_WIDTH), lambda i, *_: (i, 0, 0))
    new = pl.BlockSpec((tile, KV_WIDTH), lambda i, *_: (i, 0))
    const3 = lambda a: pl.BlockSpec(a.shape, lambda i, *_: (0, 0, 0))
    o, nk, nv = pl.pallas_call(
        _attn_sample_kernel,
        grid_spec=pltpu.PrefetchScalarGridSpec(
            num_scalar_prefetch=1,
            grid=(bd // tile,),
            in_specs=[spec4, cache, cache, new, new, const3(bias), const3(bnew)],
            out_specs=[spec4, cache, cache]),
        out_shape=[jax.ShapeDtypeStruct(q4.shape, F32),
                   jax.ShapeDtypeStruct(cache_k.shape, F32),
                   jax.ShapeDtypeStruct(cache_v.shape, F32)],
        compiler_params=_cparams(("parallel",)),
        name="attn_sample",
    )(sinks.astype(F32), q4, cache_k, cache_v, k_new, v_new, bias, bnew)
    return o.reshape(bd, ATTN_WIDTH), nk, nv


def _pool_project(d_groups, wp_ref, ps_ref):
    out = [_bdot(d, wp_ref[g]) for g, d in enumerate(d_groups)]
    return (jnp.concatenate(out, axis=-1) * ps_ref[...]).astype(BF16)


def _pool_prompt_kernel(u_ref, halo_ref, wp_ref, ps_ref, o_ref, ext, lv):
    t = pl.program_id(1)
    tile = u_ref.shape[0]
    lead, hb = SUBLANES, 2 * SUBLANES
    halo = halo_ref[...]
    ext[0:lead, :] = jnp.zeros((lead, ext.shape[1]), F32)
    ext[lead:lead + hb, :] = jnp.where(t == 0, jnp.zeros_like(halo), halo)
    ext[lead + hb:, :] = u_ref[...]
    lv[:, 0:lead, :] = jnp.zeros((lv.shape[0], lead, lv.shape[2]), F32)
    pos = t * tile + lax.broadcasted_iota(jnp.int32, (tile, 1), 0)
    n = hb + tile
    ds = []
    for g, w in enumerate(POOL_WINDOWS):
        sl = slice(g * POOL_GROUP, (g + 1) * POOL_GROUP)
        acc = ext[lead:lead + n, sl] + ext[lead - 1:lead - 1 + n, sl]
        span, level = 2, 0
        while span < w:
            lv[level, lead:lead + n, :] = acc
            acc = acc + lv[level, lead - span:lead - span + n, :]
            span, level = 2 * span, level + 1
        cnt = jnp.minimum(pos + 1, w).astype(F32)
        ds.append(acc[hb:] / cnt - ext[lead + hb:lead + hb + tile, sl])
    o_ref[...] = _pool_project(ds, wp_ref, ps_ref)


def _pool_prompt(u, w_pool, pool_scale, *, tile=512):
    b, s, c = u.shape
    hb = 2 * SUBLANES
    return pl.pallas_call(
        _pool_prompt_kernel,
        grid=(b, s // tile),
        in_specs=[pl.BlockSpec((None, tile, c), lambda i, t: (i, t, 0)),
                  pl.BlockSpec((None, hb, c), lambda i, t: (i, jnp.maximum(t * (tile // hb) - 1, 0), 0)),
                  pl.BlockSpec(w_pool.shape, lambda i, t: (0, 0, 0)),
                  pl.BlockSpec((1, c), lambda i, t: (0, 0))],
        out_specs=pl.BlockSpec((None, tile, c), lambda i, t: (i, t, 0)),
        out_shape=jax.ShapeDtypeStruct((b, s, c), BF16),
        scratch_shapes=[pltpu.VMEM((SUBLANES + hb + tile, c), F32),
                        pltpu.VMEM((len(POOL_WINDOWS) - 1, SUBLANES + hb + tile, POOL_GROUP), F32)],
        compiler_params=_cparams(("parallel", "parallel")),
        name="pool_prompt",
    )(u, u, w_pool, pool_scale.reshape(1, c))


def _pool_sample_kernel(st_ref, u_ref, wp_ref, ps_ref, o_ref, ns_ref):
    u = u_ref[...]
    ns_ref[0:POOL_BUF - 1] = st_ref[1:POOL_BUF]
    ns_ref[POOL_BUF - 1] = u
    ds = []
    for g, w in enumerate(POOL_WINDOWS):
        sl = slice(g * POOL_GROUP, (g + 1) * POOL_GROUP)
        acc = u[:, sl]
        for j in range(1, w):
            acc = acc + st_ref[POOL_BUF - j][:, sl]
        cnt = float(min(PAST_LEN + 1, w))
        ds.append(acc / cnt - u[:, sl])
    o_ref[...] = _pool_project(ds, wp_ref, ps_ref)


def _pool_sample(state_t, u, w_pool, pool_scale):
    nb, bd, c = state_t.shape
    full = lambda a: pl.BlockSpec(a.shape, lambda: (0,) * a.ndim)
    ps = pool_scale.reshape(1, c)
    return pl.pallas_call(
        _pool_sample_kernel,
        in_specs=[full(state_t), full(u), full(w_pool), full(ps)],
        out_specs=[pl.BlockSpec((bd, c), lambda: (0, 0)), full(state_t)],
        out_shape=[jax.ShapeDtypeStruct((bd, c), BF16), jax.ShapeDtypeStruct(state_t.shape, F32)],
        compiler_params=pltpu.CompilerParams(vmem_limit_bytes=VMEM_LIMIT),
        name="pool_sample",
    )(state_t, u, w_pool, ps)


def _route_kernel(x_ref, attn_ref, pool_ref, gm_ref, sh_ref, sc_ref, wo_ref, nf_ref, wr_ref, br_ref,
                  tri_t_ref, tri_e_ref,
                  x2_ref, hs_ref, slot_ref, cnt_ref):
    sub = slot_ref.shape[0]
    n = x_ref.shape[0]
    tile = n // sub
    mix = (jnp.dot(attn_ref[...], wo_ref[:ATTN_WIDTH, :], preferred_element_type=F32)
           + jnp.dot(pool_ref[...], wo_ref[ATTN_WIDTH:, :], preferred_element_type=F32))
    x2 = x_ref[...] + gm_ref[...] * mix
    x2_ref[...] = x2
    h = _rms(x2, nf_ref[...]) * (1.0 + sc_ref[...]) + sh_ref[...]

    hh, hl = _split(h)
    wh, wl = _split(wr_ref[...])
    nt = functools.partial(lax.dot_general, dimension_numbers=(((1,), (1,)), ((), ())),
                           preferred_element_type=F32)
    logits = nt(wh, hh) + nt(wl, hh) + nt(wh, hl) + br_ref[...]

    eidx = lax.broadcasted_iota(jnp.int32, (N_EXPERTS, n), 0).astype(F32)
    work = logits
    tops, picks = [], []
    for _ in range(TOP_K):
        m = jnp.max(work, axis=0, keepdims=True)
        pick = jnp.min(jnp.where(work == m, eidx, float(N_EXPERTS)), axis=0, keepdims=True)
        work = jnp.where(eidx == pick, -jnp.inf, work)
        tops.append(m)
        picks.append(pick)
    ex = [jnp.exp(v - tops[0]) for v in tops]
    den = ex[0] + ex[1] + ex[2] + ex[3]
    gates = [e / den for e in ex]

    sel = jnp.zeros((N_EXPERTS, n), F32)
    for pick in picks:
        sel = sel + (eidx == pick).astype(F32)
    rank = jnp.dot(sel.astype(BF16), tri_t_ref[...], preferred_element_type=F32)
    cnts = [jnp.sum(sel[:, t * tile:(t + 1) * tile], axis=1, keepdims=True) for t in range(sub)]
    padded = jnp.concatenate(
        [jnp.broadcast_to(jnp.ceil(c * (1.0 / CHUNK)) * CHUNK, (N_EXPERTS, LANES)) for c in cnts], axis=1)
    seg = jnp.dot(tri_e_ref[...], padded.astype(BF16), preferred_element_type=F32)
    dest = jnp.concatenate([seg[:, t * LANES:t * LANES + 1] + rank[:, t * tile:(t + 1) * tile]
                            for t in range(sub)], axis=1)
    slots = [jnp.sum(jnp.where(eidx == pick, dest, 0.0), axis=0, keepdims=True) for pick in picks]

    ridx = lax.broadcasted_iota(jnp.int32, (TILE_ROWS, tile), 0).astype(F32)
    for t in range(sub):
        cols = slice(t * tile, (t + 1) * tile)
        cnt_ref[t] = jnp.broadcast_to(cnts[t], (N_EXPERTS, LANES))
        slot_ref[t] = jnp.concatenate([v[:, cols] for v in slots + gates], axis=0)
        hit = ridx == slots[0][:, cols]
        for s in slots[1:]:
            hit = jnp.logical_or(hit, ridx == s[:, cols])
        perm = jnp.where(hit, 1.0, 0.0).astype(BF16)
        hs_ref[pl.ds(t * TILE_ROWS, TILE_ROWS), :] = _pack_rows(
            jnp.dot(perm, hh[t * tile:(t + 1) * tile, :], preferred_element_type=F32))


def _route(x2d, attn, pool, gm, sh, sc, w_out, norm_ffn, w_router_t, b_router, hs_prev, *, tile, sub,
           rows_per_mod, tile0, extra_tiles):
    n = x2d.shape[0]
    nt = n // tile
    assert nt % sub == 0 and extra_tiles % sub == 0
    own_steps = nt // sub
    steps = own_steps + extra_tiles // sub
    mrows = gm.shape[1]
    assert mrows == 1 or (sub == 1 and mrows == tile)
    last = lambda i: jnp.minimum(i, own_steps - 1)
    mod_spec = pl.BlockSpec((None, mrows, D_MODEL), lambda i: ((last(i) * sub * tile) // rows_per_mod, 0, 0))
    const = lambda shape: pl.BlockSpec(shape, lambda i: (0,) * len(shape))
    row = lambda w: pl.BlockSpec((sub * tile, w), lambda i: (last(i), 0))
    tok = jnp.arange(sub * tile)
    tri_t = jnp.logical_and(tok[:, None] < tok[None, :],
                            tok[:, None] // tile == tok[None, :] // tile).astype(BF16)
    tri_e = (jnp.arange(N_EXPERTS)[None, :] < jnp.arange(N_EXPERTS)[:, None]).astype(BF16)
    in_specs = [row(D_MODEL), row(ATTN_WIDTH), row(POOL_WIDTH), mod_spec, mod_spec, mod_spec,
                const((D_MODEL, D_MODEL)), const((1, D_MODEL)), const((N_EXPERTS, D_MODEL)),
                const((N_EXPERTS, 1)), const((sub * tile, sub * tile)), const((N_EXPERTS, N_EXPERTS))]
    args = [x2d, attn, pool, gm, sh, sc, w_out, norm_ffn.reshape(1, D_MODEL), w_router_t,
            b_router.reshape(N_EXPERTS, 1), tri_t, tri_e]
    n_in = len(args)
    kern = _route_kernel
    aliases = {}
    hs_rows = (tile0 + steps * sub) * TILE_ROWS
    assert tile0 % sub == 0
    if hs_prev is not None:
        in_specs.append(pl.BlockSpec(memory_space=pl.ANY))
        args.append(hs_prev)
        aliases = {n_in: 1}
        kern = lambda *refs: _route_kernel(*refs[:n_in], *refs[n_in + 1:])
        hs_rows = hs_prev.shape[0]
    return pl.pallas_call(
        kern,
        grid=(steps,),
        in_specs=in_specs,
        out_specs=[row(D_MODEL),
                   pl.BlockSpec((sub * TILE_ROWS, ROW_W), lambda i: (i + tile0 // sub, 0)),
                   pl.BlockSpec((sub, 2 * TOP_K, tile), lambda i: (last(i), 0, 0)),
                   pl.BlockSpec((sub, N_EXPERTS, LANES), lambda i: (last(i), 0, 0))],
        out_shape=[jax.ShapeDtypeStruct((n, D_MODEL), F32),
                   jax.ShapeDtypeStruct((hs_rows, ROW_W), jnp.int32),
                   jax.ShapeDtypeStruct((nt, 2 * TOP_K, tile), F32),
                   jax.ShapeDtypeStruct((nt, N_EXPERTS, LANES), F32)],
        input_output_aliases=aliases,
        compiler_params=_cparams(("arbitrary",)),
        name="route",
    )(*args)


def _moe_kernel(src_ref, be_ref, nxt_ref, nb_ref, hs_hbm, w1_hbm, b1_ref, w2_hbm, b2_ref, out_hbm,
                lhs, obuf, w1s, w2s, w1c, w2c, sem_in, sem_out, sem_w, *, scratch_chunk):
    nb = nb_ref[0]

    def weight_copies(e):
        return (pltpu.make_async_copy(w1_hbm.at[e], w1s, sem_w.at[0]),
                pltpu.make_async_copy(w2_hbm.at[e], w2s, sem_w.at[1]))

    def chunk_rows(c):
        return pl.ds(pl.multiple_of(c * CHUNK, CHUNK), CHUNK)

    def start_in(blk, s):
        for j in range(BLOCK_CHUNKS):
            c = src_ref[blk * BLOCK_CHUNKS + j]
            c = jnp.where(c < 0, ZERO_CHUNK, c)
            pltpu.make_async_copy(hs_hbm.at[chunk_rows(c)], lhs.at[s, pl.ds(j * CHUNK, CHUNK)],
                                  sem_in.at[s]).start()

    def wait_in(s):
        pltpu.make_async_copy(hs_hbm.at[pl.ds(0, MOE_BLOCK)], lhs.at[s], sem_in.at[s]).wait()

    def start_out(blk, s):
        for j in range(BLOCK_CHUNKS):
            c = src_ref[blk * BLOCK_CHUNKS + j]
            c = jnp.where(c < 0, scratch_chunk + s * BLOCK_CHUNKS + j, c)
            pltpu.make_async_copy(obuf.at[s, pl.ds(j * CHUNK, CHUNK)],
                                  out_hbm.at[chunk_rows(c), pl.ds(0, PACKED_W)], sem_out.at[s]).start()

    def wait_out(s):
        pltpu.make_async_copy(obuf.at[s], out_hbm.at[pl.ds(0, MOE_BLOCK), pl.ds(0, PACKED_W)],
                              sem_out.at[s]).wait()

    @pl.when(nb > 0)
    def _():
        start_in(0, 0)
        for cp in weight_copies(be_ref[0]):
            cp.start()

    def block(b, carry):
        slot = b % 2
        e = be_ref[b]

        @pl.when(jnp.logical_or(b == 0, e != be_ref[jnp.maximum(b - 1, 0)]))
        def _():
            for cp in weight_copies(e):
                cp.wait()
            w1c[...] = w1s[...].astype(BF16)
            w2c[...] = w2s[...].astype(BF16)
            nxt = nxt_ref[b]

            @pl.when(nxt >= 0)
            def _():
                for cp in weight_copies(nxt):
                    cp.start()

        wait_in(slot)

        @pl.when(b >= 2)
        def _():
            wait_out(slot)

        start_in(b + 1, 1 - slot)

        xh, xl = _unpack_rows(lhs[slot])
        gu = (jnp.dot(xh, w1c[:PACKED_W, :], preferred_element_type=F32)
              + jnp.dot(xl, w1c[PACKED_W:, :], preferred_element_type=F32) + b1_ref[e])
        gate = jnp.minimum(gu[:, :D_FF], SWIGLU_LIMIT)
        up = jnp.clip(gu[:, D_FF:], -SWIGLU_LIMIT, SWIGLU_LIMIT)
        act = (up + 1.0) * (gate * jax.nn.sigmoid(SWIGLU_ALPHA * gate))
        y = jnp.dot(act.astype(BF16), w2c[...], preferred_element_type=F32) + b2_ref[e]
        obuf[slot] = _pack_rows(y.astype(BF16).astype(F32))

        start_out(b, slot)
        return carry

    lax.fori_loop(0, nb, block, 0)

    @pl.when(nb > 0)
    def _():
        last_slot = (nb - 1) % 2
        wait_in(1 - last_slot)

        @pl.when(nb >= 2)
        def _():
            wait_out(1 - last_slot)
        wait_out(last_slot)


def _moe(hs, src, block_e, next_e, nblocks, w1, b1, w2, b2, scratch_chunk):
    full = lambda shape: pl.BlockSpec(shape, lambda i, *_: (0,) * len(shape))
    hbm = pl.BlockSpec(memory_space=pl.ANY)
    return pl.pallas_call(
        functools.partial(_moe_kernel, scratch_chunk=scratch_chunk),
        grid_spec=pltpu.PrefetchScalarGridSpec(
            num_scalar_prefetch=4,
            grid=(1,),
            in_specs=[hbm, hbm, full((N_EXPERTS, 1, 2 * D_FF)), hbm, full((N_EXPERTS, 1, D_MODEL))],
            out_specs=hbm,
            scratch_shapes=[pltpu.VMEM((2, MOE_BLOCK, ROW_W), jnp.int32),
                            pltpu.VMEM((2, MOE_BLOCK, PACKED_W), jnp.int32),
                            pltpu.VMEM((D_MODEL, 2 * D_FF), F32),
                            pltpu.VMEM((D_FF, D_MODEL), F32),
                            pltpu.VMEM((D_MODEL, 2 * D_FF), BF16),
                            pltpu.VMEM((D_FF, D_MODEL), BF16),
                            pltpu.SemaphoreType.DMA((2,)),
                            pltpu.SemaphoreType.DMA((2,)),
                            pltpu.SemaphoreType.DMA((2,))]),
        out_shape=jax.ShapeDtypeStruct(hs.shape, jnp.int32),
        input_output_aliases={4: 0},
        compiler_params=_cparams(("arbitrary",)),
        name="moe_experts",
    )(src, block_e, next_e, nblocks, hs, w1, b1.reshape(N_EXPERTS, 1, 2 * D_FF), w2,
      b2.reshape(N_EXPERTS, 1, D_MODEL))


def _plan(cnt, n_blocks_max):
    nt = cnt.shape[0]
    nch = (cnt + (CHUNK - 1)) // CHUNK
    lstart = jnp.cumsum(nch, axis=1) - nch
    ne = jnp.sum(nch, axis=0)
    nbe = (ne + (BLOCK_CHUNKS - 1)) // BLOCK_CHUNKS
    bend = jnp.cumsum(nbe)
    nblocks = bend[-1]
    gstart = (bend - nbe)[None, :] * BLOCK_CHUNKS + (jnp.cumsum(nch, axis=0) - nch)
    s0 = jnp.arange(nt, dtype=jnp.int32)[:, None] * TILE_CHUNKS + lstart
    blk = jnp.arange(n_blocks_max + 1, dtype=jnp.int32)
    be = jnp.sum((blk[:, None] >= bend[None, :]).astype(jnp.int32), axis=1)
    be_last = jnp.sum((nblocks - 1 >= bend).astype(jnp.int32))
    be = jnp.minimum(be, be_last).astype(jnp.int32)
    eid = jnp.arange(N_EXPERTS, dtype=jnp.int32)
    strips = jnp.stack([gstart.T, nch.T, s0.T])
    mine = (be[:, None] == eid[None, :])[None, :, :, None]
    gs_b, nc_b, s0_b = jnp.sum(jnp.where(mine, strips[:, None], 0), axis=2)
    c = (blk[:, None] * BLOCK_CHUNKS + jnp.arange(BLOCK_CHUNKS, dtype=jnp.int32)[None, :])[:, :, None]
    inside = jnp.logical_and(c >= gs_b[:, None, :], c < (gs_b + nc_b)[:, None, :])
    src = (jnp.sum(jnp.where(inside, (s0_b - gs_b)[:, None, :] + c + 1, 0), axis=2) - 1).reshape(-1)
    be = be[:n_blocks_max]
    later = jnp.logical_and(eid[None, :] > be[:, None], (nbe > 0)[None, :])
    nxt = jnp.min(jnp.where(later, eid[None, :], N_EXPERTS), axis=1)
    nxt = jnp.where(nxt == N_EXPERTS, -1, nxt).astype(jnp.int32)
    return src, be, nxt, nblocks.reshape(1).astype(jnp.int32)


def _combine_kernel(o_ref, slot_ref, x2_ref, gf_ref, y_ref):
    sub = slot_ref.shape[0]
    tile = x2_ref.shape[0] // sub
    ridx = lax.broadcasted_iota(jnp.int32, (TILE_ROWS, tile), 0).astype(F32)
    tn = functools.partial(lax.dot_general, dimension_numbers=(((0,), (0,)), ((), ())),
                           preferred_element_type=F32)
    for t in range(sub):
        gmat = jnp.zeros((TILE_ROWS, tile), F32)
        for k in range(TOP_K):
            gmat = jnp.where(ridx == slot_ref[t, k:k + 1, :], slot_ref[t, TOP_K + k:TOP_K + k + 1, :], gmat)
        gb = gmat.astype(BF16)
        oh, ol = _unpack_rows(o_ref[pl.ds(t * TILE_ROWS, TILE_ROWS), :])
        y = jnp.concatenate([tn(gb, oh), tn(gb, ol)], axis=1)
        rows = pl.ds(t * tile, tile)
        gf = gf_ref[...] if gf_ref.shape[0] == 1 else gf_ref[rows, :]
        y_ref[rows, :] = x2_ref[rows, :] + gf * y


def _combine(outs, slots, x2, gf, *, tile, sub, rows_per_mod, tile0):
    n = x2.shape[0]
    mrows = gf.shape[1]
    assert n % (sub * tile) == 0 and tile0 % sub == 0 and (mrows == 1 or sub == 1)
    return pl.pallas_call(
        _combine_kernel,
        grid=(n // (sub * tile),),
        in_specs=[pl.BlockSpec((sub * TILE_ROWS, PACKED_W), lambda i: (i + tile0 // sub, 0)),
                  pl.BlockSpec((sub, 2 * TOP_K, tile), lambda i: (i, 0, 0)),
                  pl.BlockSpec((sub * tile, D_MODEL), lambda i: (i, 0)),
                  pl.BlockSpec((None, mrows, D_MODEL), lambda i: ((i * sub * tile) // rows_per_mod, 0, 0))],
        out_specs=pl.BlockSpec((sub * tile, D_MODEL), lambda i: (i, 0)),
        out_shape=jax.ShapeDtypeStruct((n, D_MODEL), F32),
        compiler_params=_cparams(("parallel",)),
        name="combine",
    )(outs, slots, x2, gf)


def kernel(x_prompt, x_sample, cache_k, cache_v, state_pool, c_prompt, c_sample, rel_bias, norm_mix, w_ada,
           b_ada, w_in, q_norm, k_norm, sinks, w_pool, pool_scale, w_out, norm_ffn, w_router, b_router,
           w1, b1, w2, b2):
    depth = w_in.shape[0]
    assert depth == 1
    l = 0
    bp, sp, _ = x_prompt.shape
    bs = x_sample.shape[0]
    assert x_sample.shape[1] == 1 and sp % SORT_TILE == 0 and bs <= SORT_TILE
    n_p = bp * sp
    tiles_p = n_p // SORT_TILE
    max_chunks = tiles_p * (SORT_TILE * TOP_K // CHUNK + N_EXPERTS) + (bs * TOP_K // CHUNK + N_EXPERTS)
    n_blocks_max = -(-max_chunks // BLOCK_CHUNKS) + N_EXPERTS

    mod = _modulation(jnp.concatenate([c_prompt, c_sample], axis=0), w_ada[l], b_ada[l])
    mod_p = [m.reshape(bp, 1, D_MODEL) for m in jnp.split(mod[:bp], 6, axis=-1)]
    mod_s = [m.reshape(1, bs, D_MODEL) for m in jnp.split(mod[bp:], 6, axis=-1)]

    head = jnp.arange(ATTN_WIDTH) // HEAD_DIM
    bd = (head[:, None] == head[None, :]).astype(BF16)
    w_in_b = w_in[l].astype(BF16)
    w_out_b = w_out[l].astype(BF16)
    w_pool_b = w_pool[l].astype(BF16)
    w_router_t = w_router[l].T

    xp = x_prompt.reshape(n_p, D_MODEL)
    q, k, v, u = _mixer_inputs(xp, mod_p[0], mod_p[1], norm_mix[l], w_in_b, bd, q_norm[l], k_norm[l],
                               tile=512, rows_per_mod=sp, precise=False)
    k3 = k.reshape(bp, sp, KV_WIDTH)
    v3 = v.reshape(bp, sp, KV_WIDTH)
    u3 = u.reshape(bp, sp, POOL_WIDTH)
    attn = _attn_prompt(q.reshape(bp, sp, ATTN_WIDTH), k3, v3, sinks[l], rel_bias)
    pool = _pool_prompt(u3, w_pool_b, pool_scale[l])
    keep = min(WINDOW, sp)
    nkp = k3[:, -keep:].reshape(bp, keep, N_KV_HEADS, HEAD_DIM)
    nvp = v3[:, -keep:].reshape(bp, keep, N_KV_HEADS, HEAD_DIM)
    if sp >= POOL_BUF:
        npp = u3[:, -POOL_BUF:]
    else:
        npp = jnp.concatenate([jnp.zeros((bp, POOL_BUF - sp, POOL_WIDTH), F32), u3], axis=1)
    x2_p, hs, slots_p, cnt_p = _route(
        xp, attn.reshape(n_p, ATTN_WIDTH), pool.reshape(n_p, POOL_WIDTH), mod_p[2], mod_p[3], mod_p[4],
        w_out_b, norm_ffn[l], w_router_t, b_router[l], None,
        tile=SORT_TILE, sub=ROUTE_SUB, rows_per_mod=sp, tile0=0, extra_tiles=max(2, ROUTE_SUB))

    xs = x_sample.reshape(bs, D_MODEL)
    qs, ks, vs, us = _mixer_inputs(xs, mod_s[0], mod_s[1], norm_mix[l], w_in[l], bd, q_norm[l], k_norm[l],
                                   tile=bs, rows_per_mod=bs, precise=True)
    wbuf = cache_k.shape[2]
    attn_s, nks, nvs = _attn_sample(qs, ks, vs, cache_k[l].reshape(bs, wbuf, KV_WIDTH),
                                    cache_v[l].reshape(bs, wbuf, KV_WIDTH), sinks[l], rel_bias)
    pool_s, nps_t = _pool_sample(jnp.swapaxes(state_pool[l], 0, 1), us, w_pool_b, pool_scale[l])
    x2_s, hs, slots_s, cnt_s = _route(
        xs, attn_s.astype(BF16), pool_s, mod_s[2], mod_s[3], mod_s[4], w_out_b, norm_ffn[l],
        w_router_t, b_router[l], hs, tile=bs, sub=1, rows_per_mod=bs, tile0=tiles_p, extra_tiles=0)

    cnt = jnp.concatenate([cnt_p[:, :, 0], cnt_s[:, :, 0]], axis=0).astype(jnp.int32)
    src, block_e, next_e, nblocks = _plan(cnt, n_blocks_max)
    outs = _moe(hs, src, block_e, next_e, nblocks, w1[l], b1[l], w2[l], b2[l],
                scratch_chunk=(tiles_p + 1) * TILE_CHUNKS)

    y_p = _combine(outs, slots_p, x2_p, mod_p[5], tile=SORT_TILE, sub=COMBINE_SUB, rows_per_mod=sp, tile0=0)
    y_s = _combine(outs, slots_s, x2_s, mod_s[5], tile=bs, sub=1, rows_per_mod=bs, tile0=tiles_p)

    return (y_p.reshape(bp, sp, D_MODEL), y_s.reshape(bs, 1, D_MODEL),
            nkp[None], nvp[None], npp[None],
            nks.reshape(1, bs, wbuf, N_KV_HEADS, HEAD_DIM), nvs.reshape(1, bs, wbuf, N_KV_HEADS, HEAD_DIM),
            jnp.swapaxes(nps_t, 0, 1)[None])
```

```python
import functools
import math

import jax
import jax.numpy as jnp
from jax import lax
from jax.experimental import pallas as pl
from jax.experimental.pallas import tpu as pltpu

F32 = jnp.float32
BF16 = jnp.bfloat16

D_MODEL = 1024
HEAD_DIM = 64
N_HEADS = 8
N_KV_HEADS = 2
GROUP = N_HEADS // N_KV_HEADS
ATTN_WIDTH = N_HEADS * HEAD_DIM
KV_WIDTH = N_KV_HEADS * HEAD_DIM
POOL_WIDTH = D_MODEL - ATTN_WIDTH
POOL_WINDOWS = (2, 4, 8, 16)
POOL_GROUP = POOL_WIDTH // len(POOL_WINDOWS)
POOL_BUF = max(POOL_WINDOWS) - 1
IN_WIDTH = ATTN_WIDTH + 2 * KV_WIDTH + POOL_WIDTH
WINDOW = 128
ATTN_BLOCK = 128
N_BUCKETS = 32
MAX_EXACT = 16
REL_MAX_DIST = 128
N_EXPERTS = 32
TOP_K = 4
D_FF = D_MODEL
SWIGLU_LIMIT = 7.0
SWIGLU_ALPHA = 1.702
EPS = 1e-6
NEG_INF = -1e30
PAST_LEN = 16384

LANES = 128
SUBLANES = 8
VMEM_LIMIT = 56 * 1024 * 1024

ATTN_QB = 2

SORT_TILE = 256
ROUTE_SUB = 4
COMBINE_SUB = 2
CHUNK = SUBLANES
TILE_ROWS = -(-(SORT_TILE * TOP_K + N_EXPERTS * (CHUNK - 1)) // LANES) * LANES
TILE_CHUNKS = TILE_ROWS // CHUNK
MOE_BLOCK = 256
BLOCK_CHUNKS = MOE_BLOCK // CHUNK
PACKED_W = D_MODEL // 2
ROW_W = PACKED_W
ZERO_CHUNK = TILE_CHUNKS - 1


def _bdot(a, b):
    return jnp.dot(a.astype(BF16), b.astype(BF16), preferred_element_type=F32)


def _split(a):
    hi = a.astype(BF16)
    lo = (a - hi.astype(F32)).astype(BF16)
    return hi, lo


def _dot3(a, b):
    ah, al = _split(a)
    bh, bl = _split(b)
    d = functools.partial(jnp.dot, preferred_element_type=F32)
    return d(ah, bh) + d(al, bh) + d(ah, bl)


def _pack_rows(x):
    bits = lax.bitcast_convert_type(x, jnp.int32)
    return bits[:, :PACKED_W] | lax.shift_right_logical(bits[:, PACKED_W:], 16)


def _unpack_rows(w):
    hi = lax.bitcast_convert_type(w & jnp.int32(-65536), F32)
    lo = lax.bitcast_convert_type(lax.shift_left(w, 16), F32)
    return hi.astype(BF16), lo.astype(BF16)


def _rms(x, g):
    return x * lax.rsqrt(jnp.mean(x * x, axis=-1, keepdims=True) + EPS) * g


def _cparams(sem, **kw):
    return pltpu.CompilerParams(dimension_semantics=sem, vmem_limit_bytes=VMEM_LIMIT, **kw)


def _ada_kernel(c_ref, w_ref, b_ref, o_ref):
    c = c_ref[...]
    s = c * jax.nn.sigmoid(c)
    o_ref[...] = _dot3(s, w_ref[...]) + b_ref[...]


def _modulation(c, w_ada, b_ada):
    rows = c.shape[0]
    n = w_ada.shape[1]
    tn = 1024
    return pl.pallas_call(
        _ada_kernel,
        grid=(n // tn,),
        in_specs=[pl.BlockSpec((rows, D_MODEL), lambda j: (0, 0)),
                  pl.BlockSpec((D_MODEL, tn), lambda j: (0, j)),
                  pl.BlockSpec((1, tn), lambda j: (0, j))],
        out_specs=pl.BlockSpec((rows, tn), lambda j: (0, j)),
        out_shape=jax.ShapeDtypeStruct((rows, n), F32),
        compiler_params=_cparams(("parallel",)),
        name="modulation",
    )(c, w_ada, b_ada.reshape(1, n))


def _head_rms(t, bd, g, precise):
    if precise:
        hi, lo = _split(t * t)
        ss = jnp.dot(hi, bd, preferred_element_type=F32) + jnp.dot(lo, bd, preferred_element_type=F32)
    else:
        ss = _bdot(t * t, bd)
    return t * lax.rsqrt(ss * (1.0 / HEAD_DIM) + EPS) * g


def _mixin_kernel(x_ref, sh_ref, sc_ref, g_ref, w_ref, bd_ref, qn_ref, kn_ref,
                  q_ref, k_ref, v_ref, u_ref, *, precise):
    h = _rms(x_ref[...], g_ref[...]) * (1.0 + sc_ref[...]) + sh_ref[...]
    z = _dot3(h, w_ref[...]) if precise else _bdot(h, w_ref[...])
    q = z[:, :ATTN_WIDTH]
    k = z[:, ATTN_WIDTH:ATTN_WIDTH + KV_WIDTH]
    bd = bd_ref[...]
    q = _head_rms(q, bd, qn_ref[...], precise)
    k = _head_rms(k, bd[:KV_WIDTH, :KV_WIDTH], kn_ref[...], precise)
    q_ref[...] = (q * (HEAD_DIM ** -0.5)).astype(BF16)
    k_ref[...] = k
    v_ref[...] = z[:, ATTN_WIDTH + KV_WIDTH:ATTN_WIDTH + 2 * KV_WIDTH]
    u_ref[...] = z[:, ATTN_WIDTH + 2 * KV_WIDTH:]


def _mixer_inputs(x2d, shift, scale, norm_mix, w_in, bd, q_norm, k_norm, *, tile, rows_per_mod, precise):
    n = x2d.shape[0]
    mrows = shift.shape[1]
    mod_spec = pl.BlockSpec((None, mrows, D_MODEL), lambda i: ((i * tile) // rows_per_mod, 0, 0))
    const = lambda shape: pl.BlockSpec(shape, lambda i: (0,) * len(shape))
    row = lambda w: pl.BlockSpec((tile, w), lambda i: (i, 0))
    return pl.pallas_call(
        functools.partial(_mixin_kernel, precise=precise),
        grid=(n // tile,),
        in_specs=[row(D_MODEL), mod_spec, mod_spec, const((1, D_MODEL)), const((D_MODEL, IN_WIDTH)),
                  const((ATTN_WIDTH, ATTN_WIDTH)), const((1, ATTN_WIDTH)), const((1, KV_WIDTH))],
        out_specs=[row(ATTN_WIDTH), row(KV_WIDTH), row(KV_WIDTH), row(POOL_WIDTH)],
        out_shape=[jax.ShapeDtypeStruct((n, ATTN_WIDTH), BF16),
                   jax.ShapeDtypeStruct((n, KV_WIDTH), F32),
                   jax.ShapeDtypeStruct((n, KV_WIDTH), F32),
                   jax.ShapeDtypeStruct((n, POOL_WIDTH), F32)],
        compiler_params=_cparams(("parallel",)),
        name="mixer_inputs",
    )(x2d, shift, scale, norm_mix.reshape(1, D_MODEL), w_in, bd,
      jnp.tile(q_norm, N_HEADS).reshape(1, ATTN_WIDTH), jnp.tile(k_norm, N_KV_HEADS).reshape(1, KV_WIDTH))


def _t5_bucket(rel):
    n = jnp.maximum(rel, 0)
    nf = jnp.maximum(n, 1).astype(F32)
    large = MAX_EXACT + (jnp.log(nf / MAX_EXACT) / math.log(REL_MAX_DIST / MAX_EXACT)
                         * (N_BUCKETS - MAX_EXACT)).astype(jnp.int32)
    large = jnp.minimum(large, N_BUCKETS - 1)
    return jnp.where(n < MAX_EXACT, n, large)


def _bias_table(rel, rel_table):
    bucket = _t5_bucket(rel)
    table = rel_table.astype(F32)
    ids = jnp.arange(N_BUCKETS, dtype=bucket.dtype).reshape((N_BUCKETS, 1) + (1,) * rel.ndim)
    onehot = bucket[None, None] == ids
    bias = jnp.sum(jnp.where(onehot, table.reshape(table.shape + (1,) * rel.ndim), 0.0), axis=0)
    valid = (rel >= 0) & (rel < WINDOW)
    return jnp.where(valid[None], bias, NEG_INF)


def _attn_prompt_kernel(sink_ref, q_ref, kp_ref, kc_ref, vp_ref, vc_ref, bias_ref, o_ref):
    first = pl.program_id(1) == 0
    kk = jnp.concatenate([kp_ref[...], kc_ref[...]], axis=0).astype(BF16)
    vv = jnp.concatenate([vp_ref[...], vc_ref[...]], axis=0).astype(BF16)
    key = lax.broadcasted_iota(jnp.int32, (2 * ATTN_BLOCK, 1), 0)
    no_prev = jnp.logical_and(first, key < ATTN_BLOCK)
    lane = lax.broadcasted_iota(jnp.int32, (1, N_HEADS * ATTN_BLOCK), 1)
    sink = jnp.zeros((1, N_HEADS * ATTN_BLOCK), F32)
    for h in range(N_HEADS):
        sink = jnp.where(lane // ATTN_BLOCK == h, sink_ref[h], sink)
    contract = lambda a, b, dims: lax.dot_general(a, b, (dims, ((), ())), preferred_element_type=F32)
    part = GROUP * ATTN_BLOCK
    for i in range(ATTN_QB):
        q = q_ref[i * ATTN_BLOCK:(i + 1) * ATTN_BLOCK, :]
        keys = slice(i * ATTN_BLOCK, (i + 2) * ATTN_BLOCK)
        scores = []
        for kv in range(N_KV_HEADS):
            heads = range(kv * GROUP, (kv + 1) * GROUP)
            qg = jnp.concatenate([q[:, h * HEAD_DIM:(h + 1) * HEAD_DIM] for h in heads], axis=0)
            scores.append(contract(kk[keys, kv * HEAD_DIM:(kv + 1) * HEAD_DIM], qg, ((1,), (1,))))
        s = jnp.concatenate(scores, axis=1) + bias_ref[...]
        if i == 0:
            s = jnp.where(no_prev, NEG_INF, s)
        m = jnp.maximum(jnp.max(s, axis=0, keepdims=True), sink)
        p = jnp.exp(s - m)
        denom = jnp.sum(p, axis=0, keepdims=True) + jnp.exp(sink - m)
        p = p.astype(BF16)
        halves = [contract(vv[keys, kv * HEAD_DIM:(kv + 1) * HEAD_DIM], p[:, kv * part:(kv + 1) * part],
                           ((0,), (0,))) / denom[:, kv * part:(kv + 1) * part]
                  for kv in range(N_KV_HEADS)]
        o_t = jnp.concatenate(halves, axis=0)
        per_g = [o_t[:, g * ATTN_BLOCK:(g + 1) * ATTN_BLOCK].T for g in range(GROUP)]
        out = [t[:, kv * HEAD_DIM:(kv + 1) * HEAD_DIM] for kv in range(N_KV_HEADS) for t in per_g]
        o_ref[i * ATTN_BLOCK:(i + 1) * ATTN_BLOCK, :] = jnp.concatenate(out, axis=-1).astype(BF16)


def _attn_prompt(q, k, v, sinks, rel_table):
    b, s = q.shape[:2]
    qrows = ATTN_QB * ATTN_BLOCK
    assert s % qrows == 0
    qi = jnp.arange(ATTN_BLOCK, dtype=jnp.int32)[:, None]
    si = jnp.arange(2 * ATTN_BLOCK, dtype=jnp.int32)[None, :]
    bias = _bias_table(qi + ATTN_BLOCK - si, rel_table)
    bias = bias.reshape(N_HEADS * ATTN_BLOCK, 2 * ATTN_BLOCK).T
    cur = lambda w: pl.BlockSpec((None, qrows, w), lambda i, j, *_: (i, j, 0))
    prev = lambda w: pl.BlockSpec((None, ATTN_BLOCK, w),
                                  lambda i, j, *_: (i, jnp.maximum(j * ATTN_QB - 1, 0), 0))
    return pl.pallas_call(
        _attn_prompt_kernel,
        grid_spec=pltpu.PrefetchScalarGridSpec(
            num_scalar_prefetch=1,
            grid=(b, s // qrows),
            in_specs=[cur(ATTN_WIDTH), prev(KV_WIDTH), cur(KV_WIDTH), prev(KV_WIDTH), cur(KV_WIDTH),
                      pl.BlockSpec(bias.shape, lambda i, j, *_: (0, 0))],
            out_specs=cur(ATTN_WIDTH)),
        out_shape=jax.ShapeDtypeStruct((b, s, ATTN_WIDTH), BF16),
        compiler_params=_cparams(("parallel", "parallel")),
        name="attn_prompt",
    )(sinks.astype(F32), q, k, k, v, v, bias)


def _attn_sample_kernel(sink_ref, q_ref, kc_ref, vc_ref, kn_ref, vn_ref, bias_ref, bnew_ref,
                        o_ref, nk_ref, nv_ref):
    kc = kc_ref[...]
    vc = vc_ref[...]
    kn = kn_ref[...]
    vn = vn_ref[...]
    w = kc.shape[1]
    pos = lax.broadcasted_iota(jnp.int32, kc.shape, 1)
    nk_ref[...] = jnp.where(pos == w - 1, kn[:, None, :], pltpu.roll(kc, w - 1, 1))
    nv_ref[...] = jnp.where(pos == w - 1, vn[:, None, :], pltpu.roll(vc, w - 1, 1))
    gi = lax.broadcasted_iota(jnp.int32, (1, GROUP, 1), 1)
    for kv in range(N_KV_HEADS):
        sl = slice(kv * HEAD_DIM, (kv + 1) * HEAD_DIM)
        qg = q_ref[:, kv]
        s = jnp.einsum('bgd,bsd->bgs', qg, kc[:, :, sl].astype(BF16), preferred_element_type=F32)
        s = s + bias_ref[kv][None]
        s_new = jnp.sum(qg.astype(F32) * kn[:, None, sl], axis=-1, keepdims=True) + bnew_ref[kv][None]
        sink = jnp.zeros((1, GROUP, 1), F32)
        for g in range(GROUP):
            sink = jnp.where(gi == g, sink_ref[kv * GROUP + g], sink)
        m = jnp.maximum(jnp.maximum(jnp.max(s, axis=-1, keepdims=True), s_new), sink)
        p = jnp.exp(s - m)
        p_new = jnp.exp(s_new - m)
        denom = jnp.sum(p, axis=-1, keepdims=True) + p_new + jnp.exp(sink - m)
        o = jnp.einsum('bgs,bsd->bgd', p.astype(BF16), vc[:, :, sl].astype(BF16), preferred_element_type=F32)
        o = o + p_new * vn[:, None, sl]
        o_ref[:, kv] = o / denom


def _attn_sample(q, k_new, v_new, cache_k, cache_v, sinks, rel_table, *, tile=16):
    bd, w = cache_k.shape[:2]
    rel = w - jnp.arange(w, dtype=jnp.int32)
    bias = _bias_table(rel, rel_table).reshape(N_KV_HEADS, GROUP, w)
    bnew = _bias_table(jnp.zeros((1,), jnp.int32), rel_table).reshape(N_KV_HEADS, GROUP, 1)
    q4 = q.reshape(bd, N_KV_HEADS, GROUP, HEAD_DIM)
    spec4 = pl.BlockSpec((tile, N_KV_HEADS, GROUP, HEAD_DIM), lambda i, *_: (i, 0, 0, 0))
    cache = pl.BlockSpec((tile, w, KV_WIDTH), lambda i, *_: (i, 0, 0))
    new = pl.BlockSpec((tile, KV_WIDTH), lambda i, *_: (i, 0))
    const3 = lambda a: pl.BlockSpec(a.shape, lambda i, *_: (0, 0, 0))
    o, nk, nv = pl.pallas_call(
        _attn_sample_kernel,
        grid_spec=pltpu.PrefetchScalarGridSpec(
            num_scalar_prefetch=1,
            grid=(bd // tile,),
            in_specs=[spec4, cache, cache, new, new, const3(bias), const3(bnew)],
            out_specs=[spec4, cache, cache]),
        out_shape=[jax.ShapeDtypeStruct(q4.shape, F32),
                   jax.ShapeDtypeStruct(cache_k.shape, F32),
                   jax.ShapeDtypeStruct(cache_v.shape, F32)],
        compiler_params=_cparams(("parallel",)),
        name="attn_sample",
    )(sinks.astype(F32), q4, cache_k, cache_v, k_new, v_new, bias, bnew)
    return o.reshape(bd, ATTN_WIDTH), nk, nv


def _pool_project(d_groups, wp_ref, ps_ref):
    out = [_bdot(d, wp_ref[g]) for g, d in enumerate(d_groups)]
    return (jnp.concatenate(out, axis=-1) * ps_ref[...]).astype(BF16)


def _pool_prompt_kernel(u_ref, halo_ref, wp_ref, ps_ref, o_ref, ext, lv):
    t = pl.program_id(1)
    tile = u_ref.shape[0]
    lead, hb = SUBLANES, 2 * SUBLANES
    halo = halo_ref[...]
    ext[0:lead, :] = jnp.zeros((lead, ext.shape[1]), F32)
    ext[lead:lead + hb, :] = jnp.where(t == 0, jnp.zeros_like(halo), halo)
    ext[lead + hb:, :] = u_ref[...]
    lv[:, 0:lead, :] = jnp.zeros((lv.shape[0], lead, lv.shape[2]), F32)
    pos = t * tile + lax.broadcasted_iota(jnp.int32, (tile, 1), 0)
    n = hb + tile
    ds = []
    for g, w in enumerate(POOL_WINDOWS):
        sl = slice(g * POOL_GROUP, (g + 1) * POOL_GROUP)
        acc = ext[lead:lead + n, sl] + ext[lead - 1:lead - 1 + n, sl]
        span, level = 2, 0
        while span < w:
            lv[level, lead:lead + n, :] = acc
            acc = acc + lv[level, lead - span:lead - span + n, :]
            span, level = 2 * span, level + 1
        cnt = jnp.minimum(pos + 1, w).astype(F32)
        ds.append(acc[hb:] / cnt - ext[lead + hb:lead + hb + tile, sl])
    o_ref[...] = _pool_project(ds, wp_ref, ps_ref)


def _pool_prompt(u, w_pool, pool_scale, *, tile=1024):
    b, s, c = u.shape
    hb = 2 * SUBLANES
    return pl.pallas_call(
        _pool_prompt_kernel,
        grid=(b, s // tile),
        in_specs=[pl.BlockSpec((None, tile, c), lambda i, t: (i, t, 0)),
                  pl.BlockSpec((None, hb, c), lambda i, t: (i, jnp.maximum(t * (tile // hb) - 1, 0), 0)),
                  pl.BlockSpec(w_pool.shape, lambda i, t: (0, 0, 0)),
                  pl.BlockSpec((1, c), lambda i, t: (0, 0))],
        out_specs=pl.BlockSpec((None, tile, c), lambda i, t: (i, t, 0)),
        out_shape=jax.ShapeDtypeStruct((b, s, c), BF16),
        scratch_shapes=[pltpu.VMEM((SUBLANES + hb + tile, c), F32),
                        pltpu.VMEM((len(POOL_WINDOWS) - 1, SUBLANES + hb + tile, POOL_GROUP), F32)],
        compiler_params=_cparams(("parallel", "parallel")),
        name="pool_prompt",
    )(u, u, w_pool, pool_scale.reshape(1, c))


def _pool_sample_kernel(st_ref, u_ref, wp_ref, ps_ref, o_ref, ns_ref):
    u = u_ref[...]
    ns_ref[0:POOL_BUF - 1] = st_ref[1:POOL_BUF]
    ns_ref[POOL_BUF - 1] = u
    ds = []
    for g, w in enumerate(POOL_WINDOWS):
        sl = slice(g * POOL_GROUP, (g + 1) * POOL_GROUP)
        acc = u[:, sl]
        for j in range(1, w):
            acc = acc + st_ref[POOL_BUF - j][:, sl]
        cnt = float(min(PAST_LEN + 1, w))
        ds.append(acc / cnt - u[:, sl])
    o_ref[...] = _pool_project(ds, wp_ref, ps_ref)


def _pool_sample(state_t, u, w_pool, pool_scale):
    nb, bd, c = state_t.shape
    full = lambda a: pl.BlockSpec(a.shape, lambda: (0,) * a.ndim)
    ps = pool_scale.reshape(1, c)
    return pl.pallas_call(
        _pool_sample_kernel,
        in_specs=[full(state_t), full(u), full(w_pool), full(ps)],
        out_specs=[pl.BlockSpec((bd, c), lambda: (0, 0)), full(state_t)],
        out_shape=[jax.ShapeDtypeStruct((bd, c), BF16), jax.ShapeDtypeStruct(state_t.shape, F32)],
        compiler_params=pltpu.CompilerParams(vmem_limit_bytes=VMEM_LIMIT),
        name="pool_sample",
    )(state_t, u, w_pool, ps)


def _route_kernel(x_ref, attn_ref, pool_ref, gm_ref, sh_ref, sc_ref, wo_ref, nf_ref, wr_ref, br_ref,
                  tri_t_ref, tri_e_ref,
                  x2_ref, hs_ref, slot_ref, cnt_ref):
    sub = slot_ref.shape[0]
    n = x_ref.shape[0]
    tile = n // sub
    mix = (jnp.dot(attn_ref[...], wo_ref[:ATTN_WIDTH, :], preferred_element_type=F32)
           + jnp.dot(pool_ref[...], wo_ref[ATTN_WIDTH:, :], preferred_element_type=F32))
    x2 = x_ref[...] + gm_ref[...] * mix
    x2_ref[...] = x2
    h = _rms(x2, nf_ref[...]) * (1.0 + sc_ref[...]) + sh_ref[...]

    hh, hl = _split(h)
    wh, wl = _split(wr_ref[...])
    nt = functools.partial(lax.dot_general, dimension_numbers=(((1,), (1,)), ((), ())),
                           preferred_element_type=F32)
    logits = nt(wh, hh) + nt(wl, hh) + nt(wh, hl) + br_ref[...]

    eidx = lax.broadcasted_iota(jnp.int32, (N_EXPERTS, n), 0).astype(F32)
    work = logits
    tops, picks = [], []
    for _ in range(TOP_K):
        m = jnp.max(work, axis=0, keepdims=True)
        pick = jnp.min(jnp.where(work == m, eidx, float(N_EXPERTS)), axis=0, keepdims=True)
        work = jnp.where(eidx == pick, -jnp.inf, work)
        tops.append(m)
        picks.append(pick)
    ex = [jnp.exp(v - tops[0]) for v in tops]
    den = ex[0] + ex[1] + ex[2] + ex[3]
    gates = [e / den for e in ex]

    sel = jnp.zeros((N_EXPERTS, n), F32)
    for pick in picks:
        sel = sel + (eidx == pick).astype(F32)
    rank = jnp.dot(sel.astype(BF16), tri_t_ref[...], preferred_element_type=F32)
    cnts = [jnp.sum(sel[:, t * tile:(t + 1) * tile], axis=1, keepdims=True) for t in range(sub)]
    padded = jnp.concatenate(
        [jnp.broadcast_to(jnp.ceil(c * (1.0 / CHUNK)) * CHUNK, (N_EXPERTS, LANES)) for c in cnts], axis=1)
    seg = jnp.dot(tri_e_ref[...], padded.astype(BF16), preferred_element_type=F32)
    dest = jnp.concatenate([seg[:, t * LANES:t * LANES + 1] + rank[:, t * tile:(t + 1) * tile]
                            for t in range(sub)], axis=1)
    slots = [jnp.sum(jnp.where(eidx == pick, dest, 0.0), axis=0, keepdims=True) for pick in picks]

    ridx = lax.broadcasted_iota(jnp.int32, (TILE_ROWS, tile), 0).astype(F32)
    for t in range(sub):
        cols = slice(t * tile, (t + 1) * tile)
        cnt_ref[t] = jnp.broadcast_to(cnts[t], (N_EXPERTS, LANES))
        slot_ref[t] = jnp.concatenate([v[:, cols] for v in slots + gates], axis=0)
        hit = ridx == slots[0][:, cols]
        for s in slots[1:]:
            hit = jnp.logical_or(hit, ridx == s[:, cols])
        perm = jnp.where(hit, 1.0, 0.0).astype(BF16)
        hs_ref[pl.ds(t * TILE_ROWS, TILE_ROWS), :] = _pack_rows(
            jnp.dot(perm, hh[t * tile:(t + 1) * tile, :], preferred_element_type=F32))


def _route(x2d, attn, pool, gm, sh, sc, w_out, norm_ffn, w_router_t, b_router, hs_prev, *, tile, sub,
           rows_per_mod, tile0, extra_tiles):
    n = x2d.shape[0]
    nt = n // tile
    assert nt % sub == 0 and extra_tiles % sub == 0
    own_steps = nt // sub
    steps = own_steps + extra_tiles // sub
    mrows = gm.shape[1]
    assert mrows == 1 or (sub == 1 and mrows == tile)
    last = lambda i: jnp.minimum(i, own_steps - 1)
    mod_spec = pl.BlockSpec((None, mrows, D_MODEL), lambda i: ((last(i) * sub * tile) // rows_per_mod, 0, 0))
    const = lambda shape: pl.BlockSpec(shape, lambda i: (0,) * len(shape))
    row = lambda w: pl.BlockSpec((sub * tile, w), lambda i: (last(i), 0))
    tok = jnp.arange(sub * tile)
    tri_t = jnp.logical_and(tok[:, None] < tok[None, :],
                            tok[:, None] // tile == tok[None, :] // tile).astype(BF16)
    tri_e = (jnp.arange(N_EXPERTS)[None, :] < jnp.arange(N_EXPERTS)[:, None]).astype(BF16)
    in_specs = [row(D_MODEL), row(ATTN_WIDTH), row(POOL_WIDTH), mod_spec, mod_spec, mod_spec,
                const((D_MODEL, D_MODEL)), const((1, D_MODEL)), const((N_EXPERTS, D_MODEL)),
                const((N_EXPERTS, 1)), const((sub * tile, sub * tile)), const((N_EXPERTS, N_EXPERTS))]
    args = [x2d, attn, pool, gm, sh, sc, w_out, norm_ffn.reshape(1, D_MODEL), w_router_t,
            b_router.reshape(N_EXPERTS, 1), tri_t, tri_e]
    n_in = len(args)
    kern = _route_kernel
    aliases = {}
    hs_rows = (tile0 + steps * sub) * TILE_ROWS
    assert tile0 % sub == 0
    if hs_prev is not None:
        in_specs.append(pl.BlockSpec(memory_space=pl.ANY))
        args.append(hs_prev)
        aliases = {n_in: 1}
        kern = lambda *refs: _route_kernel(*refs[:n_in], *refs[n_in + 1:])
        hs_rows = hs_prev.shape[0]
    return pl.pallas_call(
        kern,
        grid=(steps,),
        in_specs=in_specs,
        out_specs=[row(D_MODEL),
                   pl.BlockSpec((sub * TILE_ROWS, ROW_W), lambda i: (i + tile0 // sub, 0)),
                   pl.BlockSpec((sub, 2 * TOP_K, tile), lambda i: (last(i), 0, 0)),
                   pl.BlockSpec((sub, N_EXPERTS, LANES), lambda i: (last(i), 0, 0))],
        out_shape=[jax.ShapeDtypeStruct((n, D_MODEL), F32),
                   jax.ShapeDtypeStruct((hs_rows, ROW_W), jnp.int32),
                   jax.ShapeDtypeStruct((nt, 2 * TOP_K, tile), F32),
                   jax.ShapeDtypeStruct((nt, N_EXPERTS, LANES), F32)],
        input_output_aliases=aliases,
        compiler_params=_cparams(("arbitrary",)),
        name="route",
    )(*args)


def _moe_kernel(src_ref, be_ref, nxt_ref, nb_ref, hs_hbm, w1_hbm, b1_ref, w2_hbm, b2_ref, out_hbm,
                lhs, obuf, w1s, w2s, w1c, w2c, sem_in, sem_out, sem_w, *, scratch_chunk):
    nb = nb_ref[0]

    def weight_copies(e):
        return (pltpu.make_async_copy(w1_hbm.at[e], w1s, sem_w.at[0]),
                pltpu.make_async_copy(w2_hbm.at[e], w2s, sem_w.at[1]))

    def chunk_rows(c):
        return pl.ds(pl.multiple_of(c * CHUNK, CHUNK), CHUNK)

    def start_in(blk, s):
        for j in range(BLOCK_CHUNKS):
            c = src_ref[blk * BLOCK_CHUNKS + j]
            c = jnp.where(c < 0, ZERO_CHUNK, c)
            pltpu.make_async_copy(hs_hbm.at[chunk_rows(c)], lhs.at[s, pl.ds(j * CHUNK, CHUNK)],
                                  sem_in.at[s]).start()

    def wait_in(s):
        pltpu.make_async_copy(hs_hbm.at[pl.ds(0, MOE_BLOCK)], lhs.at[s], sem_in.at[s]).wait()

    def start_out(blk, s):
        for j in range(BLOCK_CHUNKS):
            c = src_ref[blk * BLOCK_CHUNKS + j]
            c = jnp.where(c < 0, scratch_chunk + s * BLOCK_CHUNKS + j, c)
            pltpu.make_async_copy(obuf.at[s, pl.ds(j * CHUNK, CHUNK)],
                                  out_hbm.at[chunk_rows(c), pl.ds(0, PACKED_W)], sem_out.at[s]).start()

    def wait_out(s):
        pltpu.make_async_copy(obuf.at[s], out_hbm.at[pl.ds(0, MOE_BLOCK), pl.ds(0, PACKED_W)],
                              sem_out.at[s]).wait()

    @pl.when(nb > 0)
    def _():
        start_in(0, 0)
        for cp in weight_copies(be_ref[0]):
            cp.start()

    def block(b, carry):
        slot = b % 2
        e = be_ref[b]

        @pl.when(jnp.logical_or(b == 0, e != be_ref[jnp.maximum(b - 1, 0)]))
        def _():
            for cp in weight_copies(e):
                cp.wait()
            w1c[...] = w1s[...].astype(BF16)
            w2c[...] = w2s[...].astype(BF16)
            nxt = nxt_ref[b]

            @pl.when(nxt >= 0)
            def _():
                for cp in weight_copies(nxt):
                    cp.start()

        wait_in(slot)

        @pl.when(b >= 2)
        def _():
            wait_out(slot)

        start_in(b + 1, 1 - slot)

        def ffn(nrows):
            xh, xl = _unpack_rows(lhs[slot, 0:nrows, :])
            gu = (jnp.dot(xh, w1c[:PACKED_W, :], preferred_element_type=F32)
                  + jnp.dot(xl, w1c[PACKED_W:, :], preferred_element_type=F32) + b1_ref[e])
            gate = jnp.minimum(gu[:, :D_FF], SWIGLU_LIMIT)
            up = jnp.clip(gu[:, D_FF:], -SWIGLU_LIMIT, SWIGLU_LIMIT)
            act = (up + 1.0) * (gate * jax.nn.sigmoid(SWIGLU_ALPHA * gate))
            y = jnp.dot(act.astype(BF16), w2c[...], preferred_element_type=F32) + b2_ref[e]
            obuf[slot, 0:nrows, :] = _pack_rows(y.astype(BF16).astype(F32))

        real = nb_ref[1 + b]

        @pl.when(real > BLOCK_CHUNKS // 2)
        def _():
            ffn(MOE_BLOCK)

        @pl.when(real <= BLOCK_CHUNKS // 2)
        def _():
            ffn(MOE_BLOCK // 2)

        start_out(b, slot)
        return carry

    obuf[...] = jnp.zeros(obuf.shape, jnp.int32)
    lax.fori_loop(0, nb, block, 0)

    @pl.when(nb > 0)
    def _():
        last_slot = (nb - 1) % 2
        wait_in(1 - last_slot)

        @pl.when(nb >= 2)
        def _():
            wait_out(1 - last_slot)
        wait_out(last_slot)


def _moe(hs, src, block_e, next_e, nblocks, w1, b1, w2, b2, scratch_chunk):
    full = lambda shape: pl.BlockSpec(shape, lambda i, *_: (0,) * len(shape))
    hbm = pl.BlockSpec(memory_space=pl.ANY)
    return pl.pallas_call(
        functools.partial(_moe_kernel, scratch_chunk=scratch_chunk),
        grid_spec=pltpu.PrefetchScalarGridSpec(
            num_scalar_prefetch=4,
            grid=(1,),
            in_specs=[hbm, hbm, full((N_EXPERTS, 1, 2 * D_FF)), hbm, full((N_EXPERTS, 1, D_MODEL))],
            out_specs=hbm,
            scratch_shapes=[pltpu.VMEM((2, MOE_BLOCK, ROW_W), jnp.int32),
                            pltpu.VMEM((2, MOE_BLOCK, PACKED_W), jnp.int32),
                            pltpu.VMEM((D_MODEL, 2 * D_FF), F32),
                            pltpu.VMEM((D_FF, D_MODEL), F32),
                            pltpu.VMEM((D_MODEL, 2 * D_FF), BF16),
                            pltpu.VMEM((D_FF, D_MODEL), BF16),
                            pltpu.SemaphoreType.DMA((2,)),
                            pltpu.SemaphoreType.DMA((2,)),
                            pltpu.SemaphoreType.DMA((2,))]),
        out_shape=jax.ShapeDtypeStruct(hs.shape, jnp.int32),
        input_output_aliases={4: 0},
        compiler_params=_cparams(("arbitrary",)),
        name="moe_experts",
    )(src, block_e, next_e, nblocks, hs, w1, b1.reshape(N_EXPERTS, 1, 2 * D_FF), w2,
      b2.reshape(N_EXPERTS, 1, D_MODEL))


def _plan(cnt, n_blocks_max):
    nt = cnt.shape[0]
    nch = (cnt + (CHUNK - 1)) // CHUNK
    lstart = jnp.cumsum(nch, axis=1) - nch
    ne = jnp.sum(nch, axis=0)
    nbe = (ne + (BLOCK_CHUNKS - 1)) // BLOCK_CHUNKS
    bend = jnp.cumsum(nbe)
    nblocks = bend[-1]
    gstart = (bend - nbe)[None, :] * BLOCK_CHUNKS + (jnp.cumsum(nch, axis=0) - nch)
    s0 = jnp.arange(nt, dtype=jnp.int32)[:, None] * TILE_CHUNKS + lstart
    blk = jnp.arange(n_blocks_max + 1, dtype=jnp.int32)
    be = jnp.sum((blk[:, None] >= bend[None, :]).astype(jnp.int32), axis=1)
    be_last = jnp.sum((nblocks - 1 >= bend).astype(jnp.int32))
    be = jnp.minimum(be, be_last).astype(jnp.int32)
    eid = jnp.arange(N_EXPERTS, dtype=jnp.int32)
    strips = jnp.stack([gstart.T, nch.T, s0.T])
    mine = (be[:, None] == eid[None, :])[None, :, :, None]
    gs_b, nc_b, s0_b = jnp.sum(jnp.where(mine, strips[:, None], 0), axis=2)
    c = (blk[:, None] * BLOCK_CHUNKS + jnp.arange(BLOCK_CHUNKS, dtype=jnp.int32)[None, :])[:, :, None]
    inside = jnp.logical_and(c >= gs_b[:, None, :], c < (gs_b + nc_b)[:, None, :])
    src = (jnp.sum(jnp.where(inside, (s0_b - gs_b)[:, None, :] + c + 1, 0), axis=2) - 1).reshape(-1)
    be = be[:n_blocks_max]
    later = jnp.logical_and(eid[None, :] > be[:, None], (nbe > 0)[None, :])
    nxt = jnp.min(jnp.where(later, eid[None, :], N_EXPERTS), axis=1)
    nxt = jnp.where(nxt == N_EXPERTS, -1, nxt).astype(jnp.int32)
    real = jnp.sum((src.reshape(-1, BLOCK_CHUNKS) >= 0).astype(jnp.int32), axis=1)
    return src, be, nxt, jnp.concatenate([nblocks.reshape(1), real]).astype(jnp.int32)


def _combine_kernel(o_ref, slot_ref, x2_ref, gf_ref, y_ref):
    sub = slot_ref.shape[0]
    tile = x2_ref.shape[0] // sub
    ridx = lax.broadcasted_iota(jnp.int32, (TILE_ROWS, tile), 0).astype(F32)
    tn = functools.partial(lax.dot_general, dimension_numbers=(((0,), (0,)), ((), ())),
                           preferred_element_type=F32)
    for t in range(sub):
        gmat = jnp.zeros((TILE_ROWS, tile), F32)
        for k in range(TOP_K):
            gmat = jnp.where(ridx == slot_ref[t, k:k + 1, :], slot_ref[t, TOP_K + k:TOP_K + k + 1, :], gmat)
        gb = gmat.astype(BF16)
        oh, ol = _unpack_rows(o_ref[pl.ds(t * TILE_ROWS, TILE_ROWS), :])
        y = jnp.concatenate([tn(gb, oh), tn(gb, ol)], axis=1)
        rows = pl.ds(t * tile, tile)
        gf = gf_ref[...] if gf_ref.shape[0] == 1 else gf_ref[rows, :]
        y_ref[rows, :] = x2_ref[rows, :] + gf * y


def _combine(outs, slots, x2, gf, *, tile, sub, rows_per_mod, tile0):
    n = x2.shape[0]
    mrows = gf.shape[1]
    assert n % (sub * tile) == 0 and tile0 % sub == 0 and (mrows == 1 or sub == 1)
    return pl.pallas_call(
        _combine_kernel,
        grid=(n // (sub * tile),),
        in_specs=[pl.BlockSpec((sub * TILE_ROWS, PACKED_W), lambda i: (i + tile0 // sub, 0)),
                  pl.BlockSpec((sub, 2 * TOP_K, tile), lambda i: (i, 0, 0)),
                  pl.BlockSpec((sub * tile, D_MODEL), lambda i: (i, 0)),
                  pl.BlockSpec((None, mrows, D_MODEL), lambda i: ((i * sub * tile) // rows_per_mod, 0, 0))],
        out_specs=pl.BlockSpec((sub * tile, D_MODEL), lambda i: (i, 0)),
        out_shape=jax.ShapeDtypeStruct((n, D_MODEL), F32),
        compiler_params=_cparams(("parallel",)),
        name="combine",
    )(outs, slots, x2, gf)


def kernel(x_prompt, x_sample, cache_k, cache_v, state_pool, c_prompt, c_sample, rel_bias, norm_mix, w_ada,
           b_ada, w_in, q_norm, k_norm, sinks, w_pool, pool_scale, w_out, norm_ffn, w_router, b_router,
           w1, b1, w2, b2):
    depth = w_in.shape[0]
    assert depth == 1
    l = 0
    bp, sp, _ = x_prompt.shape
    bs = x_sample.shape[0]
    assert x_sample.shape[1] == 1 and sp % SORT_TILE == 0 and bs <= SORT_TILE
    n_p = bp * sp
    tiles_p = n_p // SORT_TILE
    max_chunks = tiles_p * (SORT_TILE * TOP_K // CHUNK + N_EXPERTS) + (bs * TOP_K // CHUNK + N_EXPERTS)
    n_blocks_max = -(-max_chunks // BLOCK_CHUNKS) + N_EXPERTS

    mod = _modulation(jnp.concatenate([c_prompt, c_sample], axis=0), w_ada[l], b_ada[l])
    mod_p = [m.reshape(bp, 1, D_MODEL) for m in jnp.split(mod[:bp], 6, axis=-1)]
    mod_s = [m.reshape(1, bs, D_MODEL) for m in jnp.split(mod[bp:], 6, axis=-1)]

    head = jnp.arange(ATTN_WIDTH) // HEAD_DIM
    bd = (head[:, None] == head[None, :]).astype(BF16)
    w_in_b = w_in[l].astype(BF16)
    w_out_b = w_out[l].astype(BF16)
    w_pool_b = w_pool[l].astype(BF16)
    w_router_t = w_router[l].T

    xp = x_prompt.reshape(n_p, D_MODEL)
    q, k, v, u = _mixer_inputs(xp, mod_p[0], mod_p[1], norm_mix[l], w_in_b, bd, q_norm[l], k_norm[l],
                               tile=512, rows_per_mod=sp, precise=False)
    k3 = k.reshape(bp, sp, KV_WIDTH)
    v3 = v.reshape(bp, sp, KV_WIDTH)
    u3 = u.reshape(bp, sp, POOL_WIDTH)
    attn = _attn_prompt(q.reshape(bp, sp, ATTN_WIDTH), k3, v3, sinks[l], rel_bias)
    pool = _pool_prompt(u3, w_pool_b, pool_scale[l])
    keep = min(WINDOW, sp)
    nkp = k3[:, -keep:].reshape(bp, keep, N_KV_HEADS, HEAD_DIM)
    nvp = v3[:, -keep:].reshape(bp, keep, N_KV_HEADS, HEAD_DIM)
    if sp >= POOL_BUF:
        npp = u3[:, -POOL_BUF:]
    else:
        npp = jnp.concatenate([jnp.zeros((bp, POOL_BUF - sp, POOL_WIDTH), F32), u3], axis=1)
    x2_p, hs, slots_p, cnt_p = _route(
        xp, attn.reshape(n_p, ATTN_WIDTH), pool.reshape(n_p, POOL_WIDTH), mod_p[2], mod_p[3], mod_p[4],
        w_out_b, norm_ffn[l], w_router_t, b_router[l], None,
        tile=SORT_TILE, sub=ROUTE_SUB, rows_per_mod=sp, tile0=0, extra_tiles=max(2, ROUTE_SUB))

    xs = x_sample.reshape(bs, D_MODEL)
    qs, ks, vs, us = _mixer_inputs(xs, mod_s[0], mod_s[1], norm_mix[l], w_in[l], bd, q_norm[l], k_norm[l],
                                   tile=bs, rows_per_mod=bs, precise=True)
    wbuf = cache_k.shape[2]
    attn_s, nks, nvs = _attn_sample(qs, ks, vs, cache_k[l].reshape(bs, wbuf, KV_WIDTH),
                                    cache_v[l].reshape(bs, wbuf, KV_WIDTH), sinks[l], rel_bias)
    pool_s, nps_t = _pool_sample(jnp.swapaxes(state_pool[l], 0, 1), us, w_pool_b, pool_scale[l])
    x2_s, hs, slots_s, cnt_s = _route(
        xs, attn_s.astype(BF16), pool_s, mod_s[2], mod_s[3], mod_s[4], w_out_b, norm_ffn[l],
        w_router_t, b_router[l], hs, tile=bs, sub=1, rows_per_mod=bs, tile0=tiles_p, extra_tiles=0)

    cnt = jnp.concatenate([cnt_p[:, :, 0], cnt_s[:, :, 0]], axis=0).astype(jnp.int32)
    src, block_e, next_e, nblocks = _plan(cnt, n_blocks_max)
    outs = _moe(hs, src, block_e, next_e, nblocks, w1[l], b1[l], w2[l], b2[l],
                scratch_chunk=(tiles_p + 1) * TILE_CHUNKS)

    y_p = _combine(outs, slots_p, x2_p, mod_p[5], tile=SORT_TILE, sub=COMBINE_SUB, rows_per_mod=sp, tile0=0)
    y_s = _combine(outs, slots_s, x2_s, mod_s[5], tile=bs, sub=1, rows_per_mod=bs, tile0=tiles_p)

    return (y_p.reshape(bp, sp, D_MODEL), y_s.reshape(bs, 1, D_MODEL),
            nkp[None], nvp[None], npp[None],
            nks.reshape(1, bs, wbuf, N_KV_HEADS, HEAD_DIM), nvs.reshape(1, bs, wbuf, N_KV_HEADS, HEAD_DIM),
            jnp.swapaxes(nps_t, 0, 1)[None])
```

```python
import functools
import math

import jax
import jax.numpy as jnp
from jax import lax
from jax.experimental import pallas as pl
from jax.experimental.pallas import tpu as pltpu

F32 = jnp.float32
BF16 = jnp.bfloat16

D_MODEL = 1024
HEAD_DIM = 64
N_HEADS = 8
N_KV_HEADS = 2
GROUP = N_HEADS // N_KV_HEADS
ATTN_WIDTH = N_HEADS * HEAD_DIM
KV_WIDTH = N_KV_HEADS * HEAD_DIM
POOL_WIDTH = D_MODEL - ATTN_WIDTH
POOL_WINDOWS = (2, 4, 8, 16)
POOL_GROUP = POOL_WIDTH // len(POOL_WINDOWS)
POOL_BUF = max(POOL_WINDOWS) - 1
IN_WIDTH = ATTN_WIDTH + 2 * KV_WIDTH + POOL_WIDTH
WINDOW = 128
ATTN_BLOCK = 128
N_BUCKETS = 32
MAX_EXACT = 16
REL_MAX_DIST = 128
N_EXPERTS = 32
TOP_K = 4
D_FF = D_MODEL
SWIGLU_LIMIT = 7.0
SWIGLU_ALPHA = 1.702
EPS = 1e-6
NEG_INF = -1e30
PAST_LEN = 16384

LANES = 128
SUBLANES = 8
VMEM_LIMIT = 56 * 1024 * 1024

ATTN_QB = 2

SORT_TILE = 256
ROUTE_SUB = 4
COMBINE_SUB = 2
CHUNK = SUBLANES
TILE_ROWS = -(-(SORT_TILE * TOP_K + N_EXPERTS * (CHUNK - 1)) // LANES) * LANES
TILE_CHUNKS = TILE_ROWS // CHUNK
MOE_BLOCK = 256
BLOCK_CHUNKS = MOE_BLOCK // CHUNK
WEIGHT_DMA_PRIORITY = 1
PACKED_W = D_MODEL // 2
ROW_W = PACKED_W
ZERO_CHUNK = TILE_CHUNKS - 1


def _bdot(a, b):
    return jnp.dot(a.astype(BF16), b.astype(BF16), preferred_element_type=F32)


def _split(a):
    hi = a.astype(BF16)
    lo = (a - hi.astype(F32)).astype(BF16)
    return hi, lo


def _dot3(a, b):
    ah, al = _split(a)
    bh, bl = _split(b)
    d = functools.partial(jnp.dot, preferred_element_type=F32)
    return d(ah, bh) + d(al, bh) + d(ah, bl)


def _pack_rows(x):
    bits = lax.bitcast_convert_type(x, jnp.int32)
    return bits[:, :PACKED_W] | lax.shift_right_logical(bits[:, PACKED_W:], 16)


def _unpack_rows(w):
    hi = lax.bitcast_convert_type(w & jnp.int32(-65536), F32)
    lo = lax.bitcast_convert_type(lax.shift_left(w, 16), F32)
    return hi.astype(BF16), lo.astype(BF16)


def _rms(x, g):
    return x * lax.rsqrt(jnp.mean(x * x, axis=-1, keepdims=True) + EPS) * g


def _cparams(sem, **kw):
    return pltpu.CompilerParams(dimension_semantics=sem, vmem_limit_bytes=VMEM_LIMIT, **kw)


def _ada_kernel(c_ref, w_ref, b_ref, o_ref):
    c = c_ref[...]
    s = c * jax.nn.sigmoid(c)
    o_ref[...] = _dot3(s, w_ref[...]) + b_ref[...]


def _modulation(c, w_ada, b_ada):
    rows = c.shape[0]
    n = w_ada.shape[1]
    tn = 1024
    return pl.pallas_call(
        _ada_kernel,
        grid=(n // tn,),
        in_specs=[pl.BlockSpec((rows, D_MODEL), lambda j: (0, 0)),
                  pl.BlockSpec((D_MODEL, tn), lambda j: (0, j)),
                  pl.BlockSpec((1, tn), lambda j: (0, j))],
        out_specs=pl.BlockSpec((rows, tn), lambda j: (0, j)),
        out_shape=jax.ShapeDtypeStruct((rows, n), F32),
        compiler_params=_cparams(("parallel",)),
        name="modulation",
    )(c, w_ada, b_ada.reshape(1, n))


def _head_rms(t, bd, g, precise):
    if precise:
        hi, lo = _split(t * t)
        ss = jnp.dot(hi, bd, preferred_element_type=F32) + jnp.dot(lo, bd, preferred_element_type=F32)
    else:
        ss = _bdot(t * t, bd)
    return t * lax.rsqrt(ss * (1.0 / HEAD_DIM) + EPS) * g


def _mixin_kernel(x_ref, sh_ref, sc_ref, g_ref, w_ref, bd_ref, qn_ref, kn_ref,
                  q_ref, k_ref, v_ref, u_ref, *, precise):
    h = _rms(x_ref[...], g_ref[...]) * (1.0 + sc_ref[...]) + sh_ref[...]
    z = _dot3(h, w_ref[...]) if precise else _bdot(h, w_ref[...])
    q = z[:, :ATTN_WIDTH]
    k = z[:, ATTN_WIDTH:ATTN_WIDTH + KV_WIDTH]
    bd = bd_ref[...]
    q = _head_rms(q, bd, qn_ref[...], precise)
    k = _head_rms(k, bd[:KV_WIDTH, :KV_WIDTH], kn_ref[...], precise)
    q_ref[...] = (q * (HEAD_DIM ** -0.5)).astype(BF16)
    k_ref[...] = k
    v_ref[...] = z[:, ATTN_WIDTH + KV_WIDTH:ATTN_WIDTH + 2 * KV_WIDTH]
    u_ref[...] = z[:, ATTN_WIDTH + 2 * KV_WIDTH:]


def _mixer_inputs(x2d, shift, scale, norm_mix, w_in, bd, q_norm, k_norm, *, tile, rows_per_mod, precise):
    n = x2d.shape[0]
    mrows = shift.shape[1]
    mod_spec = pl.BlockSpec((None, mrows, D_MODEL), lambda i: ((i * tile) // rows_per_mod, 0, 0))
    const = lambda shape: pl.BlockSpec(shape, lambda i: (0,) * len(shape))
    row = lambda w: pl.BlockSpec((tile, w), lambda i: (i, 0))
    return pl.pallas_call(
        functools.partial(_mixin_kernel, precise=precise),
        grid=(n // tile,),
        in_specs=[row(D_MODEL), mod_spec, mod_spec, const((1, D_MODEL)), const((D_MODEL, IN_WIDTH)),
                  const((ATTN_WIDTH, ATTN_WIDTH)), const((1, ATTN_WIDTH)), const((1, KV_WIDTH))],
        out_specs=[row(ATTN_WIDTH), row(KV_WIDTH), row(KV_WIDTH), row(POOL_WIDTH)],
        out_shape=[jax.ShapeDtypeStruct((n, ATTN_WIDTH), BF16),
                   jax.ShapeDtypeStruct((n, KV_WIDTH), F32),
                   jax.ShapeDtypeStruct((n, KV_WIDTH), F32),
                   jax.ShapeDtypeStruct((n, POOL_WIDTH), F32)],
        compiler_params=_cparams(("parallel",)),
        name="mixer_inputs",
    )(x2d, shift, scale, norm_mix.reshape(1, D_MODEL), w_in, bd,
      jnp.tile(q_norm, N_HEADS).reshape(1, ATTN_WIDTH), jnp.tile(k_norm, N_KV_HEADS).reshape(1, KV_WIDTH))


def _t5_bucket(rel):
    n = jnp.maximum(rel, 0)
    nf = jnp.maximum(n, 1).astype(F32)
    large = MAX_EXACT + (jnp.log(nf / MAX_EXACT) / math.log(REL_MAX_DIST / MAX_EXACT)
                         * (N_BUCKETS - MAX_EXACT)).astype(jnp.int32)
    large = jnp.minimum(large, N_BUCKETS - 1)
    return jnp.where(n < MAX_EXACT, n, large)


def _bias_table(rel, rel_table):
    bucket = _t5_bucket(rel)
    table = rel_table.astype(F32)
    ids = jnp.arange(N_BUCKETS, dtype=bucket.dtype).reshape((N_BUCKETS, 1) + (1,) * rel.ndim)
    onehot = bucket[None, None] == ids
    bias = jnp.sum(jnp.where(onehot, table.reshape(table.shape + (1,) * rel.ndim), 0.0), axis=0)
    valid = (rel >= 0) & (rel < WINDOW)
    return jnp.where(valid[None], bias, NEG_INF)


def _attn_prompt_kernel(sink_ref, q_ref, kp_ref, kc_ref, vp_ref, vc_ref, bias_ref, o_ref):
    first = pl.program_id(1) == 0
    kk = jnp.concatenate([kp_ref[...], kc_ref[...]], axis=0).astype(BF16)
    vv = jnp.concatenate([vp_ref[...], vc_ref[...]], axis=0).astype(BF16)
    key = lax.broadcasted_iota(jnp.int32, (2 * ATTN_BLOCK, 1), 0)
    no_prev = jnp.logical_and(first, key < ATTN_BLOCK)
    lane = lax.broadcasted_iota(jnp.int32, (1, N_HEADS * ATTN_BLOCK), 1)
    sink = jnp.zeros((1, N_HEADS * ATTN_BLOCK), F32)
    for h in range(N_HEADS):
        sink = jnp.where(lane // ATTN_BLOCK == h, sink_ref[h], sink)
    contract = lambda a, b, dims: lax.dot_general(a, b, (dims, ((), ())), preferred_element_type=F32)
    part = GROUP * ATTN_BLOCK
    for i in range(ATTN_QB):
        q = q_ref[i * ATTN_BLOCK:(i + 1) * ATTN_BLOCK, :]
        keys = slice(i * ATTN_BLOCK, (i + 2) * ATTN_BLOCK)
        scores = []
        for kv in range(N_KV_HEADS):
            heads = range(kv * GROUP, (kv + 1) * GROUP)
            qg = jnp.concatenate([q[:, h * HEAD_DIM:(h + 1) * HEAD_DIM] for h in heads], axis=0)
            scores.append(contract(kk[keys, kv * HEAD_DIM:(kv + 1) * HEAD_DIM], qg, ((1,), (1,))))
        s = jnp.concatenate(scores, axis=1) + bias_ref[...]
        if i == 0:
            s = jnp.where(no_prev, NEG_INF, s)
        m = jnp.maximum(jnp.max(s, axis=0, keepdims=True), sink)
        p = jnp.exp(s - m)
        denom = jnp.sum(p, axis=0, keepdims=True) + jnp.exp(sink - m)
        p = p.astype(BF16)
        halves = [contract(vv[keys, kv * HEAD_DIM:(kv + 1) * HEAD_DIM], p[:, kv * part:(kv + 1) * part],
                           ((0,), (0,))) / denom[:, kv * part:(kv + 1) * part]
                  for kv in range(N_KV_HEADS)]
        o_t = jnp.concatenate(halves, axis=0)
        per_g = [o_t[:, g * ATTN_BLOCK:(g + 1) * ATTN_BLOCK].T for g in range(GROUP)]
        out = [t[:, kv * HEAD_DIM:(kv + 1) * HEAD_DIM] for kv in range(N_KV_HEADS) for t in per_g]
        o_ref[i * ATTN_BLOCK:(i + 1) * ATTN_BLOCK, :] = jnp.concatenate(out, axis=-1).astype(BF16)


def _attn_prompt(q, k, v, sinks, rel_table):
    b, s = q.shape[:2]
    qrows = ATTN_QB * ATTN_BLOCK
    assert s % qrows == 0
    qi = jnp.arange(ATTN_BLOCK, dtype=jnp.int32)[:, None]
    si = jnp.arange(2 * ATTN_BLOCK, dtype=jnp.int32)[None, :]
    bias = _bias_table(qi + ATTN_BLOCK - si, rel_table)
    bias = bias.reshape(N_HEADS * ATTN_BLOCK, 2 * ATTN_BLOCK).T
    cur = lambda w: pl.BlockSpec((None, qrows, w), lambda i, j, *_: (i, j, 0))
    prev = lambda w: pl.BlockSpec((None, ATTN_BLOCK, w),
                                  lambda i, j, *_: (i, jnp.maximum(j * ATTN_QB - 1, 0), 0))
    return pl.pallas_call(
        _attn_prompt_kernel,
        grid_spec=pltpu.PrefetchScalarGridSpec(
            num_scalar_prefetch=1,
            grid=(b, s // qrows),
            in_specs=[cur(ATTN_WIDTH), prev(KV_WIDTH), cur(KV_WIDTH), prev(KV_WIDTH), cur(KV_WIDTH),
                      pl.BlockSpec(bias.shape, lambda i, j, *_: (0, 0))],
            out_specs=cur(ATTN_WIDTH)),
        out_shape=jax.ShapeDtypeStruct((b, s, ATTN_WIDTH), BF16),
        compiler_params=_cparams(("parallel", "parallel")),
        name="attn_prompt",
    )(sinks.astype(F32), q, k, k, v, v, bias)


def _attn_sample_kernel(sink_ref, q_ref, kc_ref, vc_ref, kn_ref, vn_ref, bias_ref, bnew_ref,
                        o_ref, nk_ref, nv_ref):
    kc = kc_ref[...]
    vc = vc_ref[...]
    kn = kn_ref[...]
    vn = vn_ref[...]
    w = kc.shape[1]
    pos = lax.broadcasted_iota(jnp.int32, kc.shape, 1)
    nk_ref[...] = jnp.where(pos == w - 1, kn[:, None, :], pltpu.roll(kc, w - 1, 1))
    nv_ref[...] = jnp.where(pos == w - 1, vn[:, None, :], pltpu.roll(vc, w - 1, 1))
    gi = lax.broadcasted_iota(jnp.int32, (1, GROUP, 1), 1)
    for kv in range(N_KV_HEADS):
        sl = slice(kv * HEAD_DIM, (kv + 1) * HEAD_DIM)
        qg = q_ref[:, kv]
        s = jnp.einsum('bgd,bsd->bgs', qg, kc[:, :, sl].astype(BF16), preferred_element_type=F32)
        s = s + bias_ref[kv][None]
        s_new = jnp.sum(qg.astype(F32) * kn[:, None, sl], axis=-1, keepdims=True) + bnew_ref[kv][None]
        sink = jnp.zeros((1, GROUP, 1), F32)
        for g in range(GROUP):
            sink = jnp.where(gi == g, sink_ref[kv * GROUP + g], sink)
        m = jnp.maximum(jnp.maximum(jnp.max(s, axis=-1, keepdims=True), s_new), sink)
        p = jnp.exp(s - m)
        p_new = jnp.exp(s_new - m)
        denom = jnp.sum(p, axis=-1, keepdims=True) + p_new + jnp.exp(sink - m)
        o = jnp.einsum('bgs,bsd->bgd', p.astype(BF16), vc[:, :, sl].astype(BF16), preferred_element_type=F32)
        o = o + p_new * vn[:, None, sl]
        o_ref[:, kv] = o / denom


def _attn_sample(q, k_new, v_new, cache_k, cache_v, sinks, rel_table, *, tile=16):
    bd, w = cache_k.shape[:2]
    rel = w - jnp.arange(w, dtype=jnp.int32)
    bias = _bias_table(rel, rel_table).reshape(N_KV_HEADS, GROUP, w)
    bnew = _bias_table(jnp.zeros((1,), jnp.int32), rel_table).reshape(N_KV_HEADS, GROUP, 1)
    q4 = q.reshape(bd, N_KV_HEADS, GROUP, HEAD_DIM)
    spec4 = pl.BlockSpec((tile, N_KV_HEADS, GROUP, HEAD_DIM), lambda i, *_: (i, 0, 0, 0))
    cache = pl.BlockSpec((tile, w, KV_WIDTH), lambda i, *_: (i, 0, 0))
    new = pl.BlockSpec((tile, KV_WIDTH), lambda i, *_: (i, 0))
    const3 = lambda a: pl.BlockSpec(a.shape, lambda i, *_: (0, 0, 0))
    o, nk, nv = pl.pallas_call(
        _attn_sample_kernel,
        grid_spec=pltpu.PrefetchScalarGridSpec(
            num_scalar_prefetch=1,
            grid=(bd // tile,),
            in_specs=[spec4, cache, cache, new, new, const3(bias), const3(bnew)],
            out_specs=[spec4, cache, cache]),
        out_shape=[jax.ShapeDtypeStruct(q4.shape, F32),
                   jax.ShapeDtypeStruct(cache_k.shape, F32),
                   jax.ShapeDtypeStruct(cache_v.shape, F32)],
        compiler_params=_cparams(("parallel",)),
        name="attn_sample",
    )(sinks.astype(F32), q4, cache_k, cache_v, k_new, v_new, bias, bnew)
    return o.reshape(bd, ATTN_WIDTH), nk, nv


def _pool_project(d_groups, wp_ref, ps_ref):
    out = [_bdot(d, wp_ref[g]) for g, d in enumerate(d_groups)]
    return (jnp.concatenate(out, axis=-1) * ps_ref[...]).astype(BF16)


def _pool_prompt_kernel(u_ref, halo_ref, wp_ref, ps_ref, o_ref, ext, lv):
    t = pl.program_id(1)
    tile = u_ref.shape[0]
    lead, hb = SUBLANES, 2 * SUBLANES
    halo = halo_ref[...]
    ext[0:lead, :] = jnp.zeros((lead, ext.shape[1]), F32)
    ext[lead:lead + hb, :] = jnp.where(t == 0, jnp.zeros_like(halo), halo)
    ext[lead + hb:, :] = u_ref[...]
    lv[:, 0:lead, :] = jnp.zeros((lv.shape[0], lead, lv.shape[2]), F32)
    pos = t * tile + lax.broadcasted_iota(jnp.int32, (tile, 1), 0)
    n = hb + tile
    ds = []
    for g, w in enumerate(POOL_WINDOWS):
        sl = slice(g * POOL_GROUP, (g + 1) * POOL_GROUP)
        acc = ext[lead:lead + n, sl] + ext[lead - 1:lead - 1 + n, sl]
        span, level = 2, 0
        while span < w:
            lv[level, lead:lead + n, :] = acc
            acc = acc + lv[level, lead - span:lead - span + n, :]
            span, level = 2 * span, level + 1
        cnt = jnp.minimum(pos + 1, w).astype(F32)
        ds.append(acc[hb:] / cnt - ext[lead + hb:lead + hb + tile, sl])
    o_ref[...] = _pool_project(ds, wp_ref, ps_ref)


def _pool_prompt(u, w_pool, pool_scale, *, tile=1024):
    b, s, c = u.shape
    hb = 2 * SUBLANES
    return pl.pallas_call(
        _pool_prompt_kernel,
        grid=(b, s // tile),
        in_specs=[pl.BlockSpec((None, tile, c), lambda i, t: (i, t, 0)),
                  pl.BlockSpec((None, hb, c), lambda i, t: (i, jnp.maximum(t * (tile // hb) - 1, 0), 0)),
                  pl.BlockSpec(w_pool.shape, lambda i, t: (0, 0, 0)),
                  pl.BlockSpec((1, c), lambda i, t: (0, 0))],
        out_specs=pl.BlockSpec((None, tile, c), lambda i, t: (i, t, 0)),
        out_shape=jax.ShapeDtypeStruct((b, s, c), BF16),
        scratch_shapes=[pltpu.VMEM((SUBLANES + hb + tile, c), F32),
                        pltpu.VMEM((len(POOL_WINDOWS) - 1, SUBLANES + hb + tile, POOL_GROUP), F32)],
        compiler_params=_cparams(("parallel", "parallel")),
        name="pool_prompt",
    )(u, u, w_pool, pool_scale.reshape(1, c))


def _pool_sample_kernel(st_ref, u_ref, wp_ref, ps_ref, o_ref, ns_ref):
    u = u_ref[...]
    ns_ref[0:POOL_BUF - 1] = st_ref[1:POOL_BUF]
    ns_ref[POOL_BUF - 1] = u
    ds = []
    for g, w in enumerate(POOL_WINDOWS):
        sl = slice(g * POOL_GROUP, (g + 1) * POOL_GROUP)
        acc = u[:, sl]
        for j in range(1, w):
            acc = acc + st_ref[POOL_BUF - j][:, sl]
        cnt = float(min(PAST_LEN + 1, w))
        ds.append(acc / cnt - u[:, sl])
    o_ref[...] = _pool_project(ds, wp_ref, ps_ref)


def _pool_sample(state_t, u, w_pool, pool_scale):
    nb, bd, c = state_t.shape
    full = lambda a: pl.BlockSpec(a.shape, lambda: (0,) * a.ndim)
    ps = pool_scale.reshape(1, c)
    return pl.pallas_call(
        _pool_sample_kernel,
        in_specs=[full(state_t), full(u), full(w_pool), full(ps)],
        out_specs=[pl.BlockSpec((bd, c), lambda: (0, 0)), full(state_t)],
        out_shape=[jax.ShapeDtypeStruct((bd, c), BF16), jax.ShapeDtypeStruct(state_t.shape, F32)],
        compiler_params=pltpu.CompilerParams(vmem_limit_bytes=VMEM_LIMIT),
        name="pool_sample",
    )(state_t, u, w_pool, ps)


def _route_kernel(x_ref, attn_ref, pool_ref, gm_ref, sh_ref, sc_ref, wo_ref, nf_ref, wr_ref, br_ref,
                  tri_t_ref, tri_e_ref,
                  x2_ref, hs_ref, slot_ref, cnt_ref):
    sub = slot_ref.shape[0]
    n = x_ref.shape[0]
    tile = n // sub
    mix = (jnp.dot(attn_ref[...], wo_ref[:ATTN_WIDTH, :], preferred_element_type=F32)
           + jnp.dot(pool_ref[...], wo_ref[ATTN_WIDTH:, :], preferred_element_type=F32))
    x2 = x_ref[...] + gm_ref[...] * mix
    x2_ref[...] = x2
    h = _rms(x2, nf_ref[...]) * (1.0 + sc_ref[...]) + sh_ref[...]

    hh, hl = _split(h)
    wh, wl = _split(wr_ref[...])
    nt = functools.partial(lax.dot_general, dimension_numbers=(((1,), (1,)), ((), ())),
                           preferred_element_type=F32)
    logits = nt(wh, hh) + nt(wl, hh) + nt(wh, hl) + br_ref[...]

    eidx = lax.broadcasted_iota(jnp.int32, (N_EXPERTS, n), 0).astype(F32)
    work = logits
    tops, picks = [], []
    for _ in range(TOP_K):
        m = jnp.max(work, axis=0, keepdims=True)
        pick = jnp.min(jnp.where(work == m, eidx, float(N_EXPERTS)), axis=0, keepdims=True)
        work = jnp.where(eidx == pick, -jnp.inf, work)
        tops.append(m)
        picks.append(pick)
    ex = [jnp.exp(v - tops[0]) for v in tops]
    den = ex[0] + ex[1] + ex[2] + ex[3]
    gates = [e / den for e in ex]

    sel = jnp.zeros((N_EXPERTS, n), F32)
    for pick in picks:
        sel = sel + (eidx == pick).astype(F32)
    rank = jnp.dot(sel.astype(BF16), tri_t_ref[...], preferred_element_type=F32)
    cnts = [jnp.sum(sel[:, t * tile:(t + 1) * tile], axis=1, keepdims=True) for t in range(sub)]
    padded = jnp.concatenate(
        [jnp.broadcast_to(jnp.ceil(c * (1.0 / CHUNK)) * CHUNK, (N_EXPERTS, LANES)) for c in cnts], axis=1)
    seg = jnp.dot(tri_e_ref[...], padded.astype(BF16), preferred_element_type=F32)
    dest = jnp.concatenate([seg[:, t * LANES:t * LANES + 1] + rank[:, t * tile:(t + 1) * tile]
                            for t in range(sub)], axis=1)
    slots = [jnp.sum(jnp.where(eidx == pick, dest, 0.0), axis=0, keepdims=True) for pick in picks]

    ridx = lax.broadcasted_iota(jnp.int32, (TILE_ROWS, tile), 0).astype(F32)
    for t in range(sub):
        cols = slice(t * tile, (t + 1) * tile)
        cnt_ref[t] = jnp.broadcast_to(cnts[t], (N_EXPERTS, LANES))
        slot_ref[t] = jnp.concatenate([v[:, cols] for v in slots + gates], axis=0)
        hit = ridx == slots[0][:, cols]
        for s in slots[1:]:
            hit = jnp.logical_or(hit, ridx == s[:, cols])
        perm = jnp.where(hit, 1.0, 0.0).astype(BF16)
        hs_ref[pl.ds(t * TILE_ROWS, TILE_ROWS), :] = _pack_rows(
            jnp.dot(perm, hh[t * tile:(t + 1) * tile, :], preferred_element_type=F32))


def _route(x2d, attn, pool, gm, sh, sc, w_out, norm_ffn, w_router_t, b_router, hs_prev, *, tile, sub,
           rows_per_mod, tile0, extra_tiles):
    n = x2d.shape[0]
    nt = n // tile
    assert nt % sub == 0 and extra_tiles % sub == 0
    own_steps = nt // sub
    steps = own_steps + extra_tiles // sub
    mrows = gm.shape[1]
    assert mrows == 1 or (sub == 1 and mrows == tile)
    last = lambda i: jnp.minimum(i, own_steps - 1)
    mod_spec = pl.BlockSpec((None, mrows, D_MODEL), lambda i: ((last(i) * sub * tile) // rows_per_mod, 0, 0))
    const = lambda shape: pl.BlockSpec(shape, lambda i: (0,) * len(shape))
    row = lambda w: pl.BlockSpec((sub * tile, w), lambda i: (last(i), 0))
    tok = jnp.arange(sub * tile)
    tri_t = jnp.logical_and(tok[:, None] < tok[None, :],
                            tok[:, None] // tile == tok[None, :] // tile).astype(BF16)
    tri_e = (jnp.arange(N_EXPERTS)[None, :] < jnp.arange(N_EXPERTS)[:, None]).astype(BF16)
    in_specs = [row(D_MODEL), row(ATTN_WIDTH), row(POOL_WIDTH), mod_spec, mod_spec, mod_spec,
                const((D_MODEL, D_MODEL)), const((1, D_MODEL)), const((N_EXPERTS, D_MODEL)),
                const((N_EXPERTS, 1)), const((sub * tile, sub * tile)), const((N_EXPERTS, N_EXPERTS))]
    args = [x2d, attn, pool, gm, sh, sc, w_out, norm_ffn.reshape(1, D_MODEL), w_router_t,
            b_router.reshape(N_EXPERTS, 1), tri_t, tri_e]
    n_in = len(args)
    kern = _route_kernel
    aliases = {}
    hs_rows = (tile0 + steps * sub) * TILE_ROWS
    assert tile0 % sub == 0
    if hs_prev is not None:
        in_specs.append(pl.BlockSpec(memory_space=pl.ANY))
        args.append(hs_prev)
        aliases = {n_in: 1}
        kern = lambda *refs: _route_kernel(*refs[:n_in], *refs[n_in + 1:])
        hs_rows = hs_prev.shape[0]
    return pl.pallas_call(
        kern,
        grid=(steps,),
        in_specs=in_specs,
        out_specs=[row(D_MODEL),
                   pl.BlockSpec((sub * TILE_ROWS, ROW_W), lambda i: (i + tile0 // sub, 0)),
                   pl.BlockSpec((sub, 2 * TOP_K, tile), lambda i: (last(i), 0, 0)),
                   pl.BlockSpec((sub, N_EXPERTS, LANES), lambda i: (last(i), 0, 0))],
        out_shape=[jax.ShapeDtypeStruct((n, D_MODEL), F32),
                   jax.ShapeDtypeStruct((hs_rows, ROW_W), jnp.int32),
                   jax.ShapeDtypeStruct((nt, 2 * TOP_K, tile), F32),
                   jax.ShapeDtypeStruct((nt, N_EXPERTS, LANES), F32)],
        input_output_aliases=aliases,
        compiler_params=_cparams(("arbitrary",)),
        name="route",
    )(*args)


def _moe_kernel(src_ref, be_ref, nxt_ref, nb_ref, hs_hbm, w1_hbm, b1_ref, w2_hbm, b2_ref, out_hbm,
                lhs, obuf, w1s, w2s, w1c, w2c, sem_in, sem_out, sem_w, *, scratch_chunk):
    nb = nb_ref[0]

    def weight_copies(e):
        return (pltpu.make_async_copy(w1_hbm.at[e], w1s, sem_w.at[0]),
                pltpu.make_async_copy(w2_hbm.at[e], w2s, sem_w.at[1]))

    def chunk_rows(c):
        return pl.ds(pl.multiple_of(c * CHUNK, CHUNK), CHUNK)

    def start_in(blk, s):
        for j in range(BLOCK_CHUNKS):
            c = src_ref[blk * BLOCK_CHUNKS + j]
            c = jnp.where(c < 0, ZERO_CHUNK, c)
            pltpu.make_async_copy(hs_hbm.at[chunk_rows(c)], lhs.at[s, pl.ds(j * CHUNK, CHUNK)],
                                  sem_in.at[s]).start()

    def wait_in(s):
        pltpu.make_async_copy(hs_hbm.at[pl.ds(0, MOE_BLOCK)], lhs.at[s], sem_in.at[s]).wait()

    def start_out(blk, s):
        for j in range(BLOCK_CHUNKS):
            c = src_ref[blk * BLOCK_CHUNKS + j]
            c = jnp.where(c < 0, scratch_chunk + s * BLOCK_CHUNKS + j, c)
            pltpu.make_async_copy(obuf.at[s, pl.ds(j * CHUNK, CHUNK)],
                                  out_hbm.at[chunk_rows(c), pl.ds(0, PACKED_W)], sem_out.at[s]).start()

    def wait_out(s):
        pltpu.make_async_copy(obuf.at[s], out_hbm.at[pl.ds(0, MOE_BLOCK), pl.ds(0, PACKED_W)],
                              sem_out.at[s]).wait()

    @pl.when(nb > 0)
    def _():
        start_in(0, 0)
        for cp in weight_copies(be_ref[0]):
            cp.start(priority=WEIGHT_DMA_PRIORITY)

    def block(b, carry):
        slot = b % 2
        e = be_ref[b]

        @pl.when(jnp.logical_or(b == 0, e != be_ref[jnp.maximum(b - 1, 0)]))
        def _():
            for cp in weight_copies(e):
                cp.wait()
            w1c[...] = w1s[...].astype(BF16)
            w2c[...] = w2s[...].astype(BF16)
            nxt = nxt_ref[b]

            @pl.when(nxt >= 0)
            def _():
                for cp in weight_copies(nxt):
                    cp.start(priority=WEIGHT_DMA_PRIORITY)

        wait_in(slot)

        @pl.when(b >= 2)
        def _():
            wait_out(slot)

        start_in(b + 1, 1 - slot)

        def ffn(nrows):
            xh, xl = _unpack_rows(lhs[slot, 0:nrows, :])
            gu = (jnp.dot(xh, w1c[:PACKED_W, :], preferred_element_type=F32)
                  + jnp.dot(xl, w1c[PACKED_W:, :], preferred_element_type=F32) + b1_ref[e])
            gate = jnp.minimum(gu[:, :D_FF], SWIGLU_LIMIT)
            up = jnp.clip(gu[:, D_FF:], -SWIGLU_LIMIT, SWIGLU_LIMIT)
            act = (up + 1.0) * (gate * jax.nn.sigmoid(SWIGLU_ALPHA * gate))
            y = jnp.dot(act.astype(BF16), w2c[...], preferred_element_type=F32) + b2_ref[e]
            obuf[slot, 0:nrows, :] = _pack_rows(y.astype(BF16).astype(F32))

        real = nb_ref[1 + b]

        @pl.when(real > BLOCK_CHUNKS // 2)
        def _():
            ffn(MOE_BLOCK)

        @pl.when(real <= BLOCK_CHUNKS // 2)
        def _():
            ffn(MOE_BLOCK // 2)

        start_out(b, slot)
        return carry

    obuf[...] = jnp.zeros(obuf.shape, jnp.int32)
    lax.fori_loop(0, nb, block, 0)

    @pl.when(nb > 0)
    def _():
        last_slot = (nb - 1) % 2
        wait_in(1 - last_slot)

        @pl.when(nb >= 2)
        def _():
            wait_out(1 - last_slot)
        wait_out(last_slot)


def _moe(hs, src, block_e, next_e, nblocks, w1, b1, w2, b2, scratch_chunk):
    full = lambda shape: pl.BlockSpec(shape, lambda i, *_: (0,) * len(shape))
    hbm = pl.BlockSpec(memory_space=pl.ANY)
    return pl.pallas_call(
        functools.partial(_moe_kernel, scratch_chunk=scratch_chunk),
        grid_spec=pltpu.PrefetchScalarGridSpec(
            num_scalar_prefetch=4,
            grid=(1,),
            in_specs=[hbm, hbm, full((N_EXPERTS, 1, 2 * D_FF)), hbm, full((N_EXPERTS, 1, D_MODEL))],
            out_specs=hbm,
            scratch_shapes=[pltpu.VMEM((2, MOE_BLOCK, ROW_W), jnp.int32),
                            pltpu.VMEM((2, MOE_BLOCK, PACKED_W), jnp.int32),
                            pltpu.VMEM((D_MODEL, 2 * D_FF), F32),
                            pltpu.VMEM((D_FF, D_MODEL), F32),
                            pltpu.VMEM((D_MODEL, 2 * D_FF), BF16),
                            pltpu.VMEM((D_FF, D_MODEL), BF16),
                            pltpu.SemaphoreType.DMA((2,)),
                            pltpu.SemaphoreType.DMA((2,)),
                            pltpu.SemaphoreType.DMA((2,))]),
        out_shape=jax.ShapeDtypeStruct(hs.shape, jnp.int32),
        input_output_aliases={4: 0},
        compiler_params=_cparams(("arbitrary",)),
        name="moe_experts",
    )(src, block_e, next_e, nblocks, hs, w1, b1.reshape(N_EXPERTS, 1, 2 * D_FF), w2,
      b2.reshape(N_EXPERTS, 1, D_MODEL))


def _plan(cnt, n_blocks_max):
    nt = cnt.shape[0]
    nch = (cnt + (CHUNK - 1)) // CHUNK
    lstart = jnp.cumsum(nch, axis=1) - nch
    ne = jnp.sum(nch, axis=0)
    nbe = (ne + (BLOCK_CHUNKS - 1)) // BLOCK_CHUNKS
    bend = jnp.cumsum(nbe)
    nblocks = bend[-1]
    gstart = (bend - nbe)[None, :] * BLOCK_CHUNKS + (jnp.cumsum(nch, axis=0) - nch)
    s0 = jnp.arange(nt, dtype=jnp.int32)[:, None] * TILE_CHUNKS + lstart
    blk = jnp.arange(n_blocks_max + 1, dtype=jnp.int32)
    be = jnp.sum((blk[:, None] >= bend[None, :]).astype(jnp.int32), axis=1)
    be_last = jnp.sum((nblocks - 1 >= bend).astype(jnp.int32))
    be = jnp.minimum(be, be_last).astype(jnp.int32)
    eid = jnp.arange(N_EXPERTS, dtype=jnp.int32)
    strips = jnp.stack([gstart.T, nch.T, s0.T])
    mine = (be[:, None] == eid[None, :])[None, :, :, None]
    gs_b, nc_b, s0_b = jnp.sum(jnp.where(mine, strips[:, None], 0), axis=2)
    c = (blk[:, None] * BLOCK_CHUNKS + jnp.arange(BLOCK_CHUNKS, dtype=jnp.int32)[None, :])[:, :, None]
    inside = jnp.logical_and(c >= gs_b[:, None, :], c < (gs_b + nc_b)[:, None, :])
    src = (jnp.sum(jnp.where(inside, (s0_b - gs_b)[:, None, :] + c + 1, 0), axis=2) - 1).reshape(-1)
    be = be[:n_blocks_max]
    later = jnp.logical_and(eid[None, :] > be[:, None], (nbe > 0)[None, :])
    nxt = jnp.min(jnp.where(later, eid[None, :], N_EXPERTS), axis=1)
    nxt = jnp.where(nxt == N_EXPERTS, -1, nxt).astype(jnp.int32)
    real = jnp.sum((src.reshape(-1, BLOCK_CHUNKS) >= 0).astype(jnp.int32), axis=1)
    return src, be, nxt, jnp.concatenate([nblocks.reshape(1), real]).astype(jnp.int32)


def _combine_kernel(o_ref, slot_ref, x2_ref, gf_ref, y_ref):
    sub = slot_ref.shape[0]
    tile = x2_ref.shape[0] // sub
    ridx = lax.broadcasted_iota(jnp.int32, (TILE_ROWS, tile), 0).astype(F32)
    tn = functools.partial(lax.dot_general, dimension_numbers=(((0,), (0,)), ((), ())),
                           preferred_element_type=F32)
    for t in range(sub):
        gmat = jnp.zeros((TILE_ROWS, tile), F32)
        for k in range(TOP_K):
            gmat = jnp.where(ridx == slot_ref[t, k:k + 1, :], slot_ref[t, TOP_K + k:TOP_K + k + 1, :], gmat)
        gb = gmat.astype(BF16)
        oh, ol = _unpack_rows(o_ref[pl.ds(t * TILE_ROWS, TILE_ROWS), :])
        y = jnp.concatenate([tn(gb, oh), tn(gb, ol)], axis=1)
        rows = pl.ds(t * tile, tile)
        gf = gf_ref[...] if gf_ref.shape[0] == 1 else gf_ref[rows, :]
        y_ref[rows, :] = x2_ref[rows, :] + gf * y


def _combine(outs, slots, x2, gf, *, tile, sub, rows_per_mod, tile0):
    n = x2.shape[0]
    mrows = gf.shape[1]
    assert n % (sub * tile) == 0 and tile0 % sub == 0 and (mrows == 1 or sub == 1)
    return pl.pallas_call(
        _combine_kernel,
        grid=(n // (sub * tile),),
        in_specs=[pl.BlockSpec((sub * TILE_ROWS, PACKED_W), lambda i: (i + tile0 // sub, 0)),
                  pl.BlockSpec((sub, 2 * TOP_K, tile), lambda i: (i, 0, 0)),
                  pl.BlockSpec((sub * tile, D_MODEL), lambda i: (i, 0)),
                  pl.BlockSpec((None, mrows, D_MODEL), lambda i: ((i * sub * tile) // rows_per_mod, 0, 0))],
        out_specs=pl.BlockSpec((sub * tile, D_MODEL), lambda i: (i, 0)),
        out_shape=jax.ShapeDtypeStruct((n, D_MODEL), F32),
        compiler_params=_cparams(("parallel",)),
        name="combine",
    )(outs, slots, x2, gf)


def kernel(x_prompt, x_sample, cache_k, cache_v, state_pool, c_prompt, c_sample, rel_bias, norm_mix, w_ada,
           b_ada, w_in, q_norm, k_norm, sinks, w_pool, pool_scale, w_out, norm_ffn, w_router, b_router,
           w1, b1, w2, b2):
    depth = w_in.shape[0]
    assert depth == 1
    l = 0
    bp, sp, _ = x_prompt.shape
    bs = x_sample.shape[0]
    assert x_sample.shape[1] == 1 and sp % SORT_TILE == 0 and bs <= SORT_TILE
    n_p = bp * sp
    tiles_p = n_p // SORT_TILE
    max_chunks = tiles_p * (SORT_TILE * TOP_K // CHUNK + N_EXPERTS) + (bs * TOP_K // CHUNK + N_EXPERTS)
    n_blocks_max = -(-max_chunks // BLOCK_CHUNKS) + N_EXPERTS

    mod = _modulation(jnp.concatenate([c_prompt, c_sample], axis=0), w_ada[l], b_ada[l])
    mod_p = [m.reshape(bp, 1, D_MODEL) for m in jnp.split(mod[:bp], 6, axis=-1)]
    mod_s = [m.reshape(1, bs, D_MODEL) for m in jnp.split(mod[bp:], 6, axis=-1)]

    head = jnp.arange(ATTN_WIDTH) // HEAD_DIM
    bd = (head[:, None] == head[None, :]).astype(BF16)
    w_in_b = w_in[l].astype(BF16)
    w_out_b = w_out[l].astype(BF16)
    w_pool_b = w_pool[l].astype(BF16)
    w_router_t = w_router[l].T

    xp = x_prompt.reshape(n_p, D_MODEL)
    q, k, v, u = _mixer_inputs(xp, mod_p[0], mod_p[1], norm_mix[l], w_in_b, bd, q_norm[l], k_norm[l],
                               tile=512, rows_per_mod=sp, precise=False)
    k3 = k.reshape(bp, sp, KV_WIDTH)
    v3 = v.reshape(bp, sp, KV_WIDTH)
    u3 = u.reshape(bp, sp, POOL_WIDTH)
    attn = _attn_prompt(q.reshape(bp, sp, ATTN_WIDTH), k3, v3, sinks[l], rel_bias)
    pool = _pool_prompt(u3, w_pool_b, pool_scale[l])
    keep = min(WINDOW, sp)
    nkp = k3[:, -keep:].reshape(bp, keep, N_KV_HEADS, HEAD_DIM)
    nvp = v3[:, -keep:].reshape(bp, keep, N_KV_HEADS, HEAD_DIM)
    if sp >= POOL_BUF:
        npp = u3[:, -POOL_BUF:]
    else:
        npp = jnp.concatenate([jnp.zeros((bp, POOL_BUF - sp, POOL_WIDTH), F32), u3], axis=1)
    x2_p, hs, slots_p, cnt_p = _route(
        xp, attn.reshape(n_p, ATTN_WIDTH), pool.reshape(n_p, POOL_WIDTH), mod_p[2], mod_p[3], mod_p[4],
        w_out_b, norm_ffn[l], w_router_t, b_router[l], None,
        tile=SORT_TILE, sub=ROUTE_SUB, rows_per_mod=sp, tile0=0, extra_tiles=max(2, ROUTE_SUB))

    xs = x_sample.reshape(bs, D_MODEL)
    qs, ks, vs, us = _mixer_inputs(xs, mod_s[0], mod_s[1], norm_mix[l], w_in[l], bd, q_norm[l], k_norm[l],
                                   tile=bs, rows_per_mod=bs, precise=True)
    wbuf = cache_k.shape[2]
    attn_s, nks, nvs = _attn_sample(qs, ks, vs, cache_k[l].reshape(bs, wbuf, KV_WIDTH),
                                    cache_v[l].reshape(bs, wbuf, KV_WIDTH), sinks[l], rel_bias)
    pool_s, nps_t = _pool_sample(jnp.swapaxes(state_pool[l], 0, 1), us, w_pool_b, pool_scale[l])
    x2_s, hs, slots_s, cnt_s = _route(
        xs, attn_s.astype(BF16), pool_s, mod_s[2], mod_s[3], mod_s[4], w_out_b, norm_ffn[l],
        w_router_t, b_router[l], hs, tile=bs, sub=1, rows_per_mod=bs, tile0=tiles_p, extra_tiles=0)

    cnt = jnp.concatenate([cnt_p[:, :, 0], cnt_s[:, :, 0]], axis=0).astype(jnp.int32)
    src, block_e, next_e, nblocks = _plan(cnt, n_blocks_max)
    outs = _moe(hs, src, block_e, next_e, nblocks, w1[l], b1[l], w2[l], b2[l],
                scratch_chunk=(tiles_p + 1) * TILE_CHUNKS)

    y_p = _combine(outs, slots_p, x2_p, mod_p[5], tile=SORT_TILE, sub=COMBINE_SUB, rows_per_mod=sp, tile0=0)
    y_s = _combine(outs, slots_s, x2_s, mod_s[5], tile=bs, sub=1, rows_per_mod=bs, tile0=tiles_p)

    return (y_p.reshape(bp, sp, D_MODEL), y_s.reshape(bs, 1, D_MODEL),
            nkp[None], nvp[None], npp[None],
            nks.reshape(1, bs, wbuf, N_KV_HEADS, HEAD_DIM), nvs.reshape(1, bs, wbuf, N_KV_HEADS, HEAD_DIM),
            jnp.swapaxes(nps_t, 0, 1)[None])
```

```python
import functools
import math

import jax
import jax.numpy as jnp
from jax import lax
from jax.experimental import pallas as pl
from jax.experimental.pallas import tpu as pltpu

F32 = jnp.float32
BF16 = jnp.bfloat16

D_MODEL = 1024
HEAD_DIM = 64
N_HEADS = 8
N_KV_HEADS = 2
GROUP = N_HEADS // N_KV_HEADS
ATTN_WIDTH = N_HEADS * HEAD_DIM
KV_WIDTH = N_KV_HEADS * HEAD_DIM
POOL_WIDTH = D_MODEL - ATTN_WIDTH
POOL_WINDOWS = (2, 4, 8, 16)
POOL_GROUP = POOL_WIDTH // len(POOL_WINDOWS)
POOL_BUF = max(POOL_WINDOWS) - 1
IN_WIDTH = ATTN_WIDTH + 2 * KV_WIDTH + POOL_WIDTH
WINDOW = 128
ATTN_BLOCK = 128
N_BUCKETS = 32
MAX_EXACT = 16
REL_MAX_DIST = 128
N_EXPERTS = 32
TOP_K = 4
D_FF = D_MODEL
SWIGLU_LIMIT = 7.0
SWIGLU_ALPHA = 1.702
EPS = 1e-6
NEG_INF = -1e30
PAST_LEN = 16384

LANES = 128
SUBLANES = 8
VMEM_LIMIT = 56 * 1024 * 1024

ATTN_QB = 8

SORT_TILE = 256
ROUTE_SUB = 4
COMBINE_SUB = 2
CHUNK = SUBLANES
TILE_ROWS = -(-(SORT_TILE * TOP_K + N_EXPERTS * (CHUNK - 1)) // LANES) * LANES
TILE_CHUNKS = TILE_ROWS // CHUNK
MOE_BLOCK = 256
BLOCK_CHUNKS = MOE_BLOCK // CHUNK
WEIGHT_DMA_PRIORITY = 1
PACKED_W = D_MODEL // 2
ROW_W = PACKED_W
ZERO_CHUNK = TILE_CHUNKS - 1


def _bdot(a, b):
    return jnp.dot(a.astype(BF16), b.astype(BF16), preferred_element_type=F32)


def _split(a):
    hi = a.astype(BF16)
    lo = (a - hi.astype(F32)).astype(BF16)
    return hi, lo


def _dot3(a, b):
    ah, al = _split(a)
    bh, bl = _split(b)
    d = functools.partial(jnp.dot, preferred_element_type=F32)
    return d(ah, bh) + d(al, bh) + d(ah, bl)


def _pack_rows(x):
    bits = lax.bitcast_convert_type(x, jnp.int32)
    return bits[:, :PACKED_W] | lax.shift_right_logical(bits[:, PACKED_W:], 16)


def _unpack_rows(w):
    hi = lax.bitcast_convert_type(w & jnp.int32(-65536), F32)
    lo = lax.bitcast_convert_type(lax.shift_left(w, 16), F32)
    return hi.astype(BF16), lo.astype(BF16)


def _rms(x, g):
    return x * lax.rsqrt(jnp.mean(x * x, axis=-1, keepdims=True) + EPS) * g


def _cparams(sem, **kw):
    return pltpu.CompilerParams(dimension_semantics=sem, vmem_limit_bytes=VMEM_LIMIT, **kw)


def _ada_kernel(c_ref, w_ref, b_ref, o_ref):
    c = c_ref[...]
    s = c * jax.nn.sigmoid(c)
    o_ref[...] = _dot3(s, w_ref[...]) + b_ref[...]


def _modulation(c, w_ada, b_ada):
    rows = c.shape[0]
    n = w_ada.shape[1]
    tn = 1024
    return pl.pallas_call(
        _ada_kernel,
        grid=(n // tn,),
        in_specs=[pl.BlockSpec((rows, D_MODEL), lambda j: (0, 0)),
                  pl.BlockSpec((D_MODEL, tn), lambda j: (0, j)),
                  pl.BlockSpec((1, tn), lambda j: (0, j))],
        out_specs=pl.BlockSpec((rows, tn), lambda j: (0, j)),
        out_shape=jax.ShapeDtypeStruct((rows, n), F32),
        compiler_params=_cparams(("parallel",)),
        name="modulation",
    )(c, w_ada, b_ada.reshape(1, n))


def _head_rms(t, bd, g, precise):
    if precise:
        hi, lo = _split(t * t)
        ss = jnp.dot(hi, bd, preferred_element_type=F32) + jnp.dot(lo, bd, preferred_element_type=F32)
    else:
        ss = _bdot(t * t, bd)
    return t * lax.rsqrt(ss * (1.0 / HEAD_DIM) + EPS) * g


def _mixin_kernel(x_ref, sh_ref, sc_ref, g_ref, w_ref, bd_ref, qn_ref, kn_ref,
                  q_ref, k_ref, v_ref, u_ref, *, precise):
    h = _rms(x_ref[...], g_ref[...]) * (1.0 + sc_ref[...]) + sh_ref[...]
    z = _dot3(h, w_ref[...]) if precise else _bdot(h, w_ref[...])
    q = z[:, :ATTN_WIDTH]
    k = z[:, ATTN_WIDTH:ATTN_WIDTH + KV_WIDTH]
    bd = bd_ref[...]
    q = _head_rms(q, bd, qn_ref[...], precise)
    k = _head_rms(k, bd[:KV_WIDTH, :KV_WIDTH], kn_ref[...], precise)
    q_ref[...] = (q * (HEAD_DIM ** -0.5)).astype(BF16)
    k_ref[...] = k
    v_ref[...] = z[:, ATTN_WIDTH + KV_WIDTH:ATTN_WIDTH + 2 * KV_WIDTH]
    u_ref[...] = z[:, ATTN_WIDTH + 2 * KV_WIDTH:]


def _mixer_inputs(x2d, shift, scale, norm_mix, w_in, bd, q_norm, k_norm, *, tile, rows_per_mod, precise):
    n = x2d.shape[0]
    mrows = shift.shape[1]
    mod_spec = pl.BlockSpec((None, mrows, D_MODEL), lambda i: ((i * tile) // rows_per_mod, 0, 0))
    const = lambda shape: pl.BlockSpec(shape, lambda i: (0,) * len(shape))
    row = lambda w: pl.BlockSpec((tile, w), lambda i: (i, 0))
    return pl.pallas_call(
        functools.partial(_mixin_kernel, precise=precise),
        grid=(n // tile,),
        in_specs=[row(D_MODEL), mod_spec, mod_spec, const((1, D_MODEL)), const((D_MODEL, IN_WIDTH)),
                  const((ATTN_WIDTH, ATTN_WIDTH)), const((1, ATTN_WIDTH)), const((1, KV_WIDTH))],
        out_specs=[row(ATTN_WIDTH), row(KV_WIDTH), row(KV_WIDTH), row(POOL_WIDTH)],
        out_shape=[jax.ShapeDtypeStruct((n, ATTN_WIDTH), BF16),
                   jax.ShapeDtypeStruct((n, KV_WIDTH), F32),
                   jax.ShapeDtypeStruct((n, KV_WIDTH), F32),
                   jax.ShapeDtypeStruct((n, POOL_WIDTH), F32)],
        compiler_params=_cparams(("parallel",)),
        name="mixer_inputs",
    )(x2d, shift, scale, norm_mix.reshape(1, D_MODEL), w_in, bd,
      jnp.tile(q_norm, N_HEADS).reshape(1, ATTN_WIDTH), jnp.tile(k_norm, N_KV_HEADS).reshape(1, KV_WIDTH))


def _t5_bucket(rel):
    n = jnp.maximum(rel, 0)
    nf = jnp.maximum(n, 1).astype(F32)
    large = MAX_EXACT + (jnp.log(nf / MAX_EXACT) / math.log(REL_MAX_DIST / MAX_EXACT)
                         * (N_BUCKETS - MAX_EXACT)).astype(jnp.int32)
    large = jnp.minimum(large, N_BUCKETS - 1)
    return jnp.where(n < MAX_EXACT, n, large)


def _bias_table(rel, rel_table):
    bucket = _t5_bucket(rel)
    table = rel_table.astype(F32)
    ids = jnp.arange(N_BUCKETS, dtype=bucket.dtype).reshape((N_BUCKETS, 1) + (1,) * rel.ndim)
    onehot = bucket[None, None] == ids
    bias = jnp.sum(jnp.where(onehot, table.reshape(table.shape + (1,) * rel.ndim), 0.0), axis=0)
    valid = (rel >= 0) & (rel < WINDOW)
    return jnp.where(valid[None], bias, NEG_INF)


def _attn_prompt_kernel(sink_ref, q_ref, kp_ref, kc_ref, vp_ref, vc_ref, bias_ref, o_ref):
    first = pl.program_id(1) == 0
    kk = jnp.concatenate([kp_ref[...], kc_ref[...]], axis=0).astype(BF16)
    vv = jnp.concatenate([vp_ref[...], vc_ref[...]], axis=0).astype(BF16)
    key = lax.broadcasted_iota(jnp.int32, (2 * ATTN_BLOCK, 1), 0)
    no_prev = jnp.logical_and(first, key < ATTN_BLOCK)
    lane = lax.broadcasted_iota(jnp.int32, (1, N_HEADS * ATTN_BLOCK), 1)
    sink = jnp.zeros((1, N_HEADS * ATTN_BLOCK), F32)
    for h in range(N_HEADS):
        sink = jnp.where(lane // ATTN_BLOCK == h, sink_ref[h], sink)
    contract = lambda a, b, dims: lax.dot_general(a, b, (dims, ((), ())), preferred_element_type=F32)
    part = GROUP * ATTN_BLOCK
    for i in range(ATTN_QB):
        q = q_ref[i * ATTN_BLOCK:(i + 1) * ATTN_BLOCK, :]
        keys = slice(i * ATTN_BLOCK, (i + 2) * ATTN_BLOCK)
        scores = []
        for kv in range(N_KV_HEADS):
            heads = range(kv * GROUP, (kv + 1) * GROUP)
            qg = jnp.concatenate([q[:, h * HEAD_DIM:(h + 1) * HEAD_DIM] for h in heads], axis=0)
            scores.append(contract(kk[keys, kv * HEAD_DIM:(kv + 1) * HEAD_DIM], qg, ((1,), (1,))))
        s = jnp.concatenate(scores, axis=1) + bias_ref[...]
        if i == 0:
            s = jnp.where(no_prev, NEG_INF, s)
        m = jnp.maximum(jnp.max(s, axis=0, keepdims=True), sink)
        p = jnp.exp(s - m)
        denom = jnp.sum(p, axis=0, keepdims=True) + jnp.exp(sink - m)
        p = p.astype(BF16)
        halves = [contract(vv[keys, kv * HEAD_DIM:(kv + 1) * HEAD_DIM], p[:, kv * part:(kv + 1) * part],
                           ((0,), (0,))) / denom[:, kv * part:(kv + 1) * part]
                  for kv in range(N_KV_HEADS)]
        o_t = jnp.concatenate(halves, axis=0)
        per_g = [o_t[:, g * ATTN_BLOCK:(g + 1) * ATTN_BLOCK].T for g in range(GROUP)]
        out = [t[:, kv * HEAD_DIM:(kv + 1) * HEAD_DIM] for kv in range(N_KV_HEADS) for t in per_g]
        o_ref[i * ATTN_BLOCK:(i + 1) * ATTN_BLOCK, :] = jnp.concatenate(out, axis=-1).astype(BF16)


def _attn_prompt(q, k, v, sinks, rel_table):
    b, s = q.shape[:2]
    qrows = ATTN_QB * ATTN_BLOCK
    assert s % qrows == 0
    qi = jnp.arange(ATTN_BLOCK, dtype=jnp.int32)[:, None]
    si = jnp.arange(2 * ATTN_BLOCK, dtype=jnp.int32)[None, :]
    bias = _bias_table(qi + ATTN_BLOCK - si, rel_table)
    bias = bias.reshape(N_HEADS * ATTN_BLOCK, 2 * ATTN_BLOCK).T
    cur = lambda w: pl.BlockSpec((None, qrows, w), lambda i, j, *_: (i, j, 0))
    prev = lambda w: pl.BlockSpec((None, ATTN_BLOCK, w),
                                  lambda i, j, *_: (i, jnp.maximum(j * ATTN_QB - 1, 0), 0))
    return pl.pallas_call(
        _attn_prompt_kernel,
        grid_spec=pltpu.PrefetchScalarGridSpec(
            num_scalar_prefetch=1,
            grid=(b, s // qrows),
            in_specs=[cur(ATTN_WIDTH), prev(KV_WIDTH), cur(KV_WIDTH), prev(KV_WIDTH), cur(KV_WIDTH),
                      pl.BlockSpec(bias.shape, lambda i, j, *_: (0, 0))],
            out_specs=cur(ATTN_WIDTH)),
        out_shape=jax.ShapeDtypeStruct((b, s, ATTN_WIDTH), BF16),
        compiler_params=_cparams(("parallel", "parallel")),
        name="attn_prompt",
    )(sinks.astype(F32), q, k, k, v, v, bias)


def _attn_sample_kernel(sink_ref, q_ref, kc_ref, vc_ref, kn_ref, vn_ref, bias_ref, bnew_ref,
                        o_ref, nk_ref, nv_ref):
    kc = kc_ref[...]
    vc = vc_ref[...]
    kn = kn_ref[...]
    vn = vn_ref[...]
    w = kc.shape[1]
    pos = lax.broadcasted_iota(jnp.int32, kc.shape, 1)
    nk_ref[...] = jnp.where(pos == w - 1, kn[:, None, :], pltpu.roll(kc, w - 1, 1))
    nv_ref[...] = jnp.where(pos == w - 1, vn[:, None, :], pltpu.roll(vc, w - 1, 1))
    gi = lax.broadcasted_iota(jnp.int32, (1, GROUP, 1), 1)
    for kv in range(N_KV_HEADS):
        sl = slice(kv * HEAD_DIM, (kv + 1) * HEAD_DIM)
        qg = q_ref[:, kv]
        s = jnp.einsum('bgd,bsd->bgs', qg, kc[:, :, sl].astype(BF16), preferred_element_type=F32)
        s = s + bias_ref[kv][None]
        s_new = jnp.sum(qg.astype(F32) * kn[:, None, sl], axis=-1, keepdims=True) + bnew_ref[kv][None]
        sink = jnp.zeros((1, GROUP, 1), F32)
        for g in range(GROUP):
            sink = jnp.where(gi == g, sink_ref[kv * GROUP + g], sink)
        m = jnp.maximum(jnp.maximum(jnp.max(s, axis=-1, keepdims=True), s_new), sink)
        p = jnp.exp(s - m)
        p_new = jnp.exp(s_new - m)
        denom = jnp.sum(p, axis=-1, keepdims=True) + p_new + jnp.exp(sink - m)
        o = jnp.einsum('bgs,bsd->bgd', p.astype(BF16), vc[:, :, sl].astype(BF16), preferred_element_type=F32)
        o = o + p_new * vn[:, None, sl]
        o_ref[:, kv] = o / denom


def _attn_sample(q, k_new, v_new, cache_k, cache_v, sinks, rel_table, *, tile=16):
    bd, w = cache_k.shape[:2]
    rel = w - jnp.arange(w, dtype=jnp.int32)
    bias = _bias_table(rel, rel_table).reshape(N_KV_HEADS, GROUP, w)
    bnew = _bias_table(jnp.zeros((1,), jnp.int32), rel_table).reshape(N_KV_HEADS, GROUP, 1)
    q4 = q.reshape(bd, N_KV_HEADS, GROUP, HEAD_DIM)
    spec4 = pl.BlockSpec((tile, N_KV_HEADS, GROUP, HEAD_DIM), lambda i, *_: (i, 0, 0, 0))
    cache = pl.BlockSpec((tile, w, KV_WIDTH), lambda i, *_: (i, 0, 0))
    new = pl.BlockSpec((tile, KV_WIDTH), lambda i, *_: (i, 0))
    const3 = lambda a: pl.BlockSpec(a.shape, lambda i, *_: (0, 0, 0))
    o, nk, nv = pl.pallas_call(
        _attn_sample_kernel,
        grid_spec=pltpu.PrefetchScalarGridSpec(
            num_scalar_prefetch=1,
            grid=(bd // tile,),
            in_specs=[spec4, cache, cache, new, new, const3(bias), const3(bnew)],
            out_specs=[spec4, cache, cache]),
        out_shape=[jax.ShapeDtypeStruct(q4.shape, F32),
                   jax.ShapeDtypeStruct(cache_k.shape, F32),
                   jax.ShapeDtypeStruct(cache_v.shape, F32)],
        compiler_params=_cparams(("parallel",)),
        name="attn_sample",
    )(sinks.astype(F32), q4, cache_k, cache_v, k_new, v_new, bias, bnew)
    return o.reshape(bd, ATTN_WIDTH), nk, nv


def _pool_project(d_groups, wp_ref, ps_ref):
    out = [_bdot(d, wp_ref[g]) for g, d in enumerate(d_groups)]
    return (jnp.concatenate(out, axis=-1) * ps_ref[...]).astype(BF16)


def _pool_prompt_kernel(u_ref, halo_ref, wp_ref, ps_ref, o_ref, ext, lv):
    t = pl.program_id(1)
    tile = u_ref.shape[0]
    lead, hb = SUBLANES, 2 * SUBLANES
    halo = halo_ref[...]
    ext[0:lead, :] = jnp.zeros((lead, ext.shape[1]), F32)
    ext[lead:lead + hb, :] = jnp.where(t == 0, jnp.zeros_like(halo), halo)
    ext[lead + hb:, :] = u_ref[...]
    lv[:, 0:lead, :] = jnp.zeros((lv.shape[0], lead, lv.shape[2]), F32)
    pos = t * tile + lax.broadcasted_iota(jnp.int32, (tile, 1), 0)
    n = hb + tile
    ds = []
    for g, w in enumerate(POOL_WINDOWS):
        sl = slice(g * POOL_GROUP, (g + 1) * POOL_GROUP)
        acc = ext[lead:lead + n, sl] + ext[lead - 1:lead - 1 + n, sl]
        span, level = 2, 0
        while span < w:
            lv[level, lead:lead + n, :] = acc
            acc = acc + lv[level, lead - span:lead - span + n, :]
            span, level = 2 * span, level + 1
        cnt = jnp.minimum(pos + 1, w).astype(F32)
        ds.append(acc[hb:] / cnt - ext[lead + hb:lead + hb + tile, sl])
    o_ref[...] = _pool_project(ds, wp_ref, ps_ref)


def _pool_prompt(u, w_pool, pool_scale, *, tile=1024):
    b, s, c = u.shape
    hb = 2 * SUBLANES
    return pl.pallas_call(
        _pool_prompt_kernel,
        grid=(b, s // tile),
        in_specs=[pl.BlockSpec((None, tile, c), lambda i, t: (i, t, 0)),
                  pl.BlockSpec((None, hb, c), lambda i, t: (i, jnp.maximum(t * (tile // hb) - 1, 0), 0)),
                  pl.BlockSpec(w_pool.shape, lambda i, t: (0, 0, 0)),
                  pl.BlockSpec((1, c), lambda i, t: (0, 0))],
        out_specs=pl.BlockSpec((None, tile, c), lambda i, t: (i, t, 0)),
        out_shape=jax.ShapeDtypeStruct((b, s, c), BF16),
        scratch_shapes=[pltpu.VMEM((SUBLANES + hb + tile, c), F32),
                        pltpu.VMEM((len(POOL_WINDOWS) - 1, SUBLANES + hb + tile, POOL_GROUP), F32)],
        compiler_params=_cparams(("parallel", "parallel")),
        name="pool_prompt",
    )(u, u, w_pool, pool_scale.reshape(1, c))


def _pool_sample_kernel(st_ref, u_ref, wp_ref, ps_ref, o_ref, ns_ref):
    u = u_ref[...]
    ns_ref[0:POOL_BUF - 1] = st_ref[1:POOL_BUF]
    ns_ref[POOL_BUF - 1] = u
    ds = []
    for g, w in enumerate(POOL_WINDOWS):
        sl = slice(g * POOL_GROUP, (g + 1) * POOL_GROUP)
        acc = u[:, sl]
        for j in range(1, w):
            acc = acc + st_ref[POOL_BUF - j][:, sl]
        cnt = float(min(PAST_LEN + 1, w))
        ds.append(acc / cnt - u[:, sl])
    o_ref[...] = _pool_project(ds, wp_ref, ps_ref)


def _pool_sample(state_t, u, w_pool, pool_scale):
    nb, bd, c = state_t.shape
    full = lambda a: pl.BlockSpec(a.shape, lambda: (0,) * a.ndim)
    ps = pool_scale.reshape(1, c)
    return pl.pallas_call(
        _pool_sample_kernel,
        in_specs=[full(state_t), full(u), full(w_pool), full(ps)],
        out_specs=[pl.BlockSpec((bd, c), lambda: (0, 0)), full(state_t)],
        out_shape=[jax.ShapeDtypeStruct((bd, c), BF16), jax.ShapeDtypeStruct(state_t.shape, F32)],
        compiler_params=pltpu.CompilerParams(vmem_limit_bytes=VMEM_LIMIT),
        name="pool_sample",
    )(state_t, u, w_pool, ps)


def _route_kernel(x_ref, attn_ref, pool_ref, gm_ref, sh_ref, sc_ref, wo_ref, nf_ref, wr_ref, br_ref,
                  tri_t_ref, tri_e_ref,
                  x2_ref, hs_ref, slot_ref, cnt_ref):
    sub = slot_ref.shape[0]
    n = x_ref.shape[0]
    tile = n // sub
    mix = (jnp.dot(attn_ref[...], wo_ref[:ATTN_WIDTH, :], preferred_element_type=F32)
           + jnp.dot(pool_ref[...], wo_ref[ATTN_WIDTH:, :], preferred_element_type=F32))
    x2 = x_ref[...] + gm_ref[...] * mix
    x2_ref[...] = x2
    h = _rms(x2, nf_ref[...]) * (1.0 + sc_ref[...]) + sh_ref[...]

    hh, hl = _split(h)
    wh, wl = _split(wr_ref[...])
    nt = functools.partial(lax.dot_general, dimension_numbers=(((1,), (1,)), ((), ())),
                           preferred_element_type=F32)
    logits = nt(wh, hh) + nt(wl, hh) + nt(wh, hl) + br_ref[...]

    eidx = lax.broadcasted_iota(jnp.int32, (N_EXPERTS, n), 0).astype(F32)
    work = logits
    tops, picks = [], []
    for _ in range(TOP_K):
        m = jnp.max(work, axis=0, keepdims=True)
        pick = jnp.min(jnp.where(work == m, eidx, float(N_EXPERTS)), axis=0, keepdims=True)
        work = jnp.where(eidx == pick, -jnp.inf, work)
        tops.append(m)
        picks.append(pick)
    ex = [jnp.exp(v - tops[0]) for v in tops]
    den = ex[0] + ex[1] + ex[2] + ex[3]
    gates = [e / den for e in ex]

    sel = jnp.zeros((N_EXPERTS, n), F32)
    for pick in picks:
        sel = sel + (eidx == pick).astype(F32)
    selb = sel.astype(BF16)
    rank = jnp.concatenate([jnp.dot(selb[:, t * tile:(t + 1) * tile], tri_t_ref[...], preferred_element_type=F32)
                            for t in range(sub)], axis=1)
    cnts = [jnp.sum(sel[:, t * tile:(t + 1) * tile], axis=1, keepdims=True) for t in range(sub)]
    padded = jnp.concatenate(
        [jnp.broadcast_to(jnp.ceil(c * (1.0 / CHUNK)) * CHUNK, (N_EXPERTS, LANES)) for c in cnts], axis=1)
    seg = jnp.dot(tri_e_ref[...], padded.astype(BF16), preferred_element_type=F32)
    dest = jnp.concatenate([seg[:, t * LANES:t * LANES + 1] + rank[:, t * tile:(t + 1) * tile]
                            for t in range(sub)], axis=1)
    slots = [jnp.sum(jnp.where(eidx == pick, dest, 0.0), axis=0, keepdims=True) for pick in picks]

    ridx = lax.broadcasted_iota(jnp.int32, (TILE_ROWS, tile), 0).astype(F32)
    for t in range(sub):
        cols = slice(t * tile, (t + 1) * tile)
        cnt_ref[t] = jnp.broadcast_to(cnts[t], (N_EXPERTS, LANES))
        slot_ref[t] = jnp.concatenate([v[:, cols] for v in slots + gates], axis=0)
        hit = ridx == slots[0][:, cols]
        for s in slots[1:]:
            hit = jnp.logical_or(hit, ridx == s[:, cols])
        perm = jnp.where(hit, 1.0, 0.0).astype(BF16)
        hs_ref[pl.ds(t * TILE_ROWS, TILE_ROWS), :] = _pack_rows(
            jnp.dot(perm, hh[t * tile:(t + 1) * tile, :], preferred_element_type=F32))


def _route(x2d, attn, pool, gm, sh, sc, w_out, norm_ffn, w_router_t, b_router, hs_prev, *, tile, sub,
           rows_per_mod, tile0, extra_tiles):
    n = x2d.shape[0]
    nt = n // tile
    assert nt % sub == 0 and extra_tiles % sub == 0
    own_steps = nt // sub
    steps = own_steps + extra_tiles // sub
    mrows = gm.shape[1]
    assert mrows == 1 or (sub == 1 and mrows == tile)
    last = lambda i: jnp.minimum(i, own_steps - 1)
    mod_spec = pl.BlockSpec((None, mrows, D_MODEL), lambda i: ((last(i) * sub * tile) // rows_per_mod, 0, 0))
    const = lambda shape: pl.BlockSpec(shape, lambda i: (0,) * len(shape))
    row = lambda w: pl.BlockSpec((sub * tile, w), lambda i: (last(i), 0))
    tri_t = (jnp.arange(tile)[:, None] < jnp.arange(tile)[None, :]).astype(BF16)
    tri_e = (jnp.arange(N_EXPERTS)[None, :] < jnp.arange(N_EXPERTS)[:, None]).astype(BF16)
    in_specs = [row(D_MODEL), row(ATTN_WIDTH), row(POOL_WIDTH), mod_spec, mod_spec, mod_spec,
                const((D_MODEL, D_MODEL)), const((1, D_MODEL)), const((N_EXPERTS, D_MODEL)),
                const((N_EXPERTS, 1)), const((tile, tile)), const((N_EXPERTS, N_EXPERTS))]
    args = [x2d, attn, pool, gm, sh, sc, w_out, norm_ffn.reshape(1, D_MODEL), w_router_t,
            b_router.reshape(N_EXPERTS, 1), tri_t, tri_e]
    n_in = len(args)
    kern = _route_kernel
    aliases = {}
    hs_rows = (tile0 + steps * sub) * TILE_ROWS
    assert tile0 % sub == 0
    if hs_prev is not None:
        in_specs.append(pl.BlockSpec(memory_space=pl.ANY))
        args.append(hs_prev)
        aliases = {n_in: 1}
        kern = lambda *refs: _route_kernel(*refs[:n_in], *refs[n_in + 1:])
        hs_rows = hs_prev.shape[0]
    return pl.pallas_call(
        kern,
        grid=(steps,),
        in_specs=in_specs,
        out_specs=[row(D_MODEL),
                   pl.BlockSpec((sub * TILE_ROWS, ROW_W), lambda i: (i + tile0 // sub, 0)),
                   pl.BlockSpec((sub, 2 * TOP_K, tile), lambda i: (last(i), 0, 0)),
                   pl.BlockSpec((sub, N_EXPERTS, LANES), lambda i: (last(i), 0, 0))],
        out_shape=[jax.ShapeDtypeStruct((n, D_MODEL), F32),
                   jax.ShapeDtypeStruct((hs_rows, ROW_W), jnp.int32),
                   jax.ShapeDtypeStruct((nt, 2 * TOP_K, tile), F32),
                   jax.ShapeDtypeStruct((nt, N_EXPERTS, LANES), F32)],
        input_output_aliases=aliases,
        compiler_params=_cparams(("arbitrary",)),
        name="route",
    )(*args)


def _moe_kernel(src_ref, be_ref, nxt_ref, nb_ref, hs_hbm, w1_hbm, b1_ref, w2_hbm, b2_ref, out_hbm,
                lhs, obuf, w1s, w2s, w1c, w2c, sem_in, sem_out, sem_w, *, scratch_chunk):
    nb = nb_ref[0]

    def weight_copies(e):
        return (pltpu.make_async_copy(w1_hbm.at[e], w1s, sem_w.at[0]),
                pltpu.make_async_copy(w2_hbm.at[e], w2s, sem_w.at[1]))

    def chunk_rows(c):
        return pl.ds(pl.multiple_of(c * CHUNK, CHUNK), CHUNK)

    def start_in(blk, s):
        for j in range(BLOCK_CHUNKS):
            c = src_ref[blk * BLOCK_CHUNKS + j]
            c = jnp.where(c < 0, ZERO_CHUNK, c)
            pltpu.make_async_copy(hs_hbm.at[chunk_rows(c)], lhs.at[s, pl.ds(j * CHUNK, CHUNK)],
                                  sem_in.at[s]).start()

    def wait_in(s):
        pltpu.make_async_copy(hs_hbm.at[pl.ds(0, MOE_BLOCK)], lhs.at[s], sem_in.at[s]).wait()

    def start_out(blk, s):
        for j in range(BLOCK_CHUNKS):
            c = src_ref[blk * BLOCK_CHUNKS + j]
            c = jnp.where(c < 0, scratch_chunk + s * BLOCK_CHUNKS + j, c)
            pltpu.make_async_copy(obuf.at[s, pl.ds(j * CHUNK, CHUNK)],
                                  out_hbm.at[chunk_rows(c), pl.ds(0, PACKED_W)], sem_out.at[s]).start()

    def wait_out(s):
        pltpu.make_async_copy(obuf.at[s], out_hbm.at[pl.ds(0, MOE_BLOCK), pl.ds(0, PACKED_W)],
                              sem_out.at[s]).wait()

    @pl.when(nb > 0)
    def _():
        start_in(0, 0)
        for cp in weight_copies(be_ref[0]):
            cp.start(priority=WEIGHT_DMA_PRIORITY)

    def block(b, carry):
        slot = b % 2
        e = be_ref[b]

        @pl.when(jnp.logical_or(b == 0, e != be_ref[jnp.maximum(b - 1, 0)]))
        def _():
            for cp in weight_copies(e):
                cp.wait()
            w1c[...] = w1s[...].astype(BF16)
            w2c[...] = w2s[...].astype(BF16)
            nxt = nxt_ref[b]

            @pl.when(nxt >= 0)
            def _():
                for cp in weight_copies(nxt):
                    cp.start(priority=WEIGHT_DMA_PRIORITY)

        wait_in(slot)

        @pl.when(b >= 2)
        def _():
            wait_out(slot)

        start_in(b + 1, 1 - slot)

        def ffn(nrows):
            xh, xl = _unpack_rows(lhs[slot, 0:nrows, :])
            gu = (jnp.dot(xh, w1c[:PACKED_W, :], preferred_element_type=F32)
                  + jnp.dot(xl, w1c[PACKED_W:, :], preferred_element_type=F32) + b1_ref[e])
            gate = jnp.minimum(gu[:, :D_FF], SWIGLU_LIMIT)
            up = jnp.clip(gu[:, D_FF:], -SWIGLU_LIMIT, SWIGLU_LIMIT)
            act = (up + 1.0) * (gate * jax.nn.sigmoid(SWIGLU_ALPHA * gate))
            y = jnp.dot(act.astype(BF16), w2c[...], preferred_element_type=F32) + b2_ref[e]
            obuf[slot, 0:nrows, :] = _pack_rows(y.astype(BF16).astype(F32))

        real = nb_ref[1 + b]

        @pl.when(real > BLOCK_CHUNKS // 2)
        def _():
            ffn(MOE_BLOCK)

        @pl.when(real <= BLOCK_CHUNKS // 2)
        def _():
            ffn(MOE_BLOCK // 2)

        start_out(b, slot)
        return carry

    obuf[...] = jnp.zeros(obuf.shape, jnp.int32)
    lax.fori_loop(0, nb, block, 0)

    @pl.when(nb > 0)
    def _():
        last_slot = (nb - 1) % 2
        wait_in(1 - last_slot)

        @pl.when(nb >= 2)
        def _():
            wait_out(1 - last_slot)
        wait_out(last_slot)


def _moe(hs, src, block_e, next_e, nblocks, w1, b1, w2, b2, scratch_chunk):
    full = lambda shape: pl.BlockSpec(shape, lambda i, *_: (0,) * len(shape))
    hbm = pl.BlockSpec(memory_space=pl.ANY)
    return pl.pallas_call(
        functools.partial(_moe_kernel, scratch_chunk=scratch_chunk),
        grid_spec=pltpu.PrefetchScalarGridSpec(
            num_scalar_prefetch=4,
            grid=(1,),
            in_specs=[hbm, hbm, full((N_EXPERTS, 1, 2 * D_FF)), hbm, full((N_EXPERTS, 1, D_MODEL))],
            out_specs=hbm,
            scratch_shapes=[pltpu.VMEM((2, MOE_BLOCK, ROW_W), jnp.int32),
                            pltpu.VMEM((2, MOE_BLOCK, PACKED_W), jnp.int32),
                            pltpu.VMEM((D_MODEL, 2 * D_FF), F32),
                            pltpu.VMEM((D_FF, D_MODEL), F32),
                            pltpu.VMEM((D_MODEL, 2 * D_FF), BF16),
                            pltpu.VMEM((D_FF, D_MODEL), BF16),
                            pltpu.SemaphoreType.DMA((2,)),
                            pltpu.SemaphoreType.DMA((2,)),
                            pltpu.SemaphoreType.DMA((2,))]),
        out_shape=jax.ShapeDtypeStruct(hs.shape, jnp.int32),
        input_output_aliases={4: 0},
        compiler_params=_cparams(("arbitrary",)),
        name="moe_experts",
    )(src, block_e, next_e, nblocks, hs, w1, b1.reshape(N_EXPERTS, 1, 2 * D_FF), w2,
      b2.reshape(N_EXPERTS, 1, D_MODEL))


def _plan(cnt, n_blocks_max):
    nt = cnt.shape[0]
    nch = (cnt + (CHUNK - 1)) // CHUNK
    lstart = jnp.cumsum(nch, axis=1) - nch
    ne = jnp.sum(nch, axis=0)
    nbe = (ne + (BLOCK_CHUNKS - 1)) // BLOCK_CHUNKS
    bend = jnp.cumsum(nbe)
    nblocks = bend[-1]
    gstart = (bend - nbe)[None, :] * BLOCK_CHUNKS + (jnp.cumsum(nch, axis=0) - nch)
    s0 = jnp.arange(nt, dtype=jnp.int32)[:, None] * TILE_CHUNKS + lstart
    blk = jnp.arange(n_blocks_max + 1, dtype=jnp.int32)
    be = jnp.sum((blk[:, None] >= bend[None, :]).astype(jnp.int32), axis=1)
    be_last = jnp.sum((nblocks - 1 >= bend).astype(jnp.int32))
    be = jnp.minimum(be, be_last).astype(jnp.int32)
    eid = jnp.arange(N_EXPERTS, dtype=jnp.int32)
    strips = jnp.stack([gstart.T, nch.T, s0.T])
    mine = (be[:, None] == eid[None, :])[None, :, :, None]
    gs_b, nc_b, s0_b = jnp.sum(jnp.where(mine, strips[:, None], 0), axis=2)
    c = (blk[:, None] * BLOCK_CHUNKS + jnp.arange(BLOCK_CHUNKS, dtype=jnp.int32)[None, :])[:, :, None]
    inside = jnp.logical_and(c >= gs_b[:, None, :], c < (gs_b + nc_b)[:, None, :])
    src = (jnp.sum(jnp.where(inside, (s0_b - gs_b)[:, None, :] + c + 1, 0), axis=2) - 1).reshape(-1)
    be = be[:n_blocks_max]
    later = jnp.logical_and(eid[None, :] > be[:, None], (nbe > 0)[None, :])
    nxt = jnp.min(jnp.where(later, eid[None, :], N_EXPERTS), axis=1)
    nxt = jnp.where(nxt == N_EXPERTS, -1, nxt).astype(jnp.int32)
    real = jnp.sum((src.reshape(-1, BLOCK_CHUNKS) >= 0).astype(jnp.int32), axis=1)
    return src, be, nxt, jnp.concatenate([nblocks.reshape(1), real]).astype(jnp.int32)


def _combine_kernel(o_ref, slot_ref, x2_ref, gf_ref, y_ref):
    sub = slot_ref.shape[0]
    tile = x2_ref.shape[0] // sub
    ridx = lax.broadcasted_iota(jnp.int32, (TILE_ROWS, tile), 0).astype(F32)
    tn = functools.partial(lax.dot_general, dimension_numbers=(((0,), (0,)), ((), ())),
                           preferred_element_type=F32)
    for t in range(sub):
        gmat = jnp.zeros((TILE_ROWS, tile), F32)
        for k in range(TOP_K):
            gmat = jnp.where(ridx == slot_ref[t, k:k + 1, :], slot_ref[t, TOP_K + k:TOP_K + k + 1, :], gmat)
        gb = gmat.astype(BF16)
        oh, ol = _unpack_rows(o_ref[pl.ds(t * TILE_ROWS, TILE_ROWS), :])
        y = jnp.concatenate([tn(gb, oh), tn(gb, ol)], axis=1)
        rows = pl.ds(t * tile, tile)
        gf = gf_ref[...] if gf_ref.shape[0] == 1 else gf_ref[rows, :]
        y_ref[rows, :] = x2_ref[rows, :] + gf * y


def _combine(outs, slots, x2, gf, *, tile, sub, rows_per_mod, tile0):
    n = x2.shape[0]
    mrows = gf.shape[1]
    assert n % (sub * tile) == 0 and tile0 % sub == 0 and (mrows == 1 or sub == 1)
    return pl.pallas_call(
        _combine_kernel,
        grid=(n // (sub * tile),),
        in_specs=[pl.BlockSpec((sub * TILE_ROWS, PACKED_W), lambda i: (i + tile0 // sub, 0)),
                  pl.BlockSpec((sub, 2 * TOP_K, tile), lambda i: (i, 0, 0)),
                  pl.BlockSpec((sub * tile, D_MODEL), lambda i: (i, 0)),
                  pl.BlockSpec((None, mrows, D_MODEL), lambda i: ((i * sub * tile) // rows_per_mod, 0, 0))],
        out_specs=pl.BlockSpec((sub * tile, D_MODEL), lambda i: (i, 0)),
        out_shape=jax.ShapeDtypeStruct((n, D_MODEL), F32),
        compiler_params=_cparams(("parallel",)),
        name="combine",
    )(outs, slots, x2, gf)


def kernel(x_prompt, x_sample, cache_k, cache_v, state_pool, c_prompt, c_sample, rel_bias, norm_mix, w_ada,
           b_ada, w_in, q_norm, k_norm, sinks, w_pool, pool_scale, w_out, norm_ffn, w_router, b_router,
           w1, b1, w2, b2):
    depth = w_in.shape[0]
    assert depth == 1
    l = 0
    bp, sp, _ = x_prompt.shape
    bs = x_sample.shape[0]
    assert x_sample.shape[1] == 1 and sp % SORT_TILE == 0 and bs <= SORT_TILE
    n_p = bp * sp
    tiles_p = n_p // SORT_TILE
    max_chunks = tiles_p * (SORT_TILE * TOP_K // CHUNK + N_EXPERTS) + (bs * TOP_K // CHUNK + N_EXPERTS)
    n_blocks_max = -(-max_chunks // BLOCK_CHUNKS) + N_EXPERTS

    mod = _modulation(jnp.concatenate([c_prompt, c_sample], axis=0), w_ada[l], b_ada[l])
    mod_p = [m.reshape(bp, 1, D_MODEL) for m in jnp.split(mod[:bp], 6, axis=-1)]
    mod_s = [m.reshape(1, bs, D_MODEL) for m in jnp.split(mod[bp:], 6, axis=-1)]

    head = jnp.arange(ATTN_WIDTH) // HEAD_DIM
    bd = (head[:, None] == head[None, :]).astype(BF16)
    w_in_b = w_in[l].astype(BF16)
    w_out_b = w_out[l].astype(BF16)
    w_pool_b = w_pool[l].astype(BF16)
    w_router_t = w_router[l].T

    xp = x_prompt.reshape(n_p, D_MODEL)
    q, k, v, u = _mixer_inputs(xp, mod_p[0], mod_p[1], norm_mix[l], w_in_b, bd, q_norm[l], k_norm[l],
                               tile=512, rows_per_mod=sp, precise=False)
    k3 = k.reshape(bp, sp, KV_WIDTH)
    v3 = v.reshape(bp, sp, KV_WIDTH)
    u3 = u.reshape(bp, sp, POOL_WIDTH)
    attn = _attn_prompt(q.reshape(bp, sp, ATTN_WIDTH), k3, v3, sinks[l], rel_bias)
    pool = _pool_prompt(u3, w_pool_b, pool_scale[l])
    keep = min(WINDOW, sp)
    nkp = k3[:, -keep:].reshape(bp, keep, N_KV_HEADS, HEAD_DIM)
    nvp = v3[:, -keep:].reshape(bp, keep, N_KV_HEADS, HEAD_DIM)
    if sp >= POOL_BUF:
        npp = u3[:, -POOL_BUF:]
    else:
        npp = jnp.concatenate([jnp.zeros((bp, POOL_BUF - sp, POOL_WIDTH), F32), u3], axis=1)
    x2_p, hs, slots_p, cnt_p = _route(
        xp, attn.reshape(n_p, ATTN_WIDTH), pool.reshape(n_p, POOL_WIDTH), mod_p[2], mod_p[3], mod_p[4],
        w_out_b, norm_ffn[l], w_router_t, b_router[l], None,
        tile=SORT_TILE, sub=ROUTE_SUB, rows_per_mod=sp, tile0=0, extra_tiles=max(2, ROUTE_SUB))

    xs = x_sample.reshape(bs, D_MODEL)
    qs, ks, vs, us = _mixer_inputs(xs, mod_s[0], mod_s[1], norm_mix[l], w_in[l], bd, q_norm[l], k_norm[l],
                                   tile=bs, rows_per_mod=bs, precise=True)
    wbuf = cache_k.shape[2]
    attn_s, nks, nvs = _attn_sample(qs, ks, vs, cache_k[l].reshape(bs, wbuf, KV_WIDTH),
                                    cache_v[l].reshape(bs, wbuf, KV_WIDTH), sinks[l], rel_bias)
    pool_s, nps_t = _pool_sample(jnp.swapaxes(state_pool[l], 0, 1), us, w_pool_b, pool_scale[l])
    x2_s, hs, slots_s, cnt_s = _route(
        xs, attn_s.astype(BF16), pool_s, mod_s[2], mod_s[3], mod_s[4], w_out_b, norm_ffn[l],
        w_router_t, b_router[l], hs, tile=bs, sub=1, rows_per_mod=bs, tile0=tiles_p, extra_tiles=0)

    cnt = jnp.concatenate([cnt_p[:, :, 0], cnt_s[:, :, 0]], axis=0).astype(jnp.int32)
    src, block_e, next_e, nblocks = _plan(cnt, n_blocks_max)
    outs = _moe(hs, src, block_e, next_e, nblocks, w1[l], b1[l], w2[l], b2[l],
                scratch_chunk=(tiles_p + 1) * TILE_CHUNKS)

    y_p = _combine(outs, slots_p, x2_p, mod_p[5], tile=SORT_TILE, sub=COMBINE_SUB, rows_per_mod=sp, tile0=0)
    y_s = _combine(outs, slots_s, x2_s, mod_s[5], tile=bs, sub=1, rows_per_mod=bs, tile0=tiles_p)

    return (y_p.reshape(bp, sp, D_MODEL), y_s.reshape(bs, 1, D_MODEL),
            nkp[None], nvp[None], npp[None],
            nks.reshape(1, bs, wbuf, N_KV_HEADS, HEAD_DIM), nvs.reshape(1, bs, wbuf, N_KV_HEADS, HEAD_DIM),
            jnp.swapaxes(nps_t, 0, 1)[None])
```

```python
import functools
import math

import jax
import jax.numpy as jnp
from jax import lax
from jax.experimental import pallas as pl
from jax.experimental.pallas import tpu as pltpu

F32 = jnp.float32
BF16 = jnp.bfloat16

D_MODEL = 1024
HEAD_DIM = 64
N_HEADS = 8
N_KV_HEADS = 2
GROUP = N_HEADS // N_KV_HEADS
ATTN_WIDTH = N_HEADS * HEAD_DIM
KV_WIDTH = N_KV_HEADS * HEAD_DIM
POOL_WIDTH = D_MODEL - ATTN_WIDTH
POOL_WINDOWS = (2, 4, 8, 16)
POOL_GROUP = POOL_WIDTH // len(POOL_WINDOWS)
POOL_BUF = max(POOL_WINDOWS) - 1
IN_WIDTH = ATTN_WIDTH + 2 * KV_WIDTH + POOL_WIDTH
WINDOW = 128
ATTN_BLOCK = 128
N_BUCKETS = 32
MAX_EXACT = 16
REL_MAX_DIST = 128
N_EXPERTS = 32
TOP_K = 4
D_FF = D_MODEL
SWIGLU_LIMIT = 7.0
SWIGLU_ALPHA = 1.702
EPS = 1e-6
NEG_INF = -1e30
PAST_LEN = 16384

LANES = 128
SUBLANES = 8
VMEM_LIMIT = 56 * 1024 * 1024

ATTN_QB = 8

SORT_TILE = 256
ROUTE_SUB = 4
COMBINE_SUB = 2
CHUNK = SUBLANES
TILE_ROWS = -(-(SORT_TILE * TOP_K + N_EXPERTS * (CHUNK - 1)) // LANES) * LANES
TILE_CHUNKS = TILE_ROWS // CHUNK
MOE_BLOCK = 256
BLOCK_CHUNKS = MOE_BLOCK // CHUNK
WEIGHT_DMA_PRIORITY = 1
PACKED_W = D_MODEL // 2
ROW_W = PACKED_W
ZERO_CHUNK = TILE_CHUNKS - 1


def _bdot(a, b):
    return jnp.dot(a.astype(BF16), b.astype(BF16), preferred_element_type=F32)


def _split(a):
    hi = a.astype(BF16)
    lo = (a - hi.astype(F32)).astype(BF16)
    return hi, lo


def _dot3(a, b):
    ah, al = _split(a)
    bh, bl = _split(b)
    d = functools.partial(jnp.dot, preferred_element_type=F32)
    return d(ah, bh) + d(al, bh) + d(ah, bl)


def _pack_rows(x):
    bits = lax.bitcast_convert_type(x, jnp.int32)
    return bits[:, :PACKED_W] | lax.shift_right_logical(bits[:, PACKED_W:], 16)


def _unpack_rows(w):
    hi = lax.bitcast_convert_type(w & jnp.int32(-65536), F32)
    lo = lax.bitcast_convert_type(lax.shift_left(w, 16), F32)
    return hi.astype(BF16), lo.astype(BF16)


def _rms(x, g):
    return x * lax.rsqrt(jnp.mean(x * x, axis=-1, keepdims=True) + EPS) * g


def _cparams(sem, **kw):
    return pltpu.CompilerParams(dimension_semantics=sem, vmem_limit_bytes=VMEM_LIMIT, **kw)


def _ada_kernel(c_ref, w_ref, b_ref, o_ref):
    c = c_ref[...]
    s = c * jax.nn.sigmoid(c)
    o_ref[...] = _dot3(s, w_ref[...]) + b_ref[...]


def _modulation(c, w_ada, b_ada):
    rows = c.shape[0]
    n = w_ada.shape[1]
    tn = D_MODEL
    return pl.pallas_call(
        _ada_kernel,
        grid=(n // tn,),
        in_specs=[pl.BlockSpec((rows, D_MODEL), lambda j: (0, 0)),
                  pl.BlockSpec((D_MODEL, tn), lambda j: (0, j)),
                  pl.BlockSpec((1, tn), lambda j: (0, j))],
        out_specs=pl.BlockSpec((None, rows, tn), lambda j: (j, 0, 0)),
        out_shape=jax.ShapeDtypeStruct((n // tn, rows, tn), F32),
        compiler_params=_cparams(("parallel",)),
        name="modulation",
    )(c, w_ada, b_ada.reshape(1, n))


def _head_rms(t, bd, g, precise):
    if precise:
        hi, lo = _split(t * t)
        ss = jnp.dot(hi, bd, preferred_element_type=F32) + jnp.dot(lo, bd, preferred_element_type=F32)
    else:
        ss = _bdot(t * t, bd)
    return t * lax.rsqrt(ss * (1.0 / HEAD_DIM) + EPS) * g


def _mixin_kernel(x_ref, sh_ref, sc_ref, g_ref, w_ref, bd_ref, qn_ref, kn_ref,
                  q_ref, k_ref, v_ref, u_ref, *, precise):
    h = _rms(x_ref[...], g_ref[...]) * (1.0 + sc_ref[...]) + sh_ref[...]
    z = _dot3(h, w_ref[...]) if precise else _bdot(h, w_ref[...])
    q = z[:, :ATTN_WIDTH]
    k = z[:, ATTN_WIDTH:ATTN_WIDTH + KV_WIDTH]
    bd = bd_ref[...]
    q = _head_rms(q, bd, qn_ref[...], precise)
    k = _head_rms(k, bd[:KV_WIDTH, :KV_WIDTH], kn_ref[...], precise)
    q_ref[...] = (q * (HEAD_DIM ** -0.5)).astype(BF16)
    k_ref[...] = k
    v_ref[...] = z[:, ATTN_WIDTH + KV_WIDTH:ATTN_WIDTH + 2 * KV_WIDTH]
    u_ref[...] = z[:, ATTN_WIDTH + 2 * KV_WIDTH:]


def _mod_spec(mod, k, group_of_step):
    return pl.BlockSpec((None, None, mod.shape[2], D_MODEL), lambda i: (k, group_of_step(i), 0, 0))


def _mixer_inputs(x2d, mod, norm_mix, w_in, bd, q_norm, k_norm, *, tile, rows_per_mod, precise):
    n = x2d.shape[0]
    group = lambda i: (i * tile) // rows_per_mod
    const = lambda shape: pl.BlockSpec(shape, lambda i: (0,) * len(shape))
    row = lambda w: pl.BlockSpec((tile, w), lambda i: (i, 0))
    return pl.pallas_call(
        functools.partial(_mixin_kernel, precise=precise),
        grid=(n // tile,),
        in_specs=[row(D_MODEL), _mod_spec(mod, 0, group), _mod_spec(mod, 1, group), const((1, D_MODEL)),
                  const((D_MODEL, IN_WIDTH)), const((ATTN_WIDTH, ATTN_WIDTH)), const((1, ATTN_WIDTH)),
                  const((1, KV_WIDTH))],
        out_specs=[row(ATTN_WIDTH), row(KV_WIDTH), row(KV_WIDTH), row(POOL_WIDTH)],
        out_shape=[jax.ShapeDtypeStruct((n, ATTN_WIDTH), BF16),
                   jax.ShapeDtypeStruct((n, KV_WIDTH), F32),
                   jax.ShapeDtypeStruct((n, KV_WIDTH), F32),
                   jax.ShapeDtypeStruct((n, POOL_WIDTH), F32)],
        compiler_params=_cparams(("parallel",)),
        name="mixer_inputs",
    )(x2d, mod, mod, norm_mix.reshape(1, D_MODEL), w_in, bd,
      jnp.tile(q_norm, N_HEADS).reshape(1, ATTN_WIDTH), jnp.tile(k_norm, N_KV_HEADS).reshape(1, KV_WIDTH))


def _t5_bucket(rel):
    n = jnp.maximum(rel, 0)
    nf = jnp.maximum(n, 1).astype(F32)
    large = MAX_EXACT + (jnp.log(nf / MAX_EXACT) / math.log(REL_MAX_DIST / MAX_EXACT)
                         * (N_BUCKETS - MAX_EXACT)).astype(jnp.int32)
    large = jnp.minimum(large, N_BUCKETS - 1)
    return jnp.where(n < MAX_EXACT, n, large)


def _bias_table(rel, rel_table):
    bucket = _t5_bucket(rel)
    table = rel_table.astype(F32)
    ids = jnp.arange(N_BUCKETS, dtype=bucket.dtype).reshape((N_BUCKETS, 1) + (1,) * rel.ndim)
    onehot = bucket[None, None] == ids
    bias = jnp.sum(jnp.where(onehot, table.reshape(table.shape + (1,) * rel.ndim), 0.0), axis=0)
    valid = (rel >= 0) & (rel < WINDOW)
    return jnp.where(valid[None], bias, NEG_INF)


def _attn_prompt_kernel(sink_ref, q_ref, kp_ref, kc_ref, vp_ref, vc_ref, bias_ref, o_ref):
    first = pl.program_id(1) == 0
    kk = jnp.concatenate([kp_ref[...], kc_ref[...]], axis=0).astype(BF16)
    vv = jnp.concatenate([vp_ref[...], vc_ref[...]], axis=0).astype(BF16)
    key = lax.broadcasted_iota(jnp.int32, (2 * ATTN_BLOCK, 1), 0)
    no_prev = jnp.logical_and(first, key < ATTN_BLOCK)
    lane = lax.broadcasted_iota(jnp.int32, (1, N_HEADS * ATTN_BLOCK), 1)
    sink = jnp.zeros((1, N_HEADS * ATTN_BLOCK), F32)
    for h in range(N_HEADS):
        sink = jnp.where(lane // ATTN_BLOCK == h, sink_ref[h], sink)
    contract = lambda a, b, dims: lax.dot_general(a, b, (dims, ((), ())), preferred_element_type=F32)
    part = GROUP * ATTN_BLOCK
    for i in range(ATTN_QB):
        q = q_ref[i * ATTN_BLOCK:(i + 1) * ATTN_BLOCK, :]
        keys = slice(i * ATTN_BLOCK, (i + 2) * ATTN_BLOCK)
        scores = []
        for kv in range(N_KV_HEADS):
            heads = range(kv * GROUP, (kv + 1) * GROUP)
            qg = jnp.concatenate([q[:, h * HEAD_DIM:(h + 1) * HEAD_DIM] for h in heads], axis=0)
            scores.append(contract(kk[keys, kv * HEAD_DIM:(kv + 1) * HEAD_DIM], qg, ((1,), (1,))))
        s = jnp.concatenate(scores, axis=1) + bias_ref[...]
        if i == 0:
            s = jnp.where(no_prev, NEG_INF, s)
        m = jnp.maximum(jnp.max(s, axis=0, keepdims=True), sink)
        p = jnp.exp(s - m)
        denom = jnp.sum(p, axis=0, keepdims=True) + jnp.exp(sink - m)
        p = p.astype(BF16)
        halves = [contract(vv[keys, kv * HEAD_DIM:(kv + 1) * HEAD_DIM], p[:, kv * part:(kv + 1) * part],
                           ((0,), (0,))) / denom[:, kv * part:(kv + 1) * part]
                  for kv in range(N_KV_HEADS)]
        o_t = jnp.concatenate(halves, axis=0)
        per_g = [o_t[:, g * ATTN_BLOCK:(g + 1) * ATTN_BLOCK].T for g in range(GROUP)]
        out = [t[:, kv * HEAD_DIM:(kv + 1) * HEAD_DIM] for kv in range(N_KV_HEADS) for t in per_g]
        o_ref[i * ATTN_BLOCK:(i + 1) * ATTN_BLOCK, :] = jnp.concatenate(out, axis=-1).astype(BF16)


def _attn_prompt(q, k, v, sinks, rel_table):
    b, s = q.shape[:2]
    qrows = ATTN_QB * ATTN_BLOCK
    assert s % qrows == 0
    qi = jnp.arange(ATTN_BLOCK, dtype=jnp.int32)[:, None]
    si = jnp.arange(2 * ATTN_BLOCK, dtype=jnp.int32)[None, :]
    bias = _bias_table(qi + ATTN_BLOCK - si, rel_table)
    bias = bias.reshape(N_HEADS * ATTN_BLOCK, 2 * ATTN_BLOCK).T
    cur = lambda w: pl.BlockSpec((None, qrows, w), lambda i, j, *_: (i, j, 0))
    prev = lambda w: pl.BlockSpec((None, ATTN_BLOCK, w),
                                  lambda i, j, *_: (i, jnp.maximum(j * ATTN_QB - 1, 0), 0))
    return pl.pallas_call(
        _attn_prompt_kernel,
        grid_spec=pltpu.PrefetchScalarGridSpec(
            num_scalar_prefetch=1,
            grid=(b, s // qrows),
            in_specs=[cur(ATTN_WIDTH), prev(KV_WIDTH), cur(KV_WIDTH), prev(KV_WIDTH), cur(KV_WIDTH),
                      pl.BlockSpec(bias.shape, lambda i, j, *_: (0, 0))],
            out_specs=cur(ATTN_WIDTH)),
        out_shape=jax.ShapeDtypeStruct((b, s, ATTN_WIDTH), BF16),
        compiler_params=_cparams(("parallel", "parallel")),
        name="attn_prompt",
    )(sinks.astype(F32), q, k, k, v, v, bias)


def _attn_sample_kernel(sink_ref, q_ref, kc_ref, vc_ref, kn_ref, vn_ref, bias_ref, bnew_ref,
                        o_ref, nk_ref, nv_ref):
    kc = kc_ref[...]
    vc = vc_ref[...]
    kn = kn_ref[...]
    vn = vn_ref[...]
    w = kc.shape[1]
    pos = lax.broadcasted_iota(jnp.int32, kc.shape, 1)
    nk_ref[...] = jnp.where(pos == w - 1, kn[:, None, :], pltpu.roll(kc, w - 1, 1))
    nv_ref[...] = jnp.where(pos == w - 1, vn[:, None, :], pltpu.roll(vc, w - 1, 1))
    gi = lax.broadcasted_iota(jnp.int32, (1, GROUP, 1), 1)
    for kv in range(N_KV_HEADS):
        sl = slice(kv * HEAD_DIM, (kv + 1) * HEAD_DIM)
        qg = q_ref[:, kv]
        s = jnp.einsum('bgd,bsd->bgs', qg, kc[:, :, sl].astype(BF16), preferred_element_type=F32)
        s = s + bias_ref[kv][None]
        s_new = jnp.sum(qg.astype(F32) * kn[:, None, sl], axis=-1, keepdims=True) + bnew_ref[kv][None]
        sink = jnp.zeros((1, GROUP, 1), F32)
        for g in range(GROUP):
            sink = jnp.where(gi == g, sink_ref[kv * GROUP + g], sink)
        m = jnp.maximum(jnp.maximum(jnp.max(s, axis=-1, keepdims=True), s_new), sink)
        p = jnp.exp(s - m)
        p_new = jnp.exp(s_new - m)
        denom = jnp.sum(p, axis=-1, keepdims=True) + p_new + jnp.exp(sink - m)
        o = jnp.einsum('bgs,bsd->bgd', p.astype(BF16), vc[:, :, sl].astype(BF16), preferred_element_type=F32)
        o = o + p_new * vn[:, None, sl]
        o_ref[:, kv] = o / denom


def _attn_sample(q, k_new, v_new, cache_k, cache_v, sinks, rel_table, *, tile=16):
    bd, w = cache_k.shape[:2]
    rel = w - jnp.arange(w, dtype=jnp.int32)
    bias = _bias_table(rel, rel_table).reshape(N_KV_HEADS, GROUP, w)
    bnew = _bias_table(jnp.zeros((1,), jnp.int32), rel_table).reshape(N_KV_HEADS, GROUP, 1)
    q4 = q.reshape(bd, N_KV_HEADS, GROUP, HEAD_DIM)
    spec4 = pl.BlockSpec((tile, N_KV_HEADS, GROUP, HEAD_DIM), lambda i, *_: (i, 0, 0, 0))
    cache = pl.BlockSpec((tile, w, KV_WIDTH), lambda i, *_: (i, 0, 0))
    new = pl.BlockSpec((tile, KV_WIDTH), lambda i, *_: (i, 0))
    const3 = lambda a: pl.BlockSpec(a.shape, lambda i, *_: (0, 0, 0))
    o, nk, nv = pl.pallas_call(
        _attn_sample_kernel,
        grid_spec=pltpu.PrefetchScalarGridSpec(
            num_scalar_prefetch=1,
            grid=(bd // tile,),
            in_specs=[spec4, cache, cache, new, new, const3(bias), const3(bnew)],
            out_specs=[spec4, cache, cache]),
        out_shape=[jax.ShapeDtypeStruct(q4.shape, F32),
                   jax.ShapeDtypeStruct(cache_k.shape, F32),
                   jax.ShapeDtypeStruct(cache_v.shape, F32)],
        compiler_params=_cparams(("parallel",)),
        name="attn_sample",
    )(sinks.astype(F32), q4, cache_k, cache_v, k_new, v_new, bias, bnew)
    return o.reshape(bd, ATTN_WIDTH), nk, nv


def _pool_project(d_groups, wp_ref, ps_ref):
    out = [_bdot(d, wp_ref[g]) for g, d in enumerate(d_groups)]
    return (jnp.concatenate(out, axis=-1) * ps_ref[...]).astype(BF16)


def _pool_prompt_kernel(u_ref, halo_ref, wp_ref, ps_ref, o_ref, ext, lv):
    t = pl.program_id(1)
    tile = u_ref.shape[0]
    lead, hb = SUBLANES, 2 * SUBLANES
    halo = halo_ref[...]
    ext[0:lead, :] = jnp.zeros((lead, ext.shape[1]), F32)
    ext[lead:lead + hb, :] = jnp.where(t == 0, jnp.zeros_like(halo), halo)
    ext[lead + hb:, :] = u_ref[...]
    lv[:, 0:lead, :] = jnp.zeros((lv.shape[0], lead, lv.shape[2]), F32)
    pos = t * tile + lax.broadcasted_iota(jnp.int32, (tile, 1), 0)
    n = hb + tile
    ds = []
    for g, w in enumerate(POOL_WINDOWS):
        sl = slice(g * POOL_GROUP, (g + 1) * POOL_GROUP)
        acc = ext[lead:lead + n, sl] + ext[lead - 1:lead - 1 + n, sl]
        span, level = 2, 0
        while span < w:
            lv[level, lead:lead + n, :] = acc
            acc = acc + lv[level, lead - span:lead - span + n, :]
            span, level = 2 * span, level + 1
        cnt = jnp.minimum(pos + 1, w).astype(F32)
        ds.append(acc[hb:] / cnt - ext[lead + hb:lead + hb + tile, sl])
    o_ref[...] = _pool_project(ds, wp_ref, ps_ref)


def _pool_prompt(u, w_pool, pool_scale, *, tile=1024):
    b, s, c = u.shape
    hb = 2 * SUBLANES
    return pl.pallas_call(
        _pool_prompt_kernel,
        grid=(b, s // tile),
        in_specs=[pl.BlockSpec((None, tile, c), lambda i, t: (i, t, 0)),
                  pl.BlockSpec((None, hb, c), lambda i, t: (i, jnp.maximum(t * (tile // hb) - 1, 0), 0)),
                  pl.BlockSpec(w_pool.shape, lambda i, t: (0, 0, 0)),
                  pl.BlockSpec((1, c), lambda i, t: (0, 0))],
        out_specs=pl.BlockSpec((None, tile, c), lambda i, t: (i, t, 0)),
        out_shape=jax.ShapeDtypeStruct((b, s, c), BF16),
        scratch_shapes=[pltpu.VMEM((SUBLANES + hb + tile, c), F32),
                        pltpu.VMEM((len(POOL_WINDOWS) - 1, SUBLANES + hb + tile, POOL_GROUP), F32)],
        compiler_params=_cparams(("parallel", "parallel")),
        name="pool_prompt",
    )(u, u, w_pool, pool_scale.reshape(1, c))


def _pool_sample_kernel(st_ref, u_ref, wp_ref, ps_ref, o_ref, ns_ref):
    u = u_ref[...]
    ns_ref[0:POOL_BUF - 1] = st_ref[1:POOL_BUF]
    ns_ref[POOL_BUF - 1] = u
    ds = []
    for g, w in enumerate(POOL_WINDOWS):
        sl = slice(g * POOL_GROUP, (g + 1) * POOL_GROUP)
        acc = u[:, sl]
        for j in range(1, w):
            acc = acc + st_ref[POOL_BUF - j][:, sl]
        cnt = float(min(PAST_LEN + 1, w))
        ds.append(acc / cnt - u[:, sl])
    o_ref[...] = _pool_project(ds, wp_ref, ps_ref)


def _pool_sample(state_t, u, w_pool, pool_scale):
    nb, bd, c = state_t.shape
    full = lambda a: pl.BlockSpec(a.shape, lambda: (0,) * a.ndim)
    ps = pool_scale.reshape(1, c)
    return pl.pallas_call(
        _pool_sample_kernel,
        in_specs=[full(state_t), full(u), full(w_pool), full(ps)],
        out_specs=[pl.BlockSpec((bd, c), lambda: (0, 0)), full(state_t)],
        out_shape=[jax.ShapeDtypeStruct((bd, c), BF16), jax.ShapeDtypeStruct(state_t.shape, F32)],
        compiler_params=pltpu.CompilerParams(vmem_limit_bytes=VMEM_LIMIT),
        name="pool_sample",
    )(state_t, u, w_pool, ps)


def _route_kernel(x_ref, attn_ref, pool_ref, gm_ref, sh_ref, sc_ref, wo_ref, nf_ref, wr_ref, br_ref,
                  tri_t_ref, tri_e_ref,
                  x2_ref, hs_ref, slot_ref, cnt_ref):
    sub = slot_ref.shape[0]
    n = x_ref.shape[0]
    tile = n // sub
    mix = (jnp.dot(attn_ref[...], wo_ref[:ATTN_WIDTH, :], preferred_element_type=F32)
           + jnp.dot(pool_ref[...], wo_ref[ATTN_WIDTH:, :], preferred_element_type=F32))
    x2 = x_ref[...] + gm_ref[...] * mix
    x2_ref[...] = x2
    h = _rms(x2, nf_ref[...]) * (1.0 + sc_ref[...]) + sh_ref[...]

    hh, hl = _split(h)
    wh, wl = _split(wr_ref[...])
    nt = functools.partial(lax.dot_general, dimension_numbers=(((1,), (1,)), ((), ())),
                           preferred_element_type=F32)
    logits = nt(wh, hh) + nt(wl, hh) + nt(wh, hl) + br_ref[...]

    eidx = lax.broadcasted_iota(jnp.int32, (N_EXPERTS, n), 0).astype(F32)
    work = logits
    tops, picks = [], []
    for _ in range(TOP_K):
        m = jnp.max(work, axis=0, keepdims=True)
        pick = jnp.min(jnp.where(work == m, eidx, float(N_EXPERTS)), axis=0, keepdims=True)
        work = jnp.where(eidx == pick, -jnp.inf, work)
        tops.append(m)
        picks.append(pick)
    ex = [jnp.exp(v - tops[0]) for v in tops]
    den = ex[0] + ex[1] + ex[2] + ex[3]
    gates = [e / den for e in ex]

    sel = jnp.zeros((N_EXPERTS, n), F32)
    for pick in picks:
        sel = sel + (eidx == pick).astype(F32)
    selb = sel.astype(BF16)
    rank = jnp.concatenate([jnp.dot(selb[:, t * tile:(t + 1) * tile], tri_t_ref[...], preferred_element_type=F32)
                            for t in range(sub)], axis=1)
    cnts = [jnp.sum(sel[:, t * tile:(t + 1) * tile], axis=1, keepdims=True) for t in range(sub)]
    padded = jnp.concatenate(
        [jnp.broadcast_to(jnp.ceil(c * (1.0 / CHUNK)) * CHUNK, (N_EXPERTS, LANES)) for c in cnts], axis=1)
    seg = jnp.dot(tri_e_ref[...], padded.astype(BF16), preferred_element_type=F32)
    dest = jnp.concatenate([seg[:, t * LANES:t * LANES + 1] + rank[:, t * tile:(t + 1) * tile]
                            for t in range(sub)], axis=1)
    slots = [jnp.sum(jnp.where(eidx == pick, dest, 0.0), axis=0, keepdims=True) for pick in picks]

    ridx = lax.broadcasted_iota(jnp.int32, (TILE_ROWS, tile), 0).astype(F32)
    for t in range(sub):
        cols = slice(t * tile, (t + 1) * tile)
        cnt_ref[t] = jnp.broadcast_to(cnts[t], (N_EXPERTS, LANES))
        slot_ref[t] = jnp.concatenate([v[:, cols] for v in slots + gates], axis=0)
        hit = ridx == slots[0][:, cols]
        for s in slots[1:]:
            hit = jnp.logical_or(hit, ridx == s[:, cols])
        perm = jnp.where(hit, 1.0, 0.0).astype(BF16)
        hs_ref[pl.ds(t * TILE_ROWS, TILE_ROWS), :] = _pack_rows(
            jnp.dot(perm, hh[t * tile:(t + 1) * tile, :], preferred_element_type=F32))


def _route(x2d, attn, pool, mod, w_out, norm_ffn, w_router_t, b_router, hs_prev, *, tile, sub,
           rows_per_mod, tile0, extra_tiles):
    n = x2d.shape[0]
    nt = n // tile
    assert nt % sub == 0 and extra_tiles % sub == 0
    own_steps = nt // sub
    steps = own_steps + extra_tiles // sub
    mrows = mod.shape[2]
    assert mrows == 1 or (sub == 1 and mrows == tile)
    last = lambda i: jnp.minimum(i, own_steps - 1)
    group = lambda i: (last(i) * sub * tile) // rows_per_mod
    const = lambda shape: pl.BlockSpec(shape, lambda i: (0,) * len(shape))
    row = lambda w: pl.BlockSpec((sub * tile, w), lambda i: (last(i), 0))
    tri_t = (jnp.arange(tile)[:, None] < jnp.arange(tile)[None, :]).astype(BF16)
    tri_e = (jnp.arange(N_EXPERTS)[None, :] < jnp.arange(N_EXPERTS)[:, None]).astype(BF16)
    in_specs = [row(D_MODEL), row(ATTN_WIDTH), row(POOL_WIDTH),
                _mod_spec(mod, 2, group), _mod_spec(mod, 3, group), _mod_spec(mod, 4, group),
                const((D_MODEL, D_MODEL)), const((1, D_MODEL)), const((N_EXPERTS, D_MODEL)),
                const((N_EXPERTS, 1)), const((tile, tile)), const((N_EXPERTS, N_EXPERTS))]
    args = [x2d, attn, pool, mod, mod, mod, w_out, norm_ffn.reshape(1, D_MODEL), w_router_t,
            b_router.reshape(N_EXPERTS, 1), tri_t, tri_e]
    n_in = len(args)
    kern = _route_kernel
    aliases = {}
    hs_rows = (tile0 + steps * sub) * TILE_ROWS
    assert tile0 % sub == 0
    if hs_prev is not None:
        in_specs.append(pl.BlockSpec(memory_space=pl.ANY))
        args.append(hs_prev)
        aliases = {n_in: 1}
        kern = lambda *refs: _route_kernel(*refs[:n_in], *refs[n_in + 1:])
        hs_rows = hs_prev.shape[0]
    return pl.pallas_call(
        kern,
        grid=(steps,),
        in_specs=in_specs,
        out_specs=[row(D_MODEL),
                   pl.BlockSpec((sub * TILE_ROWS, ROW_W), lambda i: (i + tile0 // sub, 0)),
                   pl.BlockSpec((sub, 2 * TOP_K, tile), lambda i: (last(i), 0, 0)),
                   pl.BlockSpec((sub, N_EXPERTS, LANES), lambda i: (last(i), 0, 0))],
        out_shape=[jax.ShapeDtypeStruct((n, D_MODEL), F32),
                   jax.ShapeDtypeStruct((hs_rows, ROW_W), jnp.int32),
                   jax.ShapeDtypeStruct((nt, 2 * TOP_K, tile), F32),
                   jax.ShapeDtypeStruct((nt, N_EXPERTS, LANES), F32)],
        input_output_aliases=aliases,
        compiler_params=_cparams(("arbitrary",)),
        name="route",
    )(*args)


def _moe_kernel(src_ref, be_ref, nxt_ref, nb_ref, hs_hbm, w1_hbm, b1_ref, w2_hbm, b2_ref, out_hbm,
                lhs, obuf, w1s, w2s, w1c, w2c, sem_in, sem_out, sem_w, *, scratch_chunk):
    nb = nb_ref[0]

    def weight_copies(e):
        return (pltpu.make_async_copy(w1_hbm.at[e], w1s, sem_w.at[0]),
                pltpu.make_async_copy(w2_hbm.at[e], w2s, sem_w.at[1]))

    def chunk_rows(c):
        return pl.ds(pl.multiple_of(c * CHUNK, CHUNK), CHUNK)

    def start_in(blk, s):
        for j in range(BLOCK_CHUNKS):
            c = src_ref[blk * BLOCK_CHUNKS + j]
            c = jnp.where(c < 0, ZERO_CHUNK, c)
            pltpu.make_async_copy(hs_hbm.at[chunk_rows(c)], lhs.at[s, pl.ds(j * CHUNK, CHUNK)],
                                  sem_in.at[s]).start()

    def wait_in(s):
        pltpu.make_async_copy(hs_hbm.at[pl.ds(0, MOE_BLOCK)], lhs.at[s], sem_in.at[s]).wait()

    def start_out(blk, s):
        for j in range(BLOCK_CHUNKS):
            c = src_ref[blk * BLOCK_CHUNKS + j]
            c = jnp.where(c < 0, scratch_chunk + s * BLOCK_CHUNKS + j, c)
            pltpu.make_async_copy(obuf.at[s, pl.ds(j * CHUNK, CHUNK)],
                                  out_hbm.at[chunk_rows(c), pl.ds(0, PACKED_W)], sem_out.at[s]).start()

    def wait_out(s):
        pltpu.make_async_copy(obuf.at[s], out_hbm.at[pl.ds(0, MOE_BLOCK), pl.ds(0, PACKED_W)],
                              sem_out.at[s]).wait()

    @pl.when(nb > 0)
    def _():
        start_in(0, 0)
        for cp in weight_copies(be_ref[0]):
            cp.start(priority=WEIGHT_DMA_PRIORITY)

    def block(b, carry):
        slot = b % 2
        e = be_ref[b]

        @pl.when(jnp.logical_or(b == 0, e != be_ref[jnp.maximum(b - 1, 0)]))
        def _():
            for cp in weight_copies(e):
                cp.wait()
            w1c[...] = w1s[...].astype(BF16)
            w2c[...] = w2s[...].astype(BF16)
            nxt = nxt_ref[b]

            @pl.when(nxt >= 0)
            def _():
                for cp in weight_copies(nxt):
                    cp.start(priority=WEIGHT_DMA_PRIORITY)

        wait_in(slot)

        @pl.when(b >= 2)
        def _():
            wait_out(slot)

        start_in(b + 1, 1 - slot)

        def ffn(nrows):
            xh, xl = _unpack_rows(lhs[slot, 0:nrows, :])
            gu = (jnp.dot(xh, w1c[:PACKED_W, :], preferred_element_type=F32)
                  + jnp.dot(xl, w1c[PACKED_W:, :], preferred_element_type=F32) + b1_ref[e])
            gate = jnp.minimum(gu[:, :D_FF], SWIGLU_LIMIT)
            up = jnp.clip(gu[:, D_FF:], -SWIGLU_LIMIT, SWIGLU_LIMIT)
            act = (up + 1.0) * (gate * jax.nn.sigmoid(SWIGLU_ALPHA * gate))
            y = jnp.dot(act.astype(BF16), w2c[...], preferred_element_type=F32) + b2_ref[e]
            obuf[slot, 0:nrows, :] = _pack_rows(y.astype(BF16).astype(F32))

        real = nb_ref[1 + b]

        @pl.when(real > BLOCK_CHUNKS // 2)
        def _():
            ffn(MOE_BLOCK)

        @pl.when(real <= BLOCK_CHUNKS // 2)
        def _():
            ffn(MOE_BLOCK // 2)

        start_out(b, slot)
        return carry

    obuf[...] = jnp.zeros(obuf.shape, jnp.int32)
    lax.fori_loop(0, nb, block, 0)

    @pl.when(nb > 0)
    def _():
        last_slot = (nb - 1) % 2
        wait_in(1 - last_slot)

        @pl.when(nb >= 2)
        def _():
            wait_out(1 - last_slot)
        wait_out(last_slot)


def _moe(hs, src, block_e, next_e, nblocks, w1, b1, w2, b2, scratch_chunk):
    full = lambda shape: pl.BlockSpec(shape, lambda i, *_: (0,) * len(shape))
    hbm = pl.BlockSpec(memory_space=pl.ANY)
    return pl.pallas_call(
        functools.partial(_moe_kernel, scratch_chunk=scratch_chunk),
        grid_spec=pltpu.PrefetchScalarGridSpec(
            num_scalar_prefetch=4,
            grid=(1,),
            in_specs=[hbm, hbm, full((N_EXPERTS, 1, 2 * D_FF)), hbm, full((N_EXPERTS, 1, D_MODEL))],
            out_specs=hbm,
            scratch_shapes=[pltpu.VMEM((2, MOE_BLOCK, ROW_W), jnp.int32),
                            pltpu.VMEM((2, MOE_BLOCK, PACKED_W), jnp.int32),
                            pltpu.VMEM((D_MODEL, 2 * D_FF), F32),
                            pltpu.VMEM((D_FF, D_MODEL), F32),
                            pltpu.VMEM((D_MODEL, 2 * D_FF), BF16),
                            pltpu.VMEM((D_FF, D_MODEL), BF16),
                            pltpu.SemaphoreType.DMA((2,)),
                            pltpu.SemaphoreType.DMA((2,)),
                            pltpu.SemaphoreType.DMA((2,))]),
        out_shape=jax.ShapeDtypeStruct(hs.shape, jnp.int32),
        input_output_aliases={4: 0},
        compiler_params=_cparams(("arbitrary",)),
        name="moe_experts",
    )(src, block_e, next_e, nblocks, hs, w1, b1.reshape(N_EXPERTS, 1, 2 * D_FF), w2,
      b2.reshape(N_EXPERTS, 1, D_MODEL))


def _plan(cnt, n_blocks_max):
    nt = cnt.shape[0]
    nch = (cnt + (CHUNK - 1)) // CHUNK
    lstart = jnp.cumsum(nch, axis=1) - nch
    ne = jnp.sum(nch, axis=0)
    nbe = (ne + (BLOCK_CHUNKS - 1)) // BLOCK_CHUNKS
    bend = jnp.cumsum(nbe)
    nblocks = bend[-1]
    gstart = (bend - nbe)[None, :] * BLOCK_CHUNKS + (jnp.cumsum(nch, axis=0) - nch)
    s0 = jnp.arange(nt, dtype=jnp.int32)[:, None] * TILE_CHUNKS + lstart
    blk = jnp.arange(n_blocks_max + 1, dtype=jnp.int32)
    be = jnp.sum((blk[:, None] >= bend[None, :]).astype(jnp.int32), axis=1)
    be_last = jnp.sum((nblocks - 1 >= bend).astype(jnp.int32))
    be = jnp.minimum(be, be_last).astype(jnp.int32)
    eid = jnp.arange(N_EXPERTS, dtype=jnp.int32)
    strips = jnp.stack([gstart.T, nch.T, s0.T])
    mine = (be[:, None] == eid[None, :])[None, :, :, None]
    gs_b, nc_b, s0_b = jnp.sum(jnp.where(mine, strips[:, None], 0), axis=2)
    c = (blk[:, None] * BLOCK_CHUNKS + jnp.arange(BLOCK_CHUNKS, dtype=jnp.int32)[None, :])[:, :, None]
    inside = jnp.logical_and(c >= gs_b[:, None, :], c < (gs_b + nc_b)[:, None, :])
    src = (jnp.sum(jnp.where(inside, (s0_b - gs_b)[:, None, :] + c + 1, 0), axis=2) - 1).reshape(-1)
    be = be[:n_blocks_max]
    later = jnp.logical_and(eid[None, :] > be[:, None], (nbe > 0)[None, :])
    nxt = jnp.min(jnp.where(later, eid[None, :], N_EXPERTS), axis=1)
    nxt = jnp.where(nxt == N_EXPERTS, -1, nxt).astype(jnp.int32)
    real = jnp.sum((src.reshape(-1, BLOCK_CHUNKS) >= 0).astype(jnp.int32), axis=1)
    return src, be, nxt, jnp.concatenate([nblocks.reshape(1), real]).astype(jnp.int32)


def _combine_kernel(o_ref, slot_ref, x2_ref, gf_ref, y_ref):
    sub = slot_ref.shape[0]
    tile = x2_ref.shape[0] // sub
    ridx = lax.broadcasted_iota(jnp.int32, (TILE_ROWS, tile), 0).astype(F32)
    tn = functools.partial(lax.dot_general, dimension_numbers=(((0,), (0,)), ((), ())),
                           preferred_element_type=F32)
    for t in range(sub):
        gmat = jnp.zeros((TILE_ROWS, tile), F32)
        for k in range(TOP_K):
            gmat = jnp.where(ridx == slot_ref[t, k:k + 1, :], slot_ref[t, TOP_K + k:TOP_K + k + 1, :], gmat)
        gb = gmat.astype(BF16)
        oh, ol = _unpack_rows(o_ref[pl.ds(t * TILE_ROWS, TILE_ROWS), :])
        y = jnp.concatenate([tn(gb, oh), tn(gb, ol)], axis=1)
        rows = pl.ds(t * tile, tile)
        gf = gf_ref[...] if gf_ref.shape[0] == 1 else gf_ref[rows, :]
        y_ref[rows, :] = x2_ref[rows, :] + gf * y


def _combine(outs, slots, x2, mod, *, tile, sub, rows_per_mod, tile0):
    n = x2.shape[0]
    mrows = mod.shape[2]
    assert n % (sub * tile) == 0 and tile0 % sub == 0 and (mrows == 1 or sub == 1)
    return pl.pallas_call(
        _combine_kernel,
        grid=(n // (sub * tile),),
        in_specs=[pl.BlockSpec((sub * TILE_ROWS, PACKED_W), lambda i: (i + tile0 // sub, 0)),
                  pl.BlockSpec((sub, 2 * TOP_K, tile), lambda i: (i, 0, 0)),
                  pl.BlockSpec((sub * tile, D_MODEL), lambda i: (i, 0)),
                  _mod_spec(mod, 5, lambda i: (i * sub * tile) // rows_per_mod)],
        out_specs=pl.BlockSpec((sub * tile, D_MODEL), lambda i: (i, 0)),
        out_shape=jax.ShapeDtypeStruct((n, D_MODEL), F32),
        compiler_params=_cparams(("parallel",)),
        name="combine",
    )(outs, slots, x2, mod)


def kernel(x_prompt, x_sample, cache_k, cache_v, state_pool, c_prompt, c_sample, rel_bias, norm_mix, w_ada,
           b_ada, w_in, q_norm, k_norm, sinks, w_pool, pool_scale, w_out, norm_ffn, w_router, b_router,
           w1, b1, w2, b2):
    depth = w_in.shape[0]
    assert depth == 1
    l = 0
    bp, sp, _ = x_prompt.shape
    bs = x_sample.shape[0]
    assert x_sample.shape[1] == 1 and sp % SORT_TILE == 0 and bs <= SORT_TILE
    n_p = bp * sp
    tiles_p = n_p // SORT_TILE
    max_chunks = tiles_p * (SORT_TILE * TOP_K // CHUNK + N_EXPERTS) + (bs * TOP_K // CHUNK + N_EXPERTS)
    n_blocks_max = -(-max_chunks // BLOCK_CHUNKS) + N_EXPERTS

    pr = -(-bp // SUBLANES) * SUBLANES
    c_all = jnp.concatenate([c_prompt, jnp.zeros((pr - bp, D_MODEL), F32), c_sample], axis=0)
    mod = _modulation(c_all, w_ada[l], b_ada[l])
    mod_p = mod[:, :bp].reshape(6, bp, 1, D_MODEL)
    mod_s = mod[:, pr:].reshape(6, 1, bs, D_MODEL)

    head = jnp.arange(ATTN_WIDTH) // HEAD_DIM
    bd = (head[:, None] == head[None, :]).astype(BF16)
    w_in_b = w_in[l].astype(BF16)
    w_out_b = w_out[l].astype(BF16)
    w_pool_b = w_pool[l].astype(BF16)
    w_router_t = w_router[l].T

    xp = x_prompt.reshape(n_p, D_MODEL)
    q, k, v, u = _mixer_inputs(xp, mod_p, norm_mix[l], w_in_b, bd, q_norm[l], k_norm[l],
                               tile=512, rows_per_mod=sp, precise=False)
    k3 = k.reshape(bp, sp, KV_WIDTH)
    v3 = v.reshape(bp, sp, KV_WIDTH)
    u3 = u.reshape(bp, sp, POOL_WIDTH)
    attn = _attn_prompt(q.reshape(bp, sp, ATTN_WIDTH), k3, v3, sinks[l], rel_bias)
    pool = _pool_prompt(u3, w_pool_b, pool_scale[l])
    keep = min(WINDOW, sp)
    nkp = k3[:, -keep:].reshape(bp, keep, N_KV_HEADS, HEAD_DIM)
    nvp = v3[:, -keep:].reshape(bp, keep, N_KV_HEADS, HEAD_DIM)
    if sp >= POOL_BUF:
        npp = u3[:, -POOL_BUF:]
    else:
        npp = jnp.concatenate([jnp.zeros((bp, POOL_BUF - sp, POOL_WIDTH), F32), u3], axis=1)
    x2_p, hs, slots_p, cnt_p = _route(
        xp, attn.reshape(n_p, ATTN_WIDTH), pool.reshape(n_p, POOL_WIDTH), mod_p,
        w_out_b, norm_ffn[l], w_router_t, b_router[l], None,
        tile=SORT_TILE, sub=ROUTE_SUB, rows_per_mod=sp, tile0=0, extra_tiles=max(2, ROUTE_SUB))

    xs = x_sample.reshape(bs, D_MODEL)
    qs, ks, vs, us = _mixer_inputs(xs, mod_s, norm_mix[l], w_in[l], bd, q_norm[l], k_norm[l],
                                   tile=bs, rows_per_mod=bs, precise=True)
    wbuf = cache_k.shape[2]
    attn_s, nks, nvs = _attn_sample(qs, ks, vs, cache_k[l].reshape(bs, wbuf, KV_WIDTH),
                                    cache_v[l].reshape(bs, wbuf, KV_WIDTH), sinks[l], rel_bias)
    pool_s, nps_t = _pool_sample(jnp.swapaxes(state_pool[l], 0, 1), us, w_pool_b, pool_scale[l])
    x2_s, hs, slots_s, cnt_s = _route(
        xs, attn_s.astype(BF16), pool_s, mod_s, w_out_b, norm_ffn[l],
        w_router_t, b_router[l], hs, tile=bs, sub=1, rows_per_mod=bs, tile0=tiles_p, extra_tiles=0)

    cnt = jnp.concatenate([cnt_p[:, :, 0], cnt_s[:, :, 0]], axis=0).astype(jnp.int32)
    src, block_e, next_e, nblocks = _plan(cnt, n_blocks_max)
    outs = _moe(hs, src, block_e, next_e, nblocks, w1[l], b1[l], w2[l], b2[l],
                scratch_chunk=(tiles_p + 1) * TILE_CHUNKS)

    y_p = _combine(outs, slots_p, x2_p, mod_p, tile=SORT_TILE, sub=COMBINE_SUB, rows_per_mod=sp, tile0=0)
    y_s = _combine(outs, slots_s, x2_s, mod_s, tile=bs, sub=1, rows_per_mod=bs, tile0=tiles_p)

    return (y_p.reshape(bp, sp, D_MODEL), y_s.reshape(bs, 1, D_MODEL),
            nkp[None], nvp[None], npp[None],
            nks.reshape(1, bs, wbuf, N_KV_HEADS, HEAD_DIM), nvs.reshape(1, bs, wbuf, N_KV_HEADS, HEAD_DIM),
            jnp.swapaxes(nps_t, 0, 1)[None])
```

```python
import functools
import math

import jax
import jax.numpy as jnp
from jax import lax
from jax.experimental import pallas as pl
from jax.experimental.pallas import tpu as pltpu

F32 = jnp.float32
BF16 = jnp.bfloat16

D_MODEL = 1024
HEAD_DIM = 64
N_HEADS = 8
N_KV_HEADS = 2
GROUP = N_HEADS // N_KV_HEADS
ATTN_WIDTH = N_HEADS * HEAD_DIM
KV_WIDTH = N_KV_HEADS * HEAD_DIM
POOL_WIDTH = D_MODEL - ATTN_WIDTH
POOL_WINDOWS = (2, 4, 8, 16)
POOL_GROUP = POOL_WIDTH // len(POOL_WINDOWS)
POOL_BUF = max(POOL_WINDOWS) - 1
IN_WIDTH = ATTN_WIDTH + 2 * KV_WIDTH + POOL_WIDTH
WINDOW = 128
ATTN_BLOCK = 128
N_BUCKETS = 32
MAX_EXACT = 16
REL_MAX_DIST = 128
N_EXPERTS = 32
TOP_K = 4
D_FF = D_MODEL
SWIGLU_LIMIT = 7.0
SWIGLU_ALPHA = 1.702
EPS = 1e-6
NEG_INF = -1e30
PAST_LEN = 16384

LANES = 128
SUBLANES = 8
VMEM_LIMIT = 56 * 1024 * 1024

ATTN_QB = 8

SORT_TILE = 256
ROUTE_SUB = 4
COMBINE_SUB = 4
CHUNK = SUBLANES
TILE_ROWS = -(-(SORT_TILE * TOP_K + N_EXPERTS * (CHUNK - 1)) // LANES) * LANES
TILE_CHUNKS = TILE_ROWS // CHUNK
MOE_BLOCK = 256
BLOCK_CHUNKS = MOE_BLOCK // CHUNK
WEIGHT_DMA_PRIORITY = 1
PACKED_W = D_MODEL // 2
ROW_W = PACKED_W
ZERO_CHUNK = TILE_CHUNKS - 1


def _bdot(a, b):
    return jnp.dot(a.astype(BF16), b.astype(BF16), preferred_element_type=F32)


def _split(a):
    hi = a.astype(BF16)
    lo = (a - hi.astype(F32)).astype(BF16)
    return hi, lo


def _dot3(a, b):
    ah, al = _split(a)
    bh, bl = _split(b)
    d = functools.partial(jnp.dot, preferred_element_type=F32)
    return d(ah, bh) + d(al, bh) + d(ah, bl)


def _pack_rows(x):
    bits = lax.bitcast_convert_type(x, jnp.int32)
    return bits[:, :PACKED_W] | lax.shift_right_logical(bits[:, PACKED_W:], 16)


def _unpack_rows(w):
    hi = lax.bitcast_convert_type(w & jnp.int32(-65536), F32)
    lo = lax.bitcast_convert_type(lax.shift_left(w, 16), F32)
    return hi.astype(BF16), lo.astype(BF16)


def _rms(x, g):
    return x * lax.rsqrt(jnp.mean(x * x, axis=-1, keepdims=True) + EPS) * g


def _cparams(sem, **kw):
    return pltpu.CompilerParams(dimension_semantics=sem, vmem_limit_bytes=VMEM_LIMIT, **kw)


def _ada_kernel(c_ref, w_ref, b_ref, o_ref):
    c = c_ref[...]
    s = c * jax.nn.sigmoid(c)
    o_ref[...] = _dot3(s, w_ref[...]) + b_ref[...]


def _modulation(c, w_ada, b_ada):
    rows = c.shape[0]
    n = w_ada.shape[1]
    tn = D_MODEL
    return pl.pallas_call(
        _ada_kernel,
        grid=(n // tn,),
        in_specs=[pl.BlockSpec((rows, D_MODEL), lambda j: (0, 0)),
                  pl.BlockSpec((D_MODEL, tn), lambda j: (0, j)),
                  pl.BlockSpec((1, tn), lambda j: (0, j))],
        out_specs=pl.BlockSpec((None, rows, tn), lambda j: (j, 0, 0)),
        out_shape=jax.ShapeDtypeStruct((n // tn, rows, tn), F32),
        compiler_params=_cparams(("parallel",)),
        name="modulation",
    )(c, w_ada, b_ada.reshape(1, n))


def _head_rms(t, bd, g, precise):
    if precise:
        hi, lo = _split(t * t)
        ss = jnp.dot(hi, bd, preferred_element_type=F32) + jnp.dot(lo, bd, preferred_element_type=F32)
    else:
        ss = _bdot(t * t, bd)
    return t * lax.rsqrt(ss * (1.0 / HEAD_DIM) + EPS) * g


def _mixin_kernel(x_ref, sh_ref, sc_ref, g_ref, w_ref, bd_ref, qn_ref, kn_ref,
                  q_ref, k_ref, v_ref, u_ref, *, precise):
    h = _rms(x_ref[...], g_ref[...]) * (1.0 + sc_ref[...]) + sh_ref[...]
    z = _dot3(h, w_ref[...]) if precise else _bdot(h, w_ref[...])
    q = z[:, :ATTN_WIDTH]
    k = z[:, ATTN_WIDTH:ATTN_WIDTH + KV_WIDTH]
    bd = bd_ref[...]
    q = _head_rms(q, bd, qn_ref[...], precise)
    k = _head_rms(k, bd[:KV_WIDTH, :KV_WIDTH], kn_ref[...], precise)
    q_ref[...] = (q * (HEAD_DIM ** -0.5)).astype(BF16)
    k_ref[...] = k
    v_ref[...] = z[:, ATTN_WIDTH + KV_WIDTH:ATTN_WIDTH + 2 * KV_WIDTH]
    u_ref[...] = z[:, ATTN_WIDTH + 2 * KV_WIDTH:]


def _mod_spec(mod, k, group_of_step):
    return pl.BlockSpec((None, None, mod.shape[2], D_MODEL), lambda i: (k, group_of_step(i), 0, 0))


def _mixer_inputs(x2d, mod, norm_mix, w_in, bd, q_norm, k_norm, *, tile, rows_per_mod, precise):
    n = x2d.shape[0]
    group = lambda i: (i * tile) // rows_per_mod
    const = lambda shape: pl.BlockSpec(shape, lambda i: (0,) * len(shape))
    row = lambda w: pl.BlockSpec((tile, w), lambda i: (i, 0))
    return pl.pallas_call(
        functools.partial(_mixin_kernel, precise=precise),
        grid=(n // tile,),
        in_specs=[row(D_MODEL), _mod_spec(mod, 0, group), _mod_spec(mod, 1, group), const((1, D_MODEL)),
                  const((D_MODEL, IN_WIDTH)), const((ATTN_WIDTH, ATTN_WIDTH)), const((1, ATTN_WIDTH)),
                  const((1, KV_WIDTH))],
        out_specs=[row(ATTN_WIDTH), row(KV_WIDTH), row(KV_WIDTH), row(POOL_WIDTH)],
        out_shape=[jax.ShapeDtypeStruct((n, ATTN_WIDTH), BF16),
                   jax.ShapeDtypeStruct((n, KV_WIDTH), F32),
                   jax.ShapeDtypeStruct((n, KV_WIDTH), F32),
                   jax.ShapeDtypeStruct((n, POOL_WIDTH), F32)],
        compiler_params=_cparams(("parallel",)),
        name="mixer_inputs",
    )(x2d, mod, mod, norm_mix.reshape(1, D_MODEL), w_in, bd,
      jnp.tile(q_norm, N_HEADS).reshape(1, ATTN_WIDTH), jnp.tile(k_norm, N_KV_HEADS).reshape(1, KV_WIDTH))


def _t5_bucket(rel):
    n = jnp.maximum(rel, 0)
    nf = jnp.maximum(n, 1).astype(F32)
    large = MAX_EXACT + (jnp.log(nf / MAX_EXACT) / math.log(REL_MAX_DIST / MAX_EXACT)
                         * (N_BUCKETS - MAX_EXACT)).astype(jnp.int32)
    large = jnp.minimum(large, N_BUCKETS - 1)
    return jnp.where(n < MAX_EXACT, n, large)


def _bias_table(rel, rel_table):
    bucket = _t5_bucket(rel)
    table = rel_table.astype(F32)
    ids = jnp.arange(N_BUCKETS, dtype=bucket.dtype).reshape((N_BUCKETS, 1) + (1,) * rel.ndim)
    onehot = bucket[None, None] == ids
    bias = jnp.sum(jnp.where(onehot, table.reshape(table.shape + (1,) * rel.ndim), 0.0), axis=0)
    valid = (rel >= 0) & (rel < WINDOW)
    return jnp.where(valid[None], bias, NEG_INF)


def _attn_prompt_kernel(sink_ref, q_ref, kp_ref, kc_ref, vp_ref, vc_ref, bias_ref, o_ref):
    first = pl.program_id(1) == 0
    kk = jnp.concatenate([kp_ref[...], kc_ref[...]], axis=0).astype(BF16)
    vv = jnp.concatenate([vp_ref[...], vc_ref[...]], axis=0).astype(BF16)
    key = lax.broadcasted_iota(jnp.int32, (2 * ATTN_BLOCK, 1), 0)
    no_prev = jnp.logical_and(first, key < ATTN_BLOCK)
    lane = lax.broadcasted_iota(jnp.int32, (1, N_HEADS * ATTN_BLOCK), 1)
    sink = jnp.zeros((1, N_HEADS * ATTN_BLOCK), F32)
    for h in range(N_HEADS):
        sink = jnp.where(lane // ATTN_BLOCK == h, sink_ref[h], sink)
    contract = lambda a, b, dims: lax.dot_general(a, b, (dims, ((), ())), preferred_element_type=F32)
    part = GROUP * ATTN_BLOCK
    for i in range(ATTN_QB):
        q = q_ref[i * ATTN_BLOCK:(i + 1) * ATTN_BLOCK, :]
        keys = slice(i * ATTN_BLOCK, (i + 2) * ATTN_BLOCK)
        scores = []
        for kv in range(N_KV_HEADS):
            heads = range(kv * GROUP, (kv + 1) * GROUP)
            qg = jnp.concatenate([q[:, h * HEAD_DIM:(h + 1) * HEAD_DIM] for h in heads], axis=0)
            scores.append(contract(kk[keys, kv * HEAD_DIM:(kv + 1) * HEAD_DIM], qg, ((1,), (1,))))
        s = jnp.concatenate(scores, axis=1) + bias_ref[...]
        if i == 0:
            s = jnp.where(no_prev, NEG_INF, s)
        m = jnp.maximum(jnp.max(s, axis=0, keepdims=True), sink)
        p = jnp.exp(s - m)
        denom = jnp.sum(p, axis=0, keepdims=True) + jnp.exp(sink - m)
        p = p.astype(BF16)
        halves = [contract(vv[keys, kv * HEAD_DIM:(kv + 1) * HEAD_DIM], p[:, kv * part:(kv + 1) * part],
                           ((0,), (0,))) / denom[:, kv * part:(kv + 1) * part]
                  for kv in range(N_KV_HEADS)]
        o_t = jnp.concatenate(halves, axis=0)
        per_g = [o_t[:, g * ATTN_BLOCK:(g + 1) * ATTN_BLOCK].T for g in range(GROUP)]
        out = [t[:, kv * HEAD_DIM:(kv + 1) * HEAD_DIM] for kv in range(N_KV_HEADS) for t in per_g]
        o_ref[i * ATTN_BLOCK:(i + 1) * ATTN_BLOCK, :] = jnp.concatenate(out, axis=-1).astype(BF16)


def _attn_prompt(q, k, v, sinks, rel_table):
    b, s = q.shape[:2]
    qrows = ATTN_QB * ATTN_BLOCK
    assert s % qrows == 0
    qi = jnp.arange(ATTN_BLOCK, dtype=jnp.int32)[:, None]
    si = jnp.arange(2 * ATTN_BLOCK, dtype=jnp.int32)[None, :]
    bias = _bias_table(qi + ATTN_BLOCK - si, rel_table)
    bias = bias.reshape(N_HEADS * ATTN_BLOCK, 2 * ATTN_BLOCK).T
    cur = lambda w: pl.BlockSpec((None, qrows, w), lambda i, j, *_: (i, j, 0))
    prev = lambda w: pl.BlockSpec((None, ATTN_BLOCK, w),
                                  lambda i, j, *_: (i, jnp.maximum(j * ATTN_QB - 1, 0), 0))
    return pl.pallas_call(
        _attn_prompt_kernel,
        grid_spec=pltpu.PrefetchScalarGridSpec(
            num_scalar_prefetch=1,
            grid=(b, s // qrows),
            in_specs=[cur(ATTN_WIDTH), prev(KV_WIDTH), cur(KV_WIDTH), prev(KV_WIDTH), cur(KV_WIDTH),
                      pl.BlockSpec(bias.shape, lambda i, j, *_: (0, 0))],
            out_specs=cur(ATTN_WIDTH)),
        out_shape=jax.ShapeDtypeStruct((b, s, ATTN_WIDTH), BF16),
        compiler_params=_cparams(("parallel", "parallel")),
        name="attn_prompt",
    )(sinks.astype(F32), q, k, k, v, v, bias)


def _attn_sample_kernel(sink_ref, q_ref, kc_ref, vc_ref, kn_ref, vn_ref, bias_ref, bnew_ref,
                        o_ref, nk_ref, nv_ref):
    kc = kc_ref[...]
    vc = vc_ref[...]
    kn = kn_ref[...]
    vn = vn_ref[...]
    w = kc.shape[1]
    pos = lax.broadcasted_iota(jnp.int32, kc.shape, 1)
    nk_ref[...] = jnp.where(pos == w - 1, kn[:, None, :], pltpu.roll(kc, w - 1, 1))
    nv_ref[...] = jnp.where(pos == w - 1, vn[:, None, :], pltpu.roll(vc, w - 1, 1))
    gi = lax.broadcasted_iota(jnp.int32, (1, GROUP, 1), 1)
    for kv in range(N_KV_HEADS):
        sl = slice(kv * HEAD_DIM, (kv + 1) * HEAD_DIM)
        qg = q_ref[:, kv]
        s = jnp.einsum('bgd,bsd->bgs', qg, kc[:, :, sl].astype(BF16), preferred_element_type=F32)
        s = s + bias_ref[kv][None]
        s_new = jnp.sum(qg.astype(F32) * kn[:, None, sl], axis=-1, keepdims=True) + bnew_ref[kv][None]
        sink = jnp.zeros((1, GROUP, 1), F32)
        for g in range(GROUP):
            sink = jnp.where(gi == g, sink_ref[kv * GROUP + g], sink)
        m = jnp.maximum(jnp.maximum(jnp.max(s, axis=-1, keepdims=True), s_new), sink)
        p = jnp.exp(s - m)
        p_new = jnp.exp(s_new - m)
        denom = jnp.sum(p, axis=-1, keepdims=True) + p_new + jnp.exp(sink - m)
        o = jnp.einsum('bgs,bsd->bgd', p.astype(BF16), vc[:, :, sl].astype(BF16), preferred_element_type=F32)
        o = o + p_new * vn[:, None, sl]
        o_ref[:, kv] = o / denom


def _attn_sample(q, k_new, v_new, cache_k, cache_v, sinks, rel_table, *, tile=32):
    bd, w = cache_k.shape[:2]
    rel = w - jnp.arange(w, dtype=jnp.int32)
    bias = _bias_table(rel, rel_table).reshape(N_KV_HEADS, GROUP, w)
    bnew = _bias_table(jnp.zeros((1,), jnp.int32), rel_table).reshape(N_KV_HEADS, GROUP, 1)
    q4 = q.reshape(bd, N_KV_HEADS, GROUP, HEAD_DIM)
    spec4 = pl.BlockSpec((tile, N_KV_HEADS, GROUP, HEAD_DIM), lambda i, *_: (i, 0, 0, 0))
    cache = pl.BlockSpec((tile, w, KV_WIDTH), lambda i, *_: (i, 0, 0))
    new = pl.BlockSpec((tile, KV_WIDTH), lambda i, *_: (i, 0))
    const3 = lambda a: pl.BlockSpec(a.shape, lambda i, *_: (0, 0, 0))
    o, nk, nv = pl.pallas_call(
        _attn_sample_kernel,
        grid_spec=pltpu.PrefetchScalarGridSpec(
            num_scalar_prefetch=1,
            grid=(bd // tile,),
            in_specs=[spec4, cache, cache, new, new, const3(bias), const3(bnew)],
            out_specs=[spec4, cache, cache]),
        out_shape=[jax.ShapeDtypeStruct(q4.shape, F32),
                   jax.ShapeDtypeStruct(cache_k.shape, F32),
                   jax.ShapeDtypeStruct(cache_v.shape, F32)],
        compiler_params=_cparams(("parallel",)),
        name="attn_sample",
    )(sinks.astype(F32), q4, cache_k, cache_v, k_new, v_new, bias, bnew)
    return o.reshape(bd, ATTN_WIDTH), nk, nv


def _pool_project(d_groups, wp_ref, ps_ref):
    out = [_bdot(d, wp_ref[g]) for g, d in enumerate(d_groups)]
    return (jnp.concatenate(out, axis=-1) * ps_ref[...]).astype(BF16)


def _pool_prompt_kernel(u_ref, halo_ref, wp_ref, ps_ref, o_ref, ext, lv):
    t = pl.program_id(1)
    tile = u_ref.shape[0]
    lead, hb = SUBLANES, 2 * SUBLANES
    halo = halo_ref[...]
    ext[0:lead, :] = jnp.zeros((lead, ext.shape[1]), F32)
    ext[lead:lead + hb, :] = jnp.where(t == 0, jnp.zeros_like(halo), halo)
    ext[lead + hb:, :] = u_ref[...]
    lv[:, 0:lead, :] = jnp.zeros((lv.shape[0], lead, lv.shape[2]), F32)
    pos = t * tile + lax.broadcasted_iota(jnp.int32, (tile, 1), 0)
    n = hb + tile
    ds = []
    for g, w in enumerate(POOL_WINDOWS):
        sl = slice(g * POOL_GROUP, (g + 1) * POOL_GROUP)
        acc = ext[lead:lead + n, sl] + ext[lead - 1:lead - 1 + n, sl]
        span, level = 2, 0
        while span < w:
            lv[level, lead:lead + n, :] = acc
            acc = acc + lv[level, lead - span:lead - span + n, :]
            span, level = 2 * span, level + 1
        cnt = jnp.minimum(pos + 1, w).astype(F32)
        ds.append(acc[hb:] / cnt - ext[lead + hb:lead + hb + tile, sl])
    o_ref[...] = _pool_project(ds, wp_ref, ps_ref)


def _pool_prompt(u, w_pool, pool_scale, *, tile=1024):
    b, s, c = u.shape
    hb = 2 * SUBLANES
    return pl.pallas_call(
        _pool_prompt_kernel,
        grid=(b, s // tile),
        in_specs=[pl.BlockSpec((None, tile, c), lambda i, t: (i, t, 0)),
                  pl.BlockSpec((None, hb, c), lambda i, t: (i, jnp.maximum(t * (tile // hb) - 1, 0), 0)),
                  pl.BlockSpec(w_pool.shape, lambda i, t: (0, 0, 0)),
                  pl.BlockSpec((1, c), lambda i, t: (0, 0))],
        out_specs=pl.BlockSpec((None, tile, c), lambda i, t: (i, t, 0)),
        out_shape=jax.ShapeDtypeStruct((b, s, c), BF16),
        scratch_shapes=[pltpu.VMEM((SUBLANES + hb + tile, c), F32),
                        pltpu.VMEM((len(POOL_WINDOWS) - 1, SUBLANES + hb + tile, POOL_GROUP), F32)],
        compiler_params=_cparams(("parallel", "parallel")),
        name="pool_prompt",
    )(u, u, w_pool, pool_scale.reshape(1, c))


def _pool_sample_kernel(st_ref, u_ref, wp_ref, ps_ref, o_ref, ns_ref):
    u = u_ref[...]
    ns_ref[0:POOL_BUF - 1] = st_ref[1:POOL_BUF]
    ns_ref[POOL_BUF - 1] = u
    ds = []
    for g, w in enumerate(POOL_WINDOWS):
        sl = slice(g * POOL_GROUP, (g + 1) * POOL_GROUP)
        acc = u[:, sl]
        for j in range(1, w):
            acc = acc + st_ref[POOL_BUF - j][:, sl]
        cnt = float(min(PAST_LEN + 1, w))
        ds.append(acc / cnt - u[:, sl])
    o_ref[...] = _pool_project(ds, wp_ref, ps_ref)


def _pool_sample(state_t, u, w_pool, pool_scale):
    nb, bd, c = state_t.shape
    full = lambda a: pl.BlockSpec(a.shape, lambda: (0,) * a.ndim)
    ps = pool_scale.reshape(1, c)
    return pl.pallas_call(
        _pool_sample_kernel,
        in_specs=[full(state_t), full(u), full(w_pool), full(ps)],
        out_specs=[pl.BlockSpec((bd, c), lambda: (0, 0)), full(state_t)],
        out_shape=[jax.ShapeDtypeStruct((bd, c), BF16), jax.ShapeDtypeStruct(state_t.shape, F32)],
        compiler_params=pltpu.CompilerParams(vmem_limit_bytes=VMEM_LIMIT),
        name="pool_sample",
    )(state_t, u, w_pool, ps)


def _route_kernel(x_ref, attn_ref, pool_ref, gm_ref, sh_ref, sc_ref, wo_ref, nf_ref, wr_ref, br_ref,
                  tri_t_ref, tri_e_ref,
                  x2_ref, hs_ref, slot_ref, cnt_ref):
    sub = slot_ref.shape[0]
    n = x_ref.shape[0]
    tile = n // sub
    mix = (jnp.dot(attn_ref[...], wo_ref[:ATTN_WIDTH, :], preferred_element_type=F32)
           + jnp.dot(pool_ref[...], wo_ref[ATTN_WIDTH:, :], preferred_element_type=F32))
    x2 = x_ref[...] + gm_ref[...] * mix
    x2_ref[...] = x2
    h = _rms(x2, nf_ref[...]) * (1.0 + sc_ref[...]) + sh_ref[...]

    hh, hl = _split(h)
    wh, wl = _split(wr_ref[...])
    nt = functools.partial(lax.dot_general, dimension_numbers=(((1,), (1,)), ((), ())),
                           preferred_element_type=F32)
    logits = nt(wh, hh) + nt(wl, hh) + nt(wh, hl) + br_ref[...]

    eidx = lax.broadcasted_iota(jnp.int32, (N_EXPERTS, n), 0).astype(F32)
    work = logits
    tops, picks = [], []
    for _ in range(TOP_K):
        m = jnp.max(work, axis=0, keepdims=True)
        pick = jnp.min(jnp.where(work == m, eidx, float(N_EXPERTS)), axis=0, keepdims=True)
        work = jnp.where(eidx == pick, -jnp.inf, work)
        tops.append(m)
        picks.append(pick)
    ex = [jnp.exp(v - tops[0]) for v in tops]
    den = ex[0] + ex[1] + ex[2] + ex[3]
    gates = [e / den for e in ex]

    sel = jnp.zeros((N_EXPERTS, n), F32)
    for pick in picks:
        sel = sel + (eidx == pick).astype(F32)
    selb = sel.astype(BF16)
    rank = jnp.concatenate([jnp.dot(selb[:, t * tile:(t + 1) * tile], tri_t_ref[...], preferred_element_type=F32)
                            for t in range(sub)], axis=1)
    cnts = [jnp.sum(sel[:, t * tile:(t + 1) * tile], axis=1, keepdims=True) for t in range(sub)]
    padded = jnp.concatenate(
        [jnp.broadcast_to(jnp.ceil(c * (1.0 / CHUNK)) * CHUNK, (N_EXPERTS, LANES)) for c in cnts], axis=1)
    seg = jnp.dot(tri_e_ref[...], padded.astype(BF16), preferred_element_type=F32)
    dest = jnp.concatenate([seg[:, t * LANES:t * LANES + 1] + rank[:, t * tile:(t + 1) * tile]
                            for t in range(sub)], axis=1)
    slots = [jnp.sum(jnp.where(eidx == pick, dest, 0.0), axis=0, keepdims=True) for pick in picks]

    ridx = lax.broadcasted_iota(jnp.int32, (TILE_ROWS, tile), 0).astype(F32)
    for t in range(sub):
        cols = slice(t * tile, (t + 1) * tile)
        cnt_ref[t] = jnp.broadcast_to(cnts[t], (N_EXPERTS, LANES))
        slot_ref[t] = jnp.concatenate([v[:, cols] for v in slots + gates], axis=0)
        hit = ridx == slots[0][:, cols]
        for s in slots[1:]:
            hit = jnp.logical_or(hit, ridx == s[:, cols])
        perm = jnp.where(hit, 1.0, 0.0).astype(BF16)
        hs_ref[pl.ds(t * TILE_ROWS, TILE_ROWS), :] = _pack_rows(
            jnp.dot(perm, hh[t * tile:(t + 1) * tile, :], preferred_element_type=F32))


def _route(x2d, attn, pool, mod, w_out, norm_ffn, w_router_t, b_router, hs_prev, *, tile, sub,
           rows_per_mod, tile0, extra_tiles):
    n = x2d.shape[0]
    nt = n // tile
    assert nt % sub == 0 and extra_tiles % sub == 0
    own_steps = nt // sub
    steps = own_steps + extra_tiles // sub
    mrows = mod.shape[2]
    assert mrows == 1 or (sub == 1 and mrows == tile)
    last = lambda i: jnp.minimum(i, own_steps - 1)
    group = lambda i: (last(i) * sub * tile) // rows_per_mod
    const = lambda shape: pl.BlockSpec(shape, lambda i: (0,) * len(shape))
    row = lambda w: pl.BlockSpec((sub * tile, w), lambda i: (last(i), 0))
    tri_t = (jnp.arange(tile)[:, None] < jnp.arange(tile)[None, :]).astype(BF16)
    tri_e = (jnp.arange(N_EXPERTS)[None, :] < jnp.arange(N_EXPERTS)[:, None]).astype(BF16)
    in_specs = [row(D_MODEL), row(ATTN_WIDTH), row(POOL_WIDTH),
                _mod_spec(mod, 2, group), _mod_spec(mod, 3, group), _mod_spec(mod, 4, group),
                const((D_MODEL, D_MODEL)), const((1, D_MODEL)), const((N_EXPERTS, D_MODEL)),
                const((N_EXPERTS, 1)), const((tile, tile)), const((N_EXPERTS, N_EXPERTS))]
    args = [x2d, attn, pool, mod, mod, mod, w_out, norm_ffn.reshape(1, D_MODEL), w_router_t,
            b_router.reshape(N_EXPERTS, 1), tri_t, tri_e]
    n_in = len(args)
    kern = _route_kernel
    aliases = {}
    hs_rows = (tile0 + steps * sub) * TILE_ROWS
    assert tile0 % sub == 0
    if hs_prev is not None:
        in_specs.append(pl.BlockSpec(memory_space=pl.ANY))
        args.append(hs_prev)
        aliases = {n_in: 1}
        kern = lambda *refs: _route_kernel(*refs[:n_in], *refs[n_in + 1:])
        hs_rows = hs_prev.shape[0]
    return pl.pallas_call(
        kern,
        grid=(steps,),
        in_specs=in_specs,
        out_specs=[row(D_MODEL),
                   pl.BlockSpec((sub * TILE_ROWS, ROW_W), lambda i: (i + tile0 // sub, 0)),
                   pl.BlockSpec((sub, 2 * TOP_K, tile), lambda i: (last(i), 0, 0)),
                   pl.BlockSpec((sub, N_EXPERTS, LANES), lambda i: (last(i), 0, 0))],
        out_shape=[jax.ShapeDtypeStruct((n, D_MODEL), F32),
                   jax.ShapeDtypeStruct((hs_rows, ROW_W), jnp.int32),
                   jax.ShapeDtypeStruct((nt, 2 * TOP_K, tile), F32),
                   jax.ShapeDtypeStruct((nt, N_EXPERTS, LANES), F32)],
        input_output_aliases=aliases,
        compiler_params=_cparams(("arbitrary",)),
        name="route",
    )(*args)


def _moe_kernel(src_ref, be_ref, nxt_ref, nb_ref, hs_hbm, w1_hbm, b1_ref, w2_hbm, b2_ref, out_hbm,
                lhs, obuf, w1s, w2s, w1c, w2c, sem_in, sem_out, sem_w, *, scratch_chunk):
    nb = nb_ref[0]

    def weight_copies(e):
        return (pltpu.make_async_copy(w1_hbm.at[e], w1s, sem_w.at[0]),
                pltpu.make_async_copy(w2_hbm.at[e], w2s, sem_w.at[1]))

    def chunk_rows(c):
        return pl.ds(pl.multiple_of(c * CHUNK, CHUNK), CHUNK)

    def start_in(blk, s):
        for j in range(BLOCK_CHUNKS):
            c = src_ref[blk * BLOCK_CHUNKS + j]
            c = jnp.where(c < 0, ZERO_CHUNK, c)
            pltpu.make_async_copy(hs_hbm.at[chunk_rows(c)], lhs.at[s, pl.ds(j * CHUNK, CHUNK)],
                                  sem_in.at[s]).start()

    def wait_in(s):
        pltpu.make_async_copy(hs_hbm.at[pl.ds(0, MOE_BLOCK)], lhs.at[s], sem_in.at[s]).wait()

    def start_out(blk, s):
        for j in range(BLOCK_CHUNKS):
            c = src_ref[blk * BLOCK_CHUNKS + j]
            c = jnp.where(c < 0, scratch_chunk + s * BLOCK_CHUNKS + j, c)
            pltpu.make_async_copy(obuf.at[s, pl.ds(j * CHUNK, CHUNK)],
                                  out_hbm.at[chunk_rows(c), pl.ds(0, PACKED_W)], sem_out.at[s]).start()

    def wait_out(s):
        pltpu.make_async_copy(obuf.at[s], out_hbm.at[pl.ds(0, MOE_BLOCK), pl.ds(0, PACKED_W)],
                              sem_out.at[s]).wait()

    @pl.when(nb > 0)
    def _():
        start_in(0, 0)
        for cp in weight_copies(be_ref[0]):
            cp.start(priority=WEIGHT_DMA_PRIORITY)

    def block(b, carry):
        slot = b % 2
        e = be_ref[b]

        @pl.when(jnp.logical_or(b == 0, e != be_ref[jnp.maximum(b - 1, 0)]))
        def _():
            for cp in weight_copies(e):
                cp.wait()
            w1c[...] = w1s[...].astype(BF16)
            w2c[...] = w2s[...].astype(BF16)
            nxt = nxt_ref[b]

            @pl.when(nxt >= 0)
            def _():
                for cp in weight_copies(nxt):
                    cp.start(priority=WEIGHT_DMA_PRIORITY)

        wait_in(slot)

        @pl.when(b >= 2)
        def _():
            wait_out(slot)

        start_in(b + 1, 1 - slot)

        def ffn(nrows):
            xh, xl = _unpack_rows(lhs[slot, 0:nrows, :])
            gu = (jnp.dot(xh, w1c[:PACKED_W, :], preferred_element_type=F32)
                  + jnp.dot(xl, w1c[PACKED_W:, :], preferred_element_type=F32) + b1_ref[e])
            gate = jnp.minimum(gu[:, :D_FF], SWIGLU_LIMIT)
            up = jnp.clip(gu[:, D_FF:], -SWIGLU_LIMIT, SWIGLU_LIMIT)
            act = (up + 1.0) * (gate * jax.nn.sigmoid(SWIGLU_ALPHA * gate))
            y = jnp.dot(act.astype(BF16), w2c[...], preferred_element_type=F32) + b2_ref[e]
            obuf[slot, 0:nrows, :] = _pack_rows(y.astype(BF16).astype(F32))

        real = nb_ref[1 + b]

        @pl.when(real > BLOCK_CHUNKS // 2)
        def _():
            ffn(MOE_BLOCK)

        @pl.when(real <= BLOCK_CHUNKS // 2)
        def _():
            ffn(MOE_BLOCK // 2)

        start_out(b, slot)
        return carry

    obuf[...] = jnp.zeros(obuf.shape, jnp.int32)
    lax.fori_loop(0, nb, block, 0)

    @pl.when(nb > 0)
    def _():
        last_slot = (nb - 1) % 2
        wait_in(1 - last_slot)

        @pl.when(nb >= 2)
        def _():
            wait_out(1 - last_slot)
        wait_out(last_slot)


def _moe(hs, src, block_e, next_e, nblocks, w1, b1, w2, b2, scratch_chunk):
    full = lambda shape: pl.BlockSpec(shape, lambda i, *_: (0,) * len(shape))
    hbm = pl.BlockSpec(memory_space=pl.ANY)
    return pl.pallas_call(
        functools.partial(_moe_kernel, scratch_chunk=scratch_chunk),
        grid_spec=pltpu.PrefetchScalarGridSpec(
            num_scalar_prefetch=4,
            grid=(1,),
            in_specs=[hbm, hbm, full((N_EXPERTS, 1, 2 * D_FF)), hbm, full((N_EXPERTS, 1, D_MODEL))],
            out_specs=hbm,
            scratch_shapes=[pltpu.VMEM((2, MOE_BLOCK, ROW_W), jnp.int32),
                            pltpu.VMEM((2, MOE_BLOCK, PACKED_W), jnp.int32),
                            pltpu.VMEM((D_MODEL, 2 * D_FF), F32),
                            pltpu.VMEM((D_FF, D_MODEL), F32),
                            pltpu.VMEM((D_MODEL, 2 * D_FF), BF16),
                            pltpu.VMEM((D_FF, D_MODEL), BF16),
                            pltpu.SemaphoreType.DMA((2,)),
                            pltpu.SemaphoreType.DMA((2,)),
                            pltpu.SemaphoreType.DMA((2,))]),
        out_shape=jax.ShapeDtypeStruct(hs.shape, jnp.int32),
        input_output_aliases={4: 0},
        compiler_params=_cparams(("arbitrary",)),
        name="moe_experts",
    )(src, block_e, next_e, nblocks, hs, w1, b1.reshape(N_EXPERTS, 1, 2 * D_FF), w2,
      b2.reshape(N_EXPERTS, 1, D_MODEL))


def _plan(cnt, n_blocks_max):
    nt = cnt.shape[0]
    nch = (cnt + (CHUNK - 1)) // CHUNK
    lstart = jnp.cumsum(nch, axis=1) - nch
    ne = jnp.sum(nch, axis=0)
    nbe = (ne + (BLOCK_CHUNKS - 1)) // BLOCK_CHUNKS
    bend = jnp.cumsum(nbe)
    nblocks = bend[-1]
    gstart = (bend - nbe)[None, :] * BLOCK_CHUNKS + (jnp.cumsum(nch, axis=0) - nch)
    s0 = jnp.arange(nt, dtype=jnp.int32)[:, None] * TILE_CHUNKS + lstart
    blk = jnp.arange(n_blocks_max + 1, dtype=jnp.int32)
    be = jnp.sum((blk[:, None] >= bend[None, :]).astype(jnp.int32), axis=1)
    be_last = jnp.sum((nblocks - 1 >= bend).astype(jnp.int32))
    be = jnp.minimum(be, be_last).astype(jnp.int32)
    eid = jnp.arange(N_EXPERTS, dtype=jnp.int32)
    strips = jnp.stack([gstart.T, nch.T, s0.T])
    mine = (be[:, None] == eid[None, :])[None, :, :, None]
    gs_b, nc_b, s0_b = jnp.sum(jnp.where(mine, strips[:, None], 0), axis=2)
    c = (blk[:, None] * BLOCK_CHUNKS + jnp.arange(BLOCK_CHUNKS, dtype=jnp.int32)[None, :])[:, :, None]
    inside = jnp.logical_and(c >= gs_b[:, None, :], c < (gs_b + nc_b)[:, None, :])
    src = (jnp.sum(jnp.where(inside, (s0_b - gs_b)[:, None, :] + c + 1, 0), axis=2) - 1).reshape(-1)
    be = be[:n_blocks_max]
    later = jnp.logical_and(eid[None, :] > be[:, None], (nbe > 0)[None, :])
    nxt = jnp.min(jnp.where(later, eid[None, :], N_EXPERTS), axis=1)
    nxt = jnp.where(nxt == N_EXPERTS, -1, nxt).astype(jnp.int32)
    real = jnp.sum((src.reshape(-1, BLOCK_CHUNKS) >= 0).astype(jnp.int32), axis=1)
    return src, be, nxt, jnp.concatenate([nblocks.reshape(1), real]).astype(jnp.int32)


def _combine_kernel(o_ref, slot_ref, x2_ref, gf_ref, y_ref):
    sub = slot_ref.shape[0]
    tile = x2_ref.shape[0] // sub
    ridx = lax.broadcasted_iota(jnp.int32, (TILE_ROWS, tile), 0).astype(F32)
    tn = functools.partial(lax.dot_general, dimension_numbers=(((0,), (0,)), ((), ())),
                           preferred_element_type=F32)
    for t in range(sub):
        gmat = jnp.zeros((TILE_ROWS, tile), F32)
        for k in range(TOP_K):
            gmat = jnp.where(ridx == slot_ref[t, k:k + 1, :], slot_ref[t, TOP_K + k:TOP_K + k + 1, :], gmat)
        gb = gmat.astype(BF16)
        oh, ol = _unpack_rows(o_ref[pl.ds(t * TILE_ROWS, TILE_ROWS), :])
        y = jnp.concatenate([tn(gb, oh), tn(gb, ol)], axis=1)
        rows = pl.ds(t * tile, tile)
        gf = gf_ref[...] if gf_ref.shape[0] == 1 else gf_ref[rows, :]
        y_ref[rows, :] = x2_ref[rows, :] + gf * y


def _combine(outs, slots, x2, mod, *, tile, sub, rows_per_mod, tile0):
    n = x2.shape[0]
    mrows = mod.shape[2]
    assert n % (sub * tile) == 0 and tile0 % sub == 0 and (mrows == 1 or sub == 1)
    return pl.pallas_call(
        _combine_kernel,
        grid=(n // (sub * tile),),
        in_specs=[pl.BlockSpec((sub * TILE_ROWS, PACKED_W), lambda i: (i + tile0 // sub, 0)),
                  pl.BlockSpec((sub, 2 * TOP_K, tile), lambda i: (i, 0, 0)),
                  pl.BlockSpec((sub * tile, D_MODEL), lambda i: (i, 0)),
                  _mod_spec(mod, 5, lambda i: (i * sub * tile) // rows_per_mod)],
        out_specs=pl.BlockSpec((sub * tile, D_MODEL), lambda i: (i, 0)),
        out_shape=jax.ShapeDtypeStruct((n, D_MODEL), F32),
        compiler_params=_cparams(("parallel",)),
        name="combine",
    )(outs, slots, x2, mod)


def kernel(x_prompt, x_sample, cache_k, cache_v, state_pool, c_prompt, c_sample, rel_bias, norm_mix, w_ada,
           b_ada, w_in, q_norm, k_norm, sinks, w_pool, pool_scale, w_out, norm_ffn, w_router, b_router,
           w1, b1, w2, b2):
    depth = w_in.shape[0]
    assert depth == 1
    l = 0
    bp, sp, _ = x_prompt.shape
    bs = x_sample.shape[0]
    assert x_sample.shape[1] == 1 and sp % SORT_TILE == 0 and bs <= SORT_TILE
    n_p = bp * sp
    tiles_p = n_p // SORT_TILE
    max_chunks = tiles_p * (SORT_TILE * TOP_K // CHUNK + N_EXPERTS) + (bs * TOP_K // CHUNK + N_EXPERTS)
    n_blocks_max = -(-max_chunks // BLOCK_CHUNKS) + N_EXPERTS

    pr = -(-bp // SUBLANES) * SUBLANES
    c_all = jnp.concatenate([c_prompt, jnp.zeros((pr - bp, D_MODEL), F32), c_sample], axis=0)
    mod = _modulation(c_all, w_ada[l], b_ada[l])
    mod_p = mod[:, :bp].reshape(6, bp, 1, D_MODEL)
    mod_s = mod[:, pr:].reshape(6, 1, bs, D_MODEL)

    head = jnp.arange(ATTN_WIDTH) // HEAD_DIM
    bd = (head[:, None] == head[None, :]).astype(BF16)
    w_in_b = w_in[l].astype(BF16)
    w_out_b = w_out[l].astype(BF16)
    w_pool_b = w_pool[l].astype(BF16)
    w_router_t = w_router[l].T

    xp = x_prompt.reshape(n_p, D_MODEL)
    q, k, v, u = _mixer_inputs(xp, mod_p, norm_mix[l], w_in_b, bd, q_norm[l], k_norm[l],
                               tile=1024, rows_per_mod=sp, precise=False)
    k3 = k.reshape(bp, sp, KV_WIDTH)
    v3 = v.reshape(bp, sp, KV_WIDTH)
    u3 = u.reshape(bp, sp, POOL_WIDTH)
    attn = _attn_prompt(q.reshape(bp, sp, ATTN_WIDTH), k3, v3, sinks[l], rel_bias)
    pool = _pool_prompt(u3, w_pool_b, pool_scale[l])
    keep = min(WINDOW, sp)
    nkp = k3[:, -keep:].reshape(bp, keep, N_KV_HEADS, HEAD_DIM)
    nvp = v3[:, -keep:].reshape(bp, keep, N_KV_HEADS, HEAD_DIM)
    if sp >= POOL_BUF:
        npp = u3[:, -POOL_BUF:]
    else:
        npp = jnp.concatenate([jnp.zeros((bp, POOL_BUF - sp, POOL_WIDTH), F32), u3], axis=1)
    x2_p, hs, slots_p, cnt_p = _route(
        xp, attn.reshape(n_p, ATTN_WIDTH), pool.reshape(n_p, POOL_WIDTH), mod_p,
        w_out_b, norm_ffn[l], w_router_t, b_router[l], None,
        tile=SORT_TILE, sub=ROUTE_SUB, rows_per_mod=sp, tile0=0, extra_tiles=max(2, ROUTE_SUB))

    xs = x_sample.reshape(bs, D_MODEL)
    qs, ks, vs, us = _mixer_inputs(xs, mod_s, norm_mix[l], w_in[l], bd, q_norm[l], k_norm[l],
                                   tile=bs, rows_per_mod=bs, precise=True)
    wbuf = cache_k.shape[2]
    attn_s, nks, nvs = _attn_sample(qs, ks, vs, cache_k[l].reshape(bs, wbuf, KV_WIDTH),
                                    cache_v[l].reshape(bs, wbuf, KV_WIDTH), sinks[l], rel_bias)
    pool_s, nps_t = _pool_sample(jnp.swapaxes(state_pool[l], 0, 1), us, w_pool_b, pool_scale[l])
    x2_s, hs, slots_s, cnt_s = _route(
        xs, attn_s.astype(BF16), pool_s, mod_s, w_out_b, norm_ffn[l],
        w_router_t, b_router[l], hs, tile=bs, sub=1, rows_per_mod=bs, tile0=tiles_p, extra_tiles=0)

    cnt = jnp.concatenate([cnt_p[:, :, 0], cnt_s[:, :, 0]], axis=0).astype(jnp.int32)
    src, block_e, next_e, nblocks = _plan(cnt, n_blocks_max)
    outs = _moe(hs, src, block_e, next_e, nblocks, w1[l], b1[l], w2[l], b2[l],
                scratch_chunk=(tiles_p + 1) * TILE_CHUNKS)

    y_p = _combine(outs, slots_p, x2_p, mod_p, tile=SORT_TILE, sub=COMBINE_SUB, rows_per_mod=sp, tile0=0)
    y_s = _combine(outs, slots_s, x2_s, mod_s, tile=bs, sub=1, rows_per_mod=bs, tile0=tiles_p)

    return (y_p.reshape(bp, sp, D_MODEL), y_s.reshape(bs, 1, D_MODEL),
            nkp[None], nvp[None], npp[None],
            nks.reshape(1, bs, wbuf, N_KV_HEADS, HEAD_DIM), nvs.reshape(1, bs, wbuf, N_KV_HEADS, HEAD_DIM),
            jnp.swapaxes(nps_t, 0, 1)[None])
```

```python
import functools
import math

import jax
import jax.numpy as jnp
from jax import lax
from jax.experimental import pallas as pl
from jax.experimental.pallas import tpu as pltpu

F32 = jnp.float32
BF16 = jnp.bfloat16

D_MODEL = 1024
HEAD_DIM = 64
N_HEADS = 8
N_KV_HEADS = 2
GROUP = N_HEADS // N_KV_HEADS
ATTN_WIDTH = N_HEADS * HEAD_DIM
KV_WIDTH = N_KV_HEADS * HEAD_DIM
POOL_WIDTH = D_MODEL - ATTN_WIDTH
POOL_WINDOWS = (2, 4, 8, 16)
POOL_GROUP = POOL_WIDTH // len(POOL_WINDOWS)
POOL_BUF = max(POOL_WINDOWS) - 1
IN_WIDTH = ATTN_WIDTH + 2 * KV_WIDTH + POOL_WIDTH
WINDOW = 128
ATTN_BLOCK = 128
N_BUCKETS = 32
MAX_EXACT = 16
REL_MAX_DIST = 128
N_EXPERTS = 32
TOP_K = 4
D_FF = D_MODEL
SWIGLU_LIMIT = 7.0
SWIGLU_ALPHA = 1.702
EPS = 1e-6
NEG_INF = -1e30
PAST_LEN = 16384

LANES = 128
SUBLANES = 8
VMEM_LIMIT = 56 * 1024 * 1024

ATTN_QB = 8

SORT_TILE = 256
ROUTE_SUB = 4
COMBINE_SUB = 4
CHUNK = SUBLANES
TILE_ROWS = -(-(SORT_TILE * TOP_K + N_EXPERTS * (CHUNK - 1)) // LANES) * LANES
TILE_CHUNKS = TILE_ROWS // CHUNK
MOE_BLOCK = 256
BLOCK_CHUNKS = MOE_BLOCK // CHUNK
WEIGHT_DMA_PRIORITY = 1
ROWS_PER_WORD = 2
CHUNK_W = CHUNK // ROWS_PER_WORD
TILE_W = TILE_ROWS // ROWS_PER_WORD
BLOCK_W = MOE_BLOCK // ROWS_PER_WORD
ZERO_CHUNK = TILE_CHUNKS - 1


def _bdot(a, b):
    return jnp.dot(a.astype(BF16), b.astype(BF16), preferred_element_type=F32)


def _split(a):
    hi = a.astype(BF16)
    lo = (a - hi.astype(F32)).astype(BF16)
    return hi, lo


def _dot3(a, b):
    ah, al = _split(a)
    bh, bl = _split(b)
    d = functools.partial(jnp.dot, preferred_element_type=F32)
    return d(ah, bh) + d(al, bh) + d(ah, bl)


def _pack_rows(x):
    return pltpu.bitcast(x.astype(BF16), jnp.int32)


def _unpack_rows(w):
    return pltpu.bitcast(w, BF16)


def _rms(x, g):
    return x * lax.rsqrt(jnp.mean(x * x, axis=-1, keepdims=True) + EPS) * g


def _cparams(sem, **kw):
    return pltpu.CompilerParams(dimension_semantics=sem, vmem_limit_bytes=VMEM_LIMIT, **kw)


def _ada_kernel(c_ref, w_ref, b_ref, o_ref):
    c = c_ref[...]
    s = c * jax.nn.sigmoid(c)
    o_ref[...] = _dot3(s, w_ref[...]) + b_ref[...]


def _modulation(c, w_ada, b_ada):
    rows = c.shape[0]
    n = w_ada.shape[1]
    tn = D_MODEL
    return pl.pallas_call(
        _ada_kernel,
        grid=(n // tn,),
        in_specs=[pl.BlockSpec((rows, D_MODEL), lambda j: (0, 0)),
                  pl.BlockSpec((D_MODEL, tn), lambda j: (0, j)),
                  pl.BlockSpec((1, tn), lambda j: (0, j))],
        out_specs=pl.BlockSpec((None, rows, tn), lambda j: (j, 0, 0)),
        out_shape=jax.ShapeDtypeStruct((n // tn, rows, tn), F32),
        compiler_params=_cparams(("parallel",)),
        name="modulation",
    )(c, w_ada, b_ada.reshape(1, n))


def _head_rms(t, bd, g, precise):
    if precise:
        hi, lo = _split(t * t)
        ss = jnp.dot(hi, bd, preferred_element_type=F32) + jnp.dot(lo, bd, preferred_element_type=F32)
    else:
        ss = _bdot(t * t, bd)
    return t * lax.rsqrt(ss * (1.0 / HEAD_DIM) + EPS) * g


def _mixin_kernel(x_ref, sh_ref, sc_ref, g_ref, w_ref, bd_ref, qn_ref, kn_ref,
                  q_ref, k_ref, v_ref, u_ref, *, precise):
    h = _rms(x_ref[...], g_ref[...]) * (1.0 + sc_ref[...]) + sh_ref[...]
    z = _dot3(h, w_ref[...]) if precise else _bdot(h, w_ref[...])
    q = z[:, :ATTN_WIDTH]
    k = z[:, ATTN_WIDTH:ATTN_WIDTH + KV_WIDTH]
    bd = bd_ref[...]
    q = _head_rms(q, bd, qn_ref[...], precise)
    k = _head_rms(k, bd[:KV_WIDTH, :KV_WIDTH], kn_ref[...], precise)
    q_ref[...] = (q * (HEAD_DIM ** -0.5)).astype(BF16)
    k_ref[...] = k
    v_ref[...] = z[:, ATTN_WIDTH + KV_WIDTH:ATTN_WIDTH + 2 * KV_WIDTH]
    u_ref[...] = z[:, ATTN_WIDTH + 2 * KV_WIDTH:]


def _mod_spec(mod, k, group_of_step):
    return pl.BlockSpec((None, None, mod.shape[2], D_MODEL), lambda i: (k, group_of_step(i), 0, 0))


def _mixer_inputs(x2d, mod, norm_mix, w_in, bd, q_norm, k_norm, *, tile, rows_per_mod, precise):
    n = x2d.shape[0]
    group = lambda i: (i * tile) // rows_per_mod
    const = lambda shape: pl.BlockSpec(shape, lambda i: (0,) * len(shape))
    row = lambda w: pl.BlockSpec((tile, w), lambda i: (i, 0))
    return pl.pallas_call(
        functools.partial(_mixin_kernel, precise=precise),
        grid=(n // tile,),
        in_specs=[row(D_MODEL), _mod_spec(mod, 0, group), _mod_spec(mod, 1, group), const((1, D_MODEL)),
                  const((D_MODEL, IN_WIDTH)), const((ATTN_WIDTH, ATTN_WIDTH)), const((1, ATTN_WIDTH)),
                  const((1, KV_WIDTH))],
        out_specs=[row(ATTN_WIDTH), row(KV_WIDTH), row(KV_WIDTH), row(POOL_WIDTH)],
        out_shape=[jax.ShapeDtypeStruct((n, ATTN_WIDTH), BF16),
                   jax.ShapeDtypeStruct((n, KV_WIDTH), F32),
                   jax.ShapeDtypeStruct((n, KV_WIDTH), F32),
                   jax.ShapeDtypeStruct((n, POOL_WIDTH), F32)],
        compiler_params=_cparams(("parallel",)),
        name="mixer_inputs",
    )(x2d, mod, mod, norm_mix.reshape(1, D_MODEL), w_in, bd,
      jnp.tile(q_norm, N_HEADS).reshape(1, ATTN_WIDTH), jnp.tile(k_norm, N_KV_HEADS).reshape(1, KV_WIDTH))


def _t5_bucket(rel):
    n = jnp.maximum(rel, 0)
    nf = jnp.maximum(n, 1).astype(F32)
    large = MAX_EXACT + (jnp.log(nf / MAX_EXACT) / math.log(REL_MAX_DIST / MAX_EXACT)
                         * (N_BUCKETS - MAX_EXACT)).astype(jnp.int32)
    large = jnp.minimum(large, N_BUCKETS - 1)
    return jnp.where(n < MAX_EXACT, n, large)


def _bias_table(rel, rel_table):
    bucket = _t5_bucket(rel)
    table = rel_table.astype(F32)
    ids = jnp.arange(N_BUCKETS, dtype=bucket.dtype).reshape((N_BUCKETS, 1) + (1,) * rel.ndim)
    onehot = bucket[None, None] == ids
    bias = jnp.sum(jnp.where(onehot, table.reshape(table.shape + (1,) * rel.ndim), 0.0), axis=0)
    valid = (rel >= 0) & (rel < WINDOW)
    return jnp.where(valid[None], bias, NEG_INF)


def _attn_prompt_kernel(sink_ref, q_ref, kp_ref, kc_ref, vp_ref, vc_ref, bias_ref, o_ref):
    first = pl.program_id(1) == 0
    kk = jnp.concatenate([kp_ref[...], kc_ref[...]], axis=0).astype(BF16)
    vv = jnp.concatenate([vp_ref[...], vc_ref[...]], axis=0).astype(BF16)
    key = lax.broadcasted_iota(jnp.int32, (2 * ATTN_BLOCK, 1), 0)
    no_prev = jnp.logical_and(first, key < ATTN_BLOCK)
    lane = lax.broadcasted_iota(jnp.int32, (1, N_HEADS * ATTN_BLOCK), 1)
    sink = jnp.zeros((1, N_HEADS * ATTN_BLOCK), F32)
    for h in range(N_HEADS):
        sink = jnp.where(lane // ATTN_BLOCK == h, sink_ref[h], sink)
    contract = lambda a, b, dims: lax.dot_general(a, b, (dims, ((), ())), preferred_element_type=F32)
    part = GROUP * ATTN_BLOCK
    for i in range(ATTN_QB):
        q = q_ref[i * ATTN_BLOCK:(i + 1) * ATTN_BLOCK, :]
        keys = slice(i * ATTN_BLOCK, (i + 2) * ATTN_BLOCK)
        scores = []
        for kv in range(N_KV_HEADS):
            heads = range(kv * GROUP, (kv + 1) * GROUP)
            qg = jnp.concatenate([q[:, h * HEAD_DIM:(h + 1) * HEAD_DIM] for h in heads], axis=0)
            scores.append(contract(kk[keys, kv * HEAD_DIM:(kv + 1) * HEAD_DIM], qg, ((1,), (1,))))
        s = jnp.concatenate(scores, axis=1) + bias_ref[...]
        if i == 0:
            s = jnp.where(no_prev, NEG_INF, s)
        m = jnp.maximum(jnp.max(s, axis=0, keepdims=True), sink)
        p = jnp.exp(s - m)
        denom = jnp.sum(p, axis=0, keepdims=True) + jnp.exp(sink - m)
        p = p.astype(BF16)
        halves = [contract(vv[keys, kv * HEAD_DIM:(kv + 1) * HEAD_DIM], p[:, kv * part:(kv + 1) * part],
                           ((0,), (0,))) / denom[:, kv * part:(kv + 1) * part]
                  for kv in range(N_KV_HEADS)]
        o_t = jnp.concatenate(halves, axis=0)
        per_g = [o_t[:, g * ATTN_BLOCK:(g + 1) * ATTN_BLOCK].T for g in range(GROUP)]
        out = [t[:, kv * HEAD_DIM:(kv + 1) * HEAD_DIM] for kv in range(N_KV_HEADS) for t in per_g]
        o_ref[i * ATTN_BLOCK:(i + 1) * ATTN_BLOCK, :] = jnp.concatenate(out, axis=-1).astype(BF16)


def _attn_prompt(q, k, v, sinks, rel_table):
    b, s = q.shape[:2]
    qrows = ATTN_QB * ATTN_BLOCK
    assert s % qrows == 0
    qi = jnp.arange(ATTN_BLOCK, dtype=jnp.int32)[:, None]
    si = jnp.arange(2 * ATTN_BLOCK, dtype=jnp.int32)[None, :]
    bias = _bias_table(qi + ATTN_BLOCK - si, rel_table)
    bias = bias.reshape(N_HEADS * ATTN_BLOCK, 2 * ATTN_BLOCK).T
    cur = lambda w: pl.BlockSpec((None, qrows, w), lambda i, j, *_: (i, j, 0))
    prev = lambda w: pl.BlockSpec((None, ATTN_BLOCK, w),
                                  lambda i, j, *_: (i, jnp.maximum(j * ATTN_QB - 1, 0), 0))
    return pl.pallas_call(
        _attn_prompt_kernel,
        grid_spec=pltpu.PrefetchScalarGridSpec(
            num_scalar_prefetch=1,
            grid=(b, s // qrows),
            in_specs=[cur(ATTN_WIDTH), prev(KV_WIDTH), cur(KV_WIDTH), prev(KV_WIDTH), cur(KV_WIDTH),
                      pl.BlockSpec(bias.shape, lambda i, j, *_: (0, 0))],
            out_specs=cur(ATTN_WIDTH)),
        out_shape=jax.ShapeDtypeStruct((b, s, ATTN_WIDTH), BF16),
        compiler_params=_cparams(("parallel", "parallel")),
        name="attn_prompt",
    )(sinks.astype(F32), q, k, k, v, v, bias)


def _attn_sample_kernel(sink_ref, q_ref, kc_ref, vc_ref, kn_ref, vn_ref, bias_ref, bnew_ref,
                        o_ref, nk_ref, nv_ref):
    kc = kc_ref[...]
    vc = vc_ref[...]
    kn = kn_ref[...]
    vn = vn_ref[...]
    w = kc.shape[1]
    pos = lax.broadcasted_iota(jnp.int32, kc.shape, 1)
    nk_ref[...] = jnp.where(pos == w - 1, kn[:, None, :], pltpu.roll(kc, w - 1, 1))
    nv_ref[...] = jnp.where(pos == w - 1, vn[:, None, :], pltpu.roll(vc, w - 1, 1))
    gi = lax.broadcasted_iota(jnp.int32, (1, GROUP, 1), 1)
    for kv in range(N_KV_HEADS):
        sl = slice(kv * HEAD_DIM, (kv + 1) * HEAD_DIM)
        qg = q_ref[:, kv]
        s = jnp.einsum('bgd,bsd->bgs', qg, kc[:, :, sl].astype(BF16), preferred_element_type=F32)
        s = s + bias_ref[kv][None]
        s_new = jnp.sum(qg.astype(F32) * kn[:, None, sl], axis=-1, keepdims=True) + bnew_ref[kv][None]
        sink = jnp.zeros((1, GROUP, 1), F32)
        for g in range(GROUP):
            sink = jnp.where(gi == g, sink_ref[kv * GROUP + g], sink)
        m = jnp.maximum(jnp.maximum(jnp.max(s, axis=-1, keepdims=True), s_new), sink)
        p = jnp.exp(s - m)
        p_new = jnp.exp(s_new - m)
        denom = jnp.sum(p, axis=-1, keepdims=True) + p_new + jnp.exp(sink - m)
        o = jnp.einsum('bgs,bsd->bgd', p.astype(BF16), vc[:, :, sl].astype(BF16), preferred_element_type=F32)
        o = o + p_new * vn[:, None, sl]
        o_ref[:, kv] = o / denom


def _attn_sample(q, k_new, v_new, cache_k, cache_v, sinks, rel_table, *, tile=32):
    bd, w = cache_k.shape[:2]
    rel = w - jnp.arange(w, dtype=jnp.int32)
    bias = _bias_table(rel, rel_table).reshape(N_KV_HEADS, GROUP, w)
    bnew = _bias_table(jnp.zeros((1,), jnp.int32), rel_table).reshape(N_KV_HEADS, GROUP, 1)
    q4 = q.reshape(bd, N_KV_HEADS, GROUP, HEAD_DIM)
    spec4 = pl.BlockSpec((tile, N_KV_HEADS, GROUP, HEAD_DIM), lambda i, *_: (i, 0, 0, 0))
    cache = pl.BlockSpec((tile, w, KV_WIDTH), lambda i, *_: (i, 0, 0))
    new = pl.BlockSpec((tile, KV_WIDTH), lambda i, *_: (i, 0))
    const3 = lambda a: pl.BlockSpec(a.shape, lambda i, *_: (0, 0, 0))
    o, nk, nv = pl.pallas_call(
        _attn_sample_kernel,
        grid_spec=pltpu.PrefetchScalarGridSpec(
            num_scalar_prefetch=1,
            grid=(bd // tile,),
            in_specs=[spec4, cache, cache, new, new, const3(bias), const3(bnew)],
            out_specs=[spec4, cache, cache]),
        out_shape=[jax.ShapeDtypeStruct(q4.shape, F32),
                   jax.ShapeDtypeStruct(cache_k.shape, F32),
                   jax.ShapeDtypeStruct(cache_v.shape, F32)],
        compiler_params=_cparams(("parallel",)),
        name="attn_sample",
    )(sinks.astype(F32), q4, cache_k, cache_v, k_new, v_new, bias, bnew)
    return o.reshape(bd, ATTN_WIDTH), nk, nv


def _pool_project(d_groups, wp_ref, ps_ref):
    out = [_bdot(d, wp_ref[g]) for g, d in enumerate(d_groups)]
    return (jnp.concatenate(out, axis=-1) * ps_ref[...]).astype(BF16)


def _pool_prompt_kernel(u_ref, halo_ref, wp_ref, ps_ref, o_ref, ext, lv):
    t = pl.program_id(1)
    tile = u_ref.shape[0]
    lead, hb = SUBLANES, 2 * SUBLANES
    halo = halo_ref[...]
    ext[0:lead, :] = jnp.zeros((lead, ext.shape[1]), F32)
    ext[lead:lead + hb, :] = jnp.where(t == 0, jnp.zeros_like(halo), halo)
    ext[lead + hb:, :] = u_ref[...]
    lv[:, 0:lead, :] = jnp.zeros((lv.shape[0], lead, lv.shape[2]), F32)
    pos = t * tile + lax.broadcasted_iota(jnp.int32, (tile, 1), 0)
    n = hb + tile
    ds = []
    for g, w in enumerate(POOL_WINDOWS):
        sl = slice(g * POOL_GROUP, (g + 1) * POOL_GROUP)
        acc = ext[lead:lead + n, sl] + ext[lead - 1:lead - 1 + n, sl]
        span, level = 2, 0
        while span < w:
            lv[level, lead:lead + n, :] = acc
            acc = acc + lv[level, lead - span:lead - span + n, :]
            span, level = 2 * span, level + 1
        cnt = jnp.minimum(pos + 1, w).astype(F32)
        ds.append(acc[hb:] / cnt - ext[lead + hb:lead + hb + tile, sl])
    o_ref[...] = _pool_project(ds, wp_ref, ps_ref)


def _pool_prompt(u, w_pool, pool_scale, *, tile=1024):
    b, s, c = u.shape
    hb = 2 * SUBLANES
    return pl.pallas_call(
        _pool_prompt_kernel,
        grid=(b, s // tile),
        in_specs=[pl.BlockSpec((None, tile, c), lambda i, t: (i, t, 0)),
                  pl.BlockSpec((None, hb, c), lambda i, t: (i, jnp.maximum(t * (tile // hb) - 1, 0), 0)),
                  pl.BlockSpec(w_pool.shape, lambda i, t: (0, 0, 0)),
                  pl.BlockSpec((1, c), lambda i, t: (0, 0))],
        out_specs=pl.BlockSpec((None, tile, c), lambda i, t: (i, t, 0)),
        out_shape=jax.ShapeDtypeStruct((b, s, c), BF16),
        scratch_shapes=[pltpu.VMEM((SUBLANES + hb + tile, c), F32),
                        pltpu.VMEM((len(POOL_WINDOWS) - 1, SUBLANES + hb + tile, POOL_GROUP), F32)],
        compiler_params=_cparams(("parallel", "parallel")),
        name="pool_prompt",
    )(u, u, w_pool, pool_scale.reshape(1, c))


def _pool_sample_kernel(st_ref, u_ref, wp_ref, ps_ref, o_ref, ns_ref):
    u = u_ref[...]
    ns_ref[0:POOL_BUF - 1] = st_ref[1:POOL_BUF]
    ns_ref[POOL_BUF - 1] = u
    ds = []
    for g, w in enumerate(POOL_WINDOWS):
        sl = slice(g * POOL_GROUP, (g + 1) * POOL_GROUP)
        acc = u[:, sl]
        for j in range(1, w):
            acc = acc + st_ref[POOL_BUF - j][:, sl]
        cnt = float(min(PAST_LEN + 1, w))
        ds.append(acc / cnt - u[:, sl])
    o_ref[...] = _pool_project(ds, wp_ref, ps_ref)


def _pool_sample(state_t, u, w_pool, pool_scale):
    nb, bd, c = state_t.shape
    full = lambda a: pl.BlockSpec(a.shape, lambda: (0,) * a.ndim)
    ps = pool_scale.reshape(1, c)
    return pl.pallas_call(
        _pool_sample_kernel,
        in_specs=[full(state_t), full(u), full(w_pool), full(ps)],
        out_specs=[pl.BlockSpec((bd, c), lambda: (0, 0)), full(state_t)],
        out_shape=[jax.ShapeDtypeStruct((bd, c), BF16), jax.ShapeDtypeStruct(state_t.shape, F32)],
        compiler_params=pltpu.CompilerParams(vmem_limit_bytes=VMEM_LIMIT),
        name="pool_sample",
    )(state_t, u, w_pool, ps)


def _route_kernel(x_ref, attn_ref, pool_ref, gm_ref, sh_ref, sc_ref, wo_ref, nf_ref, wr_ref, br_ref,
                  tri_t_ref, tri_e_ref,
                  x2_ref, hs_ref, slot_ref, cnt_ref):
    sub = slot_ref.shape[0]
    n = x_ref.shape[0]
    tile = n // sub
    mixed = jnp.concatenate([attn_ref[...], pool_ref[...]], axis=1)
    mix = jnp.dot(mixed, wo_ref[...], preferred_element_type=F32)
    x2 = x_ref[...] + gm_ref[...] * mix
    x2_ref[...] = x2
    h = _rms(x2, nf_ref[...]) * (1.0 + sc_ref[...]) + sh_ref[...]

    hh, hl = _split(h)
    wh, wl = _split(wr_ref[...])
    nt = functools.partial(lax.dot_general, dimension_numbers=(((1,), (1,)), ((), ())),
                           preferred_element_type=F32)
    logits = nt(wh, hh) + nt(wl, hh) + nt(wh, hl) + br_ref[...]

    eidx = lax.broadcasted_iota(jnp.int32, (N_EXPERTS, n), 0).astype(F32)
    work = logits
    tops, picks = [], []
    for _ in range(TOP_K):
        m = jnp.max(work, axis=0, keepdims=True)
        pick = jnp.min(jnp.where(work == m, eidx, float(N_EXPERTS)), axis=0, keepdims=True)
        work = jnp.where(eidx == pick, -jnp.inf, work)
        tops.append(m)
        picks.append(pick)
    ex = [jnp.exp(v - tops[0]) for v in tops]
    den = ex[0] + ex[1] + ex[2] + ex[3]
    gates = [e / den for e in ex]

    sel = jnp.zeros((N_EXPERTS, n), F32)
    for pick in picks:
        sel = sel + (eidx == pick).astype(F32)
    selb = sel.astype(BF16)
    rank = jnp.concatenate([jnp.dot(selb[:, t * tile:(t + 1) * tile], tri_t_ref[...], preferred_element_type=F32)
                            for t in range(sub)], axis=1)
    cnts = [jnp.sum(sel[:, t * tile:(t + 1) * tile], axis=1, keepdims=True) for t in range(sub)]
    padded = jnp.concatenate(
        [jnp.broadcast_to(jnp.ceil(c * (1.0 / CHUNK)) * CHUNK, (N_EXPERTS, LANES)) for c in cnts], axis=1)
    seg = jnp.dot(tri_e_ref[...], padded.astype(BF16), preferred_element_type=F32)
    dest = jnp.concatenate([seg[:, t * LANES:t * LANES + 1] + rank[:, t * tile:(t + 1) * tile]
                            for t in range(sub)], axis=1)
    slots = [jnp.sum(jnp.where(eidx == pick, dest, 0.0), axis=0, keepdims=True) for pick in picks]

    ridx = lax.broadcasted_iota(jnp.int32, (TILE_ROWS, tile), 0).astype(F32)
    for t in range(sub):
        cols = slice(t * tile, (t + 1) * tile)
        cnt_ref[t] = jnp.broadcast_to(cnts[t], (N_EXPERTS, LANES))
        slot_ref[t] = jnp.concatenate([v[:, cols] for v in slots + gates], axis=0)
        hit = ridx == slots[0][:, cols]
        for s in slots[1:]:
            hit = jnp.logical_or(hit, ridx == s[:, cols])
        perm = jnp.where(hit, 1.0, 0.0).astype(BF16)
        hs_ref[pl.ds(t * TILE_W, TILE_W), :] = _pack_rows(
            jnp.dot(perm, hh[t * tile:(t + 1) * tile, :], preferred_element_type=F32))


def _route(x2d, attn, pool, mod, w_out, norm_ffn, w_router_t, b_router, hs_prev, *, tile, sub,
           rows_per_mod, tile0, extra_tiles):
    n = x2d.shape[0]
    nt = n // tile
    assert nt % sub == 0 and extra_tiles % sub == 0
    own_steps = nt // sub
    steps = own_steps + extra_tiles // sub
    mrows = mod.shape[2]
    assert mrows == 1 or (sub == 1 and mrows == tile)
    last = lambda i: jnp.minimum(i, own_steps - 1)
    group = lambda i: (last(i) * sub * tile) // rows_per_mod
    const = lambda shape: pl.BlockSpec(shape, lambda i: (0,) * len(shape))
    row = lambda w: pl.BlockSpec((sub * tile, w), lambda i: (last(i), 0))
    tri_t = (jnp.arange(tile)[:, None] < jnp.arange(tile)[None, :]).astype(BF16)
    tri_e = (jnp.arange(N_EXPERTS)[None, :] < jnp.arange(N_EXPERTS)[:, None]).astype(BF16)
    in_specs = [row(D_MODEL), row(ATTN_WIDTH), row(POOL_WIDTH),
                _mod_spec(mod, 2, group), _mod_spec(mod, 3, group), _mod_spec(mod, 4, group),
                const((D_MODEL, D_MODEL)), const((1, D_MODEL)), const((N_EXPERTS, D_MODEL)),
                const((N_EXPERTS, 1)), const((tile, tile)), const((N_EXPERTS, N_EXPERTS))]
    args = [x2d, attn, pool, mod, mod, mod, w_out, norm_ffn.reshape(1, D_MODEL), w_router_t,
            b_router.reshape(N_EXPERTS, 1), tri_t, tri_e]
    n_in = len(args)
    kern = _route_kernel
    aliases = {}
    hs_rows = (tile0 + steps * sub) * TILE_W
    assert tile0 % sub == 0
    if hs_prev is not None:
        in_specs.append(pl.BlockSpec(memory_space=pl.ANY))
        args.append(hs_prev)
        aliases = {n_in: 1}
        kern = lambda *refs: _route_kernel(*refs[:n_in], *refs[n_in + 1:])
        hs_rows = hs_prev.shape[0]
    return pl.pallas_call(
        kern,
        grid=(steps,),
        in_specs=in_specs,
        out_specs=[row(D_MODEL),
                   pl.BlockSpec((sub * TILE_W, D_MODEL), lambda i: (i + tile0 // sub, 0)),
                   pl.BlockSpec((sub, 2 * TOP_K, tile), lambda i: (last(i), 0, 0)),
                   pl.BlockSpec((sub, N_EXPERTS, LANES), lambda i: (last(i), 0, 0))],
        out_shape=[jax.ShapeDtypeStruct((n, D_MODEL), F32),
                   jax.ShapeDtypeStruct((hs_rows, D_MODEL), jnp.int32),
                   jax.ShapeDtypeStruct((nt, 2 * TOP_K, tile), F32),
                   jax.ShapeDtypeStruct((nt, N_EXPERTS, LANES), F32)],
        input_output_aliases=aliases,
        compiler_params=_cparams(("arbitrary",)),
        name="route",
    )(*args)


def _moe_kernel(src_ref, be_ref, nxt_ref, nb_ref, hs_hbm, w1_hbm, b1_ref, w2_hbm, b2_ref, out_hbm,
                lhs, obuf, w1s, w2s, w1c, w2c, sem_in, sem_out, sem_w, *, scratch_chunk):
    nb = nb_ref[0]

    def weight_copies(e):
        return (pltpu.make_async_copy(w1_hbm.at[e], w1s, sem_w.at[0]),
                pltpu.make_async_copy(w2_hbm.at[e], w2s, sem_w.at[1]))

    def chunk_rows(c):
        return pl.ds(pl.multiple_of(c * CHUNK_W, CHUNK_W), CHUNK_W)

    def start_in(blk, s):
        for j in range(BLOCK_CHUNKS):
            c = src_ref[blk * BLOCK_CHUNKS + j]
            c = jnp.where(c < 0, ZERO_CHUNK, c)
            pltpu.make_async_copy(hs_hbm.at[chunk_rows(c)], lhs.at[s, pl.ds(j * CHUNK_W, CHUNK_W)],
                                  sem_in.at[s]).start()

    def wait_in(s):
        pltpu.make_async_copy(hs_hbm.at[pl.ds(0, BLOCK_W)], lhs.at[s], sem_in.at[s]).wait()

    def start_out(blk, s):
        for j in range(BLOCK_CHUNKS):
            c = src_ref[blk * BLOCK_CHUNKS + j]
            c = jnp.where(c < 0, scratch_chunk + s * BLOCK_CHUNKS + j, c)
            pltpu.make_async_copy(obuf.at[s, pl.ds(j * CHUNK_W, CHUNK_W)], out_hbm.at[chunk_rows(c)],
                                  sem_out.at[s]).start()

    def wait_out(s):
        pltpu.make_async_copy(obuf.at[s], out_hbm.at[pl.ds(0, BLOCK_W)], sem_out.at[s]).wait()

    @pl.when(nb > 0)
    def _():
        start_in(0, 0)
        for cp in weight_copies(be_ref[0]):
            cp.start(priority=WEIGHT_DMA_PRIORITY)

    def block(b, carry):
        slot = b % 2
        e = be_ref[b]

        @pl.when(jnp.logical_or(b == 0, e != be_ref[jnp.maximum(b - 1, 0)]))
        def _():
            for cp in weight_copies(e):
                cp.wait()
            w1c[...] = w1s[...].astype(BF16)
            w2c[...] = w2s[...].astype(BF16)
            nxt = nxt_ref[b]

            @pl.when(nxt >= 0)
            def _():
                for cp in weight_copies(nxt):
                    cp.start(priority=WEIGHT_DMA_PRIORITY)

        wait_in(slot)

        @pl.when(b >= 2)
        def _():
            wait_out(slot)

        start_in(b + 1, 1 - slot)

        def ffn(nrows):
            x = _unpack_rows(lhs[slot, 0:nrows // ROWS_PER_WORD, :])
            gu = jnp.dot(x, w1c[...], preferred_element_type=F32) + b1_ref[e]
            gate = jnp.minimum(gu[:, :D_FF], SWIGLU_LIMIT)
            up = jnp.clip(gu[:, D_FF:], -SWIGLU_LIMIT, SWIGLU_LIMIT)
            act = (up + 1.0) * (gate * jax.nn.sigmoid(SWIGLU_ALPHA * gate))
            y = jnp.dot(act.astype(BF16), w2c[...], preferred_element_type=F32) + b2_ref[e]
            obuf[slot, 0:nrows // ROWS_PER_WORD, :] = _pack_rows(y)

        real = nb_ref[1 + b]

        @pl.when(real > BLOCK_CHUNKS // 2)
        def _():
            ffn(MOE_BLOCK)

        @pl.when(real <= BLOCK_CHUNKS // 2)
        def _():
            ffn(MOE_BLOCK // 2)

        start_out(b, slot)
        return carry

    obuf[...] = jnp.zeros(obuf.shape, jnp.int32)
    lax.fori_loop(0, nb, block, 0)

    @pl.when(nb > 0)
    def _():
        last_slot = (nb - 1) % 2
        wait_in(1 - last_slot)

        @pl.when(nb >= 2)
        def _():
            wait_out(1 - last_slot)
        wait_out(last_slot)


def _moe(hs, src, block_e, next_e, nblocks, w1, b1, w2, b2, scratch_chunk):
    full = lambda shape: pl.BlockSpec(shape, lambda i, *_: (0,) * len(shape))
    hbm = pl.BlockSpec(memory_space=pl.ANY)
    return pl.pallas_call(
        functools.partial(_moe_kernel, scratch_chunk=scratch_chunk),
        grid_spec=pltpu.PrefetchScalarGridSpec(
            num_scalar_prefetch=4,
            grid=(1,),
            in_specs=[hbm, hbm, full((N_EXPERTS, 1, 2 * D_FF)), hbm, full((N_EXPERTS, 1, D_MODEL))],
            out_specs=hbm,
            scratch_shapes=[pltpu.VMEM((2, BLOCK_W, D_MODEL), jnp.int32),
                            pltpu.VMEM((2, BLOCK_W, D_MODEL), jnp.int32),
                            pltpu.VMEM((D_MODEL, 2 * D_FF), F32),
                            pltpu.VMEM((D_FF, D_MODEL), F32),
                            pltpu.VMEM((D_MODEL, 2 * D_FF), BF16),
                            pltpu.VMEM((D_FF, D_MODEL), BF16),
                            pltpu.SemaphoreType.DMA((2,)),
                            pltpu.SemaphoreType.DMA((2,)),
                            pltpu.SemaphoreType.DMA((2,))]),
        out_shape=jax.ShapeDtypeStruct(hs.shape, jnp.int32),
        input_output_aliases={4: 0},
        compiler_params=_cparams(("arbitrary",)),
        name="moe_experts",
    )(src, block_e, next_e, nblocks, hs, w1, b1.reshape(N_EXPERTS, 1, 2 * D_FF), w2,
      b2.reshape(N_EXPERTS, 1, D_MODEL))


def _plan(cnt, n_blocks_max):
    nt = cnt.shape[0]
    nch = (cnt + (CHUNK - 1)) // CHUNK
    lstart = jnp.cumsum(nch, axis=1) - nch
    ne = jnp.sum(nch, axis=0)
    nbe = (ne + (BLOCK_CHUNKS - 1)) // BLOCK_CHUNKS
    bend = jnp.cumsum(nbe)
    nblocks = bend[-1]
    gstart = (bend - nbe)[None, :] * BLOCK_CHUNKS + (jnp.cumsum(nch, axis=0) - nch)
    s0 = jnp.arange(nt, dtype=jnp.int32)[:, None] * TILE_CHUNKS + lstart
    blk = jnp.arange(n_blocks_max + 1, dtype=jnp.int32)
    be = jnp.sum((blk[:, None] >= bend[None, :]).astype(jnp.int32), axis=1)
    be_last = jnp.sum((nblocks - 1 >= bend).astype(jnp.int32))
    be = jnp.minimum(be, be_last).astype(jnp.int32)
    eid = jnp.arange(N_EXPERTS, dtype=jnp.int32)
    strips = jnp.stack([gstart.T, nch.T, s0.T])
    mine = (be[:, None] == eid[None, :])[None, :, :, None]
    gs_b, nc_b, s0_b = jnp.sum(jnp.where(mine, strips[:, None], 0), axis=2)
    c = (blk[:, None] * BLOCK_CHUNKS + jnp.arange(BLOCK_CHUNKS, dtype=jnp.int32)[None, :])[:, :, None]
    inside = jnp.logical_and(c >= gs_b[:, None, :], c < (gs_b + nc_b)[:, None, :])
    src = (jnp.sum(jnp.where(inside, (s0_b - gs_b)[:, None, :] + c + 1, 0), axis=2) - 1).reshape(-1)
    be = be[:n_blocks_max]
    later = jnp.logical_and(eid[None, :] > be[:, None], (nbe > 0)[None, :])
    nxt = jnp.min(jnp.where(later, eid[None, :], N_EXPERTS), axis=1)
    nxt = jnp.where(nxt == N_EXPERTS, -1, nxt).astype(jnp.int32)
    real = jnp.sum((src.reshape(-1, BLOCK_CHUNKS) >= 0).astype(jnp.int32), axis=1)
    return src, be, nxt, jnp.concatenate([nblocks.reshape(1), real]).astype(jnp.int32)


def _combine_kernel(o_ref, slot_ref, x2_ref, gf_ref, y_ref):
    sub = slot_ref.shape[0]
    tile = x2_ref.shape[0] // sub
    ridx = lax.broadcasted_iota(jnp.int32, (TILE_ROWS, tile), 0).astype(F32)
    tn = functools.partial(lax.dot_general, dimension_numbers=(((0,), (0,)), ((), ())),
                           preferred_element_type=F32)
    for t in range(sub):
        gmat = jnp.zeros((TILE_ROWS, tile), F32)
        for k in range(TOP_K):
            gmat = jnp.where(ridx == slot_ref[t, k:k + 1, :], slot_ref[t, TOP_K + k:TOP_K + k + 1, :], gmat)
        gb = gmat.astype(BF16)
        y = tn(gb, _unpack_rows(o_ref[pl.ds(t * TILE_W, TILE_W), :]))
        rows = pl.ds(t * tile, tile)
        gf = gf_ref[...] if gf_ref.shape[0] == 1 else gf_ref[rows, :]
        y_ref[rows, :] = x2_ref[rows, :] + gf * y


def _combine(outs, slots, x2, mod, *, tile, sub, rows_per_mod, tile0):
    n = x2.shape[0]
    mrows = mod.shape[2]
    assert n % (sub * tile) == 0 and tile0 % sub == 0 and (mrows == 1 or sub == 1)
    return pl.pallas_call(
        _combine_kernel,
        grid=(n // (sub * tile),),
        in_specs=[pl.BlockSpec((sub * TILE_W, D_MODEL), lambda i: (i + tile0 // sub, 0)),
                  pl.BlockSpec((sub, 2 * TOP_K, tile), lambda i: (i, 0, 0)),
                  pl.BlockSpec((sub * tile, D_MODEL), lambda i: (i, 0)),
                  _mod_spec(mod, 5, lambda i: (i * sub * tile) // rows_per_mod)],
        out_specs=pl.BlockSpec((sub * tile, D_MODEL), lambda i: (i, 0)),
        out_shape=jax.ShapeDtypeStruct((n, D_MODEL), F32),
        compiler_params=_cparams(("parallel",)),
        name="combine",
    )(outs, slots, x2, mod)


def kernel(x_prompt, x_sample, cache_k, cache_v, state_pool, c_prompt, c_sample, rel_bias, norm_mix, w_ada,
           b_ada, w_in, q_norm, k_norm, sinks, w_pool, pool_scale, w_out, norm_ffn, w_router, b_router,
           w1, b1, w2, b2):
    depth = w_in.shape[0]
    assert depth == 1
    l = 0
    bp, sp, _ = x_prompt.shape
    bs = x_sample.shape[0]
    assert x_sample.shape[1] == 1 and sp % SORT_TILE == 0 and bs <= SORT_TILE
    n_p = bp * sp
    tiles_p = n_p // SORT_TILE
    max_chunks = tiles_p * (SORT_TILE * TOP_K // CHUNK + N_EXPERTS) + (bs * TOP_K // CHUNK + N_EXPERTS)
    n_blocks_max = -(-max_chunks // BLOCK_CHUNKS) + N_EXPERTS

    pr = -(-bp // SUBLANES) * SUBLANES
    c_all = jnp.concatenate([c_prompt, jnp.zeros((pr - bp, D_MODEL), F32), c_sample], axis=0)
    mod = _modulation(c_all, w_ada[l], b_ada[l])
    mod_p = mod[:, :bp].reshape(6, bp, 1, D_MODEL)
    mod_s = mod[:, pr:].reshape(6, 1, bs, D_MODEL)

    head = jnp.arange(ATTN_WIDTH) // HEAD_DIM
    bd = (head[:, None] == head[None, :]).astype(BF16)
    w_in_b = w_in[l].astype(BF16)
    w_out_b = w_out[l].astype(BF16)
    w_pool_b = w_pool[l].astype(BF16)
    w_router_t = w_router[l].T

    xp = x_prompt.reshape(n_p, D_MODEL)
    q, k, v, u = _mixer_inputs(xp, mod_p, norm_mix[l], w_in_b, bd, q_norm[l], k_norm[l],
                               tile=1024, rows_per_mod=sp, precise=False)
    k3 = k.reshape(bp, sp, KV_WIDTH)
    v3 = v.reshape(bp, sp, KV_WIDTH)
    u3 = u.reshape(bp, sp, POOL_WIDTH)
    attn = _attn_prompt(q.reshape(bp, sp, ATTN_WIDTH), k3, v3, sinks[l], rel_bias)
    pool = _pool_prompt(u3, w_pool_b, pool_scale[l])
    keep = min(WINDOW, sp)
    nkp = k3[:, -keep:].reshape(bp, keep, N_KV_HEADS, HEAD_DIM)
    nvp = v3[:, -keep:].reshape(bp, keep, N_KV_HEADS, HEAD_DIM)
    if sp >= POOL_BUF:
        npp = u3[:, -POOL_BUF:]
    else:
        npp = jnp.concatenate([jnp.zeros((bp, POOL_BUF - sp, POOL_WIDTH), F32), u3], axis=1)
    x2_p, hs, slots_p, cnt_p = _route(
        xp, attn.reshape(n_p, ATTN_WIDTH), pool.reshape(n_p, POOL_WIDTH), mod_p,
        w_out_b, norm_ffn[l], w_router_t, b_router[l], None,
        tile=SORT_TILE, sub=ROUTE_SUB, rows_per_mod=sp, tile0=0, extra_tiles=max(2, ROUTE_SUB))

    xs = x_sample.reshape(bs, D_MODEL)
    qs, ks, vs, us = _mixer_inputs(xs, mod_s, norm_mix[l], w_in[l], bd, q_norm[l], k_norm[l],
                                   tile=bs, rows_per_mod=bs, precise=True)
    wbuf = cache_k.shape[2]
    attn_s, nks, nvs = _attn_sample(qs, ks, vs, cache_k[l].reshape(bs, wbuf, KV_WIDTH),
                                    cache_v[l].reshape(bs, wbuf, KV_WIDTH), sinks[l], rel_bias)
    pool_s, nps_t = _pool_sample(jnp.swapaxes(state_pool[l], 0, 1), us, w_pool_b, pool_scale[l])
    x2_s, hs, slots_s, cnt_s = _route(
        xs, attn_s.astype(BF16), pool_s, mod_s, w_out_b, norm_ffn[l],
        w_router_t, b_router[l], hs, tile=bs, sub=1, rows_per_mod=bs, tile0=tiles_p, extra_tiles=0)

    cnt = jnp.concatenate([cnt_p[:, :, 0], cnt_s[:, :, 0]], axis=0).astype(jnp.int32)
    src, block_e, next_e, nblocks = _plan(cnt, n_blocks_max)
    outs = _moe(hs, src, block_e, next_e, nblocks, w1[l], b1[l], w2[l], b2[l],
                scratch_chunk=(tiles_p + 1) * TILE_CHUNKS)

    y_p = _combine(outs, slots_p, x2_p, mod_p, tile=SORT_TILE, sub=COMBINE_SUB, rows_per_mod=sp, tile0=0)
    y_s = _combine(outs, slots_s, x2_s, mod_s, tile=bs, sub=1, rows_per_mod=bs, tile0=tiles_p)

    return (y_p.reshape(bp, sp, D_MODEL), y_s.reshape(bs, 1, D_MODEL),
            nkp[None], nvp[None], npp[None],
            nks.reshape(1, bs, wbuf, N_KV_HEADS, HEAD_DIM), nvs.reshape(1, bs, wbuf, N_KV_HEADS, HEAD_DIM),
            jnp.swapaxes(nps_t, 0, 1)[None])
```

```python
import functools
import math

import jax
import jax.numpy as jnp
from jax import lax
from jax.experimental import pallas as pl
from jax.experimental.pallas import tpu as pltpu

F32 = jnp.float32
BF16 = jnp.bfloat16

D_MODEL = 1024
HEAD_DIM = 64
N_HEADS = 8
N_KV_HEADS = 2
GROUP = N_HEADS // N_KV_HEADS
ATTN_WIDTH = N_HEADS * HEAD_DIM
KV_WIDTH = N_KV_HEADS * HEAD_DIM
POOL_WIDTH = D_MODEL - ATTN_WIDTH
POOL_WINDOWS = (2, 4, 8, 16)
POOL_GROUP = POOL_WIDTH // len(POOL_WINDOWS)
POOL_BUF = max(POOL_WINDOWS) - 1
IN_WIDTH = ATTN_WIDTH + 2 * KV_WIDTH + POOL_WIDTH
WINDOW = 128
ATTN_BLOCK = 128
N_BUCKETS = 32
MAX_EXACT = 16
REL_MAX_DIST = 128
N_EXPERTS = 32
TOP_K = 4
D_FF = D_MODEL
SWIGLU_LIMIT = 7.0
SWIGLU_ALPHA = 1.702
EPS = 1e-6
NEG_INF = -1e30
PAST_LEN = 16384

LANES = 128
SUBLANES = 8
VMEM_LIMIT = 56 * 1024 * 1024

ATTN_QB = 8

SORT_TILE = 256
ROUTE_SUB = 4
COMBINE_SUB = 4
CHUNK = SUBLANES
TILE_ROWS = -(-(SORT_TILE * TOP_K + N_EXPERTS * (CHUNK - 1)) // LANES) * LANES
TILE_CHUNKS = TILE_ROWS // CHUNK
MOE_BLOCK = 256
BLOCK_CHUNKS = MOE_BLOCK // CHUNK
WEIGHT_DMA_PRIORITY = 1
PACKED_W = D_MODEL // 2
ROW_W = PACKED_W
ZERO_CHUNK = TILE_CHUNKS - 1


def _bdot(a, b):
    return jnp.dot(a.astype(BF16), b.astype(BF16), preferred_element_type=F32)


def _split(a):
    hi = a.astype(BF16)
    lo = (a - hi.astype(F32)).astype(BF16)
    return hi, lo


def _dot3(a, b):
    ah, al = _split(a)
    bh, bl = _split(b)
    d = functools.partial(jnp.dot, preferred_element_type=F32)
    return d(ah, bh) + d(al, bh) + d(ah, bl)


def _pack_rows(x):
    bits = lax.bitcast_convert_type(x, jnp.int32)
    return bits[:, :PACKED_W] | lax.shift_right_logical(bits[:, PACKED_W:], 16)


def _unpack_rows(w):
    hi = lax.bitcast_convert_type(w & jnp.int32(-65536), F32)
    lo = lax.bitcast_convert_type(lax.shift_left(w, 16), F32)
    return hi.astype(BF16), lo.astype(BF16)


def _rms(x, g):
    return x * lax.rsqrt(jnp.mean(x * x, axis=-1, keepdims=True) + EPS) * g


def _cparams(sem, **kw):
    return pltpu.CompilerParams(dimension_semantics=sem, vmem_limit_bytes=VMEM_LIMIT, **kw)


def _ada_kernel(c_ref, w_ref, b_ref, o_ref):
    c = c_ref[...]
    s = c * jax.nn.sigmoid(c)
    o_ref[...] = _dot3(s, w_ref[...]) + b_ref[...]


def _modulation(c, w_ada, b_ada):
    rows = c.shape[0]
    n = w_ada.shape[1]
    tn = D_MODEL
    return pl.pallas_call(
        _ada_kernel,
        grid=(n // tn,),
        in_specs=[pl.BlockSpec((rows, D_MODEL), lambda j: (0, 0)),
                  pl.BlockSpec((D_MODEL, tn), lambda j: (0, j)),
                  pl.BlockSpec((1, tn), lambda j: (0, j))],
        out_specs=pl.BlockSpec((None, rows, tn), lambda j: (j, 0, 0)),
        out_shape=jax.ShapeDtypeStruct((n // tn, rows, tn), F32),
        compiler_params=_cparams(("parallel",)),
        name="modulation",
    )(c, w_ada, b_ada.reshape(1, n))


def _head_rms(t, bd, g, precise):
    if precise:
        hi, lo = _split(t * t)
        ss = jnp.dot(hi, bd, preferred_element_type=F32) + jnp.dot(lo, bd, preferred_element_type=F32)
    else:
        ss = _bdot(t * t, bd)
    return t * lax.rsqrt(ss * (1.0 / HEAD_DIM) + EPS) * g


def _mixin_kernel(x_ref, sh_ref, sc_ref, g_ref, w_ref, bd_ref, qn_ref, kn_ref,
                  q_ref, k_ref, v_ref, u_ref, *, precise):
    h = _rms(x_ref[...], g_ref[...]) * (1.0 + sc_ref[...]) + sh_ref[...]
    z = _dot3(h, w_ref[...]) if precise else _bdot(h, w_ref[...])
    q = z[:, :ATTN_WIDTH]
    k = z[:, ATTN_WIDTH:ATTN_WIDTH + KV_WIDTH]
    bd = bd_ref[...]
    q = _head_rms(q, bd, qn_ref[...], precise)
    k = _head_rms(k, bd[:KV_WIDTH, :KV_WIDTH], kn_ref[...], precise)
    q_ref[...] = (q * (HEAD_DIM ** -0.5)).astype(BF16)
    k_ref[...] = k
    v_ref[...] = z[:, ATTN_WIDTH + KV_WIDTH:ATTN_WIDTH + 2 * KV_WIDTH]
    u_ref[...] = z[:, ATTN_WIDTH + 2 * KV_WIDTH:]


def _mod_spec(mod, k, group_of_step):
    return pl.BlockSpec((None, None, mod.shape[2], D_MODEL), lambda i: (k, group_of_step(i), 0, 0))


def _mixer_inputs(x2d, mod, norm_mix, w_in, bd, q_norm, k_norm, *, tile, rows_per_mod, precise):
    n = x2d.shape[0]
    group = lambda i: (i * tile) // rows_per_mod
    const = lambda shape: pl.BlockSpec(shape, lambda i: (0,) * len(shape))
    row = lambda w: pl.BlockSpec((tile, w), lambda i: (i, 0))
    return pl.pallas_call(
        functools.partial(_mixin_kernel, precise=precise),
        grid=(n // tile,),
        in_specs=[row(D_MODEL), _mod_spec(mod, 0, group), _mod_spec(mod, 1, group), const((1, D_MODEL)),
                  const((D_MODEL, IN_WIDTH)), const((ATTN_WIDTH, ATTN_WIDTH)), const((1, ATTN_WIDTH)),
                  const((1, KV_WIDTH))],
        out_specs=[row(ATTN_WIDTH), row(KV_WIDTH), row(KV_WIDTH), row(POOL_WIDTH)],
        out_shape=[jax.ShapeDtypeStruct((n, ATTN_WIDTH), BF16),
                   jax.ShapeDtypeStruct((n, KV_WIDTH), F32),
                   jax.ShapeDtypeStruct((n, KV_WIDTH), F32),
                   jax.ShapeDtypeStruct((n, POOL_WIDTH), F32)],
        compiler_params=_cparams(("parallel",)),
        name="mixer_inputs",
    )(x2d, mod, mod, norm_mix.reshape(1, D_MODEL), w_in, bd,
      jnp.tile(q_norm, N_HEADS).reshape(1, ATTN_WIDTH), jnp.tile(k_norm, N_KV_HEADS).reshape(1, KV_WIDTH))


def _t5_bucket(rel):
    n = jnp.maximum(rel, 0)
    nf = jnp.maximum(n, 1).astype(F32)
    large = MAX_EXACT + (jnp.log(nf / MAX_EXACT) / math.log(REL_MAX_DIST / MAX_EXACT)
                         * (N_BUCKETS - MAX_EXACT)).astype(jnp.int32)
    large = jnp.minimum(large, N_BUCKETS - 1)
    return jnp.where(n < MAX_EXACT, n, large)


def _bias_table(rel, rel_table):
    bucket = _t5_bucket(rel)
    table = rel_table.astype(F32)
    ids = jnp.arange(N_BUCKETS, dtype=bucket.dtype).reshape((N_BUCKETS, 1) + (1,) * rel.ndim)
    onehot = bucket[None, None] == ids
    bias = jnp.sum(jnp.where(onehot, table.reshape(table.shape + (1,) * rel.ndim), 0.0), axis=0)
    valid = (rel >= 0) & (rel < WINDOW)
    return jnp.where(valid[None], bias, NEG_INF)


def _attn_prompt_kernel(sink_ref, q_ref, kp_ref, kc_ref, vp_ref, vc_ref, bias_ref, o_ref):
    first = pl.program_id(1) == 0
    kk = jnp.concatenate([kp_ref[...], kc_ref[...]], axis=0).astype(BF16)
    vv = jnp.concatenate([vp_ref[...], vc_ref[...]], axis=0).astype(BF16)
    key = lax.broadcasted_iota(jnp.int32, (2 * ATTN_BLOCK, 1), 0)
    no_prev = jnp.logical_and(first, key < ATTN_BLOCK)
    lane = lax.broadcasted_iota(jnp.int32, (1, N_HEADS * ATTN_BLOCK), 1)
    sink = jnp.zeros((1, N_HEADS * ATTN_BLOCK), F32)
    for h in range(N_HEADS):
        sink = jnp.where(lane // ATTN_BLOCK == h, sink_ref[h], sink)
    contract = lambda a, b, dims: lax.dot_general(a, b, (dims, ((), ())), preferred_element_type=F32)
    part = GROUP * ATTN_BLOCK
    for i in range(ATTN_QB):
        q = q_ref[i * ATTN_BLOCK:(i + 1) * ATTN_BLOCK, :]
        keys = slice(i * ATTN_BLOCK, (i + 2) * ATTN_BLOCK)
        scores = []
        for kv in range(N_KV_HEADS):
            heads = range(kv * GROUP, (kv + 1) * GROUP)
            qg = jnp.concatenate([q[:, h * HEAD_DIM:(h + 1) * HEAD_DIM] for h in heads], axis=0)
            scores.append(contract(kk[keys, kv * HEAD_DIM:(kv + 1) * HEAD_DIM], qg, ((1,), (1,))))
        s = jnp.concatenate(scores, axis=1) + bias_ref[...]
        if i == 0:
            s = jnp.where(no_prev, NEG_INF, s)
        m = jnp.maximum(jnp.max(s, axis=0, keepdims=True), sink)
        p = jnp.exp(s - m)
        denom = jnp.sum(p, axis=0, keepdims=True) + jnp.exp(sink - m)
        p = p.astype(BF16)
        halves = [contract(vv[keys, kv * HEAD_DIM:(kv + 1) * HEAD_DIM], p[:, kv * part:(kv + 1) * part],
                           ((0,), (0,))) / denom[:, kv * part:(kv + 1) * part]
                  for kv in range(N_KV_HEADS)]
        o_t = jnp.concatenate(halves, axis=0)
        per_g = [o_t[:, g * ATTN_BLOCK:(g + 1) * ATTN_BLOCK].T for g in range(GROUP)]
        out = [t[:, kv * HEAD_DIM:(kv + 1) * HEAD_DIM] for kv in range(N_KV_HEADS) for t in per_g]
        o_ref[i * ATTN_BLOCK:(i + 1) * ATTN_BLOCK, :] = jnp.concatenate(out, axis=-1).astype(BF16)


def _attn_prompt(q, k, v, sinks, rel_table):
    b, s = q.shape[:2]
    qrows = ATTN_QB * ATTN_BLOCK
    assert s % qrows == 0
    qi = jnp.arange(ATTN_BLOCK, dtype=jnp.int32)[:, None]
    si = jnp.arange(2 * ATTN_BLOCK, dtype=jnp.int32)[None, :]
    bias = _bias_table(qi + ATTN_BLOCK - si, rel_table)
    bias = bias.reshape(N_HEADS * ATTN_BLOCK, 2 * ATTN_BLOCK).T
    cur = lambda w: pl.BlockSpec((None, qrows, w), lambda i, j, *_: (i, j, 0))
    prev = lambda w: pl.BlockSpec((None, ATTN_BLOCK, w),
                                  lambda i, j, *_: (i, jnp.maximum(j * ATTN_QB - 1, 0), 0))
    return pl.pallas_call(
        _attn_prompt_kernel,
        grid_spec=pltpu.PrefetchScalarGridSpec(
            num_scalar_prefetch=1,
            grid=(b, s // qrows),
            in_specs=[cur(ATTN_WIDTH), prev(KV_WIDTH), cur(KV_WIDTH), prev(KV_WIDTH), cur(KV_WIDTH),
                      pl.BlockSpec(bias.shape, lambda i, j, *_: (0, 0))],
            out_specs=cur(ATTN_WIDTH)),
        out_shape=jax.ShapeDtypeStruct((b, s, ATTN_WIDTH), BF16),
        compiler_params=_cparams(("parallel", "parallel")),
        name="attn_prompt",
    )(sinks.astype(F32), q, k, k, v, v, bias)


def _attn_sample_kernel(sink_ref, q_ref, kc_ref, vc_ref, kn_ref, vn_ref, bias_ref, bnew_ref,
                        o_ref, nk_ref, nv_ref):
    kc = kc_ref[...]
    vc = vc_ref[...]
    kn = kn_ref[...]
    vn = vn_ref[...]
    w = kc.shape[1]
    pos = lax.broadcasted_iota(jnp.int32, kc.shape, 1)
    nk_ref[...] = jnp.where(pos == w - 1, kn[:, None, :], pltpu.roll(kc, w - 1, 1))
    nv_ref[...] = jnp.where(pos == w - 1, vn[:, None, :], pltpu.roll(vc, w - 1, 1))
    gi = lax.broadcasted_iota(jnp.int32, (1, GROUP, 1), 1)
    for kv in range(N_KV_HEADS):
        sl = slice(kv * HEAD_DIM, (kv + 1) * HEAD_DIM)
        qg = q_ref[:, kv]
        s = jnp.einsum('bgd,bsd->bgs', qg, kc[:, :, sl].astype(BF16), preferred_element_type=F32)
        s = s + bias_ref[kv][None]
        s_new = jnp.sum(qg.astype(F32) * kn[:, None, sl], axis=-1, keepdims=True) + bnew_ref[kv][None]
        sink = jnp.zeros((1, GROUP, 1), F32)
        for g in range(GROUP):
            sink = jnp.where(gi == g, sink_ref[kv * GROUP + g], sink)
        m = jnp.maximum(jnp.maximum(jnp.max(s, axis=-1, keepdims=True), s_new), sink)
        p = jnp.exp(s - m)
        p_new = jnp.exp(s_new - m)
        denom = jnp.sum(p, axis=-1, keepdims=True) + p_new + jnp.exp(sink - m)
        o = jnp.einsum('bgs,bsd->bgd', p.astype(BF16), vc[:, :, sl].astype(BF16), preferred_element_type=F32)
        o = o + p_new * vn[:, None, sl]
        o_ref[:, kv] = o / denom


def _attn_sample(q, k_new, v_new, cache_k, cache_v, sinks, rel_table, *, tile=32):
    bd, w = cache_k.shape[:2]
    rel = w - jnp.arange(w, dtype=jnp.int32)
    bias = _bias_table(rel, rel_table).reshape(N_KV_HEADS, GROUP, w)
    bnew = _bias_table(jnp.zeros((1,), jnp.int32), rel_table).reshape(N_KV_HEADS, GROUP, 1)
    q4 = q.reshape(bd, N_KV_HEADS, GROUP, HEAD_DIM)
    spec4 = pl.BlockSpec((tile, N_KV_HEADS, GROUP, HEAD_DIM), lambda i, *_: (i, 0, 0, 0))
    cache = pl.BlockSpec((tile, w, KV_WIDTH), lambda i, *_: (i, 0, 0))
    new = pl.BlockSpec((tile, KV_WIDTH), lambda i, *_: (i, 0))
    const3 = lambda a: pl.BlockSpec(a.shape, lambda i, *_: (0, 0, 0))
    o, nk, nv = pl.pallas_call(
        _attn_sample_kernel,
        grid_spec=pltpu.PrefetchScalarGridSpec(
            num_scalar_prefetch=1,
            grid=(bd // tile,),
            in_specs=[spec4, cache, cache, new, new, const3(bias), const3(bnew)],
            out_specs=[spec4, cache, cache]),
        out_shape=[jax.ShapeDtypeStruct(q4.shape, F32),
                   jax.ShapeDtypeStruct(cache_k.shape, F32),
                   jax.ShapeDtypeStruct(cache_v.shape, F32)],
        compiler_params=_cparams(("parallel",)),
        name="attn_sample",
    )(sinks.astype(F32), q4, cache_k, cache_v, k_new, v_new, bias, bnew)
    return o.reshape(bd, ATTN_WIDTH), nk, nv


def _pool_project(d_groups, wp_ref, ps_ref):
    out = [_bdot(d, wp_ref[g]) for g, d in enumerate(d_groups)]
    return (jnp.concatenate(out, axis=-1) * ps_ref[...]).astype(BF16)


def _pool_prompt_kernel(u_ref, halo_ref, wp_ref, ps_ref, o_ref, ext, lv):
    t = pl.program_id(1)
    tile = u_ref.shape[0]
    lead, hb = SUBLANES, 2 * SUBLANES
    halo = halo_ref[...]
    ext[0:lead, :] = jnp.zeros((lead, ext.shape[1]), F32)
    ext[lead:lead + hb, :] = jnp.where(t == 0, jnp.zeros_like(halo), halo)
    ext[lead + hb:, :] = u_ref[...]
    lv[:, 0:lead, :] = jnp.zeros((lv.shape[0], lead, lv.shape[2]), F32)
    pos = t * tile + lax.broadcasted_iota(jnp.int32, (tile, 1), 0)
    n = hb + tile
    ds = []
    for g, w in enumerate(POOL_WINDOWS):
        sl = slice(g * POOL_GROUP, (g + 1) * POOL_GROUP)
        acc = ext[lead:lead + n, sl] + ext[lead - 1:lead - 1 + n, sl]
        span, level = 2, 0
        while span < w:
            lv[level, lead:lead + n, :] = acc
            acc = acc + lv[level, lead - span:lead - span + n, :]
            span, level = 2 * span, level + 1
        cnt = jnp.minimum(pos + 1, w).astype(F32)
        ds.append(acc[hb:] / cnt - ext[lead + hb:lead + hb + tile, sl])
    o_ref[...] = _pool_project(ds, wp_ref, ps_ref)


def _pool_prompt(u, w_pool, pool_scale, *, tile=1024):
    b, s, c = u.shape
    hb = 2 * SUBLANES
    return pl.pallas_call(
        _pool_prompt_kernel,
        grid=(b, s // tile),
        in_specs=[pl.BlockSpec((None, tile, c), lambda i, t: (i, t, 0)),
                  pl.BlockSpec((None, hb, c), lambda i, t: (i, jnp.maximum(t * (tile // hb) - 1, 0), 0)),
                  pl.BlockSpec(w_pool.shape, lambda i, t: (0, 0, 0)),
                  pl.BlockSpec((1, c), lambda i, t: (0, 0))],
        out_specs=pl.BlockSpec((None, tile, c), lambda i, t: (i, t, 0)),
        out_shape=jax.ShapeDtypeStruct((b, s, c), BF16),
        scratch_shapes=[pltpu.VMEM((SUBLANES + hb + tile, c), F32),
                        pltpu.VMEM((len(POOL_WINDOWS) - 1, SUBLANES + hb + tile, POOL_GROUP), F32)],
        compiler_params=_cparams(("parallel", "parallel")),
        name="pool_prompt",
    )(u, u, w_pool, pool_scale.reshape(1, c))


def _pool_sample_kernel(st_ref, u_ref, wp_ref, ps_ref, o_ref, ns_ref):
    u = u_ref[...]
    ns_ref[0:POOL_BUF - 1] = st_ref[1:POOL_BUF]
    ns_ref[POOL_BUF - 1] = u
    ds = []
    for g, w in enumerate(POOL_WINDOWS):
        sl = slice(g * POOL_GROUP, (g + 1) * POOL_GROUP)
        acc = u[:, sl]
        for j in range(1, w):
            acc = acc + st_ref[POOL_BUF - j][:, sl]
        cnt = float(min(PAST_LEN + 1, w))
        ds.append(acc / cnt - u[:, sl])
    o_ref[...] = _pool_project(ds, wp_ref, ps_ref)


def _pool_sample(state_t, u, w_pool, pool_scale):
    nb, bd, c = state_t.shape
    full = lambda a: pl.BlockSpec(a.shape, lambda: (0,) * a.ndim)
    ps = pool_scale.reshape(1, c)
    return pl.pallas_call(
        _pool_sample_kernel,
        in_specs=[full(state_t), full(u), full(w_pool), full(ps)],
        out_specs=[pl.BlockSpec((bd, c), lambda: (0, 0)), full(state_t)],
        out_shape=[jax.ShapeDtypeStruct((bd, c), BF16), jax.ShapeDtypeStruct(state_t.shape, F32)],
        compiler_params=pltpu.CompilerParams(vmem_limit_bytes=VMEM_LIMIT),
        name="pool_sample",
    )(state_t, u, w_pool, ps)


def _route_kernel(x_ref, attn_ref, pool_ref, gm_ref, sh_ref, sc_ref, wo_ref, nf_ref, wr_ref, br_ref,
                  tri_t_ref, tri_e_ref,
                  x2_ref, hs_ref, slot_ref, cnt_ref):
    sub = slot_ref.shape[0]
    n = x_ref.shape[0]
    tile = n // sub
    mixed = jnp.concatenate([attn_ref[...], pool_ref[...]], axis=1)
    mix = jnp.dot(mixed, wo_ref[...], preferred_element_type=F32)
    x2 = x_ref[...] + gm_ref[...] * mix
    x2_ref[...] = x2
    h = _rms(x2, nf_ref[...]) * (1.0 + sc_ref[...]) + sh_ref[...]

    hh, hl = _split(h)
    wh, wl = _split(wr_ref[...])
    nt = functools.partial(lax.dot_general, dimension_numbers=(((1,), (1,)), ((), ())),
                           preferred_element_type=F32)
    logits = nt(wh, hh) + nt(wl, hh) + nt(wh, hl) + br_ref[...]

    eidx = lax.broadcasted_iota(jnp.int32, (N_EXPERTS, n), 0).astype(F32)
    work = logits
    tops, picks = [], []
    for _ in range(TOP_K):
        m = jnp.max(work, axis=0, keepdims=True)
        pick = jnp.min(jnp.where(work == m, eidx, float(N_EXPERTS)), axis=0, keepdims=True)
        work = jnp.where(eidx == pick, -jnp.inf, work)
        tops.append(m)
        picks.append(pick)
    ex = [jnp.exp(v - tops[0]) for v in tops]
    den = ex[0] + ex[1] + ex[2] + ex[3]
    gates = [e / den for e in ex]

    sel = jnp.zeros((N_EXPERTS, n), F32)
    for pick in picks:
        sel = sel + (eidx == pick).astype(F32)
    selb = sel.astype(BF16)
    rank = jnp.concatenate([jnp.dot(selb[:, t * tile:(t + 1) * tile], tri_t_ref[...], preferred_element_type=F32)
                            for t in range(sub)], axis=1)
    cnts = [jnp.sum(sel[:, t * tile:(t + 1) * tile], axis=1, keepdims=True) for t in range(sub)]
    padded = jnp.concatenate(
        [jnp.broadcast_to(jnp.ceil(c * (1.0 / CHUNK)) * CHUNK, (N_EXPERTS, LANES)) for c in cnts], axis=1)
    seg = jnp.dot(tri_e_ref[...], padded.astype(BF16), preferred_element_type=F32)
    dest = jnp.concatenate([seg[:, t * LANES:t * LANES + 1] + rank[:, t * tile:(t + 1) * tile]
                            for t in range(sub)], axis=1)
    slots = [jnp.sum(jnp.where(eidx == pick, dest, 0.0), axis=0, keepdims=True) for pick in picks]

    ridx = lax.broadcasted_iota(jnp.int32, (TILE_ROWS, tile), 0).astype(F32)
    for t in range(sub):
        cols = slice(t * tile, (t + 1) * tile)
        cnt_ref[t] = jnp.broadcast_to(cnts[t], (N_EXPERTS, LANES))
        slot_ref[t] = jnp.concatenate([v[:, cols] for v in slots + gates], axis=0)
        hit = ridx == slots[0][:, cols]
        for s in slots[1:]:
            hit = jnp.logical_or(hit, ridx == s[:, cols])
        perm = jnp.where(hit, 1.0, 0.0).astype(BF16)
        hs_ref[pl.ds(t * TILE_ROWS, TILE_ROWS), :] = _pack_rows(
            jnp.dot(perm, hh[t * tile:(t + 1) * tile, :], preferred_element_type=F32))


def _route(x2d, attn, pool, mod, w_out, norm_ffn, w_router_t, b_router, hs_prev, *, tile, sub,
           rows_per_mod, tile0, extra_tiles):
    n = x2d.shape[0]
    nt = n // tile
    assert nt % sub == 0 and extra_tiles % sub == 0
    own_steps = nt // sub
    steps = own_steps + extra_tiles // sub
    mrows = mod.shape[2]
    assert mrows == 1 or (sub == 1 and mrows == tile)
    last = lambda i: jnp.minimum(i, own_steps - 1)
    group = lambda i: (last(i) * sub * tile) // rows_per_mod
    const = lambda shape: pl.BlockSpec(shape, lambda i: (0,) * len(shape))
    row = lambda w: pl.BlockSpec((sub * tile, w), lambda i: (last(i), 0))
    tri_t = (jnp.arange(tile)[:, None] < jnp.arange(tile)[None, :]).astype(BF16)
    tri_e = (jnp.arange(N_EXPERTS)[None, :] < jnp.arange(N_EXPERTS)[:, None]).astype(BF16)
    in_specs = [row(D_MODEL), row(ATTN_WIDTH), row(POOL_WIDTH),
                _mod_spec(mod, 2, group), _mod_spec(mod, 3, group), _mod_spec(mod, 4, group),
                const((D_MODEL, D_MODEL)), const((1, D_MODEL)), const((N_EXPERTS, D_MODEL)),
                const((N_EXPERTS, 1)), const((tile, tile)), const((N_EXPERTS, N_EXPERTS))]
    args = [x2d, attn, pool, mod, mod, mod, w_out, norm_ffn.reshape(1, D_MODEL), w_router_t,
            b_router.reshape(N_EXPERTS, 1), tri_t, tri_e]
    n_in = len(args)
    kern = _route_kernel
    aliases = {}
    hs_rows = (tile0 + steps * sub) * TILE_ROWS
    assert tile0 % sub == 0
    if hs_prev is not None:
        in_specs.append(pl.BlockSpec(memory_space=pl.ANY))
        args.append(hs_prev)
        aliases = {n_in: 1}
        kern = lambda *refs: _route_kernel(*refs[:n_in], *refs[n_in + 1:])
        hs_rows = hs_prev.shape[0]
    return pl.pallas_call(
        kern,
        grid=(steps,),
        in_specs=in_specs,
        out_specs=[row(D_MODEL),
                   pl.BlockSpec((sub * TILE_ROWS, ROW_W), lambda i: (i + tile0 // sub, 0)),
                   pl.BlockSpec((sub, 2 * TOP_K, tile), lambda i: (last(i), 0, 0)),
                   pl.BlockSpec((sub, N_EXPERTS, LANES), lambda i: (last(i), 0, 0))],
        out_shape=[jax.ShapeDtypeStruct((n, D_MODEL), F32),
                   jax.ShapeDtypeStruct((hs_rows, ROW_W), jnp.int32),
                   jax.ShapeDtypeStruct((nt, 2 * TOP_K, tile), F32),
                   jax.ShapeDtypeStruct((nt, N_EXPERTS, LANES), F32)],
        input_output_aliases=aliases,
        compiler_params=_cparams(("arbitrary",)),
        name="route",
    )(*args)


def _moe_kernel(src_ref, be_ref, nxt_ref, nb_ref, hs_hbm, w1_hbm, b1_ref, w2_hbm, b2_ref, out_hbm,
                lhs, obuf, w1s, w2s, w1c, w2c, sem_in, sem_out, sem_w, *, scratch_chunk):
    nb = nb_ref[0]

    def weight_copies(e):
        return (pltpu.make_async_copy(w1_hbm.at[e], w1s, sem_w.at[0]),
                pltpu.make_async_copy(w2_hbm.at[e], w2s, sem_w.at[1]))

    def chunk_rows(c):
        return pl.ds(pl.multiple_of(c * CHUNK, CHUNK), CHUNK)

    def start_in(blk, s):
        for j in range(BLOCK_CHUNKS):
            c = src_ref[blk * BLOCK_CHUNKS + j]
            c = jnp.where(c < 0, ZERO_CHUNK, c)
            pltpu.make_async_copy(hs_hbm.at[chunk_rows(c)], lhs.at[s, pl.ds(j * CHUNK, CHUNK)],
                                  sem_in.at[s]).start()

    def wait_in(s):
        pltpu.make_async_copy(hs_hbm.at[pl.ds(0, MOE_BLOCK)], lhs.at[s], sem_in.at[s]).wait()

    def start_out(blk, s):
        for j in range(BLOCK_CHUNKS):
            c = src_ref[blk * BLOCK_CHUNKS + j]
            c = jnp.where(c < 0, scratch_chunk + s * BLOCK_CHUNKS + j, c)
            pltpu.make_async_copy(obuf.at[s, pl.ds(j * CHUNK, CHUNK)],
                                  out_hbm.at[chunk_rows(c), pl.ds(0, PACKED_W)],
                                  sem_out.at[s]).start(priority=WEIGHT_DMA_PRIORITY)

    def wait_out(s):
        pltpu.make_async_copy(obuf.at[s], out_hbm.at[pl.ds(0, MOE_BLOCK), pl.ds(0, PACKED_W)],
                              sem_out.at[s]).wait()

    @pl.when(nb > 0)
    def _():
        start_in(0, 0)
        for cp in weight_copies(be_ref[0]):
            cp.start(priority=WEIGHT_DMA_PRIORITY)

    def block(b, carry):
        slot = b % 2
        e = be_ref[b]

        @pl.when(jnp.logical_or(b == 0, e != be_ref[jnp.maximum(b - 1, 0)]))
        def _():
            for cp in weight_copies(e):
                cp.wait()
            w1c[...] = w1s[...].astype(BF16)
            w2c[...] = w2s[...].astype(BF16)
            nxt = nxt_ref[b]

            @pl.when(nxt >= 0)
            def _():
                for cp in weight_copies(nxt):
                    cp.start(priority=WEIGHT_DMA_PRIORITY)

        wait_in(slot)

        @pl.when(b >= 2)
        def _():
            wait_out(slot)

        start_in(b + 1, 1 - slot)

        def ffn(nrows):
            xh, xl = _unpack_rows(lhs[slot, 0:nrows, :])
            x = jnp.concatenate([xh, xl], axis=1)
            gu = jnp.dot(x, w1c[...], preferred_element_type=F32) + b1_ref[e]
            gate = jnp.minimum(gu[:, :D_FF], SWIGLU_LIMIT)
            up = jnp.clip(gu[:, D_FF:], -SWIGLU_LIMIT, SWIGLU_LIMIT)
            act = (up + 1.0) * (gate * jax.nn.sigmoid(SWIGLU_ALPHA * gate))
            y = jnp.dot(act.astype(BF16), w2c[...], preferred_element_type=F32) + b2_ref[e]
            obuf[slot, 0:nrows, :] = _pack_rows(y.astype(BF16).astype(F32))

        real = nb_ref[1 + b]

        @pl.when(real > BLOCK_CHUNKS // 2)
        def _():
            ffn(MOE_BLOCK)

        @pl.when(real <= BLOCK_CHUNKS // 2)
        def _():
            ffn(MOE_BLOCK // 2)

        start_out(b, slot)
        return carry

    obuf[...] = jnp.zeros(obuf.shape, jnp.int32)
    lax.fori_loop(0, nb, block, 0)

    @pl.when(nb > 0)
    def _():
        last_slot = (nb - 1) % 2
        wait_in(1 - last_slot)

        @pl.when(nb >= 2)
        def _():
            wait_out(1 - last_slot)
        wait_out(last_slot)


def _moe(hs, src, block_e, next_e, nblocks, w1, b1, w2, b2, scratch_chunk):
    full = lambda shape: pl.BlockSpec(shape, lambda i, *_: (0,) * len(shape))
    hbm = pl.BlockSpec(memory_space=pl.ANY)
    return pl.pallas_call(
        functools.partial(_moe_kernel, scratch_chunk=scratch_chunk),
        grid_spec=pltpu.PrefetchScalarGridSpec(
            num_scalar_prefetch=4,
            grid=(1,),
            in_specs=[hbm, hbm, full((N_EXPERTS, 1, 2 * D_FF)), hbm, full((N_EXPERTS, 1, D_MODEL))],
            out_specs=hbm,
            scratch_shapes=[pltpu.VMEM((2, MOE_BLOCK, ROW_W), jnp.int32),
                            pltpu.VMEM((2, MOE_BLOCK, PACKED_W), jnp.int32),
                            pltpu.VMEM((D_MODEL, 2 * D_FF), F32),
                            pltpu.VMEM((D_FF, D_MODEL), F32),
                            pltpu.VMEM((D_MODEL, 2 * D_FF), BF16),
                            pltpu.VMEM((D_FF, D_MODEL), BF16),
                            pltpu.SemaphoreType.DMA((2,)),
                            pltpu.SemaphoreType.DMA((2,)),
                            pltpu.SemaphoreType.DMA((2,))]),
        out_shape=jax.ShapeDtypeStruct(hs.shape, jnp.int32),
        input_output_aliases={4: 0},
        compiler_params=_cparams(("arbitrary",)),
        name="moe_experts",
    )(src, block_e, next_e, nblocks, hs, w1, b1.reshape(N_EXPERTS, 1, 2 * D_FF), w2,
      b2.reshape(N_EXPERTS, 1, D_MODEL))


def _plan(cnt, n_blocks_max):
    nt = cnt.shape[0]
    nch = (cnt + (CHUNK - 1)) // CHUNK
    lstart = jnp.cumsum(nch, axis=1) - nch
    ne = jnp.sum(nch, axis=0)
    nbe = (ne + (BLOCK_CHUNKS - 1)) // BLOCK_CHUNKS
    bend = jnp.cumsum(nbe)
    nblocks = bend[-1]
    gstart = (bend - nbe)[None, :] * BLOCK_CHUNKS + (jnp.cumsum(nch, axis=0) - nch)
    s0 = jnp.arange(nt, dtype=jnp.int32)[:, None] * TILE_CHUNKS + lstart
    blk = jnp.arange(n_blocks_max + 1, dtype=jnp.int32)
    be = jnp.sum((blk[:, None] >= bend[None, :]).astype(jnp.int32), axis=1)
    be_last = jnp.sum((nblocks - 1 >= bend).astype(jnp.int32))
    be = jnp.minimum(be, be_last).astype(jnp.int32)
    eid = jnp.arange(N_EXPERTS, dtype=jnp.int32)
    strips = jnp.stack([gstart.T, nch.T, s0.T])
    mine = (be[:, None] == eid[None, :])[None, :, :, None]
    gs_b, nc_b, s0_b = jnp.sum(jnp.where(mine, strips[:, None], 0), axis=2)
    c = (blk[:, None] * BLOCK_CHUNKS + jnp.arange(BLOCK_CHUNKS, dtype=jnp.int32)[None, :])[:, :, None]
    inside = jnp.logical_and(c >= gs_b[:, None, :], c < (gs_b + nc_b)[:, None, :])
    src = (jnp.sum(jnp.where(inside, (s0_b - gs_b)[:, None, :] + c + 1, 0), axis=2) - 1).reshape(-1)
    be = be[:n_blocks_max]
    later = jnp.logical_and(eid[None, :] > be[:, None], (nbe > 0)[None, :])
    nxt = jnp.min(jnp.where(later, eid[None, :], N_EXPERTS), axis=1)
    nxt = jnp.where(nxt == N_EXPERTS, -1, nxt).astype(jnp.int32)
    real = jnp.sum((src.reshape(-1, BLOCK_CHUNKS) >= 0).astype(jnp.int32), axis=1)
    return src, be, nxt, jnp.concatenate([nblocks.reshape(1), real]).astype(jnp.int32)


def _combine_kernel(o_ref, slot_ref, x2_ref, gf_ref, y_ref):
    sub = slot_ref.shape[0]
    tile = x2_ref.shape[0] // sub
    ridx = lax.broadcasted_iota(jnp.int32, (TILE_ROWS, tile), 0).astype(F32)
    tn = functools.partial(lax.dot_general, dimension_numbers=(((0,), (0,)), ((), ())),
                           preferred_element_type=F32)
    for t in range(sub):
        gmat = jnp.zeros((TILE_ROWS, tile), F32)
        for k in range(TOP_K):
            gmat = jnp.where(ridx == slot_ref[t, k:k + 1, :], slot_ref[t, TOP_K + k:TOP_K + k + 1, :], gmat)
        gb = gmat.astype(BF16)
        oh, ol = _unpack_rows(o_ref[pl.ds(t * TILE_ROWS, TILE_ROWS), :])
        y = jnp.concatenate([tn(gb, oh), tn(gb, ol)], axis=1)
        rows = pl.ds(t * tile, tile)
        gf = gf_ref[...] if gf_ref.shape[0] == 1 else gf_ref[rows, :]
        y_ref[rows, :] = x2_ref[rows, :] + gf * y


def _combine(outs, slots, x2, mod, *, tile, sub, rows_per_mod, tile0):
    n = x2.shape[0]
    mrows = mod.shape[2]
    assert n % (sub * tile) == 0 and tile0 % sub == 0 and (mrows == 1 or sub == 1)
    return pl.pallas_call(
        _combine_kernel,
        grid=(n // (sub * tile),),
        in_specs=[pl.BlockSpec((sub * TILE_ROWS, PACKED_W), lambda i: (i + tile0 // sub, 0)),
                  pl.BlockSpec((sub, 2 * TOP_K, tile), lambda i: (i, 0, 0)),
                  pl.BlockSpec((sub * tile, D_MODEL), lambda i: (i, 0)),
                  _mod_spec(mod, 5, lambda i: (i * sub * tile) // rows_per_mod)],
        out_specs=pl.BlockSpec((sub * tile, D_MODEL), lambda i: (i, 0)),
        out_shape=jax.ShapeDtypeStruct((n, D_MODEL), F32),
        compiler_params=_cparams(("parallel",)),
        name="combine",
    )(outs, slots, x2, mod)


def kernel(x_prompt, x_sample, cache_k, cache_v, state_pool, c_prompt, c_sample, rel_bias, norm_mix, w_ada,
           b_ada, w_in, q_norm, k_norm, sinks, w_pool, pool_scale, w_out, norm_ffn, w_router, b_router,
           w1, b1, w2, b2):
    depth = w_in.shape[0]
    assert depth == 1
    l = 0
    bp, sp, _ = x_prompt.shape
    bs = x_sample.shape[0]
    assert x_sample.shape[1] == 1 and sp % SORT_TILE == 0 and bs <= SORT_TILE
    n_p = bp * sp
    tiles_p = n_p // SORT_TILE
    max_chunks = tiles_p * (SORT_TILE * TOP_K // CHUNK + N_EXPERTS) + (bs * TOP_K // CHUNK + N_EXPERTS)
    n_blocks_max = -(-max_chunks // BLOCK_CHUNKS) + N_EXPERTS

    pr = -(-bp // SUBLANES) * SUBLANES
    c_all = jnp.concatenate([c_prompt, jnp.zeros((pr - bp, D_MODEL), F32), c_sample], axis=0)
    mod = _modulation(c_all, w_ada[l], b_ada[l])
    mod_p = mod[:, :bp].reshape(6, bp, 1, D_MODEL)
    mod_s = mod[:, pr:].reshape(6, 1, bs, D_MODEL)

    head = jnp.arange(ATTN_WIDTH) // HEAD_DIM
    bd = (head[:, None] == head[None, :]).astype(BF16)
    w_in_b = w_in[l].astype(BF16)
    w_out_b = w_out[l].astype(BF16)
    w_pool_b = w_pool[l].astype(BF16)
    w_router_t = w_router[l].T

    xp = x_prompt.reshape(n_p, D_MODEL)
    q, k, v, u = _mixer_inputs(xp, mod_p, norm_mix[l], w_in_b, bd, q_norm[l], k_norm[l],
                               tile=1024, rows_per_mod=sp, precise=False)
    k3 = k.reshape(bp, sp, KV_WIDTH)
    v3 = v.reshape(bp, sp, KV_WIDTH)
    u3 = u.reshape(bp, sp, POOL_WIDTH)
    attn = _attn_prompt(q.reshape(bp, sp, ATTN_WIDTH), k3, v3, sinks[l], rel_bias)
    pool = _pool_prompt(u3, w_pool_b, pool_scale[l])
    keep = min(WINDOW, sp)
    nkp = k3[:, -keep:].reshape(bp, keep, N_KV_HEADS, HEAD_DIM)
    nvp = v3[:, -keep:].reshape(bp, keep, N_KV_HEADS, HEAD_DIM)
    if sp >= POOL_BUF:
        npp = u3[:, -POOL_BUF:]
    else:
        npp = jnp.concatenate([jnp.zeros((bp, POOL_BUF - sp, POOL_WIDTH), F32), u3], axis=1)
    x2_p, hs, slots_p, cnt_p = _route(
        xp, attn.reshape(n_p, ATTN_WIDTH), pool.reshape(n_p, POOL_WIDTH), mod_p,
        w_out_b, norm_ffn[l], w_router_t, b_router[l], None,
        tile=SORT_TILE, sub=ROUTE_SUB, rows_per_mod=sp, tile0=0, extra_tiles=max(2, ROUTE_SUB))

    xs = x_sample.reshape(bs, D_MODEL)
    qs, ks, vs, us = _mixer_inputs(xs, mod_s, norm_mix[l], w_in[l], bd, q_norm[l], k_norm[l],
                                   tile=bs, rows_per_mod=bs, precise=True)
    wbuf = cache_k.shape[2]
    attn_s, nks, nvs = _attn_sample(qs, ks, vs, cache_k[l].reshape(bs, wbuf, KV_WIDTH),
                                    cache_v[l].reshape(bs, wbuf, KV_WIDTH), sinks[l], rel_bias)
    pool_s, nps_t = _pool_sample(jnp.swapaxes(state_pool[l], 0, 1), us, w_pool_b, pool_scale[l])
    x2_s, hs, slots_s, cnt_s = _route(
        xs, attn_s.astype(BF16), pool_s, mod_s, w_out_b, norm_ffn[l],
        w_router_t, b_router[l], hs, tile=bs, sub=1, rows_per_mod=bs, tile0=tiles_p, extra_tiles=0)

    cnt = jnp.concatenate([cnt_p[:, :, 0], cnt_s[:, :, 0]], axis=0).astype(jnp.int32)
    src, block_e, next_e, nblocks = _plan(cnt, n_blocks_max)
    outs = _moe(hs, src, block_e, next_e, nblocks, w1[l], b1[l], w2[l], b2[l],
                scratch_chunk=(tiles_p + 1) * TILE_CHUNKS)

    y_p = _combine(outs, slots_p, x2_p, mod_p, tile=SORT_TILE, sub=COMBINE_SUB, rows_per_mod=sp, tile0=0)
    y_s = _combine(outs, slots_s, x2_s, mod_s, tile=bs, sub=1, rows_per_mod=bs, tile0=tiles_p)

    return (y_p.reshape(bp, sp, D_MODEL), y_s.reshape(bs, 1, D_MODEL),
            nkp[None], nvp[None], npp[None],
            nks.reshape(1, bs, wbuf, N_KV_HEADS, HEAD_DIM), nvs.reshape(1, bs, wbuf, N_KV_HEADS, HEAD_DIM),
            jnp.swapaxes(nps_t, 0, 1)[None])
```

```python
import functools
import math

import jax
import jax.numpy as jnp
from jax import lax
from jax.experimental import pallas as pl
from jax.experimental.pallas import tpu as pltpu

F32 = jnp.float32
BF16 = jnp.bfloat16

D_MODEL = 1024
HEAD_DIM = 64
N_HEADS = 8
N_KV_HEADS = 2
GROUP = N_HEADS // N_KV_HEADS
ATTN_WIDTH = N_HEADS * HEAD_DIM
KV_WIDTH = N_KV_HEADS * HEAD_DIM
POOL_WIDTH = D_MODEL - ATTN_WIDTH
POOL_WINDOWS = (2, 4, 8, 16)
POOL_GROUP = POOL_WIDTH // len(POOL_WINDOWS)
POOL_BUF = max(POOL_WINDOWS) - 1
IN_WIDTH = ATTN_WIDTH + 2 * KV_WIDTH + POOL_WIDTH
WINDOW = 128
ATTN_BLOCK = 128
N_BUCKETS = 32
MAX_EXACT = 16
REL_MAX_DIST = 128
N_EXPERTS = 32
TOP_K = 4
D_FF = D_MODEL
SWIGLU_LIMIT = 7.0
SWIGLU_ALPHA = 1.702
EPS = 1e-6
NEG_INF = -1e30
PAST_LEN = 16384

LANES = 128
SUBLANES = 8
VMEM_LIMIT = 56 * 1024 * 1024

ATTN_QB = 8
POOL_HALO = 2 * SUBLANES

SORT_TILE = 256
ROUTE_SUB = 4
COMBINE_SUB = 4
CHUNK = SUBLANES
TILE_ROWS = -(-(SORT_TILE * TOP_K + N_EXPERTS * (CHUNK - 1)) // LANES) * LANES
TILE_CHUNKS = TILE_ROWS // CHUNK
MOE_BLOCK = 256
BLOCK_CHUNKS = MOE_BLOCK // CHUNK
WEIGHT_DMA_PRIORITY = 1
PACKED_W = D_MODEL // 2
ROW_W = PACKED_W
ZERO_CHUNK = TILE_CHUNKS - 1


def _bdot(a, b):
    return jnp.dot(a.astype(BF16), b.astype(BF16), preferred_element_type=F32)


def _split(a):
    hi = a.astype(BF16)
    lo = (a - hi.astype(F32)).astype(BF16)
    return hi, lo


def _dot3(a, b):
    ah, al = _split(a)
    bh, bl = _split(b)
    d = functools.partial(jnp.dot, preferred_element_type=F32)
    return d(ah, bh) + d(al, bh) + d(ah, bl)


def _pack_rows(x):
    bits = lax.bitcast_convert_type(x, jnp.int32)
    return bits[:, :PACKED_W] | lax.shift_right_logical(bits[:, PACKED_W:], 16)


def _unpack_rows(w):
    hi = lax.bitcast_convert_type(w & jnp.int32(-65536), F32)
    lo = lax.bitcast_convert_type(lax.shift_left(w, 16), F32)
    return hi.astype(BF16), lo.astype(BF16)


def _rms(x, g):
    return x * lax.rsqrt(jnp.mean(x * x, axis=-1, keepdims=True) + EPS) * g


def _cparams(sem, **kw):
    return pltpu.CompilerParams(dimension_semantics=sem, vmem_limit_bytes=VMEM_LIMIT, **kw)


def _ada_kernel(c_ref, w_ref, b_ref, o_ref):
    c = c_ref[...]
    s = c * jax.nn.sigmoid(c)
    for v in range(o_ref.shape[0]):
        cols = slice(v * D_MODEL, (v + 1) * D_MODEL)
        o_ref[v] = _dot3(s, w_ref[:, cols]) + b_ref[:, cols]


def _modulation(c, w_ada, b_ada, *, vectors_per_step=1):
    rows = c.shape[0]
    n = w_ada.shape[1]
    nvec = n // D_MODEL
    assert nvec % vectors_per_step == 0
    tn = vectors_per_step * D_MODEL
    return pl.pallas_call(
        _ada_kernel,
        grid=(nvec // vectors_per_step,),
        in_specs=[pl.BlockSpec((rows, D_MODEL), lambda j: (0, 0)),
                  pl.BlockSpec((D_MODEL, tn), lambda j: (0, j)),
                  pl.BlockSpec((1, tn), lambda j: (0, j))],
        out_specs=pl.BlockSpec((vectors_per_step, rows, D_MODEL), lambda j: (j, 0, 0)),
        out_shape=jax.ShapeDtypeStruct((nvec, rows, D_MODEL), F32),
        compiler_params=_cparams(("parallel",)),
        name="modulation",
    )(c, w_ada, b_ada.reshape(1, n))


def _head_rms(t, bd, g, precise):
    if precise:
        hi, lo = _split(t * t)
        ss = jnp.dot(hi, bd, preferred_element_type=F32) + jnp.dot(lo, bd, preferred_element_type=F32)
    else:
        ss = _bdot(t * t, bd)
    return t * lax.rsqrt(ss * (1.0 / HEAD_DIM) + EPS) * g


def _mixin_qkv(x_ref, sh_ref, sc_ref, g_ref, w_ref, bd_ref, qn_ref, kn_ref, q_ref, k_ref, v_ref, precise):
    h = _rms(x_ref[...], g_ref[...]) * (1.0 + sc_ref[...]) + sh_ref[...]
    z = _dot3(h, w_ref[...]) if precise else _bdot(h, w_ref[...])
    q = z[:, :ATTN_WIDTH]
    k = z[:, ATTN_WIDTH:ATTN_WIDTH + KV_WIDTH]
    bd = bd_ref[...]
    q = _head_rms(q, bd, qn_ref[...], precise)
    k = _head_rms(k, bd[:KV_WIDTH, :KV_WIDTH], kn_ref[...], precise)
    q_ref[...] = (q * (HEAD_DIM ** -0.5)).astype(BF16)
    k_ref[...] = k
    v_ref[...] = z[:, ATTN_WIDTH + KV_WIDTH:ATTN_WIDTH + 2 * KV_WIDTH]
    return z[:, ATTN_WIDTH + 2 * KV_WIDTH:]


def _mixin_kernel(x_ref, sh_ref, sc_ref, g_ref, w_ref, bd_ref, qn_ref, kn_ref,
                  q_ref, k_ref, v_ref, u_ref, *, precise):
    u_ref[...] = _mixin_qkv(x_ref, sh_ref, sc_ref, g_ref, w_ref, bd_ref, qn_ref, kn_ref,
                            q_ref, k_ref, v_ref, precise)


def _mixin_pool_kernel(x_ref, sh_ref, sc_ref, g_ref, w_ref, bd_ref, qn_ref, kn_ref, wp_ref, ps_ref,
                       q_ref, k_ref, v_ref, pool_ref, tail_ref, ext, lv, carry, *, tiles_per_seq):
    u = _mixin_qkv(x_ref, sh_ref, sc_ref, g_ref, w_ref, bd_ref, qn_ref, kn_ref, q_ref, k_ref, v_ref, False)
    t = pl.program_id(0) % tiles_per_seq

    @pl.when(t == 0)
    def _():
        carry[...] = jnp.zeros(carry.shape, F32)

    pool_ref[...] = _pool_tile(u, carry[...], t * u.shape[0], wp_ref, ps_ref, ext, lv)
    last = u[u.shape[0] - carry.shape[0]:, :]
    carry[...] = last
    tail_ref[...] = last


def _mod_spec(mod, k, group_of_step):
    return pl.BlockSpec((None, None, mod.shape[2], D_MODEL), lambda i: (k, group_of_step(i), 0, 0))


def _mixer_inputs(x2d, mod, norm_mix, w_in, bd, q_norm, k_norm, *, tile, rows_per_mod, precise, pool=None):
    n = x2d.shape[0]
    group = lambda i: (i * tile) // rows_per_mod
    const = lambda shape: pl.BlockSpec(shape, lambda i: (0,) * len(shape))
    row = lambda w: pl.BlockSpec((tile, w), lambda i: (i, 0))
    in_specs = [row(D_MODEL), _mod_spec(mod, 0, group), _mod_spec(mod, 1, group), const((1, D_MODEL)),
                const((D_MODEL, IN_WIDTH)), const((ATTN_WIDTH, ATTN_WIDTH)), const((1, ATTN_WIDTH)),
                const((1, KV_WIDTH))]
    args = [x2d, mod, mod, norm_mix.reshape(1, D_MODEL), w_in, bd,
            jnp.tile(q_norm, N_HEADS).reshape(1, ATTN_WIDTH), jnp.tile(k_norm, N_KV_HEADS).reshape(1, KV_WIDTH)]
    out_specs = [row(ATTN_WIDTH), row(KV_WIDTH), row(KV_WIDTH)]
    out_shape = [jax.ShapeDtypeStruct((n, ATTN_WIDTH), BF16),
                 jax.ShapeDtypeStruct((n, KV_WIDTH), F32),
                 jax.ShapeDtypeStruct((n, KV_WIDTH), F32)]
    if pool is None:
        kern = functools.partial(_mixin_kernel, precise=precise)
        out_specs.append(row(POOL_WIDTH))
        out_shape.append(jax.ShapeDtypeStruct((n, POOL_WIDTH), F32))
        scratch, semantics = [], ("parallel",)
    else:
        assert not precise and rows_per_mod % tile == 0 and tile >= POOL_HALO
        w_pool, pool_scale = pool
        kern = functools.partial(_mixin_pool_kernel, tiles_per_seq=rows_per_mod // tile)
        in_specs += [const(w_pool.shape), const((1, POOL_WIDTH))]
        args += [w_pool, pool_scale.reshape(1, POOL_WIDTH)]
        out_specs += [pl.BlockSpec((tile, POOL_WIDTH), lambda i: (i, 0)),
                      pl.BlockSpec((None, POOL_HALO, POOL_WIDTH), lambda i: (group(i), 0, 0))]
        out_shape += [jax.ShapeDtypeStruct((n, POOL_WIDTH), BF16),
                      jax.ShapeDtypeStruct((n // rows_per_mod, POOL_HALO, POOL_WIDTH), F32)]
        ext_rows = SUBLANES + POOL_HALO + tile
        scratch = [pltpu.VMEM((ext_rows, POOL_WIDTH), F32),
                   pltpu.VMEM((len(POOL_WINDOWS) - 1, ext_rows, POOL_GROUP), F32),
                   pltpu.VMEM((POOL_HALO, POOL_WIDTH), F32)]
        semantics = ("arbitrary",)
    return pl.pallas_call(
        kern,
        grid=(n // tile,),
        in_specs=in_specs,
        out_specs=out_specs,
        out_shape=out_shape,
        scratch_shapes=scratch,
        compiler_params=_cparams(semantics),
        name="mixer_inputs",
    )(*args)


def _t5_bucket(rel):
    n = jnp.maximum(rel, 0)
    nf = jnp.maximum(n, 1).astype(F32)
    large = MAX_EXACT + (jnp.log(nf / MAX_EXACT) / math.log(REL_MAX_DIST / MAX_EXACT)
                         * (N_BUCKETS - MAX_EXACT)).astype(jnp.int32)
    large = jnp.minimum(large, N_BUCKETS - 1)
    return jnp.where(n < MAX_EXACT, n, large)


def _bias_table(rel, rel_table):
    bucket = _t5_bucket(rel)
    table = rel_table.astype(F32)
    ids = jnp.arange(N_BUCKETS, dtype=bucket.dtype).reshape((N_BUCKETS, 1) + (1,) * rel.ndim)
    onehot = bucket[None, None] == ids
    bias = jnp.sum(jnp.where(onehot, table.reshape(table.shape + (1,) * rel.ndim), 0.0), axis=0)
    valid = (rel >= 0) & (rel < WINDOW)
    return jnp.where(valid[None], bias, NEG_INF)


def _attn_prompt_kernel(sink_ref, q_ref, kp_ref, kc_ref, vp_ref, vc_ref, bias_ref, o_ref):
    first = pl.program_id(1) == 0
    kk = jnp.concatenate([kp_ref[...], kc_ref[...]], axis=0).astype(BF16)
    vv = jnp.concatenate([vp_ref[...], vc_ref[...]], axis=0).astype(BF16)
    key = lax.broadcasted_iota(jnp.int32, (2 * ATTN_BLOCK, 1), 0)
    no_prev = jnp.logical_and(first, key < ATTN_BLOCK)
    lane = lax.broadcasted_iota(jnp.int32, (1, N_HEADS * ATTN_BLOCK), 1)
    sink = jnp.zeros((1, N_HEADS * ATTN_BLOCK), F32)
    for h in range(N_HEADS):
        sink = jnp.where(lane // ATTN_BLOCK == h, sink_ref[h], sink)
    contract = lambda a, b, dims: lax.dot_general(a, b, (dims, ((), ())), preferred_element_type=F32)
    part = GROUP * ATTN_BLOCK
    for i in range(ATTN_QB):
        q = q_ref[i * ATTN_BLOCK:(i + 1) * ATTN_BLOCK, :]
        keys = slice(i * ATTN_BLOCK, (i + 2) * ATTN_BLOCK)
        scores = []
        for kv in range(N_KV_HEADS):
            heads = range(kv * GROUP, (kv + 1) * GROUP)
            qg = jnp.concatenate([q[:, h * HEAD_DIM:(h + 1) * HEAD_DIM] for h in heads], axis=0)
            scores.append(contract(kk[keys, kv * HEAD_DIM:(kv + 1) * HEAD_DIM], qg, ((1,), (1,))))
        s = jnp.concatenate(scores, axis=1) + bias_ref[...]
        if i == 0:
            s = jnp.where(no_prev, NEG_INF, s)
        m = jnp.maximum(jnp.max(s, axis=0, keepdims=True), sink)
        p = jnp.exp(s - m)
        denom = jnp.sum(p, axis=0, keepdims=True) + jnp.exp(sink - m)
        p = p.astype(BF16)
        halves = [contract(vv[keys, kv * HEAD_DIM:(kv + 1) * HEAD_DIM], p[:, kv * part:(kv + 1) * part],
                           ((0,), (0,))) / denom[:, kv * part:(kv + 1) * part]
                  for kv in range(N_KV_HEADS)]
        o_t = jnp.concatenate(halves, axis=0)
        per_g = [o_t[:, g * ATTN_BLOCK:(g + 1) * ATTN_BLOCK].T for g in range(GROUP)]
        out = [t[:, kv * HEAD_DIM:(kv + 1) * HEAD_DIM] for kv in range(N_KV_HEADS) for t in per_g]
        o_ref[i * ATTN_BLOCK:(i + 1) * ATTN_BLOCK, :] = jnp.concatenate(out, axis=-1).astype(BF16)


def _attn_prompt(q, k, v, sinks, rel_table):
    b, s = q.shape[:2]
    qrows = ATTN_QB * ATTN_BLOCK
    assert s % qrows == 0
    qi = jnp.arange(ATTN_BLOCK, dtype=jnp.int32)[:, None]
    si = jnp.arange(2 * ATTN_BLOCK, dtype=jnp.int32)[None, :]
    bias = _bias_table(qi + ATTN_BLOCK - si, rel_table)
    bias = bias.reshape(N_HEADS * ATTN_BLOCK, 2 * ATTN_BLOCK).T
    cur = lambda w: pl.BlockSpec((None, qrows, w), lambda i, j, *_: (i, j, 0))
    prev = lambda w: pl.BlockSpec((None, ATTN_BLOCK, w),
                                  lambda i, j, *_: (i, jnp.maximum(j * ATTN_QB - 1, 0), 0))
    return pl.pallas_call(
        _attn_prompt_kernel,
        grid_spec=pltpu.PrefetchScalarGridSpec(
            num_scalar_prefetch=1,
            grid=(b, s // qrows),
            in_specs=[cur(ATTN_WIDTH), prev(KV_WIDTH), cur(KV_WIDTH), prev(KV_WIDTH), cur(KV_WIDTH),
                      pl.BlockSpec(bias.shape, lambda i, j, *_: (0, 0))],
            out_specs=cur(ATTN_WIDTH)),
        out_shape=jax.ShapeDtypeStruct((b, s, ATTN_WIDTH), BF16),
        compiler_params=_cparams(("parallel", "parallel")),
        name="attn_prompt",
    )(sinks.astype(F32), q, k, k, v, v, bias)


def _attn_sample_kernel(sink_ref, q_ref, kc_ref, vc_ref, kn_ref, vn_ref, bias_ref, bnew_ref,
                        o_ref, nk_ref, nv_ref):
    kc = kc_ref[...]
    vc = vc_ref[...]
    kn = kn_ref[...]
    vn = vn_ref[...]
    w = kc.shape[1]
    pos = lax.broadcasted_iota(jnp.int32, kc.shape, 1)
    nk_ref[...] = jnp.where(pos == w - 1, kn[:, None, :], pltpu.roll(kc, w - 1, 1))
    nv_ref[...] = jnp.where(pos == w - 1, vn[:, None, :], pltpu.roll(vc, w - 1, 1))
    gi = lax.broadcasted_iota(jnp.int32, (1, GROUP, 1), 1)
    for kv in range(N_KV_HEADS):
        sl = slice(kv * HEAD_DIM, (kv + 1) * HEAD_DIM)
        qg = q_ref[:, kv]
        s = jnp.einsum('bgd,bsd->bgs', qg, kc[:, :, sl].astype(BF16), preferred_element_type=F32)
        s = s + bias_ref[kv][None]
        s_new = jnp.sum(qg.astype(F32) * kn[:, None, sl], axis=-1, keepdims=True) + bnew_ref[kv][None]
        sink = jnp.zeros((1, GROUP, 1), F32)
        for g in range(GROUP):
            sink = jnp.where(gi == g, sink_ref[kv * GROUP + g], sink)
        m = jnp.maximum(jnp.maximum(jnp.max(s, axis=-1, keepdims=True), s_new), sink)
        p = jnp.exp(s - m)
        p_new = jnp.exp(s_new - m)
        denom = jnp.sum(p, axis=-1, keepdims=True) + p_new + jnp.exp(sink - m)
        o = jnp.einsum('bgs,bsd->bgd', p.astype(BF16), vc[:, :, sl].astype(BF16), preferred_element_type=F32)
        o = o + p_new * vn[:, None, sl]
        o_ref[:, kv] = o / denom


def _attn_sample(q, k_new, v_new, cache_k, cache_v, sinks, rel_table, *, tile=32):
    bd, w = cache_k.shape[:2]
    rel = w - jnp.arange(w, dtype=jnp.int32)
    bias = _bias_table(rel, rel_table).reshape(N_KV_HEADS, GROUP, w)
    bnew = _bias_table(jnp.zeros((1,), jnp.int32), rel_table).reshape(N_KV_HEADS, GROUP, 1)
    q4 = q.reshape(bd, N_KV_HEADS, GROUP, HEAD_DIM)
    spec4 = pl.BlockSpec((tile, N_KV_HEADS, GROUP, HEAD_DIM), lambda i, *_: (i, 0, 0, 0))
    cache = pl.BlockSpec((tile, w, KV_WIDTH), lambda i, *_: (i, 0, 0))
    new = pl.BlockSpec((tile, KV_WIDTH), lambda i, *_: (i, 0))
    const3 = lambda a: pl.BlockSpec(a.shape, lambda i, *_: (0, 0, 0))
    o, nk, nv = pl.pallas_call(
        _attn_sample_kernel,
        grid_spec=pltpu.PrefetchScalarGridSpec(
            num_scalar_prefetch=1,
            grid=(bd // tile,),
            in_specs=[spec4, cache, cache, new, new, const3(bias), const3(bnew)],
            out_specs=[spec4, cache, cache]),
        out_shape=[jax.ShapeDtypeStruct(q4.shape, F32),
                   jax.ShapeDtypeStruct(cache_k.shape, F32),
                   jax.ShapeDtypeStruct(cache_v.shape, F32)],
        compiler_params=_cparams(("parallel",)),
        name="attn_sample",
    )(sinks.astype(F32), q4, cache_k, cache_v, k_new, v_new, bias, bnew)
    return o.reshape(bd, ATTN_WIDTH), nk, nv


def _pool_project(d_groups, wp_ref, ps_ref):
    out = [_bdot(d, wp_ref[g]) for g, d in enumerate(d_groups)]
    return (jnp.concatenate(out, axis=-1) * ps_ref[...]).astype(BF16)


def _pool_tile(u, halo, pos0, wp_ref, ps_ref, ext, lv):
    tile = u.shape[0]
    lead, hb = SUBLANES, POOL_HALO
    ext[0:lead, :] = jnp.zeros((lead, ext.shape[1]), F32)
    ext[lead:lead + hb, :] = halo
    ext[lead + hb:, :] = u
    lv[:, 0:lead, :] = jnp.zeros((lv.shape[0], lead, lv.shape[2]), F32)
    pos = pos0 + lax.broadcasted_iota(jnp.int32, (tile, 1), 0)
    n = hb + tile
    ds = []
    for g, w in enumerate(POOL_WINDOWS):
        sl = slice(g * POOL_GROUP, (g + 1) * POOL_GROUP)
        acc = ext[lead:lead + n, sl] + ext[lead - 1:lead - 1 + n, sl]
        span, level = 2, 0
        while span < w:
            lv[level, lead:lead + n, :] = acc
            acc = acc + lv[level, lead - span:lead - span + n, :]
            span, level = 2 * span, level + 1
        cnt = jnp.minimum(pos + 1, w).astype(F32)
        ds.append(acc[hb:] / cnt - ext[lead + hb:lead + hb + tile, sl])
    return _pool_project(ds, wp_ref, ps_ref)


def _pool_sample_kernel(st_ref, u_ref, wp_ref, ps_ref, o_ref, ns_ref):
    u = u_ref[...]
    ns_ref[0:POOL_BUF - 1] = st_ref[1:POOL_BUF]
    ns_ref[POOL_BUF - 1] = u
    ds = []
    for g, w in enumerate(POOL_WINDOWS):
        sl = slice(g * POOL_GROUP, (g + 1) * POOL_GROUP)
        acc = u[:, sl]
        for j in range(1, w):
            acc = acc + st_ref[POOL_BUF - j][:, sl]
        cnt = float(min(PAST_LEN + 1, w))
        ds.append(acc / cnt - u[:, sl])
    o_ref[...] = _pool_project(ds, wp_ref, ps_ref)


def _pool_sample(state_t, u, w_pool, pool_scale):
    nb, bd, c = state_t.shape
    full = lambda a: pl.BlockSpec(a.shape, lambda: (0,) * a.ndim)
    ps = pool_scale.reshape(1, c)
    return pl.pallas_call(
        _pool_sample_kernel,
        in_specs=[full(state_t), full(u), full(w_pool), full(ps)],
        out_specs=[pl.BlockSpec((bd, c), lambda: (0, 0)), full(state_t)],
        out_shape=[jax.ShapeDtypeStruct((bd, c), BF16), jax.ShapeDtypeStruct(state_t.shape, F32)],
        compiler_params=pltpu.CompilerParams(vmem_limit_bytes=VMEM_LIMIT),
        name="pool_sample",
    )(state_t, u, w_pool, ps)


def _route_kernel(x_ref, attn_ref, pool_ref, gm_ref, sh_ref, sc_ref, wo_ref, nf_ref, wr_ref, br_ref,
                  tri_t_ref, tri_e_ref,
                  x2_ref, hs_ref, slot_ref, cnt_ref):
    sub = slot_ref.shape[0]
    n = x_ref.shape[0]
    tile = n // sub
    mixed = jnp.concatenate([attn_ref[...], pool_ref[...]], axis=1)
    mix = jnp.dot(mixed, wo_ref[...], preferred_element_type=F32)
    x2 = x_ref[...] + gm_ref[...] * mix
    x2_ref[...] = x2
    h = _rms(x2, nf_ref[...]) * (1.0 + sc_ref[...]) + sh_ref[...]

    hh, hl = _split(h)
    wh, wl = _split(wr_ref[...])
    nt = functools.partial(lax.dot_general, dimension_numbers=(((1,), (1,)), ((), ())),
                           preferred_element_type=F32)
    logits = nt(wh, hh) + nt(wl, hh) + nt(wh, hl) + br_ref[...]

    eidx = lax.broadcasted_iota(jnp.int32, (N_EXPERTS, n), 0).astype(F32)
    work = logits
    tops, picks = [], []
    for _ in range(TOP_K):
        m = jnp.max(work, axis=0, keepdims=True)
        pick = jnp.min(jnp.where(work == m, eidx, float(N_EXPERTS)), axis=0, keepdims=True)
        work = jnp.where(eidx == pick, -jnp.inf, work)
        tops.append(m)
        picks.append(pick)
    ex = [jnp.exp(v - tops[0]) for v in tops]
    den = ex[0] + ex[1] + ex[2] + ex[3]
    gates = [e / den for e in ex]

    sel = jnp.zeros((N_EXPERTS, n), F32)
    for pick in picks:
        sel = sel + (eidx == pick).astype(F32)
    selb = sel.astype(BF16)
    rank = jnp.concatenate([jnp.dot(selb[:, t * tile:(t + 1) * tile], tri_t_ref[...], preferred_element_type=F32)
                            for t in range(sub)], axis=1)
    cnts = [jnp.sum(sel[:, t * tile:(t + 1) * tile], axis=1, keepdims=True) for t in range(sub)]
    padded = jnp.concatenate(
        [jnp.broadcast_to(jnp.ceil(c * (1.0 / CHUNK)) * CHUNK, (N_EXPERTS, LANES)) for c in cnts], axis=1)
    seg = jnp.dot(tri_e_ref[...], padded.astype(BF16), preferred_element_type=F32)
    dest = jnp.concatenate([seg[:, t * LANES:t * LANES + 1] + rank[:, t * tile:(t + 1) * tile]
                            for t in range(sub)], axis=1)
    slots = [jnp.sum(jnp.where(eidx == pick, dest, 0.0), axis=0, keepdims=True) for pick in picks]

    ridx = lax.broadcasted_iota(jnp.int32, (TILE_ROWS, tile), 0).astype(F32)
    for t in range(sub):
        cols = slice(t * tile, (t + 1) * tile)
        cnt_ref[t] = jnp.broadcast_to(cnts[t], (N_EXPERTS, LANES))
        slot_ref[t] = jnp.concatenate([v[:, cols] for v in slots + gates], axis=0)
        hit = ridx == slots[0][:, cols]
        for s in slots[1:]:
            hit = jnp.logical_or(hit, ridx == s[:, cols])
        perm = jnp.where(hit, 1.0, 0.0).astype(BF16)
        hs_ref[pl.ds(t * TILE_ROWS, TILE_ROWS), :] = _pack_rows(
            jnp.dot(perm, hh[t * tile:(t + 1) * tile, :], preferred_element_type=F32))


def _route(x2d, attn, pool, mod, w_out, norm_ffn, w_router_t, b_router, hs_prev, *, tile, sub,
           rows_per_mod, tile0, extra_tiles):
    n = x2d.shape[0]
    nt = n // tile
    assert nt % sub == 0 and extra_tiles % sub == 0
    own_steps = nt // sub
    steps = own_steps + extra_tiles // sub
    mrows = mod.shape[2]
    assert mrows == 1 or (sub == 1 and mrows == tile)
    last = lambda i: jnp.minimum(i, own_steps - 1)
    group = lambda i: (last(i) * sub * tile) // rows_per_mod
    const = lambda shape: pl.BlockSpec(shape, lambda i: (0,) * len(shape))
    row = lambda w: pl.BlockSpec((sub * tile, w), lambda i: (last(i), 0))
    tri_t = (jnp.arange(tile)[:, None] < jnp.arange(tile)[None, :]).astype(BF16)
    tri_e = (jnp.arange(N_EXPERTS)[None, :] < jnp.arange(N_EXPERTS)[:, None]).astype(BF16)
    in_specs = [row(D_MODEL), row(ATTN_WIDTH), row(POOL_WIDTH),
                _mod_spec(mod, 2, group), _mod_spec(mod, 3, group), _mod_spec(mod, 4, group),
                const((D_MODEL, D_MODEL)), const((1, D_MODEL)), const((N_EXPERTS, D_MODEL)),
                const((N_EXPERTS, 1)), const((tile, tile)), const((N_EXPERTS, N_EXPERTS))]
    args = [x2d, attn, pool, mod, mod, mod, w_out, norm_ffn.reshape(1, D_MODEL), w_router_t,
            b_router.reshape(N_EXPERTS, 1), tri_t, tri_e]
    n_in = len(args)
    kern = _route_kernel
    aliases = {}
    hs_rows = (tile0 + steps * sub) * TILE_ROWS
    assert tile0 % sub == 0
    if hs_prev is not None:
        in_specs.append(pl.BlockSpec(memory_space=pl.ANY))
        args.append(hs_prev)
        aliases = {n_in: 1}
        kern = lambda *refs: _route_kernel(*refs[:n_in], *refs[n_in + 1:])
        hs_rows = hs_prev.shape[0]
    return pl.pallas_call(
        kern,
        grid=(steps,),
        in_specs=in_specs,
        out_specs=[row(D_MODEL),
                   pl.BlockSpec((sub * TILE_ROWS, ROW_W), lambda i: (i + tile0 // sub, 0)),
                   pl.BlockSpec((sub, 2 * TOP_K, tile), lambda i: (last(i), 0, 0)),
                   pl.BlockSpec((sub, N_EXPERTS, LANES), lambda i: (last(i), 0, 0))],
        out_shape=[jax.ShapeDtypeStruct((n, D_MODEL), F32),
                   jax.ShapeDtypeStruct((hs_rows, ROW_W), jnp.int32),
                   jax.ShapeDtypeStruct((nt, 2 * TOP_K, tile), F32),
                   jax.ShapeDtypeStruct((nt, N_EXPERTS, LANES), F32)],
        input_output_aliases=aliases,
        compiler_params=_cparams(("arbitrary",)),
        name="route",
    )(*args)


def _moe_kernel(src_ref, be_ref, nxt_ref, nb_ref, hs_hbm, w1_hbm, b1_ref, w2_hbm, b2_ref, out_hbm,
                lhs, obuf, w1s, w2s, w1c, w2c, sem_in, sem_out, sem_w, *, scratch_chunk):
    nb = nb_ref[0]

    def weight_copies(e):
        return (pltpu.make_async_copy(w1_hbm.at[e], w1s, sem_w.at[0]),
                pltpu.make_async_copy(w2_hbm.at[e], w2s, sem_w.at[1]))

    def chunk_rows(c):
        return pl.ds(pl.multiple_of(c * CHUNK, CHUNK), CHUNK)

    def start_in(blk, s):
        for j in range(BLOCK_CHUNKS):
            c = src_ref[blk * BLOCK_CHUNKS + j]
            c = jnp.where(c < 0, ZERO_CHUNK, c)
            pltpu.make_async_copy(hs_hbm.at[chunk_rows(c)], lhs.at[s, pl.ds(j * CHUNK, CHUNK)],
                                  sem_in.at[s]).start()

    def wait_in(s):
        pltpu.make_async_copy(hs_hbm.at[pl.ds(0, MOE_BLOCK)], lhs.at[s], sem_in.at[s]).wait()

    def start_out(blk, s):
        for j in range(BLOCK_CHUNKS):
            c = src_ref[blk * BLOCK_CHUNKS + j]
            c = jnp.where(c < 0, scratch_chunk + s * BLOCK_CHUNKS + j, c)
            pltpu.make_async_copy(obuf.at[s, pl.ds(j * CHUNK, CHUNK)],
                                  out_hbm.at[chunk_rows(c), pl.ds(0, PACKED_W)], sem_out.at[s]).start()

    def wait_out(s):
        pltpu.make_async_copy(obuf.at[s], out_hbm.at[pl.ds(0, MOE_BLOCK), pl.ds(0, PACKED_W)],
                              sem_out.at[s]).wait()

    @pl.when(nb > 0)
    def _():
        start_in(0, 0)
        for cp in weight_copies(be_ref[0]):
            cp.start(priority=WEIGHT_DMA_PRIORITY)

    def block(b, carry):
        slot = b % 2
        e = be_ref[b]

        @pl.when(jnp.logical_or(b == 0, e != be_ref[jnp.maximum(b - 1, 0)]))
        def _():
            for cp in weight_copies(e):
                cp.wait()
            w1c[...] = w1s[...].astype(BF16)
            w2c[...] = w2s[...].astype(BF16)
            nxt = nxt_ref[b]

            @pl.when(nxt >= 0)
            def _():
                for cp in weight_copies(nxt):
                    cp.start(priority=WEIGHT_DMA_PRIORITY)

        wait_in(slot)

        @pl.when(b >= 2)
        def _():
            wait_out(slot)

        start_in(b + 1, 1 - slot)

        def ffn(nrows):
            xh, xl = _unpack_rows(lhs[slot, 0:nrows, :])
            x = jnp.concatenate([xh, xl], axis=1)
            gu = jnp.dot(x, w1c[...], preferred_element_type=F32) + b1_ref[e]
            gate = jnp.minimum(gu[:, :D_FF], SWIGLU_LIMIT)
            up = jnp.clip(gu[:, D_FF:], -SWIGLU_LIMIT, SWIGLU_LIMIT)
            act = (up + 1.0) * (gate * jax.nn.sigmoid(SWIGLU_ALPHA * gate))
            y = jnp.dot(act.astype(BF16), w2c[...], preferred_element_type=F32) + b2_ref[e]
            obuf[slot, 0:nrows, :] = _pack_rows(y.astype(BF16).astype(F32))

        real = nb_ref[1 + b]

        @pl.when(real > BLOCK_CHUNKS // 2)
        def _():
            ffn(MOE_BLOCK)

        @pl.when(real <= BLOCK_CHUNKS // 2)
        def _():
            ffn(MOE_BLOCK // 2)

        start_out(b, slot)
        return carry

    obuf[...] = jnp.zeros(obuf.shape, jnp.int32)
    lax.fori_loop(0, nb, block, 0)

    @pl.when(nb > 0)
    def _():
        last_slot = (nb - 1) % 2
        wait_in(1 - last_slot)

        @pl.when(nb >= 2)
        def _():
            wait_out(1 - last_slot)
        wait_out(last_slot)


def _moe(hs, src, block_e, next_e, nblocks, w1, b1, w2, b2, scratch_chunk):
    full = lambda shape: pl.BlockSpec(shape, lambda i, *_: (0,) * len(shape))
    hbm = pl.BlockSpec(memory_space=pl.ANY)
    return pl.pallas_call(
        functools.partial(_moe_kernel, scratch_chunk=scratch_chunk),
        grid_spec=pltpu.PrefetchScalarGridSpec(
            num_scalar_prefetch=4,
            grid=(1,),
            in_specs=[hbm, hbm, full((N_EXPERTS, 1, 2 * D_FF)), hbm, full((N_EXPERTS, 1, D_MODEL))],
            out_specs=hbm,
            scratch_shapes=[pltpu.VMEM((2, MOE_BLOCK, ROW_W), jnp.int32),
                            pltpu.VMEM((2, MOE_BLOCK, PACKED_W), jnp.int32),
                            pltpu.VMEM((D_MODEL, 2 * D_FF), F32),
                            pltpu.VMEM((D_FF, D_MODEL), F32),
                            pltpu.VMEM((D_MODEL, 2 * D_FF), BF16),
                            pltpu.VMEM((D_FF, D_MODEL), BF16),
                            pltpu.SemaphoreType.DMA((2,)),
                            pltpu.SemaphoreType.DMA((2,)),
                            pltpu.SemaphoreType.DMA((2,))]),
        out_shape=jax.ShapeDtypeStruct(hs.shape, jnp.int32),
        input_output_aliases={4: 0},
        compiler_params=_cparams(("arbitrary",)),
        name="moe_experts",
    )(src, block_e, next_e, nblocks, hs, w1, b1.reshape(N_EXPERTS, 1, 2 * D_FF), w2,
      b2.reshape(N_EXPERTS, 1, D_MODEL))


def _plan(cnt, n_blocks_max):
    nt = cnt.shape[0]
    nch = (cnt + (CHUNK - 1)) // CHUNK
    lstart = jnp.cumsum(nch, axis=1) - nch
    ne = jnp.sum(nch, axis=0)
    nbe = (ne + (BLOCK_CHUNKS - 1)) // BLOCK_CHUNKS
    bend = jnp.cumsum(nbe)
    nblocks = bend[-1]
    gstart = (bend - nbe)[None, :] * BLOCK_CHUNKS + (jnp.cumsum(nch, axis=0) - nch)
    s0 = jnp.arange(nt, dtype=jnp.int32)[:, None] * TILE_CHUNKS + lstart
    blk = jnp.arange(n_blocks_max + 1, dtype=jnp.int32)
    be = jnp.sum((blk[:, None] >= bend[None, :]).astype(jnp.int32), axis=1)
    be_last = jnp.sum((nblocks - 1 >= bend).astype(jnp.int32))
    be = jnp.minimum(be, be_last).astype(jnp.int32)
    eid = jnp.arange(N_EXPERTS, dtype=jnp.int32)
    strips = jnp.stack([gstart.T, nch.T, s0.T])
    mine = (be[:, None] == eid[None, :])[None, :, :, None]
    gs_b, nc_b, s0_b = jnp.sum(jnp.where(mine, strips[:, None], 0), axis=2)
    c = (blk[:, None] * BLOCK_CHUNKS + jnp.arange(BLOCK_CHUNKS, dtype=jnp.int32)[None, :])[:, :, None]
    inside = jnp.logical_and(c >= gs_b[:, None, :], c < (gs_b + nc_b)[:, None, :])
    src = (jnp.sum(jnp.where(inside, (s0_b - gs_b)[:, None, :] + c + 1, 0), axis=2) - 1).reshape(-1)
    be = be[:n_blocks_max]
    later = jnp.logical_and(eid[None, :] > be[:, None], (nbe > 0)[None, :])
    nxt = jnp.min(jnp.where(later, eid[None, :], N_EXPERTS), axis=1)
    nxt = jnp.where(nxt == N_EXPERTS, -1, nxt).astype(jnp.int32)
    real = jnp.sum((src.reshape(-1, BLOCK_CHUNKS) >= 0).astype(jnp.int32), axis=1)
    return src, be, nxt, jnp.concatenate([nblocks.reshape(1), real]).astype(jnp.int32)


def _combine_kernel(o_ref, slot_ref, x2_ref, gf_ref, y_ref):
    sub = slot_ref.shape[0]
    tile = x2_ref.shape[0] // sub
    ridx = lax.broadcasted_iota(jnp.int32, (TILE_ROWS, tile), 0).astype(F32)
    tn = functools.partial(lax.dot_general, dimension_numbers=(((0,), (0,)), ((), ())),
                           preferred_element_type=F32)
    for t in range(sub):
        gmat = jnp.zeros((TILE_ROWS, tile), F32)
        for k in range(TOP_K):
            gmat = jnp.where(ridx == slot_ref[t, k:k + 1, :], slot_ref[t, TOP_K + k:TOP_K + k + 1, :], gmat)
        gb = gmat.astype(BF16)
        oh, ol = _unpack_rows(o_ref[pl.ds(t * TILE_ROWS, TILE_ROWS), :])
        y = jnp.concatenate([tn(gb, oh), tn(gb, ol)], axis=1)
        rows = pl.ds(t * tile, tile)
        gf = gf_ref[...] if gf_ref.shape[0] == 1 else gf_ref[rows, :]
        y_ref[rows, :] = x2_ref[rows, :] + gf * y


def _combine(outs, slots, x2, mod, *, tile, sub, rows_per_mod, tile0):
    n = x2.shape[0]
    mrows = mod.shape[2]
    assert n % (sub * tile) == 0 and tile0 % sub == 0 and (mrows == 1 or sub == 1)
    return pl.pallas_call(
        _combine_kernel,
        grid=(n // (sub * tile),),
        in_specs=[pl.BlockSpec((sub * TILE_ROWS, PACKED_W), lambda i: (i + tile0 // sub, 0)),
                  pl.BlockSpec((sub, 2 * TOP_K, tile), lambda i: (i, 0, 0)),
                  pl.BlockSpec((sub * tile, D_MODEL), lambda i: (i, 0)),
                  _mod_spec(mod, 5, lambda i: (i * sub * tile) // rows_per_mod)],
        out_specs=pl.BlockSpec((sub * tile, D_MODEL), lambda i: (i, 0)),
        out_shape=jax.ShapeDtypeStruct((n, D_MODEL), F32),
        compiler_params=_cparams(("parallel",)),
        name="combine",
    )(outs, slots, x2, mod)


def kernel(x_prompt, x_sample, cache_k, cache_v, state_pool, c_prompt, c_sample, rel_bias, norm_mix, w_ada,
           b_ada, w_in, q_norm, k_norm, sinks, w_pool, pool_scale, w_out, norm_ffn, w_router, b_router,
           w1, b1, w2, b2):
    depth = w_in.shape[0]
    assert depth == 1
    l = 0
    bp, sp, _ = x_prompt.shape
    bs = x_sample.shape[0]
    assert x_sample.shape[1] == 1 and sp % SORT_TILE == 0 and bs <= SORT_TILE
    n_p = bp * sp
    tiles_p = n_p // SORT_TILE
    max_chunks = tiles_p * (SORT_TILE * TOP_K // CHUNK + N_EXPERTS) + (bs * TOP_K // CHUNK + N_EXPERTS)
    n_blocks_max = -(-max_chunks // BLOCK_CHUNKS) + N_EXPERTS

    pr = -(-bp // SUBLANES) * SUBLANES
    c_all = jnp.concatenate([c_prompt, jnp.zeros((pr - bp, D_MODEL), F32), c_sample], axis=0)
    mod = _modulation(c_all, w_ada[l], b_ada[l])
    mod_p = mod[:, :bp].reshape(6, bp, 1, D_MODEL)
    mod_s = mod[:, pr:].reshape(6, 1, bs, D_MODEL)

    head = jnp.arange(ATTN_WIDTH) // HEAD_DIM
    bd = (head[:, None] == head[None, :]).astype(BF16)
    w_in_b = w_in[l].astype(BF16)
    w_out_b = w_out[l].astype(BF16)
    w_pool_b = w_pool[l].astype(BF16)
    w_router_t = w_router[l].T

    xp = x_prompt.reshape(n_p, D_MODEL)
    q, k, v, pool, u_tail = _mixer_inputs(xp, mod_p, norm_mix[l], w_in_b, bd, q_norm[l], k_norm[l],
                                          tile=1024, rows_per_mod=sp, precise=False,
                                          pool=(w_pool_b, pool_scale[l]))
    k3 = k.reshape(bp, sp, KV_WIDTH)
    v3 = v.reshape(bp, sp, KV_WIDTH)
    attn = _attn_prompt(q.reshape(bp, sp, ATTN_WIDTH), k3, v3, sinks[l], rel_bias)
    keep = min(WINDOW, sp)
    nkp = k3[:, -keep:].reshape(bp, keep, N_KV_HEADS, HEAD_DIM)
    nvp = v3[:, -keep:].reshape(bp, keep, N_KV_HEADS, HEAD_DIM)
    npp = u_tail[:, POOL_HALO - POOL_BUF:]
    x2_p, hs, slots_p, cnt_p = _route(
        xp, attn.reshape(n_p, ATTN_WIDTH), pool.reshape(n_p, POOL_WIDTH), mod_p,
        w_out_b, norm_ffn[l], w_router_t, b_router[l], None,
        tile=SORT_TILE, sub=ROUTE_SUB, rows_per_mod=sp, tile0=0, extra_tiles=max(2, ROUTE_SUB))

    xs = x_sample.reshape(bs, D_MODEL)
    qs, ks, vs, us = _mixer_inputs(xs, mod_s, norm_mix[l], w_in[l], bd, q_norm[l], k_norm[l],
                                   tile=bs, rows_per_mod=bs, precise=True)
    wbuf = cache_k.shape[2]
    attn_s, nks, nvs = _attn_sample(qs, ks, vs, cache_k[l].reshape(bs, wbuf, KV_WIDTH),
                                    cache_v[l].reshape(bs, wbuf, KV_WIDTH), sinks[l], rel_bias)
    pool_s, nps_t = _pool_sample(jnp.swapaxes(state_pool[l], 0, 1), us, w_pool_b, pool_scale[l])
    x2_s, hs, slots_s, cnt_s = _route(
        xs, attn_s.astype(BF16), pool_s, mod_s, w_out_b, norm_ffn[l],
        w_router_t, b_router[l], hs, tile=bs, sub=1, rows_per_mod=bs, tile0=tiles_p, extra_tiles=0)

    cnt = jnp.concatenate([cnt_p[:, :, 0], cnt_s[:, :, 0]], axis=0).astype(jnp.int32)
    src, block_e, next_e, nblocks = _plan(cnt, n_blocks_max)
    outs = _moe(hs, src, block_e, next_e, nblocks, w1[l], b1[l], w2[l], b2[l],
                scratch_chunk=(tiles_p + 1) * TILE_CHUNKS)

    y_p = _combine(outs, slots_p, x2_p, mod_p, tile=SORT_TILE, sub=COMBINE_SUB, rows_per_mod=sp, tile0=0)
    y_s = _combine(outs, slots_s, x2_s, mod_s, tile=bs, sub=1, rows_per_mod=bs, tile0=tiles_p)

    return (y_p.reshape(bp, sp, D_MODEL), y_s.reshape(bs, 1, D_MODEL),
            nkp[None], nvp[None], npp[None],
            nks.reshape(1, bs, wbuf, N_KV_HEADS, HEAD_DIM), nvs.reshape(1, bs, wbuf, N_KV_HEADS, HEAD_DIM),
            jnp.swapaxes(nps_t, 0, 1)[None])
```

```python
import functools
import math

import jax
import jax.numpy as jnp
from jax import lax
from jax.experimental import pallas as pl
from jax.experimental.pallas import tpu as pltpu

F32 = jnp.float32
BF16 = jnp.bfloat16

D_MODEL = 1024
HEAD_DIM = 64
N_HEADS = 8
N_KV_HEADS = 2
GROUP = N_HEADS // N_KV_HEADS
ATTN_WIDTH = N_HEADS * HEAD_DIM
KV_WIDTH = N_KV_HEADS * HEAD_DIM
POOL_WIDTH = D_MODEL - ATTN_WIDTH
POOL_WINDOWS = (2, 4, 8, 16)
POOL_GROUP = POOL_WIDTH // len(POOL_WINDOWS)
POOL_BUF = max(POOL_WINDOWS) - 1
IN_WIDTH = ATTN_WIDTH + 2 * KV_WIDTH + POOL_WIDTH
WINDOW = 128
ATTN_BLOCK = 128
N_BUCKETS = 32
MAX_EXACT = 16
REL_MAX_DIST = 128
N_EXPERTS = 32
TOP_K = 4
D_FF = D_MODEL
SWIGLU_LIMIT = 7.0
SWIGLU_ALPHA = 1.702
EPS = 1e-6
NEG_INF = -1e30
PAST_LEN = 16384

LANES = 128
SUBLANES = 8
VMEM_LIMIT = 56 * 1024 * 1024

ATTN_QB = 8
POOL_HALO = 2 * SUBLANES

SORT_TILE = 256
ROUTE_SUB = 4
COMBINE_SUB = 4
CHUNK = SUBLANES
TILE_ROWS = -(-(SORT_TILE * TOP_K + N_EXPERTS * (CHUNK - 1)) // LANES) * LANES
TILE_CHUNKS = TILE_ROWS // CHUNK
MOE_BLOCK = 256
BLOCK_CHUNKS = MOE_BLOCK // CHUNK
WEIGHT_DMA_PRIORITY = 1
PACKED_W = D_MODEL // 2
ROW_W = PACKED_W
ZERO_CHUNK = TILE_CHUNKS - 1


def _bdot(a, b):
    return jnp.dot(a.astype(BF16), b.astype(BF16), preferred_element_type=F32)


def _split(a):
    hi = a.astype(BF16)
    lo = (a - hi.astype(F32)).astype(BF16)
    return hi, lo


def _dot3(a, b):
    ah, al = _split(a)
    bh, bl = _split(b)
    d = functools.partial(jnp.dot, preferred_element_type=F32)
    return d(ah, bh) + d(al, bh) + d(ah, bl)


def _pack_rows(x):
    bits = lax.bitcast_convert_type(x, jnp.int32)
    return bits[:, :PACKED_W] | lax.shift_right_logical(bits[:, PACKED_W:], 16)


def _unpack_rows(w):
    hi = lax.bitcast_convert_type(w & jnp.int32(-65536), F32)
    lo = lax.bitcast_convert_type(lax.shift_left(w, 16), F32)
    return hi.astype(BF16), lo.astype(BF16)


def _rms(x, g):
    return x * lax.rsqrt(jnp.mean(x * x, axis=-1, keepdims=True) + EPS) * g


def _cparams(sem, **kw):
    return pltpu.CompilerParams(dimension_semantics=sem, vmem_limit_bytes=VMEM_LIMIT, **kw)


def _ada_kernel(c_ref, w_ref, b_ref, o_ref):
    c = c_ref[...]
    s = c * jax.nn.sigmoid(c)
    for v in range(o_ref.shape[0]):
        cols = slice(v * D_MODEL, (v + 1) * D_MODEL)
        o_ref[v] = _dot3(s, w_ref[:, cols]) + b_ref[:, cols]


def _modulation(c, w_ada, b_ada, *, vectors_per_step=1):
    rows = c.shape[0]
    n = w_ada.shape[1]
    nvec = n // D_MODEL
    assert nvec % vectors_per_step == 0
    tn = vectors_per_step * D_MODEL
    return pl.pallas_call(
        _ada_kernel,
        grid=(nvec // vectors_per_step,),
        in_specs=[pl.BlockSpec((rows, D_MODEL), lambda j: (0, 0)),
                  pl.BlockSpec((D_MODEL, tn), lambda j: (0, j)),
                  pl.BlockSpec((1, tn), lambda j: (0, j))],
        out_specs=pl.BlockSpec((vectors_per_step, rows, D_MODEL), lambda j: (j, 0, 0)),
        out_shape=jax.ShapeDtypeStruct((nvec, rows, D_MODEL), F32),
        compiler_params=_cparams(("parallel",)),
        name="modulation",
    )(c, w_ada, b_ada.reshape(1, n))


def _head_rms(t, bd, g, precise):
    if precise:
        hi, lo = _split(t * t)
        ss = jnp.dot(hi, bd, preferred_element_type=F32) + jnp.dot(lo, bd, preferred_element_type=F32)
    else:
        ss = _bdot(t * t, bd)
    return t * lax.rsqrt(ss * (1.0 / HEAD_DIM) + EPS) * g


def _mixin_qkv(x_ref, sh_ref, sc_ref, g_ref, w_ref, bd_ref, qn_ref, kn_ref, q_ref, k_ref, v_ref, precise):
    h = _rms(x_ref[...], g_ref[...]) * (1.0 + sc_ref[...]) + sh_ref[...]
    z = _dot3(h, w_ref[...]) if precise else _bdot(h, w_ref[...])
    q = z[:, :ATTN_WIDTH]
    k = z[:, ATTN_WIDTH:ATTN_WIDTH + KV_WIDTH]
    bd = bd_ref[...]
    q = _head_rms(q, bd, qn_ref[...], precise)
    k = _head_rms(k, bd[:KV_WIDTH, :KV_WIDTH], kn_ref[...], precise)
    q_ref[...] = (q * (HEAD_DIM ** -0.5)).astype(BF16)
    k_ref[...] = k
    v_ref[...] = z[:, ATTN_WIDTH + KV_WIDTH:ATTN_WIDTH + 2 * KV_WIDTH]
    return z[:, ATTN_WIDTH + 2 * KV_WIDTH:]


def _mixin_kernel(x_ref, sh_ref, sc_ref, g_ref, w_ref, bd_ref, qn_ref, kn_ref,
                  q_ref, k_ref, v_ref, u_ref, *, precise):
    u_ref[...] = _mixin_qkv(x_ref, sh_ref, sc_ref, g_ref, w_ref, bd_ref, qn_ref, kn_ref,
                            q_ref, k_ref, v_ref, precise)


def _mixin_pool_kernel(x_ref, sh_ref, sc_ref, g_ref, w_ref, bd_ref, qn_ref, kn_ref, wp_ref, ps_ref,
                       q_ref, k_ref, v_ref, pool_ref, tail_ref, ext, lv, carry, *, tiles_per_seq):
    u = _mixin_qkv(x_ref, sh_ref, sc_ref, g_ref, w_ref, bd_ref, qn_ref, kn_ref, q_ref, k_ref, v_ref, False)
    t = pl.program_id(0) % tiles_per_seq

    @pl.when(t == 0)
    def _():
        carry[...] = jnp.zeros(carry.shape, F32)

    pool_ref[...] = _pool_tile(u, carry[...], t * u.shape[0], wp_ref, ps_ref, ext, lv)
    last = u[u.shape[0] - carry.shape[0]:, :]
    carry[...] = last
    tail_ref[...] = last


def _mod_spec(mod, k, group_of_step):
    return pl.BlockSpec((None, None, mod.shape[2], D_MODEL), lambda i: (k, group_of_step(i), 0, 0))


def _mixer_inputs(x2d, mod, norm_mix, w_in, bd, q_norm, k_norm, *, tile, rows_per_mod, precise, pool=None):
    n = x2d.shape[0]
    group = lambda i: (i * tile) // rows_per_mod
    const = lambda shape: pl.BlockSpec(shape, lambda i: (0,) * len(shape))
    row = lambda w: pl.BlockSpec((tile, w), lambda i: (i, 0))
    in_specs = [row(D_MODEL), _mod_spec(mod, 0, group), _mod_spec(mod, 1, group), const((1, D_MODEL)),
                const((D_MODEL, IN_WIDTH)), const((ATTN_WIDTH, ATTN_WIDTH)), const((1, ATTN_WIDTH)),
                const((1, KV_WIDTH))]
    args = [x2d, mod, mod, norm_mix.reshape(1, D_MODEL), w_in, bd,
            jnp.tile(q_norm, N_HEADS).reshape(1, ATTN_WIDTH), jnp.tile(k_norm, N_KV_HEADS).reshape(1, KV_WIDTH)]
    out_specs = [row(ATTN_WIDTH), row(KV_WIDTH), row(KV_WIDTH)]
    out_shape = [jax.ShapeDtypeStruct((n, ATTN_WIDTH), BF16),
                 jax.ShapeDtypeStruct((n, KV_WIDTH), F32),
                 jax.ShapeDtypeStruct((n, KV_WIDTH), F32)]
    if pool is None:
        kern = functools.partial(_mixin_kernel, precise=precise)
        out_specs.append(row(POOL_WIDTH))
        out_shape.append(jax.ShapeDtypeStruct((n, POOL_WIDTH), F32))
        scratch, semantics = [], ("parallel",)
    else:
        assert not precise and rows_per_mod % tile == 0 and tile >= POOL_HALO
        w_pool, pool_scale = pool
        kern = functools.partial(_mixin_pool_kernel, tiles_per_seq=rows_per_mod // tile)
        in_specs += [const(w_pool.shape), const((1, POOL_WIDTH))]
        args += [w_pool, pool_scale.reshape(1, POOL_WIDTH)]
        out_specs += [pl.BlockSpec((tile, POOL_WIDTH), lambda i: (i, 0)),
                      pl.BlockSpec((None, POOL_HALO, POOL_WIDTH), lambda i: (group(i), 0, 0))]
        out_shape += [jax.ShapeDtypeStruct((n, POOL_WIDTH), BF16),
                      jax.ShapeDtypeStruct((n // rows_per_mod, POOL_HALO, POOL_WIDTH), F32)]
        ext_rows = SUBLANES + POOL_HALO + tile
        scratch = [pltpu.VMEM((ext_rows, POOL_WIDTH), F32),
                   pltpu.VMEM((len(POOL_WINDOWS) - 1, ext_rows, POOL_GROUP), F32),
                   pltpu.VMEM((POOL_HALO, POOL_WIDTH), F32)]
        semantics = ("arbitrary",)
    return pl.pallas_call(
        kern,
        grid=(n // tile,),
        in_specs=in_specs,
        out_specs=out_specs,
        out_shape=out_shape,
        scratch_shapes=scratch,
        compiler_params=_cparams(semantics),
        name="mixer_inputs",
    )(*args)


def _t5_bucket(rel):
    n = jnp.maximum(rel, 0)
    nf = jnp.maximum(n, 1).astype(F32)
    large = MAX_EXACT + (jnp.log(nf / MAX_EXACT) / math.log(REL_MAX_DIST / MAX_EXACT)
                         * (N_BUCKETS - MAX_EXACT)).astype(jnp.int32)
    large = jnp.minimum(large, N_BUCKETS - 1)
    return jnp.where(n < MAX_EXACT, n, large)


def _bias_table(rel, rel_table):
    bucket = _t5_bucket(rel)
    table = rel_table.astype(F32)
    ids = jnp.arange(N_BUCKETS, dtype=bucket.dtype).reshape((N_BUCKETS, 1) + (1,) * rel.ndim)
    onehot = bucket[None, None] == ids
    bias = jnp.sum(jnp.where(onehot, table.reshape(table.shape + (1,) * rel.ndim), 0.0), axis=0)
    valid = (rel >= 0) & (rel < WINDOW)
    return jnp.where(valid[None], bias, NEG_INF)


def _attn_prompt_kernel(sink_ref, q_ref, kp_ref, kc_ref, vp_ref, vc_ref, bias_ref, o_ref):
    first = pl.program_id(1) == 0
    kk = jnp.concatenate([kp_ref[...], kc_ref[...]], axis=0).astype(BF16)
    vv = jnp.concatenate([vp_ref[...], vc_ref[...]], axis=0).astype(BF16)
    key = lax.broadcasted_iota(jnp.int32, (2 * ATTN_BLOCK, 1), 0)
    no_prev = jnp.logical_and(first, key < ATTN_BLOCK)
    lane = lax.broadcasted_iota(jnp.int32, (1, N_HEADS * ATTN_BLOCK), 1)
    sink = jnp.zeros((1, N_HEADS * ATTN_BLOCK), F32)
    for h in range(N_HEADS):
        sink = jnp.where(lane // ATTN_BLOCK == h, sink_ref[h], sink)
    contract = lambda a, b, dims: lax.dot_general(a, b, (dims, ((), ())), preferred_element_type=F32)
    part = GROUP * ATTN_BLOCK
    for i in range(ATTN_QB):
        q = q_ref[i * ATTN_BLOCK:(i + 1) * ATTN_BLOCK, :]
        keys = slice(i * ATTN_BLOCK, (i + 2) * ATTN_BLOCK)
        scores = []
        for kv in range(N_KV_HEADS):
            heads = range(kv * GROUP, (kv + 1) * GROUP)
            qg = jnp.concatenate([q[:, h * HEAD_DIM:(h + 1) * HEAD_DIM] for h in heads], axis=0)
            scores.append(contract(kk[keys, kv * HEAD_DIM:(kv + 1) * HEAD_DIM], qg, ((1,), (1,))))
        s = jnp.concatenate(scores, axis=1) + bias_ref[...]
        if i == 0:
            s = jnp.where(no_prev, NEG_INF, s)
        m = jnp.maximum(jnp.max(s, axis=0, keepdims=True), sink)
        p = jnp.exp(s - m)
        denom = jnp.sum(p, axis=0, keepdims=True) + jnp.exp(sink - m)
        p = p.astype(BF16)
        halves = [contract(vv[keys, kv * HEAD_DIM:(kv + 1) * HEAD_DIM], p[:, kv * part:(kv + 1) * part],
                           ((0,), (0,))) / denom[:, kv * part:(kv + 1) * part]
                  for kv in range(N_KV_HEADS)]
        o_t = jnp.concatenate(halves, axis=0)
        per_g = [o_t[:, g * ATTN_BLOCK:(g + 1) * ATTN_BLOCK].T for g in range(GROUP)]
        out = [t[:, kv * HEAD_DIM:(kv + 1) * HEAD_DIM] for kv in range(N_KV_HEADS) for t in per_g]
        o_ref[i * ATTN_BLOCK:(i + 1) * ATTN_BLOCK, :] = jnp.concatenate(out, axis=-1).astype(BF16)


def _attn_prompt(q, k, v, sinks, rel_table):
    b, s = q.shape[:2]
    qrows = ATTN_QB * ATTN_BLOCK
    assert s % qrows == 0
    qi = jnp.arange(ATTN_BLOCK, dtype=jnp.int32)[:, None]
    si = jnp.arange(2 * ATTN_BLOCK, dtype=jnp.int32)[None, :]
    bias = _bias_table(qi + ATTN_BLOCK - si, rel_table)
    bias = bias.reshape(N_HEADS * ATTN_BLOCK, 2 * ATTN_BLOCK).T
    cur = lambda w: pl.BlockSpec((None, qrows, w), lambda i, j, *_: (i, j, 0))
    prev = lambda w: pl.BlockSpec((None, ATTN_BLOCK, w),
                                  lambda i, j, *_: (i, jnp.maximum(j * ATTN_QB - 1, 0), 0))
    return pl.pallas_call(
        _attn_prompt_kernel,
        grid_spec=pltpu.PrefetchScalarGridSpec(
            num_scalar_prefetch=1,
            grid=(b, s // qrows),
            in_specs=[cur(ATTN_WIDTH), prev(KV_WIDTH), cur(KV_WIDTH), prev(KV_WIDTH), cur(KV_WIDTH),
                      pl.BlockSpec(bias.shape, lambda i, j, *_: (0, 0))],
            out_specs=cur(ATTN_WIDTH)),
        out_shape=jax.ShapeDtypeStruct((b, s, ATTN_WIDTH), BF16),
        compiler_params=_cparams(("parallel", "parallel")),
        name="attn_prompt",
    )(sinks.astype(F32), q, k, k, v, v, bias)


def _attn_sample_kernel(sink_ref, q_ref, kc_ref, vc_ref, kn_ref, vn_ref, bias_ref, bnew_ref,
                        o_ref, nk_ref, nv_ref):
    kc = kc_ref[...]
    vc = vc_ref[...]
    kn = kn_ref[...]
    vn = vn_ref[...]
    w = kc.shape[1]
    pos = lax.broadcasted_iota(jnp.int32, kc.shape, 1)
    nk_ref[...] = jnp.where(pos == w - 1, kn[:, None, :], pltpu.roll(kc, w - 1, 1))
    nv_ref[...] = jnp.where(pos == w - 1, vn[:, None, :], pltpu.roll(vc, w - 1, 1))
    gi = lax.broadcasted_iota(jnp.int32, (1, GROUP, 1), 1)
    for kv in range(N_KV_HEADS):
        sl = slice(kv * HEAD_DIM, (kv + 1) * HEAD_DIM)
        qg = q_ref[:, kv]
        s = jnp.einsum('bgd,bsd->bgs', qg, kc[:, :, sl].astype(BF16), preferred_element_type=F32)
        s = s + bias_ref[kv][None]
        s_new = jnp.sum(qg.astype(F32) * kn[:, None, sl], axis=-1, keepdims=True) + bnew_ref[kv][None]
        sink = jnp.zeros((1, GROUP, 1), F32)
        for g in range(GROUP):
            sink = jnp.where(gi == g, sink_ref[kv * GROUP + g], sink)
        m = jnp.maximum(jnp.maximum(jnp.max(s, axis=-1, keepdims=True), s_new), sink)
        p = jnp.exp(s - m)
        p_new = jnp.exp(s_new - m)
        denom = jnp.sum(p, axis=-1, keepdims=True) + p_new + jnp.exp(sink - m)
        o = jnp.einsum('bgs,bsd->bgd', p.astype(BF16), vc[:, :, sl].astype(BF16), preferred_element_type=F32)
        o = o + p_new * vn[:, None, sl]
        o_ref[:, kv] = o / denom


def _attn_sample(q, k_new, v_new, cache_k, cache_v, sinks, rel_table, *, tile=32):
    bd, w = cache_k.shape[:2]
    rel = w - jnp.arange(w, dtype=jnp.int32)
    bias = _bias_table(rel, rel_table).reshape(N_KV_HEADS, GROUP, w)
    bnew = _bias_table(jnp.zeros((1,), jnp.int32), rel_table).reshape(N_KV_HEADS, GROUP, 1)
    q4 = q.reshape(bd, N_KV_HEADS, GROUP, HEAD_DIM)
    spec4 = pl.BlockSpec((tile, N_KV_HEADS, GROUP, HEAD_DIM), lambda i, *_: (i, 0, 0, 0))
    cache = pl.BlockSpec((tile, w, KV_WIDTH), lambda i, *_: (i, 0, 0))
    new = pl.BlockSpec((tile, KV_WIDTH), lambda i, *_: (i, 0))
    const3 = lambda a: pl.BlockSpec(a.shape, lambda i, *_: (0, 0, 0))
    o, nk, nv = pl.pallas_call(
        _attn_sample_kernel,
        grid_spec=pltpu.PrefetchScalarGridSpec(
            num_scalar_prefetch=1,
            grid=(bd // tile,),
            in_specs=[spec4, cache, cache, new, new, const3(bias), const3(bnew)],
            out_specs=[spec4, cache, cache]),
        out_shape=[jax.ShapeDtypeStruct(q4.shape, F32),
                   jax.ShapeDtypeStruct(cache_k.shape, F32),
                   jax.ShapeDtypeStruct(cache_v.shape, F32)],
        compiler_params=_cparams(("parallel",)),
        name="attn_sample",
    )(sinks.astype(F32), q4, cache_k, cache_v, k_new, v_new, bias, bnew)
    return o.reshape(bd, ATTN_WIDTH), nk, nv


def _pool_project(d_groups, wp_ref, ps_ref):
    out = [_bdot(d, wp_ref[g]) for g, d in enumerate(d_groups)]
    return (jnp.concatenate(out, axis=-1) * ps_ref[...]).astype(BF16)


def _pool_tile(u, halo, pos0, wp_ref, ps_ref, ext, lv):
    tile = u.shape[0]
    lead, hb = SUBLANES, POOL_HALO
    ext[0:lead, :] = jnp.zeros((lead, ext.shape[1]), F32)
    ext[lead:lead + hb, :] = halo
    ext[lead + hb:, :] = u
    lv[:, 0:lead, :] = jnp.zeros((lv.shape[0], lead, lv.shape[2]), F32)
    pos = pos0 + lax.broadcasted_iota(jnp.int32, (tile, 1), 0)
    n = hb + tile
    ds = []
    for g, w in enumerate(POOL_WINDOWS):
        sl = slice(g * POOL_GROUP, (g + 1) * POOL_GROUP)
        acc = ext[lead:lead + n, sl] + ext[lead - 1:lead - 1 + n, sl]
        span, level = 2, 0
        while span < w:
            lv[level, lead:lead + n, :] = acc
            acc = acc + lv[level, lead - span:lead - span + n, :]
            span, level = 2 * span, level + 1
        cnt = jnp.minimum(pos + 1, w).astype(F32)
        ds.append(acc[hb:] / cnt - ext[lead + hb:lead + hb + tile, sl])
    return _pool_project(ds, wp_ref, ps_ref)


def _pool_sample_kernel(st_ref, u_ref, wp_ref, ps_ref, o_ref, ns_ref):
    u = u_ref[...]
    ns_ref[0:POOL_BUF - 1] = st_ref[1:POOL_BUF]
    ns_ref[POOL_BUF - 1] = u
    ds = []
    for g, w in enumerate(POOL_WINDOWS):
        sl = slice(g * POOL_GROUP, (g + 1) * POOL_GROUP)
        acc = u[:, sl]
        for j in range(1, w):
            acc = acc + st_ref[POOL_BUF - j][:, sl]
        cnt = float(min(PAST_LEN + 1, w))
        ds.append(acc / cnt - u[:, sl])
    o_ref[...] = _pool_project(ds, wp_ref, ps_ref)


def _pool_sample(state_t, u, w_pool, pool_scale):
    nb, bd, c = state_t.shape
    full = lambda a: pl.BlockSpec(a.shape, lambda: (0,) * a.ndim)
    ps = pool_scale.reshape(1, c)
    return pl.pallas_call(
        _pool_sample_kernel,
        in_specs=[full(state_t), full(u), full(w_pool), full(ps)],
        out_specs=[pl.BlockSpec((bd, c), lambda: (0, 0)), full(state_t)],
        out_shape=[jax.ShapeDtypeStruct((bd, c), BF16), jax.ShapeDtypeStruct(state_t.shape, F32)],
        compiler_params=pltpu.CompilerParams(vmem_limit_bytes=VMEM_LIMIT),
        name="pool_sample",
    )(state_t, u, w_pool, ps)


def _route_kernel(*refs, own_steps):
    hs_ref = refs[-3]

    @pl.when(pl.program_id(0) < own_steps)
    def _():
        _route_tiles(*refs)

    @pl.when(pl.program_id(0) >= own_steps)
    def _():
        hs_ref[...] = jnp.zeros(hs_ref.shape, jnp.int32)


def _route_tiles(x_ref, attn_ref, pool_ref, gm_ref, sh_ref, sc_ref, wo_ref, nf_ref, wr_ref, br_ref,
                 tri_t_ref, tri_e_ref,
                 x2_ref, hs_ref, slot_ref, cnt_ref):
    sub = slot_ref.shape[0]
    n = x_ref.shape[0]
    tile = n // sub
    mixed = jnp.concatenate([attn_ref[...], pool_ref[...]], axis=1)
    mix = jnp.dot(mixed, wo_ref[...], preferred_element_type=F32)
    x2 = x_ref[...] + gm_ref[...] * mix
    x2_ref[...] = x2
    h = _rms(x2, nf_ref[...]) * (1.0 + sc_ref[...]) + sh_ref[...]

    hh, hl = _split(h)
    wh, wl = _split(wr_ref[...])
    nt = functools.partial(lax.dot_general, dimension_numbers=(((1,), (1,)), ((), ())),
                           preferred_element_type=F32)
    logits = nt(wh, hh) + nt(wl, hh) + nt(wh, hl) + br_ref[...]

    eidx = lax.broadcasted_iota(jnp.int32, (N_EXPERTS, n), 0).astype(F32)
    work = logits
    tops, picks = [], []
    for _ in range(TOP_K):
        m = jnp.max(work, axis=0, keepdims=True)
        pick = jnp.min(jnp.where(work == m, eidx, float(N_EXPERTS)), axis=0, keepdims=True)
        work = jnp.where(eidx == pick, -jnp.inf, work)
        tops.append(m)
        picks.append(pick)
    ex = [jnp.exp(v - tops[0]) for v in tops]
    den = ex[0] + ex[1] + ex[2] + ex[3]
    gates = [e / den for e in ex]

    sel = jnp.zeros((N_EXPERTS, n), F32)
    for pick in picks:
        sel = sel + (eidx == pick).astype(F32)
    selb = sel.astype(BF16)
    rank = jnp.concatenate([jnp.dot(selb[:, t * tile:(t + 1) * tile], tri_t_ref[...], preferred_element_type=F32)
                            for t in range(sub)], axis=1)
    cnts = [jnp.sum(sel[:, t * tile:(t + 1) * tile], axis=1, keepdims=True) for t in range(sub)]
    padded = jnp.concatenate(
        [jnp.broadcast_to(jnp.ceil(c * (1.0 / CHUNK)) * CHUNK, (N_EXPERTS, LANES)) for c in cnts], axis=1)
    seg = jnp.dot(tri_e_ref[...], padded.astype(BF16), preferred_element_type=F32)
    dest = jnp.concatenate([seg[:, t * LANES:t * LANES + 1] + rank[:, t * tile:(t + 1) * tile]
                            for t in range(sub)], axis=1)
    slots = [jnp.sum(jnp.where(eidx == pick, dest, 0.0), axis=0, keepdims=True) for pick in picks]

    ridx = lax.broadcasted_iota(jnp.int32, (TILE_ROWS, tile), 0).astype(F32)
    for t in range(sub):
        cols = slice(t * tile, (t + 1) * tile)
        cnt_ref[t] = jnp.broadcast_to(cnts[t], (N_EXPERTS, LANES))
        slot_ref[t] = jnp.concatenate([v[:, cols] for v in slots + gates], axis=0)
        hit = ridx == slots[0][:, cols]
        for s in slots[1:]:
            hit = jnp.logical_or(hit, ridx == s[:, cols])
        perm = jnp.where(hit, 1.0, 0.0).astype(BF16)
        hs_ref[pl.ds(t * TILE_ROWS, TILE_ROWS), :] = _pack_rows(
            jnp.dot(perm, hh[t * tile:(t + 1) * tile, :], preferred_element_type=F32))


def _route(x2d, attn, pool, mod, w_out, norm_ffn, w_router_t, b_router, hs_prev, *, tile, sub,
           rows_per_mod, tile0, extra_tiles):
    n = x2d.shape[0]
    nt = n // tile
    assert nt % sub == 0 and extra_tiles % sub == 0
    own_steps = nt // sub
    steps = own_steps + extra_tiles // sub
    mrows = mod.shape[2]
    assert mrows == 1 or (sub == 1 and mrows == tile)
    last = lambda i: jnp.minimum(i, own_steps - 1)
    group = lambda i: (last(i) * sub * tile) // rows_per_mod
    const = lambda shape: pl.BlockSpec(shape, lambda i: (0,) * len(shape))
    row = lambda w: pl.BlockSpec((sub * tile, w), lambda i: (last(i), 0))
    tri_t = (jnp.arange(tile)[:, None] < jnp.arange(tile)[None, :]).astype(BF16)
    tri_e = (jnp.arange(N_EXPERTS)[None, :] < jnp.arange(N_EXPERTS)[:, None]).astype(BF16)
    in_specs = [row(D_MODEL), row(ATTN_WIDTH), row(POOL_WIDTH),
                _mod_spec(mod, 2, group), _mod_spec(mod, 3, group), _mod_spec(mod, 4, group),
                const((D_MODEL, D_MODEL)), const((1, D_MODEL)), const((N_EXPERTS, D_MODEL)),
                const((N_EXPERTS, 1)), const((tile, tile)), const((N_EXPERTS, N_EXPERTS))]
    args = [x2d, attn, pool, mod, mod, mod, w_out, norm_ffn.reshape(1, D_MODEL), w_router_t,
            b_router.reshape(N_EXPERTS, 1), tri_t, tri_e]
    n_in = len(args)
    kern = functools.partial(_route_kernel, own_steps=own_steps)
    aliases = {}
    hs_rows = (tile0 + steps * sub) * TILE_ROWS
    assert tile0 % sub == 0
    if hs_prev is not None:
        in_specs.append(pl.BlockSpec(memory_space=pl.ANY))
        args.append(hs_prev)
        aliases = {n_in: 1}
        kern = lambda *refs: _route_kernel(*refs[:n_in], *refs[n_in + 1:], own_steps=own_steps)
        hs_rows = hs_prev.shape[0]
    return pl.pallas_call(
        kern,
        grid=(steps,),
        in_specs=in_specs,
        out_specs=[row(D_MODEL),
                   pl.BlockSpec((sub * TILE_ROWS, ROW_W), lambda i: (i + tile0 // sub, 0)),
                   pl.BlockSpec((sub, 2 * TOP_K, tile), lambda i: (last(i), 0, 0)),
                   pl.BlockSpec((sub, N_EXPERTS, LANES), lambda i: (last(i), 0, 0))],
        out_shape=[jax.ShapeDtypeStruct((n, D_MODEL), F32),
                   jax.ShapeDtypeStruct((hs_rows, ROW_W), jnp.int32),
                   jax.ShapeDtypeStruct((nt, 2 * TOP_K, tile), F32),
                   jax.ShapeDtypeStruct((nt, N_EXPERTS, LANES), F32)],
        input_output_aliases=aliases,
        compiler_params=_cparams(("arbitrary",)),
        name="route",
    )(*args)


def _moe_kernel(src_ref, be_ref, nxt_ref, nb_ref, hs_hbm, w1_hbm, b1_ref, w2_hbm, b2_ref, out_hbm,
                lhs, obuf, w1s, w2s, w1c, w2c, sem_in, sem_out, sem_w, *, scratch_chunk):
    nb = nb_ref[0]

    def weight_copies(e):
        return (pltpu.make_async_copy(w1_hbm.at[e], w1s, sem_w.at[0]),
                pltpu.make_async_copy(w2_hbm.at[e], w2s, sem_w.at[1]))

    def chunk_rows(c):
        return pl.ds(pl.multiple_of(c * CHUNK, CHUNK), CHUNK)

    def start_in(blk, s):
        for j in range(BLOCK_CHUNKS):
            c = src_ref[blk * BLOCK_CHUNKS + j]
            c = jnp.where(c < 0, ZERO_CHUNK, c)
            pltpu.make_async_copy(hs_hbm.at[chunk_rows(c)], lhs.at[s, pl.ds(j * CHUNK, CHUNK)],
                                  sem_in.at[s]).start()

    def wait_in(s):
        pltpu.make_async_copy(hs_hbm.at[pl.ds(0, MOE_BLOCK)], lhs.at[s], sem_in.at[s]).wait()

    def start_out(blk, s):
        for j in range(BLOCK_CHUNKS):
            c = src_ref[blk * BLOCK_CHUNKS + j]
            c = jnp.where(c < 0, scratch_chunk + s * BLOCK_CHUNKS + j, c)
            pltpu.make_async_copy(obuf.at[s, pl.ds(j * CHUNK, CHUNK)],
                                  out_hbm.at[chunk_rows(c), pl.ds(0, PACKED_W)], sem_out.at[s]).start()

    def wait_out(s):
        pltpu.make_async_copy(obuf.at[s], out_hbm.at[pl.ds(0, MOE_BLOCK), pl.ds(0, PACKED_W)],
                              sem_out.at[s]).wait()

    @pl.when(nb > 0)
    def _():
        start_in(0, 0)
        for cp in weight_copies(be_ref[0]):
            cp.start(priority=WEIGHT_DMA_PRIORITY)

    def block(b, carry):
        slot = b % 2
        e = be_ref[b]

        @pl.when(jnp.logical_or(b == 0, e != be_ref[jnp.maximum(b - 1, 0)]))
        def _():
            for cp in weight_copies(e):
                cp.wait()
            w1c[...] = w1s[...].astype(BF16)
            w2c[...] = w2s[...].astype(BF16)
            nxt = nxt_ref[b]

            @pl.when(nxt >= 0)
            def _():
                for cp in weight_copies(nxt):
                    cp.start(priority=WEIGHT_DMA_PRIORITY)

        wait_in(slot)

        @pl.when(b >= 2)
        def _():
            wait_out(slot)

        start_in(b + 1, 1 - slot)

        def ffn(nrows):
            xh, xl = _unpack_rows(lhs[slot, 0:nrows, :])
            x = jnp.concatenate([xh, xl], axis=1)
            gu = jnp.dot(x, w1c[...], preferred_element_type=F32) + b1_ref[e]
            gate = jnp.minimum(gu[:, :D_FF], SWIGLU_LIMIT)
            up = jnp.clip(gu[:, D_FF:], -SWIGLU_LIMIT, SWIGLU_LIMIT)
            act = (up + 1.0) * (gate * jax.nn.sigmoid(SWIGLU_ALPHA * gate))
            y = jnp.dot(act.astype(BF16), w2c[...], preferred_element_type=F32) + b2_ref[e]
            obuf[slot, 0:nrows, :] = _pack_rows(y.astype(BF16).astype(F32))

        real = nb_ref[1 + b]

        @pl.when(real > BLOCK_CHUNKS // 2)
        def _():
            ffn(MOE_BLOCK)

        @pl.when(real <= BLOCK_CHUNKS // 2)
        def _():
            ffn(MOE_BLOCK // 2)

        start_out(b, slot)
        return carry

    obuf[...] = jnp.zeros(obuf.shape, jnp.int32)
    lax.fori_loop(0, nb, block, 0)

    @pl.when(nb > 0)
    def _():
        last_slot = (nb - 1) % 2
        wait_in(1 - last_slot)

        @pl.when(nb >= 2)
        def _():
            wait_out(1 - last_slot)
        wait_out(last_slot)


def _moe(hs, src, block_e, next_e, nblocks, w1, b1, w2, b2, scratch_chunk):
    full = lambda shape: pl.BlockSpec(shape, lambda i, *_: (0,) * len(shape))
    hbm = pl.BlockSpec(memory_space=pl.ANY)
    return pl.pallas_call(
        functools.partial(_moe_kernel, scratch_chunk=scratch_chunk),
        grid_spec=pltpu.PrefetchScalarGridSpec(
            num_scalar_prefetch=4,
            grid=(1,),
            in_specs=[hbm, hbm, full((N_EXPERTS, 1, 2 * D_FF)), hbm, full((N_EXPERTS, 1, D_MODEL))],
            out_specs=hbm,
            scratch_shapes=[pltpu.VMEM((2, MOE_BLOCK, ROW_W), jnp.int32),
                            pltpu.VMEM((2, MOE_BLOCK, PACKED_W), jnp.int32),
                            pltpu.VMEM((D_MODEL, 2 * D_FF), F32),
                            pltpu.VMEM((D_FF, D_MODEL), F32),
                            pltpu.VMEM((D_MODEL, 2 * D_FF), BF16),
                            pltpu.VMEM((D_FF, D_MODEL), BF16),
                            pltpu.SemaphoreType.DMA((2,)),
                            pltpu.SemaphoreType.DMA((2,)),
                            pltpu.SemaphoreType.DMA((2,))]),
        out_shape=jax.ShapeDtypeStruct(hs.shape, jnp.int32),
        input_output_aliases={4: 0},
        compiler_params=_cparams(("arbitrary",)),
        name="moe_experts",
    )(src, block_e, next_e, nblocks, hs, w1, b1.reshape(N_EXPERTS, 1, 2 * D_FF), w2,
      b2.reshape(N_EXPERTS, 1, D_MODEL))


def _plan(cnt, n_blocks_max):
    nt = cnt.shape[0]
    nch = (cnt + (CHUNK - 1)) // CHUNK
    lstart = jnp.cumsum(nch, axis=1) - nch
    ne = jnp.sum(nch, axis=0)
    nbe = (ne + (BLOCK_CHUNKS - 1)) // BLOCK_CHUNKS
    bend = jnp.cumsum(nbe)
    nblocks = bend[-1]
    gstart = (bend - nbe)[None, :] * BLOCK_CHUNKS + (jnp.cumsum(nch, axis=0) - nch)
    s0 = jnp.arange(nt, dtype=jnp.int32)[:, None] * TILE_CHUNKS + lstart
    blk = jnp.arange(n_blocks_max + 1, dtype=jnp.int32)
    be = jnp.sum((blk[:, None] >= bend[None, :]).astype(jnp.int32), axis=1)
    be_last = jnp.sum((nblocks - 1 >= bend).astype(jnp.int32))
    be = jnp.minimum(be, be_last).astype(jnp.int32)
    eid = jnp.arange(N_EXPERTS, dtype=jnp.int32)
    strips = jnp.stack([gstart.T, nch.T, s0.T])
    mine = (be[:, None] == eid[None, :])[None, :, :, None]
    gs_b, nc_b, s0_b = jnp.sum(jnp.where(mine, strips[:, None], 0), axis=2)
    c = (blk[:, None] * BLOCK_CHUNKS + jnp.arange(BLOCK_CHUNKS, dtype=jnp.int32)[None, :])[:, :, None]
    inside = jnp.logical_and(c >= gs_b[:, None, :], c < (gs_b + nc_b)[:, None, :])
    src = (jnp.sum(jnp.where(inside, (s0_b - gs_b)[:, None, :] + c + 1, 0), axis=2) - 1).reshape(-1)
    be = be[:n_blocks_max]
    later = jnp.logical_and(eid[None, :] > be[:, None], (nbe > 0)[None, :])
    nxt = jnp.min(jnp.where(later, eid[None, :], N_EXPERTS), axis=1)
    nxt = jnp.where(nxt == N_EXPERTS, -1, nxt).astype(jnp.int32)
    real = jnp.sum((src.reshape(-1, BLOCK_CHUNKS) >= 0).astype(jnp.int32), axis=1)
    return src, be, nxt, jnp.concatenate([nblocks.reshape(1), real]).astype(jnp.int32)


def _combine_kernel(o_ref, slot_ref, x2_ref, gf_ref, y_ref):
    sub = slot_ref.shape[0]
    tile = x2_ref.shape[0] // sub
    ridx = lax.broadcasted_iota(jnp.int32, (TILE_ROWS, tile), 0).astype(F32)
    tn = functools.partial(lax.dot_general, dimension_numbers=(((0,), (0,)), ((), ())),
                           preferred_element_type=F32)
    for t in range(sub):
        gmat = jnp.zeros((TILE_ROWS, tile), F32)
        for k in range(TOP_K):
            gmat = jnp.where(ridx == slot_ref[t, k:k + 1, :], slot_ref[t, TOP_K + k:TOP_K + k + 1, :], gmat)
        gb = gmat.astype(BF16)
        oh, ol = _unpack_rows(o_ref[pl.ds(t * TILE_ROWS, TILE_ROWS), :])
        y = jnp.concatenate([tn(gb, oh), tn(gb, ol)], axis=1)
        rows = pl.ds(t * tile, tile)
        gf = gf_ref[...] if gf_ref.shape[0] == 1 else gf_ref[rows, :]
        y_ref[rows, :] = x2_ref[rows, :] + gf * y


def _combine(outs, slots, x2, mod, *, tile, sub, rows_per_mod, tile0):
    n = x2.shape[0]
    mrows = mod.shape[2]
    assert n % (sub * tile) == 0 and tile0 % sub == 0 and (mrows == 1 or sub == 1)
    return pl.pallas_call(
        _combine_kernel,
        grid=(n // (sub * tile),),
        in_specs=[pl.BlockSpec((sub * TILE_ROWS, PACKED_W), lambda i: (i + tile0 // sub, 0)),
                  pl.BlockSpec((sub, 2 * TOP_K, tile), lambda i: (i, 0, 0)),
                  pl.BlockSpec((sub * tile, D_MODEL), lambda i: (i, 0)),
                  _mod_spec(mod, 5, lambda i: (i * sub * tile) // rows_per_mod)],
        out_specs=pl.BlockSpec((sub * tile, D_MODEL), lambda i: (i, 0)),
        out_shape=jax.ShapeDtypeStruct((n, D_MODEL), F32),
        compiler_params=_cparams(("parallel",)),
        name="combine",
    )(outs, slots, x2, mod)


def kernel(x_prompt, x_sample, cache_k, cache_v, state_pool, c_prompt, c_sample, rel_bias, norm_mix, w_ada,
           b_ada, w_in, q_norm, k_norm, sinks, w_pool, pool_scale, w_out, norm_ffn, w_router, b_router,
           w1, b1, w2, b2):
    depth = w_in.shape[0]
    assert depth == 1
    l = 0
    bp, sp, _ = x_prompt.shape
    bs = x_sample.shape[0]
    assert x_sample.shape[1] == 1 and sp % SORT_TILE == 0 and bs <= SORT_TILE
    n_p = bp * sp
    tiles_p = n_p // SORT_TILE
    max_chunks = tiles_p * (SORT_TILE * TOP_K // CHUNK + N_EXPERTS) + (bs * TOP_K // CHUNK + N_EXPERTS)
    n_blocks_max = -(-max_chunks // BLOCK_CHUNKS) + N_EXPERTS

    pr = -(-bp // SUBLANES) * SUBLANES
    c_all = jnp.concatenate([c_prompt, jnp.zeros((pr - bp, D_MODEL), F32), c_sample], axis=0)
    mod = _modulation(c_all, w_ada[l], b_ada[l])
    mod_p = mod[:, :bp].reshape(6, bp, 1, D_MODEL)
    mod_s = mod[:, pr:].reshape(6, 1, bs, D_MODEL)

    head = jnp.arange(ATTN_WIDTH) // HEAD_DIM
    bd = (head[:, None] == head[None, :]).astype(BF16)
    w_in_b = w_in[l].astype(BF16)
    w_out_b = w_out[l].astype(BF16)
    w_pool_b = w_pool[l].astype(BF16)
    w_router_t = w_router[l].T

    xp = x_prompt.reshape(n_p, D_MODEL)
    q, k, v, pool, u_tail = _mixer_inputs(xp, mod_p, norm_mix[l], w_in_b, bd, q_norm[l], k_norm[l],
                                          tile=1024, rows_per_mod=sp, precise=False,
                                          pool=(w_pool_b, pool_scale[l]))
    k3 = k.reshape(bp, sp, KV_WIDTH)
    v3 = v.reshape(bp, sp, KV_WIDTH)
    attn = _attn_prompt(q.reshape(bp, sp, ATTN_WIDTH), k3, v3, sinks[l], rel_bias)
    keep = min(WINDOW, sp)
    nkp = k3[:, -keep:].reshape(bp, keep, N_KV_HEADS, HEAD_DIM)
    nvp = v3[:, -keep:].reshape(bp, keep, N_KV_HEADS, HEAD_DIM)
    npp = u_tail[:, POOL_HALO - POOL_BUF:]
    x2_p, hs, slots_p, cnt_p = _route(
        xp, attn.reshape(n_p, ATTN_WIDTH), pool.reshape(n_p, POOL_WIDTH), mod_p,
        w_out_b, norm_ffn[l], w_router_t, b_router[l], None,
        tile=SORT_TILE, sub=ROUTE_SUB, rows_per_mod=sp, tile0=0, extra_tiles=max(2, ROUTE_SUB))

    xs = x_sample.reshape(bs, D_MODEL)
    qs, ks, vs, us = _mixer_inputs(xs, mod_s, norm_mix[l], w_in[l], bd, q_norm[l], k_norm[l],
                                   tile=bs, rows_per_mod=bs, precise=True)
    wbuf = cache_k.shape[2]
    attn_s, nks, nvs = _attn_sample(qs, ks, vs, cache_k[l].reshape(bs, wbuf, KV_WIDTH),
                                    cache_v[l].reshape(bs, wbuf, KV_WIDTH), sinks[l], rel_bias)
    pool_s, nps_t = _pool_sample(jnp.swapaxes(state_pool[l], 0, 1), us, w_pool_b, pool_scale[l])
    x2_s, hs, slots_s, cnt_s = _route(
        xs, attn_s.astype(BF16), pool_s, mod_s, w_out_b, norm_ffn[l],
        w_router_t, b_router[l], hs, tile=bs, sub=1, rows_per_mod=bs, tile0=tiles_p, extra_tiles=0)

    cnt = jnp.concatenate([cnt_p[:, :, 0], cnt_s[:, :, 0]], axis=0).astype(jnp.int32)
    src, block_e, next_e, nblocks = _plan(cnt, n_blocks_max)
    outs = _moe(hs, src, block_e, next_e, nblocks, w1[l], b1[l], w2[l], b2[l],
                scratch_chunk=(tiles_p + 1) * TILE_CHUNKS)

    y_p = _combine(outs, slots_p, x2_p, mod_p, tile=SORT_TILE, sub=COMBINE_SUB, rows_per_mod=sp, tile0=0)
    y_s = _combine(outs, slots_s, x2_s, mod_s, tile=bs, sub=1, rows_per_mod=bs, tile0=tiles_p)

    return (y_p.reshape(bp, sp, D_MODEL), y_s.reshape(bs, 1, D_MODEL),
            nkp[None], nvp[None], npp[None],
            nks.reshape(1, bs, wbuf, N_KV_HEADS, HEAD_DIM), nvs.reshape(1, bs, wbuf, N_KV_HEADS, HEAD_DIM),
            jnp.swapaxes(nps_t, 0, 1)[None])
```

```python
import functools
import math

import jax
import jax.numpy as jnp
from jax import lax
from jax.experimental import pallas as pl
from jax.experimental.pallas import tpu as pltpu

F32 = jnp.float32
BF16 = jnp.bfloat16

D_MODEL = 1024
HEAD_DIM = 64
N_HEADS = 8
N_KV_HEADS = 2
GROUP = N_HEADS // N_KV_HEADS
ATTN_WIDTH = N_HEADS * HEAD_DIM
KV_WIDTH = N_KV_HEADS * HEAD_DIM
POOL_WIDTH = D_MODEL - ATTN_WIDTH
POOL_WINDOWS = (2, 4, 8, 16)
POOL_GROUP = POOL_WIDTH // len(POOL_WINDOWS)
POOL_BUF = max(POOL_WINDOWS) - 1
IN_WIDTH = ATTN_WIDTH + 2 * KV_WIDTH + POOL_WIDTH
WINDOW = 128
ATTN_BLOCK = 128
N_BUCKETS = 32
MAX_EXACT = 16
REL_MAX_DIST = 128
N_EXPERTS = 32
TOP_K = 4
D_FF = D_MODEL
SWIGLU_LIMIT = 7.0
SWIGLU_ALPHA = 1.702
EPS = 1e-6
NEG_INF = -1e30
PAST_LEN = 16384

LANES = 128
SUBLANES = 8
VMEM_LIMIT = 56 * 1024 * 1024

ATTN_QB = 8
POOL_HALO = 2 * SUBLANES

SORT_TILE = 256
ROUTE_SUB = 4
COMBINE_SUB = 4
CHUNK = SUBLANES
TILE_ROWS = -(-(SORT_TILE * TOP_K + N_EXPERTS * (CHUNK - 1)) // LANES) * LANES
TILE_CHUNKS = TILE_ROWS // CHUNK
MOE_BLOCK = 256
BLOCK_CHUNKS = MOE_BLOCK // CHUNK
WEIGHT_DMA_PRIORITY = 1
PACKED_W = D_MODEL // 2
ROW_W = PACKED_W
ZERO_CHUNK = TILE_CHUNKS - 1


def _bdot(a, b):
    return jnp.dot(a.astype(BF16), b.astype(BF16), preferred_element_type=F32)


def _split(a):
    hi = a.astype(BF16)
    lo = (a - hi.astype(F32)).astype(BF16)
    return hi, lo


def _dot3(a, b):
    ah, al = _split(a)
    bh, bl = _split(b)
    d = functools.partial(jnp.dot, preferred_element_type=F32)
    return d(ah, bh) + d(al, bh) + d(ah, bl)


def _pack_rows(x):
    bits = lax.bitcast_convert_type(x, jnp.int32)
    return bits[:, :PACKED_W] | lax.shift_right_logical(bits[:, PACKED_W:], 16)


def _unpack_rows(w):
    hi = lax.bitcast_convert_type(w & jnp.int32(-65536), F32)
    lo = lax.bitcast_convert_type(lax.shift_left(w, 16), F32)
    return hi.astype(BF16), lo.astype(BF16)


def _rms(x, g):
    return x * lax.rsqrt(jnp.mean(x * x, axis=-1, keepdims=True) + EPS) * g


def _cparams(sem, **kw):
    return pltpu.CompilerParams(dimension_semantics=sem, vmem_limit_bytes=VMEM_LIMIT, **kw)


def _ada_kernel(c_ref, w_ref, b_ref, o_ref):
    c = c_ref[...]
    s = c * jax.nn.sigmoid(c)
    for v in range(o_ref.shape[0]):
        cols = slice(v * D_MODEL, (v + 1) * D_MODEL)
        o_ref[v] = _dot3(s, w_ref[:, cols]) + b_ref[:, cols]


def _modulation(c, w_ada, b_ada, *, vectors_per_step=1):
    rows = c.shape[0]
    n = w_ada.shape[1]
    nvec = n // D_MODEL
    assert nvec % vectors_per_step == 0
    tn = vectors_per_step * D_MODEL
    return pl.pallas_call(
        _ada_kernel,
        grid=(nvec // vectors_per_step,),
        in_specs=[pl.BlockSpec((rows, D_MODEL), lambda j: (0, 0)),
                  pl.BlockSpec((D_MODEL, tn), lambda j: (0, j)),
                  pl.BlockSpec((1, tn), lambda j: (0, j))],
        out_specs=pl.BlockSpec((vectors_per_step, rows, D_MODEL), lambda j: (j, 0, 0)),
        out_shape=jax.ShapeDtypeStruct((nvec, rows, D_MODEL), F32),
        compiler_params=_cparams(("parallel",)),
        name="modulation",
    )(c, w_ada, b_ada.reshape(1, n))


def _head_rms(t, bd, g, precise):
    if precise:
        hi, lo = _split(t * t)
        ss = jnp.dot(hi, bd, preferred_element_type=F32) + jnp.dot(lo, bd, preferred_element_type=F32)
    else:
        ss = _bdot(t * t, bd)
    return t * lax.rsqrt(ss * (1.0 / HEAD_DIM) + EPS) * g


def _mixin_qkv(x_ref, sh_ref, sc_ref, g_ref, w_ref, bd_ref, qn_ref, kn_ref, q_ref, k_ref, v_ref, precise):
    h = _rms(x_ref[...], g_ref[...]) * (1.0 + sc_ref[...]) + sh_ref[...]
    z = _dot3(h, w_ref[...]) if precise else _bdot(h, w_ref[...])
    q = z[:, :ATTN_WIDTH]
    k = z[:, ATTN_WIDTH:ATTN_WIDTH + KV_WIDTH]
    bd = bd_ref[...]
    q = _head_rms(q, bd, qn_ref[...], precise)
    k = _head_rms(k, bd[:KV_WIDTH, :KV_WIDTH], kn_ref[...], precise)
    q_ref[...] = (q * (HEAD_DIM ** -0.5)).astype(BF16)
    k_ref[...] = k
    v_ref[...] = z[:, ATTN_WIDTH + KV_WIDTH:ATTN_WIDTH + 2 * KV_WIDTH]
    return z[:, ATTN_WIDTH + 2 * KV_WIDTH:]


def _mixin_kernel(x_ref, sh_ref, sc_ref, g_ref, w_ref, bd_ref, qn_ref, kn_ref,
                  q_ref, k_ref, v_ref, u_ref, *, precise):
    u_ref[...] = _mixin_qkv(x_ref, sh_ref, sc_ref, g_ref, w_ref, bd_ref, qn_ref, kn_ref,
                            q_ref, k_ref, v_ref, precise)


def _mixin_pool_kernel(x_ref, sh_ref, sc_ref, g_ref, w_ref, bd_ref, qn_ref, kn_ref, wp_ref, ps_ref,
                       q_ref, k_ref, v_ref, pool_ref, tail_ref, ext, lv, carry, *, tiles_per_seq):
    u = _mixin_qkv(x_ref, sh_ref, sc_ref, g_ref, w_ref, bd_ref, qn_ref, kn_ref, q_ref, k_ref, v_ref, False)
    t = pl.program_id(0) % tiles_per_seq

    @pl.when(t == 0)
    def _():
        carry[...] = jnp.zeros(carry.shape, F32)

    pool_ref[...] = _pool_tile(u, carry[...], t * u.shape[0], wp_ref, ps_ref, ext, lv)
    last = u[u.shape[0] - carry.shape[0]:, :]
    carry[...] = last
    tail_ref[...] = last


def _mod_spec(mod, k, group_of_step):
    return pl.BlockSpec((None, None, mod.shape[2], D_MODEL), lambda i: (k, group_of_step(i), 0, 0))


def _mixer_inputs(x2d, mod, norm_mix, w_in, bd, q_norm, k_norm, *, tile, rows_per_mod, precise, pool=None):
    n = x2d.shape[0]
    group = lambda i: (i * tile) // rows_per_mod
    const = lambda shape: pl.BlockSpec(shape, lambda i: (0,) * len(shape))
    row = lambda w: pl.BlockSpec((tile, w), lambda i: (i, 0))
    in_specs = [row(D_MODEL), _mod_spec(mod, 0, group), _mod_spec(mod, 1, group), const((1, D_MODEL)),
                const((D_MODEL, IN_WIDTH)), const((ATTN_WIDTH, ATTN_WIDTH)), const((1, ATTN_WIDTH)),
                const((1, KV_WIDTH))]
    args = [x2d, mod, mod, norm_mix.reshape(1, D_MODEL), w_in, bd,
            jnp.tile(q_norm, N_HEADS).reshape(1, ATTN_WIDTH), jnp.tile(k_norm, N_KV_HEADS).reshape(1, KV_WIDTH)]
    out_specs = [row(ATTN_WIDTH), row(KV_WIDTH), row(KV_WIDTH)]
    out_shape = [jax.ShapeDtypeStruct((n, ATTN_WIDTH), BF16),
                 jax.ShapeDtypeStruct((n, KV_WIDTH), F32),
                 jax.ShapeDtypeStruct((n, KV_WIDTH), F32)]
    if pool is None:
        kern = functools.partial(_mixin_kernel, precise=precise)
        out_specs.append(row(POOL_WIDTH))
        out_shape.append(jax.ShapeDtypeStruct((n, POOL_WIDTH), F32))
        scratch, semantics = [], ("parallel",)
    else:
        assert not precise and rows_per_mod % tile == 0 and tile >= POOL_HALO
        w_pool, pool_scale = pool
        kern = functools.partial(_mixin_pool_kernel, tiles_per_seq=rows_per_mod // tile)
        in_specs += [const(w_pool.shape), const((1, POOL_WIDTH))]
        args += [w_pool, pool_scale.reshape(1, POOL_WIDTH)]
        out_specs += [pl.BlockSpec((tile, POOL_WIDTH), lambda i: (i, 0)),
                      pl.BlockSpec((None, POOL_HALO, POOL_WIDTH), lambda i: (group(i), 0, 0))]
        out_shape += [jax.ShapeDtypeStruct((n, POOL_WIDTH), BF16),
                      jax.ShapeDtypeStruct((n // rows_per_mod, POOL_HALO, POOL_WIDTH), F32)]
        ext_rows = SUBLANES + POOL_HALO + tile
        scratch = [pltpu.VMEM((ext_rows, POOL_WIDTH), F32),
                   pltpu.VMEM((len(POOL_WINDOWS) - 1, ext_rows, POOL_GROUP), F32),
                   pltpu.VMEM((POOL_HALO, POOL_WIDTH), F32)]
        semantics = ("arbitrary",)
    return pl.pallas_call(
        kern,
        grid=(n // tile,),
        in_specs=in_specs,
        out_specs=out_specs,
        out_shape=out_shape,
        scratch_shapes=scratch,
        compiler_params=_cparams(semantics),
        name="mixer_inputs",
    )(*args)


def _t5_bucket(rel):
    n = jnp.maximum(rel, 0)
    nf = jnp.maximum(n, 1).astype(F32)
    large = MAX_EXACT + (jnp.log(nf / MAX_EXACT) / math.log(REL_MAX_DIST / MAX_EXACT)
                         * (N_BUCKETS - MAX_EXACT)).astype(jnp.int32)
    large = jnp.minimum(large, N_BUCKETS - 1)
    return jnp.where(n < MAX_EXACT, n, large)


def _bias_table(rel, rel_table):
    bucket = _t5_bucket(rel)
    table = rel_table.astype(F32)
    ids = jnp.arange(N_BUCKETS, dtype=bucket.dtype).reshape((N_BUCKETS, 1) + (1,) * rel.ndim)
    onehot = bucket[None, None] == ids
    bias = jnp.sum(jnp.where(onehot, table.reshape(table.shape + (1,) * rel.ndim), 0.0), axis=0)
    valid = (rel >= 0) & (rel < WINDOW)
    return jnp.where(valid[None], bias, NEG_INF)


def _attn_prompt_kernel(sink_ref, q_ref, kp_ref, kc_ref, vp_ref, vc_ref, bias_ref, o_ref):
    first = pl.program_id(1) == 0
    kk = jnp.concatenate([kp_ref[...], kc_ref[...]], axis=0).astype(BF16)
    vv = jnp.concatenate([vp_ref[...], vc_ref[...]], axis=0).astype(BF16)
    key = lax.broadcasted_iota(jnp.int32, (2 * ATTN_BLOCK, 1), 0)
    no_prev = jnp.logical_and(first, key < ATTN_BLOCK)
    lane = lax.broadcasted_iota(jnp.int32, (1, N_HEADS * ATTN_BLOCK), 1)
    sink = jnp.zeros((1, N_HEADS * ATTN_BLOCK), F32)
    for h in range(N_HEADS):
        sink = jnp.where(lane // ATTN_BLOCK == h, sink_ref[h], sink)
    contract = lambda a, b, dims: lax.dot_general(a, b, (dims, ((), ())), preferred_element_type=F32)
    part = GROUP * ATTN_BLOCK
    for i in range(ATTN_QB):
        q = q_ref[i * ATTN_BLOCK:(i + 1) * ATTN_BLOCK, :]
        keys = slice(i * ATTN_BLOCK, (i + 2) * ATTN_BLOCK)
        scores = []
        for kv in range(N_KV_HEADS):
            heads = range(kv * GROUP, (kv + 1) * GROUP)
            qg = jnp.concatenate([q[:, h * HEAD_DIM:(h + 1) * HEAD_DIM] for h in heads], axis=0)
            scores.append(contract(kk[keys, kv * HEAD_DIM:(kv + 1) * HEAD_DIM], qg, ((1,), (1,))))
        s = jnp.concatenate(scores, axis=1) + bias_ref[...]
        if i == 0:
            s = jnp.where(no_prev, NEG_INF, s)
        m = jnp.maximum(jnp.max(s, axis=0, keepdims=True), sink)
        p = jnp.exp(s - m)
        denom = jnp.sum(p, axis=0, keepdims=True) + jnp.exp(sink - m)
        p = p.astype(BF16)
        halves = [contract(vv[keys, kv * HEAD_DIM:(kv + 1) * HEAD_DIM], p[:, kv * part:(kv + 1) * part],
                           ((0,), (0,))) / denom[:, kv * part:(kv + 1) * part]
                  for kv in range(N_KV_HEADS)]
        o_t = jnp.concatenate(halves, axis=0)
        per_g = [o_t[:, g * ATTN_BLOCK:(g + 1) * ATTN_BLOCK].T for g in range(GROUP)]
        out = [t[:, kv * HEAD_DIM:(kv + 1) * HEAD_DIM] for kv in range(N_KV_HEADS) for t in per_g]
        o_ref[i * ATTN_BLOCK:(i + 1) * ATTN_BLOCK, :] = jnp.concatenate(out, axis=-1).astype(BF16)


def _attn_prompt(q, k, v, sinks, rel_table):
    b, s = q.shape[:2]
    qrows = ATTN_QB * ATTN_BLOCK
    assert s % qrows == 0
    qi = jnp.arange(ATTN_BLOCK, dtype=jnp.int32)[:, None]
    si = jnp.arange(2 * ATTN_BLOCK, dtype=jnp.int32)[None, :]
    bias = _bias_table(qi + ATTN_BLOCK - si, rel_table)
    bias = bias.reshape(N_HEADS * ATTN_BLOCK, 2 * ATTN_BLOCK).T
    cur = lambda w: pl.BlockSpec((None, qrows, w), lambda i, j, *_: (i, j, 0))
    prev = lambda w: pl.BlockSpec((None, ATTN_BLOCK, w),
                                  lambda i, j, *_: (i, jnp.maximum(j * ATTN_QB - 1, 0), 0))
    return pl.pallas_call(
        _attn_prompt_kernel,
        grid_spec=pltpu.PrefetchScalarGridSpec(
            num_scalar_prefetch=1,
            grid=(b, s // qrows),
            in_specs=[cur(ATTN_WIDTH), prev(KV_WIDTH), cur(KV_WIDTH), prev(KV_WIDTH), cur(KV_WIDTH),
                      pl.BlockSpec(bias.shape, lambda i, j, *_: (0, 0))],
            out_specs=cur(ATTN_WIDTH)),
        out_shape=jax.ShapeDtypeStruct((b, s, ATTN_WIDTH), BF16),
        compiler_params=_cparams(("parallel", "parallel")),
        name="attn_prompt",
    )(sinks.astype(F32), q, k, k, v, v, bias)


def _attn_sample_kernel(sink_ref, q_ref, kc_ref, vc_ref, kn_ref, vn_ref, bias_ref, bnew_ref,
                        o_ref, nk_ref, nv_ref):
    kc = kc_ref[...]
    vc = vc_ref[...]
    kn = kn_ref[...]
    vn = vn_ref[...]
    w = kc.shape[1]
    pos = lax.broadcasted_iota(jnp.int32, kc.shape, 1)
    nk_ref[...] = jnp.where(pos == w - 1, kn[:, None, :], pltpu.roll(kc, w - 1, 1))
    nv_ref[...] = jnp.where(pos == w - 1, vn[:, None, :], pltpu.roll(vc, w - 1, 1))
    gi = lax.broadcasted_iota(jnp.int32, (1, GROUP, 1), 1)
    for kv in range(N_KV_HEADS):
        sl = slice(kv * HEAD_DIM, (kv + 1) * HEAD_DIM)
        qg = q_ref[:, kv]
        s = jnp.einsum('bgd,bsd->bgs', qg, kc[:, :, sl].astype(BF16), preferred_element_type=F32)
        s = s + bias_ref[kv][None]
        s_new = jnp.sum(qg.astype(F32) * kn[:, None, sl], axis=-1, keepdims=True) + bnew_ref[kv][None]
        sink = jnp.zeros((1, GROUP, 1), F32)
        for g in range(GROUP):
            sink = jnp.where(gi == g, sink_ref[kv * GROUP + g], sink)
        m = jnp.maximum(jnp.maximum(jnp.max(s, axis=-1, keepdims=True), s_new), sink)
        p = jnp.exp(s - m)
        p_new = jnp.exp(s_new - m)
        denom = jnp.sum(p, axis=-1, keepdims=True) + p_new + jnp.exp(sink - m)
        o = jnp.einsum('bgs,bsd->bgd', p.astype(BF16), vc[:, :, sl].astype(BF16), preferred_element_type=F32)
        o = o + p_new * vn[:, None, sl]
        o_ref[:, kv] = o / denom


def _attn_sample(q, k_new, v_new, cache_k, cache_v, sinks, rel_table, *, tile=32):
    bd, w = cache_k.shape[:2]
    rel = w - jnp.arange(w, dtype=jnp.int32)
    bias = _bias_table(rel, rel_table).reshape(N_KV_HEADS, GROUP, w)
    bnew = _bias_table(jnp.zeros((1,), jnp.int32), rel_table).reshape(N_KV_HEADS, GROUP, 1)
    q4 = q.reshape(bd, N_KV_HEADS, GROUP, HEAD_DIM)
    spec4 = pl.BlockSpec((tile, N_KV_HEADS, GROUP, HEAD_DIM), lambda i, *_: (i, 0, 0, 0))
    cache = pl.BlockSpec((tile, w, KV_WIDTH), lambda i, *_: (i, 0, 0))
    new = pl.BlockSpec((tile, KV_WIDTH), lambda i, *_: (i, 0))
    const3 = lambda a: pl.BlockSpec(a.shape, lambda i, *_: (0, 0, 0))
    o, nk, nv = pl.pallas_call(
        _attn_sample_kernel,
        grid_spec=pltpu.PrefetchScalarGridSpec(
            num_scalar_prefetch=1,
            grid=(bd // tile,),
            in_specs=[spec4, cache, cache, new, new, const3(bias), const3(bnew)],
            out_specs=[spec4, cache, cache]),
        out_shape=[jax.ShapeDtypeStruct(q4.shape, F32),
                   jax.ShapeDtypeStruct(cache_k.shape, F32),
                   jax.ShapeDtypeStruct(cache_v.shape, F32)],
        compiler_params=_cparams(("parallel",)),
        name="attn_sample",
    )(sinks.astype(F32), q4, cache_k, cache_v, k_new, v_new, bias, bnew)
    return o.reshape(bd, ATTN_WIDTH), nk, nv


def _pool_project(d_groups, wp_ref, ps_ref):
    out = [_bdot(d, wp_ref[g]) for g, d in enumerate(d_groups)]
    return (jnp.concatenate(out, axis=-1) * ps_ref[...]).astype(BF16)


def _pool_tile(u, halo, pos0, wp_ref, ps_ref, ext, lv):
    tile = u.shape[0]
    lead, hb = SUBLANES, POOL_HALO
    ext[0:lead, :] = jnp.zeros((lead, ext.shape[1]), F32)
    ext[lead:lead + hb, :] = halo
    ext[lead + hb:, :] = u
    lv[:, 0:lead, :] = jnp.zeros((lv.shape[0], lead, lv.shape[2]), F32)
    pos = pos0 + lax.broadcasted_iota(jnp.int32, (tile, 1), 0)
    n = hb + tile
    ds = []
    for g, w in enumerate(POOL_WINDOWS):
        sl = slice(g * POOL_GROUP, (g + 1) * POOL_GROUP)
        acc = ext[lead:lead + n, sl] + ext[lead - 1:lead - 1 + n, sl]
        span, level = 2, 0
        while span < w:
            lv[level, lead:lead + n, :] = acc
            acc = acc + lv[level, lead - span:lead - span + n, :]
            span, level = 2 * span, level + 1
        cnt = jnp.minimum(pos + 1, w).astype(F32)
        ds.append(acc[hb:] / cnt - ext[lead + hb:lead + hb + tile, sl])
    return _pool_project(ds, wp_ref, ps_ref)


def _pool_sample_kernel(st_ref, u_ref, wp_ref, ps_ref, o_ref, ns_ref):
    u = u_ref[...]
    ns_ref[0:POOL_BUF - 1] = st_ref[1:POOL_BUF]
    ns_ref[POOL_BUF - 1] = u
    ds = []
    for g, w in enumerate(POOL_WINDOWS):
        sl = slice(g * POOL_GROUP, (g + 1) * POOL_GROUP)
        acc = u[:, sl]
        for j in range(1, w):
            acc = acc + st_ref[POOL_BUF - j][:, sl]
        cnt = float(min(PAST_LEN + 1, w))
        ds.append(acc / cnt - u[:, sl])
    o_ref[...] = _pool_project(ds, wp_ref, ps_ref)


def _pool_sample(state_t, u, w_pool, pool_scale):
    nb, bd, c = state_t.shape
    full = lambda a: pl.BlockSpec(a.shape, lambda: (0,) * a.ndim)
    ps = pool_scale.reshape(1, c)
    return pl.pallas_call(
        _pool_sample_kernel,
        in_specs=[full(state_t), full(u), full(w_pool), full(ps)],
        out_specs=[pl.BlockSpec((bd, c), lambda: (0, 0)), full(state_t)],
        out_shape=[jax.ShapeDtypeStruct((bd, c), BF16), jax.ShapeDtypeStruct(state_t.shape, F32)],
        compiler_params=pltpu.CompilerParams(vmem_limit_bytes=VMEM_LIMIT),
        name="pool_sample",
    )(state_t, u, w_pool, ps)


def _route_kernel(*refs, own_steps):
    hs_ref = refs[-3]

    @pl.when(pl.program_id(0) < own_steps)
    def _():
        _route_tiles(*refs)

    @pl.when(pl.program_id(0) >= own_steps)
    def _():
        hs_ref[...] = jnp.zeros(hs_ref.shape, jnp.int32)


def _route_tiles(x_ref, attn_ref, pool_ref, gm_ref, sh_ref, sc_ref, wo_ref, nf_ref, wr_ref, br_ref,
                 tri_t_ref, tri_e_ref,
                 x2_ref, hs_ref, slot_ref, cnt_ref):
    sub = slot_ref.shape[0]
    n = x_ref.shape[0]
    tile = n // sub
    mixed = jnp.concatenate([attn_ref[...], pool_ref[...]], axis=1)
    mix = jnp.dot(mixed, wo_ref[...], preferred_element_type=F32)
    x2 = x_ref[...] + gm_ref[...] * mix
    x2_ref[...] = x2
    h = _rms(x2, nf_ref[...]) * (1.0 + sc_ref[...]) + sh_ref[...]

    hh, hl = _split(h)
    wh, wl = _split(wr_ref[...])
    nt = functools.partial(lax.dot_general, dimension_numbers=(((1,), (1,)), ((), ())),
                           preferred_element_type=F32)
    logits = nt(wh, hh) + nt(wl, hh) + nt(wh, hl) + br_ref[...]

    eidx = lax.broadcasted_iota(jnp.int32, (N_EXPERTS, n), 0).astype(F32)
    work = logits
    tops, picks = [], []
    for _ in range(TOP_K):
        m = jnp.max(work, axis=0, keepdims=True)
        pick = jnp.min(jnp.where(work == m, eidx, float(N_EXPERTS)), axis=0, keepdims=True)
        work = jnp.where(eidx == pick, -jnp.inf, work)
        tops.append(m)
        picks.append(pick)
    ex = [jnp.exp(v - tops[0]) for v in tops]
    den = ex[0] + ex[1] + ex[2] + ex[3]
    gates = [e / den for e in ex]

    sel = jnp.zeros((N_EXPERTS, n), F32)
    for pick in picks:
        sel = sel + (eidx == pick).astype(F32)
    selb = sel.astype(BF16)
    rank = jnp.concatenate([jnp.dot(selb[:, t * tile:(t + 1) * tile], tri_t_ref[...], preferred_element_type=F32)
                            for t in range(sub)], axis=1)
    cnts = [jnp.sum(sel[:, t * tile:(t + 1) * tile], axis=1, keepdims=True) for t in range(sub)]
    padded = jnp.concatenate(
        [jnp.broadcast_to(jnp.ceil(c * (1.0 / CHUNK)) * CHUNK, (N_EXPERTS, LANES)) for c in cnts], axis=1)
    seg = jnp.dot(tri_e_ref[...], padded.astype(BF16), preferred_element_type=F32)
    dest = jnp.concatenate([seg[:, t * LANES:t * LANES + 1] + rank[:, t * tile:(t + 1) * tile]
                            for t in range(sub)], axis=1)
    slots = [jnp.sum(jnp.where(eidx == pick, dest, 0.0), axis=0, keepdims=True) for pick in picks]

    ridx = lax.broadcasted_iota(jnp.int32, (TILE_ROWS, tile), 0).astype(jnp.int16)
    slots16 = [v.astype(jnp.int32).astype(jnp.int16) for v in slots]
    for t in range(sub):
        cols = slice(t * tile, (t + 1) * tile)
        cnt_ref[t] = jnp.broadcast_to(cnts[t], (N_EXPERTS, LANES))
        slot_ref[t] = jnp.concatenate([v[:, cols] for v in slots + gates], axis=0)
        hit = ridx == slots16[0][:, cols]
        for s in slots16[1:]:
            hit = jnp.logical_or(hit, ridx == s[:, cols])
        perm = jnp.where(hit, jnp.ones((), BF16), jnp.zeros((), BF16))
        hs_ref[pl.ds(t * TILE_ROWS, TILE_ROWS), :] = _pack_rows(
            jnp.dot(perm, hh[t * tile:(t + 1) * tile, :], preferred_element_type=F32))


def _route(x2d, attn, pool, mod, w_out, norm_ffn, w_router_t, b_router, hs_prev, *, tile, sub,
           rows_per_mod, tile0, extra_tiles):
    n = x2d.shape[0]
    nt = n // tile
    assert nt % sub == 0 and extra_tiles % sub == 0
    own_steps = nt // sub
    steps = own_steps + extra_tiles // sub
    mrows = mod.shape[2]
    assert mrows == 1 or (sub == 1 and mrows == tile)
    last = lambda i: jnp.minimum(i, own_steps - 1)
    group = lambda i: (last(i) * sub * tile) // rows_per_mod
    const = lambda shape: pl.BlockSpec(shape, lambda i: (0,) * len(shape))
    row = lambda w: pl.BlockSpec((sub * tile, w), lambda i: (last(i), 0))
    tri_t = (jnp.arange(tile)[:, None] < jnp.arange(tile)[None, :]).astype(BF16)
    tri_e = (jnp.arange(N_EXPERTS)[None, :] < jnp.arange(N_EXPERTS)[:, None]).astype(BF16)
    in_specs = [row(D_MODEL), row(ATTN_WIDTH), row(POOL_WIDTH),
                _mod_spec(mod, 2, group), _mod_spec(mod, 3, group), _mod_spec(mod, 4, group),
                const((D_MODEL, D_MODEL)), const((1, D_MODEL)), const((N_EXPERTS, D_MODEL)),
                const((N_EXPERTS, 1)), const((tile, tile)), const((N_EXPERTS, N_EXPERTS))]
    args = [x2d, attn, pool, mod, mod, mod, w_out, norm_ffn.reshape(1, D_MODEL), w_router_t,
            b_router.reshape(N_EXPERTS, 1), tri_t, tri_e]
    n_in = len(args)
    kern = functools.partial(_route_kernel, own_steps=own_steps)
    aliases = {}
    hs_rows = (tile0 + steps * sub) * TILE_ROWS
    assert tile0 % sub == 0
    if hs_prev is not None:
        in_specs.append(pl.BlockSpec(memory_space=pl.ANY))
        args.append(hs_prev)
        aliases = {n_in: 1}
        kern = lambda *refs: _route_kernel(*refs[:n_in], *refs[n_in + 1:], own_steps=own_steps)
        hs_rows = hs_prev.shape[0]
    return pl.pallas_call(
        kern,
        grid=(steps,),
        in_specs=in_specs,
        out_specs=[row(D_MODEL),
                   pl.BlockSpec((sub * TILE_ROWS, ROW_W), lambda i: (i + tile0 // sub, 0)),
                   pl.BlockSpec((sub, 2 * TOP_K, tile), lambda i: (last(i), 0, 0)),
                   pl.BlockSpec((sub, N_EXPERTS, LANES), lambda i: (last(i), 0, 0))],
        out_shape=[jax.ShapeDtypeStruct((n, D_MODEL), F32),
                   jax.ShapeDtypeStruct((hs_rows, ROW_W), jnp.int32),
                   jax.ShapeDtypeStruct((nt, 2 * TOP_K, tile), F32),
                   jax.ShapeDtypeStruct((nt, N_EXPERTS, LANES), F32)],
        input_output_aliases=aliases,
        compiler_params=_cparams(("arbitrary",)),
        name="route",
    )(*args)


def _moe_kernel(src_ref, be_ref, nxt_ref, nb_ref, hs_hbm, w1_hbm, b1_ref, w2_hbm, b2_ref, out_hbm,
                lhs, obuf, w1s, w2s, w1c, w2c, sem_in, sem_out, sem_w, *, scratch_chunk):
    nb = nb_ref[0]

    def weight_copies(e):
        return (pltpu.make_async_copy(w1_hbm.at[e], w1s, sem_w.at[0]),
                pltpu.make_async_copy(w2_hbm.at[e], w2s, sem_w.at[1]))

    def chunk_rows(c):
        return pl.ds(pl.multiple_of(c * CHUNK, CHUNK), CHUNK)

    def start_in(blk, s):
        for j in range(BLOCK_CHUNKS):
            c = src_ref[blk * BLOCK_CHUNKS + j]
            c = jnp.where(c < 0, ZERO_CHUNK, c)
            pltpu.make_async_copy(hs_hbm.at[chunk_rows(c)], lhs.at[s, pl.ds(j * CHUNK, CHUNK)],
                                  sem_in.at[s]).start()

    def wait_in(s):
        pltpu.make_async_copy(hs_hbm.at[pl.ds(0, MOE_BLOCK)], lhs.at[s], sem_in.at[s]).wait()

    def start_out(blk, s):
        for j in range(BLOCK_CHUNKS):
            c = src_ref[blk * BLOCK_CHUNKS + j]
            c = jnp.where(c < 0, scratch_chunk + s * BLOCK_CHUNKS + j, c)
            pltpu.make_async_copy(obuf.at[s, pl.ds(j * CHUNK, CHUNK)],
                                  out_hbm.at[chunk_rows(c), pl.ds(0, PACKED_W)], sem_out.at[s]).start()

    def wait_out(s):
        pltpu.make_async_copy(obuf.at[s], out_hbm.at[pl.ds(0, MOE_BLOCK), pl.ds(0, PACKED_W)],
                              sem_out.at[s]).wait()

    @pl.when(nb > 0)
    def _():
        start_in(0, 0)
        for cp in weight_copies(be_ref[0]):
            cp.start(priority=WEIGHT_DMA_PRIORITY)

    def block(b, carry):
        slot = b % 2
        e = be_ref[b]

        @pl.when(jnp.logical_or(b == 0, e != be_ref[jnp.maximum(b - 1, 0)]))
        def _():
            for cp in weight_copies(e):
                cp.wait()
            w1c[...] = w1s[...].astype(BF16)
            w2c[...] = w2s[...].astype(BF16)
            nxt = nxt_ref[b]

            @pl.when(nxt >= 0)
            def _():
                for cp in weight_copies(nxt):
                    cp.start(priority=WEIGHT_DMA_PRIORITY)

        wait_in(slot)

        @pl.when(b >= 2)
        def _():
            wait_out(slot)

        start_in(b + 1, 1 - slot)

        def ffn(nrows):
            xh, xl = _unpack_rows(lhs[slot, 0:nrows, :])
            x = jnp.concatenate([xh, xl], axis=1)
            gu = jnp.dot(x, w1c[...], preferred_element_type=F32) + b1_ref[e]
            gate = jnp.minimum(gu[:, :D_FF], SWIGLU_LIMIT)
            up = jnp.clip(gu[:, D_FF:], -SWIGLU_LIMIT, SWIGLU_LIMIT)
            act = (up + 1.0) * (gate * jax.nn.sigmoid(SWIGLU_ALPHA * gate))
            y = jnp.dot(act.astype(BF16), w2c[...], preferred_element_type=F32) + b2_ref[e]
            obuf[slot, 0:nrows, :] = _pack_rows(y.astype(BF16).astype(F32))

        real = nb_ref[1 + b]

        @pl.when(real > BLOCK_CHUNKS // 2)
        def _():
            ffn(MOE_BLOCK)

        @pl.when(real <= BLOCK_CHUNKS // 2)
        def _():
            ffn(MOE_BLOCK // 2)

        start_out(b, slot)
        return carry

    obuf[...] = jnp.zeros(obuf.shape, jnp.int32)
    lax.fori_loop(0, nb, block, 0)

    @pl.when(nb > 0)
    def _():
        last_slot = (nb - 1) % 2
        wait_in(1 - last_slot)

        @pl.when(nb >= 2)
        def _():
            wait_out(1 - last_slot)
        wait_out(last_slot)


def _moe(hs, src, block_e, next_e, nblocks, w1, b1, w2, b2, scratch_chunk):
    full = lambda shape: pl.BlockSpec(shape, lambda i, *_: (0,) * len(shape))
    hbm = pl.BlockSpec(memory_space=pl.ANY)
    return pl.pallas_call(
        functools.partial(_moe_kernel, scratch_chunk=scratch_chunk),
        grid_spec=pltpu.PrefetchScalarGridSpec(
            num_scalar_prefetch=4,
            grid=(1,),
            in_specs=[hbm, hbm, full((N_EXPERTS, 1, 2 * D_FF)), hbm, full((N_EXPERTS, 1, D_MODEL))],
            out_specs=hbm,
            scratch_shapes=[pltpu.VMEM((2, MOE_BLOCK, ROW_W), jnp.int32),
                            pltpu.VMEM((2, MOE_BLOCK, PACKED_W), jnp.int32),
                            pltpu.VMEM((D_MODEL, 2 * D_FF), F32),
                            pltpu.VMEM((D_FF, D_MODEL), F32),
                            pltpu.VMEM((D_MODEL, 2 * D_FF), BF16),
                            pltpu.VMEM((D_FF, D_MODEL), BF16),
                            pltpu.SemaphoreType.DMA((2,)),
                            pltpu.SemaphoreType.DMA((2,)),
                            pltpu.SemaphoreType.DMA((2,))]),
        out_shape=jax.ShapeDtypeStruct(hs.shape, jnp.int32),
        input_output_aliases={4: 0},
        compiler_params=_cparams(("arbitrary",)),
        name="moe_experts",
    )(src, block_e, next_e, nblocks, hs, w1, b1.reshape(N_EXPERTS, 1, 2 * D_FF), w2,
      b2.reshape(N_EXPERTS, 1, D_MODEL))


def _plan(cnt, n_blocks_max):
    nt = cnt.shape[0]
    nch = (cnt + (CHUNK - 1)) // CHUNK
    lstart = jnp.cumsum(nch, axis=1) - nch
    ne = jnp.sum(nch, axis=0)
    nbe = (ne + (BLOCK_CHUNKS - 1)) // BLOCK_CHUNKS
    bend = jnp.cumsum(nbe)
    nblocks = bend[-1]
    gstart = (bend - nbe)[None, :] * BLOCK_CHUNKS + (jnp.cumsum(nch, axis=0) - nch)
    s0 = jnp.arange(nt, dtype=jnp.int32)[:, None] * TILE_CHUNKS + lstart
    blk = jnp.arange(n_blocks_max + 1, dtype=jnp.int32)
    be = jnp.sum((blk[:, None] >= bend[None, :]).astype(jnp.int32), axis=1)
    be_last = jnp.sum((nblocks - 1 >= bend).astype(jnp.int32))
    be = jnp.minimum(be, be_last).astype(jnp.int32)
    eid = jnp.arange(N_EXPERTS, dtype=jnp.int32)
    strips = jnp.stack([gstart.T, nch.T, s0.T])
    mine = (be[:, None] == eid[None, :])[None, :, :, None]
    gs_b, nc_b, s0_b = jnp.sum(jnp.where(mine, strips[:, None], 0), axis=2)
    c = (blk[:, None] * BLOCK_CHUNKS + jnp.arange(BLOCK_CHUNKS, dtype=jnp.int32)[None, :])[:, :, None]
    inside = jnp.logical_and(c >= gs_b[:, None, :], c < (gs_b + nc_b)[:, None, :])
    src = (jnp.sum(jnp.where(inside, (s0_b - gs_b)[:, None, :] + c + 1, 0), axis=2) - 1).reshape(-1)
    be = be[:n_blocks_max]
    later = jnp.logical_and(eid[None, :] > be[:, None], (nbe > 0)[None, :])
    nxt = jnp.min(jnp.where(later, eid[None, :], N_EXPERTS), axis=1)
    nxt = jnp.where(nxt == N_EXPERTS, -1, nxt).astype(jnp.int32)
    real = jnp.sum((src.reshape(-1, BLOCK_CHUNKS) >= 0).astype(jnp.int32), axis=1)
    return src, be, nxt, jnp.concatenate([nblocks.reshape(1), real]).astype(jnp.int32)


def _combine_kernel(o_ref, slot_ref, x2_ref, gf_ref, y_ref):
    sub = slot_ref.shape[0]
    tile = x2_ref.shape[0] // sub
    ridx = lax.broadcasted_iota(jnp.int32, (TILE_ROWS, tile), 0).astype(F32)
    tn = functools.partial(lax.dot_general, dimension_numbers=(((0,), (0,)), ((), ())),
                           preferred_element_type=F32)
    for t in range(sub):
        gmat = jnp.zeros((TILE_ROWS, tile), F32)
        for k in range(TOP_K):
            gmat = jnp.where(ridx == slot_ref[t, k:k + 1, :], slot_ref[t, TOP_K + k:TOP_K + k + 1, :], gmat)
        gb = gmat.astype(BF16)
        oh, ol = _unpack_rows(o_ref[pl.ds(t * TILE_ROWS, TILE_ROWS), :])
        y = jnp.concatenate([tn(gb, oh), tn(gb, ol)], axis=1)
        rows = pl.ds(t * tile, tile)
        gf = gf_ref[...] if gf_ref.shape[0] == 1 else gf_ref[rows, :]
        y_ref[rows, :] = x2_ref[rows, :] + gf * y


def _combine(outs, slots, x2, mod, *, tile, sub, rows_per_mod, tile0):
    n = x2.shape[0]
    mrows = mod.shape[2]
    assert n % (sub * tile) == 0 and tile0 % sub == 0 and (mrows == 1 or sub == 1)
    return pl.pallas_call(
        _combine_kernel,
        grid=(n // (sub * tile),),
        in_specs=[pl.BlockSpec((sub * TILE_ROWS, PACKED_W), lambda i: (i + tile0 // sub, 0)),
                  pl.BlockSpec((sub, 2 * TOP_K, tile), lambda i: (i, 0, 0)),
                  pl.BlockSpec((sub * tile, D_MODEL), lambda i: (i, 0)),
                  _mod_spec(mod, 5, lambda i: (i * sub * tile) // rows_per_mod)],
        out_specs=pl.BlockSpec((sub * tile, D_MODEL), lambda i: (i, 0)),
        out_shape=jax.ShapeDtypeStruct((n, D_MODEL), F32),
        compiler_params=_cparams(("parallel",)),
        name="combine",
    )(outs, slots, x2, mod)


def kernel(x_prompt, x_sample, cache_k, cache_v, state_pool, c_prompt, c_sample, rel_bias, norm_mix, w_ada,
           b_ada, w_in, q_norm, k_norm, sinks, w_pool, pool_scale, w_out, norm_ffn, w_router, b_router,
           w1, b1, w2, b2):
    depth = w_in.shape[0]
    assert depth == 1
    l = 0
    bp, sp, _ = x_prompt.shape
    bs = x_sample.shape[0]
    assert x_sample.shape[1] == 1 and sp % SORT_TILE == 0 and bs <= SORT_TILE
    n_p = bp * sp
    tiles_p = n_p // SORT_TILE
    max_chunks = tiles_p * (SORT_TILE * TOP_K // CHUNK + N_EXPERTS) + (bs * TOP_K // CHUNK + N_EXPERTS)
    n_blocks_max = -(-max_chunks // BLOCK_CHUNKS) + N_EXPERTS

    pr = -(-bp // SUBLANES) * SUBLANES
    c_all = jnp.concatenate([c_prompt, jnp.zeros((pr - bp, D_MODEL), F32), c_sample], axis=0)
    mod = _modulation(c_all, w_ada[l], b_ada[l])
    mod_p = mod[:, :bp].reshape(6, bp, 1, D_MODEL)
    mod_s = mod[:, pr:].reshape(6, 1, bs, D_MODEL)

    head = jnp.arange(ATTN_WIDTH) // HEAD_DIM
    bd = (head[:, None] == head[None, :]).astype(BF16)
    w_in_b = w_in[l].astype(BF16)
    w_out_b = w_out[l].astype(BF16)
    w_pool_b = w_pool[l].astype(BF16)
    w_router_t = w_router[l].T

    xp = x_prompt.reshape(n_p, D_MODEL)
    q, k, v, pool, u_tail = _mixer_inputs(xp, mod_p, norm_mix[l], w_in_b, bd, q_norm[l], k_norm[l],
                                          tile=1024, rows_per_mod=sp, precise=False,
                                          pool=(w_pool_b, pool_scale[l]))
    k3 = k.reshape(bp, sp, KV_WIDTH)
    v3 = v.reshape(bp, sp, KV_WIDTH)
    attn = _attn_prompt(q.reshape(bp, sp, ATTN_WIDTH), k3, v3, sinks[l], rel_bias)
    keep = min(WINDOW, sp)
    nkp = k3[:, -keep:].reshape(bp, keep, N_KV_HEADS, HEAD_DIM)
    nvp = v3[:, -keep:].reshape(bp, keep, N_KV_HEADS, HEAD_DIM)
    npp = u_tail[:, POOL_HALO - POOL_BUF:]
    x2_p, hs, slots_p, cnt_p = _route(
        xp, attn.reshape(n_p, ATTN_WIDTH), pool.reshape(n_p, POOL_WIDTH), mod_p,
        w_out_b, norm_ffn[l], w_router_t, b_router[l], None,
        tile=SORT_TILE, sub=ROUTE_SUB, rows_per_mod=sp, tile0=0, extra_tiles=max(2, ROUTE_SUB))

    xs = x_sample.reshape(bs, D_MODEL)
    qs, ks, vs, us = _mixer_inputs(xs, mod_s, norm_mix[l], w_in[l], bd, q_norm[l], k_norm[l],
                                   tile=bs, rows_per_mod=bs, precise=True)
    wbuf = cache_k.shape[2]
    attn_s, nks, nvs = _attn_sample(qs, ks, vs, cache_k[l].reshape(bs, wbuf, KV_WIDTH),
                                    cache_v[l].reshape(bs, wbuf, KV_WIDTH), sinks[l], rel_bias)
    pool_s, nps_t = _pool_sample(jnp.swapaxes(state_pool[l], 0, 1), us, w_pool_b, pool_scale[l])
    x2_s, hs, slots_s, cnt_s = _route(
        xs, attn_s.astype(BF16), pool_s, mod_s, w_out_b, norm_ffn[l],
        w_router_t, b_router[l], hs, tile=bs, sub=1, rows_per_mod=bs, tile0=tiles_p, extra_tiles=0)

    cnt = jnp.concatenate([cnt_p[:, :, 0], cnt_s[:, :, 0]], axis=0).astype(jnp.int32)
    src, block_e, next_e, nblocks = _plan(cnt, n_blocks_max)
    outs = _moe(hs, src, block_e, next_e, nblocks, w1[l], b1[l], w2[l], b2[l],
                scratch_chunk=(tiles_p + 1) * TILE_CHUNKS)

    y_p = _combine(outs, slots_p, x2_p, mod_p, tile=SORT_TILE, sub=COMBINE_SUB, rows_per_mod=sp, tile0=0)
    y_s = _combine(outs, slots_s, x2_s, mod_s, tile=bs, sub=1, rows_per_mod=bs, tile0=tiles_p)

    return (y_p.reshape(bp, sp, D_MODEL), y_s.reshape(bs, 1, D_MODEL),
            nkp[None], nvp[None], npp[None],
            nks.reshape(1, bs, wbuf, N_KV_HEADS, HEAD_DIM), nvs.reshape(1, bs, wbuf, N_KV_HEADS, HEAD_DIM),
            jnp.swapaxes(nps_t, 0, 1)[None])
```

```python
import functools
import math

import jax
import jax.numpy as jnp
from jax import lax
from jax.experimental import pallas as pl
from jax.experimental.pallas import tpu as pltpu

F32 = jnp.float32
BF16 = jnp.bfloat16

D_MODEL = 1024
HEAD_DIM = 64
N_HEADS = 8
N_KV_HEADS = 2
GROUP = N_HEADS // N_KV_HEADS
ATTN_WIDTH = N_HEADS * HEAD_DIM
KV_WIDTH = N_KV_HEADS * HEAD_DIM
POOL_WIDTH = D_MODEL - ATTN_WIDTH
POOL_WINDOWS = (2, 4, 8, 16)
POOL_GROUP = POOL_WIDTH // len(POOL_WINDOWS)
POOL_BUF = max(POOL_WINDOWS) - 1
IN_WIDTH = ATTN_WIDTH + 2 * KV_WIDTH + POOL_WIDTH
WINDOW = 128
ATTN_BLOCK = 128
N_BUCKETS = 32
MAX_EXACT = 16
REL_MAX_DIST = 128
N_EXPERTS = 32
TOP_K = 4
D_FF = D_MODEL
SWIGLU_LIMIT = 7.0
SWIGLU_ALPHA = 1.702
EPS = 1e-6
NEG_INF = -1e30
PAST_LEN = 16384

LANES = 128
SUBLANES = 8
VMEM_LIMIT = 56 * 1024 * 1024

ATTN_QB = 16
POOL_HALO = 2 * SUBLANES

SORT_TILE = 256
ROUTE_SUB = 4
COMBINE_SUB = 4
CHUNK = SUBLANES
TILE_ROWS = -(-(SORT_TILE * TOP_K + N_EXPERTS * (CHUNK - 1)) // LANES) * LANES
TILE_CHUNKS = TILE_ROWS // CHUNK
MOE_BLOCK = 256
BLOCK_CHUNKS = MOE_BLOCK // CHUNK
WEIGHT_DMA_PRIORITY = 1
PACKED_W = D_MODEL // 2
ROW_W = PACKED_W
ZERO_CHUNK = TILE_CHUNKS - 1


def _bdot(a, b):
    return jnp.dot(a.astype(BF16), b.astype(BF16), preferred_element_type=F32)


def _split(a):
    hi = a.astype(BF16)
    lo = (a - hi.astype(F32)).astype(BF16)
    return hi, lo


def _dot3(a, b):
    ah, al = _split(a)
    bh, bl = _split(b)
    d = functools.partial(jnp.dot, preferred_element_type=F32)
    return d(ah, bh) + d(al, bh) + d(ah, bl)


def _pack_rows(x):
    bits = lax.bitcast_convert_type(x, jnp.int32)
    return bits[:, :PACKED_W] | lax.shift_right_logical(bits[:, PACKED_W:], 16)


def _unpack_rows(w):
    hi = lax.bitcast_convert_type(w & jnp.int32(-65536), F32)
    lo = lax.bitcast_convert_type(lax.shift_left(w, 16), F32)
    return hi.astype(BF16), lo.astype(BF16)


def _rms(x, g):
    return x * lax.rsqrt(jnp.mean(x * x, axis=-1, keepdims=True) + EPS) * g


def _cparams(sem, **kw):
    return pltpu.CompilerParams(dimension_semantics=sem, vmem_limit_bytes=VMEM_LIMIT, **kw)


def _ada_kernel(c_ref, w_ref, b_ref, o_ref):
    c = c_ref[...]
    s = c * jax.nn.sigmoid(c)
    for v in range(o_ref.shape[0]):
        cols = slice(v * D_MODEL, (v + 1) * D_MODEL)
        o_ref[v] = _dot3(s, w_ref[:, cols]) + b_ref[:, cols]


def _modulation(c, w_ada, b_ada, *, vectors_per_step=1):
    rows = c.shape[0]
    n = w_ada.shape[1]
    nvec = n // D_MODEL
    assert nvec % vectors_per_step == 0
    tn = vectors_per_step * D_MODEL
    return pl.pallas_call(
        _ada_kernel,
        grid=(nvec // vectors_per_step,),
        in_specs=[pl.BlockSpec((rows, D_MODEL), lambda j: (0, 0)),
                  pl.BlockSpec((D_MODEL, tn), lambda j: (0, j)),
                  pl.BlockSpec((1, tn), lambda j: (0, j))],
        out_specs=pl.BlockSpec((vectors_per_step, rows, D_MODEL), lambda j: (j, 0, 0)),
        out_shape=jax.ShapeDtypeStruct((nvec, rows, D_MODEL), F32),
        compiler_params=_cparams(("parallel",)),
        name="modulation",
    )(c, w_ada, b_ada.reshape(1, n))


def _head_rms(t, bd, g, precise):
    if precise:
        hi, lo = _split(t * t)
        ss = jnp.dot(hi, bd, preferred_element_type=F32) + jnp.dot(lo, bd, preferred_element_type=F32)
    else:
        ss = _bdot(t * t, bd)
    return t * lax.rsqrt(ss * (1.0 / HEAD_DIM) + EPS) * g


def _mixin_qkv(x_ref, sh_ref, sc_ref, g_ref, w_ref, bd_ref, qn_ref, kn_ref, q_ref, k_ref, v_ref, precise):
    h = _rms(x_ref[...], g_ref[...] * (1.0 + sc_ref[...])) + sh_ref[...]
    z = _dot3(h, w_ref[...]) if precise else _bdot(h, w_ref[...])
    q = z[:, :ATTN_WIDTH]
    k = z[:, ATTN_WIDTH:ATTN_WIDTH + KV_WIDTH]
    bd = bd_ref[...]
    q = _head_rms(q, bd, qn_ref[...], precise)
    k = _head_rms(k, bd[:KV_WIDTH, :KV_WIDTH], kn_ref[...], precise)
    q_ref[...] = (q * (HEAD_DIM ** -0.5)).astype(BF16)
    k_ref[...] = k
    v_ref[...] = z[:, ATTN_WIDTH + KV_WIDTH:ATTN_WIDTH + 2 * KV_WIDTH]
    return z[:, ATTN_WIDTH + 2 * KV_WIDTH:]


def _mixin_kernel(x_ref, sh_ref, sc_ref, g_ref, w_ref, bd_ref, qn_ref, kn_ref,
                  q_ref, k_ref, v_ref, u_ref, *, precise):
    u_ref[...] = _mixin_qkv(x_ref, sh_ref, sc_ref, g_ref, w_ref, bd_ref, qn_ref, kn_ref,
                            q_ref, k_ref, v_ref, precise)


def _mixin_pool_kernel(x_ref, sh_ref, sc_ref, g_ref, w_ref, bd_ref, qn_ref, kn_ref, wp_ref, ps_ref,
                       q_ref, k_ref, v_ref, pool_ref, tail_ref, ext, lv, carry, *, tiles_per_seq):
    u = _mixin_qkv(x_ref, sh_ref, sc_ref, g_ref, w_ref, bd_ref, qn_ref, kn_ref, q_ref, k_ref, v_ref, False)
    t = pl.program_id(0) % tiles_per_seq

    @pl.when(t == 0)
    def _():
        carry[...] = jnp.zeros(carry.shape, F32)

    pool_ref[...] = _pool_tile(u, carry[...], t * u.shape[0], wp_ref, ps_ref, ext, lv)
    last = u[u.shape[0] - carry.shape[0]:, :]
    carry[...] = last
    tail_ref[...] = last


def _mod_spec(mod, k, group_of_step):
    return pl.BlockSpec((None, None, mod.shape[2], D_MODEL), lambda i: (k, group_of_step(i), 0, 0))


def _mixer_inputs(x2d, mod, norm_mix, w_in, bd, q_norm, k_norm, *, tile, rows_per_mod, precise, pool=None):
    n = x2d.shape[0]
    group = lambda i: (i * tile) // rows_per_mod
    const = lambda shape: pl.BlockSpec(shape, lambda i: (0,) * len(shape))
    row = lambda w: pl.BlockSpec((tile, w), lambda i: (i, 0))
    in_specs = [row(D_MODEL), _mod_spec(mod, 0, group), _mod_spec(mod, 1, group), const((1, D_MODEL)),
                const((D_MODEL, IN_WIDTH)), const((ATTN_WIDTH, ATTN_WIDTH)), const((1, ATTN_WIDTH)),
                const((1, KV_WIDTH))]
    args = [x2d, mod, mod, norm_mix.reshape(1, D_MODEL), w_in, bd,
            jnp.tile(q_norm, N_HEADS).reshape(1, ATTN_WIDTH), jnp.tile(k_norm, N_KV_HEADS).reshape(1, KV_WIDTH)]
    out_specs = [row(ATTN_WIDTH), row(KV_WIDTH), row(KV_WIDTH)]
    out_shape = [jax.ShapeDtypeStruct((n, ATTN_WIDTH), BF16),
                 jax.ShapeDtypeStruct((n, KV_WIDTH), F32),
                 jax.ShapeDtypeStruct((n, KV_WIDTH), F32)]
    if pool is None:
        kern = functools.partial(_mixin_kernel, precise=precise)
        out_specs.append(row(POOL_WIDTH))
        out_shape.append(jax.ShapeDtypeStruct((n, POOL_WIDTH), F32))
        scratch, semantics = [], ("parallel",)
    else:
        assert not precise and rows_per_mod % tile == 0 and tile >= POOL_HALO
        w_pool, pool_scale = pool
        kern = functools.partial(_mixin_pool_kernel, tiles_per_seq=rows_per_mod // tile)
        in_specs += [const(w_pool.shape), const((1, POOL_WIDTH))]
        args += [w_pool, pool_scale.reshape(1, POOL_WIDTH)]
        out_specs += [pl.BlockSpec((tile, POOL_WIDTH), lambda i: (i, 0)),
                      pl.BlockSpec((None, POOL_HALO, POOL_WIDTH), lambda i: (group(i), 0, 0))]
        out_shape += [jax.ShapeDtypeStruct((n, POOL_WIDTH), BF16),
                      jax.ShapeDtypeStruct((n // rows_per_mod, POOL_HALO, POOL_WIDTH), F32)]
        ext_rows = SUBLANES + POOL_HALO + tile
        scratch = [pltpu.VMEM((ext_rows, POOL_WIDTH), F32),
                   pltpu.VMEM((len(POOL_WINDOWS) - 1, ext_rows, POOL_GROUP), F32),
                   pltpu.VMEM((POOL_HALO, POOL_WIDTH), F32)]
        semantics = ("arbitrary",)
    return pl.pallas_call(
        kern,
        grid=(n // tile,),
        in_specs=in_specs,
        out_specs=out_specs,
        out_shape=out_shape,
        scratch_shapes=scratch,
        compiler_params=_cparams(semantics),
        name="mixer_inputs",
    )(*args)


def _t5_bucket(rel):
    n = jnp.maximum(rel, 0)
    nf = jnp.maximum(n, 1).astype(F32)
    large = MAX_EXACT + (jnp.log(nf / MAX_EXACT) / math.log(REL_MAX_DIST / MAX_EXACT)
                         * (N_BUCKETS - MAX_EXACT)).astype(jnp.int32)
    large = jnp.minimum(large, N_BUCKETS - 1)
    return jnp.where(n < MAX_EXACT, n, large)


def _bias_table(rel, rel_table):
    bucket = _t5_bucket(rel)
    table = rel_table.astype(F32)
    ids = jnp.arange(N_BUCKETS, dtype=bucket.dtype).reshape((N_BUCKETS, 1) + (1,) * rel.ndim)
    onehot = bucket[None, None] == ids
    bias = jnp.sum(jnp.where(onehot, table.reshape(table.shape + (1,) * rel.ndim), 0.0), axis=0)
    valid = (rel >= 0) & (rel < WINDOW)
    return jnp.where(valid[None], bias, NEG_INF)


def _attn_prompt_kernel(sink_ref, q_ref, kp_ref, kc_ref, vp_ref, vc_ref, bias_ref, o_ref):
    first = pl.program_id(1) == 0
    kk = jnp.concatenate([kp_ref[...], kc_ref[...]], axis=0).astype(BF16)
    vv = jnp.concatenate([vp_ref[...], vc_ref[...]], axis=0).astype(BF16)
    key = lax.broadcasted_iota(jnp.int32, (2 * ATTN_BLOCK, 1), 0)
    no_prev = jnp.logical_and(first, key < ATTN_BLOCK)
    lane = lax.broadcasted_iota(jnp.int32, (1, N_HEADS * ATTN_BLOCK), 1)
    sink = jnp.zeros((1, N_HEADS * ATTN_BLOCK), F32)
    for h in range(N_HEADS):
        sink = jnp.where(lane // ATTN_BLOCK == h, sink_ref[h], sink)
    contract = lambda a, b, dims: lax.dot_general(a, b, (dims, ((), ())), preferred_element_type=F32)
    part = GROUP * ATTN_BLOCK
    for i in range(ATTN_QB):
        q = q_ref[i * ATTN_BLOCK:(i + 1) * ATTN_BLOCK, :]
        keys = slice(i * ATTN_BLOCK, (i + 2) * ATTN_BLOCK)
        scores = []
        for kv in range(N_KV_HEADS):
            heads = range(kv * GROUP, (kv + 1) * GROUP)
            qg = jnp.concatenate([q[:, h * HEAD_DIM:(h + 1) * HEAD_DIM] for h in heads], axis=0)
            scores.append(contract(kk[keys, kv * HEAD_DIM:(kv + 1) * HEAD_DIM], qg, ((1,), (1,))))
        s = jnp.concatenate(scores, axis=1) + bias_ref[...]
        if i == 0:
            s = jnp.where(no_prev, NEG_INF, s)
        m = jnp.maximum(jnp.max(s, axis=0, keepdims=True), sink)
        p = jnp.exp(s - m)
        denom = jnp.sum(p, axis=0, keepdims=True) + jnp.exp(sink - m)
        p = p.astype(BF16)
        halves = [contract(vv[keys, kv * HEAD_DIM:(kv + 1) * HEAD_DIM], p[:, kv * part:(kv + 1) * part],
                           ((0,), (0,))) / denom[:, kv * part:(kv + 1) * part]
                  for kv in range(N_KV_HEADS)]
        o_t = jnp.concatenate(halves, axis=0)
        per_g = [o_t[:, g * ATTN_BLOCK:(g + 1) * ATTN_BLOCK].T for g in range(GROUP)]
        out = [t[:, kv * HEAD_DIM:(kv + 1) * HEAD_DIM] for kv in range(N_KV_HEADS) for t in per_g]
        o_ref[i * ATTN_BLOCK:(i + 1) * ATTN_BLOCK, :] = jnp.concatenate(out, axis=-1).astype(BF16)


def _attn_prompt(q, k, v, sinks, rel_table):
    b, s = q.shape[:2]
    qrows = ATTN_QB * ATTN_BLOCK
    assert s % qrows == 0
    qi = jnp.arange(ATTN_BLOCK, dtype=jnp.int32)[:, None]
    si = jnp.arange(2 * ATTN_BLOCK, dtype=jnp.int32)[None, :]
    bias = _bias_table(qi + ATTN_BLOCK - si, rel_table)
    bias = bias.reshape(N_HEADS * ATTN_BLOCK, 2 * ATTN_BLOCK).T
    cur = lambda w: pl.BlockSpec((None, qrows, w), lambda i, j, *_: (i, j, 0))
    prev = lambda w: pl.BlockSpec((None, ATTN_BLOCK, w),
                                  lambda i, j, *_: (i, jnp.maximum(j * ATTN_QB - 1, 0), 0))
    return pl.pallas_call(
        _attn_prompt_kernel,
        grid_spec=pltpu.PrefetchScalarGridSpec(
            num_scalar_prefetch=1,
            grid=(b, s // qrows),
            in_specs=[cur(ATTN_WIDTH), prev(KV_WIDTH), cur(KV_WIDTH), prev(KV_WIDTH), cur(KV_WIDTH),
                      pl.BlockSpec(bias.shape, lambda i, j, *_: (0, 0))],
            out_specs=cur(ATTN_WIDTH)),
        out_shape=jax.ShapeDtypeStruct((b, s, ATTN_WIDTH), BF16),
        compiler_params=_cparams(("parallel", "parallel")),
        name="attn_prompt",
    )(sinks.astype(F32), q, k, k, v, v, bias)


def _attn_sample_kernel(sink_ref, q_ref, kc_ref, vc_ref, kn_ref, vn_ref, bias_ref, bnew_ref,
                        o_ref, nk_ref, nv_ref):
    kc = kc_ref[...]
    vc = vc_ref[...]
    kn = kn_ref[...]
    vn = vn_ref[...]
    w = kc.shape[1]
    pos = lax.broadcasted_iota(jnp.int32, kc.shape, 1)
    nk_ref[...] = jnp.where(pos == w - 1, kn[:, None, :], pltpu.roll(kc, w - 1, 1))
    nv_ref[...] = jnp.where(pos == w - 1, vn[:, None, :], pltpu.roll(vc, w - 1, 1))
    gi = lax.broadcasted_iota(jnp.int32, (1, GROUP, 1), 1)
    for kv in range(N_KV_HEADS):
        sl = slice(kv * HEAD_DIM, (kv + 1) * HEAD_DIM)
        qg = q_ref[:, kv]
        s = jnp.einsum('bgd,bsd->bgs', qg, kc[:, :, sl].astype(BF16), preferred_element_type=F32)
        s = s + bias_ref[kv][None]
        s_new = jnp.sum(qg.astype(F32) * kn[:, None, sl], axis=-1, keepdims=True) + bnew_ref[kv][None]
        sink = jnp.zeros((1, GROUP, 1), F32)
        for g in range(GROUP):
            sink = jnp.where(gi == g, sink_ref[kv * GROUP + g], sink)
        m = jnp.maximum(jnp.maximum(jnp.max(s, axis=-1, keepdims=True), s_new), sink)
        p = jnp.exp(s - m)
        p_new = jnp.exp(s_new - m)
        denom = jnp.sum(p, axis=-1, keepdims=True) + p_new + jnp.exp(sink - m)
        o = jnp.einsum('bgs,bsd->bgd', p.astype(BF16), vc[:, :, sl].astype(BF16), preferred_element_type=F32)
        o = o + p_new * vn[:, None, sl]
        o_ref[:, kv] = o / denom


def _attn_sample(q, k_new, v_new, cache_k, cache_v, sinks, rel_table, *, tile=32):
    bd, w = cache_k.shape[:2]
    rel = w - jnp.arange(w, dtype=jnp.int32)
    bias = _bias_table(rel, rel_table).reshape(N_KV_HEADS, GROUP, w)
    bnew = _bias_table(jnp.zeros((1,), jnp.int32), rel_table).reshape(N_KV_HEADS, GROUP, 1)
    q4 = q.reshape(bd, N_KV_HEADS, GROUP, HEAD_DIM)
    spec4 = pl.BlockSpec((tile, N_KV_HEADS, GROUP, HEAD_DIM), lambda i, *_: (i, 0, 0, 0))
    cache = pl.BlockSpec((tile, w, KV_WIDTH), lambda i, *_: (i, 0, 0))
    new = pl.BlockSpec((tile, KV_WIDTH), lambda i, *_: (i, 0))
    const3 = lambda a: pl.BlockSpec(a.shape, lambda i, *_: (0, 0, 0))
    o, nk, nv = pl.pallas_call(
        _attn_sample_kernel,
        grid_spec=pltpu.PrefetchScalarGridSpec(
            num_scalar_prefetch=1,
            grid=(bd // tile,),
            in_specs=[spec4, cache, cache, new, new, const3(bias), const3(bnew)],
            out_specs=[spec4, cache, cache]),
        out_shape=[jax.ShapeDtypeStruct(q4.shape, F32),
                   jax.ShapeDtypeStruct(cache_k.shape, F32),
                   jax.ShapeDtypeStruct(cache_v.shape, F32)],
        compiler_params=_cparams(("parallel",)),
        name="attn_sample",
    )(sinks.astype(F32), q4, cache_k, cache_v, k_new, v_new, bias, bnew)
    return o.reshape(bd, ATTN_WIDTH), nk, nv


def _pool_project(d_groups, wp_ref, ps_ref):
    out = [_bdot(d, wp_ref[g]) for g, d in enumerate(d_groups)]
    return (jnp.concatenate(out, axis=-1) * ps_ref[...]).astype(BF16)


def _pool_tile(u, halo, pos0, wp_ref, ps_ref, ext, lv):
    tile = u.shape[0]
    lead, hb = SUBLANES, POOL_HALO
    ext[0:lead, :] = jnp.zeros((lead, ext.shape[1]), F32)
    ext[lead:lead + hb, :] = halo
    ext[lead + hb:, :] = u
    lv[:, 0:lead, :] = jnp.zeros((lv.shape[0], lead, lv.shape[2]), F32)
    pos = pos0 + lax.broadcasted_iota(jnp.int32, (tile, 1), 0)
    n = hb + tile
    ds = []
    for g, w in enumerate(POOL_WINDOWS):
        sl = slice(g * POOL_GROUP, (g + 1) * POOL_GROUP)
        acc = ext[lead:lead + n, sl] + ext[lead - 1:lead - 1 + n, sl]
        span, level = 2, 0
        while span < w:
            lv[level, lead:lead + n, :] = acc
            acc = acc + lv[level, lead - span:lead - span + n, :]
            span, level = 2 * span, level + 1
        cnt = jnp.minimum(pos + 1, w).astype(F32)
        ds.append(acc[hb:] / cnt - ext[lead + hb:lead + hb + tile, sl])
    return _pool_project(ds, wp_ref, ps_ref)


def _pool_sample_kernel(st_ref, u_ref, wp_ref, ps_ref, o_ref, ns_ref):
    u = u_ref[...]
    ns_ref[0:POOL_BUF - 1] = st_ref[1:POOL_BUF]
    ns_ref[POOL_BUF - 1] = u
    ds = []
    for g, w in enumerate(POOL_WINDOWS):
        sl = slice(g * POOL_GROUP, (g + 1) * POOL_GROUP)
        acc = u[:, sl]
        for j in range(1, w):
            acc = acc + st_ref[POOL_BUF - j][:, sl]
        cnt = float(min(PAST_LEN + 1, w))
        ds.append(acc / cnt - u[:, sl])
    o_ref[...] = _pool_project(ds, wp_ref, ps_ref)


def _pool_sample(state_t, u, w_pool, pool_scale):
    nb, bd, c = state_t.shape
    full = lambda a: pl.BlockSpec(a.shape, lambda: (0,) * a.ndim)
    ps = pool_scale.reshape(1, c)
    return pl.pallas_call(
        _pool_sample_kernel,
        in_specs=[full(state_t), full(u), full(w_pool), full(ps)],
        out_specs=[pl.BlockSpec((bd, c), lambda: (0, 0)), full(state_t)],
        out_shape=[jax.ShapeDtypeStruct((bd, c), BF16), jax.ShapeDtypeStruct(state_t.shape, F32)],
        compiler_params=pltpu.CompilerParams(vmem_limit_bytes=VMEM_LIMIT),
        name="pool_sample",
    )(state_t, u, w_pool, ps)


def _route_kernel(*refs, own_steps):
    hs_ref = refs[-3]

    @pl.when(pl.program_id(0) < own_steps)
    def _():
        _route_tiles(*refs)

    @pl.when(pl.program_id(0) >= own_steps)
    def _():
        hs_ref[...] = jnp.zeros(hs_ref.shape, jnp.int32)


def _route_tiles(x_ref, attn_ref, pool_ref, gm_ref, sh_ref, sc_ref, wo_ref, nf_ref, wr_ref, br_ref,
                 tri_t_ref, tri_e_ref,
                 x2_ref, hs_ref, slot_ref, cnt_ref):
    sub = slot_ref.shape[0]
    n = x_ref.shape[0]
    tile = n // sub
    mixed = jnp.concatenate([attn_ref[...], pool_ref[...]], axis=1)
    mix = jnp.dot(mixed, wo_ref[...], preferred_element_type=F32)
    x2 = x_ref[...] + gm_ref[...] * mix
    x2_ref[...] = x2
    h = _rms(x2, nf_ref[...] * (1.0 + sc_ref[...])) + sh_ref[...]

    hh, hl = _split(h)
    wh, wl = _split(wr_ref[...])
    nt = functools.partial(lax.dot_general, dimension_numbers=(((1,), (1,)), ((), ())),
                           preferred_element_type=F32)
    logits = nt(wh, hh) + nt(wl, hh) + nt(wh, hl) + br_ref[...]

    eidx = lax.broadcasted_iota(jnp.int32, (N_EXPERTS, n), 0).astype(F32)
    work = logits
    tops, picks = [], []
    for _ in range(TOP_K):
        m = jnp.max(work, axis=0, keepdims=True)
        pick = jnp.min(jnp.where(work == m, eidx, float(N_EXPERTS)), axis=0, keepdims=True)
        work = jnp.where(eidx == pick, -jnp.inf, work)
        tops.append(m)
        picks.append(pick)
    ex = [jnp.exp(v - tops[0]) for v in tops]
    den = ex[0] + ex[1] + ex[2] + ex[3]
    gates = [e / den for e in ex]

    sel = jnp.zeros((N_EXPERTS, n), F32)
    for pick in picks:
        sel = sel + (eidx == pick).astype(F32)
    selb = sel.astype(BF16)
    rank = jnp.concatenate([jnp.dot(selb[:, t * tile:(t + 1) * tile], tri_t_ref[...], preferred_element_type=F32)
                            for t in range(sub)], axis=1)
    cnts = [jnp.sum(sel[:, t * tile:(t + 1) * tile], axis=1, keepdims=True) for t in range(sub)]
    padded = jnp.concatenate(
        [jnp.broadcast_to(jnp.ceil(c * (1.0 / CHUNK)) * CHUNK, (N_EXPERTS, LANES)) for c in cnts], axis=1)
    seg = jnp.dot(tri_e_ref[...], padded.astype(BF16), preferred_element_type=F32)
    dest = jnp.concatenate([seg[:, t * LANES:t * LANES + 1] + rank[:, t * tile:(t + 1) * tile]
                            for t in range(sub)], axis=1)
    slots = [jnp.sum(jnp.where(eidx == pick, dest, 0.0), axis=0, keepdims=True) for pick in picks]

    ridx = lax.broadcasted_iota(jnp.int32, (TILE_ROWS, tile), 0).astype(jnp.int16)
    slots16 = [v.astype(jnp.int32).astype(jnp.int16) for v in slots]
    for t in range(sub):
        cols = slice(t * tile, (t + 1) * tile)
        cnt_ref[t] = jnp.broadcast_to(cnts[t], (N_EXPERTS, LANES))
        slot_ref[t] = jnp.concatenate([v[:, cols] for v in slots + gates], axis=0)
        hit = ridx == slots16[0][:, cols]
        for s in slots16[1:]:
            hit = jnp.logical_or(hit, ridx == s[:, cols])
        perm = jnp.where(hit, jnp.ones((), BF16), jnp.zeros((), BF16))
        hs_ref[pl.ds(t * TILE_ROWS, TILE_ROWS), :] = _pack_rows(
            jnp.dot(perm, hh[t * tile:(t + 1) * tile, :], preferred_element_type=F32))


def _route(x2d, attn, pool, mod, w_out, norm_ffn, w_router_t, b_router, hs_prev, *, tile, sub,
           rows_per_mod, tile0, extra_tiles):
    n = x2d.shape[0]
    nt = n // tile
    assert nt % sub == 0 and extra_tiles % sub == 0
    own_steps = nt // sub
    steps = own_steps + extra_tiles // sub
    mrows = mod.shape[2]
    assert mrows == 1 or (sub == 1 and mrows == tile)
    last = lambda i: jnp.minimum(i, own_steps - 1)
    group = lambda i: (last(i) * sub * tile) // rows_per_mod
    const = lambda shape: pl.BlockSpec(shape, lambda i: (0,) * len(shape))
    row = lambda w: pl.BlockSpec((sub * tile, w), lambda i: (last(i), 0))
    tri_t = (jnp.arange(tile)[:, None] < jnp.arange(tile)[None, :]).astype(BF16)
    tri_e = (jnp.arange(N_EXPERTS)[None, :] < jnp.arange(N_EXPERTS)[:, None]).astype(BF16)
    in_specs = [row(D_MODEL), row(ATTN_WIDTH), row(POOL_WIDTH),
                _mod_spec(mod, 2, group), _mod_spec(mod, 3, group), _mod_spec(mod, 4, group),
                const((D_MODEL, D_MODEL)), const((1, D_MODEL)), const((N_EXPERTS, D_MODEL)),
                const((N_EXPERTS, 1)), const((tile, tile)), const((N_EXPERTS, N_EXPERTS))]
    args = [x2d, attn, pool, mod, mod, mod, w_out, norm_ffn.reshape(1, D_MODEL), w_router_t,
            b_router.reshape(N_EXPERTS, 1), tri_t, tri_e]
    n_in = len(args)
    kern = functools.partial(_route_kernel, own_steps=own_steps)
    aliases = {}
    hs_rows = (tile0 + steps * sub) * TILE_ROWS
    assert tile0 % sub == 0
    if hs_prev is not None:
        in_specs.append(pl.BlockSpec(memory_space=pl.ANY))
        args.append(hs_prev)
        aliases = {n_in: 1}
        kern = lambda *refs: _route_kernel(*refs[:n_in], *refs[n_in + 1:], own_steps=own_steps)
        hs_rows = hs_prev.shape[0]
    return pl.pallas_call(
        kern,
        grid=(steps,),
        in_specs=in_specs,
        out_specs=[row(D_MODEL),
                   pl.BlockSpec((sub * TILE_ROWS, ROW_W), lambda i: (i + tile0 // sub, 0)),
                   pl.BlockSpec((sub, 2 * TOP_K, tile), lambda i: (last(i), 0, 0)),
                   pl.BlockSpec((sub, N_EXPERTS, LANES), lambda i: (last(i), 0, 0))],
        out_shape=[jax.ShapeDtypeStruct((n, D_MODEL), F32),
                   jax.ShapeDtypeStruct((hs_rows, ROW_W), jnp.int32),
                   jax.ShapeDtypeStruct((nt, 2 * TOP_K, tile), F32),
                   jax.ShapeDtypeStruct((nt, N_EXPERTS, LANES), F32)],
        input_output_aliases=aliases,
        compiler_params=_cparams(("arbitrary",)),
        name="route",
    )(*args)


def _moe_kernel(src_ref, be_ref, nxt_ref, nb_ref, hs_hbm, w1_hbm, b1_ref, w2_hbm, b2_ref, out_hbm,
                lhs, obuf, w1s, w2s, w1c, w2c, sem_in, sem_out, sem_w, *, scratch_chunk):
    nb = nb_ref[0]

    def weight_copies(e):
        return (pltpu.make_async_copy(w1_hbm.at[e], w1s, sem_w.at[0]),
                pltpu.make_async_copy(w2_hbm.at[e], w2s, sem_w.at[1]))

    def chunk_rows(c):
        return pl.ds(pl.multiple_of(c * CHUNK, CHUNK), CHUNK)

    def start_in(blk, s):
        for j in range(BLOCK_CHUNKS):
            c = src_ref[blk * BLOCK_CHUNKS + j]
            c = jnp.where(c < 0, ZERO_CHUNK, c)
            pltpu.make_async_copy(hs_hbm.at[chunk_rows(c)], lhs.at[s, pl.ds(j * CHUNK, CHUNK)],
                                  sem_in.at[s]).start()

    def wait_in(s):
        pltpu.make_async_copy(hs_hbm.at[pl.ds(0, MOE_BLOCK)], lhs.at[s], sem_in.at[s]).wait()

    def start_out(blk, s):
        for j in range(BLOCK_CHUNKS):
            c = src_ref[blk * BLOCK_CHUNKS + j]
            c = jnp.where(c < 0, scratch_chunk + s * BLOCK_CHUNKS + j, c)
            pltpu.make_async_copy(obuf.at[s, pl.ds(j * CHUNK, CHUNK)],
                                  out_hbm.at[chunk_rows(c), pl.ds(0, PACKED_W)], sem_out.at[s]).start()

    def wait_out(s):
        pltpu.make_async_copy(obuf.at[s], out_hbm.at[pl.ds(0, MOE_BLOCK), pl.ds(0, PACKED_W)],
                              sem_out.at[s]).wait()

    @pl.when(nb > 0)
    def _():
        start_in(0, 0)
        for cp in weight_copies(be_ref[0]):
            cp.start(priority=WEIGHT_DMA_PRIORITY)

    def block(b, carry):
        slot = b % 2
        e = be_ref[b]

        @pl.when(jnp.logical_or(b == 0, e != be_ref[jnp.maximum(b - 1, 0)]))
        def _():
            for cp in weight_copies(e):
                cp.wait()
            w1c[...] = w1s[...].astype(BF16)
            w2c[...] = w2s[...].astype(BF16)
            nxt = nxt_ref[b]

            @pl.when(nxt >= 0)
            def _():
                for cp in weight_copies(nxt):
                    cp.start(priority=WEIGHT_DMA_PRIORITY)

        wait_in(slot)

        @pl.when(b >= 2)
        def _():
            wait_out(slot)

        start_in(b + 1, 1 - slot)

        def ffn(nrows):
            xh, xl = _unpack_rows(lhs[slot, 0:nrows, :])
            x = jnp.concatenate([xh, xl], axis=1)
            gu = jnp.dot(x, w1c[...], preferred_element_type=F32) + b1_ref[e]
            gate = jnp.minimum(gu[:, :D_FF], SWIGLU_LIMIT)
            up = jnp.clip(gu[:, D_FF:], -SWIGLU_LIMIT, SWIGLU_LIMIT)
            act = (up + 1.0) * (gate * jax.nn.sigmoid(SWIGLU_ALPHA * gate))
            y = jnp.dot(act.astype(BF16), w2c[...], preferred_element_type=F32) + b2_ref[e]
            obuf[slot, 0:nrows, :] = _pack_rows(y.astype(BF16).astype(F32))

        real = nb_ref[1 + b]

        @pl.when(real > BLOCK_CHUNKS // 2)
        def _():
            ffn(MOE_BLOCK)

        @pl.when(real <= BLOCK_CHUNKS // 2)
        def _():
            ffn(MOE_BLOCK // 2)

        start_out(b, slot)
        return carry

    obuf[...] = jnp.zeros(obuf.shape, jnp.int32)
    lax.fori_loop(0, nb, block, 0)

    @pl.when(nb > 0)
    def _():
        last_slot = (nb - 1) % 2
        wait_in(1 - last_slot)

        @pl.when(nb >= 2)
        def _():
            wait_out(1 - last_slot)
        wait_out(last_slot)


def _moe(hs, src, block_e, next_e, nblocks, w1, b1, w2, b2, scratch_chunk):
    full = lambda shape: pl.BlockSpec(shape, lambda i, *_: (0,) * len(shape))
    hbm = pl.BlockSpec(memory_space=pl.ANY)
    return pl.pallas_call(
        functools.partial(_moe_kernel, scratch_chunk=scratch_chunk),
        grid_spec=pltpu.PrefetchScalarGridSpec(
            num_scalar_prefetch=4,
            grid=(1,),
            in_specs=[hbm, hbm, full((N_EXPERTS, 1, 2 * D_FF)), hbm, full((N_EXPERTS, 1, D_MODEL))],
            out_specs=hbm,
            scratch_shapes=[pltpu.VMEM((2, MOE_BLOCK, ROW_W), jnp.int32),
                            pltpu.VMEM((2, MOE_BLOCK, PACKED_W), jnp.int32),
                            pltpu.VMEM((D_MODEL, 2 * D_FF), F32),
                            pltpu.VMEM((D_FF, D_MODEL), F32),
                            pltpu.VMEM((D_MODEL, 2 * D_FF), BF16),
                            pltpu.VMEM((D_FF, D_MODEL), BF16),
                            pltpu.SemaphoreType.DMA((2,)),
                            pltpu.SemaphoreType.DMA((2,)),
                            pltpu.SemaphoreType.DMA((2,))]),
        out_shape=jax.ShapeDtypeStruct(hs.shape, jnp.int32),
        input_output_aliases={4: 0},
        compiler_params=_cparams(("arbitrary",)),
        name="moe_experts",
    )(src, block_e, next_e, nblocks, hs, w1, b1.reshape(N_EXPERTS, 1, 2 * D_FF), w2,
      b2.reshape(N_EXPERTS, 1, D_MODEL))


def _plan(cnt, n_blocks_max):
    nt = cnt.shape[0]
    nch = (cnt + (CHUNK - 1)) // CHUNK
    lstart = jnp.cumsum(nch, axis=1) - nch
    ne = jnp.sum(nch, axis=0)
    nbe = (ne + (BLOCK_CHUNKS - 1)) // BLOCK_CHUNKS
    bend = jnp.cumsum(nbe)
    nblocks = bend[-1]
    gstart = (bend - nbe)[None, :] * BLOCK_CHUNKS + (jnp.cumsum(nch, axis=0) - nch)
    s0 = jnp.arange(nt, dtype=jnp.int32)[:, None] * TILE_CHUNKS + lstart
    blk = jnp.arange(n_blocks_max + 1, dtype=jnp.int32)
    be = jnp.sum((blk[:, None] >= bend[None, :]).astype(jnp.int32), axis=1)
    be_last = jnp.sum((nblocks - 1 >= bend).astype(jnp.int32))
    be = jnp.minimum(be, be_last).astype(jnp.int32)
    eid = jnp.arange(N_EXPERTS, dtype=jnp.int32)
    strips = jnp.stack([gstart.T, nch.T, s0.T])
    mine = (be[:, None] == eid[None, :])[None, :, :, None]
    gs_b, nc_b, s0_b = jnp.sum(jnp.where(mine, strips[:, None], 0), axis=2)
    c = (blk[:, None] * BLOCK_CHUNKS + jnp.arange(BLOCK_CHUNKS, dtype=jnp.int32)[None, :])[:, :, None]
    inside = jnp.logical_and(c >= gs_b[:, None, :], c < (gs_b + nc_b)[:, None, :])
    src = (jnp.sum(jnp.where(inside, (s0_b - gs_b)[:, None, :] + c + 1, 0), axis=2) - 1).reshape(-1)
    be = be[:n_blocks_max]
    later = jnp.logical_and(eid[None, :] > be[:, None], (nbe > 0)[None, :])
    nxt = jnp.min(jnp.where(later, eid[None, :], N_EXPERTS), axis=1)
    nxt = jnp.where(nxt == N_EXPERTS, -1, nxt).astype(jnp.int32)
    real = jnp.sum((src.reshape(-1, BLOCK_CHUNKS) >= 0).astype(jnp.int32), axis=1)
    return src, be, nxt, jnp.concatenate([nblocks.reshape(1), real]).astype(jnp.int32)


def _combine_kernel(o_ref, slot_ref, x2_ref, gf_ref, y_ref):
    sub = slot_ref.shape[0]
    tile = x2_ref.shape[0] // sub
    ridx = lax.broadcasted_iota(jnp.int32, (TILE_ROWS, tile), 0).astype(F32)
    tn = functools.partial(lax.dot_general, dimension_numbers=(((0,), (0,)), ((), ())),
                           preferred_element_type=F32)
    for t in range(sub):
        gmat = jnp.zeros((TILE_ROWS, tile), F32)
        for k in range(TOP_K):
            gmat = jnp.where(ridx == slot_ref[t, k:k + 1, :], slot_ref[t, TOP_K + k:TOP_K + k + 1, :], gmat)
        gb = gmat.astype(BF16)
        oh, ol = _unpack_rows(o_ref[pl.ds(t * TILE_ROWS, TILE_ROWS), :])
        y = jnp.concatenate([tn(gb, oh), tn(gb, ol)], axis=1)
        rows = pl.ds(t * tile, tile)
        gf = gf_ref[...] if gf_ref.shape[0] == 1 else gf_ref[rows, :]
        y_ref[rows, :] = x2_ref[rows, :] + gf * y


def _combine(outs, slots, x2, mod, *, tile, sub, rows_per_mod, tile0):
    n = x2.shape[0]
    mrows = mod.shape[2]
    assert n % (sub * tile) == 0 and tile0 % sub == 0 and (mrows == 1 or sub == 1)
    return pl.pallas_call(
        _combine_kernel,
        grid=(n // (sub * tile),),
        in_specs=[pl.BlockSpec((sub * TILE_ROWS, PACKED_W), lambda i: (i + tile0 // sub, 0)),
                  pl.BlockSpec((sub, 2 * TOP_K, tile), lambda i: (i, 0, 0)),
                  pl.BlockSpec((sub * tile, D_MODEL), lambda i: (i, 0)),
                  _mod_spec(mod, 5, lambda i: (i * sub * tile) // rows_per_mod)],
        out_specs=pl.BlockSpec((sub * tile, D_MODEL), lambda i: (i, 0)),
        out_shape=jax.ShapeDtypeStruct((n, D_MODEL), F32),
        compiler_params=_cparams(("parallel",)),
        name="combine",
    )(outs, slots, x2, mod)


def kernel(x_prompt, x_sample, cache_k, cache_v, state_pool, c_prompt, c_sample, rel_bias, norm_mix, w_ada,
           b_ada, w_in, q_norm, k_norm, sinks, w_pool, pool_scale, w_out, norm_ffn, w_router, b_router,
           w1, b1, w2, b2):
    depth = w_in.shape[0]
    assert depth == 1
    l = 0
    bp, sp, _ = x_prompt.shape
    bs = x_sample.shape[0]
    assert x_sample.shape[1] == 1 and sp % SORT_TILE == 0 and bs <= SORT_TILE
    n_p = bp * sp
    tiles_p = n_p // SORT_TILE
    max_chunks = tiles_p * (SORT_TILE * TOP_K // CHUNK + N_EXPERTS) + (bs * TOP_K // CHUNK + N_EXPERTS)
    n_blocks_max = -(-max_chunks // BLOCK_CHUNKS) + N_EXPERTS

    pr = -(-bp // SUBLANES) * SUBLANES
    c_all = jnp.concatenate([c_prompt, jnp.zeros((pr - bp, D_MODEL), F32), c_sample], axis=0)
    mod = _modulation(c_all, w_ada[l], b_ada[l])
    mod_p = mod[:, :bp].reshape(6, bp, 1, D_MODEL)
    mod_s = mod[:, pr:].reshape(6, 1, bs, D_MODEL)

    head = jnp.arange(ATTN_WIDTH) // HEAD_DIM
    bd = (head[:, None] == head[None, :]).astype(BF16)
    w_in_b = w_in[l].astype(BF16)
    w_out_b = w_out[l].astype(BF16)
    w_pool_b = w_pool[l].astype(BF16)
    w_router_t = w_router[l].T

    xp = x_prompt.reshape(n_p, D_MODEL)
    q, k, v, pool, u_tail = _mixer_inputs(xp, mod_p, norm_mix[l], w_in_b, bd, q_norm[l], k_norm[l],
                                          tile=1024, rows_per_mod=sp, precise=False,
                                          pool=(w_pool_b, pool_scale[l]))
    k3 = k.reshape(bp, sp, KV_WIDTH)
    v3 = v.reshape(bp, sp, KV_WIDTH)
    attn = _attn_prompt(q.reshape(bp, sp, ATTN_WIDTH), k3, v3, sinks[l], rel_bias)
    keep = min(WINDOW, sp)
    nkp = k3[:, -keep:].reshape(bp, keep, N_KV_HEADS, HEAD_DIM)
    nvp = v3[:, -keep:].reshape(bp, keep, N_KV_HEADS, HEAD_DIM)
    npp = u_tail[:, POOL_HALO - POOL_BUF:]
    x2_p, hs, slots_p, cnt_p = _route(
        xp, attn.reshape(n_p, ATTN_WIDTH), pool.reshape(n_p, POOL_WIDTH), mod_p,
        w_out_b, norm_ffn[l], w_router_t, b_router[l], None,
        tile=SORT_TILE, sub=ROUTE_SUB, rows_per_mod=sp, tile0=0, extra_tiles=max(2, ROUTE_SUB))

    xs = x_sample.reshape(bs, D_MODEL)
    qs, ks, vs, us = _mixer_inputs(xs, mod_s, norm_mix[l], w_in[l], bd, q_norm[l], k_norm[l],
                                   tile=bs, rows_per_mod=bs, precise=True)
    wbuf = cache_k.shape[2]
    attn_s, nks, nvs = _attn_sample(qs, ks, vs, cache_k[l].reshape(bs, wbuf, KV_WIDTH),
                                    cache_v[l].reshape(bs, wbuf, KV_WIDTH), sinks[l], rel_bias)
    pool_s, nps_t = _pool_sample(jnp.swapaxes(state_pool[l], 0, 1), us, w_pool_b, pool_scale[l])
    x2_s, hs, slots_s, cnt_s = _route(
        xs, attn_s.astype(BF16), pool_s, mod_s, w_out_b, norm_ffn[l],
        w_router_t, b_router[l], hs, tile=bs, sub=1, rows_per_mod=bs, tile0=tiles_p, extra_tiles=0)

    cnt = jnp.concatenate([cnt_p[:, :, 0], cnt_s[:, :, 0]], axis=0).astype(jnp.int32)
    src, block_e, next_e, nblocks = _plan(cnt, n_blocks_max)
    outs = _moe(hs, src, block_e, next_e, nblocks, w1[l], b1[l], w2[l], b2[l],
                scratch_chunk=(tiles_p + 1) * TILE_CHUNKS)

    y_p = _combine(outs, slots_p, x2_p, mod_p, tile=SORT_TILE, sub=COMBINE_SUB, rows_per_mod=sp, tile0=0)
    y_s = _combine(outs, slots_s, x2_s, mod_s, tile=bs, sub=1, rows_per_mod=bs, tile0=tiles_p)

    return (y_p.reshape(bp, sp, D_MODEL), y_s.reshape(bs, 1, D_MODEL),
            nkp[None], nvp[None], npp[None],
            nks.reshape(1, bs, wbuf, N_KV_HEADS, HEAD_DIM), nvs.reshape(1, bs, wbuf, N_KV_HEADS, HEAD_DIM),
            jnp.swapaxes(nps_t, 0, 1)[None])
```

```python
import functools
import math

import jax
import jax.numpy as jnp
from jax import lax
from jax.experimental import pallas as pl
from jax.experimental.pallas import tpu as pltpu

F32 = jnp.float32
BF16 = jnp.bfloat16

D_MODEL = 1024
HEAD_DIM = 64
N_HEADS = 8
N_KV_HEADS = 2
GROUP = N_HEADS // N_KV_HEADS
ATTN_WIDTH = N_HEADS * HEAD_DIM
KV_WIDTH = N_KV_HEADS * HEAD_DIM
POOL_WIDTH = D_MODEL - ATTN_WIDTH
POOL_WINDOWS = (2, 4, 8, 16)
POOL_GROUP = POOL_WIDTH // len(POOL_WINDOWS)
POOL_BUF = max(POOL_WINDOWS) - 1
IN_WIDTH = ATTN_WIDTH + 2 * KV_WIDTH + POOL_WIDTH
WINDOW = 128
ATTN_BLOCK = 128
N_BUCKETS = 32
MAX_EXACT = 16
REL_MAX_DIST = 128
N_EXPERTS = 32
TOP_K = 4
D_FF = D_MODEL
SWIGLU_LIMIT = 7.0
SWIGLU_ALPHA = 1.702
EPS = 1e-6
NEG_INF = -1e30
PAST_LEN = 16384

LANES = 128
SUBLANES = 8
VMEM_LIMIT = 56 * 1024 * 1024

ATTN_QB = 16
ATTN_FUSE = 2
POOL_HALO = 2 * SUBLANES

SORT_TILE = 256
ROUTE_SUB = 4
COMBINE_SUB = 4
CHUNK = SUBLANES
TILE_ROWS = -(-(SORT_TILE * TOP_K + N_EXPERTS * (CHUNK - 1)) // LANES) * LANES
TILE_CHUNKS = TILE_ROWS // CHUNK
MOE_BLOCK = 256
BLOCK_CHUNKS = MOE_BLOCK // CHUNK
WEIGHT_DMA_PRIORITY = 1
PACKED_W = D_MODEL // 2
ROW_W = PACKED_W
ZERO_CHUNK = TILE_CHUNKS - 1


def _bdot(a, b):
    return jnp.dot(a.astype(BF16), b.astype(BF16), preferred_element_type=F32)


def _split(a):
    hi = a.astype(BF16)
    lo = (a - hi.astype(F32)).astype(BF16)
    return hi, lo


def _dot3(a, b):
    ah, al = _split(a)
    bh, bl = _split(b)
    d = functools.partial(jnp.dot, preferred_element_type=F32)
    return d(ah, bh) + d(al, bh) + d(ah, bl)


def _pack_rows(x):
    bits = lax.bitcast_convert_type(x, jnp.int32)
    return bits[:, :PACKED_W] | lax.shift_right_logical(bits[:, PACKED_W:], 16)


def _unpack_rows(w):
    hi = lax.bitcast_convert_type(w & jnp.int32(-65536), F32)
    lo = lax.bitcast_convert_type(lax.shift_left(w, 16), F32)
    return hi.astype(BF16), lo.astype(BF16)


def _rms(x, g):
    return x * lax.rsqrt(jnp.mean(x * x, axis=-1, keepdims=True) + EPS) * g


def _cparams(sem, **kw):
    return pltpu.CompilerParams(dimension_semantics=sem, vmem_limit_bytes=VMEM_LIMIT, **kw)


def _ada_kernel(c_ref, w_ref, b_ref, o_ref):
    c = c_ref[...]
    s = c * jax.nn.sigmoid(c)
    for v in range(o_ref.shape[0]):
        cols = slice(v * D_MODEL, (v + 1) * D_MODEL)
        o_ref[v] = _dot3(s, w_ref[:, cols]) + b_ref[:, cols]


def _modulation(c, w_ada, b_ada, *, vectors_per_step=1):
    rows = c.shape[0]
    n = w_ada.shape[1]
    nvec = n // D_MODEL
    assert nvec % vectors_per_step == 0
    tn = vectors_per_step * D_MODEL
    return pl.pallas_call(
        _ada_kernel,
        grid=(nvec // vectors_per_step,),
        in_specs=[pl.BlockSpec((rows, D_MODEL), lambda j: (0, 0)),
                  pl.BlockSpec((D_MODEL, tn), lambda j: (0, j)),
                  pl.BlockSpec((1, tn), lambda j: (0, j))],
        out_specs=pl.BlockSpec((vectors_per_step, rows, D_MODEL), lambda j: (j, 0, 0)),
        out_shape=jax.ShapeDtypeStruct((nvec, rows, D_MODEL), F32),
        compiler_params=_cparams(("parallel",)),
        name="modulation",
    )(c, w_ada, b_ada.reshape(1, n))


def _head_rms(t, bd, g, precise):
    if precise:
        hi, lo = _split(t * t)
        ss = jnp.dot(hi, bd, preferred_element_type=F32) + jnp.dot(lo, bd, preferred_element_type=F32)
    else:
        ss = _bdot(t * t, bd)
    return t * lax.rsqrt(ss * (1.0 / HEAD_DIM) + EPS) * g


def _mixin_qkv(x_ref, sh_ref, sc_ref, g_ref, w_ref, bd_ref, qn_ref, kn_ref, q_ref, k_ref, v_ref, precise):
    h = _rms(x_ref[...], g_ref[...] * (1.0 + sc_ref[...])) + sh_ref[...]
    z = _dot3(h, w_ref[...]) if precise else _bdot(h, w_ref[...])
    q = z[:, :ATTN_WIDTH]
    k = z[:, ATTN_WIDTH:ATTN_WIDTH + KV_WIDTH]
    bd = bd_ref[...]
    q = _head_rms(q, bd, qn_ref[...], precise)
    k = _head_rms(k, bd[:KV_WIDTH, :KV_WIDTH], kn_ref[...], precise)
    q_ref[...] = (q * (HEAD_DIM ** -0.5)).astype(BF16)
    k_ref[...] = k
    v_ref[...] = z[:, ATTN_WIDTH + KV_WIDTH:ATTN_WIDTH + 2 * KV_WIDTH]
    return z[:, ATTN_WIDTH + 2 * KV_WIDTH:]


def _mixin_kernel(x_ref, sh_ref, sc_ref, g_ref, w_ref, bd_ref, qn_ref, kn_ref,
                  q_ref, k_ref, v_ref, u_ref, *, precise):
    u_ref[...] = _mixin_qkv(x_ref, sh_ref, sc_ref, g_ref, w_ref, bd_ref, qn_ref, kn_ref,
                            q_ref, k_ref, v_ref, precise)


def _mixin_pool_kernel(x_ref, sh_ref, sc_ref, g_ref, w_ref, bd_ref, qn_ref, kn_ref, wp_ref, ps_ref,
                       q_ref, k_ref, v_ref, pool_ref, tail_ref, ext, lv, carry, *, tiles_per_seq):
    u = _mixin_qkv(x_ref, sh_ref, sc_ref, g_ref, w_ref, bd_ref, qn_ref, kn_ref, q_ref, k_ref, v_ref, False)
    t = pl.program_id(0) % tiles_per_seq

    @pl.when(t == 0)
    def _():
        carry[...] = jnp.zeros(carry.shape, F32)

    pool_ref[...] = _pool_tile(u, carry[...], t * u.shape[0], wp_ref, ps_ref, ext, lv)
    last = u[u.shape[0] - carry.shape[0]:, :]
    carry[...] = last
    tail_ref[...] = last


def _mod_spec(mod, k, group_of_step):
    return pl.BlockSpec((None, None, mod.shape[2], D_MODEL), lambda i: (k, group_of_step(i), 0, 0))


def _mixer_inputs(x2d, mod, norm_mix, w_in, bd, q_norm, k_norm, *, tile, rows_per_mod, precise, pool=None):
    n = x2d.shape[0]
    group = lambda i: (i * tile) // rows_per_mod
    const = lambda shape: pl.BlockSpec(shape, lambda i: (0,) * len(shape))
    row = lambda w: pl.BlockSpec((tile, w), lambda i: (i, 0))
    in_specs = [row(D_MODEL), _mod_spec(mod, 0, group), _mod_spec(mod, 1, group), const((1, D_MODEL)),
                const((D_MODEL, IN_WIDTH)), const((ATTN_WIDTH, ATTN_WIDTH)), const((1, ATTN_WIDTH)),
                const((1, KV_WIDTH))]
    args = [x2d, mod, mod, norm_mix.reshape(1, D_MODEL), w_in, bd,
            jnp.tile(q_norm, N_HEADS).reshape(1, ATTN_WIDTH), jnp.tile(k_norm, N_KV_HEADS).reshape(1, KV_WIDTH)]
    out_specs = [row(ATTN_WIDTH), row(KV_WIDTH), row(KV_WIDTH)]
    out_shape = [jax.ShapeDtypeStruct((n, ATTN_WIDTH), BF16),
                 jax.ShapeDtypeStruct((n, KV_WIDTH), F32),
                 jax.ShapeDtypeStruct((n, KV_WIDTH), F32)]
    if pool is None:
        kern = functools.partial(_mixin_kernel, precise=precise)
        out_specs.append(row(POOL_WIDTH))
        out_shape.append(jax.ShapeDtypeStruct((n, POOL_WIDTH), F32))
        scratch, semantics = [], ("parallel",)
    else:
        assert not precise and rows_per_mod % tile == 0 and tile >= POOL_HALO
        w_pool, pool_scale = pool
        kern = functools.partial(_mixin_pool_kernel, tiles_per_seq=rows_per_mod // tile)
        in_specs += [const(w_pool.shape), const((1, POOL_WIDTH))]
        args += [w_pool, pool_scale.reshape(1, POOL_WIDTH)]
        out_specs += [pl.BlockSpec((tile, POOL_WIDTH), lambda i: (i, 0)),
                      pl.BlockSpec((None, POOL_HALO, POOL_WIDTH), lambda i: (group(i), 0, 0))]
        out_shape += [jax.ShapeDtypeStruct((n, POOL_WIDTH), BF16),
                      jax.ShapeDtypeStruct((n // rows_per_mod, POOL_HALO, POOL_WIDTH), F32)]
        ext_rows = SUBLANES + POOL_HALO + tile
        scratch = [pltpu.VMEM((ext_rows, POOL_WIDTH), F32),
                   pltpu.VMEM((len(POOL_WINDOWS) - 1, ext_rows, POOL_GROUP), F32),
                   pltpu.VMEM((POOL_HALO, POOL_WIDTH), F32)]
        semantics = ("arbitrary",)
    return pl.pallas_call(
        kern,
        grid=(n // tile,),
        in_specs=in_specs,
        out_specs=out_specs,
        out_shape=out_shape,
        scratch_shapes=scratch,
        compiler_params=_cparams(semantics),
        name="mixer_inputs",
    )(*args)


def _t5_bucket(rel):
    n = jnp.maximum(rel, 0)
    nf = jnp.maximum(n, 1).astype(F32)
    large = MAX_EXACT + (jnp.log(nf / MAX_EXACT) / math.log(REL_MAX_DIST / MAX_EXACT)
                         * (N_BUCKETS - MAX_EXACT)).astype(jnp.int32)
    large = jnp.minimum(large, N_BUCKETS - 1)
    return jnp.where(n < MAX_EXACT, n, large)


def _bias_table(rel, rel_table):
    bucket = _t5_bucket(rel)
    table = rel_table.astype(F32)
    ids = jnp.arange(N_BUCKETS, dtype=bucket.dtype).reshape((N_BUCKETS, 1) + (1,) * rel.ndim)
    onehot = bucket[None, None] == ids
    bias = jnp.sum(jnp.where(onehot, table.reshape(table.shape + (1,) * rel.ndim), 0.0), axis=0)
    valid = (rel >= 0) & (rel < WINDOW)
    return jnp.where(valid[None], bias, NEG_INF)


def _attn_prompt_kernel(sink_ref, q_ref, kp_ref, kc_ref, vp_ref, vc_ref, bias_ref, o_ref):
    first = pl.program_id(1) == 0
    kk = jnp.concatenate([kp_ref[...], kc_ref[...]], axis=0).astype(BF16)
    vv = jnp.concatenate([vp_ref[...], vc_ref[...]], axis=0).astype(BF16)
    key = lax.broadcasted_iota(jnp.int32, (2 * ATTN_BLOCK, 1), 0)
    no_prev = jnp.logical_and(first, key < ATTN_BLOCK)
    lane = lax.broadcasted_iota(jnp.int32, (1, N_HEADS * ATTN_BLOCK), 1)
    sink = jnp.zeros((1, N_HEADS * ATTN_BLOCK), F32)
    for h in range(N_HEADS):
        sink = jnp.where(lane // ATTN_BLOCK == h, sink_ref[h], sink)
    contract = lambda a, b, dims: lax.dot_general(a, b, (dims, ((), ())), preferred_element_type=F32)
    part = GROUP * ATTN_BLOCK
    width = N_HEADS * ATTN_BLOCK
    bias_g = jnp.concatenate([bias_ref[...]] * ATTN_FUSE, axis=1)
    sink_g = jnp.concatenate([sink] * ATTN_FUSE, axis=1)
    lane_g = lax.broadcasted_iota(jnp.int32, (1, ATTN_FUSE * width), 1)
    for i0 in range(0, ATTN_QB, ATTN_FUSE):
        blocks = range(i0, i0 + ATTN_FUSE)
        scores = []
        for i in blocks:
            q = q_ref[i * ATTN_BLOCK:(i + 1) * ATTN_BLOCK, :]
            for kv in range(N_KV_HEADS):
                heads = range(kv * GROUP, (kv + 1) * GROUP)
                qg = jnp.concatenate([q[:, h * HEAD_DIM:(h + 1) * HEAD_DIM] for h in heads], axis=0)
                scores.append(contract(kk[i * ATTN_BLOCK:(i + 2) * ATTN_BLOCK, kv * HEAD_DIM:(kv + 1) * HEAD_DIM],
                                       qg, ((1,), (1,))))
        s = jnp.concatenate(scores, axis=1) + bias_g
        if i0 == 0:
            s = jnp.where(jnp.logical_and(no_prev, lane_g < width), NEG_INF, s)
        m = jnp.maximum(jnp.max(s, axis=0, keepdims=True), sink_g)
        p = jnp.exp(s - m)
        denom = jnp.sum(p, axis=0, keepdims=True) + jnp.exp(sink_g - m)
        p = p.astype(BF16)
        for n, i in enumerate(blocks):
            keys = slice(i * ATTN_BLOCK, (i + 2) * ATTN_BLOCK)
            halves = []
            for kv in range(N_KV_HEADS):
                cols = slice(n * width + kv * part, n * width + (kv + 1) * part)
                halves.append(contract(vv[keys, kv * HEAD_DIM:(kv + 1) * HEAD_DIM], p[:, cols], ((0,), (0,)))
                              / denom[:, cols])
            o_t = jnp.concatenate(halves, axis=0)
            per_g = [o_t[:, g * ATTN_BLOCK:(g + 1) * ATTN_BLOCK].T for g in range(GROUP)]
            out = [t[:, kv * HEAD_DIM:(kv + 1) * HEAD_DIM] for kv in range(N_KV_HEADS) for t in per_g]
            o_ref[i * ATTN_BLOCK:(i + 1) * ATTN_BLOCK, :] = jnp.concatenate(out, axis=-1).astype(BF16)


def _attn_prompt(q, k, v, sinks, rel_table):
    b, s = q.shape[:2]
    qrows = ATTN_QB * ATTN_BLOCK
    assert s % qrows == 0
    qi = jnp.arange(ATTN_BLOCK, dtype=jnp.int32)[:, None]
    si = jnp.arange(2 * ATTN_BLOCK, dtype=jnp.int32)[None, :]
    bias = _bias_table(qi + ATTN_BLOCK - si, rel_table)
    bias = bias.reshape(N_HEADS * ATTN_BLOCK, 2 * ATTN_BLOCK).T
    cur = lambda w: pl.BlockSpec((None, qrows, w), lambda i, j, *_: (i, j, 0))
    prev = lambda w: pl.BlockSpec((None, ATTN_BLOCK, w),
                                  lambda i, j, *_: (i, jnp.maximum(j * ATTN_QB - 1, 0), 0))
    return pl.pallas_call(
        _attn_prompt_kernel,
        grid_spec=pltpu.PrefetchScalarGridSpec(
            num_scalar_prefetch=1,
            grid=(b, s // qrows),
            in_specs=[cur(ATTN_WIDTH), prev(KV_WIDTH), cur(KV_WIDTH), prev(KV_WIDTH), cur(KV_WIDTH),
                      pl.BlockSpec(bias.shape, lambda i, j, *_: (0, 0))],
            out_specs=cur(ATTN_WIDTH)),
        out_shape=jax.ShapeDtypeStruct((b, s, ATTN_WIDTH), BF16),
        compiler_params=_cparams(("parallel", "parallel")),
        name="attn_prompt",
    )(sinks.astype(F32), q, k, k, v, v, bias)


def _attn_sample_kernel(sink_ref, q_ref, kc_ref, vc_ref, kn_ref, vn_ref, bias_ref, bnew_ref,
                        o_ref, nk_ref, nv_ref):
    kc = kc_ref[...]
    vc = vc_ref[...]
    kn = kn_ref[...]
    vn = vn_ref[...]
    w = kc.shape[1]
    pos = lax.broadcasted_iota(jnp.int32, kc.shape, 1)
    nk_ref[...] = jnp.where(pos == w - 1, kn[:, None, :], pltpu.roll(kc, w - 1, 1))
    nv_ref[...] = jnp.where(pos == w - 1, vn[:, None, :], pltpu.roll(vc, w - 1, 1))
    gi = lax.broadcasted_iota(jnp.int32, (1, GROUP, 1), 1)
    for kv in range(N_KV_HEADS):
        sl = slice(kv * HEAD_DIM, (kv + 1) * HEAD_DIM)
        qg = q_ref[:, kv]
        s = jnp.einsum('bgd,bsd->bgs', qg, kc[:, :, sl].astype(BF16), preferred_element_type=F32)
        s = s + bias_ref[kv][None]
        s_new = jnp.sum(qg.astype(F32) * kn[:, None, sl], axis=-1, keepdims=True) + bnew_ref[kv][None]
        sink = jnp.zeros((1, GROUP, 1), F32)
        for g in range(GROUP):
            sink = jnp.where(gi == g, sink_ref[kv * GROUP + g], sink)
        m = jnp.maximum(jnp.maximum(jnp.max(s, axis=-1, keepdims=True), s_new), sink)
        p = jnp.exp(s - m)
        p_new = jnp.exp(s_new - m)
        denom = jnp.sum(p, axis=-1, keepdims=True) + p_new + jnp.exp(sink - m)
        o = jnp.einsum('bgs,bsd->bgd', p.astype(BF16), vc[:, :, sl].astype(BF16), preferred_element_type=F32)
        o = o + p_new * vn[:, None, sl]
        o_ref[:, kv] = o / denom


def _attn_sample(q, k_new, v_new, cache_k, cache_v, sinks, rel_table, *, tile=32):
    bd, w = cache_k.shape[:2]
    rel = w - jnp.arange(w, dtype=jnp.int32)
    bias = _bias_table(rel, rel_table).reshape(N_KV_HEADS, GROUP, w)
    bnew = _bias_table(jnp.zeros((1,), jnp.int32), rel_table).reshape(N_KV_HEADS, GROUP, 1)
    q4 = q.reshape(bd, N_KV_HEADS, GROUP, HEAD_DIM)
    spec4 = pl.BlockSpec((tile, N_KV_HEADS, GROUP, HEAD_DIM), lambda i, *_: (i, 0, 0, 0))
    cache = pl.BlockSpec((tile, w, KV_WIDTH), lambda i, *_: (i, 0, 0))
    new = pl.BlockSpec((tile, KV_WIDTH), lambda i, *_: (i, 0))
    const3 = lambda a: pl.BlockSpec(a.shape, lambda i, *_: (0, 0, 0))
    o, nk, nv = pl.pallas_call(
        _attn_sample_kernel,
        grid_spec=pltpu.PrefetchScalarGridSpec(
            num_scalar_prefetch=1,
            grid=(bd // tile,),
            in_specs=[spec4, cache, cache, new, new, const3(bias), const3(bnew)],
            out_specs=[spec4, cache, cache]),
        out_shape=[jax.ShapeDtypeStruct(q4.shape, F32),
                   jax.ShapeDtypeStruct(cache_k.shape, F32),
                   jax.ShapeDtypeStruct(cache_v.shape, F32)],
        compiler_params=_cparams(("parallel",)),
        name="attn_sample",
    )(sinks.astype(F32), q4, cache_k, cache_v, k_new, v_new, bias, bnew)
    return o.reshape(bd, ATTN_WIDTH), nk, nv


def _pool_project(d_groups, wp_ref, ps_ref):
    out = [_bdot(d, wp_ref[g]) for g, d in enumerate(d_groups)]
    return (jnp.concatenate(out, axis=-1) * ps_ref[...]).astype(BF16)


def _pool_tile(u, halo, pos0, wp_ref, ps_ref, ext, lv):
    tile = u.shape[0]
    lead, hb = SUBLANES, POOL_HALO
    ext[0:lead, :] = jnp.zeros((lead, ext.shape[1]), F32)
    ext[lead:lead + hb, :] = halo
    ext[lead + hb:, :] = u
    lv[:, 0:lead, :] = jnp.zeros((lv.shape[0], lead, lv.shape[2]), F32)
    pos = pos0 + lax.broadcasted_iota(jnp.int32, (tile, 1), 0)
    n = hb + tile
    ds = []
    for g, w in enumerate(POOL_WINDOWS):
        sl = slice(g * POOL_GROUP, (g + 1) * POOL_GROUP)
        acc = ext[lead:lead + n, sl] + ext[lead - 1:lead - 1 + n, sl]
        span, level = 2, 0
        while span < w:
            lv[level, lead:lead + n, :] = acc
            acc = acc + lv[level, lead - span:lead - span + n, :]
            span, level = 2 * span, level + 1
        cnt = jnp.minimum(pos + 1, w).astype(F32)
        ds.append(acc[hb:] / cnt - ext[lead + hb:lead + hb + tile, sl])
    return _pool_project(ds, wp_ref, ps_ref)


def _pool_sample_kernel(st_ref, u_ref, wp_ref, ps_ref, o_ref, ns_ref):
    u = u_ref[...]
    ns_ref[0:POOL_BUF - 1] = st_ref[1:POOL_BUF]
    ns_ref[POOL_BUF - 1] = u
    ds = []
    for g, w in enumerate(POOL_WINDOWS):
        sl = slice(g * POOL_GROUP, (g + 1) * POOL_GROUP)
        acc = u[:, sl]
        for j in range(1, w):
            acc = acc + st_ref[POOL_BUF - j][:, sl]
        cnt = float(min(PAST_LEN + 1, w))
        ds.append(acc / cnt - u[:, sl])
    o_ref[...] = _pool_project(ds, wp_ref, ps_ref)


def _pool_sample(state_t, u, w_pool, pool_scale):
    nb, bd, c = state_t.shape
    full = lambda a: pl.BlockSpec(a.shape, lambda: (0,) * a.ndim)
    ps = pool_scale.reshape(1, c)
    return pl.pallas_call(
        _pool_sample_kernel,
        in_specs=[full(state_t), full(u), full(w_pool), full(ps)],
        out_specs=[pl.BlockSpec((bd, c), lambda: (0, 0)), full(state_t)],
        out_shape=[jax.ShapeDtypeStruct((bd, c), BF16), jax.ShapeDtypeStruct(state_t.shape, F32)],
        compiler_params=pltpu.CompilerParams(vmem_limit_bytes=VMEM_LIMIT),
        name="pool_sample",
    )(state_t, u, w_pool, ps)


def _route_kernel(*refs, own_steps):
    hs_ref = refs[-3]

    @pl.when(pl.program_id(0) < own_steps)
    def _():
        _route_tiles(*refs)

    @pl.when(pl.program_id(0) >= own_steps)
    def _():
        hs_ref[...] = jnp.zeros(hs_ref.shape, jnp.int32)


def _route_tiles(x_ref, attn_ref, pool_ref, gm_ref, sh_ref, sc_ref, wo_ref, nf_ref, wr_ref, br_ref,
                 tri_t_ref, tri_e_ref,
                 x2_ref, hs_ref, slot_ref, cnt_ref):
    sub = slot_ref.shape[0]
    n = x_ref.shape[0]
    tile = n // sub
    mixed = jnp.concatenate([attn_ref[...], pool_ref[...]], axis=1)
    mix = jnp.dot(mixed, wo_ref[...], preferred_element_type=F32)
    x2 = x_ref[...] + gm_ref[...] * mix
    x2_ref[...] = x2
    h = _rms(x2, nf_ref[...] * (1.0 + sc_ref[...])) + sh_ref[...]

    hh, hl = _split(h)
    wh, wl = _split(wr_ref[...])
    nt = functools.partial(lax.dot_general, dimension_numbers=(((1,), (1,)), ((), ())),
                           preferred_element_type=F32)
    logits = nt(wh, hh) + nt(wl, hh) + nt(wh, hl) + br_ref[...]

    eidx = lax.broadcasted_iota(jnp.int32, (N_EXPERTS, n), 0).astype(F32)
    work = logits
    tops, picks = [], []
    for _ in range(TOP_K):
        m = jnp.max(work, axis=0, keepdims=True)
        pick = jnp.min(jnp.where(work == m, eidx, float(N_EXPERTS)), axis=0, keepdims=True)
        work = jnp.where(eidx == pick, -jnp.inf, work)
        tops.append(m)
        picks.append(pick)
    ex = [jnp.exp(v - tops[0]) for v in tops]
    den = ex[0] + ex[1] + ex[2] + ex[3]
    gates = [e / den for e in ex]

    sel = jnp.zeros((N_EXPERTS, n), F32)
    for pick in picks:
        sel = sel + (eidx == pick).astype(F32)
    selb = sel.astype(BF16)
    rank = jnp.concatenate([jnp.dot(selb[:, t * tile:(t + 1) * tile], tri_t_ref[...], preferred_element_type=F32)
                            for t in range(sub)], axis=1)
    cnts = [jnp.sum(sel[:, t * tile:(t + 1) * tile], axis=1, keepdims=True) for t in range(sub)]
    padded = jnp.concatenate(
        [jnp.broadcast_to(jnp.ceil(c * (1.0 / CHUNK)) * CHUNK, (N_EXPERTS, LANES)) for c in cnts], axis=1)
    seg = jnp.dot(tri_e_ref[...], padded.astype(BF16), preferred_element_type=F32)
    dest = jnp.concatenate([seg[:, t * LANES:t * LANES + 1] + rank[:, t * tile:(t + 1) * tile]
                            for t in range(sub)], axis=1)
    slots = [jnp.sum(jnp.where(eidx == pick, dest, 0.0), axis=0, keepdims=True) for pick in picks]

    ridx = lax.broadcasted_iota(jnp.int32, (TILE_ROWS, tile), 0).astype(jnp.int16)
    slots16 = [v.astype(jnp.int32).astype(jnp.int16) for v in slots]
    for t in range(sub):
        cols = slice(t * tile, (t + 1) * tile)
        cnt_ref[t] = jnp.broadcast_to(cnts[t], (N_EXPERTS, LANES))
        slot_ref[t] = jnp.concatenate([v[:, cols] for v in slots + gates], axis=0)
        hit = ridx == slots16[0][:, cols]
        for s in slots16[1:]:
            hit = jnp.logical_or(hit, ridx == s[:, cols])
        perm = jnp.where(hit, jnp.ones((), BF16), jnp.zeros((), BF16))
        hs_ref[pl.ds(t * TILE_ROWS, TILE_ROWS), :] = _pack_rows(
            jnp.dot(perm, hh[t * tile:(t + 1) * tile, :], preferred_element_type=F32))


def _route(x2d, attn, pool, mod, w_out, norm_ffn, w_router_t, b_router, hs_prev, *, tile, sub,
           rows_per_mod, tile0, extra_tiles):
    n = x2d.shape[0]
    nt = n // tile
    assert nt % sub == 0 and extra_tiles % sub == 0
    own_steps = nt // sub
    steps = own_steps + extra_tiles // sub
    mrows = mod.shape[2]
    assert mrows == 1 or (sub == 1 and mrows == tile)
    last = lambda i: jnp.minimum(i, own_steps - 1)
    group = lambda i: (last(i) * sub * tile) // rows_per_mod
    const = lambda shape: pl.BlockSpec(shape, lambda i: (0,) * len(shape))
    row = lambda w: pl.BlockSpec((sub * tile, w), lambda i: (last(i), 0))
    tri_t = (jnp.arange(tile)[:, None] < jnp.arange(tile)[None, :]).astype(BF16)
    tri_e = (jnp.arange(N_EXPERTS)[None, :] < jnp.arange(N_EXPERTS)[:, None]).astype(BF16)
    in_specs = [row(D_MODEL), row(ATTN_WIDTH), row(POOL_WIDTH),
                _mod_spec(mod, 2, group), _mod_spec(mod, 3, group), _mod_spec(mod, 4, group),
                const((D_MODEL, D_MODEL)), const((1, D_MODEL)), const((N_EXPERTS, D_MODEL)),
                const((N_EXPERTS, 1)), const((tile, tile)), const((N_EXPERTS, N_EXPERTS))]
    args = [x2d, attn, pool, mod, mod, mod, w_out, norm_ffn.reshape(1, D_MODEL), w_router_t,
            b_router.reshape(N_EXPERTS, 1), tri_t, tri_e]
    n_in = len(args)
    kern = functools.partial(_route_kernel, own_steps=own_steps)
    aliases = {}
    hs_rows = (tile0 + steps * sub) * TILE_ROWS
    assert tile0 % sub == 0
    if hs_prev is not None:
        in_specs.append(pl.BlockSpec(memory_space=pl.ANY))
        args.append(hs_prev)
        aliases = {n_in: 1}
        kern = lambda *refs: _route_kernel(*refs[:n_in], *refs[n_in + 1:], own_steps=own_steps)
        hs_rows = hs_prev.shape[0]
    return pl.pallas_call(
        kern,
        grid=(steps,),
        in_specs=in_specs,
        out_specs=[row(D_MODEL),
                   pl.BlockSpec((sub * TILE_ROWS, ROW_W), lambda i: (i + tile0 // sub, 0)),
                   pl.BlockSpec((sub, 2 * TOP_K, tile), lambda i: (last(i), 0, 0)),
                   pl.BlockSpec((sub, N_EXPERTS, LANES), lambda i: (last(i), 0, 0))],
        out_shape=[jax.ShapeDtypeStruct((n, D_MODEL), F32),
                   jax.ShapeDtypeStruct((hs_rows, ROW_W), jnp.int32),
                   jax.ShapeDtypeStruct((nt, 2 * TOP_K, tile), F32),
                   jax.ShapeDtypeStruct((nt, N_EXPERTS, LANES), F32)],
        input_output_aliases=aliases,
        compiler_params=_cparams(("arbitrary",)),
        name="route",
    )(*args)


def _moe_kernel(src_ref, be_ref, nxt_ref, nb_ref, hs_hbm, w1_hbm, b1_ref, w2_hbm, b2_ref, out_hbm,
                lhs, obuf, w1s, w2s, w1c, w2c, sem_in, sem_out, sem_w, *, scratch_chunk):
    nb = nb_ref[0]

    def weight_copies(e):
        return (pltpu.make_async_copy(w1_hbm.at[e], w1s, sem_w.at[0]),
                pltpu.make_async_copy(w2_hbm.at[e], w2s, sem_w.at[1]))

    def chunk_rows(c):
        return pl.ds(pl.multiple_of(c * CHUNK, CHUNK), CHUNK)

    def start_in(blk, s):
        for j in range(BLOCK_CHUNKS):
            c = src_ref[blk * BLOCK_CHUNKS + j]
            c = jnp.where(c < 0, ZERO_CHUNK, c)
            pltpu.make_async_copy(hs_hbm.at[chunk_rows(c)], lhs.at[s, pl.ds(j * CHUNK, CHUNK)],
                                  sem_in.at[s]).start()

    def wait_in(s):
        pltpu.make_async_copy(hs_hbm.at[pl.ds(0, MOE_BLOCK)], lhs.at[s], sem_in.at[s]).wait()

    def start_out(blk, s):
        for j in range(BLOCK_CHUNKS):
            c = src_ref[blk * BLOCK_CHUNKS + j]
            c = jnp.where(c < 0, scratch_chunk + s * BLOCK_CHUNKS + j, c)
            pltpu.make_async_copy(obuf.at[s, pl.ds(j * CHUNK, CHUNK)],
                                  out_hbm.at[chunk_rows(c), pl.ds(0, PACKED_W)], sem_out.at[s]).start()

    def wait_out(s):
        pltpu.make_async_copy(obuf.at[s], out_hbm.at[pl.ds(0, MOE_BLOCK), pl.ds(0, PACKED_W)],
                              sem_out.at[s]).wait()

    @pl.when(nb > 0)
    def _():
        start_in(0, 0)
        for cp in weight_copies(be_ref[0]):
            cp.start(priority=WEIGHT_DMA_PRIORITY)

    def block(b, carry):
        slot = b % 2
        e = be_ref[b]

        @pl.when(jnp.logical_or(b == 0, e != be_ref[jnp.maximum(b - 1, 0)]))
        def _():
            for cp in weight_copies(e):
                cp.wait()
            w1c[...] = w1s[...].astype(BF16)
            w2c[...] = w2s[...].astype(BF16)
            nxt = nxt_ref[b]

            @pl.when(nxt >= 0)
            def _():
                for cp in weight_copies(nxt):
                    cp.start(priority=WEIGHT_DMA_PRIORITY)

        wait_in(slot)

        @pl.when(b >= 2)
        def _():
            wait_out(slot)

        start_in(b + 1, 1 - slot)

        def ffn(nrows):
            xh, xl = _unpack_rows(lhs[slot, 0:nrows, :])
            x = jnp.concatenate([xh, xl], axis=1)
            gu = jnp.dot(x, w1c[...], preferred_element_type=F32) + b1_ref[e]
            gate = jnp.minimum(gu[:, :D_FF], SWIGLU_LIMIT)
            up = jnp.clip(gu[:, D_FF:], -SWIGLU_LIMIT, SWIGLU_LIMIT)
            act = (up + 1.0) * (gate * jax.nn.sigmoid(SWIGLU_ALPHA * gate))
            y = jnp.dot(act.astype(BF16), w2c[...], preferred_element_type=F32) + b2_ref[e]
            obuf[slot, 0:nrows, :] = _pack_rows(y.astype(BF16).astype(F32))

        real = nb_ref[1 + b]

        @pl.when(real > BLOCK_CHUNKS // 2)
        def _():
            ffn(MOE_BLOCK)

        @pl.when(real <= BLOCK_CHUNKS // 2)
        def _():
            ffn(MOE_BLOCK // 2)

        start_out(b, slot)
        return carry

    obuf[...] = jnp.zeros(obuf.shape, jnp.int32)
    lax.fori_loop(0, nb, block, 0)

    @pl.when(nb > 0)
    def _():
        last_slot = (nb - 1) % 2
        wait_in(1 - last_slot)

        @pl.when(nb >= 2)
        def _():
            wait_out(1 - last_slot)
        wait_out(last_slot)


def _moe(hs, src, block_e, next_e, nblocks, w1, b1, w2, b2, scratch_chunk):
    full = lambda shape: pl.BlockSpec(shape, lambda i, *_: (0,) * len(shape))
    hbm = pl.BlockSpec(memory_space=pl.ANY)
    return pl.pallas_call(
        functools.partial(_moe_kernel, scratch_chunk=scratch_chunk),
        grid_spec=pltpu.PrefetchScalarGridSpec(
            num_scalar_prefetch=4,
            grid=(1,),
            in_specs=[hbm, hbm, full((N_EXPERTS, 1, 2 * D_FF)), hbm, full((N_EXPERTS, 1, D_MODEL))],
            out_specs=hbm,
            scratch_shapes=[pltpu.VMEM((2, MOE_BLOCK, ROW_W), jnp.int32),
                            pltpu.VMEM((2, MOE_BLOCK, PACKED_W), jnp.int32),
                            pltpu.VMEM((D_MODEL, 2 * D_FF), F32),
                            pltpu.VMEM((D_FF, D_MODEL), F32),
                            pltpu.VMEM((D_MODEL, 2 * D_FF), BF16),
                            pltpu.VMEM((D_FF, D_MODEL), BF16),
                            pltpu.SemaphoreType.DMA((2,)),
                            pltpu.SemaphoreType.DMA((2,)),
                            pltpu.SemaphoreType.DMA((2,))]),
        out_shape=jax.ShapeDtypeStruct(hs.shape, jnp.int32),
        input_output_aliases={4: 0},
        compiler_params=_cparams(("arbitrary",)),
        name="moe_experts",
    )(src, block_e, next_e, nblocks, hs, w1, b1.reshape(N_EXPERTS, 1, 2 * D_FF), w2,
      b2.reshape(N_EXPERTS, 1, D_MODEL))


def _plan(cnt, n_blocks_max):
    nt = cnt.shape[0]
    nch = (cnt + (CHUNK - 1)) // CHUNK
    lstart = jnp.cumsum(nch, axis=1) - nch
    ne = jnp.sum(nch, axis=0)
    nbe = (ne + (BLOCK_CHUNKS - 1)) // BLOCK_CHUNKS
    bend = jnp.cumsum(nbe)
    nblocks = bend[-1]
    gstart = (bend - nbe)[None, :] * BLOCK_CHUNKS + (jnp.cumsum(nch, axis=0) - nch)
    s0 = jnp.arange(nt, dtype=jnp.int32)[:, None] * TILE_CHUNKS + lstart
    blk = jnp.arange(n_blocks_max + 1, dtype=jnp.int32)
    be = jnp.sum((blk[:, None] >= bend[None, :]).astype(jnp.int32), axis=1)
    be_last = jnp.sum((nblocks - 1 >= bend).astype(jnp.int32))
    be = jnp.minimum(be, be_last).astype(jnp.int32)
    eid = jnp.arange(N_EXPERTS, dtype=jnp.int32)
    strips = jnp.stack([gstart.T, nch.T, s0.T])
    mine = (be[:, None] == eid[None, :])[None, :, :, None]
    gs_b, nc_b, s0_b = jnp.sum(jnp.where(mine, strips[:, None], 0), axis=2)
    c = (blk[:, None] * BLOCK_CHUNKS + jnp.arange(BLOCK_CHUNKS, dtype=jnp.int32)[None, :])[:, :, None]
    inside = jnp.logical_and(c >= gs_b[:, None, :], c < (gs_b + nc_b)[:, None, :])
    src = (jnp.sum(jnp.where(inside, (s0_b - gs_b)[:, None, :] + c + 1, 0), axis=2) - 1).reshape(-1)
    be = be[:n_blocks_max]
    later = jnp.logical_and(eid[None, :] > be[:, None], (nbe > 0)[None, :])
    nxt = jnp.min(jnp.where(later, eid[None, :], N_EXPERTS), axis=1)
    nxt = jnp.where(nxt == N_EXPERTS, -1, nxt).astype(jnp.int32)
    real = jnp.sum((src.reshape(-1, BLOCK_CHUNKS) >= 0).astype(jnp.int32), axis=1)
    return src, be, nxt, jnp.concatenate([nblocks.reshape(1), real]).astype(jnp.int32)


def _combine_kernel(o_ref, slot_ref, x2_ref, gf_ref, y_ref):
    sub = slot_ref.shape[0]
    tile = x2_ref.shape[0] // sub
    ridx = lax.broadcasted_iota(jnp.int32, (TILE_ROWS, tile), 0).astype(F32)
    tn = functools.partial(lax.dot_general, dimension_numbers=(((0,), (0,)), ((), ())),
                           preferred_element_type=F32)
    for t in range(sub):
        gmat = jnp.zeros((TILE_ROWS, tile), F32)
        for k in range(TOP_K):
            gmat = jnp.where(ridx == slot_ref[t, k:k + 1, :], slot_ref[t, TOP_K + k:TOP_K + k + 1, :], gmat)
        gb = gmat.astype(BF16)
        oh, ol = _unpack_rows(o_ref[pl.ds(t * TILE_ROWS, TILE_ROWS), :])
        y = jnp.concatenate([tn(gb, oh), tn(gb, ol)], axis=1)
        rows = pl.ds(t * tile, tile)
        gf = gf_ref[...] if gf_ref.shape[0] == 1 else gf_ref[rows, :]
        y_ref[rows, :] = x2_ref[rows, :] + gf * y


def _combine(outs, slots, x2, mod, *, tile, sub, rows_per_mod, tile0):
    n = x2.shape[0]
    mrows = mod.shape[2]
    assert n % (sub * tile) == 0 and tile0 % sub == 0 and (mrows == 1 or sub == 1)
    return pl.pallas_call(
        _combine_kernel,
        grid=(n // (sub * tile),),
        in_specs=[pl.BlockSpec((sub * TILE_ROWS, PACKED_W), lambda i: (i + tile0 // sub, 0)),
                  pl.BlockSpec((sub, 2 * TOP_K, tile), lambda i: (i, 0, 0)),
                  pl.BlockSpec((sub * tile, D_MODEL), lambda i: (i, 0)),
                  _mod_spec(mod, 5, lambda i: (i * sub * tile) // rows_per_mod)],
        out_specs=pl.BlockSpec((sub * tile, D_MODEL), lambda i: (i, 0)),
        out_shape=jax.ShapeDtypeStruct((n, D_MODEL), F32),
        compiler_params=_cparams(("parallel",)),
        name="combine",
    )(outs, slots, x2, mod)


def kernel(x_prompt, x_sample, cache_k, cache_v, state_pool, c_prompt, c_sample, rel_bias, norm_mix, w_ada,
           b_ada, w_in, q_norm, k_norm, sinks, w_pool, pool_scale, w_out, norm_ffn, w_router, b_router,
           w1, b1, w2, b2):
    depth = w_in.shape[0]
    assert depth == 1
    l = 0
    bp, sp, _ = x_prompt.shape
    bs = x_sample.shape[0]
    assert x_sample.shape[1] == 1 and sp % SORT_TILE == 0 and bs <= SORT_TILE
    n_p = bp * sp
    tiles_p = n_p // SORT_TILE
    max_chunks = tiles_p * (SORT_TILE * TOP_K // CHUNK + N_EXPERTS) + (bs * TOP_K // CHUNK + N_EXPERTS)
    n_blocks_max = -(-max_chunks // BLOCK_CHUNKS) + N_EXPERTS

    pr = -(-bp // SUBLANES) * SUBLANES
    c_all = jnp.concatenate([c_prompt, jnp.zeros((pr - bp, D_MODEL), F32), c_sample], axis=0)
    mod = _modulation(c_all, w_ada[l], b_ada[l])
    mod_p = mod[:, :bp].reshape(6, bp, 1, D_MODEL)
    mod_s = mod[:, pr:].reshape(6, 1, bs, D_MODEL)

    head = jnp.arange(ATTN_WIDTH) // HEAD_DIM
    bd = (head[:, None] == head[None, :]).astype(BF16)
    w_in_b = w_in[l].astype(BF16)
    w_out_b = w_out[l].astype(BF16)
    w_pool_b = w_pool[l].astype(BF16)
    w_router_t = w_router[l].T

    xp = x_prompt.reshape(n_p, D_MODEL)
    q, k, v, pool, u_tail = _mixer_inputs(xp, mod_p, norm_mix[l], w_in_b, bd, q_norm[l], k_norm[l],
                                          tile=1024, rows_per_mod=sp, precise=False,
                                          pool=(w_pool_b, pool_scale[l]))
    k3 = k.reshape(bp, sp, KV_WIDTH)
    v3 = v.reshape(bp, sp, KV_WIDTH)
    attn = _attn_prompt(q.reshape(bp, sp, ATTN_WIDTH), k3, v3, sinks[l], rel_bias)
    keep = min(WINDOW, sp)
    nkp = k3[:, -keep:].reshape(bp, keep, N_KV_HEADS, HEAD_DIM)
    nvp = v3[:, -keep:].reshape(bp, keep, N_KV_HEADS, HEAD_DIM)
    npp = u_tail[:, POOL_HALO - POOL_BUF:]
    x2_p, hs, slots_p, cnt_p = _route(
        xp, attn.reshape(n_p, ATTN_WIDTH), pool.reshape(n_p, POOL_WIDTH), mod_p,
        w_out_b, norm_ffn[l], w_router_t, b_router[l], None,
        tile=SORT_TILE, sub=ROUTE_SUB, rows_per_mod=sp, tile0=0, extra_tiles=max(2, ROUTE_SUB))

    xs = x_sample.reshape(bs, D_MODEL)
    qs, ks, vs, us = _mixer_inputs(xs, mod_s, norm_mix[l], w_in[l], bd, q_norm[l], k_norm[l],
                                   tile=bs, rows_per_mod=bs, precise=True)
    wbuf = cache_k.shape[2]
    attn_s, nks, nvs = _attn_sample(qs, ks, vs, cache_k[l].reshape(bs, wbuf, KV_WIDTH),
                                    cache_v[l].reshape(bs, wbuf, KV_WIDTH), sinks[l], rel_bias)
    pool_s, nps_t = _pool_sample(jnp.swapaxes(state_pool[l], 0, 1), us, w_pool_b, pool_scale[l])
    x2_s, hs, slots_s, cnt_s = _route(
        xs, attn_s.astype(BF16), pool_s, mod_s, w_out_b, norm_ffn[l],
        w_router_t, b_router[l], hs, tile=bs, sub=1, rows_per_mod=bs, tile0=tiles_p, extra_tiles=0)

    cnt = jnp.concatenate([cnt_p[:, :, 0], cnt_s[:, :, 0]], axis=0).astype(jnp.int32)
    src, block_e, next_e, nblocks = _plan(cnt, n_blocks_max)
    outs = _moe(hs, src, block_e, next_e, nblocks, w1[l], b1[l], w2[l], b2[l],
                scratch_chunk=(tiles_p + 1) * TILE_CHUNKS)

    y_p = _combine(outs, slots_p, x2_p, mod_p, tile=SORT_TILE, sub=COMBINE_SUB, rows_per_mod=sp, tile0=0)
    y_s = _combine(outs, slots_s, x2_s, mod_s, tile=bs, sub=1, rows_per_mod=bs, tile0=tiles_p)

    return (y_p.reshape(bp, sp, D_MODEL), y_s.reshape(bs, 1, D_MODEL),
            nkp[None], nvp[None], npp[None],
            nks.reshape(1, bs, wbuf, N_KV_HEADS, HEAD_DIM), nvs.reshape(1, bs, wbuf, N_KV_HEADS, HEAD_DIM),
            jnp.swapaxes(nps_t, 0, 1)[None])
```

```python
import functools
import math

import jax
import jax.numpy as jnp
from jax import lax
from jax.experimental import pallas as pl
from jax.experimental.pallas import tpu as pltpu

F32 = jnp.float32
BF16 = jnp.bfloat16

D_MODEL = 1024
HEAD_DIM = 64
N_HEADS = 8
N_KV_HEADS = 2
GROUP = N_HEADS // N_KV_HEADS
ATTN_WIDTH = N_HEADS * HEAD_DIM
KV_WIDTH = N_KV_HEADS * HEAD_DIM
POOL_WIDTH = D_MODEL - ATTN_WIDTH
POOL_WINDOWS = (2, 4, 8, 16)
POOL_GROUP = POOL_WIDTH // len(POOL_WINDOWS)
POOL_BUF = max(POOL_WINDOWS) - 1
IN_WIDTH = ATTN_WIDTH + 2 * KV_WIDTH + POOL_WIDTH
WINDOW = 128
ATTN_BLOCK = 128
N_BUCKETS = 32
MAX_EXACT = 16
REL_MAX_DIST = 128
N_EXPERTS = 32
TOP_K = 4
D_FF = D_MODEL
SWIGLU_LIMIT = 7.0
SWIGLU_ALPHA = 1.702
EPS = 1e-6
NEG_INF = -1e30
PAST_LEN = 16384

LANES = 128
SUBLANES = 8
VMEM_LIMIT = 56 * 1024 * 1024

ATTN_QB = 16
ATTN_FUSE = 2
POOL_HALO = 2 * SUBLANES

SORT_TILE = 256
ROUTE_SUB = 4
COMBINE_SUB = 4
CHUNK = SUBLANES
TILE_ROWS = -(-(SORT_TILE * TOP_K + N_EXPERTS * (CHUNK - 1)) // LANES) * LANES
TILE_CHUNKS = TILE_ROWS // CHUNK
MOE_BLOCK = 256
BLOCK_CHUNKS = MOE_BLOCK // CHUNK
WEIGHT_DMA_PRIORITY = 1
CAST_SLICES = 8
CAST_FIRST = 2
CAST_ROWS = D_FF // CAST_SLICES
PACKED_W = D_MODEL // 2
ROW_W = PACKED_W
ZERO_CHUNK = TILE_CHUNKS - 1


def _bdot(a, b):
    return jnp.dot(a.astype(BF16), b.astype(BF16), preferred_element_type=F32)


def _split(a):
    hi = a.astype(BF16)
    lo = (a - hi.astype(F32)).astype(BF16)
    return hi, lo


def _dot3(a, b):
    ah, al = _split(a)
    bh, bl = _split(b)
    d = functools.partial(jnp.dot, preferred_element_type=F32)
    return d(ah, bh) + d(al, bh) + d(ah, bl)


def _pack_rows(x):
    bits = lax.bitcast_convert_type(x, jnp.int32)
    return bits[:, :PACKED_W] | lax.shift_right_logical(bits[:, PACKED_W:], 16)


def _unpack_rows(w):
    hi = lax.bitcast_convert_type(w & jnp.int32(-65536), F32)
    lo = lax.bitcast_convert_type(lax.shift_left(w, 16), F32)
    return hi.astype(BF16), lo.astype(BF16)


def _rms(x, g):
    return x * lax.rsqrt(jnp.mean(x * x, axis=-1, keepdims=True) + EPS) * g


def _cparams(sem, **kw):
    return pltpu.CompilerParams(dimension_semantics=sem, vmem_limit_bytes=VMEM_LIMIT, **kw)


def _ada_kernel(c_ref, w_ref, b_ref, o_ref):
    c = c_ref[...]
    s = c * jax.nn.sigmoid(c)
    for v in range(o_ref.shape[0]):
        cols = slice(v * D_MODEL, (v + 1) * D_MODEL)
        o_ref[v] = _dot3(s, w_ref[:, cols]) + b_ref[:, cols]


def _modulation(c, w_ada, b_ada, *, vectors_per_step=1):
    rows = c.shape[0]
    n = w_ada.shape[1]
    nvec = n // D_MODEL
    assert nvec % vectors_per_step == 0
    tn = vectors_per_step * D_MODEL
    return pl.pallas_call(
        _ada_kernel,
        grid=(nvec // vectors_per_step,),
        in_specs=[pl.BlockSpec((rows, D_MODEL), lambda j: (0, 0)),
                  pl.BlockSpec((D_MODEL, tn), lambda j: (0, j)),
                  pl.BlockSpec((1, tn), lambda j: (0, j))],
        out_specs=pl.BlockSpec((vectors_per_step, rows, D_MODEL), lambda j: (j, 0, 0)),
        out_shape=jax.ShapeDtypeStruct((nvec, rows, D_MODEL), F32),
        compiler_params=_cparams(("parallel",)),
        name="modulation",
    )(c, w_ada, b_ada.reshape(1, n))


def _head_rms(t, bd, g, precise):
    if precise:
        hi, lo = _split(t * t)
        ss = jnp.dot(hi, bd, preferred_element_type=F32) + jnp.dot(lo, bd, preferred_element_type=F32)
    else:
        ss = _bdot(t * t, bd)
    return t * lax.rsqrt(ss * (1.0 / HEAD_DIM) + EPS) * g


def _mixin_qkv(x_ref, sh_ref, sc_ref, g_ref, w_ref, bd_ref, qn_ref, kn_ref, q_ref, k_ref, v_ref, precise):
    h = _rms(x_ref[...], g_ref[...] * (1.0 + sc_ref[...])) + sh_ref[...]
    z = _dot3(h, w_ref[...]) if precise else _bdot(h, w_ref[...])
    q = z[:, :ATTN_WIDTH]
    k = z[:, ATTN_WIDTH:ATTN_WIDTH + KV_WIDTH]
    bd = bd_ref[...]
    q = _head_rms(q, bd, qn_ref[...], precise)
    k = _head_rms(k, bd[:KV_WIDTH, :KV_WIDTH], kn_ref[...], precise)
    q_ref[...] = (q * (HEAD_DIM ** -0.5)).astype(BF16)
    k_ref[...] = k
    v_ref[...] = z[:, ATTN_WIDTH + KV_WIDTH:ATTN_WIDTH + 2 * KV_WIDTH]
    return z[:, ATTN_WIDTH + 2 * KV_WIDTH:]


def _mixin_kernel(x_ref, sh_ref, sc_ref, g_ref, w_ref, bd_ref, qn_ref, kn_ref,
                  q_ref, k_ref, v_ref, u_ref, *, precise):
    u_ref[...] = _mixin_qkv(x_ref, sh_ref, sc_ref, g_ref, w_ref, bd_ref, qn_ref, kn_ref,
                            q_ref, k_ref, v_ref, precise)


def _mixin_pool_kernel(x_ref, sh_ref, sc_ref, g_ref, w_ref, bd_ref, qn_ref, kn_ref, wp_ref, ps_ref,
                       q_ref, k_ref, v_ref, pool_ref, tail_ref, ext, lv, carry, *, tiles_per_seq):
    u = _mixin_qkv(x_ref, sh_ref, sc_ref, g_ref, w_ref, bd_ref, qn_ref, kn_ref, q_ref, k_ref, v_ref, False)
    t = pl.program_id(0) % tiles_per_seq

    @pl.when(t == 0)
    def _():
        carry[...] = jnp.zeros(carry.shape, F32)

    pool_ref[...] = _pool_tile(u, carry[...], t * u.shape[0], wp_ref, ps_ref, ext, lv)
    last = u[u.shape[0] - carry.shape[0]:, :]
    carry[...] = last
    tail_ref[...] = last


def _mod_spec(mod, k, group_of_step):
    return pl.BlockSpec((None, None, mod.shape[2], D_MODEL), lambda i: (k, group_of_step(i), 0, 0))


def _mixer_inputs(x2d, mod, norm_mix, w_in, bd, q_norm, k_norm, *, tile, rows_per_mod, precise, pool=None):
    n = x2d.shape[0]
    group = lambda i: (i * tile) // rows_per_mod
    const = lambda shape: pl.BlockSpec(shape, lambda i: (0,) * len(shape))
    row = lambda w: pl.BlockSpec((tile, w), lambda i: (i, 0))
    in_specs = [row(D_MODEL), _mod_spec(mod, 0, group), _mod_spec(mod, 1, group), const((1, D_MODEL)),
                const((D_MODEL, IN_WIDTH)), const((ATTN_WIDTH, ATTN_WIDTH)), const((1, ATTN_WIDTH)),
                const((1, KV_WIDTH))]
    args = [x2d, mod, mod, norm_mix.reshape(1, D_MODEL), w_in, bd,
            jnp.tile(q_norm, N_HEADS).reshape(1, ATTN_WIDTH), jnp.tile(k_norm, N_KV_HEADS).reshape(1, KV_WIDTH)]
    out_specs = [row(ATTN_WIDTH), row(KV_WIDTH), row(KV_WIDTH)]
    out_shape = [jax.ShapeDtypeStruct((n, ATTN_WIDTH), BF16),
                 jax.ShapeDtypeStruct((n, KV_WIDTH), F32),
                 jax.ShapeDtypeStruct((n, KV_WIDTH), F32)]
    if pool is None:
        kern = functools.partial(_mixin_kernel, precise=precise)
        out_specs.append(row(POOL_WIDTH))
        out_shape.append(jax.ShapeDtypeStruct((n, POOL_WIDTH), F32))
        scratch, semantics = [], ("parallel",)
    else:
        assert not precise and rows_per_mod % tile == 0 and tile >= POOL_HALO
        w_pool, pool_scale = pool
        kern = functools.partial(_mixin_pool_kernel, tiles_per_seq=rows_per_mod // tile)
        in_specs += [const(w_pool.shape), const((1, POOL_WIDTH))]
        args += [w_pool, pool_scale.reshape(1, POOL_WIDTH)]
        out_specs += [pl.BlockSpec((tile, POOL_WIDTH), lambda i: (i, 0)),
                      pl.BlockSpec((None, POOL_HALO, POOL_WIDTH), lambda i: (group(i), 0, 0))]
        out_shape += [jax.ShapeDtypeStruct((n, POOL_WIDTH), BF16),
                      jax.ShapeDtypeStruct((n // rows_per_mod, POOL_HALO, POOL_WIDTH), F32)]
        ext_rows = SUBLANES + POOL_HALO + tile
        scratch = [pltpu.VMEM((ext_rows, POOL_WIDTH), F32),
                   pltpu.VMEM((len(POOL_WINDOWS) - 1, ext_rows, POOL_GROUP), F32),
                   pltpu.VMEM((POOL_HALO, POOL_WIDTH), F32)]
        semantics = ("arbitrary",)
    return pl.pallas_call(
        kern,
        grid=(n // tile,),
        in_specs=in_specs,
        out_specs=out_specs,
        out_shape=out_shape,
        scratch_shapes=scratch,
        compiler_params=_cparams(semantics),
        name="mixer_inputs",
    )(*args)


def _t5_bucket(rel):
    n = jnp.maximum(rel, 0)
    nf = jnp.maximum(n, 1).astype(F32)
    large = MAX_EXACT + (jnp.log(nf / MAX_EXACT) / math.log(REL_MAX_DIST / MAX_EXACT)
                         * (N_BUCKETS - MAX_EXACT)).astype(jnp.int32)
    large = jnp.minimum(large, N_BUCKETS - 1)
    return jnp.where(n < MAX_EXACT, n, large)


def _bias_table(rel, rel_table):
    bucket = _t5_bucket(rel)
    table = rel_table.astype(F32)
    ids = jnp.arange(N_BUCKETS, dtype=bucket.dtype).reshape((N_BUCKETS, 1) + (1,) * rel.ndim)
    onehot = bucket[None, None] == ids
    bias = jnp.sum(jnp.where(onehot, table.reshape(table.shape + (1,) * rel.ndim), 0.0), axis=0)
    valid = (rel >= 0) & (rel < WINDOW)
    return jnp.where(valid[None], bias, NEG_INF)


def _attn_prompt_kernel(sink_ref, q_ref, kp_ref, kc_ref, vp_ref, vc_ref, bias_ref, o_ref):
    first = pl.program_id(1) == 0
    kk = jnp.concatenate([kp_ref[...], kc_ref[...]], axis=0).astype(BF16)
    vv = jnp.concatenate([vp_ref[...], vc_ref[...]], axis=0).astype(BF16)
    key = lax.broadcasted_iota(jnp.int32, (2 * ATTN_BLOCK, 1), 0)
    no_prev = jnp.logical_and(first, key < ATTN_BLOCK)
    lane = lax.broadcasted_iota(jnp.int32, (1, N_HEADS * ATTN_BLOCK), 1)
    sink = jnp.zeros((1, N_HEADS * ATTN_BLOCK), F32)
    for h in range(N_HEADS):
        sink = jnp.where(lane // ATTN_BLOCK == h, sink_ref[h], sink)
    contract = lambda a, b, dims: lax.dot_general(a, b, (dims, ((), ())), preferred_element_type=F32)
    part = GROUP * ATTN_BLOCK
    width = N_HEADS * ATTN_BLOCK
    bias_g = jnp.concatenate([bias_ref[...]] * ATTN_FUSE, axis=1)
    sink_g = jnp.concatenate([sink] * ATTN_FUSE, axis=1)
    lane_g = lax.broadcasted_iota(jnp.int32, (1, ATTN_FUSE * width), 1)
    for i0 in range(0, ATTN_QB, ATTN_FUSE):
        blocks = range(i0, i0 + ATTN_FUSE)
        scores = []
        for i in blocks:
            q = q_ref[i * ATTN_BLOCK:(i + 1) * ATTN_BLOCK, :]
            for kv in range(N_KV_HEADS):
                heads = range(kv * GROUP, (kv + 1) * GROUP)
                qg = jnp.concatenate([q[:, h * HEAD_DIM:(h + 1) * HEAD_DIM] for h in heads], axis=0)
                scores.append(contract(kk[i * ATTN_BLOCK:(i + 2) * ATTN_BLOCK, kv * HEAD_DIM:(kv + 1) * HEAD_DIM],
                                       qg, ((1,), (1,))))
        s = jnp.concatenate(scores, axis=1) + bias_g
        if i0 == 0:
            s = jnp.where(jnp.logical_and(no_prev, lane_g < width), NEG_INF, s)
        m = jnp.maximum(jnp.max(s, axis=0, keepdims=True), sink_g)
        p = jnp.exp(s - m)
        denom = jnp.sum(p, axis=0, keepdims=True) + jnp.exp(sink_g - m)
        p = p.astype(BF16)
        for n, i in enumerate(blocks):
            keys = slice(i * ATTN_BLOCK, (i + 2) * ATTN_BLOCK)
            halves = []
            for kv in range(N_KV_HEADS):
                cols = slice(n * width + kv * part, n * width + (kv + 1) * part)
                halves.append(contract(vv[keys, kv * HEAD_DIM:(kv + 1) * HEAD_DIM], p[:, cols], ((0,), (0,)))
                              / denom[:, cols])
            o_t = jnp.concatenate(halves, axis=0)
            per_g = [o_t[:, g * ATTN_BLOCK:(g + 1) * ATTN_BLOCK].T for g in range(GROUP)]
            out = [t[:, kv * HEAD_DIM:(kv + 1) * HEAD_DIM] for kv in range(N_KV_HEADS) for t in per_g]
            o_ref[i * ATTN_BLOCK:(i + 1) * ATTN_BLOCK, :] = jnp.concatenate(out, axis=-1).astype(BF16)


def _attn_prompt(q, k, v, sinks, rel_table):
    b, s = q.shape[:2]
    qrows = ATTN_QB * ATTN_BLOCK
    assert s % qrows == 0
    qi = jnp.arange(ATTN_BLOCK, dtype=jnp.int32)[:, None]
    si = jnp.arange(2 * ATTN_BLOCK, dtype=jnp.int32)[None, :]
    bias = _bias_table(qi + ATTN_BLOCK - si, rel_table)
    bias = bias.reshape(N_HEADS * ATTN_BLOCK, 2 * ATTN_BLOCK).T
    cur = lambda w: pl.BlockSpec((None, qrows, w), lambda i, j, *_: (i, j, 0))
    prev = lambda w: pl.BlockSpec((None, ATTN_BLOCK, w),
                                  lambda i, j, *_: (i, jnp.maximum(j * ATTN_QB - 1, 0), 0))
    return pl.pallas_call(
        _attn_prompt_kernel,
        grid_spec=pltpu.PrefetchScalarGridSpec(
            num_scalar_prefetch=1,
            grid=(b, s // qrows),
            in_specs=[cur(ATTN_WIDTH), prev(KV_WIDTH), cur(KV_WIDTH), prev(KV_WIDTH), cur(KV_WIDTH),
                      pl.BlockSpec(bias.shape, lambda i, j, *_: (0, 0))],
            out_specs=cur(ATTN_WIDTH)),
        out_shape=jax.ShapeDtypeStruct((b, s, ATTN_WIDTH), BF16),
        compiler_params=_cparams(("parallel", "parallel")),
        name="attn_prompt",
    )(sinks.astype(F32), q, k, k, v, v, bias)


def _attn_sample_kernel(sink_ref, q_ref, kc_ref, vc_ref, kn_ref, vn_ref, bias_ref, bnew_ref,
                        o_ref, nk_ref, nv_ref):
    kc = kc_ref[...]
    vc = vc_ref[...]
    kn = kn_ref[...]
    vn = vn_ref[...]
    w = kc.shape[1]
    pos = lax.broadcasted_iota(jnp.int32, kc.shape, 1)
    nk_ref[...] = jnp.where(pos == w - 1, kn[:, None, :], pltpu.roll(kc, w - 1, 1))
    nv_ref[...] = jnp.where(pos == w - 1, vn[:, None, :], pltpu.roll(vc, w - 1, 1))
    gi = lax.broadcasted_iota(jnp.int32, (1, GROUP, 1), 1)
    for kv in range(N_KV_HEADS):
        sl = slice(kv * HEAD_DIM, (kv + 1) * HEAD_DIM)
        qg = q_ref[:, kv]
        s = jnp.einsum('bgd,bsd->bgs', qg, kc[:, :, sl].astype(BF16), preferred_element_type=F32)
        s = s + bias_ref[kv][None]
        s_new = jnp.sum(qg.astype(F32) * kn[:, None, sl], axis=-1, keepdims=True) + bnew_ref[kv][None]
        sink = jnp.zeros((1, GROUP, 1), F32)
        for g in range(GROUP):
            sink = jnp.where(gi == g, sink_ref[kv * GROUP + g], sink)
        m = jnp.maximum(jnp.maximum(jnp.max(s, axis=-1, keepdims=True), s_new), sink)
        p = jnp.exp(s - m)
        p_new = jnp.exp(s_new - m)
        denom = jnp.sum(p, axis=-1, keepdims=True) + p_new + jnp.exp(sink - m)
        o = jnp.einsum('bgs,bsd->bgd', p.astype(BF16), vc[:, :, sl].astype(BF16), preferred_element_type=F32)
        o = o + p_new * vn[:, None, sl]
        o_ref[:, kv] = o / denom


def _attn_sample(q, k_new, v_new, cache_k, cache_v, sinks, rel_table, *, tile=32):
    bd, w = cache_k.shape[:2]
    rel = w - jnp.arange(w, dtype=jnp.int32)
    bias = _bias_table(rel, rel_table).reshape(N_KV_HEADS, GROUP, w)
    bnew = _bias_table(jnp.zeros((1,), jnp.int32), rel_table).reshape(N_KV_HEADS, GROUP, 1)
    q4 = q.reshape(bd, N_KV_HEADS, GROUP, HEAD_DIM)
    spec4 = pl.BlockSpec((tile, N_KV_HEADS, GROUP, HEAD_DIM), lambda i, *_: (i, 0, 0, 0))
    cache = pl.BlockSpec((tile, w, KV_WIDTH), lambda i, *_: (i, 0, 0))
    new = pl.BlockSpec((tile, KV_WIDTH), lambda i, *_: (i, 0))
    const3 = lambda a: pl.BlockSpec(a.shape, lambda i, *_: (0, 0, 0))
    o, nk, nv = pl.pallas_call(
        _attn_sample_kernel,
        grid_spec=pltpu.PrefetchScalarGridSpec(
            num_scalar_prefetch=1,
            grid=(bd // tile,),
            in_specs=[spec4, cache, cache, new, new, const3(bias), const3(bnew)],
            out_specs=[spec4, cache, cache]),
        out_shape=[jax.ShapeDtypeStruct(q4.shape, F32),
                   jax.ShapeDtypeStruct(cache_k.shape, F32),
                   jax.ShapeDtypeStruct(cache_v.shape, F32)],
        compiler_params=_cparams(("parallel",)),
        name="attn_sample",
    )(sinks.astype(F32), q4, cache_k, cache_v, k_new, v_new, bias, bnew)
    return o.reshape(bd, ATTN_WIDTH), nk, nv


def _pool_project(d_groups, wp_ref, ps_ref):
    out = [_bdot(d, wp_ref[g]) for g, d in enumerate(d_groups)]
    return (jnp.concatenate(out, axis=-1) * ps_ref[...]).astype(BF16)


def _pool_tile(u, halo, pos0, wp_ref, ps_ref, ext, lv):
    tile = u.shape[0]
    lead, hb = SUBLANES, POOL_HALO
    ext[0:lead, :] = jnp.zeros((lead, ext.shape[1]), F32)
    ext[lead:lead + hb, :] = halo
    ext[lead + hb:, :] = u
    lv[:, 0:lead, :] = jnp.zeros((lv.shape[0], lead, lv.shape[2]), F32)
    pos = pos0 + lax.broadcasted_iota(jnp.int32, (tile, 1), 0)
    n = hb + tile
    ds = []
    for g, w in enumerate(POOL_WINDOWS):
        sl = slice(g * POOL_GROUP, (g + 1) * POOL_GROUP)
        acc = ext[lead:lead + n, sl] + ext[lead - 1:lead - 1 + n, sl]
        span, level = 2, 0
        while span < w:
            lv[level, lead:lead + n, :] = acc
            acc = acc + lv[level, lead - span:lead - span + n, :]
            span, level = 2 * span, level + 1
        cnt = jnp.minimum(pos + 1, w).astype(F32)
        ds.append(acc[hb:] / cnt - ext[lead + hb:lead + hb + tile, sl])
    return _pool_project(ds, wp_ref, ps_ref)


def _pool_sample_kernel(st_ref, u_ref, wp_ref, ps_ref, o_ref, ns_ref):
    u = u_ref[...]
    ns_ref[0:POOL_BUF - 1] = st_ref[1:POOL_BUF]
    ns_ref[POOL_BUF - 1] = u
    ds = []
    for g, w in enumerate(POOL_WINDOWS):
        sl = slice(g * POOL_GROUP, (g + 1) * POOL_GROUP)
        acc = u[:, sl]
        for j in range(1, w):
            acc = acc + st_ref[POOL_BUF - j][:, sl]
        cnt = float(min(PAST_LEN + 1, w))
        ds.append(acc / cnt - u[:, sl])
    o_ref[...] = _pool_project(ds, wp_ref, ps_ref)


def _pool_sample(state_t, u, w_pool, pool_scale):
    nb, bd, c = state_t.shape
    full = lambda a: pl.BlockSpec(a.shape, lambda: (0,) * a.ndim)
    ps = pool_scale.reshape(1, c)
    return pl.pallas_call(
        _pool_sample_kernel,
        in_specs=[full(state_t), full(u), full(w_pool), full(ps)],
        out_specs=[pl.BlockSpec((bd, c), lambda: (0, 0)), full(state_t)],
        out_shape=[jax.ShapeDtypeStruct((bd, c), BF16), jax.ShapeDtypeStruct(state_t.shape, F32)],
        compiler_params=pltpu.CompilerParams(vmem_limit_bytes=VMEM_LIMIT),
        name="pool_sample",
    )(state_t, u, w_pool, ps)


def _route_kernel(*refs, own_steps):
    hs_ref = refs[-3]

    @pl.when(pl.program_id(0) < own_steps)
    def _():
        _route_tiles(*refs)

    @pl.when(pl.program_id(0) >= own_steps)
    def _():
        hs_ref[...] = jnp.zeros(hs_ref.shape, jnp.int32)


def _route_tiles(x_ref, attn_ref, pool_ref, gm_ref, sh_ref, sc_ref, wo_ref, nf_ref, wr_ref, br_ref,
                 tri_t_ref, tri_e_ref,
                 x2_ref, hs_ref, slot_ref, cnt_ref):
    sub = slot_ref.shape[0]
    n = x_ref.shape[0]
    tile = n // sub
    mixed = jnp.concatenate([attn_ref[...], pool_ref[...]], axis=1)
    mix = jnp.dot(mixed, wo_ref[...], preferred_element_type=F32)
    x2 = x_ref[...] + gm_ref[...] * mix
    x2_ref[...] = x2
    h = _rms(x2, nf_ref[...] * (1.0 + sc_ref[...])) + sh_ref[...]

    hh, hl = _split(h)
    wh, wl = _split(wr_ref[...])
    nt = functools.partial(lax.dot_general, dimension_numbers=(((1,), (1,)), ((), ())),
                           preferred_element_type=F32)
    both = nt(jnp.concatenate([wh, wl], axis=0), hh)
    logits = both[:N_EXPERTS] + both[N_EXPERTS:] + nt(wh, hl) + br_ref[...]

    eidx = lax.broadcasted_iota(jnp.int32, (N_EXPERTS, n), 0).astype(F32)
    work = logits
    tops, picks = [], []
    for _ in range(TOP_K):
        m = jnp.max(work, axis=0, keepdims=True)
        pick = jnp.min(jnp.where(work == m, eidx, float(N_EXPERTS)), axis=0, keepdims=True)
        work = jnp.where(eidx == pick, -jnp.inf, work)
        tops.append(m)
        picks.append(pick)
    ex = [jnp.exp(v - tops[0]) for v in tops]
    den = ex[0] + ex[1] + ex[2] + ex[3]
    gates = [e / den for e in ex]

    sel = jnp.zeros((N_EXPERTS, n), F32)
    for pick in picks:
        sel = sel + (eidx == pick).astype(F32)
    selb = sel.astype(BF16)
    rank = jnp.concatenate([jnp.dot(selb[:, t * tile:(t + 1) * tile], tri_t_ref[...], preferred_element_type=F32)
                            for t in range(sub)], axis=1)
    cnts = [jnp.sum(sel[:, t * tile:(t + 1) * tile], axis=1, keepdims=True) for t in range(sub)]
    padded = jnp.concatenate(
        [jnp.broadcast_to(jnp.ceil(c * (1.0 / CHUNK)) * CHUNK, (N_EXPERTS, LANES)) for c in cnts], axis=1)
    seg = jnp.dot(tri_e_ref[...], padded.astype(BF16), preferred_element_type=F32)
    dest = jnp.concatenate([seg[:, t * LANES:t * LANES + 1] + rank[:, t * tile:(t + 1) * tile]
                            for t in range(sub)], axis=1)
    slots = [jnp.sum(jnp.where(eidx == pick, dest, 0.0), axis=0, keepdims=True) for pick in picks]

    ridx = lax.broadcasted_iota(jnp.int32, (TILE_ROWS, tile), 0).astype(jnp.int16)
    slots16 = [v.astype(jnp.int32).astype(jnp.int16) for v in slots]
    for t in range(sub):
        cols = slice(t * tile, (t + 1) * tile)
        cnt_ref[t] = jnp.broadcast_to(cnts[t], (N_EXPERTS, LANES))
        slot_ref[t] = jnp.concatenate([v[:, cols] for v in slots + gates], axis=0)
        hit = ridx == slots16[0][:, cols]
        for s in slots16[1:]:
            hit = jnp.logical_or(hit, ridx == s[:, cols])
        perm = jnp.where(hit, jnp.ones((), BF16), jnp.zeros((), BF16))
        hs_ref[pl.ds(t * TILE_ROWS, TILE_ROWS), :] = _pack_rows(
            jnp.dot(perm, hh[t * tile:(t + 1) * tile, :], preferred_element_type=F32))


def _route(x2d, attn, pool, mod, w_out, norm_ffn, w_router_t, b_router, hs_prev, *, tile, sub,
           rows_per_mod, tile0, extra_tiles):
    n = x2d.shape[0]
    nt = n // tile
    assert nt % sub == 0 and extra_tiles % sub == 0
    own_steps = nt // sub
    steps = own_steps + extra_tiles // sub
    mrows = mod.shape[2]
    assert mrows == 1 or (sub == 1 and mrows == tile)
    last = lambda i: jnp.minimum(i, own_steps - 1)
    group = lambda i: (last(i) * sub * tile) // rows_per_mod
    const = lambda shape: pl.BlockSpec(shape, lambda i: (0,) * len(shape))
    row = lambda w: pl.BlockSpec((sub * tile, w), lambda i: (last(i), 0))
    tri_t = (jnp.arange(tile)[:, None] < jnp.arange(tile)[None, :]).astype(BF16)
    tri_e = (jnp.arange(N_EXPERTS)[None, :] < jnp.arange(N_EXPERTS)[:, None]).astype(BF16)
    in_specs = [row(D_MODEL), row(ATTN_WIDTH), row(POOL_WIDTH),
                _mod_spec(mod, 2, group), _mod_spec(mod, 3, group), _mod_spec(mod, 4, group),
                const((D_MODEL, D_MODEL)), const((1, D_MODEL)), const((N_EXPERTS, D_MODEL)),
                const((N_EXPERTS, 1)), const((tile, tile)), const((N_EXPERTS, N_EXPERTS))]
    args = [x2d, attn, pool, mod, mod, mod, w_out, norm_ffn.reshape(1, D_MODEL), w_router_t,
            b_router.reshape(N_EXPERTS, 1), tri_t, tri_e]
    n_in = len(args)
    kern = functools.partial(_route_kernel, own_steps=own_steps)
    aliases = {}
    hs_rows = (tile0 + steps * sub) * TILE_ROWS
    assert tile0 % sub == 0
    if hs_prev is not None:
        in_specs.append(pl.BlockSpec(memory_space=pl.ANY))
        args.append(hs_prev)
        aliases = {n_in: 1}
        kern = lambda *refs: _route_kernel(*refs[:n_in], *refs[n_in + 1:], own_steps=own_steps)
        hs_rows = hs_prev.shape[0]
    return pl.pallas_call(
        kern,
        grid=(steps,),
        in_specs=in_specs,
        out_specs=[row(D_MODEL),
                   pl.BlockSpec((sub * TILE_ROWS, ROW_W), lambda i: (i + tile0 // sub, 0)),
                   pl.BlockSpec((sub, 2 * TOP_K, tile), lambda i: (last(i), 0, 0)),
                   pl.BlockSpec((sub, N_EXPERTS, LANES), lambda i: (last(i), 0, 0))],
        out_shape=[jax.ShapeDtypeStruct((n, D_MODEL), F32),
                   jax.ShapeDtypeStruct((hs_rows, ROW_W), jnp.int32),
                   jax.ShapeDtypeStruct((nt, 2 * TOP_K, tile), F32),
                   jax.ShapeDtypeStruct((nt, N_EXPERTS, LANES), F32)],
        input_output_aliases=aliases,
        compiler_params=_cparams(("arbitrary",)),
        name="route",
    )(*args)


def _moe_kernel(src_ref, be_ref, nxt_ref, nb_ref, hs_hbm, w1_hbm, b1_ref, w2_hbm, b2_ref, out_hbm,
                lhs, obuf, w1s, w2s, w1c, w2c, sem_in, sem_out, sem_w, *, scratch_chunk):
    nb = nb_ref[0]

    def weight_copies(e):
        return (pltpu.make_async_copy(w1_hbm.at[e], w1s, sem_w.at[0]),
                pltpu.make_async_copy(w2_hbm.at[e], w2s, sem_w.at[1]))

    def chunk_rows(c):
        return pl.ds(pl.multiple_of(c * CHUNK, CHUNK), CHUNK)

    def start_in(blk, s):
        for j in range(BLOCK_CHUNKS):
            c = src_ref[blk * BLOCK_CHUNKS + j]
            c = jnp.where(c < 0, ZERO_CHUNK, c)
            pltpu.make_async_copy(hs_hbm.at[chunk_rows(c)], lhs.at[s, pl.ds(j * CHUNK, CHUNK)],
                                  sem_in.at[s]).start()

    def wait_in(s):
        pltpu.make_async_copy(hs_hbm.at[pl.ds(0, MOE_BLOCK)], lhs.at[s], sem_in.at[s]).wait()

    def start_out(blk, s):
        for j in range(BLOCK_CHUNKS):
            c = src_ref[blk * BLOCK_CHUNKS + j]
            c = jnp.where(c < 0, scratch_chunk + s * BLOCK_CHUNKS + j, c)
            pltpu.make_async_copy(obuf.at[s, pl.ds(j * CHUNK, CHUNK)],
                                  out_hbm.at[chunk_rows(c), pl.ds(0, PACKED_W)], sem_out.at[s]).start()

    def wait_out(s):
        pltpu.make_async_copy(obuf.at[s], out_hbm.at[pl.ds(0, MOE_BLOCK), pl.ds(0, PACKED_W)],
                              sem_out.at[s]).wait()

    @pl.when(nb > 0)
    def _():
        start_in(0, 0)
        for cp in weight_copies(be_ref[0]):
            cp.start(priority=WEIGHT_DMA_PRIORITY)

    def cast_slice(j, q):
        r = pl.ds(pl.multiple_of(j * CAST_ROWS, CAST_ROWS), CAST_ROWS)
        w1c[q, r, :] = w1s[r, :].astype(BF16)
        w2c[q, r, :] = w2s[r, :].astype(BF16)

    def block(b, carry):
        k, p = carry
        slot = b % 2
        e = be_ref[b]
        nxt = nxt_ref[b]
        change = jnp.logical_or(b == 0, e != be_ref[jnp.maximum(b - 1, 0)])
        q = jnp.where(b == 0, 0, 1 - p)

        @pl.when(change)
        def _():
            @pl.when(jnp.logical_or(b == 0, k <= CAST_FIRST))
            def _():
                for cp in weight_copies(e):
                    cp.wait()
            done = jnp.where(b == 0, 0, jnp.clip(k - CAST_FIRST, 0, CAST_SLICES))

            def rest(j, c):
                cast_slice(j, q)
                return c
            lax.fori_loop(done, CAST_SLICES, rest, 0)

            @pl.when(nxt >= 0)
            def _():
                for cp in weight_copies(nxt):
                    cp.start(priority=WEIGHT_DMA_PRIORITY)

        p = jnp.where(change, q, p)
        k = jnp.where(change, 0, k)

        @pl.when(jnp.logical_and(k == CAST_FIRST, nxt >= 0))
        def _():
            for cp in weight_copies(nxt):
                cp.wait()
        cast_now = jnp.logical_and(jnp.logical_and(k >= CAST_FIRST, k < CAST_FIRST + CAST_SLICES), nxt >= 0)

        wait_in(slot)

        @pl.when(b >= 2)
        def _():
            wait_out(slot)

        start_in(b + 1, 1 - slot)

        def ffn(nrows):
            xh, xl = _unpack_rows(lhs[slot, 0:nrows, :])
            x = jnp.concatenate([xh, xl], axis=1)
            gu = jnp.dot(x, w1c[p], preferred_element_type=F32) + b1_ref[e]
            gate = jnp.minimum(gu[:, :D_FF], SWIGLU_LIMIT)
            up = jnp.clip(gu[:, D_FF:], -SWIGLU_LIMIT, SWIGLU_LIMIT)
            act = (up + 1.0) * (gate * jax.nn.sigmoid(SWIGLU_ALPHA * gate))
            y = jnp.dot(act.astype(BF16), w2c[p], preferred_element_type=F32) + b2_ref[e]
            obuf[slot, 0:nrows, :] = _pack_rows(y.astype(BF16).astype(F32))

        real = nb_ref[1 + b]
        whole = real > BLOCK_CHUNKS // 2

        @pl.when(jnp.logical_and(whole, cast_now))
        def _():
            cast_slice(k - CAST_FIRST, 1 - p)
            ffn(MOE_BLOCK)

        @pl.when(jnp.logical_and(whole, jnp.logical_not(cast_now)))
        def _():
            ffn(MOE_BLOCK)

        @pl.when(jnp.logical_not(whole))
        def _():
            ffn(MOE_BLOCK // 2)

            @pl.when(cast_now)
            def _():
                cast_slice(k - CAST_FIRST, 1 - p)

        start_out(b, slot)
        return k + 1, p

    obuf[...] = jnp.zeros(obuf.shape, jnp.int32)
    lax.fori_loop(0, nb, block, (jnp.int32(0), jnp.int32(0)))

    @pl.when(nb > 0)
    def _():
        last_slot = (nb - 1) % 2
        wait_in(1 - last_slot)

        @pl.when(nb >= 2)
        def _():
            wait_out(1 - last_slot)
        wait_out(last_slot)


def _moe(hs, src, block_e, next_e, nblocks, w1, b1, w2, b2, scratch_chunk):
    full = lambda shape: pl.BlockSpec(shape, lambda i, *_: (0,) * len(shape))
    hbm = pl.BlockSpec(memory_space=pl.ANY)
    return pl.pallas_call(
        functools.partial(_moe_kernel, scratch_chunk=scratch_chunk),
        grid_spec=pltpu.PrefetchScalarGridSpec(
            num_scalar_prefetch=4,
            grid=(1,),
            in_specs=[hbm, hbm, full((N_EXPERTS, 1, 2 * D_FF)), hbm, full((N_EXPERTS, 1, D_MODEL))],
            out_specs=hbm,
            scratch_shapes=[pltpu.VMEM((2, MOE_BLOCK, ROW_W), jnp.int32),
                            pltpu.VMEM((2, MOE_BLOCK, PACKED_W), jnp.int32),
                            pltpu.VMEM((D_MODEL, 2 * D_FF), F32),
                            pltpu.VMEM((D_FF, D_MODEL), F32),
                            pltpu.VMEM((2, D_MODEL, 2 * D_FF), BF16),
                            pltpu.VMEM((2, D_FF, D_MODEL), BF16),
                            pltpu.SemaphoreType.DMA((2,)),
                            pltpu.SemaphoreType.DMA((2,)),
                            pltpu.SemaphoreType.DMA((2,))]),
        out_shape=jax.ShapeDtypeStruct(hs.shape, jnp.int32),
        input_output_aliases={4: 0},
        compiler_params=_cparams(("arbitrary",)),
        name="moe_experts",
    )(src, block_e, next_e, nblocks, hs, w1, b1.reshape(N_EXPERTS, 1, 2 * D_FF), w2,
      b2.reshape(N_EXPERTS, 1, D_MODEL))


def _plan(cnt, n_blocks_max):
    nt = cnt.shape[0]
    nch = (cnt + (CHUNK - 1)) // CHUNK
    lstart = jnp.cumsum(nch, axis=1) - nch
    ne = jnp.sum(nch, axis=0)
    nbe = (ne + (BLOCK_CHUNKS - 1)) // BLOCK_CHUNKS
    bend = jnp.cumsum(nbe)
    nblocks = bend[-1]
    gstart = (bend - nbe)[None, :] * BLOCK_CHUNKS + (jnp.cumsum(nch, axis=0) - nch)
    s0 = jnp.arange(nt, dtype=jnp.int32)[:, None] * TILE_CHUNKS + lstart
    blk = jnp.arange(n_blocks_max + 1, dtype=jnp.int32)
    be = jnp.sum((blk[:, None] >= bend[None, :]).astype(jnp.int32), axis=1)
    be_last = jnp.sum((nblocks - 1 >= bend).astype(jnp.int32))
    be = jnp.minimum(be, be_last).astype(jnp.int32)
    eid = jnp.arange(N_EXPERTS, dtype=jnp.int32)
    strips = jnp.stack([gstart.T, nch.T, s0.T])
    mine = (be[:, None] == eid[None, :])[None, :, :, None]
    gs_b, nc_b, s0_b = jnp.sum(jnp.where(mine, strips[:, None], 0), axis=2)
    c = (blk[:, None] * BLOCK_CHUNKS + jnp.arange(BLOCK_CHUNKS, dtype=jnp.int32)[None, :])[:, :, None]
    inside = jnp.logical_and(c >= gs_b[:, None, :], c < (gs_b + nc_b)[:, None, :])
    src = (jnp.sum(jnp.where(inside, (s0_b - gs_b)[:, None, :] + c + 1, 0), axis=2) - 1).reshape(-1)
    be = be[:n_blocks_max]
    later = jnp.logical_and(eid[None, :] > be[:, None], (nbe > 0)[None, :])
    nxt = jnp.min(jnp.where(later, eid[None, :], N_EXPERTS), axis=1)
    nxt = jnp.where(nxt == N_EXPERTS, -1, nxt).astype(jnp.int32)
    real = jnp.sum((src.reshape(-1, BLOCK_CHUNKS) >= 0).astype(jnp.int32), axis=1)
    return src, be, nxt, jnp.concatenate([nblocks.reshape(1), real]).astype(jnp.int32)


def _combine_kernel(o_ref, slot_ref, x2_ref, gf_ref, y_ref):
    sub = slot_ref.shape[0]
    tile = x2_ref.shape[0] // sub
    ridx = lax.broadcasted_iota(jnp.int32, (TILE_ROWS, tile), 0).astype(F32)
    tn = functools.partial(lax.dot_general, dimension_numbers=(((0,), (0,)), ((), ())),
                           preferred_element_type=F32)
    for t in range(sub):
        gmat = jnp.zeros((TILE_ROWS, tile), F32)
        for k in range(TOP_K):
            gmat = jnp.where(ridx == slot_ref[t, k:k + 1, :], slot_ref[t, TOP_K + k:TOP_K + k + 1, :], gmat)
        gb = gmat.astype(BF16)
        oh, ol = _unpack_rows(o_ref[pl.ds(t * TILE_ROWS, TILE_ROWS), :])
        y = jnp.concatenate([tn(gb, oh), tn(gb, ol)], axis=1)
        rows = pl.ds(t * tile, tile)
        gf = gf_ref[...] if gf_ref.shape[0] == 1 else gf_ref[rows, :]
        y_ref[rows, :] = x2_ref[rows, :] + gf * y


def _combine(outs, slots, x2, mod, *, tile, sub, rows_per_mod, tile0):
    n = x2.shape[0]
    mrows = mod.shape[2]
    assert n % (sub * tile) == 0 and tile0 % sub == 0 and (mrows == 1 or sub == 1)
    return pl.pallas_call(
        _combine_kernel,
        grid=(n // (sub * tile),),
        in_specs=[pl.BlockSpec((sub * TILE_ROWS, PACKED_W), lambda i: (i + tile0 // sub, 0)),
                  pl.BlockSpec((sub, 2 * TOP_K, tile), lambda i: (i, 0, 0)),
                  pl.BlockSpec((sub * tile, D_MODEL), lambda i: (i, 0)),
                  _mod_spec(mod, 5, lambda i: (i * sub * tile) // rows_per_mod)],
        out_specs=pl.BlockSpec((sub * tile, D_MODEL), lambda i: (i, 0)),
        out_shape=jax.ShapeDtypeStruct((n, D_MODEL), F32),
        compiler_params=_cparams(("parallel",)),
        name="combine",
    )(outs, slots, x2, mod)


def kernel(x_prompt, x_sample, cache_k, cache_v, state_pool, c_prompt, c_sample, rel_bias, norm_mix, w_ada,
           b_ada, w_in, q_norm, k_norm, sinks, w_pool, pool_scale, w_out, norm_ffn, w_router, b_router,
           w1, b1, w2, b2):
    depth = w_in.shape[0]
    assert depth == 1
    l = 0
    bp, sp, _ = x_prompt.shape
    bs = x_sample.shape[0]
    assert x_sample.shape[1] == 1 and sp % SORT_TILE == 0 and bs <= SORT_TILE
    n_p = bp * sp
    tiles_p = n_p // SORT_TILE
    max_chunks = tiles_p * (SORT_TILE * TOP_K // CHUNK + N_EXPERTS) + (bs * TOP_K // CHUNK + N_EXPERTS)
    n_blocks_max = -(-max_chunks // BLOCK_CHUNKS) + N_EXPERTS

    pr = -(-bp // SUBLANES) * SUBLANES
    c_all = jnp.concatenate([c_prompt, jnp.zeros((pr - bp, D_MODEL), F32), c_sample], axis=0)
    mod = _modulation(c_all, w_ada[l], b_ada[l])
    mod_p = mod[:, :bp].reshape(6, bp, 1, D_MODEL)
    mod_s = mod[:, pr:].reshape(6, 1, bs, D_MODEL)

    head = jnp.arange(ATTN_WIDTH) // HEAD_DIM
    bd = (head[:, None] == head[None, :]).astype(BF16)
    w_in_b = w_in[l].astype(BF16)
    w_out_b = w_out[l].astype(BF16)
    w_pool_b = w_pool[l].astype(BF16)
    w_router_t = w_router[l].T

    xp = x_prompt.reshape(n_p, D_MODEL)
    q, k, v, pool, u_tail = _mixer_inputs(xp, mod_p, norm_mix[l], w_in_b, bd, q_norm[l], k_norm[l],
                                          tile=1024, rows_per_mod=sp, precise=False,
                                          pool=(w_pool_b, pool_scale[l]))
    k3 = k.reshape(bp, sp, KV_WIDTH)
    v3 = v.reshape(bp, sp, KV_WIDTH)
    attn = _attn_prompt(q.reshape(bp, sp, ATTN_WIDTH), k3, v3, sinks[l], rel_bias)
    keep = min(WINDOW, sp)
    nkp = k3[:, -keep:].reshape(bp, keep, N_KV_HEADS, HEAD_DIM)
    nvp = v3[:, -keep:].reshape(bp, keep, N_KV_HEADS, HEAD_DIM)
    npp = u_tail[:, POOL_HALO - POOL_BUF:]
    x2_p, hs, slots_p, cnt_p = _route(
        xp, attn.reshape(n_p, ATTN_WIDTH), pool.reshape(n_p, POOL_WIDTH), mod_p,
        w_out_b, norm_ffn[l], w_router_t, b_router[l], None,
        tile=SORT_TILE, sub=ROUTE_SUB, rows_per_mod=sp, tile0=0, extra_tiles=max(2, ROUTE_SUB))

    xs = x_sample.reshape(bs, D_MODEL)
    qs, ks, vs, us = _mixer_inputs(xs, mod_s, norm_mix[l], w_in[l], bd, q_norm[l], k_norm[l],
                                   tile=bs, rows_per_mod=bs, precise=True)
    wbuf = cache_k.shape[2]
    attn_s, nks, nvs = _attn_sample(qs, ks, vs, cache_k[l].reshape(bs, wbuf, KV_WIDTH),
                                    cache_v[l].reshape(bs, wbuf, KV_WIDTH), sinks[l], rel_bias)
    pool_s, nps_t = _pool_sample(jnp.swapaxes(state_pool[l], 0, 1), us, w_pool_b, pool_scale[l])
    x2_s, hs, slots_s, cnt_s = _route(
        xs, attn_s.astype(BF16), pool_s, mod_s, w_out_b, norm_ffn[l],
        w_router_t, b_router[l], hs, tile=bs, sub=1, rows_per_mod=bs, tile0=tiles_p, extra_tiles=0)

    cnt = jnp.concatenate([cnt_p[:, :, 0], cnt_s[:, :, 0]], axis=0).astype(jnp.int32)
    src, block_e, next_e, nblocks = _plan(cnt, n_blocks_max)
    outs = _moe(hs, src, block_e, next_e, nblocks, w1[l], b1[l], w2[l], b2[l],
                scratch_chunk=(tiles_p + 1) * TILE_CHUNKS)

    y_p = _combine(outs, slots_p, x2_p, mod_p, tile=SORT_TILE, sub=COMBINE_SUB, rows_per_mod=sp, tile0=0)
    y_s = _combine(outs, slots_s, x2_s, mod_s, tile=bs, sub=1, rows_per_mod=bs, tile0=tiles_p)

    return (y_p.reshape(bp, sp, D_MODEL), y_s.reshape(bs, 1, D_MODEL),
            nkp[None], nvp[None], npp[None],
            nks.reshape(1, bs, wbuf, N_KV_HEADS, HEAD_DIM), nvs.reshape(1, bs, wbuf, N_KV_HEADS, HEAD_DIM),
            jnp.swapaxes(nps_t, 0, 1)[None])
```

```python
import functools
import math

import jax
import jax.numpy as jnp
from jax import lax
from jax.experimental import pallas as pl
from jax.experimental.pallas import tpu as pltpu

F32 = jnp.float32
BF16 = jnp.bfloat16

D_MODEL = 1024
HEAD_DIM = 64
N_HEADS = 8
N_KV_HEADS = 2
GROUP = N_HEADS // N_KV_HEADS
ATTN_WIDTH = N_HEADS * HEAD_DIM
KV_WIDTH = N_KV_HEADS * HEAD_DIM
POOL_WIDTH = D_MODEL - ATTN_WIDTH
POOL_WINDOWS = (2, 4, 8, 16)
POOL_GROUP = POOL_WIDTH // len(POOL_WINDOWS)
POOL_BUF = max(POOL_WINDOWS) - 1
IN_WIDTH = ATTN_WIDTH + 2 * KV_WIDTH + POOL_WIDTH
WINDOW = 128
ATTN_BLOCK = 128
N_BUCKETS = 32
MAX_EXACT = 16
REL_MAX_DIST = 128
N_EXPERTS = 32
TOP_K = 4
D_FF = D_MODEL
SWIGLU_LIMIT = 7.0
SWIGLU_ALPHA = 1.702
EPS = 1e-6
NEG_INF = -1e30
PAST_LEN = 16384

LANES = 128
SUBLANES = 8
VMEM_LIMIT = 56 * 1024 * 1024

ATTN_QB = 16
ATTN_FUSE = 2
POOL_HALO = 2 * SUBLANES

SORT_TILE = 256
ROUTE_SUB = 4
COMBINE_SUB = 4
CHUNK = SUBLANES
TILE_ROWS = -(-(SORT_TILE * TOP_K + N_EXPERTS * (CHUNK - 1)) // LANES) * LANES
TILE_CHUNKS = TILE_ROWS // CHUNK
MOE_BLOCK = 256
BLOCK_CHUNKS = MOE_BLOCK // CHUNK
WEIGHT_DMA_PRIORITY = 1
CAST_SLICES = 8
CAST_FIRST = 2
CAST_ROWS = D_FF // CAST_SLICES
PACKED_W = D_MODEL // 2
ROW_W = PACKED_W
ZERO_CHUNK = TILE_CHUNKS - 1


def _bdot(a, b):
    return jnp.dot(a.astype(BF16), b.astype(BF16), preferred_element_type=F32)


def _split(a):
    hi = a.astype(BF16)
    lo = (a - hi.astype(F32)).astype(BF16)
    return hi, lo


def _dot3(a, b):
    ah, al = _split(a)
    bh, bl = _split(b)
    d = functools.partial(jnp.dot, preferred_element_type=F32)
    return d(ah, bh) + d(al, bh) + d(ah, bl)


def _pack_rows(x):
    bits = lax.bitcast_convert_type(x, jnp.int32)
    return bits[:, :PACKED_W] | lax.shift_right_logical(bits[:, PACKED_W:], 16)


def _unpack_rows(w):
    hi = lax.bitcast_convert_type(w & jnp.int32(-65536), F32)
    lo = lax.bitcast_convert_type(lax.shift_left(w, 16), F32)
    return hi.astype(BF16), lo.astype(BF16)


def _rms(x, g):
    return x * lax.rsqrt(jnp.mean(x * x, axis=-1, keepdims=True) + EPS) * g


def _cparams(sem, **kw):
    return pltpu.CompilerParams(dimension_semantics=sem, vmem_limit_bytes=VMEM_LIMIT, **kw)


def _ada_kernel(c_ref, w_ref, b_ref, o_ref):
    c = c_ref[...]
    s = c * jax.nn.sigmoid(c)
    for v in range(o_ref.shape[0]):
        cols = slice(v * D_MODEL, (v + 1) * D_MODEL)
        o_ref[v] = _dot3(s, w_ref[:, cols]) + b_ref[:, cols]


def _modulation(c, w_ada, b_ada, *, vectors_per_step=1):
    rows = c.shape[0]
    n = w_ada.shape[1]
    nvec = n // D_MODEL
    assert nvec % vectors_per_step == 0
    tn = vectors_per_step * D_MODEL
    return pl.pallas_call(
        _ada_kernel,
        grid=(nvec // vectors_per_step,),
        in_specs=[pl.BlockSpec((rows, D_MODEL), lambda j: (0, 0)),
                  pl.BlockSpec((D_MODEL, tn), lambda j: (0, j)),
                  pl.BlockSpec((1, tn), lambda j: (0, j))],
        out_specs=pl.BlockSpec((vectors_per_step, rows, D_MODEL), lambda j: (j, 0, 0)),
        out_shape=jax.ShapeDtypeStruct((nvec, rows, D_MODEL), F32),
        compiler_params=_cparams(("parallel",)),
        name="modulation",
    )(c, w_ada, b_ada.reshape(1, n))


def _head_rms(t, bd, g, precise):
    if precise:
        hi, lo = _split(t * t)
        ss = jnp.dot(hi, bd, preferred_element_type=F32) + jnp.dot(lo, bd, preferred_element_type=F32)
    else:
        ss = _bdot(t * t, bd)
    return t * lax.rsqrt(ss * (1.0 / HEAD_DIM) + EPS) * g


def _mixin_qkv(x_ref, sh_ref, sc_ref, g_ref, w_ref, bd_ref, qn_ref, kn_ref, q_ref, k_ref, v_ref, precise):
    h = _rms(x_ref[...], g_ref[...] * (1.0 + sc_ref[...])) + sh_ref[...]
    z = _dot3(h, w_ref[...]) if precise else _bdot(h, w_ref[...])
    q = z[:, :ATTN_WIDTH]
    k = z[:, ATTN_WIDTH:ATTN_WIDTH + KV_WIDTH]
    bd = bd_ref[...]
    q = _head_rms(q, bd, qn_ref[...], precise)
    k = _head_rms(k, bd[:KV_WIDTH, :KV_WIDTH], kn_ref[...], precise)
    q_ref[...] = (q * (HEAD_DIM ** -0.5)).astype(BF16)
    k_ref[...] = k
    v_ref[...] = z[:, ATTN_WIDTH + KV_WIDTH:ATTN_WIDTH + 2 * KV_WIDTH]
    return z[:, ATTN_WIDTH + 2 * KV_WIDTH:]


def _mixin_kernel(x_ref, sh_ref, sc_ref, g_ref, w_ref, bd_ref, qn_ref, kn_ref,
                  q_ref, k_ref, v_ref, u_ref, *, precise):
    u_ref[...] = _mixin_qkv(x_ref, sh_ref, sc_ref, g_ref, w_ref, bd_ref, qn_ref, kn_ref,
                            q_ref, k_ref, v_ref, precise)


def _mixin_pool_kernel(x_ref, sh_ref, sc_ref, g_ref, w_ref, bd_ref, qn_ref, kn_ref, wp_ref, ps_ref,
                       q_ref, k_ref, v_ref, pool_ref, tail_ref, ext, lv, carry, *, tiles_per_seq):
    u = _mixin_qkv(x_ref, sh_ref, sc_ref, g_ref, w_ref, bd_ref, qn_ref, kn_ref, q_ref, k_ref, v_ref, False)
    t = pl.program_id(0) % tiles_per_seq

    @pl.when(t == 0)
    def _():
        carry[...] = jnp.zeros(carry.shape, F32)

    pool_ref[...] = _pool_tile(u, carry[...], t * u.shape[0], wp_ref, ps_ref, ext, lv)
    last = u[u.shape[0] - carry.shape[0]:, :]
    carry[...] = last
    tail_ref[...] = last


def _mod_spec(mod, k, group_of_step):
    return pl.BlockSpec((None, None, mod.shape[2], D_MODEL), lambda i: (k, group_of_step(i), 0, 0))


def _mixer_inputs(x2d, mod, norm_mix, w_in, bd, q_norm, k_norm, *, tile, rows_per_mod, precise, pool=None):
    n = x2d.shape[0]
    group = lambda i: (i * tile) // rows_per_mod
    const = lambda shape: pl.BlockSpec(shape, lambda i: (0,) * len(shape))
    row = lambda w: pl.BlockSpec((tile, w), lambda i: (i, 0))
    in_specs = [row(D_MODEL), _mod_spec(mod, 0, group), _mod_spec(mod, 1, group), const((1, D_MODEL)),
                const((D_MODEL, IN_WIDTH)), const((ATTN_WIDTH, ATTN_WIDTH)), const((1, ATTN_WIDTH)),
                const((1, KV_WIDTH))]
    args = [x2d, mod, mod, norm_mix.reshape(1, D_MODEL), w_in, bd,
            jnp.tile(q_norm, N_HEADS).reshape(1, ATTN_WIDTH), jnp.tile(k_norm, N_KV_HEADS).reshape(1, KV_WIDTH)]
    out_specs = [row(ATTN_WIDTH), row(KV_WIDTH), row(KV_WIDTH)]
    out_shape = [jax.ShapeDtypeStruct((n, ATTN_WIDTH), BF16),
                 jax.ShapeDtypeStruct((n, KV_WIDTH), F32),
                 jax.ShapeDtypeStruct((n, KV_WIDTH), F32)]
    if pool is None:
        kern = functools.partial(_mixin_kernel, precise=precise)
        out_specs.append(row(POOL_WIDTH))
        out_shape.append(jax.ShapeDtypeStruct((n, POOL_WIDTH), F32))
        scratch, semantics = [], ("parallel",)
    else:
        assert not precise and rows_per_mod % tile == 0 and tile >= POOL_HALO
        w_pool, pool_scale = pool
        kern = functools.partial(_mixin_pool_kernel, tiles_per_seq=rows_per_mod // tile)
        in_specs += [const(w_pool.shape), const((1, POOL_WIDTH))]
        args += [w_pool, pool_scale.reshape(1, POOL_WIDTH)]
        out_specs += [pl.BlockSpec((tile, POOL_WIDTH), lambda i: (i, 0)),
                      pl.BlockSpec((None, POOL_HALO, POOL_WIDTH), lambda i: (group(i), 0, 0))]
        out_shape += [jax.ShapeDtypeStruct((n, POOL_WIDTH), BF16),
                      jax.ShapeDtypeStruct((n // rows_per_mod, POOL_HALO, POOL_WIDTH), F32)]
        ext_rows = SUBLANES + POOL_HALO + tile
        scratch = [pltpu.VMEM((ext_rows, POOL_WIDTH), F32),
                   pltpu.VMEM((len(POOL_WINDOWS) - 1, ext_rows, POOL_GROUP), F32),
                   pltpu.VMEM((POOL_HALO, POOL_WIDTH), F32)]
        semantics = ("arbitrary",)
    return pl.pallas_call(
        kern,
        grid=(n // tile,),
        in_specs=in_specs,
        out_specs=out_specs,
        out_shape=out_shape,
        scratch_shapes=scratch,
        compiler_params=_cparams(semantics),
        name="mixer_inputs",
    )(*args)


def _t5_bucket(rel):
    n = jnp.maximum(rel, 0)
    nf = jnp.maximum(n, 1).astype(F32)
    large = MAX_EXACT + (jnp.log(nf / MAX_EXACT) / math.log(REL_MAX_DIST / MAX_EXACT)
                         * (N_BUCKETS - MAX_EXACT)).astype(jnp.int32)
    large = jnp.minimum(large, N_BUCKETS - 1)
    return jnp.where(n < MAX_EXACT, n, large)


def _bias_table(rel, rel_table):
    bucket = _t5_bucket(rel)
    table = rel_table.astype(F32)
    ids = jnp.arange(N_BUCKETS, dtype=bucket.dtype).reshape((N_BUCKETS, 1) + (1,) * rel.ndim)
    onehot = bucket[None, None] == ids
    bias = jnp.sum(jnp.where(onehot, table.reshape(table.shape + (1,) * rel.ndim), 0.0), axis=0)
    valid = (rel >= 0) & (rel < WINDOW)
    return jnp.where(valid[None], bias, NEG_INF)


def _attn_prompt_kernel(sink_ref, q_ref, kp_ref, kc_ref, vp_ref, vc_ref, bias_ref, o_ref):
    first = pl.program_id(1) == 0
    kk = jnp.concatenate([kp_ref[...], kc_ref[...]], axis=0).astype(BF16)
    vv = jnp.concatenate([vp_ref[...], vc_ref[...]], axis=0).astype(BF16)
    key = lax.broadcasted_iota(jnp.int32, (2 * ATTN_BLOCK, 1), 0)
    no_prev = jnp.logical_and(first, key < ATTN_BLOCK)
    lane = lax.broadcasted_iota(jnp.int32, (1, N_HEADS * ATTN_BLOCK), 1)
    sink = jnp.zeros((1, N_HEADS * ATTN_BLOCK), F32)
    for h in range(N_HEADS):
        sink = jnp.where(lane // ATTN_BLOCK == h, sink_ref[h], sink)
    contract = lambda a, b, dims: lax.dot_general(a, b, (dims, ((), ())), preferred_element_type=F32)
    part = GROUP * ATTN_BLOCK
    width = N_HEADS * ATTN_BLOCK
    bias_g = jnp.concatenate([bias_ref[...]] * ATTN_FUSE, axis=1)
    sink_g = jnp.concatenate([sink] * ATTN_FUSE, axis=1)
    lane_g = lax.broadcasted_iota(jnp.int32, (1, ATTN_FUSE * width), 1)
    for i0 in range(0, ATTN_QB, ATTN_FUSE):
        blocks = range(i0, i0 + ATTN_FUSE)
        scores = []
        for i in blocks:
            q = q_ref[i * ATTN_BLOCK:(i + 1) * ATTN_BLOCK, :]
            for kv in range(N_KV_HEADS):
                heads = range(kv * GROUP, (kv + 1) * GROUP)
                qg = jnp.concatenate([q[:, h * HEAD_DIM:(h + 1) * HEAD_DIM] for h in heads], axis=0)
                scores.append(contract(kk[i * ATTN_BLOCK:(i + 2) * ATTN_BLOCK, kv * HEAD_DIM:(kv + 1) * HEAD_DIM],
                                       qg, ((1,), (1,))))
        s = jnp.concatenate(scores, axis=1) + bias_g
        if i0 == 0:
            s = jnp.where(jnp.logical_and(no_prev, lane_g < width), NEG_INF, s)
        m = jnp.maximum(jnp.max(s, axis=0, keepdims=True), sink_g)
        p = jnp.exp(s - m)
        denom = jnp.sum(p, axis=0, keepdims=True) + jnp.exp(sink_g - m)
        p = p.astype(BF16)
        for n, i in enumerate(blocks):
            keys = slice(i * ATTN_BLOCK, (i + 2) * ATTN_BLOCK)
            halves = []
            for kv in range(N_KV_HEADS):
                cols = slice(n * width + kv * part, n * width + (kv + 1) * part)
                halves.append(contract(vv[keys, kv * HEAD_DIM:(kv + 1) * HEAD_DIM], p[:, cols], ((0,), (0,)))
                              / denom[:, cols])
            o_t = jnp.concatenate(halves, axis=0)
            per_g = [o_t[:, g * ATTN_BLOCK:(g + 1) * ATTN_BLOCK].T for g in range(GROUP)]
            out = [t[:, kv * HEAD_DIM:(kv + 1) * HEAD_DIM] for kv in range(N_KV_HEADS) for t in per_g]
            o_ref[i * ATTN_BLOCK:(i + 1) * ATTN_BLOCK, :] = jnp.concatenate(out, axis=-1).astype(BF16)


def _attn_prompt(q, k, v, sinks, rel_table):
    b, s = q.shape[:2]
    qrows = ATTN_QB * ATTN_BLOCK
    assert s % qrows == 0
    qi = jnp.arange(ATTN_BLOCK, dtype=jnp.int32)[:, None]
    si = jnp.arange(2 * ATTN_BLOCK, dtype=jnp.int32)[None, :]
    bias = _bias_table(qi + ATTN_BLOCK - si, rel_table)
    bias = bias.reshape(N_HEADS * ATTN_BLOCK, 2 * ATTN_BLOCK).T
    cur = lambda w: pl.BlockSpec((None, qrows, w), lambda i, j, *_: (i, j, 0))
    prev = lambda w: pl.BlockSpec((None, ATTN_BLOCK, w),
                                  lambda i, j, *_: (i, jnp.maximum(j * ATTN_QB - 1, 0), 0))
    return pl.pallas_call(
        _attn_prompt_kernel,
        grid_spec=pltpu.PrefetchScalarGridSpec(
            num_scalar_prefetch=1,
            grid=(b, s // qrows),
            in_specs=[cur(ATTN_WIDTH), prev(KV_WIDTH), cur(KV_WIDTH), prev(KV_WIDTH), cur(KV_WIDTH),
                      pl.BlockSpec(bias.shape, lambda i, j, *_: (0, 0))],
            out_specs=cur(ATTN_WIDTH)),
        out_shape=jax.ShapeDtypeStruct((b, s, ATTN_WIDTH), BF16),
        compiler_params=_cparams(("parallel", "parallel")),
        name="attn_prompt",
    )(sinks.astype(F32), q, k, k, v, v, bias)


def _attn_sample_kernel(sink_ref, q_ref, kc_ref, vc_ref, kn_ref, vn_ref, bias_ref, bnew_ref,
                        o_ref, nk_ref, nv_ref):
    kc = kc_ref[...]
    vc = vc_ref[...]
    kn = kn_ref[...]
    vn = vn_ref[...]
    w = kc.shape[1]
    pos = lax.broadcasted_iota(jnp.int32, kc.shape, 1)
    nk_ref[...] = jnp.where(pos == w - 1, kn[:, None, :], pltpu.roll(kc, w - 1, 1))
    nv_ref[...] = jnp.where(pos == w - 1, vn[:, None, :], pltpu.roll(vc, w - 1, 1))
    gi = lax.broadcasted_iota(jnp.int32, (1, GROUP, 1), 1)
    for kv in range(N_KV_HEADS):
        sl = slice(kv * HEAD_DIM, (kv + 1) * HEAD_DIM)
        qg = q_ref[:, kv]
        s = jnp.einsum('bgd,bsd->bgs', qg, kc[:, :, sl].astype(BF16), preferred_element_type=F32)
        s = s + bias_ref[kv][None]
        s_new = jnp.sum(qg.astype(F32) * kn[:, None, sl], axis=-1, keepdims=True) + bnew_ref[kv][None]
        sink = jnp.zeros((1, GROUP, 1), F32)
        for g in range(GROUP):
            sink = jnp.where(gi == g, sink_ref[kv * GROUP + g], sink)
        m = jnp.maximum(jnp.maximum(jnp.max(s, axis=-1, keepdims=True), s_new), sink)
        p = jnp.exp(s - m)
        p_new = jnp.exp(s_new - m)
        denom = jnp.sum(p, axis=-1, keepdims=True) + p_new + jnp.exp(sink - m)
        o = jnp.einsum('bgs,bsd->bgd', p.astype(BF16), vc[:, :, sl].astype(BF16), preferred_element_type=F32)
        o = o + p_new * vn[:, None, sl]
        o_ref[:, kv] = o / denom


def _attn_sample(q, k_new, v_new, cache_k, cache_v, sinks, rel_table, *, tile=32):
    bd, w = cache_k.shape[:2]
    rel = w - jnp.arange(w, dtype=jnp.int32)
    bias = _bias_table(rel, rel_table).reshape(N_KV_HEADS, GROUP, w)
    bnew = _bias_table(jnp.zeros((1,), jnp.int32), rel_table).reshape(N_KV_HEADS, GROUP, 1)
    q4 = q.reshape(bd, N_KV_HEADS, GROUP, HEAD_DIM)
    spec4 = pl.BlockSpec((tile, N_KV_HEADS, GROUP, HEAD_DIM), lambda i, *_: (i, 0, 0, 0))
    cache = pl.BlockSpec((tile, w, KV_WIDTH), lambda i, *_: (i, 0, 0))
    new = pl.BlockSpec((tile, KV_WIDTH), lambda i, *_: (i, 0))
    const3 = lambda a: pl.BlockSpec(a.shape, lambda i, *_: (0, 0, 0))
    o, nk, nv = pl.pallas_call(
        _attn_sample_kernel,
        grid_spec=pltpu.PrefetchScalarGridSpec(
            num_scalar_prefetch=1,
            grid=(bd // tile,),
            in_specs=[spec4, cache, cache, new, new, const3(bias), const3(bnew)],
            out_specs=[spec4, cache, cache]),
        out_shape=[jax.ShapeDtypeStruct(q4.shape, F32),
                   jax.ShapeDtypeStruct(cache_k.shape, F32),
                   jax.ShapeDtypeStruct(cache_v.shape, F32)],
        compiler_params=_cparams(("parallel",)),
        name="attn_sample",
    )(sinks.astype(F32), q4, cache_k, cache_v, k_new, v_new, bias, bnew)
    return o.reshape(bd, ATTN_WIDTH), nk, nv


def _pool_project(d_groups, wp_ref, ps_ref):
    out = [_bdot(d, wp_ref[g]) for g, d in enumerate(d_groups)]
    return (jnp.concatenate(out, axis=-1) * ps_ref[...]).astype(BF16)


def _pool_tile(u, halo, pos0, wp_ref, ps_ref, ext, lv):
    tile = u.shape[0]
    lead, hb = SUBLANES, POOL_HALO
    ext[0:lead, :] = jnp.zeros((lead, ext.shape[1]), F32)
    ext[lead:lead + hb, :] = halo
    ext[lead + hb:, :] = u
    lv[:, 0:lead, :] = jnp.zeros((lv.shape[0], lead, lv.shape[2]), F32)
    pos = pos0 + lax.broadcasted_iota(jnp.int32, (tile, 1), 0)
    n = hb + tile
    ds = []
    for g, w in enumerate(POOL_WINDOWS):
        sl = slice(g * POOL_GROUP, (g + 1) * POOL_GROUP)
        acc = ext[lead:lead + n, sl] + ext[lead - 1:lead - 1 + n, sl]
        span, level = 2, 0
        while span < w:
            lv[level, lead:lead + n, :] = acc
            acc = acc + lv[level, lead - span:lead - span + n, :]
            span, level = 2 * span, level + 1
        cnt = jnp.minimum(pos + 1, w).astype(F32)
        ds.append(acc[hb:] / cnt - ext[lead + hb:lead + hb + tile, sl])
    return _pool_project(ds, wp_ref, ps_ref)


def _pool_sample_kernel(st_ref, u_ref, wp_ref, ps_ref, o_ref, ns_ref):
    u = u_ref[...]
    ns_ref[0:POOL_BUF - 1] = st_ref[1:POOL_BUF]
    ns_ref[POOL_BUF - 1] = u
    ds = []
    for g, w in enumerate(POOL_WINDOWS):
        sl = slice(g * POOL_GROUP, (g + 1) * POOL_GROUP)
        acc = u[:, sl]
        for j in range(1, w):
            acc = acc + st_ref[POOL_BUF - j][:, sl]
        cnt = float(min(PAST_LEN + 1, w))
        ds.append(acc / cnt - u[:, sl])
    o_ref[...] = _pool_project(ds, wp_ref, ps_ref)


def _pool_sample(state_t, u, w_pool, pool_scale):
    nb, bd, c = state_t.shape
    full = lambda a: pl.BlockSpec(a.shape, lambda: (0,) * a.ndim)
    ps = pool_scale.reshape(1, c)
    return pl.pallas_call(
        _pool_sample_kernel,
        in_specs=[full(state_t), full(u), full(w_pool), full(ps)],
        out_specs=[pl.BlockSpec((bd, c), lambda: (0, 0)), full(state_t)],
        out_shape=[jax.ShapeDtypeStruct((bd, c), BF16), jax.ShapeDtypeStruct(state_t.shape, F32)],
        compiler_params=pltpu.CompilerParams(vmem_limit_bytes=VMEM_LIMIT),
        name="pool_sample",
    )(state_t, u, w_pool, ps)


def _route_kernel(*refs, own_steps):
    hs_ref = refs[-3]

    @pl.when(pl.program_id(0) < own_steps)
    def _():
        _route_tiles(*refs)

    @pl.when(pl.program_id(0) >= own_steps)
    def _():
        hs_ref[...] = jnp.zeros(hs_ref.shape, jnp.int32)


def _route_tiles(x_ref, attn_ref, pool_ref, gm_ref, sh_ref, sc_ref, wo_ref, nf_ref, wr_ref, br_ref,
                 tri_t_ref, tri_e_ref,
                 x2_ref, hs_ref, slot_ref, cnt_ref):
    sub = slot_ref.shape[0]
    n = x_ref.shape[0]
    tile = n // sub
    mixed = jnp.concatenate([attn_ref[...], pool_ref[...]], axis=1)
    mix = jnp.dot(mixed, wo_ref[...], preferred_element_type=F32)
    x2 = x_ref[...] + gm_ref[...] * mix
    x2_ref[...] = x2
    h = _rms(x2, nf_ref[...] * (1.0 + sc_ref[...])) + sh_ref[...]

    hh, hl = _split(h)
    wh, wl = _split(wr_ref[...])
    nt = functools.partial(lax.dot_general, dimension_numbers=(((1,), (1,)), ((), ())),
                           preferred_element_type=F32)
    both = nt(jnp.concatenate([wh, wl], axis=0), hh)
    logits = both[:N_EXPERTS] + both[N_EXPERTS:] + nt(wh, hl) + br_ref[...]

    eidx = lax.broadcasted_iota(jnp.int32, (N_EXPERTS, n), 0).astype(F32)
    work = logits
    tops, picks = [], []
    for _ in range(TOP_K):
        m = jnp.max(work, axis=0, keepdims=True)
        pick = jnp.min(jnp.where(work == m, eidx, float(N_EXPERTS)), axis=0, keepdims=True)
        work = jnp.where(eidx == pick, -jnp.inf, work)
        tops.append(m)
        picks.append(pick)
    ex = [jnp.exp(v - tops[0]) for v in tops]
    den = ex[0] + ex[1] + ex[2] + ex[3]
    gates = [e / den for e in ex]

    sel = jnp.zeros((N_EXPERTS, n), F32)
    for pick in picks:
        sel = sel + (eidx == pick).astype(F32)
    selb = sel.astype(BF16)
    rank = jnp.concatenate([jnp.dot(selb[:, t * tile:(t + 1) * tile], tri_t_ref[...], preferred_element_type=F32)
                            for t in range(sub)], axis=1)
    cnts = [jnp.sum(sel[:, t * tile:(t + 1) * tile], axis=1, keepdims=True) for t in range(sub)]
    padded = jnp.concatenate(
        [jnp.broadcast_to(jnp.ceil(c * (1.0 / CHUNK)) * CHUNK, (N_EXPERTS, LANES)) for c in cnts], axis=1)
    seg = jnp.dot(tri_e_ref[...], padded.astype(BF16), preferred_element_type=F32)
    dest = jnp.concatenate([seg[:, t * LANES:t * LANES + 1] + rank[:, t * tile:(t + 1) * tile]
                            for t in range(sub)], axis=1)
    slots = [jnp.sum(jnp.where(eidx == pick, dest, 0.0), axis=0, keepdims=True) for pick in picks]

    ridx = lax.broadcasted_iota(jnp.int32, (TILE_ROWS, tile), 0).astype(jnp.int16)
    slots16 = [v.astype(jnp.int32).astype(jnp.int16) for v in slots]
    for t in range(sub):
        cols = slice(t * tile, (t + 1) * tile)
        cnt_ref[t] = jnp.broadcast_to(cnts[t], (N_EXPERTS, LANES))
        slot_ref[t] = jnp.concatenate([v[:, cols] for v in slots + gates], axis=0)
        hit = ridx == slots16[0][:, cols]
        for s in slots16[1:]:
            hit = jnp.logical_or(hit, ridx == s[:, cols])
        perm = jnp.where(hit, jnp.ones((), BF16), jnp.zeros((), BF16))
        hs_ref[pl.ds(t * TILE_ROWS, TILE_ROWS), :] = _pack_rows(
            jnp.dot(perm, hh[t * tile:(t + 1) * tile, :], preferred_element_type=F32))


def _route(x2d, attn, pool, mod, w_out, norm_ffn, w_router_t, b_router, hs_prev, *, tile, sub,
           rows_per_mod, tile0, extra_tiles):
    n = x2d.shape[0]
    nt = n // tile
    assert nt % sub == 0 and extra_tiles % sub == 0
    own_steps = nt // sub
    steps = own_steps + extra_tiles // sub
    mrows = mod.shape[2]
    assert mrows == 1 or (sub == 1 and mrows == tile)
    last = lambda i: jnp.minimum(i, own_steps - 1)
    group = lambda i: (last(i) * sub * tile) // rows_per_mod
    const = lambda shape: pl.BlockSpec(shape, lambda i: (0,) * len(shape))
    row = lambda w: pl.BlockSpec((sub * tile, w), lambda i: (last(i), 0))
    tri_t = (jnp.arange(tile)[:, None] < jnp.arange(tile)[None, :]).astype(BF16)
    tri_e = (jnp.arange(N_EXPERTS)[None, :] < jnp.arange(N_EXPERTS)[:, None]).astype(BF16)
    in_specs = [row(D_MODEL), row(ATTN_WIDTH), row(POOL_WIDTH),
                _mod_spec(mod, 2, group), _mod_spec(mod, 3, group), _mod_spec(mod, 4, group),
                const((D_MODEL, D_MODEL)), const((1, D_MODEL)), const((N_EXPERTS, D_MODEL)),
                const((N_EXPERTS, 1)), const((tile, tile)), const((N_EXPERTS, N_EXPERTS))]
    args = [x2d, attn, pool, mod, mod, mod, w_out, norm_ffn.reshape(1, D_MODEL), w_router_t,
            b_router.reshape(N_EXPERTS, 1), tri_t, tri_e]
    n_in = len(args)
    kern = functools.partial(_route_kernel, own_steps=own_steps)
    aliases = {}
    hs_rows = (tile0 + steps * sub) * TILE_ROWS
    assert tile0 % sub == 0
    if hs_prev is not None:
        in_specs.append(pl.BlockSpec(memory_space=pl.ANY))
        args.append(hs_prev)
        aliases = {n_in: 1}
        kern = lambda *refs: _route_kernel(*refs[:n_in], *refs[n_in + 1:], own_steps=own_steps)
        hs_rows = hs_prev.shape[0]
    return pl.pallas_call(
        kern,
        grid=(steps,),
        in_specs=in_specs,
        out_specs=[row(D_MODEL),
                   pl.BlockSpec((sub * TILE_ROWS, ROW_W), lambda i: (i + tile0 // sub, 0)),
                   pl.BlockSpec((sub, 2 * TOP_K, tile), lambda i: (last(i), 0, 0)),
                   pl.BlockSpec((sub, N_EXPERTS, LANES), lambda i: (last(i), 0, 0))],
        out_shape=[jax.ShapeDtypeStruct((n, D_MODEL), F32),
                   jax.ShapeDtypeStruct((hs_rows, ROW_W), jnp.int32),
                   jax.ShapeDtypeStruct((nt, 2 * TOP_K, tile), F32),
                   jax.ShapeDtypeStruct((nt, N_EXPERTS, LANES), F32)],
        input_output_aliases=aliases,
        compiler_params=_cparams(("arbitrary",)),
        name="route",
    )(*args)


def _moe_kernel(src_ref, be_ref, nxt_ref, nb_ref, hs_hbm, w1_hbm, b1_ref, w2_hbm, b2_ref, out_hbm,
                lhs, obuf, w1s, w2s, w1c0, w2c0, w1c1, w2c1, sem_in, sem_out, sem_w, *, scratch_chunk):
    nb = nb_ref[0]

    def weight_copies(e):
        return (pltpu.make_async_copy(w1_hbm.at[e], w1s, sem_w.at[0]),
                pltpu.make_async_copy(w2_hbm.at[e], w2s, sem_w.at[1]))

    def chunk_rows(c):
        return pl.ds(pl.multiple_of(c * CHUNK, CHUNK), CHUNK)

    def start_in(blk, s):
        for j in range(BLOCK_CHUNKS):
            c = src_ref[blk * BLOCK_CHUNKS + j]
            c = jnp.where(c < 0, ZERO_CHUNK, c)
            pltpu.make_async_copy(hs_hbm.at[chunk_rows(c)], lhs.at[s, pl.ds(j * CHUNK, CHUNK)],
                                  sem_in.at[s]).start()

    def wait_in(s):
        pltpu.make_async_copy(hs_hbm.at[pl.ds(0, MOE_BLOCK)], lhs.at[s], sem_in.at[s]).wait()

    def start_out(blk, s):
        for j in range(BLOCK_CHUNKS):
            c = src_ref[blk * BLOCK_CHUNKS + j]
            c = jnp.where(c < 0, scratch_chunk + s * BLOCK_CHUNKS + j, c)
            pltpu.make_async_copy(obuf.at[s, pl.ds(j * CHUNK, CHUNK)],
                                  out_hbm.at[chunk_rows(c), pl.ds(0, PACKED_W)], sem_out.at[s]).start()

    def wait_out(s):
        pltpu.make_async_copy(obuf.at[s], out_hbm.at[pl.ds(0, MOE_BLOCK), pl.ds(0, PACKED_W)],
                              sem_out.at[s]).wait()

    @pl.when(nb > 0)
    def _():
        start_in(0, 0)
        for cp in weight_copies(be_ref[0]):
            cp.start(priority=WEIGHT_DMA_PRIORITY)

    copies = ((w1c0, w2c0), (w1c1, w2c1))

    def cast_slice(j, q):
        w1c, w2c = copies[q]
        r = pl.ds(pl.multiple_of(j * CAST_ROWS, CAST_ROWS), CAST_ROWS)
        w1c[r, :] = w1s[r, :].astype(BF16)
        w2c[r, :] = w2s[r, :].astype(BF16)

    def block(b, carry):
        k, p = carry
        slot = b % 2
        e = be_ref[b]
        nxt = nxt_ref[b]
        change = jnp.logical_or(b == 0, e != be_ref[jnp.maximum(b - 1, 0)])
        q = jnp.where(b == 0, 0, 1 - p)

        @pl.when(change)
        def _():
            @pl.when(jnp.logical_or(b == 0, k <= CAST_FIRST))
            def _():
                for cp in weight_copies(e):
                    cp.wait()
            done = jnp.where(b == 0, 0, jnp.clip(k - CAST_FIRST, 0, CAST_SLICES))

            for parity in range(2):
                def rest(j, c, parity=parity):
                    cast_slice(j, parity)
                    return c

                @pl.when(q == parity)
                def _():
                    lax.fori_loop(done, CAST_SLICES, rest, 0)

            @pl.when(nxt >= 0)
            def _():
                for cp in weight_copies(nxt):
                    cp.start(priority=WEIGHT_DMA_PRIORITY)

        p = jnp.where(change, q, p)
        k = jnp.where(change, 0, k)

        @pl.when(jnp.logical_and(k == CAST_FIRST, nxt >= 0))
        def _():
            for cp in weight_copies(nxt):
                cp.wait()
        cast_now = jnp.logical_and(jnp.logical_and(k >= CAST_FIRST, k < CAST_FIRST + CAST_SLICES), nxt >= 0)

        wait_in(slot)

        @pl.when(b >= 2)
        def _():
            wait_out(slot)

        start_in(b + 1, 1 - slot)

        def ffn(nrows, parity):
            w1c, w2c = copies[parity]
            xh, xl = _unpack_rows(lhs[slot, 0:nrows, :])
            x = jnp.concatenate([xh, xl], axis=1)
            gu = jnp.dot(x, w1c[...], preferred_element_type=F32) + b1_ref[e]
            gate = jnp.minimum(gu[:, :D_FF], SWIGLU_LIMIT)
            up = jnp.clip(gu[:, D_FF:], -SWIGLU_LIMIT, SWIGLU_LIMIT)
            act = (up + 1.0) * (gate * jax.nn.sigmoid(SWIGLU_ALPHA * gate))
            y = jnp.dot(act.astype(BF16), w2c[...], preferred_element_type=F32) + b2_ref[e]
            obuf[slot, 0:nrows, :] = _pack_rows(y.astype(BF16).astype(F32))

        real = nb_ref[1 + b]
        whole = real > BLOCK_CHUNKS // 2

        for parity in range(2):
            mine = p == parity

            @pl.when(jnp.logical_and(mine, jnp.logical_and(whole, cast_now)))
            def _():
                cast_slice(k - CAST_FIRST, 1 - parity)
                ffn(MOE_BLOCK, parity)

            @pl.when(jnp.logical_and(mine, jnp.logical_and(whole, jnp.logical_not(cast_now))))
            def _():
                ffn(MOE_BLOCK, parity)

            @pl.when(jnp.logical_and(mine, jnp.logical_not(whole)))
            def _():
                ffn(MOE_BLOCK // 2, parity)

                @pl.when(cast_now)
                def _():
                    cast_slice(k - CAST_FIRST, 1 - parity)

        start_out(b, slot)
        return k + 1, p

    obuf[...] = jnp.zeros(obuf.shape, jnp.int32)
    lax.fori_loop(0, nb, block, (jnp.int32(0), jnp.int32(0)))

    @pl.when(nb > 0)
    def _():
        last_slot = (nb - 1) % 2
        wait_in(1 - last_slot)

        @pl.when(nb >= 2)
        def _():
            wait_out(1 - last_slot)
        wait_out(last_slot)


def _moe(hs, src, block_e, next_e, nblocks, w1, b1, w2, b2, scratch_chunk):
    full = lambda shape: pl.BlockSpec(shape, lambda i, *_: (0,) * len(shape))
    hbm = pl.BlockSpec(memory_space=pl.ANY)
    return pl.pallas_call(
        functools.partial(_moe_kernel, scratch_chunk=scratch_chunk),
        grid_spec=pltpu.PrefetchScalarGridSpec(
            num_scalar_prefetch=4,
            grid=(1,),
            in_specs=[hbm, hbm, full((N_EXPERTS, 1, 2 * D_FF)), hbm, full((N_EXPERTS, 1, D_MODEL))],
            out_specs=hbm,
            scratch_shapes=[pltpu.VMEM((2, MOE_BLOCK, ROW_W), jnp.int32),
                            pltpu.VMEM((2, MOE_BLOCK, PACKED_W), jnp.int32),
                            pltpu.VMEM((D_MODEL, 2 * D_FF), F32),
                            pltpu.VMEM((D_FF, D_MODEL), F32),
                            pltpu.VMEM((D_MODEL, 2 * D_FF), BF16),
                            pltpu.VMEM((D_FF, D_MODEL), BF16),
                            pltpu.VMEM((D_MODEL, 2 * D_FF), BF16),
                            pltpu.VMEM((D_FF, D_MODEL), BF16),
                            pltpu.SemaphoreType.DMA((2,)),
                            pltpu.SemaphoreType.DMA((2,)),
                            pltpu.SemaphoreType.DMA((2,))]),
        out_shape=jax.ShapeDtypeStruct(hs.shape, jnp.int32),
        input_output_aliases={4: 0},
        compiler_params=_cparams(("arbitrary",)),
        name="moe_experts",
    )(src, block_e, next_e, nblocks, hs, w1, b1.reshape(N_EXPERTS, 1, 2 * D_FF), w2,
      b2.reshape(N_EXPERTS, 1, D_MODEL))


def _plan(cnt, n_blocks_max):
    nt = cnt.shape[0]
    nch = (cnt + (CHUNK - 1)) // CHUNK
    lstart = jnp.cumsum(nch, axis=1) - nch
    ne = jnp.sum(nch, axis=0)
    nbe = (ne + (BLOCK_CHUNKS - 1)) // BLOCK_CHUNKS
    bend = jnp.cumsum(nbe)
    nblocks = bend[-1]
    gstart = (bend - nbe)[None, :] * BLOCK_CHUNKS + (jnp.cumsum(nch, axis=0) - nch)
    s0 = jnp.arange(nt, dtype=jnp.int32)[:, None] * TILE_CHUNKS + lstart
    blk = jnp.arange(n_blocks_max + 1, dtype=jnp.int32)
    be = jnp.sum((blk[:, None] >= bend[None, :]).astype(jnp.int32), axis=1)
    be_last = jnp.sum((nblocks - 1 >= bend).astype(jnp.int32))
    be = jnp.minimum(be, be_last).astype(jnp.int32)
    eid = jnp.arange(N_EXPERTS, dtype=jnp.int32)
    strips = jnp.stack([gstart.T, nch.T, s0.T])
    mine = (be[:, None] == eid[None, :])[None, :, :, None]
    gs_b, nc_b, s0_b = jnp.sum(jnp.where(mine, strips[:, None], 0), axis=2)
    c = (blk[:, None] * BLOCK_CHUNKS + jnp.arange(BLOCK_CHUNKS, dtype=jnp.int32)[None, :])[:, :, None]
    inside = jnp.logical_and(c >= gs_b[:, None, :], c < (gs_b + nc_b)[:, None, :])
    src = (jnp.sum(jnp.where(inside, (s0_b - gs_b)[:, None, :] + c + 1, 0), axis=2) - 1).reshape(-1)
    be = be[:n_blocks_max]
    later = jnp.logical_and(eid[None, :] > be[:, None], (nbe > 0)[None, :])
    nxt = jnp.min(jnp.where(later, eid[None, :], N_EXPERTS), axis=1)
    nxt = jnp.where(nxt == N_EXPERTS, -1, nxt).astype(jnp.int32)
    real = jnp.sum((src.reshape(-1, BLOCK_CHUNKS) >= 0).astype(jnp.int32), axis=1)
    return src, be, nxt, jnp.concatenate([nblocks.reshape(1), real]).astype(jnp.int32)


def _combine_kernel(o_ref, slot_ref, x2_ref, gf_ref, y_ref):
    sub = slot_ref.shape[0]
    tile = x2_ref.shape[0] // sub
    ridx = lax.broadcasted_iota(jnp.int32, (TILE_ROWS, tile), 0).astype(F32)
    tn = functools.partial(lax.dot_general, dimension_numbers=(((0,), (0,)), ((), ())),
                           preferred_element_type=F32)
    for t in range(sub):
        gmat = jnp.zeros((TILE_ROWS, tile), F32)
        for k in range(TOP_K):
            gmat = jnp.where(ridx == slot_ref[t, k:k + 1, :], slot_ref[t, TOP_K + k:TOP_K + k + 1, :], gmat)
        gb = gmat.astype(BF16)
        oh, ol = _unpack_rows(o_ref[pl.ds(t * TILE_ROWS, TILE_ROWS), :])
        y = jnp.concatenate([tn(gb, oh), tn(gb, ol)], axis=1)
        rows = pl.ds(t * tile, tile)
        gf = gf_ref[...] if gf_ref.shape[0] == 1 else gf_ref[rows, :]
        y_ref[rows, :] = x2_ref[rows, :] + gf * y


def _combine(outs, slots, x2, mod, *, tile, sub, rows_per_mod, tile0):
    n = x2.shape[0]
    mrows = mod.shape[2]
    assert n % (sub * tile) == 0 and tile0 % sub == 0 and (mrows == 1 or sub == 1)
    return pl.pallas_call(
        _combine_kernel,
        grid=(n // (sub * tile),),
        in_specs=[pl.BlockSpec((sub * TILE_ROWS, PACKED_W), lambda i: (i + tile0 // sub, 0)),
                  pl.BlockSpec((sub, 2 * TOP_K, tile), lambda i: (i, 0, 0)),
                  pl.BlockSpec((sub * tile, D_MODEL), lambda i: (i, 0)),
                  _mod_spec(mod, 5, lambda i: (i * sub * tile) // rows_per_mod)],
        out_specs=pl.BlockSpec((sub * tile, D_MODEL), lambda i: (i, 0)),
        out_shape=jax.ShapeDtypeStruct((n, D_MODEL), F32),
        compiler_params=_cparams(("parallel",)),
        name="combine",
    )(outs, slots, x2, mod)


def kernel(x_prompt, x_sample, cache_k, cache_v, state_pool, c_prompt, c_sample, rel_bias, norm_mix, w_ada,
           b_ada, w_in, q_norm, k_norm, sinks, w_pool, pool_scale, w_out, norm_ffn, w_router, b_router,
           w1, b1, w2, b2):
    depth = w_in.shape[0]
    assert depth == 1
    l = 0
    bp, sp, _ = x_prompt.shape
    bs = x_sample.shape[0]
    assert x_sample.shape[1] == 1 and sp % SORT_TILE == 0 and bs <= SORT_TILE
    n_p = bp * sp
    tiles_p = n_p // SORT_TILE
    max_chunks = tiles_p * (SORT_TILE * TOP_K // CHUNK + N_EXPERTS) + (bs * TOP_K // CHUNK + N_EXPERTS)
    n_blocks_max = -(-max_chunks // BLOCK_CHUNKS) + N_EXPERTS

    pr = -(-bp // SUBLANES) * SUBLANES
    c_all = jnp.concatenate([c_prompt, jnp.zeros((pr - bp, D_MODEL), F32), c_sample], axis=0)
    mod = _modulation(c_all, w_ada[l], b_ada[l])
    mod_p = mod[:, :bp].reshape(6, bp, 1, D_MODEL)
    mod_s = mod[:, pr:].reshape(6, 1, bs, D_MODEL)

    head = jnp.arange(ATTN_WIDTH) // HEAD_DIM
    bd = (head[:, None] == head[None, :]).astype(BF16)
    w_in_b = w_in[l].astype(BF16)
    w_out_b = w_out[l].astype(BF16)
    w_pool_b = w_pool[l].astype(BF16)
    w_router_t = w_router[l].T

    xp = x_prompt.reshape(n_p, D_MODEL)
    q, k, v, pool, u_tail = _mixer_inputs(xp, mod_p, norm_mix[l], w_in_b, bd, q_norm[l], k_norm[l],
                                          tile=1024, rows_per_mod=sp, precise=False,
                                          pool=(w_pool_b, pool_scale[l]))
    k3 = k.reshape(bp, sp, KV_WIDTH)
    v3 = v.reshape(bp, sp, KV_WIDTH)
    attn = _attn_prompt(q.reshape(bp, sp, ATTN_WIDTH), k3, v3, sinks[l], rel_bias)
    keep = min(WINDOW, sp)
    nkp = k3[:, -keep:].reshape(bp, keep, N_KV_HEADS, HEAD_DIM)
    nvp = v3[:, -keep:].reshape(bp, keep, N_KV_HEADS, HEAD_DIM)
    npp = u_tail[:, POOL_HALO - POOL_BUF:]
    x2_p, hs, slots_p, cnt_p = _route(
        xp, attn.reshape(n_p, ATTN_WIDTH), pool.reshape(n_p, POOL_WIDTH), mod_p,
        w_out_b, norm_ffn[l], w_router_t, b_router[l], None,
        tile=SORT_TILE, sub=ROUTE_SUB, rows_per_mod=sp, tile0=0, extra_tiles=max(2, ROUTE_SUB))

    xs = x_sample.reshape(bs, D_MODEL)
    qs, ks, vs, us = _mixer_inputs(xs, mod_s, norm_mix[l], w_in[l], bd, q_norm[l], k_norm[l],
                                   tile=bs, rows_per_mod=bs, precise=True)
    wbuf = cache_k.shape[2]
    attn_s, nks, nvs = _attn_sample(qs, ks, vs, cache_k[l].reshape(bs, wbuf, KV_WIDTH),
                                    cache_v[l].reshape(bs, wbuf, KV_WIDTH), sinks[l], rel_bias)
    pool_s, nps_t = _pool_sample(jnp.swapaxes(state_pool[l], 0, 1), us, w_pool_b, pool_scale[l])
    x2_s, hs, slots_s, cnt_s = _route(
        xs, attn_s.astype(BF16), pool_s, mod_s, w_out_b, norm_ffn[l],
        w_router_t, b_router[l], hs, tile=bs, sub=1, rows_per_mod=bs, tile0=tiles_p, extra_tiles=0)

    cnt = jnp.concatenate([cnt_p[:, :, 0], cnt_s[:, :, 0]], axis=0).astype(jnp.int32)
    src, block_e, next_e, nblocks = _plan(cnt, n_blocks_max)
    outs = _moe(hs, src, block_e, next_e, nblocks, w1[l], b1[l], w2[l], b2[l],
                scratch_chunk=(tiles_p + 1) * TILE_CHUNKS)

    y_p = _combine(outs, slots_p, x2_p, mod_p, tile=SORT_TILE, sub=COMBINE_SUB, rows_per_mod=sp, tile0=0)
    y_s = _combine(outs, slots_s, x2_s, mod_s, tile=bs, sub=1, rows_per_mod=bs, tile0=tiles_p)

    return (y_p.reshape(bp, sp, D_MODEL), y_s.reshape(bs, 1, D_MODEL),
            nkp[None], nvp[None], npp[None],
            nks.reshape(1, bs, wbuf, N_KV_HEADS, HEAD_DIM), nvs.reshape(1, bs, wbuf, N_KV_HEADS, HEAD_DIM),
            jnp.swapaxes(nps_t, 0, 1)[None])
```
